```python
import math
import jax, jax.numpy as jnp
from jax import lax
import numpy as np

D_MODEL = 1024
BATCH = 8
SEQ = 2048
DEPTH = 2
DEC_BATCH = 32
DEC_SEQ = 4
PAST_LEN = 8192
PAGE_SIZE = 128

A_HEADS = 8
A_KV = 2
A_REP = A_HEADS // A_KV
A_HD = 64
CMP_BLOCK = 32
CMP_STRIDE = 16
SEL_BLOCK = 64
TOP_N = 16
WINDOW = 512
Q_BLOCK = 128
N_BUCKETS = 32
MAX_DIST = 2048
M_HEADS = 4
M_HD = 128
M_CHUNK = 64
G_HEADS = 4
G_HD = 128
G_CHUNK = 64
CONV_W = 4

W_A = A_HEADS * A_HD
W_M = M_HEADS * M_HD
W_G = G_HEADS * G_HD
N_BRANCH = 3
EPS = 1e-6
NEG = -1e30

IN_SPLITS = (
    ('a_q', W_A),
    ('a_kv', 3 * 2 * A_KV * A_HD),
    ('a_gate', A_HEADS * 3),
    ('a_z', W_A),
    ('m_qkv', 3 * W_M),
    ('m_if', 2 * M_HEADS),
    ('m_o', W_M),
    ('m_z', W_M),
    ('g_qkv', 3 * W_G),
    ('g_ab', 2 * G_HEADS),
    ('g_z', W_G),
    ('merge', N_BRANCH * D_MODEL),
)
N_IN = sum(w for _, w in IN_SPLITS)

kernel_name = 'hybrid_nsa_mlstm_gdn_step'


def rmsnorm(x, g):
    xf = x.astype(jnp.float32)
    y = xf * lax.rsqrt(jnp.mean(xf * xf, axis=-1, keepdims=True) + EPS)
    return (y * g.astype(jnp.float32)).astype(x.dtype)


def l2norm(x):
    return x * lax.rsqrt(jnp.sum(x * x, axis=-1, keepdims=True) + EPS)


def split_cols(u):
    parts, off = {}, 0
    for name, width in IN_SPLITS:
        parts[name] = u[..., off:off + width]
        off += width
    return parts


def rel_bucket(dist):
    n = jnp.maximum(dist, 0)
    exact = N_BUCKETS // 2
    nf = jnp.maximum(n, exact).astype(jnp.float32)
    large = exact + (jnp.log(nf / exact) / math.log(MAX_DIST / exact) * (N_BUCKETS - exact)).astype(jnp.int32)
    return jnp.where(n < exact, n, jnp.minimum(large, N_BUCKETS - 1))


def head_bias(rel_bias, dist):
    Q, K = dist.shape
    b = rel_bias[rel_bucket(dist)].astype(jnp.float32)
    return b.reshape(Q, K, A_KV, A_REP).transpose(2, 0, 3, 1)


def masked_softmax(s, valid):
    return jax.nn.softmax(jnp.where(valid, s, NEG), axis=-1) * valid


def gather_pages(pool, page_table):
    g = pool[page_table]
    return g.reshape((page_table.shape[0], -1) + pool.shape[2:])


def to_chunks(a, L):
    B, T = a.shape[:2]
    a = a.reshape((B, T // L, L) + a.shape[2:])
    return jnp.moveaxis(a, (1, 3), (0, 2))


def nsa_compress(x, w):
    B, Tk = x.shape[:2]
    ratio = CMP_BLOCK // CMP_STRIDE
    nch = -(-Tk // CMP_STRIDE)
    xc = jnp.pad(x, ((0, 0), (0, nch * CMP_STRIDE - Tk), (0, 0), (0, 0))).reshape(B, nch, CMP_STRIDE, A_KV, A_HD)
    wr = w.reshape(A_KV, ratio, CMP_STRIDE, A_HD, A_HD)
    nb = nch - ratio + 1
    out = 0
    for m in range(ratio):
        out = out + jnp.einsum('bnsgd,gsde->bnge', xc[:, m:m + nb], wr[:, m])
    return out


def nsa_attention(q, full_cmp, full_sel, win, w_off, gates, cmp_wk, cmp_wv, rel_bias, q_off):
    B, Tq = q.shape[:2]
    Tk = full_cmp.shape[1]
    f32 = jnp.float32
    kcmp = nsa_compress(full_cmp[:, :, 0], cmp_wk)
    vcmp = nsa_compress(full_cmp[:, :, 1], cmp_wv)
    nb = kcmp.shape[1]
    cmp_start = jnp.arange(nb) * CMP_STRIDE
    cmp_end = cmp_start + CMP_BLOCK - 1
    ns = -(-Tk // SEL_BLOCK)
    sel = jnp.pad(full_sel, ((0, 0), (0, ns * SEL_BLOCK - Tk), (0, 0), (0, 0), (0, 0)))
    sel = sel.reshape(B, ns, SEL_BLOCK, 2, A_KV, A_HD).transpose(0, 4, 1, 2, 3, 5)
    sel_start = jnp.arange(ns) * SEL_BLOCK
    cover = ((cmp_start[:, None] < sel_start[None, :] + SEL_BLOCK) & (sel_start[None, :] <= cmp_end[:, None])).astype(f32)
    n_top = min(TOP_N, ns)
    winp = jnp.pad(win, ((0, 0), (WINDOW, 0), (0, 0), (0, 0), (0, 0)))
    QB = math.gcd(Tq, Q_BLOCK)
    nqb = Tq // QB
    n_wk = WINDOW + QB
    qblk = q.reshape(B, nqb, QB, A_KV, A_REP, A_HD).transpose(1, 0, 3, 2, 4, 5)
    gblk = gates.reshape(B, nqb, QB, A_KV, A_REP, 3).transpose(1, 0, 3, 2, 4, 5)
    b_idx = jnp.arange(B)[:, None, None, None]
    g_idx = jnp.arange(A_KV)[None, :, None, None]
    table_g = rel_bias.reshape(N_BUCKETS, A_KV, A_REP).transpose(1, 0, 2)
    j_sel = jnp.arange(ns)

    def block(args):
        qb, gb, bi = args
        t0 = q_off + bi * QB
        t = t0 + jnp.arange(QB)
        dist = t[:, None] - cmp_end[None, :]
        s = jnp.einsum('bgqrd,bngd->bgqrn', qb, kcmp).astype(f32) + head_bias(rel_bias, dist)
        p_c = masked_softmax(s, (dist >= 0)[:, None, :])
        o_c = jnp.einsum('bgqrn,bngd->bgqrd', p_c, vcmp)
        imp = jnp.einsum('bgqrn,ns->bgqs', p_c, cover)
        cur = t // SEL_BLOCK
        forced = (j_sel[None] == 0) | (j_sel[None] == cur[:, None]) | (j_sel[None] == cur[:, None] - 1)
        future = sel_start[None, :] > t[:, None]
        score = jnp.where(future, NEG, jnp.where(forced, -NEG, imp))
        _, idx = lax.top_k(score, n_top)
        kv_sel = sel[b_idx, g_idx, idx]
        kpos = idx[..., None] * SEL_BLOCK + jnp.arange(SEL_BLOCK)
        dist = t[:, None, None] - kpos
        s = jnp.einsum('bgqrd,bgqnjd->bgqrnj', qb, kv_sel[..., 0, :]).astype(f32)
        bias = table_g[g_idx[..., None], rel_bucket(dist)]
        s = (s + jnp.moveaxis(bias, -1, 3).astype(f32)).reshape(B, A_KV, QB, A_REP, n_top * SEL_BLOCK)
        valid = (dist >= 0).reshape(B, A_KV, QB, 1, n_top * SEL_BLOCK)
        p_s = masked_softmax(s, valid)
        o_s = jnp.einsum('bgqrk,bgqkd->bgqrd', p_s, kv_sel[..., 1, :].reshape(B, A_KV, QB, n_top * SEL_BLOCK, A_HD))
        wkv = lax.dynamic_slice_in_dim(winp, t0 - w_off, n_wk, axis=1)
        wpos = t0 - WINDOW + jnp.arange(n_wk)
        dist = t[:, None] - wpos[None, :]
        valid = (dist >= 0) & (dist < WINDOW) & (wpos[None, :] >= w_off)
        s = jnp.einsum('bgqrd,bkgd->bgqrk', qb, wkv[:, :, 0]).astype(f32) + head_bias(rel_bias, dist)
        p_w = masked_softmax(s, valid[:, None, :])
        o_w = jnp.einsum('bgqrk,bkgd->bgqrd', p_w, wkv[:, :, 1])
        return gb[..., 0:1] * o_c + gb[..., 1:2] * o_s + gb[..., 2:3] * o_w

    out = lax.map(block, (qblk, gblk, jnp.arange(nqb)))
    return out.transpose(1, 0, 3, 2, 4, 5).reshape(B, Tq, W_A)


def mlstm_chunked(q, k, v, li, lf, C, n, m):
    B, T = q.shape[:2]
    L = math.gcd(T, M_CHUNK)
    causal = jnp.tril(jnp.ones((L, L), bool))

    def step(carry, xs):
        C, n, m = carry
        qc, kc, vc, lic, lfc = xs
        b = jnp.cumsum(lfc, axis=-1)
        Dm = jnp.where(causal, b[..., :, None] - b[..., None, :] + lic[..., None, :], -jnp.inf)
        a = b + m[..., None]
        mt = jnp.maximum(a, Dm.max(-1))
        S = jnp.einsum('bhtd,bhsd->bhts', qc, kc) * jnp.exp(Dm - mt[..., None])
        inter = jnp.exp(a - mt)
        num = inter[..., None] * jnp.einsum('bhtd,bhde->bhte', qc, C) + jnp.einsum('bhts,bhse->bhte', S, vc)
        den = inter * jnp.einsum('bhtd,bhd->bht', qc, n) + S.sum(-1)
        h = num / jnp.maximum(jnp.abs(den), jnp.exp(-mt))[..., None]
        bL = b[..., -1]
        wlog = bL[..., None] - b + lic
        m_new = jnp.maximum(bL + m, wlog.max(-1))
        ws = jnp.exp(wlog - m_new[..., None])
        dec = jnp.exp(bL + m - m_new)
        C = dec[..., None, None] * C + jnp.einsum('bhs,bhsd,bhse->bhde', ws, kc, vc)
        n = dec[..., None] * n + jnp.einsum('bhs,bhsd->bhd', ws, kc)
        return (C, n, m_new), h

    xs = tuple(to_chunks(a, L) for a in (q, k, v, li, lf))
    (C, n, m), hs = lax.scan(step, (C, n, m), xs)
    h = hs.transpose(1, 0, 3, 2, 4).reshape(B, T, q.shape[2], v.shape[-1])
    return h, C, n, m


def gdn_chunked(q, k, v, beta, g, S):
    B, T = q.shape[:2]
    L = math.gcd(T, G_CHUNK)
    dv = v.shape[-1]
    lower = jnp.tril(jnp.ones((L, L), bool))
    strict = jnp.tril(jnp.ones((L, L), jnp.float32), -1)
    eye = jnp.eye(L, dtype=jnp.float32)

    def step(S, xs):
        qc, kc, vc, bc, gc = xs
        G = jnp.cumsum(gc, axis=-1)
        dmask = jnp.exp(jnp.where(lower, G[..., :, None] - G[..., None, :], -jnp.inf))
        kb = kc * bc[..., None]
        A = jnp.einsum('bhid,bhjd->bhij', kb, kc) * dmask * strict
        rhs = jnp.concatenate([vc * bc[..., None], kb * jnp.exp(G)[..., None]], axis=-1)
        sol = lax.linalg.triangular_solve(eye + A, rhs, left_side=True, lower=True, unit_diagonal=True)
        u, w = sol[..., :dv], sol[..., dv:]
        v_new = u - jnp.einsum('bhid,bhde->bhie', w, S)
        attn = jnp.einsum('bhid,bhjd->bhij', qc, kc) * dmask
        o = jnp.einsum('bhid,bhde->bhie', qc * jnp.exp(G)[..., None], S) + jnp.einsum('bhij,bhje->bhie', attn, v_new)
        GL = G[..., -1]
        S = jnp.exp(GL)[..., None, None] * S + jnp.einsum('bhjd,bhje->bhde', kc * jnp.exp(GL[..., None] - G)[..., None], v_new)
        return S, o

    xs = tuple(to_chunks(a, L) for a in (q, k, v, beta, g))
    S, os_ = lax.scan(step, S, xs)
    o = os_.transpose(1, 0, 3, 2, 4).reshape(B, T, q.shape[2], dv)
    return o, S


def mixer_layer(x, lp, rel_bias, past, q_off):
    B, T, _ = x.shape
    f32 = jnp.float32
    dt = x.dtype
    parts = split_cols(rmsnorm(x, lp['norm_g']) @ lp['w_in'])

    q = rmsnorm(parts['a_q'].reshape(B, T, A_HEADS, A_HD), lp['a_qn']) * (A_HD ** -0.5)
    kv = parts['a_kv'].reshape(B, T, 3, 2, A_KV, A_HD)
    rows = jnp.stack([rmsnorm(kv[:, :, :, 0], lp['a_kn'][:, None, :]), kv[:, :, :, 1]], axis=3)
    new_cmp, new_sel, new_win = rows[:, :, 0], rows[:, :, 1], rows[:, :, 2]
    if past is None:
        full_cmp, full_sel, win, w_off = new_cmp, new_sel, new_win, 0
        C0 = jnp.zeros((B, M_HEADS, M_HD, M_HD), f32)
        n0 = jnp.zeros((B, M_HEADS, M_HD), f32)
        m0 = jnp.zeros((B, M_HEADS), f32)
        S0 = jnp.zeros((B, G_HEADS, G_HD, G_HD), f32)
        buf = jnp.zeros((B, CONV_W - 1, 3 * W_G), dt)
    else:
        full_cmp = jnp.concatenate([past['cmp'].astype(dt), new_cmp], axis=1)
        full_sel = jnp.concatenate([past['sel'].astype(dt), new_sel], axis=1)
        win = jnp.concatenate([past['win'].astype(dt), new_win], axis=1)
        w_off = q_off - past['win'].shape[1]
        C0 = past['mC'].astype(f32)
        n0 = past['mn'].astype(f32)
        m0 = past['mm'].astype(f32)
        S0 = past['gS'].astype(f32)
        buf = past['gconv'].astype(dt)
    gates = jax.nn.sigmoid(parts['a_gate'].reshape(B, T, A_HEADS, 3).astype(f32))
    o_a = nsa_attention(q, full_cmp, full_sel, win, w_off, gates, lp['a_cmp_wk'], lp['a_cmp_wv'], rel_bias, q_off)
    o_a = o_a * jax.nn.silu(parts['a_z'].astype(f32))

    mq = parts['m_qkv'].reshape(B, T, 3, M_HEADS, M_HD).astype(f32)
    mif = parts['m_if'].reshape(B, T, 2, M_HEADS).astype(f32)
    li = mif[:, :, 0] + lp['m_bi'].astype(f32)
    lf = jax.nn.log_sigmoid(mif[:, :, 1] + lp['m_bf'].astype(f32))
    hm, Cn, nn_, mn = mlstm_chunked(mq[:, :, 0], mq[:, :, 1] * (M_HD ** -0.5), mq[:, :, 2], li, lf, C0, n0, m0)
    o_gate = jax.nn.sigmoid(parts['m_o'].reshape(B, T, M_HEADS, M_HD).astype(f32))
    o_m = (rmsnorm(hm, lp['m_hn']) * o_gate).reshape(B, T, W_M) * jax.nn.silu(parts['m_z'].astype(f32))

    full = jnp.concatenate([buf, parts['g_qkv']], axis=1)
    conv = sum(full[:, j:j + T] * lp['g_conv'][j] for j in range(CONV_W))
    conv = jax.nn.silu(conv.astype(f32)).reshape(B, T, 3, G_HEADS, G_HD)
    gq = l2norm(conv[:, :, 0]) * (G_HD ** -0.5)
    gk = l2norm(conv[:, :, 1])
    gab = parts['g_ab'].reshape(B, T, 2, G_HEADS).astype(f32)
    decay = -jnp.exp(lp['g_A_log'].astype(f32)) * jax.nn.softplus(gab[:, :, 0] + lp['g_dt_bias'].astype(f32))
    beta = jax.nn.sigmoid(gab[:, :, 1])
    og, Sn = gdn_chunked(gq, gk, conv[:, :, 2], beta, decay, S0)
    o_g = rmsnorm(og, lp['g_hn']).reshape(B, T, W_G) * jax.nn.silu(parts['g_z'].astype(f32))

    br = jnp.stack([o_a, o_m, o_g], axis=2)
    proj = jnp.einsum('btie,ied->btid', br, lp['w_branch'].astype(f32))
    mg = jax.nn.sigmoid(parts['merge'].reshape(B, T, N_BRANCH, D_MODEL).astype(f32))
    y = (mg * proj).sum(axis=2).astype(dt)
    x = x + y @ lp['w_out']
    new_state = {
        'cmp': new_cmp, 'sel': new_sel, 'win': win[:, -min(WINDOW, win.shape[1]):],
        'mC': Cn.astype(dt), 'mn': nn_.astype(dt), 'mm': mn.astype(dt),
        'gS': Sn.astype(dt), 'gconv': full[:, -(CONV_W - 1):],
    }
    return x, new_state


def setup_inputs(seed: int = 0) -> dict:
    key = jax.random.key(seed)
    keys = iter(jax.random.split(key, 40))

    def nrm(shape, scale=1.0):
        return scale * jax.random.normal(next(keys), shape, jnp.float32)

    n_pages = PAST_LEN // PAGE_SIZE
    n_pool = (DEC_BATCH * n_pages * 5) // 4
    w_buf = min(WINDOW, PAST_LEN)
    page_table = jax.random.permutation(next(keys), n_pool)[:DEC_BATCH * n_pages].reshape(DEC_BATCH, n_pages).astype(jnp.int32)
    dt_init = jnp.exp(jax.random.uniform(next(keys), (DEPTH, G_HEADS), jnp.float32, math.log(1e-3), math.log(1e-1)))
    a_log = jnp.log(jax.random.uniform(next(keys), (DEPTH, G_HEADS), jnp.float32, 1.0, 16.0))
    return {
        'x_prompt': nrm((BATCH, SEQ, D_MODEL)),
        'x_sample': nrm((DEC_BATCH, DEC_SEQ, D_MODEL)),
        'cache_cmp_kv': nrm((DEPTH, n_pool, PAGE_SIZE, 2, A_KV, A_HD)),
        'cache_sel_kv': nrm((DEPTH, n_pool, PAGE_SIZE, 2, A_KV, A_HD)),
        'cache_win_kv': nrm((DEPTH, DEC_BATCH, w_buf, 2, A_KV, A_HD)),
        'state_mlstm_C': nrm((DEPTH, DEC_BATCH, M_HEADS, M_HD, M_HD), 0.1),
        'state_mlstm_n': nrm((DEPTH, DEC_BATCH, M_HEADS, M_HD), 0.5),
        'state_mlstm_m': nrm((DEPTH, DEC_BATCH, M_HEADS)),
        'state_gdn_S': nrm((DEPTH, DEC_BATCH, G_HEADS, G_HD, G_HD), 0.1),
        'state_gdn_conv': nrm((DEPTH, DEC_BATCH, CONV_W - 1, 3 * W_G)),
        'page_table': page_table,
        'norm_g': 1.0 + nrm((DEPTH, D_MODEL), 0.05),
        'w_in': nrm((DEPTH, D_MODEL, N_IN), D_MODEL ** -0.5),
        'a_qn': 1.0 + nrm((DEPTH, A_HD), 0.05),
        'a_kn': 1.0 + nrm((DEPTH, 3, A_HD), 0.05),
        'a_cmp_wk': nrm((DEPTH, A_KV, CMP_BLOCK, A_HD, A_HD), (CMP_BLOCK * A_HD) ** -0.5 * 4.0),
        'a_cmp_wv': nrm((DEPTH, A_KV, CMP_BLOCK, A_HD, A_HD), (CMP_BLOCK * A_HD) ** -0.5 * 4.0),
        'rel_bias': nrm((N_BUCKETS, A_HEADS), 0.5),
        'm_bi': nrm((DEPTH, M_HEADS), 0.1),
        'm_bf': jnp.linspace(3.0, 6.0, M_HEADS)[None, :] + nrm((DEPTH, M_HEADS), 0.1),
        'm_hn': 1.0 + nrm((DEPTH, M_HD), 0.05),
        'g_conv': nrm((DEPTH, CONV_W, 3 * W_G), CONV_W ** -0.5),
        'g_A_log': a_log,
        'g_dt_bias': dt_init + jnp.log(-jnp.expm1(-dt_init)),
        'g_hn': 1.0 + nrm((DEPTH, G_HD), 0.05),
        'w_branch': nrm((DEPTH, N_BRANCH, W_A, D_MODEL), W_A ** -0.5),
        'w_out': nrm((DEPTH, D_MODEL, D_MODEL), D_MODEL ** -0.5),
    }


def reference(x_prompt, x_sample, cache_cmp_kv, cache_sel_kv, cache_win_kv, state_mlstm_C, state_mlstm_n,
              state_mlstm_m, state_gdn_S, state_gdn_conv, page_table, norm_g, w_in, a_qn, a_kn, a_cmp_wk,
              a_cmp_wv, rel_bias, m_bi, m_bf, m_hn, g_conv, g_A_log, g_dt_bias, g_hn, w_branch, w_out):
    layer_w = {
        'norm_g': norm_g, 'w_in': w_in, 'a_qn': a_qn, 'a_kn': a_kn, 'a_cmp_wk': a_cmp_wk, 'a_cmp_wv': a_cmp_wv,
        'm_bi': m_bi, 'm_bf': m_bf, 'm_hn': m_hn, 'g_conv': g_conv, 'g_A_log': g_A_log,
        'g_dt_bias': g_dt_bias, 'g_hn': g_hn, 'w_branch': w_branch, 'w_out': w_out,
    }
    names = ('cmp', 'sel', 'win', 'mC', 'mn', 'mm', 'gS', 'gconv')
    st_p = {k: [] for k in names}
    st_s = {k: [] for k in names}
    y_prompt, y_sample = x_prompt, x_sample
    for l in range(DEPTH):
        lp = {name: arr[l] for name, arr in layer_w.items()}
        y_prompt, new_p = mixer_layer(y_prompt, lp, rel_bias, None, 0)
        past = {
            'cmp': gather_pages(cache_cmp_kv[l], page_table),
            'sel': gather_pages(cache_sel_kv[l], page_table),
            'win': cache_win_kv[l], 'mC': state_mlstm_C[l], 'mn': state_mlstm_n[l], 'mm': state_mlstm_m[l],
            'gS': state_gdn_S[l], 'gconv': state_gdn_conv[l],
        }
        y_sample, new_s = mixer_layer(y_sample, lp, rel_bias, past, PAST_LEN)
        for k in names:
            st_p[k].append(new_p[k])
            st_s[k].append(new_s[k])
    P = {k: jnp.stack(v) for k, v in st_p.items()}
    S = {k: jnp.stack(v) for k, v in st_s.items()}
    return (y_prompt, y_sample, P['cmp'], S['cmp'], P['sel'], S['sel'], P['win'], S['win'],
            P['mC'], S['mC'], P['mn'], S['mn'], P['mm'], S['mm'], P['gS'], S['gS'], P['gconv'], S['gconv'])
```

```python
import functools
import math

import jax
import jax.numpy as jnp
from jax import lax
from jax.experimental import pallas as pl
from jax.experimental.pallas import tpu as pltpu

D_MODEL = 1024
DEPTH = 2
PAGE_SIZE = 128
A_HEADS = 8
A_KV = 2
A_REP = A_HEADS // A_KV
A_HD = 64
CMP_BLOCK = 32
CMP_STRIDE = 16
SEL_BLOCK = 64
TOP_N = 16
WINDOW = 512
Q_BLOCK = 128
N_BUCKETS = 32
MAX_DIST = 2048
M_HEADS = 4
M_HD = 128
M_CHUNK = 64
G_HEADS = 4
G_HD = 128
G_CHUNK = 64
CONV_W = 4
W_A = A_HEADS * A_HD
W_M = M_HEADS * M_HD
W_G = G_HEADS * G_HD
N_BRANCH = 3
EPS = 1e-6
NEG = -1e30
TINY = 1e-30

F32 = jnp.float32
BF16 = jnp.bfloat16
HI = lax.Precision.HIGHEST
NT = (((1,), (1,)), ((), ()))
TN = (((0,), (0,)), ((), ()))

LANES = 128
VMEM_LIMIT = 56 * 1024 * 1024

IN_ORDER = ('a_q', 'a_kv', 'a_gate', 'a_z', 'm_qkv', 'm_if', 'm_o', 'm_z', 'g_qkv', 'g_ab', 'g_z', 'merge')
IN_WIDTH = dict(a_q=W_A, a_kv=3 * 2 * A_KV * A_HD, a_gate=A_HEADS * 3, a_z=W_A, m_qkv=3 * W_M, m_if=2 * M_HEADS,
                m_o=W_M, m_z=W_M, g_qkv=3 * W_G, g_ab=2 * G_HEADS, g_z=W_G, merge=N_BRANCH * D_MODEL)
OFF = dict(merge=0, m_qkv=3072, g_qkv=4608, a_q=6144, a_z=6656, m_o=7168, m_z=7680, g_z=8192, a_kv=8704,
           small=9472)
N_PERM = 9600
SM_GATE, SM_MIF, SM_GAB = 0, 24, 32


def _cparams(sem):
    return pltpu.CompilerParams(dimension_semantics=sem, vmem_limit_bytes=VMEM_LIMIT)


def _silu(x):
    return x * jax.nn.sigmoid(x)


def _log_sigmoid(x):
    return jnp.minimum(x, 0.0) - jnp.log(1.0 + jnp.exp(-jnp.abs(x)))


def _softplus(x):
    return jnp.maximum(x, 0.0) + jnp.log(1.0 + jnp.exp(-jnp.abs(x)))


def _permute_w_in(w_in):
    parts, off = {}, 0
    for name in IN_ORDER:
        parts[name] = w_in[:, off:off + IN_WIDTH[name]]
        off += IN_WIDTH[name]
    small = jnp.concatenate([parts['a_gate'], parts['m_if'], parts['g_ab'],
                             jnp.zeros((w_in.shape[0], LANES - 40), w_in.dtype)], axis=1)
    cols = [parts['merge'], parts['m_qkv'], parts['g_qkv'], parts['a_q'], parts['a_z'], parts['m_o'],
            parts['m_z'], parts['g_z'], parts['a_kv'], small]
    return jnp.concatenate(cols, axis=1).astype(BF16)


def _rel_bucket(dist):
    n = jnp.maximum(dist, 0)
    exact = N_BUCKETS // 2
    nf = jnp.maximum(n, exact).astype(F32)
    large = exact + (jnp.log(nf / exact) / math.log(MAX_DIST / exact) * (N_BUCKETS - exact)).astype(jnp.int32)
    return jnp.where(n < exact, n, jnp.minimum(large, N_BUCKETS - 1))


def _bias_rows(rel_bias, dist):
    b = rel_bias[_rel_bucket(dist)].astype(F32)
    lead = dist.shape[:-2]
    Q, K = dist.shape[-2:]
    b = b.reshape(lead + (Q, K, A_KV, A_REP))
    nl = len(lead)
    b = jnp.transpose(b, tuple(range(nl)) + (nl + 2, nl + 3, nl, nl + 1))
    return b.reshape(lead + (A_KV, A_REP * Q, K))


def _cmp_weights(w):
    wr = w.reshape(A_KV, 2, CMP_STRIDE, A_HD, A_HD)
    eye = jnp.eye(A_KV, dtype=w.dtype)
    full = jnp.einsum('gmjde,gh->jgdmhe', wr, eye)
    return full.reshape(CMP_STRIDE, A_KV * A_HD, 2 * A_KV * A_HD).astype(BF16)


def _cover_t(ns_pad, nch):
    s0 = jnp.arange(ns_pad)[:, None] * SEL_BLOCK
    c0 = jnp.arange(nch)[None, :] * CMP_STRIDE
    return ((c0 < s0 + SEL_BLOCK) & (s0 <= c0 + CMP_BLOCK - 1)).astype(F32)


def _expand_mat(ns_pad, nk):
    return (jnp.arange(nk)[None, :] // SEL_BLOCK == jnp.arange(ns_pad)[:, None]).astype(BF16)


def _proj_in_kernel(x_ref, g_ref, w_ref, o_ref, hn_ref):
    @pl.when(pl.program_id(1) == 0)
    def _():
        x = x_ref[...]
        ms = jnp.mean(x * x, axis=-1, keepdims=True)
        hn_ref[...] = (x * lax.rsqrt(ms + EPS) * g_ref[...]).astype(BF16)

    o_ref[...] = jnp.dot(hn_ref[...], w_ref[...], preferred_element_type=F32)


def _proj_in(x2, norm_g, w_perm):
    n = x2.shape[0]
    tm = min(1024, n)
    tn = 1920
    return pl.pallas_call(
        _proj_in_kernel,
        grid=(n // tm, N_PERM // tn),
        in_specs=[pl.BlockSpec((tm, D_MODEL), lambda i, j: (i, 0)),
                  pl.BlockSpec((1, D_MODEL), lambda i, j: (0, 0)),
                  pl.BlockSpec((D_MODEL, tn), lambda i, j: (0, j))],
        out_specs=pl.BlockSpec((tm, tn), lambda i, j: (i, j)),
        out_shape=jax.ShapeDtypeStruct((n, N_PERM), F32),
        scratch_shapes=[pltpu.VMEM((tm, D_MODEL), BF16)],
        compiler_params=_cparams(("parallel", "arbitrary")),
        name="proj_in",
    )(x2, norm_g.reshape(1, D_MODEL), w_perm)


def _prep_a_kernel(q_ref, kv0_ref, kv1_ref, kv2_ref, bdq_ref, bdk_ref, qg_ref, kg_ref, qo_ref, ro_ref):
    q = q_ref[...]
    ms = jnp.dot(q * q, bdq_ref[...], precision=HI, preferred_element_type=F32) * (1.0 / A_HD)
    qo_ref[...] = q * lax.rsqrt(ms + EPS) * qg_ref[...] * (A_HD ** -0.5)
    for b, kv_ref in enumerate((kv0_ref, kv1_ref, kv2_ref)):
        k = kv_ref[:, 0:128]
        ms = jnp.dot(k * k, bdk_ref[...], precision=HI, preferred_element_type=F32) * (1.0 / A_HD)
        ro_ref[2 * b] = k * lax.rsqrt(ms + EPS) * kg_ref[b:b + 1, :]
        ro_ref[2 * b + 1] = kv_ref[:, 128:256]


def _prep_a(u2, a_qn, a_kn):
    n = u2.shape[0]
    tm = min(512, n)
    bd = lambda w: (jnp.arange(w)[:, None] // A_HD == jnp.arange(w)[None, :] // A_HD).astype(F32)
    qg = jnp.tile(a_qn, A_HEADS).reshape(1, W_A)
    kg = jnp.tile(a_kn, (1, A_KV))
    const = lambda shape: pl.BlockSpec(shape, lambda i: (0, 0))
    kvs = lambda b: pl.BlockSpec((tm, 256), lambda i: (i, OFF['a_kv'] // 256 + b))
    return pl.pallas_call(
        _prep_a_kernel,
        grid=(n // tm,),
        in_specs=[pl.BlockSpec((tm, W_A), lambda i: (i, OFF['a_q'] // W_A)), kvs(0), kvs(1), kvs(2),
                  const((W_A, W_A)), const((128, 128)), const((1, W_A)), const((3, 128))],
        out_specs=[pl.BlockSpec((tm, W_A), lambda i: (i, 0)), pl.BlockSpec((6, tm, 128), lambda i: (0, i, 0))],
        out_shape=[jax.ShapeDtypeStruct((n, W_A), F32), jax.ShapeDtypeStruct((6, n, 128), F32)],
        compiler_params=_cparams(("parallel",)),
        name="prep_a",
    )(u2, u2, u2, u2, bd(W_A), bd(128), qg, kg)


def _stack_heads(q, g):
    return jnp.concatenate([q[:, (A_REP * g + r) * A_HD:(A_REP * g + r + 1) * A_HD] for r in range(A_REP)], axis=0)


def _rep_rows(x):
    return jnp.concatenate([x] * A_REP, axis=0)


def _masked_softmax(s, valid):
    sm = jnp.where(valid, s, NEG)
    m = jnp.max(sm, axis=-1, keepdims=True)
    e = jnp.where(valid, jnp.exp(sm - m), 0.0)
    l = jnp.sum(e, axis=-1, keepdims=True)
    return e / jnp.maximum(l, TINY)


def _select_blocks(pc_sum, covt_ref, st_ref, t_row, ns, n_top):
    ns_pad = covt_ref.shape[0]
    imp = lax.dot_general(covt_ref[...], pc_sum, NT, precision=HI, preferred_element_type=F32)
    jj = lax.broadcasted_iota(jnp.int32, (ns_pad, LANES), 0)
    tt = jnp.broadcast_to(t_row, (ns_pad, LANES))
    cur = tt // SEL_BLOCK
    forced = (jj == 0) | (jj == cur) | (jj == cur - 1)
    future = jj * SEL_BLOCK > tt
    score = jnp.where(future, NEG, jnp.where(forced, -NEG, imp))
    score = jnp.where(jj < ns, score, -jnp.inf)
    st_ref[...] = score

    def beats(k, rank):
        row = jnp.broadcast_to(st_ref[pl.ds(k, 1), :], (ns_pad, LANES))
        b = (row > score) | ((row == score) & (k < jj))
        return rank + jnp.where(b, 1.0, 0.0)

    rank = jnp.zeros((ns_pad, LANES), F32)
    if ns <= 32:
        for k in range(ns):
            rank = beats(k, rank)
    else:
        rank = lax.fori_loop(0, ns, beats, rank)
    return jnp.where(rank < n_top, 1.0, 0.0)


def _pad_rows(x, rows):
    if x.shape[0] == rows:
        return x
    return jnp.concatenate([x, jnp.zeros((rows - x.shape[0], x.shape[1]), x.dtype)], axis=0)


def _compress(src_ref, nrow, wk_ref, wv_ref, pbuf_ref, kc_ref, vc_ref):
    acck = jnp.zeros((nrow, 256), F32)
    accv = jnp.zeros((nrow, 256), F32)
    for j in range(CMP_STRIDE):
        xk = src_ref[0, pl.ds(j, nrow, stride=CMP_STRIDE), :].astype(BF16)
        xv = src_ref[1, pl.ds(j, nrow, stride=CMP_STRIDE), :].astype(BF16)
        acck = acck + jnp.dot(xk, wk_ref[j], preferred_element_type=F32)
        accv = accv + jnp.dot(xv, wv_ref[j], preferred_element_type=F32)
    nout = kc_ref.shape[0]
    for acc, dst in ((acck, kc_ref), (accv, vc_ref)):
        pbuf_ref[0:nrow, :] = acc[:, 128:256]
        dst[...] = acc[0:nout, 0:128] + pbuf_ref[pl.ds(1, nout), :]


def _nsa_prompt_kernel(q_ref, rows_ref, small_ref, wk_ref, wv_ref, bcmp_ref, tsel_ref, covt_ref, e_ref,
                       o_ref, kc_ref, vc_ref, pbuf_ref, msel_ref, mwin_ref, acc_ref, m_ref, st_ref,
                       *, ns, n_top):
    bi = pl.program_id(1)
    T = rows_ref.shape[1]
    nch = T // CMP_STRIDE
    nkb = T // Q_BLOCK
    QB = Q_BLOCK

    @pl.when(bi == 0)
    def _():
        pbuf_ref[...] = jnp.zeros(pbuf_ref.shape, F32)
        _compress(rows_ref, nch, wk_ref, wv_ref, pbuf_ref, kc_ref, vc_ref)

    t0 = bi * QB
    q = q_ref[0]
    gates = jax.nn.sigmoid(small_ref[0])
    ti = lax.broadcasted_iota(jnp.int32, (QB, T), 0) + t0
    dist = ti - lax.broadcasted_iota(jnp.int32, (QB, T), 1)
    causal = dist >= 0
    wmask = jnp.where(causal & (dist < WINDOW), 1.0, 0.0)
    for kb in range(nkb):
        mwin_ref[kb] = wmask[:, kb * QB:(kb + 1) * QB]
    tc = lax.broadcasted_iota(jnp.int32, (QB, nch), 0) + t0
    cend = lax.broadcasted_iota(jnp.int32, (QB, nch), 1) * CMP_STRIDE + (CMP_BLOCK - 1)
    cvalid = _rep_rows(tc - cend >= 0)
    t_row = lax.broadcasted_iota(jnp.int32, (1, LANES), 1) + t0
    lane = lax.broadcasted_iota(jnp.int32, (QB, LANES), 1)

    for g in range(A_KV):
        qg = _stack_heads(q, g).astype(BF16)
        in_g = (lane >= g * A_HD) & (lane < (g + 1) * A_HD)

        kc = kc_ref[:, g * A_HD:(g + 1) * A_HD].astype(BF16)
        vc = vc_ref[:, g * A_HD:(g + 1) * A_HD].astype(BF16)
        s = lax.dot_general(qg, kc, NT, preferred_element_type=F32) + bcmp_ref[0, g]
        p_c = _masked_softmax(s, cvalid)
        o_c = jnp.dot(p_c.astype(BF16), vc, preferred_element_type=F32)
        pc_sum = p_c[0:QB] + p_c[QB:2 * QB] + p_c[2 * QB:3 * QB] + p_c[3 * QB:4 * QB]

        sel_t = _select_blocks(pc_sum, covt_ref, st_ref, t_row, ns, n_top)
        sel = _pad_rows(sel_t, LANES).T.astype(BF16)
        kmask = jnp.dot(sel, e_ref[...], preferred_element_type=F32)
        kmask = jnp.where(causal, kmask, 0.0)
        for kb in range(nkb):
            msel_ref[kb] = kmask[:, kb * QB:(kb + 1) * QB]

        def flash(br, mask_ref, lo, hi):
            m_ref[...] = jnp.full(m_ref.shape, NEG, F32)
            acc_ref[...] = jnp.zeros(acc_ref.shape, F32)

            def body(kb, carry):
                off = pl.multiple_of(kb * QB, QB)
                k = rows_ref[2 * br, pl.ds(off, QB), g * A_HD:(g + 1) * A_HD].astype(BF16)
                v = rows_ref[2 * br + 1, pl.ds(off, QB), :]
                v1 = jnp.where(in_g, v, 1.0).astype(BF16)
                sc = lax.dot_general(qg, k, NT, preferred_element_type=F32) + tsel_ref[bi - kb, g]
                valid = _rep_rows(mask_ref[kb]) > 0.5
                sm = jnp.where(valid, sc, NEG)
                m_prev = m_ref[...]
                m_new = jnp.maximum(m_prev, jnp.max(sm, axis=-1, keepdims=True))
                p = jnp.where(valid, jnp.exp(sm - m_new), 0.0)
                alpha = jnp.exp(m_prev - m_new)
                acc_ref[...] = alpha * acc_ref[...] + jnp.dot(p.astype(BF16), v1, preferred_element_type=F32)
                m_ref[...] = m_new
                return carry

            lax.fori_loop(lo, hi, body, 0)
            acc = acc_ref[...]
            l = acc[:, (1 - g) * A_HD:(1 - g) * A_HD + 1]
            return acc[:, g * A_HD:(g + 1) * A_HD] / jnp.maximum(l, TINY)

        o_s = flash(1, msel_ref, 0, bi + 1)
        o_w = flash(2, mwin_ref, jnp.maximum(bi - WINDOW // QB, 0), bi + 1)

        for r in range(A_REP):
            h = A_REP * g + r
            rs = slice(r * QB, (r + 1) * QB)
            out = (gates[:, 3 * h:3 * h + 1] * o_c[rs] + gates[:, 3 * h + 1:3 * h + 2] * o_s[rs]
                   + gates[:, 3 * h + 2:3 * h + 3] * o_w[rs])
            o_ref[0, :, h * A_HD:(h + 1) * A_HD] = out


def _nsa_prompt(qn3, rows6, u3, wk, wv, rel_bias):
    B, T, _ = qn3.shape
    QB = Q_BLOCK
    nqb = T // QB
    nch = T // CMP_STRIDE
    ns = T // SEL_BLOCK
    ns_pad = -(-ns // 8) * 8
    n_top = min(TOP_N, ns)
    i_ = jnp.arange(QB)
    t = (jnp.arange(nqb) * QB)[:, None, None] + i_[None, :, None]
    cend = (jnp.arange(nch) * CMP_STRIDE + CMP_BLOCK - 1)[None, None, :]
    bcmp = _bias_rows(rel_bias, t - cend)
    d = (jnp.arange(nqb) * QB)[:, None, None] + i_[None, :, None] - i_[None, None, :]
    tsel = _bias_rows(rel_bias, d)
    covt = _cover_t(ns_pad, nch)
    emat = _expand_mat(LANES, T)
    kern = functools.partial(_nsa_prompt_kernel, ns=ns, n_top=n_top)
    c2 = lambda shape: pl.BlockSpec(shape, lambda b, i: (0,) * len(shape))
    return pl.pallas_call(
        kern,
        grid=(B, nqb),
        in_specs=[pl.BlockSpec((1, QB, W_A), lambda b, i: (b, i, 0)),
                  pl.BlockSpec((6, T, 128), lambda b, i: (0, b, 0)),
                  pl.BlockSpec((1, QB, LANES), lambda b, i: (b, i, OFF['small'] // LANES)),
                  c2((CMP_STRIDE, 128, 256)), c2((CMP_STRIDE, 128, 256)),
                  pl.BlockSpec((1, A_KV, A_REP * QB, nch), lambda b, i: (i, 0, 0, 0)),
                  c2((nqb, A_KV, A_REP * QB, QB)),
                  c2((ns_pad, nch)), c2((LANES, T))],
        out_specs=pl.BlockSpec((1, QB, W_A), lambda b, i: (b, i, 0)),
        out_shape=jax.ShapeDtypeStruct((B, T, W_A), F32),
        scratch_shapes=[pltpu.VMEM((nch, 128), F32), pltpu.VMEM((nch, 128), F32),
                        pltpu.VMEM((nch + 8, 128), F32),
                        pltpu.VMEM((nqb, QB, QB), F32), pltpu.VMEM((nqb, QB, QB), F32),
                        pltpu.VMEM((A_REP * QB, LANES), F32), pltpu.VMEM((A_REP * QB, LANES), F32),
                        pltpu.VMEM((ns_pad, LANES), F32)],
        compiler_params=_cparams(("parallel", "arbitrary")),
        name="nsa_prompt",
    )(qn3, rows6, u3, wk, wv, bcmp, tsel, covt, emat)


def _nsa_sample_kernel(pt_ref, *refs, pg, past, tp, ns, n_top):
    cmp_refs = refs[0:pg]
    sel_refs = refs[pg:2 * pg]
    (q_ref, rows_ref, small_ref, win_ref, wk_ref, wv_ref, bcmp_ref, bsel_ref, bwin_ref, covt_ref, e_ref,
     o_ref, cslab, sslab, wslab, kc_ref, vc_ref, pbuf_ref, st_ref, s_ref) = refs[2 * pg:]
    p = pl.program_id(1)
    npg = pl.num_programs(1)
    SL = cslab.shape[1]
    nblk = kc_ref.shape[0]
    WK = wslab.shape[1]

    for k in range(pg):
        off = pl.multiple_of((p * pg + k) * PAGE_SIZE, PAGE_SIZE)
        for kv in range(2):
            cslab[kv, pl.ds(off, PAGE_SIZE), :] = cmp_refs[k][:, kv * 128:(kv + 1) * 128]
            sslab[kv, pl.ds(off, PAGE_SIZE), :] = sel_refs[k][:, kv * 128:(kv + 1) * 128]

    @pl.when(p == npg - 1)
    def _():
        tail = jnp.zeros((SL - past - tp, 128), F32)
        for kv in range(2):
            cslab[kv, past:past + tp, :] = rows_ref[kv]
            cslab[kv, past + tp:SL, :] = tail
            sslab[kv, past:past + tp, :] = rows_ref[2 + kv]
            sslab[kv, past + tp:SL, :] = tail
            wslab[kv, 0:WINDOW, :] = win_ref[0, :, kv * 128:(kv + 1) * 128]
            wslab[kv, WINDOW:WINDOW + tp, :] = rows_ref[4 + kv]
            wslab[kv, WINDOW + tp:WK, :] = jnp.zeros((WK - WINDOW - tp, 128), F32)
        pbuf_ref[...] = jnp.zeros(pbuf_ref.shape, F32)
        _compress(cslab, nblk + 8, wk_ref, wv_ref, pbuf_ref, kc_ref, vc_ref)

        q = q_ref[0]
        gates = jax.nn.sigmoid(small_ref[0])
        R = A_REP * tp
        ti = lax.broadcasted_iota(jnp.int32, (tp, SL), 0) + past
        causal = ti - lax.broadcasted_iota(jnp.int32, (tp, SL), 1) >= 0
        tc = lax.broadcasted_iota(jnp.int32, (tp, nblk), 0) + past
        cend = lax.broadcasted_iota(jnp.int32, (tp, nblk), 1) * CMP_STRIDE + (CMP_BLOCK - 1)
        cvalid = _rep_rows(tc - cend >= 0)
        wd = (lax.broadcasted_iota(jnp.int32, (tp, WK), 0) + WINDOW
              - lax.broadcasted_iota(jnp.int32, (tp, WK), 1))
        wvalid = _rep_rows((wd >= 0) & (wd < WINDOW))
        t_row = lax.broadcasted_iota(jnp.int32, (1, LANES), 1) + past
        nck = 5 if (SL // 5) % LANES == 0 else SL // LANES
        ck = SL // nck

        for g in range(A_KV):
            gs = slice(g * A_HD, (g + 1) * A_HD)
            qg = _stack_heads(q, g).astype(BF16)
            s = lax.dot_general(qg, kc_ref[:, gs].astype(BF16), NT, preferred_element_type=F32) + bcmp_ref[g]
            p_c = _masked_softmax(s, cvalid)
            o_c = jnp.dot(p_c.astype(BF16), vc_ref[:, gs].astype(BF16), preferred_element_type=F32)
            pc_sum = p_c[0:tp] + p_c[tp:2 * tp] + p_c[2 * tp:3 * tp] + p_c[3 * tp:4 * tp]

            sel_t = _select_blocks(_pad_rows(pc_sum, LANES), covt_ref, st_ref, t_row, ns, n_top)
            sel = sel_t.T[0:tp].astype(BF16)
            kmask = jnp.dot(sel, e_ref[...], preferred_element_type=F32)
            svalid = _rep_rows(causal & (kmask > 0.5))

            for c in range(nck):
                kk = sslab[0, c * ck:(c + 1) * ck, gs].astype(BF16)
                s_ref[:, c * ck:(c + 1) * ck] = lax.dot_general(qg, kk, NT, preferred_element_type=F32)
            p_s = _masked_softmax(s_ref[...] + bsel_ref[g], svalid)
            s_ref[...] = p_s
            o_s = jnp.zeros((R, A_HD), F32)
            for c in range(nck):
                vv = sslab[1, c * ck:(c + 1) * ck, gs].astype(BF16)
                o_s = o_s + jnp.dot(s_ref[:, c * ck:(c + 1) * ck].astype(BF16), vv, preferred_element_type=F32)

            kw = wslab[0, :, gs].astype(BF16)
            vw = wslab[1, :, gs].astype(BF16)
            sw = lax.dot_general(qg, kw, NT, preferred_element_type=F32) + bwin_ref[g]
            p_w = _masked_softmax(sw, wvalid)
            o_w = jnp.dot(p_w.astype(BF16), vw, preferred_element_type=F32)

            for r in range(A_REP):
                h = A_REP * g + r
                rs = slice(r * tp, (r + 1) * tp)
                out = (gates[:, 3 * h:3 * h + 1] * o_c[rs] + gates[:, 3 * h + 1:3 * h + 2] * o_s[rs]
                       + gates[:, 3 * h + 2:3 * h + 3] * o_w[rs])
                o_ref[0, :, h * A_HD:(h + 1) * A_HD] = out


def _nsa_sample(qn3, rows6, u3, pool_cmp, pool_sel, win3, page_table, wk, wv, rel_bias, past, t_real):
    B, tp, _ = qn3.shape
    n_pages = past // PAGE_SIZE
    pg = 8 if n_pages % 8 == 0 else n_pages
    npg = n_pages // pg
    SL = past + LANES
    nblk = past // CMP_STRIDE
    tk = past + t_real
    ns = -(-tk // SEL_BLOCK)
    ns_pad = -(-ns // LANES) * LANES
    n_top = min(TOP_N, ns)
    WK = WINDOW + LANES
    i_ = jnp.arange(tp)
    t = past + i_
    bcmp = _bias_rows(rel_bias, t[:, None] - (jnp.arange(nblk) * CMP_STRIDE + CMP_BLOCK - 1)[None, :])
    bsel = _bias_rows(rel_bias, t[:, None] - jnp.arange(SL)[None, :])
    bwin = _bias_rows(rel_bias, i_[:, None] + WINDOW - jnp.arange(WK)[None, :])
    covt = _cover_t(ns_pad, nblk)
    emat = _expand_mat(ns_pad, SL)
    kern = functools.partial(_nsa_sample_kernel, pg=pg, past=past, tp=tp, ns=ns, n_top=n_top)

    def page_spec(k):
        return pl.BlockSpec((None, PAGE_SIZE, 256), lambda b, p, pt: (pt[b, p * pg + k], 0, 0))

    def c_(shape):
        return pl.BlockSpec(shape, lambda b, p, pt: (0,) * len(shape))

    R = A_REP * tp
    grid_spec = pltpu.PrefetchScalarGridSpec(
        num_scalar_prefetch=1,
        grid=(B, npg),
        in_specs=([page_spec(k) for k in range(pg)] + [page_spec(k) for k in range(pg)]
                  + [pl.BlockSpec((1, tp, W_A), lambda b, p, pt: (b, 0, 0)),
                     pl.BlockSpec((6, tp, 128), lambda b, p, pt: (0, b, 0)),
                     pl.BlockSpec((1, tp, LANES), lambda b, p, pt: (b, 0, OFF['small'] // LANES)),
                     pl.BlockSpec((1, WINDOW, 256), lambda b, p, pt: (b, 0, 0)),
                     c_((CMP_STRIDE, 128, 256)), c_((CMP_STRIDE, 128, 256)),
                     c_((A_KV, R, nblk)), c_((A_KV, R, SL)), c_((A_KV, R, WK)),
                     c_((ns_pad, nblk)), c_((ns_pad, SL))]),
        out_specs=pl.BlockSpec((1, tp, W_A), lambda b, p, pt: (b, 0, 0)),
        scratch_shapes=[pltpu.VMEM((2, SL, 128), F32), pltpu.VMEM((2, SL, 128), F32),
                        pltpu.VMEM((2, WK, 128), F32),
                        pltpu.VMEM((nblk, 128), F32), pltpu.VMEM((nblk, 128), F32),
                        pltpu.VMEM((nblk + 16, 128), F32),
                        pltpu.VMEM((ns_pad, LANES), F32), pltpu.VMEM((R, SL), F32)],
    )
    return pl.pallas_call(
        kern,
        grid_spec=grid_spec,
        out_shape=jax.ShapeDtypeStruct((B, tp, W_A), F32),
        compiler_params=_cparams(("parallel", "arbitrary")),
        name="nsa_sample",
    )(page_table, *([pool_cmp] * pg), *([pool_sel] * pg), qn3, rows6, u3, win3, wk, wv,
      bcmp, bsel, bwin, covt, emat)


def _tri(L, lower_incl):
    r = lax.broadcasted_iota(jnp.int32, (L, L), 0)
    c = lax.broadcasted_iota(jnp.int32, (L, L), 1)
    return (r >= c) if lower_incl else (r > c)


def _mlstm_kernel(x_ref, small_ref, smallt_ref, brow_ref, bcol_ref, c0_ref, n0_ref, m0_ref,
                  h_ref, co_ref, no_ref, mo_ref, c_s, n_s, m_s, *, L, t_real):
    c = pl.program_id(1)
    nc = pl.num_programs(1)

    @pl.when(c == 0)
    def _():
        c_s[...] = c0_ref[0]
        n_s[...] = n0_ref[0]
        m_s[...] = m0_ref[0]

    sm = small_ref[0]
    smt = smallt_ref[0, 0]
    li_col = sm[:, SM_MIF:SM_MIF + 4] + brow_ref[0:1, 0:4]
    lf_col = _log_sigmoid(sm[:, SM_MIF + 4:SM_MIF + 8] + brow_ref[0:1, 4:8])
    li_row = smt[0:4, :] + bcol_ref[0:4, :]
    lf_row = _log_sigmoid(smt[4:8, :] + bcol_ref[4:8, :])
    if t_real % L:
        tcol = lax.broadcasted_iota(jnp.int32, (L, 4), 0) + c * L
        trow = lax.broadcasted_iota(jnp.int32, (4, L), 1) + c * L
        li_col = jnp.where(tcol < t_real, li_col, NEG)
        lf_col = jnp.where(tcol < t_real, lf_col, 0.0)
        li_row = jnp.where(trow < t_real, li_row, NEG)
        lf_row = jnp.where(trow < t_real, lf_row, 0.0)
    low = _tri(L, True)
    b_col = jnp.dot(low.astype(F32), lf_col, precision=HI, preferred_element_type=F32)
    b_row = lax.dot_general(lf_row, low.astype(F32), NT, precision=HI, preferred_element_type=F32)

    for h in range(M_HEADS):
        hs = slice(h * M_HD, (h + 1) * M_HD)
        q = x_ref[0, :, h * M_HD:(h + 1) * M_HD]
        k = x_ref[0, :, W_M + h * M_HD:W_M + (h + 1) * M_HD] * (M_HD ** -0.5)
        v = x_ref[0, :, 2 * W_M + h * M_HD:2 * W_M + (h + 1) * M_HD]
        bc = b_col[:, h:h + 1]
        br = b_row[h:h + 1, :]
        Dm = jnp.where(low, bc - br + li_row[h:h + 1, :], NEG)
        m_prev = m_s[h][:, 0:1]
        a = bc + m_prev
        mt = jnp.maximum(a, jnp.max(Dm, axis=-1, keepdims=True))
        S = lax.dot_general(q, k, NT, precision=HI, preferred_element_type=F32) * jnp.exp(Dm - mt)
        inter = jnp.exp(a - mt)
        C = c_s[h]
        n = n_s[h]
        num = inter * jnp.dot(q, C, precision=HI, preferred_element_type=F32) \
            + jnp.dot(S, v, precision=HI, preferred_element_type=F32)
        den = inter * jnp.sum(q * n, axis=-1, keepdims=True) + jnp.sum(S, axis=-1, keepdims=True)
        h_ref[0, :, hs] = num / jnp.maximum(jnp.abs(den), jnp.exp(-mt))
        bL = bc[L - 1:L, :]
        wlog = bL - bc + li_col[:, h:h + 1]
        m_new = jnp.maximum(bL + m_prev, jnp.max(wlog, axis=0, keepdims=True))
        ws = jnp.exp(wlog - m_new)
        dec = jnp.exp(bL + m_prev - m_new)
        kw = k * ws
        c_s[h] = dec * C + lax.dot_general(kw, v, TN, precision=HI, preferred_element_type=F32)
        n_s[h] = dec * n + jnp.sum(kw, axis=0, keepdims=True)
        m_s[h] = jnp.broadcast_to(m_new, (1, LANES))

    @pl.when(c == nc - 1)
    def _():
        co_ref[0] = c_s[...]
        no_ref[0] = n_s[...]
        mo_ref[0] = m_s[...]


def _mlstm(u3, smallt, m_bi, m_bf, C0, n0, m0, L, t_real):
    B, T, _ = u3.shape
    nc = T // L
    brow = jnp.concatenate([m_bi, m_bf]).reshape(1, 8)
    bcol = jnp.concatenate([m_bi, m_bf]).reshape(8, 1)
    n0 = n0.reshape(B, M_HEADS, 1, M_HD)
    m0 = jnp.broadcast_to(m0[:, :, None, None], (B, M_HEADS, 1, LANES))
    kern = functools.partial(_mlstm_kernel, L=L, t_real=t_real)
    st = lambda shape: pl.BlockSpec(shape, lambda b, c: (b,) + (0,) * (len(shape) - 1))
    h, Cn, nn, mn = pl.pallas_call(
        kern,
        grid=(B, nc),
        in_specs=[pl.BlockSpec((1, L, 3 * W_M), lambda b, c: (b, c, OFF['m_qkv'] // (3 * W_M))),
                  pl.BlockSpec((1, L, LANES), lambda b, c: (b, c, OFF['small'] // LANES)),
                  pl.BlockSpec((1, 1, 16, L), lambda b, c: (b, c, 0, 0)),
                  pl.BlockSpec((1, 8), lambda b, c: (0, 0)), pl.BlockSpec((8, 1), lambda b, c: (0, 0)),
                  st((1, M_HEADS, M_HD, M_HD)), st((1, M_HEADS, 1, M_HD)), st((1, M_HEADS, 1, LANES))],
        out_specs=[pl.BlockSpec((1, L, W_M), lambda b, c: (b, c, 0)),
                   st((1, M_HEADS, M_HD, M_HD)), st((1, M_HEADS, 1, M_HD)), st((1, M_HEADS, 1, LANES))],
        out_shape=[jax.ShapeDtypeStruct((B, T, W_M), F32),
                   jax.ShapeDtypeStruct((B, M_HEADS, M_HD, M_HD), F32),
                   jax.ShapeDtypeStruct((B, M_HEADS, 1, M_HD), F32),
                   jax.ShapeDtypeStruct((B, M_HEADS, 1, LANES), F32)],
        scratch_shapes=[pltpu.VMEM((M_HEADS, M_HD, M_HD), F32), pltpu.VMEM((M_HEADS, 1, M_HD), F32),
                        pltpu.VMEM((M_HEADS, 1, LANES), F32)],
        compiler_params=_cparams(("parallel", "arbitrary")),
        name="mlstm",
    )(u3, u3, smallt, brow, bcol, C0, n0, m0)
    return h, Cn, nn.reshape(B, M_HEADS, M_HD), mn[:, :, 0, 0]


def _gdn_kernel(x_ref, small_ref, smallt_ref, cw_ref, prow_ref, pcol_ref, buf_ref, s0_ref,
                o_ref, so_ref, xbuf, s_s, *, L, t_real):
    c = pl.program_id(1)
    nc = pl.num_programs(1)
    PRE = 8

    @pl.when(c == 0)
    def _():
        s_s[...] = s0_ref[0]
        xbuf[0:PRE, :] = buf_ref[0]

    xbuf[PRE:PRE + L, :] = x_ref[0]
    conv = jnp.zeros((L, 3 * W_G), F32)
    for j in range(CONV_W):
        conv = conv + xbuf[pl.ds(PRE - (CONV_W - 1) + j, L), :] * cw_ref[j:j + 1, :]
    xbuf[0:PRE, :] = xbuf[L:L + PRE, :]
    conv = _silu(conv)

    sm = small_ref[0]
    smt = smallt_ref[0, 0]
    g_col = -jnp.exp(prow_ref[0:1, 0:4]) * _softplus(sm[:, SM_GAB:SM_GAB + 4] + prow_ref[1:2, 0:4])
    beta_col = jax.nn.sigmoid(sm[:, SM_GAB + 4:SM_GAB + 8])
    g_row = -jnp.exp(pcol_ref[0:4, 0:1]) * _softplus(smt[8:12, :] + pcol_ref[0:4, 1:2])
    if t_real % L:
        tcol = lax.broadcasted_iota(jnp.int32, (L, 4), 0) + c * L
        trow = lax.broadcasted_iota(jnp.int32, (4, L), 1) + c * L
        g_col = jnp.where(tcol < t_real, g_col, 0.0)
        beta_col = jnp.where(tcol < t_real, beta_col, 0.0)
        g_row = jnp.where(trow < t_real, g_row, 0.0)
    low = _tri(L, True)
    strict = _tri(L, False)
    eye = (low & ~strict).astype(F32)
    G_col = jnp.dot(low.astype(F32), g_col, precision=HI, preferred_element_type=F32)
    G_row = lax.dot_general(g_row, low.astype(F32), NT, precision=HI, preferred_element_type=F32)

    for h in range(G_HEADS):
        hs = slice(h * G_HD, (h + 1) * G_HD)
        cq = conv[:, h * G_HD:(h + 1) * G_HD]
        ck = conv[:, W_G + h * G_HD:W_G + (h + 1) * G_HD]
        v = conv[:, 2 * W_G + h * G_HD:2 * W_G + (h + 1) * G_HD]
        q = cq * lax.rsqrt(jnp.sum(cq * cq, axis=-1, keepdims=True) + EPS) * (G_HD ** -0.5)
        k = ck * lax.rsqrt(jnp.sum(ck * ck, axis=-1, keepdims=True) + EPS)
        Gc = G_col[:, h:h + 1]
        Gr = G_row[h:h + 1, :]
        dmask = jnp.where(low, jnp.exp(jnp.where(low, Gc - Gr, 0.0)), 0.0)
        bcol = beta_col[:, h:h + 1]
        kb = k * bcol
        A = lax.dot_general(kb, k, NT, precision=HI, preferred_element_type=F32) * jnp.where(strict, dmask, 0.0)
        eG = jnp.exp(Gc)
        rhs = jnp.concatenate([v * bcol, kb * eG], axis=1)
        X = eye - A
        Pw = jnp.dot(A, A, precision=HI, preferred_element_type=F32)
        span = 2
        while span < L:
            X = X + jnp.dot(X, Pw, precision=HI, preferred_element_type=F32)
            span *= 2
            if span < L:
                Pw = jnp.dot(Pw, Pw, precision=HI, preferred_element_type=F32)
        sol = jnp.dot(X, rhs, precision=HI, preferred_element_type=F32)
        S = s_s[h]
        v_new = sol[:, 0:G_HD] - jnp.dot(sol[:, G_HD:2 * G_HD], S, precision=HI, preferred_element_type=F32)
        attn = lax.dot_general(q, k, NT, precision=HI, preferred_element_type=F32) * dmask
        o_ref[0, :, hs] = jnp.dot(q * eG, S, precision=HI, preferred_element_type=F32) \
            + jnp.dot(attn, v_new, precision=HI, preferred_element_type=F32)
        GL = Gc[L - 1:L, :]
        s_s[h] = jnp.exp(GL) * S + lax.dot_general(k * jnp.exp(GL - Gc), v_new, TN, precision=HI,
                                                   preferred_element_type=F32)

    @pl.when(c == nc - 1)
    def _():
        so_ref[0] = s_s[...]


def _gdn(u3, smallt, g_conv, g_A_log, g_dt_bias, buf, S0, L, t_real):
    B, T, _ = u3.shape
    nc = T // L
    prow = jnp.stack([g_A_log, g_dt_bias])
    pcol = jnp.stack([g_A_log, g_dt_bias], axis=1)
    buf8 = jnp.concatenate([jnp.zeros((B, 8 - (CONV_W - 1), 3 * W_G), F32), buf], axis=1)
    kern = functools.partial(_gdn_kernel, L=L, t_real=t_real)
    st = lambda shape: pl.BlockSpec(shape, lambda b, c: (b,) + (0,) * (len(shape) - 1))
    cst = lambda shape: pl.BlockSpec(shape, lambda b, c: (0,) * len(shape))
    return pl.pallas_call(
        kern,
        grid=(B, nc),
        in_specs=[pl.BlockSpec((1, L, 3 * W_G), lambda b, c: (b, c, OFF['g_qkv'] // (3 * W_G))),
                  pl.BlockSpec((1, L, LANES), lambda b, c: (b, c, OFF['small'] // LANES)),
                  pl.BlockSpec((1, 1, 16, L), lambda b, c: (b, c, 0, 0)),
                  cst((CONV_W, 3 * W_G)), cst((2, 4)), cst((4, 2)),
                  st((1, 8, 3 * W_G)), st((1, G_HEADS, G_HD, G_HD))],
        out_specs=[pl.BlockSpec((1, L, W_G), lambda b, c: (b, c, 0)), st((1, G_HEADS, G_HD, G_HD))],
        out_shape=[jax.ShapeDtypeStruct((B, T, W_G), F32),
                   jax.ShapeDtypeStruct((B, G_HEADS, G_HD, G_HD), F32)],
        scratch_shapes=[pltpu.VMEM((L + 8, 3 * W_G), F32), pltpu.VMEM((G_HEADS, G_HD, G_HD), F32)],
        compiler_params=_cparams(("parallel", "arbitrary")),
        name="gdn",
    )(u3, u3, smallt, g_conv, prow, pcol, buf8, S0)


def _head_rmsnorm(x, gain_ref, nheads, hd):
    outs = []
    for h in range(nheads):
        xh = x[:, h * hd:(h + 1) * hd]
        ms = jnp.mean(xh * xh, axis=-1, keepdims=True)
        outs.append(xh * lax.rsqrt(ms + EPS) * gain_ref[...])
    return jnp.concatenate(outs, axis=1)


def _merge_kernel(x_ref, oa_ref, hm_ref, og_ref, az_ref, mo_ref, mz_ref, gz_ref, mg_ref,
                  wb_ref, wo_ref, mhn_ref, ghn_ref, y_ref):
    oa = oa_ref[...] * _silu(az_ref[...])
    om = _head_rmsnorm(hm_ref[...], mhn_ref, M_HEADS, M_HD) * jax.nn.sigmoid(mo_ref[...]) * _silu(mz_ref[...])
    og = _head_rmsnorm(og_ref[...], ghn_ref, G_HEADS, G_HD) * _silu(gz_ref[...])
    y = jnp.zeros(y_ref.shape, F32)
    for i, br in enumerate((oa, om, og)):
        proj = jnp.dot(br.astype(BF16), wb_ref[i], preferred_element_type=F32)
        y = y + jax.nn.sigmoid(mg_ref[:, i * D_MODEL:(i + 1) * D_MODEL]) * proj
    y_ref[...] = x_ref[...] + jnp.dot(y.astype(BF16), wo_ref[...], preferred_element_type=F32)


def _merge_out(x2, o_a, h_m, o_g, u2, w_branch, w_out, m_hn, g_hn):
    n = x2.shape[0]
    tm = min(512, n)
    row = lambda w, off: pl.BlockSpec((tm, w), lambda i: (i, off // w))
    cst = lambda shape: pl.BlockSpec(shape, lambda i: (0,) * len(shape))
    return pl.pallas_call(
        _merge_kernel,
        grid=(n // tm,),
        in_specs=[row(D_MODEL, 0), row(W_A, 0), row(W_M, 0), row(W_G, 0),
                  row(W_A, OFF['a_z']), row(W_M, OFF['m_o']), row(W_M, OFF['m_z']), row(W_G, OFF['g_z']),
                  row(N_BRANCH * D_MODEL, OFF['merge']),
                  cst((N_BRANCH, W_A, D_MODEL)), cst((D_MODEL, D_MODEL)), cst((1, M_HD)), cst((1, G_HD))],
        out_specs=row(D_MODEL, 0),
        out_shape=jax.ShapeDtypeStruct((n, D_MODEL), F32),
        compiler_params=_cparams(("parallel",)),
        name="merge_out",
    )(x2, o_a, h_m, o_g, u2, u2, u2, u2, u2, w_branch.astype(BF16), w_out.astype(BF16),
      m_hn.reshape(1, M_HD), g_hn.reshape(1, G_HD))


def _small_t(u3, L):
    B, T, _ = u3.shape
    s = u3[:, :, OFF['small'] + SM_MIF:OFF['small'] + SM_MIF + 16]
    return s.reshape(B, T // L, L, 16).transpose(0, 1, 3, 2)


def _layer(x, lw, rel_bias, past, q_off):
    B, T, _ = x.shape
    x2 = x.reshape(B * T, D_MODEL)
    u2 = _proj_in(x2, lw['norm_g'], lw['w_perm'])
    if past is None:
        tp, L = T, math.gcd(T, M_CHUNK)
        u3 = u2.reshape(B, T, N_PERM)
        C0 = jnp.zeros((B, M_HEADS, M_HD, M_HD), F32)
        n0 = jnp.zeros((B, M_HEADS, M_HD), F32)
        m0 = jnp.zeros((B, M_HEADS), F32)
        S0 = jnp.zeros((B, G_HEADS, G_HD, G_HD), F32)
        buf = jnp.zeros((B, CONV_W - 1, 3 * W_G), F32)
    else:
        tp = -(-T // 8) * 8
        L = tp
        u3 = jnp.pad(u2.reshape(B, T, N_PERM), ((0, 0), (0, tp - T), (0, 0)))
        C0, n0, m0, S0, buf = past['mC'], past['mn'], past['mm'], past['gS'], past['gconv']
    up = u3.reshape(B * tp, N_PERM)
    qn, rows6 = _prep_a(up, lw['a_qn'], lw['a_kn'])
    qn3 = qn.reshape(B, tp, W_A)
    if past is None:
        o_a = _nsa_prompt(qn3, rows6, u3, lw['wk'], lw['wv'], rel_bias)
    else:
        o_a = _nsa_sample(qn3, rows6, u3, past['cmp'], past['sel'], past['win'], past['page_table'],
                          lw['wk'], lw['wv'], rel_bias, q_off, T)
    smallt = _small_t(u3, L)
    h_m, Cn, nn, mn = _mlstm(u3, smallt, lw['m_bi'], lw['m_bf'], C0, n0, m0, L, T)
    o_g, Sn = _gdn(u3, smallt, lw['g_conv'], lw['g_A_log'], lw['g_dt_bias'], buf, S0, L, T)
    if tp != T:
        o_a, h_m, o_g = o_a[:, :T], h_m[:, :T], o_g[:, :T]
    y2 = _merge_out(x2, o_a.reshape(B * T, W_A), h_m.reshape(B * T, W_M), o_g.reshape(B * T, W_G), u2,
                    lw['w_branch'], lw['w_out'], lw['m_hn'], lw['g_hn'])
    rows = rows6.reshape(3, 2, B, tp, A_KV, A_HD)[:, :, :, :T]
    new_cmp, new_sel, new_win = (jnp.moveaxis(rows[b], 0, 2) for b in range(3))
    g_qkv = u3[:, :T, OFF['g_qkv']:OFF['g_qkv'] + 3 * W_G]
    if past is None:
        win = new_win
    else:
        win = jnp.concatenate([past['win'].reshape(B, -1, 2, A_KV, A_HD), new_win], axis=1)
    full = jnp.concatenate([buf, g_qkv[:, max(T - (CONV_W - 1), 0):]], axis=1)
    state = dict(cmp=new_cmp, sel=new_sel, win=win[:, -min(WINDOW, win.shape[1]):],
                 mC=Cn, mn=nn, mm=mn, gS=Sn, gconv=full[:, -(CONV_W - 1):])
    return y2.reshape(B, T, D_MODEL), state


def kernel(x_prompt, x_sample, cache_cmp_kv, cache_sel_kv, cache_win_kv, state_mlstm_C, state_mlstm_n,
           state_mlstm_m, state_gdn_S, state_gdn_conv, page_table, norm_g, w_in, a_qn, a_kn, a_cmp_wk,
           a_cmp_wv, rel_bias, m_bi, m_bf, m_hn, g_conv, g_A_log, g_dt_bias, g_hn, w_branch, w_out):
    names = ('cmp', 'sel', 'win', 'mC', 'mn', 'mm', 'gS', 'gconv')
    st_p = {k: [] for k in names}
    st_s = {k: [] for k in names}
    past_len = page_table.shape[1] * PAGE_SIZE
    n_pool = cache_cmp_kv.shape[1]
    db = x_sample.shape[0]
    y_p, y_s = x_prompt, x_sample
    for l in range(DEPTH):
        lw = dict(norm_g=norm_g[l], w_perm=_permute_w_in(w_in[l]), a_qn=a_qn[l], a_kn=a_kn[l],
                  wk=_cmp_weights(a_cmp_wk[l]), wv=_cmp_weights(a_cmp_wv[l]),
                  m_bi=m_bi[l], m_bf=m_bf[l], m_hn=m_hn[l], g_conv=g_conv[l], g_A_log=g_A_log[l],
                  g_dt_bias=g_dt_bias[l], g_hn=g_hn[l], w_branch=w_branch[l], w_out=w_out[l])
        y_p, new_p = _layer(y_p, lw, rel_bias, None, 0)
        past = dict(cmp=cache_cmp_kv[l].reshape(n_pool, PAGE_SIZE, 256),
                    sel=cache_sel_kv[l].reshape(n_pool, PAGE_SIZE, 256),
                    win=cache_win_kv[l].reshape(db, -1, 256), page_table=page_table,
                    mC=state_mlstm_C[l], mn=state_mlstm_n[l], mm=state_mlstm_m[l],
                    gS=state_gdn_S[l], gconv=state_gdn_conv[l])
        y_s, new_s = _layer(y_s, lw, rel_bias, past, past_len)
        for k in names:
            st_p[k].append(new_p[k])
            st_s[k].append(new_s[k])
    P = {k: jnp.stack(v) for k, v in st_p.items()}
    S = {k: jnp.stack(v) for k, v in st_s.items()}
    return (y_p, y_s, P['cmp'], S['cmp'], P['sel'], S['sel'], P['win'], S['win'],
            P['mC'], S['mC'], P['mn'], S['mn'], P['mm'], S['mm'], P['gS'], S['gS'], P['gconv'], S['gconv'])
```

```python
import functools
import math

import jax
import jax.numpy as jnp
from jax import lax
from jax.experimental import pallas as pl
from jax.experimental.pallas import tpu as pltpu

D_MODEL = 1024
DEPTH = 2
PAGE_SIZE = 128
A_HEADS = 8
A_KV = 2
A_REP = A_HEADS // A_KV
A_HD = 64
CMP_BLOCK = 32
CMP_STRIDE = 16
SEL_BLOCK = 64
TOP_N = 16
WINDOW = 512
Q_BLOCK = 128
N_BUCKETS = 32
MAX_DIST = 2048
M_HEADS = 4
M_HD = 128
M_CHUNK = 64
G_HEADS = 4
G_HD = 128
G_CHUNK = 64
CONV_W = 4
W_A = A_HEADS * A_HD
W_M = M_HEADS * M_HD
W_G = G_HEADS * G_HD
N_BRANCH = 3
EPS = 1e-6
NEG = -1e30
TINY = 1e-30

F32 = jnp.float32
BF16 = jnp.bfloat16
HI = lax.Precision.HIGHEST
NN = (((1,), (0,)), ((), ()))
NT = (((1,), (1,)), ((), ()))
TN = (((0,), (0,)), ((), ()))

LANES = 128
VMEM_LIMIT = 56 * 1024 * 1024

IN_ORDER = ('a_q', 'a_kv', 'a_gate', 'a_z', 'm_qkv', 'm_if', 'm_o', 'm_z', 'g_qkv', 'g_ab', 'g_z', 'merge')
IN_WIDTH = dict(a_q=W_A, a_kv=3 * 2 * A_KV * A_HD, a_gate=A_HEADS * 3, a_z=W_A, m_qkv=3 * W_M, m_if=2 * M_HEADS,
                m_o=W_M, m_z=W_M, g_qkv=3 * W_G, g_ab=2 * G_HEADS, g_z=W_G, merge=N_BRANCH * D_MODEL)
OFF = dict(merge=0, m_qkv=3072, g_qkv=4608, a_q=6144, a_z=6656, m_o=7168, m_z=7680, g_z=8192, a_kv=8704,
           small=9472)
N_PERM = 9600
SM_GATE, SM_MIF, SM_GAB = 0, 24, 32


def _cparams(sem):
    return pltpu.CompilerParams(dimension_semantics=sem, vmem_limit_bytes=VMEM_LIMIT)


def _silu(x):
    return x * jax.nn.sigmoid(x)


def _log_sigmoid(x):
    return jnp.minimum(x, 0.0) - jnp.log(1.0 + jnp.exp(-jnp.abs(x)))


def _softplus(x):
    return jnp.maximum(x, 0.0) + jnp.log(1.0 + jnp.exp(-jnp.abs(x)))


def _split(a):
    hi = a.astype(BF16)
    return hi, (a - hi.astype(F32)).astype(BF16)


def _dot3(a, b, dims=NN):
    mm = lambda x, y: lax.dot_general(x, y, dims, preferred_element_type=F32)
    return mm(a[0], b[0]) + mm(a[0], b[1]) + mm(a[1], b[0])


def _permute_w_in(w_in):
    parts, off = {}, 0
    for name in IN_ORDER:
        parts[name] = w_in[:, off:off + IN_WIDTH[name]]
        off += IN_WIDTH[name]
    small = jnp.concatenate([parts['a_gate'], parts['m_if'], parts['g_ab'],
                             jnp.zeros((w_in.shape[0], LANES - 40), w_in.dtype)], axis=1)
    cols = [parts['merge'], parts['m_qkv'], parts['g_qkv'], parts['a_q'], parts['a_z'], parts['m_o'],
            parts['m_z'], parts['g_z'], parts['a_kv'], small]
    return jnp.concatenate(cols, axis=1).astype(BF16)


def _rel_bucket(dist):
    n = jnp.maximum(dist, 0)
    exact = N_BUCKETS // 2
    nf = jnp.maximum(n, exact).astype(F32)
    large = exact + (jnp.log(nf / exact) / math.log(MAX_DIST / exact) * (N_BUCKETS - exact)).astype(jnp.int32)
    return jnp.where(n < exact, n, jnp.minimum(large, N_BUCKETS - 1))


def _bias_rows(rel_bias, dist):
    b = rel_bias[_rel_bucket(dist)].astype(F32)
    lead = dist.shape[:-2]
    Q, K = dist.shape[-2:]
    b = b.reshape(lead + (Q, K, A_KV, A_REP))
    nl = len(lead)
    b = jnp.transpose(b, tuple(range(nl)) + (nl + 2, nl + 3, nl, nl + 1))
    return b.reshape(lead + (A_KV, A_REP * Q, K))


def _cmp_weights(w):
    wr = w.reshape(A_KV, 2, CMP_STRIDE, A_HD, A_HD)
    eye = jnp.eye(A_KV, dtype=w.dtype)
    full = jnp.einsum('gmjde,gh->jgdmhe', wr, eye)
    return full.reshape(CMP_STRIDE, A_KV * A_HD, 2 * A_KV * A_HD).astype(BF16)


def _cover_t(ns_pad, nch):
    s0 = jnp.arange(ns_pad)[:, None] * SEL_BLOCK
    c0 = jnp.arange(nch)[None, :] * CMP_STRIDE
    return ((c0 < s0 + SEL_BLOCK) & (s0 <= c0 + CMP_BLOCK - 1)).astype(F32)


def _expand_mat(ns_pad, nk):
    return (jnp.arange(nk)[None, :] // SEL_BLOCK == jnp.arange(ns_pad)[:, None]).astype(BF16)


def _gate_expand():
    col = jnp.arange(N_BRANCH * W_A)
    src = 3 * ((col % W_A) // A_HD) + col // W_A
    return (jnp.arange(LANES)[:, None] == src[None, :]).astype(BF16)


def _prompt_tables(rel_bias, T):
    QB = Q_BLOCK
    nqb = T // QB
    nch = T // CMP_STRIDE
    ns = T // SEL_BLOCK
    ns_pad = -(-ns // 8) * 8
    i_ = jnp.arange(QB)
    t = (jnp.arange(nqb) * QB)[:, None, None] + i_[None, :, None]
    cend = (jnp.arange(nch) * CMP_STRIDE + CMP_BLOCK - 1)[None, None, :]
    d = (jnp.arange(nqb) * QB)[:, None, None] + i_[None, :, None] - i_[None, None, :]
    nw = WINDOW // QB + 1
    return dict(
        bcmp=_bias_rows(rel_bias, t - cend),
        tsel=_bias_rows(rel_bias, d),
        covt=_cover_t(ns_pad, nch),
        e3=_expand_mat(LANES, T).reshape(LANES, nqb, QB).transpose(1, 0, 2),
        wm=((d[:nw] >= 0) & (d[:nw] < WINDOW)).astype(F32),
        cm=(d[:2] >= 0).astype(F32),
        gx=_gate_expand())


def _sample_tables(rel_bias, past, tp, t_real):
    SL = past + LANES
    nblk = past // CMP_STRIDE
    ns = -(-(past + t_real) // SEL_BLOCK)
    ns_pad = -(-ns // LANES) * LANES
    WK = WINDOW + LANES
    i_ = jnp.arange(tp)
    t = past + i_
    return dict(
        bcmp=_bias_rows(rel_bias, t[:, None] - (jnp.arange(nblk) * CMP_STRIDE + CMP_BLOCK - 1)[None, :]),
        bsel=_bias_rows(rel_bias, t[:, None] - jnp.arange(SL)[None, :]),
        bwin=_bias_rows(rel_bias, i_[:, None] + WINDOW - jnp.arange(WK)[None, :]),
        covt=_cover_t(ns_pad, nblk), emat=_expand_mat(ns_pad, SL))


def _proj_in_kernel(x_ref, g_ref, w_ref, o_ref, hn_ref):
    @pl.when(pl.program_id(1) == 0)
    def _():
        x = x_ref[...]
        ms = jnp.mean(x * x, axis=-1, keepdims=True)
        hn_ref[...] = (x * lax.rsqrt(ms + EPS) * g_ref[...]).astype(BF16)

    o_ref[...] = jnp.dot(hn_ref[...], w_ref[...], preferred_element_type=F32)


def _proj_in(x2, norm_g, w_perm):
    n = x2.shape[0]
    tm = min(1024, n)
    tn = 1920
    return pl.pallas_call(
        _proj_in_kernel,
        grid=(n // tm, N_PERM // tn),
        in_specs=[pl.BlockSpec((tm, D_MODEL), lambda i, j: (i, 0)),
                  pl.BlockSpec((1, D_MODEL), lambda i, j: (0, 0)),
                  pl.BlockSpec((D_MODEL, tn), lambda i, j: (0, j))],
        out_specs=pl.BlockSpec((tm, tn), lambda i, j: (i, j)),
        out_shape=jax.ShapeDtypeStruct((n, N_PERM), F32),
        scratch_shapes=[pltpu.VMEM((tm, D_MODEL), BF16)],
        compiler_params=_cparams(("parallel", "arbitrary")),
        name="proj_in",
    )(x2, norm_g.reshape(1, D_MODEL), w_perm)


def _prep_a_kernel(q_ref, kv0_ref, kv1_ref, kv2_ref, bdq_ref, bdk_ref, qg_ref, kg_ref, qo_ref, ro_ref):
    q = q_ref[...]
    ms = jnp.dot(q * q, bdq_ref[...], precision=HI, preferred_element_type=F32) * (1.0 / A_HD)
    qo_ref[...] = q * lax.rsqrt(ms + EPS) * qg_ref[...] * (A_HD ** -0.5)
    for b, kv_ref in enumerate((kv0_ref, kv1_ref, kv2_ref)):
        k = kv_ref[:, 0:128]
        ms = jnp.dot(k * k, bdk_ref[...], precision=HI, preferred_element_type=F32) * (1.0 / A_HD)
        ro_ref[2 * b] = k * lax.rsqrt(ms + EPS) * kg_ref[b:b + 1, :]
        ro_ref[2 * b + 1] = kv_ref[:, 128:256]


def _prep_a(u2, a_qn, a_kn):
    n = u2.shape[0]
    tm = min(512, n)
    bd = lambda w: (jnp.arange(w)[:, None] // A_HD == jnp.arange(w)[None, :] // A_HD).astype(F32)
    qg = jnp.tile(a_qn, A_HEADS).reshape(1, W_A)
    kg = jnp.tile(a_kn, (1, A_KV))
    const = lambda shape: pl.BlockSpec(shape, lambda i: (0, 0))
    kvs = lambda b: pl.BlockSpec((tm, 256), lambda i: (i, OFF['a_kv'] // 256 + b))
    return pl.pallas_call(
        _prep_a_kernel,
        grid=(n // tm,),
        in_specs=[pl.BlockSpec((tm, W_A), lambda i: (i, OFF['a_q'] // W_A)), kvs(0), kvs(1), kvs(2),
                  const((W_A, W_A)), const((128, 128)), const((1, W_A)), const((3, 128))],
        out_specs=[pl.BlockSpec((tm, W_A), lambda i: (i, 0)), pl.BlockSpec((6, tm, 128), lambda i: (0, i, 0))],
        out_shape=[jax.ShapeDtypeStruct((n, W_A), F32), jax.ShapeDtypeStruct((6, n, 128), F32)],
        compiler_params=_cparams(("parallel",)),
        name="prep_a",
    )(u2, u2, u2, u2, bd(W_A), bd(128), qg, kg)


def _stack_heads(q, g):
    return jnp.concatenate([q[:, (A_REP * g + r) * A_HD:(A_REP * g + r + 1) * A_HD] for r in range(A_REP)], axis=0)


def _rep_rows(x):
    return jnp.concatenate([x] * A_REP, axis=0)


def _masked_softmax(s, valid):
    sm = jnp.where(valid, s, NEG)
    m = jnp.max(sm, axis=-1, keepdims=True)
    e = jnp.where(valid, jnp.exp(sm - m), 0.0)
    l = jnp.sum(e, axis=-1, keepdims=True)
    return e / jnp.maximum(l, TINY)


def _select_blocks(pc_sum, covt_ref, st_ref, t_row, ns, n_top):
    ns_pad = covt_ref.shape[0]
    imp = lax.dot_general(covt_ref[...], pc_sum, NT, precision=HI, preferred_element_type=F32)
    jj = lax.broadcasted_iota(jnp.int32, (ns_pad, LANES), 0)
    tt = jnp.broadcast_to(t_row, (ns_pad, LANES))
    cur = tt // SEL_BLOCK
    forced = (jj == 0) | (jj == cur) | (jj == cur - 1)
    future = jj * SEL_BLOCK > tt
    score = jnp.where(future, NEG, jnp.where(forced, -NEG, imp))
    score = jnp.where(jj < ns, score, -jnp.inf)
    st_ref[...] = score

    def beats(k, rank):
        row = jnp.broadcast_to(st_ref[pl.ds(k, 1), :], (ns_pad, LANES))
        b = (row > score) | ((row == score) & (k < jj))
        return rank + jnp.where(b, 1.0, 0.0)

    rank = jnp.zeros((ns_pad, LANES), F32)
    if ns <= 32:
        for k in range(ns):
            rank = beats(k, rank)
    else:
        rank = lax.fori_loop(0, ns, beats, rank)
    return jnp.where(rank < n_top, 1.0, 0.0)


def _pad_rows(x, rows):
    if x.shape[0] == rows:
        return x
    return jnp.concatenate([x, jnp.zeros((rows - x.shape[0], x.shape[1]), x.dtype)], axis=0)


def _compress(src_ref, nrow, wk_ref, wv_ref, pbuf_ref, kc_ref, vc_ref):
    acck = jnp.zeros((nrow, 256), F32)
    accv = jnp.zeros((nrow, 256), F32)
    for j in range(CMP_STRIDE):
        xk = src_ref[0, pl.ds(j, nrow, stride=CMP_STRIDE), :].astype(BF16)
        xv = src_ref[1, pl.ds(j, nrow, stride=CMP_STRIDE), :].astype(BF16)
        acck = acck + jnp.dot(xk, wk_ref[j], preferred_element_type=F32)
        accv = accv + jnp.dot(xv, wv_ref[j], preferred_element_type=F32)
    nout = kc_ref.shape[0]
    for acc, dst in ((acck, kc_ref), (accv, vc_ref)):
        pbuf_ref[0:nrow, :] = acc[:, 128:256]
        dst[...] = acc[0:nout, 0:128] + pbuf_ref[pl.ds(1, nout), :]


def _gated_sum(gates, gx_ref, ocat_ref):
    gh, gl = _split(gates)
    gexp = (jnp.dot(gh, gx_ref[...], preferred_element_type=F32)
            + jnp.dot(gl, gx_ref[...], preferred_element_type=F32))
    out = gexp[:, 0:W_A] * ocat_ref[0]
    for c in range(1, N_BRANCH):
        out = out + gexp[:, c * W_A:(c + 1) * W_A] * ocat_ref[c]
    return out


def _nsa_prompt_kernel(q_ref, rows_ref, small_ref, wk_ref, wv_ref, bcmp_ref, tsel_ref, covt_ref, e3_ref,
                       wm_ref, cm_ref, gx_ref,
                       o_ref, kc_ref, vc_ref, pbuf_ref, sel_ref, acc_ref, m_ref, st_ref, ocat_ref,
                       *, ns, n_top):
    bi = pl.program_id(1)
    T = rows_ref.shape[1]
    nch = T // CMP_STRIDE
    QB = Q_BLOCK
    SEL, WIN = 1, 2

    @pl.when(bi == 0)
    def _():
        pbuf_ref[...] = jnp.zeros(pbuf_ref.shape, F32)
        _compress(rows_ref, nch, wk_ref, wv_ref, pbuf_ref, kc_ref, vc_ref)

    t0 = bi * QB
    q = q_ref[0]
    tc = lax.broadcasted_iota(jnp.int32, (QB, nch), 0) + t0
    cend = lax.broadcasted_iota(jnp.int32, (QB, nch), 1) * CMP_STRIDE + (CMP_BLOCK - 1)
    cvalid = _rep_rows(tc - cend >= 0)
    t_row = lax.broadcasted_iota(jnp.int32, (1, LANES), 1) + t0
    lane = lax.broadcasted_iota(jnp.int32, (QB, LANES), 1)
    in_g = [(lane >= g * A_HD) & (lane < (g + 1) * A_HD) for g in range(A_KV)]
    qg = [_stack_heads(q, g).astype(BF16) for g in range(A_KV)]

    def put_heads(c, g, o):
        for r in range(A_REP):
            h = A_REP * g + r
            ocat_ref[c, :, h * A_HD:(h + 1) * A_HD] = o[r * QB:(r + 1) * QB]

    for g in range(A_KV):
        kc = kc_ref[:, g * A_HD:(g + 1) * A_HD].astype(BF16)
        vc = vc_ref[:, g * A_HD:(g + 1) * A_HD].astype(BF16)
        s = lax.dot_general(qg[g], kc, NT, preferred_element_type=F32) + bcmp_ref[0, g]
        p_c = _masked_softmax(s, cvalid)
        put_heads(0, g, jnp.dot(p_c.astype(BF16), vc, preferred_element_type=F32))
        pc_sum = p_c[0:QB] + p_c[QB:2 * QB] + p_c[2 * QB:3 * QB] + p_c[3 * QB:4 * QB]
        sel_t = _select_blocks(pc_sum, covt_ref, st_ref, t_row, ns, n_top)
        sel_ref[g] = _pad_rows(sel_t, LANES).T.astype(BF16)

    m_ref[...] = jnp.full(m_ref.shape, NEG, F32)
    acc_ref[...] = jnp.zeros(acc_ref.shape, F32)

    def tile(br, g, kb):
        ci = 2 * (br - 1) + g
        off = pl.multiple_of(kb * QB, QB)
        k = rows_ref[2 * br, pl.ds(off, QB), g * A_HD:(g + 1) * A_HD].astype(BF16)
        v1 = jnp.where(in_g[g], rows_ref[2 * br + 1, pl.ds(off, QB), :], 1.0).astype(BF16)
        sc = lax.dot_general(qg[g], k, NT, preferred_element_type=F32) + tsel_ref[bi - kb, g]
        if br == SEL:
            mt = jnp.dot(sel_ref[g], e3_ref[kb], preferred_element_type=F32) * cm_ref[jnp.minimum(bi - kb, 1)]
        else:
            mt = wm_ref[bi - kb]
        valid = _rep_rows(mt) > 0.5
        sm = jnp.where(valid, sc, NEG)
        m_prev = m_ref[ci]
        m_new = jnp.maximum(m_prev, jnp.max(sm, axis=-1, keepdims=True))
        p = jnp.where(valid, jnp.exp(sm - m_new), 0.0)
        acc_ref[ci] = jnp.exp(m_prev - m_new) * acc_ref[ci] + jnp.dot(p.astype(BF16), v1,
                                                                       preferred_element_type=F32)
        m_ref[ci] = m_new

    def body_sel(kb, carry):
        for g in range(A_KV):
            tile(SEL, g, kb)
        return carry

    def body_both(kb, carry):
        for g in range(A_KV):
            tile(SEL, g, kb)
            tile(WIN, g, kb)
        return carry

    lo = jnp.maximum(bi - WINDOW // QB, 0)
    lax.fori_loop(0, lo, body_sel, 0)
    lax.fori_loop(lo, bi + 1, body_both, 0)

    for br in (SEL, WIN):
        for g in range(A_KV):
            acc = acc_ref[2 * (br - 1) + g]
            linv = pltpu.roll(1.0 / jnp.maximum(acc, TINY), A_HD, axis=1)
            put_heads(br, g, (acc * linv)[:, g * A_HD:(g + 1) * A_HD])

    o_ref[0] = _gated_sum(jax.nn.sigmoid(small_ref[0]), gx_ref, ocat_ref)


def _nsa_prompt(qn3, rows6, u3, wk, wv, tb):
    B, T, _ = qn3.shape
    QB = Q_BLOCK
    nqb = T // QB
    nch = T // CMP_STRIDE
    ns = T // SEL_BLOCK
    ns_pad = tb['covt'].shape[0]
    n_top = min(TOP_N, ns)
    nw = tb['wm'].shape[0]
    kern = functools.partial(_nsa_prompt_kernel, ns=ns, n_top=n_top)
    c2 = lambda shape: pl.BlockSpec(shape, lambda b, i: (0,) * len(shape))
    return pl.pallas_call(
        kern,
        grid=(B, nqb),
        in_specs=[pl.BlockSpec((1, QB, W_A), lambda b, i: (b, i, 0)),
                  pl.BlockSpec((6, T, 128), lambda b, i: (0, b, 0)),
                  pl.BlockSpec((1, QB, LANES), lambda b, i: (b, i, OFF['small'] // LANES)),
                  c2((CMP_STRIDE, 128, 256)), c2((CMP_STRIDE, 128, 256)),
                  pl.BlockSpec((1, A_KV, A_REP * QB, nch), lambda b, i: (i, 0, 0, 0)),
                  c2((nqb, A_KV, A_REP * QB, QB)),
                  c2((ns_pad, nch)), c2((nqb, LANES, QB)), c2((nw, QB, QB)), c2((2, QB, QB)),
                  c2((LANES, N_BRANCH * W_A))],
        out_specs=pl.BlockSpec((1, QB, W_A), lambda b, i: (b, i, 0)),
        out_shape=jax.ShapeDtypeStruct((B, T, W_A), F32),
        scratch_shapes=[pltpu.VMEM((nch, 128), F32), pltpu.VMEM((nch, 128), F32),
                        pltpu.VMEM((nch + 8, 128), F32),
                        pltpu.VMEM((A_KV, QB, LANES), BF16),
                        pltpu.VMEM((2 * A_KV, A_REP * QB, LANES), F32),
                        pltpu.VMEM((2 * A_KV, A_REP * QB, LANES), F32),
                        pltpu.VMEM((ns_pad, LANES), F32),
                        pltpu.VMEM((N_BRANCH, QB, W_A), F32)],
        compiler_params=_cparams(("parallel", "arbitrary")),
        name="nsa_prompt",
    )(qn3, rows6, u3, wk, wv, tb['bcmp'], tb['tsel'], tb['covt'], tb['e3'], tb['wm'], tb['cm'], tb['gx'])


def _nsa_sample_kernel(pt_ref, *refs, pg, past, tp, ns, n_top):
    cmp_refs = refs[0:pg]
    sel_refs = refs[pg:2 * pg]
    (q_ref, rows_ref, small_ref, win_ref, wk_ref, wv_ref, bcmp_ref, bsel_ref, bwin_ref, covt_ref, e_ref,
     o_ref, cslab, sslab, wslab, kc_ref, vc_ref, pbuf_ref, st_ref, s_ref) = refs[2 * pg:]
    p = pl.program_id(1)
    npg = pl.num_programs(1)
    SL = cslab.shape[1]
    nblk = kc_ref.shape[0]
    WK = wslab.shape[1]

    for k in range(pg):
        off = pl.multiple_of((p * pg + k) * PAGE_SIZE, PAGE_SIZE)
        for kv in range(2):
            cslab[kv, pl.ds(off, PAGE_SIZE), :] = cmp_refs[k][:, kv * 128:(kv + 1) * 128]
            sslab[kv, pl.ds(off, PAGE_SIZE), :] = sel_refs[k][:, kv * 128:(kv + 1) * 128]

    @pl.when(p == npg - 1)
    def _():
        tail = jnp.zeros((SL - past - tp, 128), F32)
        for kv in range(2):
            cslab[kv, past:past + tp, :] = rows_ref[kv]
            cslab[kv, past + tp:SL, :] = tail
            sslab[kv, past:past + tp, :] = rows_ref[2 + kv]
            sslab[kv, past + tp:SL, :] = tail
            wslab[kv, 0:WINDOW, :] = win_ref[0, :, kv * 128:(kv + 1) * 128]
            wslab[kv, WINDOW:WINDOW + tp, :] = rows_ref[4 + kv]
            wslab[kv, WINDOW + tp:WK, :] = jnp.zeros((WK - WINDOW - tp, 128), F32)
        pbuf_ref[...] = jnp.zeros(pbuf_ref.shape, F32)
        _compress(cslab, nblk + 8, wk_ref, wv_ref, pbuf_ref, kc_ref, vc_ref)

        q = q_ref[0]
        gates = jax.nn.sigmoid(small_ref[0])
        R = A_REP * tp
        ti = lax.broadcasted_iota(jnp.int32, (tp, SL), 0) + past
        causal = ti - lax.broadcasted_iota(jnp.int32, (tp, SL), 1) >= 0
        tc = lax.broadcasted_iota(jnp.int32, (tp, nblk), 0) + past
        cend = lax.broadcasted_iota(jnp.int32, (tp, nblk), 1) * CMP_STRIDE + (CMP_BLOCK - 1)
        cvalid = _rep_rows(tc - cend >= 0)
        wd = (lax.broadcasted_iota(jnp.int32, (tp, WK), 0) + WINDOW
              - lax.broadcasted_iota(jnp.int32, (tp, WK), 1))
        wvalid = _rep_rows((wd >= 0) & (wd < WINDOW))
        t_row = lax.broadcasted_iota(jnp.int32, (1, LANES), 1) + past
        nck = 5 if (SL // 5) % LANES == 0 else SL // LANES
        ck = SL // nck

        for g in range(A_KV):
            gs = slice(g * A_HD, (g + 1) * A_HD)
            qg = _stack_heads(q, g).astype(BF16)
            s = lax.dot_general(qg, kc_ref[:, gs].astype(BF16), NT, preferred_element_type=F32) + bcmp_ref[g]
            p_c = _masked_softmax(s, cvalid)
            o_c = jnp.dot(p_c.astype(BF16), vc_ref[:, gs].astype(BF16), preferred_element_type=F32)
            pc_sum = p_c[0:tp] + p_c[tp:2 * tp] + p_c[2 * tp:3 * tp] + p_c[3 * tp:4 * tp]

            sel_t = _select_blocks(_pad_rows(pc_sum, LANES), covt_ref, st_ref, t_row, ns, n_top)
            sel = sel_t.T[0:tp].astype(BF16)
            kmask = jnp.dot(sel, e_ref[...], preferred_element_type=F32)
            svalid = _rep_rows(causal & (kmask > 0.5))

            for c in range(nck):
                kk = sslab[0, c * ck:(c + 1) * ck, gs].astype(BF16)
                s_ref[:, c * ck:(c + 1) * ck] = lax.dot_general(qg, kk, NT, preferred_element_type=F32)
            p_s = _masked_softmax(s_ref[...] + bsel_ref[g], svalid)
            s_ref[...] = p_s
            o_s = jnp.zeros((R, A_HD), F32)
            for c in range(nck):
                vv = sslab[1, c * ck:(c + 1) * ck, gs].astype(BF16)
                o_s = o_s + jnp.dot(s_ref[:, c * ck:(c + 1) * ck].astype(BF16), vv, preferred_element_type=F32)

            kw = wslab[0, :, gs].astype(BF16)
            vw = wslab[1, :, gs].astype(BF16)
            sw = lax.dot_general(qg, kw, NT, preferred_element_type=F32) + bwin_ref[g]
            p_w = _masked_softmax(sw, wvalid)
            o_w = jnp.dot(p_w.astype(BF16), vw, preferred_element_type=F32)

            for r in range(A_REP):
                h = A_REP * g + r
                rs = slice(r * tp, (r + 1) * tp)
                out = (gates[:, 3 * h:3 * h + 1] * o_c[rs] + gates[:, 3 * h + 1:3 * h + 2] * o_s[rs]
                       + gates[:, 3 * h + 2:3 * h + 3] * o_w[rs])
                o_ref[0, :, h * A_HD:(h + 1) * A_HD] = out


def _nsa_sample(qn3, rows6, u3, pool_cmp, pool_sel, win3, page_table, wk, wv, tb, past, t_real):
    B, tp, _ = qn3.shape
    n_pages = past // PAGE_SIZE
    pg = 8 if n_pages % 8 == 0 else n_pages
    npg = n_pages // pg
    SL = past + LANES
    nblk = past // CMP_STRIDE
    ns = -(-(past + t_real) // SEL_BLOCK)
    ns_pad = tb['covt'].shape[0]
    n_top = min(TOP_N, ns)
    WK = WINDOW + LANES
    kern = functools.partial(_nsa_sample_kernel, pg=pg, past=past, tp=tp, ns=ns, n_top=n_top)

    def page_spec(k):
        return pl.BlockSpec((None, PAGE_SIZE, 256), lambda b, p, pt: (pt[b, p * pg + k], 0, 0))

    def c_(shape):
        return pl.BlockSpec(shape, lambda b, p, pt: (0,) * len(shape))

    R = A_REP * tp
    grid_spec = pltpu.PrefetchScalarGridSpec(
        num_scalar_prefetch=1,
        grid=(B, npg),
        in_specs=([page_spec(k) for k in range(pg)] + [page_spec(k) for k in range(pg)]
                  + [pl.BlockSpec((1, tp, W_A), lambda b, p, pt: (b, 0, 0)),
                     pl.BlockSpec((6, tp, 128), lambda b, p, pt: (0, b, 0)),
                     pl.BlockSpec((1, tp, LANES), lambda b, p, pt: (b, 0, OFF['small'] // LANES)),
                     pl.BlockSpec((1, WINDOW, 256), lambda b, p, pt: (b, 0, 0)),
                     c_((CMP_STRIDE, 128, 256)), c_((CMP_STRIDE, 128, 256)),
                     c_((A_KV, R, nblk)), c_((A_KV, R, SL)), c_((A_KV, R, WK)),
                     c_((ns_pad, nblk)), c_((ns_pad, SL))]),
        out_specs=pl.BlockSpec((1, tp, W_A), lambda b, p, pt: (b, 0, 0)),
        scratch_shapes=[pltpu.VMEM((2, SL, 128), F32), pltpu.VMEM((2, SL, 128), F32),
                        pltpu.VMEM((2, WK, 128), F32),
                        pltpu.VMEM((nblk, 128), F32), pltpu.VMEM((nblk, 128), F32),
                        pltpu.VMEM((nblk + 16, 128), F32),
                        pltpu.VMEM((ns_pad, LANES), F32), pltpu.VMEM((R, SL), F32)],
    )
    return pl.pallas_call(
        kern,
        grid_spec=grid_spec,
        out_shape=jax.ShapeDtypeStruct((B, tp, W_A), F32),
        compiler_params=_cparams(("parallel", "arbitrary")),
        name="nsa_sample",
    )(page_table, *([pool_cmp] * pg), *([pool_sel] * pg), qn3, rows6, u3, win3, wk, wv,
      tb['bcmp'], tb['bsel'], tb['bwin'], tb['covt'], tb['emat'])


def _tri(L, lower_incl):
    r = lax.broadcasted_iota(jnp.int32, (L, L), 0)
    c = lax.broadcasted_iota(jnp.int32, (L, L), 1)
    return (r >= c) if lower_incl else (r > c)


def _mlstm_kernel(x_ref, small_ref, smallt_ref, brow_ref, bcol_ref, c0_ref, n0_ref, m0_ref,
                  h_ref, co_ref, no_ref, mo_ref, c_s, n_s, m_s, *, L, t_real):
    c = pl.program_id(1)
    nc = pl.num_programs(1)

    @pl.when(c == 0)
    def _():
        c_s[...] = c0_ref[0]
        n_s[...] = n0_ref[0]
        m_s[...] = m0_ref[0]

    sm = small_ref[0]
    smt = smallt_ref[0, 0]
    li_col = sm[:, SM_MIF:SM_MIF + 4] + brow_ref[0:1, 0:4]
    lf_col = _log_sigmoid(sm[:, SM_MIF + 4:SM_MIF + 8] + brow_ref[0:1, 4:8])
    li_row = smt[0:4, :] + bcol_ref[0:4, :]
    lf_row = _log_sigmoid(smt[4:8, :] + bcol_ref[4:8, :])
    if t_real % L:
        tcol = lax.broadcasted_iota(jnp.int32, (L, 4), 0) + c * L
        trow = lax.broadcasted_iota(jnp.int32, (4, L), 1) + c * L
        li_col = jnp.where(tcol < t_real, li_col, NEG)
        lf_col = jnp.where(tcol < t_real, lf_col, 0.0)
        li_row = jnp.where(trow < t_real, li_row, NEG)
        lf_row = jnp.where(trow < t_real, lf_row, 0.0)
    low = _tri(L, True)
    b_col = jnp.dot(low.astype(F32), lf_col, precision=HI, preferred_element_type=F32)
    b_row = lax.dot_general(lf_row, low.astype(F32), NT, precision=HI, preferred_element_type=F32)

    for h in range(M_HEADS):
        hs = slice(h * M_HD, (h + 1) * M_HD)
        q = x_ref[0, :, h * M_HD:(h + 1) * M_HD]
        k = x_ref[0, :, W_M + h * M_HD:W_M + (h + 1) * M_HD] * (M_HD ** -0.5)
        v = x_ref[0, :, 2 * W_M + h * M_HD:2 * W_M + (h + 1) * M_HD]
        bc = b_col[:, h:h + 1]
        br = b_row[h:h + 1, :]
        Dm = jnp.where(low, bc - br + li_row[h:h + 1, :], NEG)
        m_prev = m_s[h][:, 0:1]
        a = bc + m_prev
        mt = jnp.maximum(a, jnp.max(Dm, axis=-1, keepdims=True))
        S = lax.dot_general(q, k, NT, preferred_element_type=F32) * jnp.exp(Dm - mt)
        inter = jnp.exp(a - mt)
        C = c_s[h]
        n = n_s[h]
        num = inter * jnp.dot(q, C, preferred_element_type=F32) + jnp.dot(S, v, preferred_element_type=F32)
        den = inter * jnp.sum(q * n, axis=-1, keepdims=True) + jnp.sum(S, axis=-1, keepdims=True)
        h_ref[0, :, hs] = num / jnp.maximum(jnp.abs(den), jnp.exp(-mt))
        bL = bc[L - 1:L, :]
        wlog = bL - bc + li_col[:, h:h + 1]
        m_new = jnp.maximum(bL + m_prev, jnp.max(wlog, axis=0, keepdims=True))
        ws = jnp.exp(wlog - m_new)
        dec = jnp.exp(bL + m_prev - m_new)
        kw = k * ws
        c_s[h] = dec * C + lax.dot_general(kw, v, TN, preferred_element_type=F32)
        n_s[h] = dec * n + jnp.sum(kw, axis=0, keepdims=True)
        m_s[h] = jnp.broadcast_to(m_new, (1, LANES))

    @pl.when(c == nc - 1)
    def _():
        co_ref[0] = c_s[...]
        no_ref[0] = n_s[...]
        mo_ref[0] = m_s[...]


def _mlstm(u3, smallt, m_bi, m_bf, C0, n0, m0, L, t_real):
    B, T, _ = u3.shape
    nc = T // L
    brow = jnp.concatenate([m_bi, m_bf]).reshape(1, 8)
    bcol = jnp.concatenate([m_bi, m_bf]).reshape(8, 1)
    n0 = n0.reshape(B, M_HEADS, 1, M_HD)
    m0 = jnp.broadcast_to(m0[:, :, None, None], (B, M_HEADS, 1, LANES))
    kern = functools.partial(_mlstm_kernel, L=L, t_real=t_real)
    st = lambda shape: pl.BlockSpec(shape, lambda b, c: (b,) + (0,) * (len(shape) - 1))
    h, Cn, nn, mn = pl.pallas_call(
        kern,
        grid=(B, nc),
        in_specs=[pl.BlockSpec((1, L, 3 * W_M), lambda b, c: (b, c, OFF['m_qkv'] // (3 * W_M))),
                  pl.BlockSpec((1, L, LANES), lambda b, c: (b, c, OFF['small'] // LANES)),
                  pl.BlockSpec((1, 1, 16, L), lambda b, c: (b, c, 0, 0)),
                  pl.BlockSpec((1, 8), lambda b, c: (0, 0)), pl.BlockSpec((8, 1), lambda b, c: (0, 0)),
                  st((1, M_HEADS, M_HD, M_HD)), st((1, M_HEADS, 1, M_HD)), st((1, M_HEADS, 1, LANES))],
        out_specs=[pl.BlockSpec((1, L, W_M), lambda b, c: (b, c, 0)),
                   st((1, M_HEADS, M_HD, M_HD)), st((1, M_HEADS, 1, M_HD)), st((1, M_HEADS, 1, LANES))],
        out_shape=[jax.ShapeDtypeStruct((B, T, W_M), F32),
                   jax.ShapeDtypeStruct((B, M_HEADS, M_HD, M_HD), F32),
                   jax.ShapeDtypeStruct((B, M_HEADS, 1, M_HD), F32),
                   jax.ShapeDtypeStruct((B, M_HEADS, 1, LANES), F32)],
        scratch_shapes=[pltpu.VMEM((M_HEADS, M_HD, M_HD), F32), pltpu.VMEM((M_HEADS, 1, M_HD), F32),
                        pltpu.VMEM((M_HEADS, 1, LANES), F32)],
        compiler_params=_cparams(("parallel", "arbitrary")),
        name="mlstm",
    )(u3, u3, smallt, brow, bcol, C0, n0, m0)
    return h, Cn, nn.reshape(B, M_HEADS, M_HD), mn[:, :, 0, 0]


def _gdn_a_kernel(x_ref, small_ref, smallt_ref, cw_ref, prow_ref, pcol_ref, buf_ref,
                  u_ref, w_ref, qe_ref, kd_ref, at_ref, eg_ref, xbuf, *, L, CB, t_real):
    c = pl.program_id(1)
    PRE = 8
    R = CB * L

    @pl.when(c == 0)
    def _():
        xbuf[0:PRE, :] = buf_ref[0]

    xbuf[PRE:PRE + R, :] = x_ref[0]
    conv = jnp.zeros((R, 3 * W_G), F32)
    for j in range(CONV_W):
        conv = conv + xbuf[pl.ds(PRE - (CONV_W - 1) + j, R), :] * cw_ref[j:j + 1, :]
    xbuf[0:PRE, :] = xbuf[R:R + PRE, :]
    conv = _silu(conv)

    sm = small_ref[0]
    g_all = -jnp.exp(prow_ref[0:1, 0:4]) * _softplus(sm[:, SM_GAB:SM_GAB + 4] + prow_ref[1:2, 0:4])
    beta_all = jax.nn.sigmoid(sm[:, SM_GAB + 4:SM_GAB + 8])
    low = _tri(L, True)
    strict = _tri(L, False)
    eye = (low & ~strict).astype(F32)
    lowf = low.astype(F32)

    for cb in range(CB):
        rs = slice(cb * L, (cb + 1) * L)
        smt = smallt_ref[0, cb]
        g_col = g_all[rs]
        beta_col = beta_all[rs]
        g_row = -jnp.exp(pcol_ref[0:4, 0:1]) * _softplus(smt[8:12, :] + pcol_ref[0:4, 1:2])
        if t_real % L:
            t0 = (c * CB + cb) * L
            tcol = lax.broadcasted_iota(jnp.int32, (L, 4), 0) + t0
            trow = lax.broadcasted_iota(jnp.int32, (4, L), 1) + t0
            g_col = jnp.where(tcol < t_real, g_col, 0.0)
            beta_col = jnp.where(tcol < t_real, beta_col, 0.0)
            g_row = jnp.where(trow < t_real, g_row, 0.0)
        G_col = jnp.dot(lowf, g_col, precision=HI, preferred_element_type=F32)
        G_row = lax.dot_general(g_row, lowf, NT, precision=HI, preferred_element_type=F32)

        for h in range(G_HEADS):
            hs = slice(h * G_HD, (h + 1) * G_HD)
            cq = conv[rs, h * G_HD:(h + 1) * G_HD]
            ck = conv[rs, W_G + h * G_HD:W_G + (h + 1) * G_HD]
            v = conv[rs, 2 * W_G + h * G_HD:2 * W_G + (h + 1) * G_HD]
            q = cq * lax.rsqrt(jnp.sum(cq * cq, axis=-1, keepdims=True) + EPS) * (G_HD ** -0.5)
            k = ck * lax.rsqrt(jnp.sum(ck * ck, axis=-1, keepdims=True) + EPS)
            Gc = G_col[:, h:h + 1]
            Gr = G_row[h:h + 1, :]
            dmask = jnp.where(low, jnp.exp(jnp.where(low, Gc - Gr, 0.0)), 0.0)
            bcol = beta_col[:, h:h + 1]
            kb = k * bcol
            ks = _split(k)
            A = _dot3(_split(kb), ks, NT) * jnp.where(strict, dmask, 0.0)
            eG = jnp.exp(Gc)
            rhs = jnp.concatenate([v * bcol, kb * eG], axis=1)
            X = eye - A
            As = _split(A)
            Pw = _dot3(As, As)
            span = 2
            while span < L:
                Ps = _split(Pw)
                X = X + _dot3(_split(X), Ps)
                span *= 2
                if span < L:
                    Pw = _dot3(Ps, Ps)
            sol = _dot3(_split(X), _split(rhs))
            GL = Gc[L - 1:L, :]
            u_ref[0, rs, hs] = sol[:, 0:G_HD]
            w_ref[0, rs, hs] = sol[:, G_HD:2 * G_HD]
            qe_ref[0, rs, hs] = q * eG
            kd_ref[0, rs, hs] = k * jnp.exp(GL - Gc)
            at_ref[0, rs, h * L:(h + 1) * L] = lax.dot_general(q, k, NT, preferred_element_type=F32) * dmask
            eg_ref[0, cb, h:h + 1, :] = jnp.broadcast_to(jnp.exp(GL), (1, LANES))
        eg_ref[0, cb, G_HEADS:8, :] = jnp.zeros((8 - G_HEADS, LANES), F32)


def _gdn_b_kernel(u_ref, w_ref, qe_ref, kd_ref, at_ref, eg_ref, s0_ref, o_ref, so_ref, s_s, *, L, BB):
    c = pl.program_id(1)
    nc = pl.num_programs(1)

    @pl.when(c == 0)
    def _():
        s_s[...] = s0_ref[...]

    for b in range(BB):
        for h in range(G_HEADS):
            hs = slice(h * G_HD, (h + 1) * G_HD)
            S = s_s[b, h]
            wq = jnp.concatenate([w_ref[b, :, hs], qe_ref[b, :, hs]], axis=0)
            r = jnp.dot(wq, S, preferred_element_type=F32)
            v_new = u_ref[b, :, hs] - r[0:L]
            o_ref[b, :, hs] = r[L:2 * L] + jnp.dot(at_ref[b, :, h * L:(h + 1) * L], v_new,
                                                   preferred_element_type=F32)
            s_s[b, h] = eg_ref[b, 0, h:h + 1, :] * S + lax.dot_general(kd_ref[b, :, hs], v_new, TN,
                                                                      preferred_element_type=F32)

    @pl.when(c == nc - 1)
    def _():
        so_ref[...] = s_s[...]


def _gdn(u3, smallt, g_conv, g_A_log, g_dt_bias, buf, S0, L, t_real):
    B, T, _ = u3.shape
    nc = T // L
    CB = 2 if nc % 2 == 0 else 1
    BB = 8 if B % 8 == 0 else B
    prow = jnp.stack([g_A_log, g_dt_bias])
    pcol = jnp.stack([g_A_log, g_dt_bias], axis=1)
    buf8 = jnp.concatenate([jnp.zeros((B, 8 - (CONV_W - 1), 3 * W_G), F32), buf], axis=1)
    st = lambda shape: pl.BlockSpec(shape, lambda b, c: (b,) + (0,) * (len(shape) - 1))
    cst = lambda shape: pl.BlockSpec(shape, lambda b, c: (0,) * len(shape))
    row = lambda nb, r, w: pl.BlockSpec((nb, r, w), lambda b, c: (b, c, 0))
    tok = lambda w: jax.ShapeDtypeStruct((B, T, w), F32)
    u, w, qe, kd, at, eg = pl.pallas_call(
        functools.partial(_gdn_a_kernel, L=L, CB=CB, t_real=t_real),
        grid=(B, nc // CB),
        in_specs=[pl.BlockSpec((1, CB * L, 3 * W_G), lambda b, c: (b, c, OFF['g_qkv'] // (3 * W_G))),
                  pl.BlockSpec((1, CB * L, LANES), lambda b, c: (b, c, OFF['small'] // LANES)),
                  pl.BlockSpec((1, CB, 16, L), lambda b, c: (b, c, 0, 0)),
                  cst((CONV_W, 3 * W_G)), cst((2, 4)), cst((4, 2)), st((1, 8, 3 * W_G))],
        out_specs=[row(1, CB * L, W_G)] * 4 + [row(1, CB * L, G_HEADS * L),
                                               pl.BlockSpec((1, CB, 8, LANES), lambda b, c: (b, c, 0, 0))],
        out_shape=[tok(W_G)] * 4 + [tok(G_HEADS * L), jax.ShapeDtypeStruct((B, nc, 8, LANES), F32)],
        scratch_shapes=[pltpu.VMEM((CB * L + 8, 3 * W_G), F32)],
        compiler_params=_cparams(("parallel", "arbitrary")),
        name="gdn_a",
    )(u3, u3, smallt, g_conv, prow, pcol, buf8)
    sblk = pl.BlockSpec((BB, G_HEADS, G_HD, G_HD), lambda b, c: (b, 0, 0, 0))
    return pl.pallas_call(
        functools.partial(_gdn_b_kernel, L=L, BB=BB),
        grid=(B // BB, nc),
        in_specs=[row(BB, L, W_G)] * 4 + [row(BB, L, G_HEADS * L),
                                          pl.BlockSpec((BB, 1, 8, LANES), lambda b, c: (b, c, 0, 0)), sblk],
        out_specs=[row(BB, L, W_G), sblk],
        out_shape=[tok(W_G), jax.ShapeDtypeStruct((B, G_HEADS, G_HD, G_HD), F32)],
        scratch_shapes=[pltpu.VMEM((BB, G_HEADS, G_HD, G_HD), F32)],
        compiler_params=_cparams(("parallel", "arbitrary")),
        name="gdn_b",
    )(u, w, qe, kd, at, eg, S0)


def _head_rmsnorm(x, gain_ref, nheads, hd):
    outs = []
    for h in range(nheads):
        xh = x[:, h * hd:(h + 1) * hd]
        ms = jnp.mean(xh * xh, axis=-1, keepdims=True)
        outs.append(xh * lax.rsqrt(ms + EPS) * gain_ref[...])
    return jnp.concatenate(outs, axis=1)


def _merge_kernel(x_ref, oa_ref, hm_ref, og_ref, az_ref, mo_ref, mz_ref, gz_ref, mg_ref,
                  wb_ref, wo_ref, mhn_ref, ghn_ref, y_ref):
    oa = oa_ref[...] * _silu(az_ref[...])
    om = _head_rmsnorm(hm_ref[...], mhn_ref, M_HEADS, M_HD) * jax.nn.sigmoid(mo_ref[...]) * _silu(mz_ref[...])
    og = _head_rmsnorm(og_ref[...], ghn_ref, G_HEADS, G_HD) * _silu(gz_ref[...])
    y = jnp.zeros(y_ref.shape, F32)
    for i, br in enumerate((oa, om, og)):
        proj = jnp.dot(br.astype(BF16), wb_ref[i], preferred_element_type=F32)
        y = y + jax.nn.sigmoid(mg_ref[:, i * D_MODEL:(i + 1) * D_MODEL]) * proj
    y_ref[...] = x_ref[...] + jnp.dot(y.astype(BF16), wo_ref[...], preferred_element_type=F32)


def _merge_out(x2, o_a, h_m, o_g, u2, w_branch, w_out, m_hn, g_hn):
    n = x2.shape[0]
    tm = min(512, n)
    row = lambda w, off: pl.BlockSpec((tm, w), lambda i: (i, off // w))
    cst = lambda shape: pl.BlockSpec(shape, lambda i: (0,) * len(shape))
    return pl.pallas_call(
        _merge_kernel,
        grid=(n // tm,),
        in_specs=[row(D_MODEL, 0), row(W_A, 0), row(W_M, 0), row(W_G, 0),
                  row(W_A, OFF['a_z']), row(W_M, OFF['m_o']), row(W_M, OFF['m_z']), row(W_G, OFF['g_z']),
                  row(N_BRANCH * D_MODEL, OFF['merge']),
                  cst((N_BRANCH, W_A, D_MODEL)), cst((D_MODEL, D_MODEL)), cst((1, M_HD)), cst((1, G_HD))],
        out_specs=row(D_MODEL, 0),
        out_shape=jax.ShapeDtypeStruct((n, D_MODEL), F32),
        compiler_params=_cparams(("parallel",)),
        name="merge_out",
    )(x2, o_a, h_m, o_g, u2, u2, u2, u2, u2, w_branch.astype(BF16), w_out.astype(BF16),
      m_hn.reshape(1, M_HD), g_hn.reshape(1, G_HD))


def _small_t(u3, L):
    B, T, _ = u3.shape
    s = u3[:, :, OFF['small'] + SM_MIF:OFF['small'] + SM_MIF + 16]
    return s.reshape(B, T // L, L, 16).transpose(0, 1, 3, 2)


def _layer(x, lw, tb, past, q_off):
    B, T, _ = x.shape
    x2 = x.reshape(B * T, D_MODEL)
    u2 = _proj_in(x2, lw['norm_g'], lw['w_perm'])
    if past is None:
        tp, L = T, math.gcd(T, M_CHUNK)
        u3 = u2.reshape(B, T, N_PERM)
        C0 = jnp.zeros((B, M_HEADS, M_HD, M_HD), F32)
        n0 = jnp.zeros((B, M_HEADS, M_HD), F32)
        m0 = jnp.zeros((B, M_HEADS), F32)
        S0 = jnp.zeros((B, G_HEADS, G_HD, G_HD), F32)
        buf = jnp.zeros((B, CONV_W - 1, 3 * W_G), F32)
    else:
        tp = -(-T // 8) * 8
        L = tp
        u3 = jnp.pad(u2.reshape(B, T, N_PERM), ((0, 0), (0, tp - T), (0, 0)))
        C0, n0, m0, S0, buf = past['mC'], past['mn'], past['mm'], past['gS'], past['gconv']
    up = u3.reshape(B * tp, N_PERM)
    qn, rows6 = _prep_a(up, lw['a_qn'], lw['a_kn'])
    qn3 = qn.reshape(B, tp, W_A)
    if past is None:
        o_a = _nsa_prompt(qn3, rows6, u3, lw['wk'], lw['wv'], tb)
    else:
        o_a = _nsa_sample(qn3, rows6, u3, past['cmp'], past['sel'], past['win'], past['page_table'],
                          lw['wk'], lw['wv'], tb, q_off, T)
    smallt = _small_t(u3, L)
    h_m, Cn, nn, mn = _mlstm(u3, smallt, lw['m_bi'], lw['m_bf'], C0, n0, m0, L, T)
    o_g, Sn = _gdn(u3, smallt, lw['g_conv'], lw['g_A_log'], lw['g_dt_bias'], buf, S0, L, T)
    if tp != T:
        o_a, h_m, o_g = o_a[:, :T], h_m[:, :T], o_g[:, :T]
    y2 = _merge_out(x2, o_a.reshape(B * T, W_A), h_m.reshape(B * T, W_M), o_g.reshape(B * T, W_G), u2,
                    lw['w_branch'], lw['w_out'], lw['m_hn'], lw['g_hn'])
    rows = rows6.reshape(3, 2, B, tp, A_KV, A_HD)[:, :, :, :T]
    new_cmp, new_sel, new_win = (jnp.moveaxis(rows[b], 0, 2) for b in range(3))
    g_qkv = u3[:, :T, OFF['g_qkv']:OFF['g_qkv'] + 3 * W_G]
    if past is None:
        win = new_win
    else:
        win = jnp.concatenate([past['win'].reshape(B, -1, 2, A_KV, A_HD), new_win], axis=1)
    full = jnp.concatenate([buf, g_qkv[:, max(T - (CONV_W - 1), 0):]], axis=1)
    state = dict(cmp=new_cmp, sel=new_sel, win=win[:, -min(WINDOW, win.shape[1]):],
                 mC=Cn, mn=nn, mm=mn, gS=Sn, gconv=full[:, -(CONV_W - 1):])
    return y2.reshape(B, T, D_MODEL), state


def kernel(x_prompt, x_sample, cache_cmp_kv, cache_sel_kv, cache_win_kv, state_mlstm_C, state_mlstm_n,
           state_mlstm_m, state_gdn_S, state_gdn_conv, page_table, norm_g, w_in, a_qn, a_kn, a_cmp_wk,
           a_cmp_wv, rel_bias, m_bi, m_bf, m_hn, g_conv, g_A_log, g_dt_bias, g_hn, w_branch, w_out):
    names = ('cmp', 'sel', 'win', 'mC', 'mn', 'mm', 'gS', 'gconv')
    st_p = {k: [] for k in names}
    st_s = {k: [] for k in names}
    past_len = page_table.shape[1] * PAGE_SIZE
    n_pool = cache_cmp_kv.shape[1]
    db, dt = x_sample.shape[0], x_sample.shape[1]
    tb_p = _prompt_tables(rel_bias, x_prompt.shape[1])
    tb_s = _sample_tables(rel_bias, past_len, -(-dt // 8) * 8, dt)
    y_p, y_s = x_prompt, x_sample
    for l in range(DEPTH):
        lw = dict(norm_g=norm_g[l], w_perm=_permute_w_in(w_in[l]), a_qn=a_qn[l], a_kn=a_kn[l],
                  wk=_cmp_weights(a_cmp_wk[l]), wv=_cmp_weights(a_cmp_wv[l]),
                  m_bi=m_bi[l], m_bf=m_bf[l], m_hn=m_hn[l], g_conv=g_conv[l], g_A_log=g_A_log[l],
                  g_dt_bias=g_dt_bias[l], g_hn=g_hn[l], w_branch=w_branch[l], w_out=w_out[l])
        y_p, new_p = _layer(y_p, lw, tb_p, None, 0)
        past = dict(cmp=cache_cmp_kv[l].reshape(n_pool, PAGE_SIZE, 256),
                    sel=cache_sel_kv[l].reshape(n_pool, PAGE_SIZE, 256),
                    win=cache_win_kv[l].reshape(db, -1, 256), page_table=page_table,
                    mC=state_mlstm_C[l], mn=state_mlstm_n[l], mm=state_mlstm_m[l],
                    gS=state_gdn_S[l], gconv=state_gdn_conv[l])
        y_s, new_s = _layer(y_s, lw, tb_s, past, past_len)
        for k in names:
            st_p[k].append(new_p[k])
            st_s[k].append(new_s[k])
    P = {k: jnp.stack(v) for k, v in st_p.items()}
    S = {k: jnp.stack(v) for k, v in st_s.items()}
    return (y_p, y_s, P['cmp'], S['cmp'], P['sel'], S['sel'], P['win'], S['win'],
            P['mC'], S['mC'], P['mn'], S['mn'], P['mm'], S['mm'], P['gS'], S['gS'], P['gconv'], S['gconv'])
```

```python
import functools
import math

import jax
import jax.numpy as jnp
from jax import lax
from jax.experimental import pallas as pl
from jax.experimental.pallas import tpu as pltpu

D_MODEL = 1024
DEPTH = 2
PAGE_SIZE = 128
A_HEADS = 8
A_KV = 2
A_REP = A_HEADS // A_KV
A_HD = 64
CMP_BLOCK = 32
CMP_STRIDE = 16
SEL_BLOCK = 64
TOP_N = 16
WINDOW = 512
Q_BLOCK = 128
N_BUCKETS = 32
MAX_DIST = 2048
M_HEADS = 4
M_HD = 128
M_CHUNK = 64
G_HEADS = 4
G_HD = 128
G_CHUNK = 64
CONV_W = 4
W_A = A_HEADS * A_HD
W_M = M_HEADS * M_HD
W_G = G_HEADS * G_HD
N_BRANCH = 3
EPS = 1e-6
NEG = -1e30
TINY = 1e-30

F32 = jnp.float32
BF16 = jnp.bfloat16
HI = lax.Precision.HIGHEST
NN = (((1,), (0,)), ((), ()))
NT = (((1,), (1,)), ((), ()))
TN = (((0,), (0,)), ((), ()))

LANES = 128
VMEM_LIMIT = 56 * 1024 * 1024

IN_ORDER = ('a_q', 'a_kv', 'a_gate', 'a_z', 'm_qkv', 'm_if', 'm_o', 'm_z', 'g_qkv', 'g_ab', 'g_z', 'merge')
IN_WIDTH = dict(a_q=W_A, a_kv=3 * 2 * A_KV * A_HD, a_gate=A_HEADS * 3, a_z=W_A, m_qkv=3 * W_M, m_if=2 * M_HEADS,
                m_o=W_M, m_z=W_M, g_qkv=3 * W_G, g_ab=2 * G_HEADS, g_z=W_G, merge=N_BRANCH * D_MODEL)
OFF = dict(merge=0, m_qkv=3072, g_qkv=4608, a_q=6144, a_z=6656, m_o=7168, m_z=7680, g_z=8192, a_kv=8704,
           small=9472)
N_PERM = 9600
SM_GATE, SM_MIF, SM_GAB = 0, 24, 32


def _cparams(sem):
    return pltpu.CompilerParams(dimension_semantics=sem, vmem_limit_bytes=VMEM_LIMIT)


def _silu(x):
    return x * jax.nn.sigmoid(x)


def _log_sigmoid(x):
    return jnp.minimum(x, 0.0) - jnp.log(1.0 + jnp.exp(-jnp.abs(x)))


def _softplus(x):
    return jnp.maximum(x, 0.0) + jnp.log(1.0 + jnp.exp(-jnp.abs(x)))


def _split(a):
    hi = a.astype(BF16)
    return hi, (a - hi.astype(F32)).astype(BF16)


def _dot3(a, b, dims=NN):
    mm = lambda x, y: lax.dot_general(x, y, dims, preferred_element_type=F32)
    return mm(a[0], b[0]) + mm(a[0], b[1]) + mm(a[1], b[0])


def _src_offsets():
    offs, off = {}, 0
    for name in IN_ORDER:
        offs[name] = off
        off += IN_WIDTH[name]
    return offs, off


def _permute_kernel(w_ref, o_ref):
    src, n_in = _src_offsets()
    dst = dict(OFF, a_gate=OFF['small'] + SM_GATE, m_if=OFF['small'] + SM_MIF, g_ab=OFF['small'] + SM_GAB)
    for name in IN_ORDER:
        w = IN_WIDTH[name]
        a0 = src[name] // LANES * LANES
        a1 = min(-(-(src[name] + w) // LANES) * LANES, n_in)
        win = w_ref[:, a0:a1]
        o_ref[:, dst[name]:dst[name] + w] = win[:, src[name] - a0:src[name] - a0 + w].astype(BF16)
    used = SM_GAB + IN_WIDTH['g_ab']
    o_ref[:, OFF['small'] + used:N_PERM] = jnp.zeros((o_ref.shape[0], LANES - used), BF16)


def _permute_w_in(w_in):
    d, n_in = w_in.shape
    tr = 128
    return pl.pallas_call(
        _permute_kernel,
        grid=(d // tr,),
        in_specs=[pl.BlockSpec((tr, n_in), lambda i: (i, 0))],
        out_specs=pl.BlockSpec((tr, N_PERM), lambda i: (i, 0)),
        out_shape=jax.ShapeDtypeStruct((d, N_PERM), BF16),
        compiler_params=_cparams(("parallel",)),
        name="permute_w",
    )(w_in)


def _rel_bucket(dist):
    n = jnp.maximum(dist, 0)
    exact = N_BUCKETS // 2
    nf = jnp.maximum(n, exact).astype(F32)
    large = exact + (jnp.log(nf / exact) / math.log(MAX_DIST / exact) * (N_BUCKETS - exact)).astype(jnp.int32)
    return jnp.where(n < exact, n, jnp.minimum(large, N_BUCKETS - 1))


def _bias_kernel(thr_ref, tab_ref, d_ref, o_ref):
    n = jnp.maximum(d_ref[0], 0)
    for h in range(A_HEADS):
        acc = jnp.full(n.shape, tab_ref[h], F32)
        for k in range(1, N_BUCKETS):
            acc = jnp.where(n >= thr_ref[k], tab_ref[k * A_HEADS + h], acc)
        o_ref[0, h // A_REP, h % A_REP] = acc


def _bias_rows(rel_bias, dist):
    N, Q, K = dist.shape
    nmax = 2 * MAX_DIST
    thr = jnp.sum(_rel_bucket(jnp.arange(nmax))[None, :] < jnp.arange(N_BUCKETS)[:, None], axis=1).astype(jnp.int32)
    smem = pl.BlockSpec(memory_space=pltpu.SMEM)
    out = pl.pallas_call(
        _bias_kernel,
        grid=(N,),
        in_specs=[smem, smem, pl.BlockSpec((1, Q, K), lambda i: (i, 0, 0))],
        out_specs=pl.BlockSpec((1, A_KV, A_REP, Q, K), lambda i: (i, 0, 0, 0, 0)),
        out_shape=jax.ShapeDtypeStruct((N, A_KV, A_REP, Q, K), F32),
        compiler_params=_cparams(("parallel",)),
        name="bias_rows",
    )(thr, rel_bias.astype(F32).reshape(N_BUCKETS * A_HEADS), dist.astype(jnp.int32))
    return out.reshape(N, A_KV, A_REP * Q, K)


def _cmp_weights(w):
    wr = w.reshape(A_KV, 2, CMP_STRIDE, A_HD, A_HD)
    eye = jnp.eye(A_KV, dtype=w.dtype)
    full = jnp.einsum('gmjde,gh->jgdmhe', wr, eye)
    return full.reshape(CMP_STRIDE, A_KV * A_HD, 2 * A_KV * A_HD).astype(BF16)


def _cover_t(ns_pad, nch):
    s0 = jnp.arange(ns_pad)[:, None] * SEL_BLOCK
    c0 = jnp.arange(nch)[None, :] * CMP_STRIDE
    return ((c0 < s0 + SEL_BLOCK) & (s0 <= c0 + CMP_BLOCK - 1)).astype(F32)


def _expand_mat(ns_pad, nk):
    return (jnp.arange(nk)[None, :] // SEL_BLOCK == jnp.arange(ns_pad)[:, None]).astype(BF16)


def _gate_expand():
    col = jnp.arange(N_BRANCH * W_A)
    src = 3 * ((col % W_A) // A_HD) + col // W_A
    return (jnp.arange(LANES)[:, None] == src[None, :]).astype(BF16)


def _prompt_tables(rel_bias, T):
    QB = Q_BLOCK
    nqb = T // QB
    nch = T // CMP_STRIDE
    ns = T // SEL_BLOCK
    ns_pad = -(-ns // 8) * 8
    i_ = jnp.arange(QB)
    t = (jnp.arange(nqb) * QB)[:, None, None] + i_[None, :, None]
    cend = (jnp.arange(nch) * CMP_STRIDE + CMP_BLOCK - 1)[None, None, :]
    d = (jnp.arange(nqb) * QB)[:, None, None] + i_[None, :, None] - i_[None, None, :]
    nw = WINDOW // QB + 1
    return dict(
        bcmp=_bias_rows(rel_bias, t - cend),
        tsel=_bias_rows(rel_bias, d),
        covt=_cover_t(ns_pad, nch),
        e3=_expand_mat(LANES, T).reshape(LANES, nqb, QB).transpose(1, 0, 2),
        wm=((d[:nw] >= 0) & (d[:nw] < WINDOW)).astype(F32),
        cm=(d[:2] >= 0).astype(F32),
        gx=_gate_expand())


def _sample_tables(rel_bias, past, tp, t_real):
    SL = past + LANES
    nblk = past // CMP_STRIDE
    ns = -(-(past + t_real) // SEL_BLOCK)
    ns_pad = -(-ns // LANES) * LANES
    WK = WINDOW + LANES
    i_ = jnp.arange(tp)
    t = past + i_
    rows = lambda dist: _bias_rows(rel_bias, dist[None])[0]
    return dict(
        bcmp=rows(t[:, None] - (jnp.arange(nblk) * CMP_STRIDE + CMP_BLOCK - 1)[None, :]),
        bsel=rows(t[:, None] - jnp.arange(SL)[None, :]),
        bwin=rows(i_[:, None] + WINDOW - jnp.arange(WK)[None, :]),
        cov=_cover_t(ns_pad, nblk).T, emat=_expand_mat(ns_pad, SL))


def _proj_in_kernel(x_ref, g_ref, w_ref, o_ref, hn_ref):
    @pl.when(pl.program_id(1) == 0)
    def _():
        x = x_ref[...]
        ms = jnp.mean(x * x, axis=-1, keepdims=True)
        hn_ref[...] = (x * lax.rsqrt(ms + EPS) * g_ref[...]).astype(BF16)

    o_ref[...] = jnp.dot(hn_ref[...], w_ref[...], preferred_element_type=F32)


def _proj_in(x2, norm_g, w_perm):
    n = x2.shape[0]
    tm = min(1024, n)
    tn = 1920
    return pl.pallas_call(
        _proj_in_kernel,
        grid=(n // tm, N_PERM // tn),
        in_specs=[pl.BlockSpec((tm, D_MODEL), lambda i, j: (i, 0)),
                  pl.BlockSpec((1, D_MODEL), lambda i, j: (0, 0)),
                  pl.BlockSpec((D_MODEL, tn), lambda i, j: (0, j))],
        out_specs=pl.BlockSpec((tm, tn), lambda i, j: (i, j)),
        out_shape=jax.ShapeDtypeStruct((n, N_PERM), F32),
        scratch_shapes=[pltpu.VMEM((tm, D_MODEL), BF16)],
        compiler_params=_cparams(("parallel", "arbitrary")),
        name="proj_in",
    )(x2, norm_g.reshape(1, D_MODEL), w_perm)


def _prep_a_kernel(q_ref, kv0_ref, kv1_ref, kv2_ref, bdq_ref, bdk_ref, qg_ref, kg_ref, qo_ref, ro_ref):
    q = q_ref[...]
    ms = jnp.dot(q * q, bdq_ref[...], precision=HI, preferred_element_type=F32) * (1.0 / A_HD)
    qo_ref[...] = q * lax.rsqrt(ms + EPS) * qg_ref[...] * (A_HD ** -0.5)
    for b, kv_ref in enumerate((kv0_ref, kv1_ref, kv2_ref)):
        k = kv_ref[:, 0:128]
        ms = jnp.dot(k * k, bdk_ref[...], precision=HI, preferred_element_type=F32) * (1.0 / A_HD)
        ro_ref[2 * b] = k * lax.rsqrt(ms + EPS) * kg_ref[b:b + 1, :]
        ro_ref[2 * b + 1] = kv_ref[:, 128:256]


def _prep_a(u2, a_qn, a_kn):
    n = u2.shape[0]
    tm = min(512, n)
    bd = lambda w: (jnp.arange(w)[:, None] // A_HD == jnp.arange(w)[None, :] // A_HD).astype(F32)
    qg = jnp.tile(a_qn, A_HEADS).reshape(1, W_A)
    kg = jnp.tile(a_kn, (1, A_KV))
    const = lambda shape: pl.BlockSpec(shape, lambda i: (0, 0))
    kvs = lambda b: pl.BlockSpec((tm, 256), lambda i: (i, OFF['a_kv'] // 256 + b))
    return pl.pallas_call(
        _prep_a_kernel,
        grid=(n // tm,),
        in_specs=[pl.BlockSpec((tm, W_A), lambda i: (i, OFF['a_q'] // W_A)), kvs(0), kvs(1), kvs(2),
                  const((W_A, W_A)), const((128, 128)), const((1, W_A)), const((3, 128))],
        out_specs=[pl.BlockSpec((tm, W_A), lambda i: (i, 0)), pl.BlockSpec((6, tm, 128), lambda i: (0, i, 0))],
        out_shape=[jax.ShapeDtypeStruct((n, W_A), F32), jax.ShapeDtypeStruct((6, n, 128), F32)],
        compiler_params=_cparams(("parallel",)),
        name="prep_a",
    )(u2, u2, u2, u2, bd(W_A), bd(128), qg, kg)


def _stack_heads(q, g):
    return jnp.concatenate([q[:, (A_REP * g + r) * A_HD:(A_REP * g + r + 1) * A_HD] for r in range(A_REP)], axis=0)


def _rep_rows(x):
    return jnp.concatenate([x] * A_REP, axis=0)


def _masked_softmax(s, valid):
    sm = jnp.where(valid, s, NEG)
    m = jnp.max(sm, axis=-1, keepdims=True)
    e = jnp.where(valid, jnp.exp(sm - m), 0.0)
    l = jnp.sum(e, axis=-1, keepdims=True)
    return e / jnp.maximum(l, TINY)


def _select_blocks(pc_sum, covt_ref, st_ref, t_row, ns, n_top):
    ns_pad = covt_ref.shape[0]
    imp = lax.dot_general(covt_ref[...], pc_sum, NT, precision=HI, preferred_element_type=F32)
    jj = lax.broadcasted_iota(jnp.int32, (ns_pad, LANES), 0)
    tt = jnp.broadcast_to(t_row, (ns_pad, LANES))
    cur = tt // SEL_BLOCK
    forced = (jj == 0) | (jj == cur) | (jj == cur - 1)
    future = jj * SEL_BLOCK > tt
    score = jnp.where(future, NEG, jnp.where(forced, -NEG, imp))
    score = jnp.where(jj < ns, score, -jnp.inf)
    st_ref[...] = score

    def beats(k, rank):
        row = jnp.broadcast_to(st_ref[pl.ds(k, 1), :], (ns_pad, LANES))
        b = (row > score) | ((row == score) & (k < jj))
        return rank + jnp.where(b, 1.0, 0.0)

    rank = jnp.zeros((ns_pad, LANES), F32)
    if ns <= 32:
        for k in range(ns):
            rank = beats(k, rank)
    else:
        rank = lax.fori_loop(0, ns, beats, rank)
    return jnp.where(rank < n_top, 1.0, 0.0)


def _pad_rows(x, rows):
    if x.shape[0] == rows:
        return x
    return jnp.concatenate([x, jnp.zeros((rows - x.shape[0], x.shape[1]), x.dtype)], axis=0)


def _select_rows(pc_sum, cov_ref, sel_ref, t0, ns, n_top, nq):
    Q = pc_sum.shape[0]
    ns_pad = cov_ref.shape[1]
    nk = -(-ns // 8) * 8
    imp = jnp.dot(pc_sum, cov_ref[...], precision=HI, preferred_element_type=F32)
    jj = lax.broadcasted_iota(jnp.int32, (Q, ns_pad), 1)
    tt = lax.broadcasted_iota(jnp.int32, (Q, ns_pad), 0) + t0
    cur = tt // SEL_BLOCK
    forced = (jj == 0) | (jj == cur) | (jj == cur - 1)
    future = jj * SEL_BLOCK > tt
    score = jnp.where(future, NEG, jnp.where(forced, -NEG, imp))
    score = jnp.where(jj < ns, score, -jnp.inf)
    score_col = _pad_rows(score, LANES).T
    kk = lax.broadcasted_iota(jnp.int32, (nk, ns_pad), 0)
    jl = lax.broadcasted_iota(jnp.int32, (nk, ns_pad), 1)
    sel_ref[...] = jnp.zeros(sel_ref.shape, F32)
    for i in range(nq):
        col = jnp.broadcast_to(score_col[0:nk, i:i + 1], (nk, ns_pad))
        row = jnp.broadcast_to(score[i:i + 1, :], (nk, ns_pad))
        beats = (col > row) | ((col == row) & (kk < jl))
        rank = jnp.sum(jnp.where(beats, 1.0, 0.0), axis=0, keepdims=True)
        sel_ref[i:i + 1, :] = jnp.where(rank < n_top, 1.0, 0.0)
    return sel_ref[...]


def _compress(src_ref, nrow, wk_ref, wv_ref, pbuf_ref, kc_ref, vc_ref):
    acck = jnp.zeros((nrow, 256), F32)
    accv = jnp.zeros((nrow, 256), F32)
    for j in range(CMP_STRIDE):
        xk = src_ref[0, pl.ds(j, nrow, stride=CMP_STRIDE), :].astype(BF16)
        xv = src_ref[1, pl.ds(j, nrow, stride=CMP_STRIDE), :].astype(BF16)
        acck = acck + jnp.dot(xk, wk_ref[j], preferred_element_type=F32)
        accv = accv + jnp.dot(xv, wv_ref[j], preferred_element_type=F32)
    nout = kc_ref.shape[0]
    for acc, dst in ((acck, kc_ref), (accv, vc_ref)):
        pbuf_ref[0:nrow, :] = acc[:, 128:256]
        dst[...] = acc[0:nout, 0:128] + pbuf_ref[pl.ds(1, nout), :]


def _gated_sum(gates, gx_ref, ocat_ref):
    gh, gl = _split(gates)
    gexp = (jnp.dot(gh, gx_ref[...], preferred_element_type=F32)
            + jnp.dot(gl, gx_ref[...], preferred_element_type=F32))
    out = gexp[:, 0:W_A] * ocat_ref[0]
    for c in range(1, N_BRANCH):
        out = out + gexp[:, c * W_A:(c + 1) * W_A] * ocat_ref[c]
    return out


def _nsa_prompt_kernel(q_ref, rows_ref, small_ref, wk_ref, wv_ref, bcmp_ref, tsel_ref, covt_ref, e3_ref,
                       wm_ref, cm_ref, gx_ref,
                       o_ref, kc_ref, vc_ref, pbuf_ref, sel_ref, acc_ref, m_ref, st_ref, ocat_ref,
                       *, ns, n_top):
    bi = pl.program_id(1)
    T = rows_ref.shape[1]
    nch = T // CMP_STRIDE
    QB = Q_BLOCK
    SEL, WIN = 1, 2

    @pl.when(bi == 0)
    def _():
        pbuf_ref[...] = jnp.zeros(pbuf_ref.shape, F32)
        _compress(rows_ref, nch, wk_ref, wv_ref, pbuf_ref, kc_ref, vc_ref)

    t0 = bi * QB
    q = q_ref[0]
    tc = lax.broadcasted_iota(jnp.int32, (QB, nch), 0) + t0
    cend = lax.broadcasted_iota(jnp.int32, (QB, nch), 1) * CMP_STRIDE + (CMP_BLOCK - 1)
    cvalid = _rep_rows(tc - cend >= 0)
    t_row = lax.broadcasted_iota(jnp.int32, (1, LANES), 1) + t0
    lane = lax.broadcasted_iota(jnp.int32, (QB, LANES), 1)
    in_g = [(lane >= g * A_HD) & (lane < (g + 1) * A_HD) for g in range(A_KV)]
    qg = [_stack_heads(q, g).astype(BF16) for g in range(A_KV)]

    def put_heads(c, g, o):
        for r in range(A_REP):
            h = A_REP * g + r
            ocat_ref[c, :, h * A_HD:(h + 1) * A_HD] = o[r * QB:(r + 1) * QB]

    for g in range(A_KV):
        kc = kc_ref[:, g * A_HD:(g + 1) * A_HD].astype(BF16)
        vc = vc_ref[:, g * A_HD:(g + 1) * A_HD].astype(BF16)
        s = lax.dot_general(qg[g], kc, NT, preferred_element_type=F32) + bcmp_ref[0, g]
        p_c = _masked_softmax(s, cvalid)
        put_heads(0, g, jnp.dot(p_c.astype(BF16), vc, preferred_element_type=F32))
        pc_sum = p_c[0:QB] + p_c[QB:2 * QB] + p_c[2 * QB:3 * QB] + p_c[3 * QB:4 * QB]
        sel_t = _select_blocks(pc_sum, covt_ref, st_ref, t_row, ns, n_top)
        sel_ref[g] = _pad_rows(sel_t, LANES).T.astype(BF16)

    m_ref[...] = jnp.full(m_ref.shape, NEG, F32)
    acc_ref[...] = jnp.zeros(acc_ref.shape, F32)

    def tile(br, g, kb, nb=1):
        ci = 2 * (br - 1) + g
        ks, vs, bs, ms = [], [], [], []
        for j in range(nb):
            off = pl.multiple_of((kb + j) * QB, QB)
            ks.append(rows_ref[2 * br, pl.ds(off, QB), g * A_HD:(g + 1) * A_HD])
            vs.append(jnp.where(in_g[g], rows_ref[2 * br + 1, pl.ds(off, QB), :], 1.0))
            bs.append(tsel_ref[bi - kb - j, g])
            if br == SEL:
                mt = jnp.dot(sel_ref[g], e3_ref[kb + j], preferred_element_type=F32)
                ms.append(mt * cm_ref[jnp.minimum(bi - kb, 1)] if nb == 1 else mt)
            else:
                ms.append(wm_ref[bi - kb - j])
        cat = lambda xs, ax: xs[0] if nb == 1 else jnp.concatenate(xs, axis=ax)
        k = cat(ks, 0).astype(BF16)
        v1 = cat(vs, 0).astype(BF16)
        sc = lax.dot_general(qg[g], k, NT, preferred_element_type=F32) + cat(bs, 1)
        valid = _rep_rows(cat(ms, 1)) > 0.5
        sm = jnp.where(valid, sc, NEG)
        m_prev = m_ref[ci]
        m_new = jnp.maximum(m_prev, jnp.max(sm, axis=-1, keepdims=True))
        p = jnp.where(valid, jnp.exp(sm - cat([m_new] * nb, 1)), 0.0)
        acc_ref[ci] = jnp.exp(m_prev - m_new) * acc_ref[ci] + jnp.dot(p.astype(BF16), v1,
                                                                       preferred_element_type=F32)
        m_ref[ci] = m_new

    def body_sel(kb, carry):
        for g in range(A_KV):
            tile(SEL, g, kb)
        return carry

    def body_sel2(pair, carry):
        for g in range(A_KV):
            tile(SEL, g, 2 * pair, nb=2)
        return carry

    def body_both(kb, carry):
        for g in range(A_KV):
            tile(SEL, g, kb)
            tile(WIN, g, kb)
        return carry

    lo = jnp.maximum(bi - WINDOW // QB, 0)
    lax.fori_loop(0, lo // 2, body_sel2, 0)
    lax.fori_loop(2 * (lo // 2), lo, body_sel, 0)
    lax.fori_loop(lo, bi + 1, body_both, 0)

    for br in (SEL, WIN):
        for g in range(A_KV):
            acc = acc_ref[2 * (br - 1) + g]
            linv = pltpu.roll(1.0 / jnp.maximum(acc, TINY), A_HD, axis=1)
            put_heads(br, g, (acc * linv)[:, g * A_HD:(g + 1) * A_HD])

    o_ref[0] = _gated_sum(jax.nn.sigmoid(small_ref[0]), gx_ref, ocat_ref)


def _nsa_prompt(qn3, rows6, u3, wk, wv, tb):
    B, T, _ = qn3.shape
    QB = Q_BLOCK
    nqb = T // QB
    nch = T // CMP_STRIDE
    ns = T // SEL_BLOCK
    ns_pad = tb['covt'].shape[0]
    n_top = min(TOP_N, ns)
    nw = tb['wm'].shape[0]
    kern = functools.partial(_nsa_prompt_kernel, ns=ns, n_top=n_top)
    c2 = lambda shape: pl.BlockSpec(shape, lambda b, i: (0,) * len(shape))
    return pl.pallas_call(
        kern,
        grid=(B, nqb),
        in_specs=[pl.BlockSpec((1, QB, W_A), lambda b, i: (b, i, 0)),
                  pl.BlockSpec((6, T, 128), lambda b, i: (0, b, 0)),
                  pl.BlockSpec((1, QB, LANES), lambda b, i: (b, i, OFF['small'] // LANES)),
                  c2((CMP_STRIDE, 128, 256)), c2((CMP_STRIDE, 128, 256)),
                  pl.BlockSpec((1, A_KV, A_REP * QB, nch), lambda b, i: (i, 0, 0, 0)),
                  c2((nqb, A_KV, A_REP * QB, QB)),
                  c2((ns_pad, nch)), c2((nqb, LANES, QB)), c2((nw, QB, QB)), c2((2, QB, QB)),
                  c2((LANES, N_BRANCH * W_A))],
        out_specs=pl.BlockSpec((1, QB, W_A), lambda b, i: (b, i, 0)),
        out_shape=jax.ShapeDtypeStruct((B, T, W_A), F32),
        scratch_shapes=[pltpu.VMEM((nch, 128), F32), pltpu.VMEM((nch, 128), F32),
                        pltpu.VMEM((nch + 8, 128), F32),
                        pltpu.VMEM((A_KV, QB, LANES), BF16),
                        pltpu.VMEM((2 * A_KV, A_REP * QB, LANES), F32),
                        pltpu.VMEM((2 * A_KV, A_REP * QB, LANES), F32),
                        pltpu.VMEM((ns_pad, LANES), F32),
                        pltpu.VMEM((N_BRANCH, QB, W_A), F32)],
        compiler_params=_cparams(("parallel", "arbitrary")),
        name="nsa_prompt",
    )(qn3, rows6, u3, wk, wv, tb['bcmp'], tb['tsel'], tb['covt'], tb['e3'], tb['wm'], tb['cm'], tb['gx'])


def _nsa_sample_kernel(pt_ref, *refs, pg, past, tp, t_real, ns, n_top):
    cmp_refs = refs[0:pg]
    sel_refs = refs[pg:2 * pg]
    (q_ref, rows_ref, small_ref, win_ref, wk_ref, wv_ref, bcmp_ref, bsel_ref, bwin_ref, cov_ref, e_ref,
     o_ref, cslab, sslab, wslab, kc_ref, vc_ref, pbuf_ref, st_ref, s_ref) = refs[2 * pg:]
    p = pl.program_id(1)
    npg = pl.num_programs(1)
    SL = cslab.shape[1]
    nblk = kc_ref.shape[0]
    WK = wslab.shape[1]

    for k in range(pg):
        off = pl.multiple_of((p * pg + k) * PAGE_SIZE, PAGE_SIZE)
        for kv in range(2):
            cslab[kv, pl.ds(off, PAGE_SIZE), :] = cmp_refs[k][:, kv * 128:(kv + 1) * 128]
            sslab[kv, pl.ds(off, PAGE_SIZE), :] = sel_refs[k][:, kv * 128:(kv + 1) * 128]

    @pl.when(p == npg - 1)
    def _():
        tail = jnp.zeros((SL - past - tp, 128), F32)
        for kv in range(2):
            cslab[kv, past:past + tp, :] = rows_ref[kv]
            cslab[kv, past + tp:SL, :] = tail
            sslab[kv, past:past + tp, :] = rows_ref[2 + kv]
            sslab[kv, past + tp:SL, :] = tail
            wslab[kv, 0:WINDOW, :] = win_ref[0, :, kv * 128:(kv + 1) * 128]
            wslab[kv, WINDOW:WINDOW + tp, :] = rows_ref[4 + kv]
            wslab[kv, WINDOW + tp:WK, :] = jnp.zeros((WK - WINDOW - tp, 128), F32)
        pbuf_ref[...] = jnp.zeros(pbuf_ref.shape, F32)
        _compress(cslab, nblk + 8, wk_ref, wv_ref, pbuf_ref, kc_ref, vc_ref)

        q = q_ref[0]
        gates = jax.nn.sigmoid(small_ref[0])
        R = A_REP * tp
        ti = lax.broadcasted_iota(jnp.int32, (tp, SL), 0) + past
        causal = ti - lax.broadcasted_iota(jnp.int32, (tp, SL), 1) >= 0
        tc = lax.broadcasted_iota(jnp.int32, (tp, nblk), 0) + past
        cend = lax.broadcasted_iota(jnp.int32, (tp, nblk), 1) * CMP_STRIDE + (CMP_BLOCK - 1)
        cvalid = _rep_rows(tc - cend >= 0)
        wd = (lax.broadcasted_iota(jnp.int32, (tp, WK), 0) + WINDOW
              - lax.broadcasted_iota(jnp.int32, (tp, WK), 1))
        wvalid = _rep_rows((wd >= 0) & (wd < WINDOW))
        nck = 5 if (SL // 5) % LANES == 0 else SL // LANES
        ck = SL // nck

        for g in range(A_KV):
            gs = slice(g * A_HD, (g + 1) * A_HD)
            qg = _stack_heads(q, g).astype(BF16)
            s = lax.dot_general(qg, kc_ref[:, gs].astype(BF16), NT, preferred_element_type=F32) + bcmp_ref[g]
            p_c = _masked_softmax(s, cvalid)
            o_c = jnp.dot(p_c.astype(BF16), vc_ref[:, gs].astype(BF16), preferred_element_type=F32)
            pc_sum = p_c[0:tp] + p_c[tp:2 * tp] + p_c[2 * tp:3 * tp] + p_c[3 * tp:4 * tp]

            sel = _select_rows(pc_sum, cov_ref, st_ref, past, ns, n_top, t_real).astype(BF16)
            kmask = jnp.dot(sel, e_ref[...], preferred_element_type=F32)
            svalid = _rep_rows(causal & (kmask > 0.5))

            for c in range(nck):
                kk = sslab[0, c * ck:(c + 1) * ck, gs].astype(BF16)
                s_ref[:, c * ck:(c + 1) * ck] = lax.dot_general(qg, kk, NT, preferred_element_type=F32)
            p_s = _masked_softmax(s_ref[...] + bsel_ref[g], svalid)
            s_ref[...] = p_s
            o_s = jnp.zeros((R, A_HD), F32)
            for c in range(nck):
                vv = sslab[1, c * ck:(c + 1) * ck, gs].astype(BF16)
                o_s = o_s + jnp.dot(s_ref[:, c * ck:(c + 1) * ck].astype(BF16), vv, preferred_element_type=F32)

            kw = wslab[0, :, gs].astype(BF16)
            vw = wslab[1, :, gs].astype(BF16)
            sw = lax.dot_general(qg, kw, NT, preferred_element_type=F32) + bwin_ref[g]
            p_w = _masked_softmax(sw, wvalid)
            o_w = jnp.dot(p_w.astype(BF16), vw, preferred_element_type=F32)

            for r in range(A_REP):
                h = A_REP * g + r
                rs = slice(r * tp, (r + 1) * tp)
                out = (gates[:, 3 * h:3 * h + 1] * o_c[rs] + gates[:, 3 * h + 1:3 * h + 2] * o_s[rs]
                       + gates[:, 3 * h + 2:3 * h + 3] * o_w[rs])
                o_ref[0, :, h * A_HD:(h + 1) * A_HD] = out


def _nsa_sample(qn3, rows6, u3, pool_cmp, pool_sel, win3, page_table, wk, wv, tb, past, t_real):
    B, tp, _ = qn3.shape
    n_pages = past // PAGE_SIZE
    pg = 8 if n_pages % 8 == 0 else n_pages
    npg = n_pages // pg
    SL = past + LANES
    nblk = past // CMP_STRIDE
    ns = -(-(past + t_real) // SEL_BLOCK)
    ns_pad = tb['cov'].shape[1]
    n_top = min(TOP_N, ns)
    WK = WINDOW + LANES
    kern = functools.partial(_nsa_sample_kernel, pg=pg, past=past, tp=tp, t_real=t_real, ns=ns, n_top=n_top)

    def page_spec(k):
        return pl.BlockSpec((None, PAGE_SIZE, 256), lambda b, p, pt: (pt[b, p * pg + k], 0, 0))

    def c_(shape):
        return pl.BlockSpec(shape, lambda b, p, pt: (0,) * len(shape))

    R = A_REP * tp
    grid_spec = pltpu.PrefetchScalarGridSpec(
        num_scalar_prefetch=1,
        grid=(B, npg),
        in_specs=([page_spec(k) for k in range(pg)] + [page_spec(k) for k in range(pg)]
                  + [pl.BlockSpec((1, tp, W_A), lambda b, p, pt: (b, 0, 0)),
                     pl.BlockSpec((6, tp, 128), lambda b, p, pt: (0, b, 0)),
                     pl.BlockSpec((1, tp, LANES), lambda b, p, pt: (b, 0, OFF['small'] // LANES)),
                     pl.BlockSpec((1, WINDOW, 256), lambda b, p, pt: (b, 0, 0)),
                     c_((CMP_STRIDE, 128, 256)), c_((CMP_STRIDE, 128, 256)),
                     c_((A_KV, R, nblk)), c_((A_KV, R, SL)), c_((A_KV, R, WK)),
                     c_((nblk, ns_pad)), c_((ns_pad, SL))]),
        out_specs=pl.BlockSpec((1, tp, W_A), lambda b, p, pt: (b, 0, 0)),
        scratch_shapes=[pltpu.VMEM((2, SL, 128), F32), pltpu.VMEM((2, SL, 128), F32),
                        pltpu.VMEM((2, WK, 128), F32),
                        pltpu.VMEM((nblk, 128), F32), pltpu.VMEM((nblk, 128), F32),
                        pltpu.VMEM((nblk + 16, 128), F32),
                        pltpu.VMEM((tp, ns_pad), F32), pltpu.VMEM((R, SL), F32)],
    )
    return pl.pallas_call(
        kern,
        grid_spec=grid_spec,
        out_shape=jax.ShapeDtypeStruct((B, tp, W_A), F32),
        compiler_params=_cparams(("parallel", "arbitrary")),
        name="nsa_sample",
    )(page_table, *([pool_cmp] * pg), *([pool_sel] * pg), qn3, rows6, u3, win3, wk, wv,
      tb['bcmp'], tb['bsel'], tb['bwin'], tb['cov'], tb['emat'])


def _tri(L, lower_incl):
    r = lax.broadcasted_iota(jnp.int32, (L, L), 0)
    c = lax.broadcasted_iota(jnp.int32, (L, L), 1)
    return (r >= c) if lower_incl else (r > c)


def _mlstm_kernel(x_ref, small_ref, smallt_ref, brow_ref, bcol_ref, c0_ref, n0_ref, m0_ref,
                  h_ref, co_ref, no_ref, mo_ref, c_s, n_s, m_s, *, L, t_real):
    c = pl.program_id(1)
    nc = pl.num_programs(1)

    @pl.when(c == 0)
    def _():
        c_s[...] = c0_ref[0]
        n_s[...] = n0_ref[0]
        m_s[...] = m0_ref[0]

    sm = small_ref[0]
    smt = smallt_ref[0, 0]
    li_col = sm[:, SM_MIF:SM_MIF + 4] + brow_ref[0:1, 0:4]
    lf_col = _log_sigmoid(sm[:, SM_MIF + 4:SM_MIF + 8] + brow_ref[0:1, 4:8])
    li_row = smt[0:4, :] + bcol_ref[0:4, :]
    lf_row = _log_sigmoid(smt[4:8, :] + bcol_ref[4:8, :])
    if t_real % L:
        tcol = lax.broadcasted_iota(jnp.int32, (L, 4), 0) + c * L
        trow = lax.broadcasted_iota(jnp.int32, (4, L), 1) + c * L
        li_col = jnp.where(tcol < t_real, li_col, NEG)
        lf_col = jnp.where(tcol < t_real, lf_col, 0.0)
        li_row = jnp.where(trow < t_real, li_row, NEG)
        lf_row = jnp.where(trow < t_real, lf_row, 0.0)
    low = _tri(L, True)
    b_col = jnp.dot(low.astype(F32), lf_col, precision=HI, preferred_element_type=F32)
    b_row = lax.dot_general(lf_row, low.astype(F32), NT, precision=HI, preferred_element_type=F32)

    for h in range(M_HEADS):
        hs = slice(h * M_HD, (h + 1) * M_HD)
        q = x_ref[0, :, h * M_HD:(h + 1) * M_HD]
        k = x_ref[0, :, W_M + h * M_HD:W_M + (h + 1) * M_HD] * (M_HD ** -0.5)
        v = x_ref[0, :, 2 * W_M + h * M_HD:2 * W_M + (h + 1) * M_HD]
        bc = b_col[:, h:h + 1]
        br = b_row[h:h + 1, :]
        Dm = jnp.where(low, bc - br + li_row[h:h + 1, :], NEG)
        m_prev = m_s[h][:, 0:1]
        a = bc + m_prev
        mt = jnp.maximum(a, jnp.max(Dm, axis=-1, keepdims=True))
        S = lax.dot_general(q, k, NT, preferred_element_type=F32) * jnp.exp(Dm - mt)
        inter = jnp.exp(a - mt)
        C = c_s[h]
        n = n_s[h]
        num = inter * jnp.dot(q, C, preferred_element_type=F32) + jnp.dot(S, v, preferred_element_type=F32)
        den = inter * jnp.sum(q * n, axis=-1, keepdims=True) + jnp.sum(S, axis=-1, keepdims=True)
        h_ref[0, :, hs] = num / jnp.maximum(jnp.abs(den), jnp.exp(-mt))
        bL = bc[L - 1:L, :]
        wlog = bL - bc + li_col[:, h:h + 1]
        m_new = jnp.maximum(bL + m_prev, jnp.max(wlog, axis=0, keepdims=True))
        ws = jnp.exp(wlog - m_new)
        dec = jnp.exp(bL + m_prev - m_new)
        kw = k * ws
        c_s[h] = dec * C + lax.dot_general(kw, v, TN, preferred_element_type=F32)
        n_s[h] = dec * n + jnp.sum(kw, axis=0, keepdims=True)
        m_s[h] = jnp.broadcast_to(m_new, (1, LANES))

    @pl.when(c == nc - 1)
    def _():
        co_ref[0] = c_s[...]
        no_ref[0] = n_s[...]
        mo_ref[0] = m_s[...]


def _mlstm(u3, smallt, m_bi, m_bf, C0, n0, m0, L, t_real):
    B, T, _ = u3.shape
    nc = T // L
    brow = jnp.concatenate([m_bi, m_bf]).reshape(1, 8)
    bcol = jnp.concatenate([m_bi, m_bf]).reshape(8, 1)
    n0 = n0.reshape(B, M_HEADS, 1, M_HD)
    m0 = jnp.broadcast_to(m0[:, :, None, None], (B, M_HEADS, 1, LANES))
    kern = functools.partial(_mlstm_kernel, L=L, t_real=t_real)
    st = lambda shape: pl.BlockSpec(shape, lambda b, c: (b,) + (0,) * (len(shape) - 1))
    h, Cn, nn, mn = pl.pallas_call(
        kern,
        grid=(B, nc),
        in_specs=[pl.BlockSpec((1, L, 3 * W_M), lambda b, c: (b, c, OFF['m_qkv'] // (3 * W_M))),
                  pl.BlockSpec((1, L, LANES), lambda b, c: (b, c, OFF['small'] // LANES)),
                  pl.BlockSpec((1, 1, 16, L), lambda b, c: (b, c, 0, 0)),
                  pl.BlockSpec((1, 8), lambda b, c: (0, 0)), pl.BlockSpec((8, 1), lambda b, c: (0, 0)),
                  st((1, M_HEADS, M_HD, M_HD)), st((1, M_HEADS, 1, M_HD)), st((1, M_HEADS, 1, LANES))],
        out_specs=[pl.BlockSpec((1, L, W_M), lambda b, c: (b, c, 0)),
                   st((1, M_HEADS, M_HD, M_HD)), st((1, M_HEADS, 1, M_HD)), st((1, M_HEADS, 1, LANES))],
        out_shape=[jax.ShapeDtypeStruct((B, T, W_M), F32),
                   jax.ShapeDtypeStruct((B, M_HEADS, M_HD, M_HD), F32),
                   jax.ShapeDtypeStruct((B, M_HEADS, 1, M_HD), F32),
                   jax.ShapeDtypeStruct((B, M_HEADS, 1, LANES), F32)],
        scratch_shapes=[pltpu.VMEM((M_HEADS, M_HD, M_HD), F32), pltpu.VMEM((M_HEADS, 1, M_HD), F32),
                        pltpu.VMEM((M_HEADS, 1, LANES), F32)],
        compiler_params=_cparams(("parallel", "arbitrary")),
        name="mlstm",
    )(u3, u3, smallt, brow, bcol, C0, n0, m0)
    return h, Cn, nn.reshape(B, M_HEADS, M_HD), mn[:, :, 0, 0]


def _gdn_a_kernel(x_ref, small_ref, smallt_ref, cw_ref, prow_ref, pcol_ref, buf_ref,
                  u_ref, w_ref, qe_ref, kd_ref, at_ref, eg_ref, xbuf, *, L, CB, t_real):
    c = pl.program_id(1)
    PRE = 8
    R = CB * L

    @pl.when(c == 0)
    def _():
        xbuf[0:PRE, :] = buf_ref[0]

    xbuf[PRE:PRE + R, :] = x_ref[0]
    conv = jnp.zeros((R, 3 * W_G), F32)
    for j in range(CONV_W):
        conv = conv + xbuf[pl.ds(PRE - (CONV_W - 1) + j, R), :] * cw_ref[j:j + 1, :]
    xbuf[0:PRE, :] = xbuf[R:R + PRE, :]
    conv = _silu(conv)

    sm = small_ref[0]
    g_all = -jnp.exp(prow_ref[0:1, 0:4]) * _softplus(sm[:, SM_GAB:SM_GAB + 4] + prow_ref[1:2, 0:4])
    beta_all = jax.nn.sigmoid(sm[:, SM_GAB + 4:SM_GAB + 8])
    low = _tri(L, True)
    strict = _tri(L, False)
    eye = (low & ~strict).astype(F32)
    lowf = low.astype(F32)

    for cb in range(CB):
        rs = slice(cb * L, (cb + 1) * L)
        smt = smallt_ref[0, cb]
        g_col = g_all[rs]
        beta_col = beta_all[rs]
        g_row = -jnp.exp(pcol_ref[0:4, 0:1]) * _softplus(smt[8:12, :] + pcol_ref[0:4, 1:2])
        if t_real % L:
            t0 = (c * CB + cb) * L
            tcol = lax.broadcasted_iota(jnp.int32, (L, 4), 0) + t0
            trow = lax.broadcasted_iota(jnp.int32, (4, L), 1) + t0
            g_col = jnp.where(tcol < t_real, g_col, 0.0)
            beta_col = jnp.where(tcol < t_real, beta_col, 0.0)
            g_row = jnp.where(trow < t_real, g_row, 0.0)
        G_col = jnp.dot(lowf, g_col, precision=HI, preferred_element_type=F32)
        G_row = lax.dot_general(g_row, lowf, NT, precision=HI, preferred_element_type=F32)

        for h in range(G_HEADS):
            hs = slice(h * G_HD, (h + 1) * G_HD)
            cq = conv[rs, h * G_HD:(h + 1) * G_HD]
            ck = conv[rs, W_G + h * G_HD:W_G + (h + 1) * G_HD]
            v = conv[rs, 2 * W_G + h * G_HD:2 * W_G + (h + 1) * G_HD]
            q = cq * lax.rsqrt(jnp.sum(cq * cq, axis=-1, keepdims=True) + EPS) * (G_HD ** -0.5)
            k = ck * lax.rsqrt(jnp.sum(ck * ck, axis=-1, keepdims=True) + EPS)
            Gc = G_col[:, h:h + 1]
            Gr = G_row[h:h + 1, :]
            dmask = jnp.where(low, jnp.exp(jnp.where(low, Gc - Gr, 0.0)), 0.0)
            bcol = beta_col[:, h:h + 1]
            kb = k * bcol
            ks = _split(k)
            A = _dot3(_split(kb), ks, NT) * jnp.where(strict, dmask, 0.0)
            eG = jnp.exp(Gc)
            rhs = jnp.concatenate([v * bcol, kb * eG], axis=1)
            X = eye - A
            As = _split(A)
            Pw = _dot3(As, As)
            span = 2
            while span < L:
                Ps = _split(Pw)
                X = X + _dot3(_split(X), Ps)
                span *= 2
                if span < L:
                    Pw = _dot3(Ps, Ps)
            sol = _dot3(_split(X), _split(rhs))
            GL = Gc[L - 1:L, :]
            u_ref[0, rs, hs] = sol[:, 0:G_HD]
            w_ref[0, rs, hs] = sol[:, G_HD:2 * G_HD]
            qe_ref[0, rs, hs] = q * eG
            kd_ref[0, rs, hs] = k * jnp.exp(GL - Gc)
            at_ref[0, rs, h * L:(h + 1) * L] = lax.dot_general(q, k, NT, preferred_element_type=F32) * dmask
            eg_ref[0, cb, h:h + 1, :] = jnp.broadcast_to(jnp.exp(GL), (1, LANES))
        eg_ref[0, cb, G_HEADS:8, :] = jnp.zeros((8 - G_HEADS, LANES), F32)


def _gdn_b_kernel(u_ref, w_ref, qe_ref, kd_ref, at_ref, eg_ref, s0_ref, o_ref, so_ref, s_s, *, L, BB):
    c = pl.program_id(1)
    nc = pl.num_programs(1)

    @pl.when(c == 0)
    def _():
        s_s[...] = s0_ref[...]

    for b in range(BB):
        for h in range(G_HEADS):
            hs = slice(h * G_HD, (h + 1) * G_HD)
            S = s_s[b, h]
            wq = jnp.concatenate([w_ref[b, :, hs], qe_ref[b, :, hs]], axis=0)
            r = jnp.dot(wq, S, preferred_element_type=F32)
            v_new = u_ref[b, :, hs] - r[0:L]
            o_ref[b, :, hs] = r[L:2 * L] + jnp.dot(at_ref[b, :, h * L:(h + 1) * L], v_new,
                                                   preferred_element_type=F32)
            s_s[b, h] = eg_ref[b, 0, h:h + 1, :] * S + lax.dot_general(kd_ref[b, :, hs], v_new, TN,
                                                                      preferred_element_type=F32)

    @pl.when(c == nc - 1)
    def _():
        so_ref[...] = s_s[...]


def _gdn(u3, smallt, g_conv, g_A_log, g_dt_bias, buf, S0, L, t_real):
    B, T, _ = u3.shape
    nc = T // L
    CB = 2 if nc % 2 == 0 else 1
    BB = 8 if B % 8 == 0 else B
    prow = jnp.stack([g_A_log, g_dt_bias])
    pcol = jnp.stack([g_A_log, g_dt_bias], axis=1)
    buf8 = jnp.concatenate([jnp.zeros((B, 8 - (CONV_W - 1), 3 * W_G), F32), buf], axis=1)
    st = lambda shape: pl.BlockSpec(shape, lambda b, c: (b,) + (0,) * (len(shape) - 1))
    cst = lambda shape: pl.BlockSpec(shape, lambda b, c: (0,) * len(shape))
    row = lambda nb, r, w: pl.BlockSpec((nb, r, w), lambda b, c: (b, c, 0))
    tok = lambda w: jax.ShapeDtypeStruct((B, T, w), F32)
    u, w, qe, kd, at, eg = pl.pallas_call(
        functools.partial(_gdn_a_kernel, L=L, CB=CB, t_real=t_real),
        grid=(B, nc // CB),
        in_specs=[pl.BlockSpec((1, CB * L, 3 * W_G), lambda b, c: (b, c, OFF['g_qkv'] // (3 * W_G))),
                  pl.BlockSpec((1, CB * L, LANES), lambda b, c: (b, c, OFF['small'] // LANES)),
                  pl.BlockSpec((1, CB, 16, L), lambda b, c: (b, c, 0, 0)),
                  cst((CONV_W, 3 * W_G)), cst((2, 4)), cst((4, 2)), st((1, 8, 3 * W_G))],
        out_specs=[row(1, CB * L, W_G)] * 4 + [row(1, CB * L, G_HEADS * L),
                                               pl.BlockSpec((1, CB, 8, LANES), lambda b, c: (b, c, 0, 0))],
        out_shape=[tok(W_G)] * 4 + [tok(G_HEADS * L), jax.ShapeDtypeStruct((B, nc, 8, LANES), F32)],
        scratch_shapes=[pltpu.VMEM((CB * L + 8, 3 * W_G), F32)],
        compiler_params=_cparams(("parallel", "arbitrary")),
        name="gdn_a",
    )(u3, u3, smallt, g_conv, prow, pcol, buf8)
    sblk = pl.BlockSpec((BB, G_HEADS, G_HD, G_HD), lambda b, c: (b, 0, 0, 0))
    return pl.pallas_call(
        functools.partial(_gdn_b_kernel, L=L, BB=BB),
        grid=(B // BB, nc),
        in_specs=[row(BB, L, W_G)] * 4 + [row(BB, L, G_HEADS * L),
                                          pl.BlockSpec((BB, 1, 8, LANES), lambda b, c: (b, c, 0, 0)), sblk],
        out_specs=[row(BB, L, W_G), sblk],
        out_shape=[tok(W_G), jax.ShapeDtypeStruct((B, G_HEADS, G_HD, G_HD), F32)],
        scratch_shapes=[pltpu.VMEM((BB, G_HEADS, G_HD, G_HD), F32)],
        compiler_params=_cparams(("parallel", "arbitrary")),
        name="gdn_b",
    )(u, w, qe, kd, at, eg, S0)


def _head_rmsnorm(x, gain_ref, nheads, hd):
    outs = []
    for h in range(nheads):
        xh = x[:, h * hd:(h + 1) * hd]
        ms = jnp.mean(xh * xh, axis=-1, keepdims=True)
        outs.append(xh * lax.rsqrt(ms + EPS) * gain_ref[...])
    return jnp.concatenate(outs, axis=1)


def _merge_kernel(x_ref, oa_ref, hm_ref, og_ref, az_ref, mo_ref, mz_ref, gz_ref, mg_ref,
                  wb_ref, wo_ref, mhn_ref, ghn_ref, y_ref):
    oa = oa_ref[...] * _silu(az_ref[...])
    om = _head_rmsnorm(hm_ref[...], mhn_ref, M_HEADS, M_HD) * jax.nn.sigmoid(mo_ref[...]) * _silu(mz_ref[...])
    og = _head_rmsnorm(og_ref[...], ghn_ref, G_HEADS, G_HD) * _silu(gz_ref[...])
    y = jnp.zeros(y_ref.shape, F32)
    for i, br in enumerate((oa, om, og)):
        proj = jnp.dot(br.astype(BF16), wb_ref[i], preferred_element_type=F32)
        y = y + jax.nn.sigmoid(mg_ref[:, i * D_MODEL:(i + 1) * D_MODEL]) * proj
    y_ref[...] = x_ref[...] + jnp.dot(y.astype(BF16), wo_ref[...], preferred_element_type=F32)


def _merge_out(x2, o_a, h_m, o_g, u2, w_branch, w_out, m_hn, g_hn):
    n = x2.shape[0]
    tm = min(512, n)
    row = lambda w, off: pl.BlockSpec((tm, w), lambda i: (i, off // w))
    cst = lambda shape: pl.BlockSpec(shape, lambda i: (0,) * len(shape))
    return pl.pallas_call(
        _merge_kernel,
        grid=(n // tm,),
        in_specs=[row(D_MODEL, 0), row(W_A, 0), row(W_M, 0), row(W_G, 0),
                  row(W_A, OFF['a_z']), row(W_M, OFF['m_o']), row(W_M, OFF['m_z']), row(W_G, OFF['g_z']),
                  row(N_BRANCH * D_MODEL, OFF['merge']),
                  cst((N_BRANCH, W_A, D_MODEL)), cst((D_MODEL, D_MODEL)), cst((1, M_HD)), cst((1, G_HD))],
        out_specs=row(D_MODEL, 0),
        out_shape=jax.ShapeDtypeStruct((n, D_MODEL), F32),
        compiler_params=_cparams(("parallel",)),
        name="merge_out",
    )(x2, o_a, h_m, o_g, u2, u2, u2, u2, u2, w_branch.astype(BF16), w_out.astype(BF16),
      m_hn.reshape(1, M_HD), g_hn.reshape(1, G_HD))


def _small_t(u3, L):
    B, T, _ = u3.shape
    s = u3[:, :, OFF['small'] + SM_MIF:OFF['small'] + SM_MIF + 16]
    return s.reshape(B, T // L, L, 16).transpose(0, 1, 3, 2)


def _layer(x, lw, tb, past, q_off):
    B, T, _ = x.shape
    x2 = x.reshape(B * T, D_MODEL)
    u2 = _proj_in(x2, lw['norm_g'], lw['w_perm'])
    if past is None:
        tp, L = T, math.gcd(T, M_CHUNK)
        u3 = u2.reshape(B, T, N_PERM)
        C0 = jnp.zeros((B, M_HEADS, M_HD, M_HD), F32)
        n0 = jnp.zeros((B, M_HEADS, M_HD), F32)
        m0 = jnp.zeros((B, M_HEADS), F32)
        S0 = jnp.zeros((B, G_HEADS, G_HD, G_HD), F32)
        buf = jnp.zeros((B, CONV_W - 1, 3 * W_G), F32)
    else:
        tp = -(-T // 8) * 8
        L = tp
        u3 = jnp.pad(u2.reshape(B, T, N_PERM), ((0, 0), (0, tp - T), (0, 0)))
        C0, n0, m0, S0, buf = past['mC'], past['mn'], past['mm'], past['gS'], past['gconv']
    up = u3.reshape(B * tp, N_PERM)
    qn, rows6 = _prep_a(up, lw['a_qn'], lw['a_kn'])
    qn3 = qn.reshape(B, tp, W_A)
    if past is None:
        o_a = _nsa_prompt(qn3, rows6, u3, lw['wk'], lw['wv'], tb)
    else:
        o_a = _nsa_sample(qn3, rows6, u3, past['cmp'], past['sel'], past['win'], past['page_table'],
                          lw['wk'], lw['wv'], tb, q_off, T)
    smallt = _small_t(u3, L)
    h_m, Cn, nn, mn = _mlstm(u3, smallt, lw['m_bi'], lw['m_bf'], C0, n0, m0, L, T)
    o_g, Sn = _gdn(u3, smallt, lw['g_conv'], lw['g_A_log'], lw['g_dt_bias'], buf, S0, L, T)
    if tp != T:
        o_a, h_m, o_g = o_a[:, :T], h_m[:, :T], o_g[:, :T]
    y2 = _merge_out(x2, o_a.reshape(B * T, W_A), h_m.reshape(B * T, W_M), o_g.reshape(B * T, W_G), u2,
                    lw['w_branch'], lw['w_out'], lw['m_hn'], lw['g_hn'])
    rows = rows6.reshape(3, 2, B, tp, A_KV, A_HD)[:, :, :, :T]
    new_cmp, new_sel, new_win = (jnp.moveaxis(rows[b], 0, 2) for b in range(3))
    g_qkv = u3[:, :T, OFF['g_qkv']:OFF['g_qkv'] + 3 * W_G]
    if past is None:
        win = new_win
    else:
        win = jnp.concatenate([past['win'].reshape(B, -1, 2, A_KV, A_HD), new_win], axis=1)
    full = jnp.concatenate([buf, g_qkv[:, max(T - (CONV_W - 1), 0):]], axis=1)
    state = dict(cmp=new_cmp, sel=new_sel, win=win[:, -min(WINDOW, win.shape[1]):],
                 mC=Cn, mn=nn, mm=mn, gS=Sn, gconv=full[:, -(CONV_W - 1):])
    return y2.reshape(B, T, D_MODEL), state


def kernel(x_prompt, x_sample, cache_cmp_kv, cache_sel_kv, cache_win_kv, state_mlstm_C, state_mlstm_n,
           state_mlstm_m, state_gdn_S, state_gdn_conv, page_table, norm_g, w_in, a_qn, a_kn, a_cmp_wk,
           a_cmp_wv, rel_bias, m_bi, m_bf, m_hn, g_conv, g_A_log, g_dt_bias, g_hn, w_branch, w_out):
    names = ('cmp', 'sel', 'win', 'mC', 'mn', 'mm', 'gS', 'gconv')
    st_p = {k: [] for k in names}
    st_s = {k: [] for k in names}
    past_len = page_table.shape[1] * PAGE_SIZE
    n_pool = cache_cmp_kv.shape[1]
    db, dt = x_sample.shape[0], x_sample.shape[1]
    tb_p = _prompt_tables(rel_bias, x_prompt.shape[1])
    tb_s = _sample_tables(rel_bias, past_len, -(-dt // 8) * 8, dt)
    y_p, y_s = x_prompt, x_sample
    for l in range(DEPTH):
        lw = dict(norm_g=norm_g[l], w_perm=_permute_w_in(w_in[l]), a_qn=a_qn[l], a_kn=a_kn[l],
                  wk=_cmp_weights(a_cmp_wk[l]), wv=_cmp_weights(a_cmp_wv[l]),
                  m_bi=m_bi[l], m_bf=m_bf[l], m_hn=m_hn[l], g_conv=g_conv[l], g_A_log=g_A_log[l],
                  g_dt_bias=g_dt_bias[l], g_hn=g_hn[l], w_branch=w_branch[l], w_out=w_out[l])
        y_p, new_p = _layer(y_p, lw, tb_p, None, 0)
        past = dict(cmp=cache_cmp_kv[l].reshape(n_pool, PAGE_SIZE, 256),
                    sel=cache_sel_kv[l].reshape(n_pool, PAGE_SIZE, 256),
                    win=cache_win_kv[l].reshape(db, -1, 256), page_table=page_table,
                    mC=state_mlstm_C[l], mn=state_mlstm_n[l], mm=state_mlstm_m[l],
                    gS=state_gdn_S[l], gconv=state_gdn_conv[l])
        y_s, new_s = _layer(y_s, lw, tb_s, past, past_len)
        for k in names:
            st_p[k].append(new_p[k])
            st_s[k].append(new_s[k])
    P = {k: jnp.stack(v) for k, v in st_p.items()}
    S = {k: jnp.stack(v) for k, v in st_s.items()}
    return (y_p, y_s, P['cmp'], S['cmp'], P['sel'], S['sel'], P['win'], S['win'],
            P['mC'], S['mC'], P['mn'], S['mn'], P['mm'], S['mm'], P['gS'], S['gS'], P['gconv'], S['gconv'])
```

```python
import functools
import math

import jax
import jax.numpy as jnp
from jax import lax
from jax.experimental import pallas as pl
from jax.experimental.pallas import tpu as pltpu

D_MODEL = 1024
DEPTH = 2
PAGE_SIZE = 128
A_HEADS = 8
A_KV = 2
A_REP = A_HEADS // A_KV
A_HD = 64
CMP_BLOCK = 32
CMP_STRIDE = 16
SEL_BLOCK = 64
TOP_N = 16
WINDOW = 512
Q_BLOCK = 128
N_BUCKETS = 32
MAX_DIST = 2048
M_HEADS = 4
M_HD = 128
M_CHUNK = 64
G_HEADS = 4
G_HD = 128
G_CHUNK = 64
CONV_W = 4
W_A = A_HEADS * A_HD
W_M = M_HEADS * M_HD
W_G = G_HEADS * G_HD
N_BRANCH = 3
EPS = 1e-6
NEG = -1e30
TINY = 1e-30

F32 = jnp.float32
BF16 = jnp.bfloat16
HI = lax.Precision.HIGHEST
NN = (((1,), (0,)), ((), ()))
NT = (((1,), (1,)), ((), ()))
TN = (((0,), (0,)), ((), ()))

LANES = 128
VMEM_LIMIT = 56 * 1024 * 1024

IN_ORDER = ('a_q', 'a_kv', 'a_gate', 'a_z', 'm_qkv', 'm_if', 'm_o', 'm_z', 'g_qkv', 'g_ab', 'g_z', 'merge')
IN_WIDTH = dict(a_q=W_A, a_kv=3 * 2 * A_KV * A_HD, a_gate=A_HEADS * 3, a_z=W_A, m_qkv=3 * W_M, m_if=2 * M_HEADS,
                m_o=W_M, m_z=W_M, g_qkv=3 * W_G, g_ab=2 * G_HEADS, g_z=W_G, merge=N_BRANCH * D_MODEL)
OFF = dict(merge=0, m_qkv=3072, g_qkv=4608, a_q=6144, a_z=6656, m_o=7168, m_z=7680, g_z=8192, a_kv=8704,
           small=9472)
N_PERM = 9600
SM_GATE, SM_MIF, SM_GAB = 0, 24, 32


def _cparams(sem):
    return pltpu.CompilerParams(dimension_semantics=sem, vmem_limit_bytes=VMEM_LIMIT)


def _silu(x):
    return x * jax.nn.sigmoid(x)


def _log_sigmoid(x):
    return jnp.minimum(x, 0.0) - jnp.log(1.0 + jnp.exp(-jnp.abs(x)))


def _softplus(x):
    return jnp.maximum(x, 0.0) + jnp.log(1.0 + jnp.exp(-jnp.abs(x)))


def _split(a):
    hi = a.astype(BF16)
    return hi, (a - hi.astype(F32)).astype(BF16)


def _dot3(a, b, dims=NN):
    mm = lambda x, y: lax.dot_general(x, y, dims, preferred_element_type=F32)
    return mm(a[0], b[0]) + mm(a[0], b[1]) + mm(a[1], b[0])


def _src_offsets():
    offs, off = {}, 0
    for name in IN_ORDER:
        offs[name] = off
        off += IN_WIDTH[name]
    return offs, off


def _permute_kernel(w_ref, o_ref):
    src, n_in = _src_offsets()
    dst = dict(OFF, a_gate=OFF['small'] + SM_GATE, m_if=OFF['small'] + SM_MIF, g_ab=OFF['small'] + SM_GAB)
    for name in IN_ORDER:
        w = IN_WIDTH[name]
        a0 = src[name] // LANES * LANES
        a1 = min(-(-(src[name] + w) // LANES) * LANES, n_in)
        win = w_ref[:, a0:a1]
        o_ref[:, dst[name]:dst[name] + w] = win[:, src[name] - a0:src[name] - a0 + w].astype(BF16)
    used = SM_GAB + IN_WIDTH['g_ab']
    o_ref[:, OFF['small'] + used:N_PERM] = jnp.zeros((o_ref.shape[0], LANES - used), BF16)


def _permute_w_in(w_in):
    d, n_in = w_in.shape
    tr = 128
    return pl.pallas_call(
        _permute_kernel,
        grid=(d // tr,),
        in_specs=[pl.BlockSpec((tr, n_in), lambda i: (i, 0))],
        out_specs=pl.BlockSpec((tr, N_PERM), lambda i: (i, 0)),
        out_shape=jax.ShapeDtypeStruct((d, N_PERM), BF16),
        compiler_params=_cparams(("parallel",)),
        name="permute_w",
    )(w_in)


def _rel_bucket(dist):
    n = jnp.maximum(dist, 0)
    exact = N_BUCKETS // 2
    nf = jnp.maximum(n, exact).astype(F32)
    large = exact + (jnp.log(nf / exact) / math.log(MAX_DIST / exact) * (N_BUCKETS - exact)).astype(jnp.int32)
    return jnp.where(n < exact, n, jnp.minimum(large, N_BUCKETS - 1))


def _bias_kernel(thr_ref, tab_ref, d_ref, o_ref):
    n = jnp.maximum(d_ref[0], 0)
    for h in range(A_HEADS):
        acc = jnp.full(n.shape, tab_ref[h], F32)
        for k in range(1, N_BUCKETS):
            acc = jnp.where(n >= thr_ref[k], tab_ref[k * A_HEADS + h], acc)
        o_ref[0, h // A_REP, h % A_REP] = acc


def _bias_rows(rel_bias, dist):
    N, Q, K = dist.shape
    nmax = 2 * MAX_DIST
    thr = jnp.sum(_rel_bucket(jnp.arange(nmax))[None, :] < jnp.arange(N_BUCKETS)[:, None], axis=1).astype(jnp.int32)
    smem = pl.BlockSpec(memory_space=pltpu.SMEM)
    out = pl.pallas_call(
        _bias_kernel,
        grid=(N,),
        in_specs=[smem, smem, pl.BlockSpec((1, Q, K), lambda i: (i, 0, 0))],
        out_specs=pl.BlockSpec((1, A_KV, A_REP, Q, K), lambda i: (i, 0, 0, 0, 0)),
        out_shape=jax.ShapeDtypeStruct((N, A_KV, A_REP, Q, K), F32),
        compiler_params=_cparams(("parallel",)),
        name="bias_rows",
    )(thr, rel_bias.astype(F32).reshape(N_BUCKETS * A_HEADS), dist.astype(jnp.int32))
    return out.reshape(N, A_KV, A_REP * Q, K)


def _cmp_weights(w):
    wr = w.reshape(A_KV, 2, CMP_STRIDE, A_HD, A_HD)
    eye = jnp.eye(A_KV, dtype=w.dtype)
    full = jnp.einsum('gmjde,gh->jgdmhe', wr, eye)
    return full.reshape(CMP_STRIDE, A_KV * A_HD, 2 * A_KV * A_HD).astype(BF16)


def _cover_t(ns_pad, nch):
    s0 = jnp.arange(ns_pad)[:, None] * SEL_BLOCK
    c0 = jnp.arange(nch)[None, :] * CMP_STRIDE
    return ((c0 < s0 + SEL_BLOCK) & (s0 <= c0 + CMP_BLOCK - 1)).astype(F32)


def _expand_mat(ns_pad, nk):
    return (jnp.arange(nk)[None, :] // SEL_BLOCK == jnp.arange(ns_pad)[:, None]).astype(BF16)


def _gate_expand():
    col = jnp.arange(N_BRANCH * W_A)
    src = 3 * ((col % W_A) // A_HD) + col // W_A
    return (jnp.arange(LANES)[:, None] == src[None, :]).astype(BF16)


def _prompt_tables(rel_bias, T):
    QB = Q_BLOCK
    nqb = T // QB
    nch = T // CMP_STRIDE
    ns = T // SEL_BLOCK
    ns_pad = -(-ns // 8) * 8
    i_ = jnp.arange(QB)
    t = (jnp.arange(nqb) * QB)[:, None, None] + i_[None, :, None]
    cend = (jnp.arange(nch) * CMP_STRIDE + CMP_BLOCK - 1)[None, None, :]
    d = (jnp.arange(nqb) * QB)[:, None, None] + i_[None, :, None] - i_[None, None, :]
    nw = WINDOW // QB + 1
    return dict(
        bcmp=_bias_rows(rel_bias, t - cend),
        tsel=_bias_rows(rel_bias, d),
        covt=_cover_t(ns_pad, nch),
        e3=_expand_mat(LANES, T).reshape(LANES, nqb, QB).transpose(1, 0, 2),
        wm=((d[:nw] >= 0) & (d[:nw] < WINDOW)).astype(F32),
        cm=(d[:2] >= 0).astype(F32),
        gx=_gate_expand())


def _sample_tables(rel_bias, past, tp, t_real):
    SL = past + LANES
    nblk = past // CMP_STRIDE
    ns = -(-(past + t_real) // SEL_BLOCK)
    ns_pad = -(-ns // LANES) * LANES
    WK = WINDOW + LANES
    i_ = jnp.arange(tp)
    t = past + i_
    rows = lambda dist: _bias_rows(rel_bias, dist[None])[0]
    return dict(
        bcmp=rows(t[:, None] - (jnp.arange(nblk) * CMP_STRIDE + CMP_BLOCK - 1)[None, :]),
        bsel=rows(t[:, None] - jnp.arange(SL)[None, :]),
        bwin=rows(i_[:, None] + WINDOW - jnp.arange(WK)[None, :]),
        cov=_cover_t(ns_pad, nblk).T, emat=_expand_mat(ns_pad, SL))


def _proj_in_kernel(x_ref, g_ref, w_ref, o_ref, hn_ref):
    @pl.when(pl.program_id(1) == 0)
    def _():
        x = x_ref[...]
        ms = jnp.mean(x * x, axis=-1, keepdims=True)
        hn_ref[...] = (x * lax.rsqrt(ms + EPS) * g_ref[...]).astype(BF16)

    o_ref[...] = jnp.dot(hn_ref[...], w_ref[...], preferred_element_type=F32)


def _proj_in(x2, norm_g, w_perm):
    n = x2.shape[0]
    tm = min(1024, n)
    tn = 1920
    return pl.pallas_call(
        _proj_in_kernel,
        grid=(n // tm, N_PERM // tn),
        in_specs=[pl.BlockSpec((tm, D_MODEL), lambda i, j: (i, 0)),
                  pl.BlockSpec((1, D_MODEL), lambda i, j: (0, 0)),
                  pl.BlockSpec((D_MODEL, tn), lambda i, j: (0, j))],
        out_specs=pl.BlockSpec((tm, tn), lambda i, j: (i, j)),
        out_shape=jax.ShapeDtypeStruct((n, N_PERM), F32),
        scratch_shapes=[pltpu.VMEM((tm, D_MODEL), BF16)],
        compiler_params=_cparams(("parallel", "arbitrary")),
        name="proj_in",
    )(x2, norm_g.reshape(1, D_MODEL), w_perm)


def _prep_a_kernel(q_ref, kv0_ref, kv1_ref, kv2_ref, bdq_ref, bdk_ref, qg_ref, kg_ref, qo_ref, ro_ref):
    q = q_ref[...]
    ms = jnp.dot(q * q, bdq_ref[...], precision=HI, preferred_element_type=F32) * (1.0 / A_HD)
    qo_ref[...] = q * lax.rsqrt(ms + EPS) * qg_ref[...] * (A_HD ** -0.5)
    for b, kv_ref in enumerate((kv0_ref, kv1_ref, kv2_ref)):
        k = kv_ref[:, 0:128]
        ms = jnp.dot(k * k, bdk_ref[...], precision=HI, preferred_element_type=F32) * (1.0 / A_HD)
        ro_ref[2 * b] = k * lax.rsqrt(ms + EPS) * kg_ref[b:b + 1, :]
        ro_ref[2 * b + 1] = kv_ref[:, 128:256]


def _prep_a(u2, a_qn, a_kn):
    n = u2.shape[0]
    tm = min(512, n)
    bd = lambda w: (jnp.arange(w)[:, None] // A_HD == jnp.arange(w)[None, :] // A_HD).astype(F32)
    qg = jnp.tile(a_qn, A_HEADS).reshape(1, W_A)
    kg = jnp.tile(a_kn, (1, A_KV))
    const = lambda shape: pl.BlockSpec(shape, lambda i: (0, 0))
    kvs = lambda b: pl.BlockSpec((tm, 256), lambda i: (i, OFF['a_kv'] // 256 + b))
    return pl.pallas_call(
        _prep_a_kernel,
        grid=(n // tm,),
        in_specs=[pl.BlockSpec((tm, W_A), lambda i: (i, OFF['a_q'] // W_A)), kvs(0), kvs(1), kvs(2),
                  const((W_A, W_A)), const((128, 128)), const((1, W_A)), const((3, 128))],
        out_specs=[pl.BlockSpec((tm, W_A), lambda i: (i, 0)), pl.BlockSpec((6, tm, 128), lambda i: (0, i, 0))],
        out_shape=[jax.ShapeDtypeStruct((n, W_A), F32), jax.ShapeDtypeStruct((6, n, 128), F32)],
        compiler_params=_cparams(("parallel",)),
        name="prep_a",
    )(u2, u2, u2, u2, bd(W_A), bd(128), qg, kg)


def _stack_heads(q, g):
    return jnp.concatenate([q[:, (A_REP * g + r) * A_HD:(A_REP * g + r + 1) * A_HD] for r in range(A_REP)], axis=0)


def _rep_rows(x):
    return jnp.concatenate([x] * A_REP, axis=0)


def _masked_softmax(s, valid):
    sm = jnp.where(valid, s, NEG)
    m = jnp.max(sm, axis=-1, keepdims=True)
    e = jnp.where(valid, jnp.exp(sm - m), 0.0)
    l = jnp.sum(e, axis=-1, keepdims=True)
    return e / jnp.maximum(l, TINY)


def _select_blocks(pc_sum, covt_ref, st_ref, t_row, ns, n_top):
    ns_pad = covt_ref.shape[0]
    imp = lax.dot_general(covt_ref[...], pc_sum, NT, precision=HI, preferred_element_type=F32)
    jj = lax.broadcasted_iota(jnp.int32, (ns_pad, LANES), 0)
    tt = jnp.broadcast_to(t_row, (ns_pad, LANES))
    cur = tt // SEL_BLOCK
    forced = (jj == 0) | (jj == cur) | (jj == cur - 1)
    future = jj * SEL_BLOCK > tt
    score = jnp.where(future, NEG, jnp.where(forced, -NEG, imp))
    score = jnp.where(jj < ns, score, -jnp.inf)
    st_ref[...] = score

    def beats(k, rank):
        row = jnp.broadcast_to(st_ref[pl.ds(k, 1), :], (ns_pad, LANES))
        b = (row > score) | ((row == score) & (k < jj))
        return rank + jnp.where(b, 1.0, 0.0)

    rank = jnp.zeros((ns_pad, LANES), F32)
    if ns <= 32:
        for k in range(ns):
            rank = beats(k, rank)
    else:
        rank = lax.fori_loop(0, ns, beats, rank)
    return jnp.where(rank < n_top, 1.0, 0.0)


def _pad_rows(x, rows):
    if x.shape[0] == rows:
        return x
    return jnp.concatenate([x, jnp.zeros((rows - x.shape[0], x.shape[1]), x.dtype)], axis=0)


def _select_rows(pc_sum, cov_ref, sel_ref, t0, ns, n_top, nq):
    Q = pc_sum.shape[0]
    ns_pad = cov_ref.shape[1]
    nk = -(-ns // 8) * 8
    imp = jnp.dot(pc_sum, cov_ref[...], precision=HI, preferred_element_type=F32)
    jj = lax.broadcasted_iota(jnp.int32, (Q, ns_pad), 1)
    tt = lax.broadcasted_iota(jnp.int32, (Q, ns_pad), 0) + t0
    cur = tt // SEL_BLOCK
    forced = (jj == 0) | (jj == cur) | (jj == cur - 1)
    future = jj * SEL_BLOCK > tt
    score = jnp.where(future, NEG, jnp.where(forced, -NEG, imp))
    score = jnp.where(jj < ns, score, -jnp.inf)
    score_col = _pad_rows(score, LANES).T
    kk = lax.broadcasted_iota(jnp.int32, (nk, ns_pad), 0)
    jl = lax.broadcasted_iota(jnp.int32, (nk, ns_pad), 1)
    sel_ref[...] = jnp.zeros(sel_ref.shape, F32)
    for i in range(nq):
        col = jnp.broadcast_to(score_col[0:nk, i:i + 1], (nk, ns_pad))
        row = jnp.broadcast_to(score[i:i + 1, :], (nk, ns_pad))
        beats = (col > row) | ((col == row) & (kk < jl))
        rank = jnp.sum(jnp.where(beats, 1.0, 0.0), axis=0, keepdims=True)
        sel_ref[i:i + 1, :] = jnp.where(rank < n_top, 1.0, 0.0)
    return sel_ref[...]


def _compress(src_ref, nrow, wk_ref, wv_ref, pbuf_ref, kc_ref, vc_ref):
    acck = jnp.zeros((nrow, 256), F32)
    accv = jnp.zeros((nrow, 256), F32)
    for j in range(CMP_STRIDE):
        xk = src_ref[0, pl.ds(j, nrow, stride=CMP_STRIDE), :].astype(BF16)
        xv = src_ref[1, pl.ds(j, nrow, stride=CMP_STRIDE), :].astype(BF16)
        acck = acck + jnp.dot(xk, wk_ref[j], preferred_element_type=F32)
        accv = accv + jnp.dot(xv, wv_ref[j], preferred_element_type=F32)
    nout = kc_ref.shape[0]
    for acc, dst in ((acck, kc_ref), (accv, vc_ref)):
        pbuf_ref[0:nrow, :] = acc[:, 128:256]
        dst[...] = acc[0:nout, 0:128] + pbuf_ref[pl.ds(1, nout), :]


def _gated_sum(gates, gx_ref, ocat_ref):
    gh, gl = _split(gates)
    gexp = (jnp.dot(gh, gx_ref[...], preferred_element_type=F32)
            + jnp.dot(gl, gx_ref[...], preferred_element_type=F32))
    out = gexp[:, 0:W_A] * ocat_ref[0]
    for c in range(1, N_BRANCH):
        out = out + gexp[:, c * W_A:(c + 1) * W_A] * ocat_ref[c]
    return out


def _nsa_prompt_kernel(q_ref, rows_ref, small_ref, wk_ref, wv_ref, bcmp_ref, tsel_ref, covt_ref, e3_ref,
                       wm_ref, cm_ref, gx_ref,
                       o_ref, kc_ref, vc_ref, pbuf_ref, sel_ref, acc_ref, m_ref, st_ref, ocat_ref,
                       *, ns, n_top):
    bi = pl.program_id(1)
    T = rows_ref.shape[1]
    nch = T // CMP_STRIDE
    QB = Q_BLOCK
    SEL, WIN = 1, 2

    @pl.when(bi == 0)
    def _():
        pbuf_ref[...] = jnp.zeros(pbuf_ref.shape, F32)
        _compress(rows_ref, nch, wk_ref, wv_ref, pbuf_ref, kc_ref, vc_ref)

    t0 = bi * QB
    q = q_ref[0]
    tc = lax.broadcasted_iota(jnp.int32, (QB, nch), 0) + t0
    cend = lax.broadcasted_iota(jnp.int32, (QB, nch), 1) * CMP_STRIDE + (CMP_BLOCK - 1)
    cvalid = _rep_rows(tc - cend >= 0)
    t_row = lax.broadcasted_iota(jnp.int32, (1, LANES), 1) + t0
    lane = lax.broadcasted_iota(jnp.int32, (QB, LANES), 1)
    in_g = [(lane >= g * A_HD) & (lane < (g + 1) * A_HD) for g in range(A_KV)]
    qg = [_stack_heads(q, g).astype(BF16) for g in range(A_KV)]

    def put_heads(c, g, o):
        for r in range(A_REP):
            h = A_REP * g + r
            ocat_ref[c, :, h * A_HD:(h + 1) * A_HD] = o[r * QB:(r + 1) * QB]

    for g in range(A_KV):
        kc = kc_ref[:, g * A_HD:(g + 1) * A_HD].astype(BF16)
        vc = vc_ref[:, g * A_HD:(g + 1) * A_HD].astype(BF16)
        s = lax.dot_general(qg[g], kc, NT, preferred_element_type=F32) + bcmp_ref[0, g]
        p_c = _masked_softmax(s, cvalid)
        put_heads(0, g, jnp.dot(p_c.astype(BF16), vc, preferred_element_type=F32))
        pc_sum = p_c[0:QB] + p_c[QB:2 * QB] + p_c[2 * QB:3 * QB] + p_c[3 * QB:4 * QB]
        sel_t = _select_blocks(pc_sum, covt_ref, st_ref, t_row, ns, n_top)
        sel_ref[g] = _pad_rows(sel_t, LANES).T.astype(BF16)

    m_ref[...] = jnp.full(m_ref.shape, NEG, F32)
    acc_ref[...] = jnp.zeros(acc_ref.shape, F32)

    def tile(br, g, kb, nb=1):
        ci = 2 * (br - 1) + g
        ks, vs, bs, ms = [], [], [], []
        for j in range(nb):
            off = pl.multiple_of((kb + j) * QB, QB)
            ks.append(rows_ref[2 * br, pl.ds(off, QB), g * A_HD:(g + 1) * A_HD])
            vs.append(jnp.where(in_g[g], rows_ref[2 * br + 1, pl.ds(off, QB), :], 1.0))
            bs.append(tsel_ref[bi - kb - j, g])
            if br == SEL:
                mt = jnp.dot(sel_ref[g], e3_ref[kb + j], preferred_element_type=F32)
                ms.append(mt * cm_ref[jnp.minimum(bi - kb, 1)] if nb == 1 else mt)
            else:
                ms.append(wm_ref[bi - kb - j])
        cat = lambda xs, ax: xs[0] if nb == 1 else jnp.concatenate(xs, axis=ax)
        k = cat(ks, 0).astype(BF16)
        v1 = cat(vs, 0).astype(BF16)
        sc = lax.dot_general(qg[g], k, NT, preferred_element_type=F32) + cat(bs, 1)
        valid = _rep_rows(cat(ms, 1)) > 0.5
        sm = jnp.where(valid, sc, NEG)
        m_prev = m_ref[ci]
        m_new = jnp.maximum(m_prev, jnp.max(sm, axis=-1, keepdims=True))
        p = jnp.where(valid, jnp.exp(sm - cat([m_new] * nb, 1)), 0.0)
        acc_ref[ci] = jnp.exp(m_prev - m_new) * acc_ref[ci] + jnp.dot(p.astype(BF16), v1,
                                                                       preferred_element_type=F32)
        m_ref[ci] = m_new

    def body_sel(kb, carry):
        for g in range(A_KV):
            tile(SEL, g, kb)
        return carry

    def body_sel2(pair, carry):
        for g in range(A_KV):
            tile(SEL, g, 2 * pair, nb=2)
        return carry

    def body_both(kb, carry):
        for g in range(A_KV):
            tile(SEL, g, kb)
            tile(WIN, g, kb)
        return carry

    lo = jnp.maximum(bi - WINDOW // QB, 0)
    lax.fori_loop(0, lo // 2, body_sel2, 0)
    lax.fori_loop(2 * (lo // 2), lo, body_sel, 0)
    lax.fori_loop(lo, bi + 1, body_both, 0)

    for br in (SEL, WIN):
        for g in range(A_KV):
            acc = acc_ref[2 * (br - 1) + g]
            linv = pltpu.roll(1.0 / jnp.maximum(acc, TINY), A_HD, axis=1)
            put_heads(br, g, (acc * linv)[:, g * A_HD:(g + 1) * A_HD])

    o_ref[0] = _gated_sum(jax.nn.sigmoid(small_ref[0]), gx_ref, ocat_ref)


def _nsa_prompt(qn3, rows6, u3, wk, wv, tb):
    B, T, _ = qn3.shape
    QB = Q_BLOCK
    nqb = T // QB
    nch = T // CMP_STRIDE
    ns = T // SEL_BLOCK
    ns_pad = tb['covt'].shape[0]
    n_top = min(TOP_N, ns)
    nw = tb['wm'].shape[0]
    kern = functools.partial(_nsa_prompt_kernel, ns=ns, n_top=n_top)
    c2 = lambda shape: pl.BlockSpec(shape, lambda b, i: (0,) * len(shape))
    return pl.pallas_call(
        kern,
        grid=(B, nqb),
        in_specs=[pl.BlockSpec((1, QB, W_A), lambda b, i: (b, i, 0)),
                  pl.BlockSpec((6, T, 128), lambda b, i: (0, b, 0)),
                  pl.BlockSpec((1, QB, LANES), lambda b, i: (b, i, OFF['small'] // LANES)),
                  c2((CMP_STRIDE, 128, 256)), c2((CMP_STRIDE, 128, 256)),
                  pl.BlockSpec((1, A_KV, A_REP * QB, nch), lambda b, i: (i, 0, 0, 0)),
                  c2((nqb, A_KV, A_REP * QB, QB)),
                  c2((ns_pad, nch)), c2((nqb, LANES, QB)), c2((nw, QB, QB)), c2((2, QB, QB)),
                  c2((LANES, N_BRANCH * W_A))],
        out_specs=pl.BlockSpec((1, QB, W_A), lambda b, i: (b, i, 0)),
        out_shape=jax.ShapeDtypeStruct((B, T, W_A), F32),
        scratch_shapes=[pltpu.VMEM((nch, 128), F32), pltpu.VMEM((nch, 128), F32),
                        pltpu.VMEM((nch + 8, 128), F32),
                        pltpu.VMEM((A_KV, QB, LANES), BF16),
                        pltpu.VMEM((2 * A_KV, A_REP * QB, LANES), F32),
                        pltpu.VMEM((2 * A_KV, A_REP * QB, LANES), F32),
                        pltpu.VMEM((ns_pad, LANES), F32),
                        pltpu.VMEM((N_BRANCH, QB, W_A), F32)],
        compiler_params=_cparams(("parallel", "arbitrary")),
        name="nsa_prompt",
    )(qn3, rows6, u3, wk, wv, tb['bcmp'], tb['tsel'], tb['covt'], tb['e3'], tb['wm'], tb['cm'], tb['gx'])


def _nsa_sample_kernel(pt_ref, *refs, pg, past, tp, t_real, ns, n_top):
    cmp_refs = refs[0:pg]
    sel_refs = refs[pg:2 * pg]
    (q_ref, rows_ref, small_ref, win_ref, wk_ref, wv_ref, bcmp_ref, bsel_ref, bwin_ref, cov_ref, e_ref,
     o_ref, cslab, sslab, wslab, kc_ref, vc_ref, pbuf_ref, st_ref, s_ref) = refs[2 * pg:]
    p = pl.program_id(1)
    npg = pl.num_programs(1)
    SL = cslab.shape[1]
    nblk = kc_ref.shape[0]
    WK = wslab.shape[1]

    for k in range(pg):
        off = pl.multiple_of((p * pg + k) * PAGE_SIZE, PAGE_SIZE)
        for kv in range(2):
            cslab[kv, pl.ds(off, PAGE_SIZE), :] = cmp_refs[k][:, kv * 128:(kv + 1) * 128]
            sslab[kv, pl.ds(off, PAGE_SIZE), :] = sel_refs[k][:, kv * 128:(kv + 1) * 128]

    @pl.when(p == npg - 1)
    def _():
        tail = jnp.zeros((SL - past - tp, 128), F32)
        for kv in range(2):
            cslab[kv, past:past + tp, :] = rows_ref[kv]
            cslab[kv, past + tp:SL, :] = tail
            sslab[kv, past:past + tp, :] = rows_ref[2 + kv]
            sslab[kv, past + tp:SL, :] = tail
            wslab[kv, 0:WINDOW, :] = win_ref[0, :, kv * 128:(kv + 1) * 128]
            wslab[kv, WINDOW:WINDOW + tp, :] = rows_ref[4 + kv]
            wslab[kv, WINDOW + tp:WK, :] = jnp.zeros((WK - WINDOW - tp, 128), F32)
        pbuf_ref[...] = jnp.zeros(pbuf_ref.shape, F32)
        _compress(cslab, nblk + 8, wk_ref, wv_ref, pbuf_ref, kc_ref, vc_ref)

        q = q_ref[0]
        gates = jax.nn.sigmoid(small_ref[0])
        R = A_REP * tp
        ti = lax.broadcasted_iota(jnp.int32, (tp, SL), 0) + past
        causal = ti - lax.broadcasted_iota(jnp.int32, (tp, SL), 1) >= 0
        tc = lax.broadcasted_iota(jnp.int32, (tp, nblk), 0) + past
        cend = lax.broadcasted_iota(jnp.int32, (tp, nblk), 1) * CMP_STRIDE + (CMP_BLOCK - 1)
        cvalid = _rep_rows(tc - cend >= 0)
        wd = (lax.broadcasted_iota(jnp.int32, (tp, WK), 0) + WINDOW
              - lax.broadcasted_iota(jnp.int32, (tp, WK), 1))
        wvalid = _rep_rows((wd >= 0) & (wd < WINDOW))
        nck = 5 if (SL // 5) % LANES == 0 else SL // LANES
        ck = SL // nck

        for g in range(A_KV):
            gs = slice(g * A_HD, (g + 1) * A_HD)
            qg = _stack_heads(q, g).astype(BF16)
            s = lax.dot_general(qg, kc_ref[:, gs].astype(BF16), NT, preferred_element_type=F32) + bcmp_ref[g]
            p_c = _masked_softmax(s, cvalid)
            o_c = jnp.dot(p_c.astype(BF16), vc_ref[:, gs].astype(BF16), preferred_element_type=F32)
            pc_sum = p_c[0:tp] + p_c[tp:2 * tp] + p_c[2 * tp:3 * tp] + p_c[3 * tp:4 * tp]

            sel = _select_rows(pc_sum, cov_ref, st_ref, past, ns, n_top, t_real).astype(BF16)
            kmask = jnp.dot(sel, e_ref[...], preferred_element_type=F32)
            svalid = _rep_rows(causal & (kmask > 0.5))

            for c in range(nck):
                kk = sslab[0, c * ck:(c + 1) * ck, gs].astype(BF16)
                s_ref[:, c * ck:(c + 1) * ck] = lax.dot_general(qg, kk, NT, preferred_element_type=F32)
            p_s = _masked_softmax(s_ref[...] + bsel_ref[g], svalid)
            s_ref[...] = p_s
            o_s = jnp.zeros((R, A_HD), F32)
            for c in range(nck):
                vv = sslab[1, c * ck:(c + 1) * ck, gs].astype(BF16)
                o_s = o_s + jnp.dot(s_ref[:, c * ck:(c + 1) * ck].astype(BF16), vv, preferred_element_type=F32)

            kw = wslab[0, :, gs].astype(BF16)
            vw = wslab[1, :, gs].astype(BF16)
            sw = lax.dot_general(qg, kw, NT, preferred_element_type=F32) + bwin_ref[g]
            p_w = _masked_softmax(sw, wvalid)
            o_w = jnp.dot(p_w.astype(BF16), vw, preferred_element_type=F32)

            for r in range(A_REP):
                h = A_REP * g + r
                rs = slice(r * tp, (r + 1) * tp)
                out = (gates[:, 3 * h:3 * h + 1] * o_c[rs] + gates[:, 3 * h + 1:3 * h + 2] * o_s[rs]
                       + gates[:, 3 * h + 2:3 * h + 3] * o_w[rs])
                o_ref[0, :, h * A_HD:(h + 1) * A_HD] = out


def _nsa_sample(qn3, rows6, u3, pool_cmp, pool_sel, win3, page_table, wk, wv, tb, past, t_real):
    B, tp, _ = qn3.shape
    n_pages = past // PAGE_SIZE
    pg = 8 if n_pages % 8 == 0 else n_pages
    npg = n_pages // pg
    SL = past + LANES
    nblk = past // CMP_STRIDE
    ns = -(-(past + t_real) // SEL_BLOCK)
    ns_pad = tb['cov'].shape[1]
    n_top = min(TOP_N, ns)
    WK = WINDOW + LANES
    kern = functools.partial(_nsa_sample_kernel, pg=pg, past=past, tp=tp, t_real=t_real, ns=ns, n_top=n_top)

    def page_spec(k):
        return pl.BlockSpec((None, PAGE_SIZE, 256), lambda b, p, pt: (pt[b, p * pg + k], 0, 0))

    def c_(shape):
        return pl.BlockSpec(shape, lambda b, p, pt: (0,) * len(shape))

    R = A_REP * tp
    grid_spec = pltpu.PrefetchScalarGridSpec(
        num_scalar_prefetch=1,
        grid=(B, npg),
        in_specs=([page_spec(k) for k in range(pg)] + [page_spec(k) for k in range(pg)]
                  + [pl.BlockSpec((1, tp, W_A), lambda b, p, pt: (b, 0, 0)),
                     pl.BlockSpec((6, tp, 128), lambda b, p, pt: (0, b, 0)),
                     pl.BlockSpec((1, tp, LANES), lambda b, p, pt: (b, 0, OFF['small'] // LANES)),
                     pl.BlockSpec((1, WINDOW, 256), lambda b, p, pt: (b, 0, 0)),
                     c_((CMP_STRIDE, 128, 256)), c_((CMP_STRIDE, 128, 256)),
                     c_((A_KV, R, nblk)), c_((A_KV, R, SL)), c_((A_KV, R, WK)),
                     c_((nblk, ns_pad)), c_((ns_pad, SL))]),
        out_specs=pl.BlockSpec((1, tp, W_A), lambda b, p, pt: (b, 0, 0)),
        scratch_shapes=[pltpu.VMEM((2, SL, 128), F32), pltpu.VMEM((2, SL, 128), F32),
                        pltpu.VMEM((2, WK, 128), F32),
                        pltpu.VMEM((nblk, 128), F32), pltpu.VMEM((nblk, 128), F32),
                        pltpu.VMEM((nblk + 16, 128), F32),
                        pltpu.VMEM((tp, ns_pad), F32), pltpu.VMEM((R, SL), F32)],
    )
    return pl.pallas_call(
        kern,
        grid_spec=grid_spec,
        out_shape=jax.ShapeDtypeStruct((B, tp, W_A), F32),
        compiler_params=_cparams(("parallel", "arbitrary")),
        name="nsa_sample",
    )(page_table, *([pool_cmp] * pg), *([pool_sel] * pg), qn3, rows6, u3, win3, wk, wv,
      tb['bcmp'], tb['bsel'], tb['bwin'], tb['cov'], tb['emat'])


def _tri(L, lower_incl):
    r = lax.broadcasted_iota(jnp.int32, (L, L), 0)
    c = lax.broadcasted_iota(jnp.int32, (L, L), 1)
    return (r >= c) if lower_incl else (r > c)


def _mlstm_kernel(x_ref, small_ref, smallt_ref, brow_ref, bcol_ref, c0_ref, n0_ref, m0_ref,
                  h_ref, co_ref, no_ref, mo_ref, c_s, n_s, m_s, *, L, t_real):
    c = pl.program_id(1)
    nc = pl.num_programs(1)

    @pl.when(c == 0)
    def _():
        c_s[...] = c0_ref[0]
        n_s[...] = n0_ref[0]
        m_s[...] = m0_ref[0]

    sm = small_ref[0]
    smt = smallt_ref[0, 0]
    li_col = sm[:, SM_MIF:SM_MIF + 4] + brow_ref[0:1, 0:4]
    lf_col = _log_sigmoid(sm[:, SM_MIF + 4:SM_MIF + 8] + brow_ref[0:1, 4:8])
    li_row = smt[0:4, :] + bcol_ref[0:4, :]
    lf_row = _log_sigmoid(smt[4:8, :] + bcol_ref[4:8, :])
    if t_real % L:
        tcol = lax.broadcasted_iota(jnp.int32, (L, 4), 0) + c * L
        trow = lax.broadcasted_iota(jnp.int32, (4, L), 1) + c * L
        li_col = jnp.where(tcol < t_real, li_col, NEG)
        lf_col = jnp.where(tcol < t_real, lf_col, 0.0)
        li_row = jnp.where(trow < t_real, li_row, NEG)
        lf_row = jnp.where(trow < t_real, lf_row, 0.0)
    low = _tri(L, True)
    b_col = jnp.dot(low.astype(F32), lf_col, precision=HI, preferred_element_type=F32)
    b_row = lax.dot_general(lf_row, low.astype(F32), NT, precision=HI, preferred_element_type=F32)

    for h in range(M_HEADS):
        hs = slice(h * M_HD, (h + 1) * M_HD)
        q = x_ref[0, :, h * M_HD:(h + 1) * M_HD]
        k = x_ref[0, :, W_M + h * M_HD:W_M + (h + 1) * M_HD] * (M_HD ** -0.5)
        v = x_ref[0, :, 2 * W_M + h * M_HD:2 * W_M + (h + 1) * M_HD]
        bc = b_col[:, h:h + 1]
        br = b_row[h:h + 1, :]
        Dm = jnp.where(low, bc - br + li_row[h:h + 1, :], NEG)
        m_prev = m_s[h][:, 0:1]
        a = bc + m_prev
        mt = jnp.maximum(a, jnp.max(Dm, axis=-1, keepdims=True))
        S = lax.dot_general(q, k, NT, preferred_element_type=F32) * jnp.exp(Dm - mt)
        inter = jnp.exp(a - mt)
        C = c_s[h]
        n = n_s[h]
        num = inter * jnp.dot(q, C, preferred_element_type=F32) + jnp.dot(S, v, preferred_element_type=F32)
        den = inter * jnp.sum(q * n, axis=-1, keepdims=True) + jnp.sum(S, axis=-1, keepdims=True)
        h_ref[0, :, hs] = num / jnp.maximum(jnp.abs(den), jnp.exp(-mt))
        bL = bc[L - 1:L, :]
        wlog = bL - bc + li_col[:, h:h + 1]
        m_new = jnp.maximum(bL + m_prev, jnp.max(wlog, axis=0, keepdims=True))
        ws = jnp.exp(wlog - m_new)
        dec = jnp.exp(bL + m_prev - m_new)
        kw = k * ws
        c_s[h] = dec * C + lax.dot_general(kw, v, TN, preferred_element_type=F32)
        n_s[h] = dec * n + jnp.sum(kw, axis=0, keepdims=True)
        m_s[h] = jnp.broadcast_to(m_new, (1, LANES))

    @pl.when(c == nc - 1)
    def _():
        co_ref[0] = c_s[...]
        no_ref[0] = n_s[...]
        mo_ref[0] = m_s[...]


def _mlstm(u3, smallt, m_bi, m_bf, C0, n0, m0, L, t_real):
    B, T, _ = u3.shape
    nc = T // L
    brow = jnp.concatenate([m_bi, m_bf]).reshape(1, 8)
    bcol = jnp.concatenate([m_bi, m_bf]).reshape(8, 1)
    n0 = n0.reshape(B, M_HEADS, 1, M_HD)
    m0 = jnp.broadcast_to(m0[:, :, None, None], (B, M_HEADS, 1, LANES))
    kern = functools.partial(_mlstm_kernel, L=L, t_real=t_real)
    st = lambda shape: pl.BlockSpec(shape, lambda b, c: (b,) + (0,) * (len(shape) - 1))
    h, Cn, nn, mn = pl.pallas_call(
        kern,
        grid=(B, nc),
        in_specs=[pl.BlockSpec((1, L, 3 * W_M), lambda b, c: (b, c, OFF['m_qkv'] // (3 * W_M))),
                  pl.BlockSpec((1, L, LANES), lambda b, c: (b, c, OFF['small'] // LANES)),
                  pl.BlockSpec((1, 1, 16, L), lambda b, c: (b, c, 0, 0)),
                  pl.BlockSpec((1, 8), lambda b, c: (0, 0)), pl.BlockSpec((8, 1), lambda b, c: (0, 0)),
                  st((1, M_HEADS, M_HD, M_HD)), st((1, M_HEADS, 1, M_HD)), st((1, M_HEADS, 1, LANES))],
        out_specs=[pl.BlockSpec((1, L, W_M), lambda b, c: (b, c, 0)),
                   st((1, M_HEADS, M_HD, M_HD)), st((1, M_HEADS, 1, M_HD)), st((1, M_HEADS, 1, LANES))],
        out_shape=[jax.ShapeDtypeStruct((B, T, W_M), F32),
                   jax.ShapeDtypeStruct((B, M_HEADS, M_HD, M_HD), F32),
                   jax.ShapeDtypeStruct((B, M_HEADS, 1, M_HD), F32),
                   jax.ShapeDtypeStruct((B, M_HEADS, 1, LANES), F32)],
        scratch_shapes=[pltpu.VMEM((M_HEADS, M_HD, M_HD), F32), pltpu.VMEM((M_HEADS, 1, M_HD), F32),
                        pltpu.VMEM((M_HEADS, 1, LANES), F32)],
        compiler_params=_cparams(("parallel", "arbitrary")),
        name="mlstm",
    )(u3, u3, smallt, brow, bcol, C0, n0, m0)
    return h, Cn, nn.reshape(B, M_HEADS, M_HD), mn[:, :, 0, 0]


def _gdn_a_kernel(x_ref, small_ref, smallt_ref, cw_ref, prow_ref, pcol_ref, buf_ref,
                  u_ref, w_ref, qe_ref, kd_ref, at_ref, eg_ref, xbuf, *, L, CB, t_real):
    c = pl.program_id(1)
    PRE = 8
    R = CB * L

    @pl.when(c == 0)
    def _():
        xbuf[0:PRE, :] = buf_ref[0]

    xbuf[PRE:PRE + R, :] = x_ref[0]
    conv = jnp.zeros((R, 3 * W_G), F32)
    for j in range(CONV_W):
        conv = conv + xbuf[pl.ds(PRE - (CONV_W - 1) + j, R), :] * cw_ref[j:j + 1, :]
    xbuf[0:PRE, :] = xbuf[R:R + PRE, :]
    conv = _silu(conv)

    sm = small_ref[0]
    g_all = -jnp.exp(prow_ref[0:1, 0:4]) * _softplus(sm[:, SM_GAB:SM_GAB + 4] + prow_ref[1:2, 0:4])
    beta_all = jax.nn.sigmoid(sm[:, SM_GAB + 4:SM_GAB + 8])
    lowf = _tri(L, True).astype(F32)

    for cb in range(CB):
        rs = slice(cb * L, (cb + 1) * L)
        smt = smallt_ref[0, cb]
        g_col = g_all[rs]
        beta_col = beta_all[rs]
        g_row = -jnp.exp(pcol_ref[0:4, 0:1]) * _softplus(smt[8:12, :] + pcol_ref[0:4, 1:2])
        if t_real % L:
            t0 = (c * CB + cb) * L
            tcol = lax.broadcasted_iota(jnp.int32, (L, 4), 0) + t0
            trow = lax.broadcasted_iota(jnp.int32, (4, L), 1) + t0
            g_col = jnp.where(tcol < t_real, g_col, 0.0)
            beta_col = jnp.where(tcol < t_real, beta_col, 0.0)
            g_row = jnp.where(trow < t_real, g_row, 0.0)
        G_col = jnp.dot(lowf, g_col, precision=HI, preferred_element_type=F32)
        G_row = lax.dot_general(g_row, lowf, NT, precision=HI, preferred_element_type=F32)

        qs, ks, kbs, rhss = [], [], [], []
        for h in range(G_HEADS):
            hs = slice(h * G_HD, (h + 1) * G_HD)
            cq = conv[rs, h * G_HD:(h + 1) * G_HD]
            ck = conv[rs, W_G + h * G_HD:W_G + (h + 1) * G_HD]
            v = conv[rs, 2 * W_G + h * G_HD:2 * W_G + (h + 1) * G_HD]
            q = cq * lax.rsqrt(jnp.sum(cq * cq, axis=-1, keepdims=True) + EPS) * (G_HD ** -0.5)
            k = ck * lax.rsqrt(jnp.sum(ck * ck, axis=-1, keepdims=True) + EPS)
            Gc = G_col[:, h:h + 1]
            bcol = beta_col[:, h:h + 1]
            kb = k * bcol
            eG = jnp.exp(Gc)
            GL = Gc[L - 1:L, :]
            qs.append(q)
            ks.append(k)
            kbs.append(kb)
            rhss.append(jnp.concatenate([v * bcol, kb * eG], axis=1))
            qe_ref[0, rs, hs] = q * eG
            kd_ref[0, rs, hs] = k * jnp.exp(GL - Gc)
            eg_ref[0, cb, h:h + 1, :] = jnp.broadcast_to(jnp.exp(GL), (1, LANES))
        eg_ref[0, cb, G_HEADS:8, :] = jnp.zeros((8 - G_HEADS, LANES), F32)

        H = G_HEADS
        lane_h = lax.broadcasted_iota(jnp.int32, (L, H * L), 1) // L
        blockmask = (lax.broadcasted_iota(jnp.int32, (H * L, H * L), 0) // L
                     == lax.broadcasted_iota(jnp.int32, (H * L, H * L), 1) // L)

        def pack_diag(full):
            out = full[0:L]
            for h in range(1, H):
                out = jnp.where(lane_h == h, full[h * L:(h + 1) * L], out)
            return out

        def block_diag(m, parts):
            if L % 16:
                return _split(jnp.where(blockmask, jnp.concatenate([m] * H, axis=0), 0.0))
            return tuple(jnp.where(blockmask, jnp.concatenate([p] * H, axis=0), jnp.zeros((), BF16))
                         for p in parts)

        Gc_p = jnp.concatenate([jnp.broadcast_to(G_col[:, h:h + 1], (L, L)) for h in range(H)], axis=1)
        Gr_p = jnp.concatenate([jnp.broadcast_to(G_row[h:h + 1, :], (L, L)) for h in range(H)], axis=1)
        row_p = lax.broadcasted_iota(jnp.int32, (L, H * L), 0)
        col_p = lax.broadcasted_iota(jnp.int32, (L, H * L), 1) % L
        low_p = row_p >= col_p
        dmask = jnp.where(low_p, jnp.exp(jnp.where(low_p, Gc_p - Gr_p, 0.0)), 0.0)
        k_st = _split(jnp.concatenate(ks, axis=0))
        A = pack_diag(_dot3(_split(jnp.concatenate(kbs, axis=0)), k_st, NT)) * jnp.where(row_p > col_p, dmask, 0.0)
        attn = pack_diag(lax.dot_general(jnp.concatenate(qs, axis=0).astype(BF16), k_st[0], NT,
                                         preferred_element_type=F32)) * dmask
        at_ref[0, rs, :] = attn
        X = jnp.where(row_p == col_p, 1.0, 0.0) - A
        As = _split(A)
        Pw = _dot3(As, block_diag(A, As))
        span = 2
        while span < L:
            Ps = _split(Pw)
            Pbd = block_diag(Pw, Ps)
            X = X + _dot3(_split(X), Pbd)
            span *= 2
            if span < L:
                Pw = _dot3(Ps, Pbd)
        sol = _dot3(block_diag(X, _split(X)), _split(jnp.concatenate(rhss, axis=0)))
        for h in range(H):
            hs = slice(h * G_HD, (h + 1) * G_HD)
            u_ref[0, rs, hs] = sol[h * L:(h + 1) * L, 0:G_HD]
            w_ref[0, rs, hs] = sol[h * L:(h + 1) * L, G_HD:2 * G_HD]


def _gdn_b_kernel(u_ref, w_ref, qe_ref, kd_ref, at_ref, eg_ref, s0_ref, o_ref, so_ref, s_s, *, L, BB):
    c = pl.program_id(1)
    nc = pl.num_programs(1)

    @pl.when(c == 0)
    def _():
        s_s[...] = s0_ref[...]

    for b in range(BB):
        for h in range(G_HEADS):
            hs = slice(h * G_HD, (h + 1) * G_HD)
            S = s_s[b, h]
            wq = jnp.concatenate([w_ref[b, :, hs], qe_ref[b, :, hs]], axis=0)
            r = jnp.dot(wq, S, preferred_element_type=F32)
            v_new = u_ref[b, :, hs] - r[0:L]
            o_ref[b, :, hs] = r[L:2 * L] + jnp.dot(at_ref[b, :, h * L:(h + 1) * L], v_new,
                                                   preferred_element_type=F32)
            s_s[b, h] = eg_ref[b, 0, h:h + 1, :] * S + lax.dot_general(kd_ref[b, :, hs], v_new, TN,
                                                                      preferred_element_type=F32)

    @pl.when(c == nc - 1)
    def _():
        so_ref[...] = s_s[...]


def _gdn(u3, smallt, g_conv, g_A_log, g_dt_bias, buf, S0, L, t_real):
    B, T, _ = u3.shape
    nc = T // L
    CB = 2 if nc % 2 == 0 else 1
    BB = 8 if B % 8 == 0 else B
    prow = jnp.stack([g_A_log, g_dt_bias])
    pcol = jnp.stack([g_A_log, g_dt_bias], axis=1)
    buf8 = jnp.concatenate([jnp.zeros((B, 8 - (CONV_W - 1), 3 * W_G), F32), buf], axis=1)
    st = lambda shape: pl.BlockSpec(shape, lambda b, c: (b,) + (0,) * (len(shape) - 1))
    cst = lambda shape: pl.BlockSpec(shape, lambda b, c: (0,) * len(shape))
    row = lambda nb, r, w: pl.BlockSpec((nb, r, w), lambda b, c: (b, c, 0))
    tok = lambda w: jax.ShapeDtypeStruct((B, T, w), F32)
    u, w, qe, kd, at, eg = pl.pallas_call(
        functools.partial(_gdn_a_kernel, L=L, CB=CB, t_real=t_real),
        grid=(B, nc // CB),
        in_specs=[pl.BlockSpec((1, CB * L, 3 * W_G), lambda b, c: (b, c, OFF['g_qkv'] // (3 * W_G))),
                  pl.BlockSpec((1, CB * L, LANES), lambda b, c: (b, c, OFF['small'] // LANES)),
                  pl.BlockSpec((1, CB, 16, L), lambda b, c: (b, c, 0, 0)),
                  cst((CONV_W, 3 * W_G)), cst((2, 4)), cst((4, 2)), st((1, 8, 3 * W_G))],
        out_specs=[row(1, CB * L, W_G)] * 4 + [row(1, CB * L, G_HEADS * L),
                                               pl.BlockSpec((1, CB, 8, LANES), lambda b, c: (b, c, 0, 0))],
        out_shape=[tok(W_G)] * 4 + [tok(G_HEADS * L), jax.ShapeDtypeStruct((B, nc, 8, LANES), F32)],
        scratch_shapes=[pltpu.VMEM((CB * L + 8, 3 * W_G), F32)],
        compiler_params=_cparams(("parallel", "arbitrary")),
        name="gdn_a",
    )(u3, u3, smallt, g_conv, prow, pcol, buf8)
    sblk = pl.BlockSpec((BB, G_HEADS, G_HD, G_HD), lambda b, c: (b, 0, 0, 0))
    return pl.pallas_call(
        functools.partial(_gdn_b_kernel, L=L, BB=BB),
        grid=(B // BB, nc),
        in_specs=[row(BB, L, W_G)] * 4 + [row(BB, L, G_HEADS * L),
                                          pl.BlockSpec((BB, 1, 8, LANES), lambda b, c: (b, c, 0, 0)), sblk],
        out_specs=[row(BB, L, W_G), sblk],
        out_shape=[tok(W_G), jax.ShapeDtypeStruct((B, G_HEADS, G_HD, G_HD), F32)],
        scratch_shapes=[pltpu.VMEM((BB, G_HEADS, G_HD, G_HD), F32)],
        compiler_params=_cparams(("parallel", "arbitrary")),
        name="gdn_b",
    )(u, w, qe, kd, at, eg, S0)


def _head_rmsnorm(x, gain_ref, nheads, hd):
    outs = []
    for h in range(nheads):
        xh = x[:, h * hd:(h + 1) * hd]
        ms = jnp.mean(xh * xh, axis=-1, keepdims=True)
        outs.append(xh * lax.rsqrt(ms + EPS) * gain_ref[...])
    return jnp.concatenate(outs, axis=1)


def _merge_kernel(x_ref, oa_ref, hm_ref, og_ref, az_ref, mo_ref, mz_ref, gz_ref, mg_ref,
                  wb_ref, wo_ref, mhn_ref, ghn_ref, y_ref):
    oa = oa_ref[...] * _silu(az_ref[...])
    om = _head_rmsnorm(hm_ref[...], mhn_ref, M_HEADS, M_HD) * jax.nn.sigmoid(mo_ref[...]) * _silu(mz_ref[...])
    og = _head_rmsnorm(og_ref[...], ghn_ref, G_HEADS, G_HD) * _silu(gz_ref[...])
    y = jnp.zeros(y_ref.shape, F32)
    for i, br in enumerate((oa, om, og)):
        proj = jnp.dot(br.astype(BF16), wb_ref[i], preferred_element_type=F32)
        y = y + jax.nn.sigmoid(mg_ref[:, i * D_MODEL:(i + 1) * D_MODEL]) * proj
    y_ref[...] = x_ref[...] + jnp.dot(y.astype(BF16), wo_ref[...], preferred_element_type=F32)


def _merge_out(x2, o_a, h_m, o_g, u2, w_branch, w_out, m_hn, g_hn):
    n = x2.shape[0]
    tm = min(512, n)
    row = lambda w, off: pl.BlockSpec((tm, w), lambda i: (i, off // w))
    cst = lambda shape: pl.BlockSpec(shape, lambda i: (0,) * len(shape))
    return pl.pallas_call(
        _merge_kernel,
        grid=(n // tm,),
        in_specs=[row(D_MODEL, 0), row(W_A, 0), row(W_M, 0), row(W_G, 0),
                  row(W_A, OFF['a_z']), row(W_M, OFF['m_o']), row(W_M, OFF['m_z']), row(W_G, OFF['g_z']),
                  row(N_BRANCH * D_MODEL, OFF['merge']),
                  cst((N_BRANCH, W_A, D_MODEL)), cst((D_MODEL, D_MODEL)), cst((1, M_HD)), cst((1, G_HD))],
        out_specs=row(D_MODEL, 0),
        out_shape=jax.ShapeDtypeStruct((n, D_MODEL), F32),
        compiler_params=_cparams(("parallel",)),
        name="merge_out",
    )(x2, o_a, h_m, o_g, u2, u2, u2, u2, u2, w_branch.astype(BF16), w_out.astype(BF16),
      m_hn.reshape(1, M_HD), g_hn.reshape(1, G_HD))


def _small_t(u3, L):
    B, T, _ = u3.shape
    s = u3[:, :, OFF['small'] + SM_MIF:OFF['small'] + SM_MIF + 16]
    return s.reshape(B, T // L, L, 16).transpose(0, 1, 3, 2)


def _layer(x, lw, tb, past, q_off):
    B, T, _ = x.shape
    x2 = x.reshape(B * T, D_MODEL)
    u2 = _proj_in(x2, lw['norm_g'], lw['w_perm'])
    if past is None:
        tp, L = T, math.gcd(T, M_CHUNK)
        u3 = u2.reshape(B, T, N_PERM)
        C0 = jnp.zeros((B, M_HEADS, M_HD, M_HD), F32)
        n0 = jnp.zeros((B, M_HEADS, M_HD), F32)
        m0 = jnp.zeros((B, M_HEADS), F32)
        S0 = jnp.zeros((B, G_HEADS, G_HD, G_HD), F32)
        buf = jnp.zeros((B, CONV_W - 1, 3 * W_G), F32)
    else:
        tp = -(-T // 8) * 8
        L = tp
        u3 = jnp.pad(u2.reshape(B, T, N_PERM), ((0, 0), (0, tp - T), (0, 0)))
        C0, n0, m0, S0, buf = past['mC'], past['mn'], past['mm'], past['gS'], past['gconv']
    up = u3.reshape(B * tp, N_PERM)
    qn, rows6 = _prep_a(up, lw['a_qn'], lw['a_kn'])
    qn3 = qn.reshape(B, tp, W_A)
    if past is None:
        o_a = _nsa_prompt(qn3, rows6, u3, lw['wk'], lw['wv'], tb)
    else:
        o_a = _nsa_sample(qn3, rows6, u3, past['cmp'], past['sel'], past['win'], past['page_table'],
                          lw['wk'], lw['wv'], tb, q_off, T)
    smallt = _small_t(u3, L)
    h_m, Cn, nn, mn = _mlstm(u3, smallt, lw['m_bi'], lw['m_bf'], C0, n0, m0, L, T)
    o_g, Sn = _gdn(u3, smallt, lw['g_conv'], lw['g_A_log'], lw['g_dt_bias'], buf, S0, L, T)
    if tp != T:
        o_a, h_m, o_g = o_a[:, :T], h_m[:, :T], o_g[:, :T]
    y2 = _merge_out(x2, o_a.reshape(B * T, W_A), h_m.reshape(B * T, W_M), o_g.reshape(B * T, W_G), u2,
                    lw['w_branch'], lw['w_out'], lw['m_hn'], lw['g_hn'])
    rows = rows6.reshape(3, 2, B, tp, A_KV, A_HD)[:, :, :, :T]
    new_cmp, new_sel, new_win = (jnp.moveaxis(rows[b], 0, 2) for b in range(3))
    g_qkv = u3[:, :T, OFF['g_qkv']:OFF['g_qkv'] + 3 * W_G]
    if past is None:
        win = new_win
    else:
        win = jnp.concatenate([past['win'].reshape(B, -1, 2, A_KV, A_HD), new_win], axis=1)
    full = jnp.concatenate([buf, g_qkv[:, max(T - (CONV_W - 1), 0):]], axis=1)
    state = dict(cmp=new_cmp, sel=new_sel, win=win[:, -min(WINDOW, win.shape[1]):],
                 mC=Cn, mn=nn, mm=mn, gS=Sn, gconv=full[:, -(CONV_W - 1):])
    return y2.reshape(B, T, D_MODEL), state


def kernel(x_prompt, x_sample, cache_cmp_kv, cache_sel_kv, cache_win_kv, state_mlstm_C, state_mlstm_n,
           state_mlstm_m, state_gdn_S, state_gdn_conv, page_table, norm_g, w_in, a_qn, a_kn, a_cmp_wk,
           a_cmp_wv, rel_bias, m_bi, m_bf, m_hn, g_conv, g_A_log, g_dt_bias, g_hn, w_branch, w_out):
    names = ('cmp', 'sel', 'win', 'mC', 'mn', 'mm', 'gS', 'gconv')
    st_p = {k: [] for k in names}
    st_s = {k: [] for k in names}
    past_len = page_table.shape[1] * PAGE_SIZE
    n_pool = cache_cmp_kv.shape[1]
    db, dt = x_sample.shape[0], x_sample.shape[1]
    tb_p = _prompt_tables(rel_bias, x_prompt.shape[1])
    tb_s = _sample_tables(rel_bias, past_len, -(-dt // 8) * 8, dt)
    y_p, y_s = x_prompt, x_sample
    for l in range(DEPTH):
        lw = dict(norm_g=norm_g[l], w_perm=_permute_w_in(w_in[l]), a_qn=a_qn[l], a_kn=a_kn[l],
                  wk=_cmp_weights(a_cmp_wk[l]), wv=_cmp_weights(a_cmp_wv[l]),
                  m_bi=m_bi[l], m_bf=m_bf[l], m_hn=m_hn[l], g_conv=g_conv[l], g_A_log=g_A_log[l],
                  g_dt_bias=g_dt_bias[l], g_hn=g_hn[l], w_branch=w_branch[l], w_out=w_out[l])
        y_p, new_p = _layer(y_p, lw, tb_p, None, 0)
        past = dict(cmp=cache_cmp_kv[l].reshape(n_pool, PAGE_SIZE, 256),
                    sel=cache_sel_kv[l].reshape(n_pool, PAGE_SIZE, 256),
                    win=cache_win_kv[l].reshape(db, -1, 256), page_table=page_table,
                    mC=state_mlstm_C[l], mn=state_mlstm_n[l], mm=state_mlstm_m[l],
                    gS=state_gdn_S[l], gconv=state_gdn_conv[l])
        y_s, new_s = _layer(y_s, lw, tb_s, past, past_len)
        for k in names:
            st_p[k].append(new_p[k])
            st_s[k].append(new_s[k])
    P = {k: jnp.stack(v) for k, v in st_p.items()}
    S = {k: jnp.stack(v) for k, v in st_s.items()}
    return (y_p, y_s, P['cmp'], S['cmp'], P['sel'], S['sel'], P['win'], S['win'],
            P['mC'], S['mC'], P['mn'], S['mn'], P['mm'], S['mm'], P['gS'], S['gS'], P['gconv'], S['gconv'])
```

```python
import functools
import math

import jax
import jax.numpy as jnp
from jax import lax
from jax.experimental import pallas as pl
from jax.experimental.pallas import tpu as pltpu

D_MODEL = 1024
DEPTH = 2
PAGE_SIZE = 128
A_HEADS = 8
A_KV = 2
A_REP = A_HEADS // A_KV
A_HD = 64
CMP_BLOCK = 32
CMP_STRIDE = 16
SEL_BLOCK = 64
TOP_N = 16
WINDOW = 512
Q_BLOCK = 128
N_BUCKETS = 32
MAX_DIST = 2048
M_HEADS = 4
M_HD = 128
M_CHUNK = 64
G_HEADS = 4
G_HD = 128
G_CHUNK = 64
CONV_W = 4
W_A = A_HEADS * A_HD
W_M = M_HEADS * M_HD
W_G = G_HEADS * G_HD
N_BRANCH = 3
EPS = 1e-6
NEG = -1e30
TINY = 1e-30

F32 = jnp.float32
BF16 = jnp.bfloat16
HI = lax.Precision.HIGHEST
NN = (((1,), (0,)), ((), ()))
NT = (((1,), (1,)), ((), ()))
TN = (((0,), (0,)), ((), ()))

LANES = 128
VMEM_LIMIT = 56 * 1024 * 1024

IN_ORDER = ('a_q', 'a_kv', 'a_gate', 'a_z', 'm_qkv', 'm_if', 'm_o', 'm_z', 'g_qkv', 'g_ab', 'g_z', 'merge')
IN_WIDTH = dict(a_q=W_A, a_kv=3 * 2 * A_KV * A_HD, a_gate=A_HEADS * 3, a_z=W_A, m_qkv=3 * W_M, m_if=2 * M_HEADS,
                m_o=W_M, m_z=W_M, g_qkv=3 * W_G, g_ab=2 * G_HEADS, g_z=W_G, merge=N_BRANCH * D_MODEL)
OFF = dict(merge=0, m_qkv=3072, g_qkv=4608, a_q=6144, a_z=6656, m_o=7168, m_z=7680, g_z=8192, a_kv=8704,
           small=9472)
N_PERM = 9600
SM_GATE, SM_MIF, SM_GAB = 0, 24, 32


def _cparams(sem):
    return pltpu.CompilerParams(dimension_semantics=sem, vmem_limit_bytes=VMEM_LIMIT)


def _silu(x):
    return x * jax.nn.sigmoid(x)


def _log_sigmoid(x):
    return jnp.minimum(x, 0.0) - jnp.log(1.0 + jnp.exp(-jnp.abs(x)))


def _softplus(x):
    return jnp.maximum(x, 0.0) + jnp.log(1.0 + jnp.exp(-jnp.abs(x)))


def _split(a):
    hi = a.astype(BF16)
    return hi, (a - hi.astype(F32)).astype(BF16)


def _dot3(a, b, dims=NN):
    mm = lambda x, y: lax.dot_general(x, y, dims, preferred_element_type=F32)
    return mm(a[0], b[0]) + mm(a[0], b[1]) + mm(a[1], b[0])


def _src_offsets():
    offs, off = {}, 0
    for name in IN_ORDER:
        offs[name] = off
        off += IN_WIDTH[name]
    return offs, off


def _permute_kernel(w_ref, o_ref):
    src, _ = _src_offsets()
    small = []
    for name in IN_ORDER:
        w = IN_WIDTH[name]
        rows = w_ref[src[name]:src[name] + w, :]
        if w % LANES:
            small.append(rows)
        else:
            o_ref[:, OFF[name]:OFF[name] + w] = rows.T.astype(BF16)
    used = sum(r.shape[0] for r in small)
    small.append(jnp.zeros((LANES - used, w_ref.shape[1]), F32))
    o_ref[:, OFF['small']:N_PERM] = jnp.concatenate(small, axis=0).T.astype(BF16)


def _permute_w_in(w_t, layer):
    _, n_in, d = w_t.shape
    tr = 128
    return pl.pallas_call(
        _permute_kernel,
        grid=(d // tr,),
        in_specs=[pl.BlockSpec((None, n_in, tr), lambda i: (layer, 0, i))],
        out_specs=pl.BlockSpec((tr, N_PERM), lambda i: (i, 0)),
        out_shape=jax.ShapeDtypeStruct((d, N_PERM), BF16),
        compiler_params=_cparams(("parallel",)),
        name="permute_w",
    )(w_t)


def _rel_bucket(dist):
    n = jnp.maximum(dist, 0)
    exact = N_BUCKETS // 2
    nf = jnp.maximum(n, exact).astype(F32)
    large = exact + (jnp.log(nf / exact) / math.log(MAX_DIST / exact) * (N_BUCKETS - exact)).astype(jnp.int32)
    return jnp.where(n < exact, n, jnp.minimum(large, N_BUCKETS - 1))


def _bias_kernel(thr_ref, tab_ref, d_ref, o_ref):
    n = jnp.maximum(d_ref[0], 0)
    for h in range(A_HEADS):
        acc = jnp.full(n.shape, tab_ref[h], F32)
        for k in range(1, N_BUCKETS):
            acc = jnp.where(n >= thr_ref[k], tab_ref[k * A_HEADS + h], acc)
        o_ref[0, h // A_REP, h % A_REP] = acc


def _bias_rows(rel_bias, dist):
    N, Q, K = dist.shape
    nmax = 2 * MAX_DIST
    thr = jnp.sum(_rel_bucket(jnp.arange(nmax))[None, :] < jnp.arange(N_BUCKETS)[:, None], axis=1).astype(jnp.int32)
    smem = pl.BlockSpec(memory_space=pltpu.SMEM)
    out = pl.pallas_call(
        _bias_kernel,
        grid=(N,),
        in_specs=[smem, smem, pl.BlockSpec((1, Q, K), lambda i: (i, 0, 0))],
        out_specs=pl.BlockSpec((1, A_KV, A_REP, Q, K), lambda i: (i, 0, 0, 0, 0)),
        out_shape=jax.ShapeDtypeStruct((N, A_KV, A_REP, Q, K), F32),
        compiler_params=_cparams(("parallel",)),
        name="bias_rows",
    )(thr, rel_bias.astype(F32).reshape(N_BUCKETS * A_HEADS), dist.astype(jnp.int32))
    return out.reshape(N, A_KV, A_REP * Q, K)


def _cmp_weights(w):
    wr = w.reshape(A_KV, 2, CMP_STRIDE, A_HD, A_HD)
    eye = jnp.eye(A_KV, dtype=w.dtype)
    full = jnp.einsum('gmjde,gh->jgdmhe', wr, eye)
    return full.reshape(CMP_STRIDE, A_KV * A_HD, 2 * A_KV * A_HD).astype(BF16)


def _cover_t(ns_pad, nch):
    s0 = jnp.arange(ns_pad)[:, None] * SEL_BLOCK
    c0 = jnp.arange(nch)[None, :] * CMP_STRIDE
    return ((c0 < s0 + SEL_BLOCK) & (s0 <= c0 + CMP_BLOCK - 1)).astype(F32)


def _expand_mat(ns_pad, nk):
    return (jnp.arange(nk)[None, :] // SEL_BLOCK == jnp.arange(ns_pad)[:, None]).astype(BF16)


def _gate_expand():
    col = jnp.arange(N_BRANCH * W_A)
    src = 3 * ((col % W_A) // A_HD) + col // W_A
    return (jnp.arange(LANES)[:, None] == src[None, :]).astype(BF16)


def _prompt_tables(rel_bias, T):
    QB = Q_BLOCK
    nqb = T // QB
    nch = T // CMP_STRIDE
    ns = T // SEL_BLOCK
    ns_pad = -(-ns // 8) * 8
    i_ = jnp.arange(QB)
    t = (jnp.arange(nqb) * QB)[:, None, None] + i_[None, :, None]
    cend = (jnp.arange(nch) * CMP_STRIDE + CMP_BLOCK - 1)[None, None, :]
    d = (jnp.arange(nqb) * QB)[:, None, None] + i_[None, :, None] - i_[None, None, :]
    nw = WINDOW // QB + 1
    return dict(
        bcmp=_bias_rows(rel_bias, t - cend),
        tsel=_bias_rows(rel_bias, d),
        covt=_cover_t(ns_pad, nch),
        e3=_expand_mat(LANES, T).reshape(LANES, nqb, QB).transpose(1, 0, 2),
        wm=((d[:nw] >= 0) & (d[:nw] < WINDOW)).astype(F32),
        cm=(d[:2] >= 0).astype(F32),
        gx=_gate_expand())


def _sample_tables(rel_bias, past, tp, t_real):
    SL = past + LANES
    nblk = past // CMP_STRIDE
    ns = -(-(past + t_real) // SEL_BLOCK)
    ns_pad = -(-ns // LANES) * LANES
    WK = WINDOW + LANES
    i_ = jnp.arange(tp)
    t = past + i_
    rows = lambda dist: _bias_rows(rel_bias, dist[None])[0]
    return dict(
        bcmp=rows(t[:, None] - (jnp.arange(nblk) * CMP_STRIDE + CMP_BLOCK - 1)[None, :]),
        bsel=rows(t[:, None] - jnp.arange(SL)[None, :]),
        bwin=rows(i_[:, None] + WINDOW - jnp.arange(WK)[None, :]),
        cov=_cover_t(ns_pad, nblk).T, emat=_expand_mat(ns_pad, SL))


def _proj_in_kernel(x_ref, g_ref, w_ref, o_ref, hn_ref):
    @pl.when(pl.program_id(1) == 0)
    def _():
        x = x_ref[...]
        ms = jnp.mean(x * x, axis=-1, keepdims=True)
        hn_ref[...] = (x * lax.rsqrt(ms + EPS) * g_ref[...]).astype(BF16)

    o_ref[...] = jnp.dot(hn_ref[...], w_ref[...], preferred_element_type=F32)


def _proj_in(x2, norm_g, w_perm):
    n = x2.shape[0]
    tm = min(1024, n)
    tn = 1920
    return pl.pallas_call(
        _proj_in_kernel,
        grid=(n // tm, N_PERM // tn),
        in_specs=[pl.BlockSpec((tm, D_MODEL), lambda i, j: (i, 0)),
                  pl.BlockSpec((1, D_MODEL), lambda i, j: (0, 0)),
                  pl.BlockSpec((D_MODEL, tn), lambda i, j: (0, j))],
        out_specs=pl.BlockSpec((tm, tn), lambda i, j: (i, j)),
        out_shape=jax.ShapeDtypeStruct((n, N_PERM), F32),
        scratch_shapes=[pltpu.VMEM((tm, D_MODEL), BF16)],
        compiler_params=_cparams(("parallel", "arbitrary")),
        name="proj_in",
    )(x2, norm_g.reshape(1, D_MODEL), w_perm)


def _prep_a_kernel(q_ref, kv0_ref, kv1_ref, kv2_ref, bdq_ref, bdk_ref, qg_ref, kg_ref, qo_ref, ro_ref):
    q = q_ref[...]
    ms = jnp.dot(q * q, bdq_ref[...], precision=HI, preferred_element_type=F32) * (1.0 / A_HD)
    qo_ref[...] = q * lax.rsqrt(ms + EPS) * qg_ref[...] * (A_HD ** -0.5)
    for b, kv_ref in enumerate((kv0_ref, kv1_ref, kv2_ref)):
        k = kv_ref[:, 0:128]
        ms = jnp.dot(k * k, bdk_ref[...], precision=HI, preferred_element_type=F32) * (1.0 / A_HD)
        ro_ref[2 * b] = k * lax.rsqrt(ms + EPS) * kg_ref[b:b + 1, :]
        ro_ref[2 * b + 1] = kv_ref[:, 128:256]


def _prep_a(u2, a_qn, a_kn):
    n = u2.shape[0]
    tm = min(512, n)
    bd = lambda w: (jnp.arange(w)[:, None] // A_HD == jnp.arange(w)[None, :] // A_HD).astype(F32)
    qg = jnp.tile(a_qn, A_HEADS).reshape(1, W_A)
    kg = jnp.tile(a_kn, (1, A_KV))
    const = lambda shape: pl.BlockSpec(shape, lambda i: (0, 0))
    kvs = lambda b: pl.BlockSpec((tm, 256), lambda i: (i, OFF['a_kv'] // 256 + b))
    return pl.pallas_call(
        _prep_a_kernel,
        grid=(n // tm,),
        in_specs=[pl.BlockSpec((tm, W_A), lambda i: (i, OFF['a_q'] // W_A)), kvs(0), kvs(1), kvs(2),
                  const((W_A, W_A)), const((128, 128)), const((1, W_A)), const((3, 128))],
        out_specs=[pl.BlockSpec((tm, W_A), lambda i: (i, 0)), pl.BlockSpec((6, tm, 128), lambda i: (0, i, 0))],
        out_shape=[jax.ShapeDtypeStruct((n, W_A), F32), jax.ShapeDtypeStruct((6, n, 128), F32)],
        compiler_params=_cparams(("parallel",)),
        name="prep_a",
    )(u2, u2, u2, u2, bd(W_A), bd(128), qg, kg)


def _stack_heads(q, g):
    return jnp.concatenate([q[:, (A_REP * g + r) * A_HD:(A_REP * g + r + 1) * A_HD] for r in range(A_REP)], axis=0)


def _rep_rows(x):
    return jnp.concatenate([x] * A_REP, axis=0)


def _masked_softmax(s, valid):
    sm = jnp.where(valid, s, NEG)
    m = jnp.max(sm, axis=-1, keepdims=True)
    e = jnp.where(valid, jnp.exp(sm - m), 0.0)
    l = jnp.sum(e, axis=-1, keepdims=True)
    return e / jnp.maximum(l, TINY)


def _select_blocks(pc_sum, covt_ref, st_ref, t_row, ns, n_top):
    ns_pad = covt_ref.shape[0]
    imp = lax.dot_general(covt_ref[...], pc_sum, NT, precision=HI, preferred_element_type=F32)
    jj = lax.broadcasted_iota(jnp.int32, (ns_pad, LANES), 0)
    tt = jnp.broadcast_to(t_row, (ns_pad, LANES))
    cur = tt // SEL_BLOCK
    forced = (jj == 0) | (jj == cur) | (jj == cur - 1)
    future = jj * SEL_BLOCK > tt
    score = jnp.where(future, NEG, jnp.where(forced, -NEG, imp))
    score = jnp.where(jj < ns, score, -jnp.inf)
    st_ref[...] = score

    def beats(k, rank):
        row = jnp.broadcast_to(st_ref[pl.ds(k, 1), :], (ns_pad, LANES))
        b = (row > score) | ((row == score) & (k < jj))
        return rank + jnp.where(b, 1.0, 0.0)

    rank = jnp.zeros((ns_pad, LANES), F32)
    if ns <= 32:
        for k in range(ns):
            rank = beats(k, rank)
    else:
        rank = lax.fori_loop(0, ns, beats, rank)
    return jnp.where(rank < n_top, 1.0, 0.0)


def _pad_rows(x, rows):
    if x.shape[0] == rows:
        return x
    return jnp.concatenate([x, jnp.zeros((rows - x.shape[0], x.shape[1]), x.dtype)], axis=0)


def _select_rows(pc_sum, cov_ref, sel_ref, t0, ns, n_top, nq):
    Q = pc_sum.shape[0]
    ns_pad = cov_ref.shape[1]
    nk = -(-ns // 8) * 8
    imp = jnp.dot(pc_sum, cov_ref[...], precision=HI, preferred_element_type=F32)
    jj = lax.broadcasted_iota(jnp.int32, (Q, ns_pad), 1)
    tt = lax.broadcasted_iota(jnp.int32, (Q, ns_pad), 0) + t0
    cur = tt // SEL_BLOCK
    forced = (jj == 0) | (jj == cur) | (jj == cur - 1)
    future = jj * SEL_BLOCK > tt
    score = jnp.where(future, NEG, jnp.where(forced, -NEG, imp))
    score = jnp.where(jj < ns, score, -jnp.inf)
    score_col = _pad_rows(score, LANES).T
    kk = lax.broadcasted_iota(jnp.int32, (nk, ns_pad), 0)
    jl = lax.broadcasted_iota(jnp.int32, (nk, ns_pad), 1)
    sel_ref[...] = jnp.zeros(sel_ref.shape, F32)
    for i in range(nq):
        col = jnp.broadcast_to(score_col[0:nk, i:i + 1], (nk, ns_pad))
        row = jnp.broadcast_to(score[i:i + 1, :], (nk, ns_pad))
        beats = (col > row) | ((col == row) & (kk < jl))
        rank = jnp.sum(jnp.where(beats, 1.0, 0.0), axis=0, keepdims=True)
        sel_ref[i:i + 1, :] = jnp.where(rank < n_top, 1.0, 0.0)
    return sel_ref[...]


def _compress(src_ref, nrow, wk_ref, wv_ref, pbuf_ref, kc_ref, vc_ref):
    acck = jnp.zeros((nrow, 256), F32)
    accv = jnp.zeros((nrow, 256), F32)
    for j in range(CMP_STRIDE):
        xk = src_ref[0, pl.ds(j, nrow, stride=CMP_STRIDE), :].astype(BF16)
        xv = src_ref[1, pl.ds(j, nrow, stride=CMP_STRIDE), :].astype(BF16)
        acck = acck + jnp.dot(xk, wk_ref[j], preferred_element_type=F32)
        accv = accv + jnp.dot(xv, wv_ref[j], preferred_element_type=F32)
    nout = kc_ref.shape[0]
    for acc, dst in ((acck, kc_ref), (accv, vc_ref)):
        pbuf_ref[0:nrow, :] = acc[:, 128:256]
        dst[...] = acc[0:nout, 0:128] + pbuf_ref[pl.ds(1, nout), :]


def _gated_sum(gates, gx_ref, ocat_ref):
    gh, gl = _split(gates)
    gexp = (jnp.dot(gh, gx_ref[...], preferred_element_type=F32)
            + jnp.dot(gl, gx_ref[...], preferred_element_type=F32))
    out = gexp[:, 0:W_A] * ocat_ref[0]
    for c in range(1, N_BRANCH):
        out = out + gexp[:, c * W_A:(c + 1) * W_A] * ocat_ref[c]
    return out


def _nsa_prompt_kernel(q_ref, rows_ref, small_ref, wk_ref, wv_ref, bcmp_ref, tsel_ref, covt_ref, e3_ref,
                       wm_ref, cm_ref, gx_ref,
                       o_ref, kc_ref, vc_ref, pbuf_ref, sel_ref, acc_ref, m_ref, st_ref, ocat_ref,
                       *, ns, n_top):
    bi = pl.program_id(1)
    T = rows_ref.shape[1]
    nch = T // CMP_STRIDE
    QB = Q_BLOCK
    SEL, WIN = 1, 2

    @pl.when(bi == 0)
    def _():
        pbuf_ref[...] = jnp.zeros(pbuf_ref.shape, F32)
        _compress(rows_ref, nch, wk_ref, wv_ref, pbuf_ref, kc_ref, vc_ref)

    t0 = bi * QB
    q = q_ref[0]
    tc = lax.broadcasted_iota(jnp.int32, (QB, nch), 0) + t0
    cend = lax.broadcasted_iota(jnp.int32, (QB, nch), 1) * CMP_STRIDE + (CMP_BLOCK - 1)
    cvalid = _rep_rows(tc - cend >= 0)
    t_row = lax.broadcasted_iota(jnp.int32, (1, LANES), 1) + t0
    lane = lax.broadcasted_iota(jnp.int32, (QB, LANES), 1)
    in_g = [(lane >= g * A_HD) & (lane < (g + 1) * A_HD) for g in range(A_KV)]
    qg = [_stack_heads(q, g).astype(BF16) for g in range(A_KV)]

    def put_heads(c, g, o):
        for r in range(A_REP):
            h = A_REP * g + r
            ocat_ref[c, :, h * A_HD:(h + 1) * A_HD] = o[r * QB:(r + 1) * QB]

    for g in range(A_KV):
        kc = kc_ref[:, g * A_HD:(g + 1) * A_HD].astype(BF16)
        vc = vc_ref[:, g * A_HD:(g + 1) * A_HD].astype(BF16)
        s = lax.dot_general(qg[g], kc, NT, preferred_element_type=F32) + bcmp_ref[0, g]
        p_c = _masked_softmax(s, cvalid)
        put_heads(0, g, jnp.dot(p_c.astype(BF16), vc, preferred_element_type=F32))
        pc_sum = p_c[0:QB] + p_c[QB:2 * QB] + p_c[2 * QB:3 * QB] + p_c[3 * QB:4 * QB]
        sel_t = _select_blocks(pc_sum, covt_ref, st_ref, t_row, ns, n_top)
        sel_ref[g] = _pad_rows(sel_t, LANES).T.astype(BF16)

    m_ref[...] = jnp.full(m_ref.shape, NEG, F32)
    acc_ref[...] = jnp.zeros(acc_ref.shape, F32)

    def tile(br, g, kb, nb=1):
        ci = 2 * (br - 1) + g
        ks, vs, bs, ms = [], [], [], []
        for j in range(nb):
            off = pl.multiple_of((kb + j) * QB, QB)
            ks.append(rows_ref[2 * br, pl.ds(off, QB), g * A_HD:(g + 1) * A_HD])
            vs.append(jnp.where(in_g[g], rows_ref[2 * br + 1, pl.ds(off, QB), :], 1.0))
            bs.append(tsel_ref[bi - kb - j, g])
            if br == SEL:
                mt = jnp.dot(sel_ref[g], e3_ref[kb + j], preferred_element_type=F32)
                ms.append(mt * cm_ref[jnp.minimum(bi - kb, 1)] if nb == 1 else mt)
            else:
                ms.append(wm_ref[bi - kb - j])
        cat = lambda xs, ax: xs[0] if nb == 1 else jnp.concatenate(xs, axis=ax)
        k = cat(ks, 0).astype(BF16)
        v1 = cat(vs, 0).astype(BF16)
        sc = lax.dot_general(qg[g], k, NT, preferred_element_type=F32) + cat(bs, 1)
        valid = _rep_rows(cat(ms, 1)) > 0.5
        sm = jnp.where(valid, sc, NEG)
        m_prev = m_ref[ci]
        m_new = jnp.maximum(m_prev, jnp.max(sm, axis=-1, keepdims=True))
        p = jnp.where(valid, jnp.exp(sm - cat([m_new] * nb, 1)), 0.0)
        acc_ref[ci] = jnp.exp(m_prev - m_new) * acc_ref[ci] + jnp.dot(p.astype(BF16), v1,
                                                                       preferred_element_type=F32)
        m_ref[ci] = m_new

    def body_sel(kb, carry):
        for g in range(A_KV):
            tile(SEL, g, kb)
        return carry

    def body_sel2(pair, carry):
        for g in range(A_KV):
            tile(SEL, g, 2 * pair, nb=2)
        return carry

    def body_both(kb, carry):
        for g in range(A_KV):
            tile(SEL, g, kb)
            tile(WIN, g, kb)
        return carry

    lo = jnp.maximum(bi - WINDOW // QB, 0)
    lax.fori_loop(0, lo // 2, body_sel2, 0)
    lax.fori_loop(2 * (lo // 2), lo, body_sel, 0)
    lax.fori_loop(lo, bi + 1, body_both, 0)

    for br in (SEL, WIN):
        for g in range(A_KV):
            acc = acc_ref[2 * (br - 1) + g]
            linv = pltpu.roll(1.0 / jnp.maximum(acc, TINY), A_HD, axis=1)
            put_heads(br, g, (acc * linv)[:, g * A_HD:(g + 1) * A_HD])

    o_ref[0] = _gated_sum(jax.nn.sigmoid(small_ref[0]), gx_ref, ocat_ref)


def _nsa_prompt(qn3, rows6, u3, wk, wv, tb):
    B, T, _ = qn3.shape
    QB = Q_BLOCK
    nqb = T // QB
    nch = T // CMP_STRIDE
    ns = T // SEL_BLOCK
    ns_pad = tb['covt'].shape[0]
    n_top = min(TOP_N, ns)
    nw = tb['wm'].shape[0]
    kern = functools.partial(_nsa_prompt_kernel, ns=ns, n_top=n_top)
    c2 = lambda shape: pl.BlockSpec(shape, lambda b, i: (0,) * len(shape))
    return pl.pallas_call(
        kern,
        grid=(B, nqb),
        in_specs=[pl.BlockSpec((1, QB, W_A), lambda b, i: (b, i, 0)),
                  pl.BlockSpec((6, T, 128), lambda b, i: (0, b, 0)),
                  pl.BlockSpec((1, QB, LANES), lambda b, i: (b, i, OFF['small'] // LANES)),
                  c2((CMP_STRIDE, 128, 256)), c2((CMP_STRIDE, 128, 256)),
                  pl.BlockSpec((1, A_KV, A_REP * QB, nch), lambda b, i: (i, 0, 0, 0)),
                  c2((nqb, A_KV, A_REP * QB, QB)),
                  c2((ns_pad, nch)), c2((nqb, LANES, QB)), c2((nw, QB, QB)), c2((2, QB, QB)),
                  c2((LANES, N_BRANCH * W_A))],
        out_specs=pl.BlockSpec((1, QB, W_A), lambda b, i: (b, i, 0)),
        out_shape=jax.ShapeDtypeStruct((B, T, W_A), F32),
        scratch_shapes=[pltpu.VMEM((nch, 128), F32), pltpu.VMEM((nch, 128), F32),
                        pltpu.VMEM((nch + 8, 128), F32),
                        pltpu.VMEM((A_KV, QB, LANES), BF16),
                        pltpu.VMEM((2 * A_KV, A_REP * QB, LANES), F32),
                        pltpu.VMEM((2 * A_KV, A_REP * QB, LANES), F32),
                        pltpu.VMEM((ns_pad, LANES), F32),
                        pltpu.VMEM((N_BRANCH, QB, W_A), F32)],
        compiler_params=_cparams(("parallel", "arbitrary")),
        name="nsa_prompt",
    )(qn3, rows6, u3, wk, wv, tb['bcmp'], tb['tsel'], tb['covt'], tb['e3'], tb['wm'], tb['cm'], tb['gx'])


def _nsa_sample_kernel(pt_ref, *refs, pg, past, tp, t_real, ns, n_top):
    cmp_refs = refs[0:pg]
    sel_refs = refs[pg:2 * pg]
    (q_ref, rows_ref, small_ref, win_ref, wk_ref, wv_ref, bcmp_ref, bsel_ref, bwin_ref, cov_ref, e_ref,
     o_ref, cslab, sslab, kc_ref, vc_ref, pbuf_ref, st_ref, s_ref) = refs[2 * pg:]
    p = pl.program_id(1)
    npg = pl.num_programs(1)
    SL = cslab.shape[1]
    n_pages = past // PAGE_SIZE
    nblk = kc_ref.shape[0]
    WK = WINDOW + LANES

    for k in range(pg):
        page = p * pg + k
        off = pl.multiple_of(page * PAGE_SIZE, PAGE_SIZE)
        for kv in range(2):
            cslab[kv, pl.ds(off, PAGE_SIZE), :] = cmp_refs[k][kv * 128:(kv + 1) * 128, :].T
            sslab[kv, page] = sel_refs[k][kv * 128:(kv + 1) * 128, :]

    @pl.when(p == npg - 1)
    def _():
        new_t = [_pad_rows(rows_ref[i], LANES).T for i in range(2, 6)]
        for kv in range(2):
            cslab[kv, past:past + tp, :] = rows_ref[kv]
            cslab[kv, past + tp:SL, :] = jnp.zeros((SL - past - tp, 128), F32)
            sslab[kv, n_pages] = new_t[kv]
        pbuf_ref[...] = jnp.zeros(pbuf_ref.shape, F32)
        _compress(cslab, nblk + 8, wk_ref, wv_ref, pbuf_ref, kc_ref, vc_ref)

        q = q_ref[0]
        gates = jax.nn.sigmoid(small_ref[0])
        R = A_REP * tp
        ti = lax.broadcasted_iota(jnp.int32, (tp, SL), 0) + past
        causal = ti - lax.broadcasted_iota(jnp.int32, (tp, SL), 1) >= 0
        tc = lax.broadcasted_iota(jnp.int32, (tp, nblk), 0) + past
        cend = lax.broadcasted_iota(jnp.int32, (tp, nblk), 1) * CMP_STRIDE + (CMP_BLOCK - 1)
        cvalid = _rep_rows(tc - cend >= 0)
        wd = (lax.broadcasted_iota(jnp.int32, (tp, WK), 0) + WINDOW
              - lax.broadcasted_iota(jnp.int32, (tp, WK), 1))
        wvalid = _rep_rows((wd >= 0) & (wd < WINDOW))
        npt = n_pages + 1
        nck = 5 if npt % 5 == 0 else 1
        cpt = npt // nck

        for g in range(A_KV):
            gs = slice(g * A_HD, (g + 1) * A_HD)
            qg = _stack_heads(q, g).astype(BF16)
            s = lax.dot_general(qg, kc_ref[:, gs].astype(BF16), NT, preferred_element_type=F32) + bcmp_ref[g]
            p_c = _masked_softmax(s, cvalid)
            o_c = jnp.dot(p_c.astype(BF16), vc_ref[:, gs].astype(BF16), preferred_element_type=F32)
            pc_sum = p_c[0:tp] + p_c[tp:2 * tp] + p_c[2 * tp:3 * tp] + p_c[3 * tp:4 * tp]

            sel = _select_rows(pc_sum, cov_ref, st_ref, past, ns, n_top, t_real).astype(BF16)
            kmask = jnp.dot(sel, e_ref[...], preferred_element_type=F32)
            svalid = _rep_rows(causal & (kmask > 0.5))

            def sel_t(kv, c):
                return jnp.concatenate([sslab[kv, t, gs, :] for t in range(c * cpt, (c + 1) * cpt)],
                                       axis=1).astype(BF16)

            ck = cpt * LANES
            for c in range(nck):
                s_ref[:, c * ck:(c + 1) * ck] = jnp.dot(qg, sel_t(0, c), preferred_element_type=F32)
            p_s = _masked_softmax(s_ref[...] + bsel_ref[g], svalid)
            s_ref[...] = p_s
            o_s = jnp.zeros((R, A_HD), F32)
            for c in range(nck):
                o_s = o_s + lax.dot_general(s_ref[:, c * ck:(c + 1) * ck].astype(BF16), sel_t(1, c), NT,
                                            preferred_element_type=F32)

            kw = jnp.concatenate([win_ref[gs, :], new_t[2][gs, :]], axis=1).astype(BF16)
            vw = jnp.concatenate([win_ref[128 + g * A_HD:128 + (g + 1) * A_HD, :], new_t[3][gs, :]],
                                 axis=1).astype(BF16)
            sw = jnp.dot(qg, kw, preferred_element_type=F32) + bwin_ref[g]
            p_w = _masked_softmax(sw, wvalid)
            o_w = lax.dot_general(p_w.astype(BF16), vw, NT, preferred_element_type=F32)

            for r in range(A_REP):
                h = A_REP * g + r
                rs = slice(r * tp, (r + 1) * tp)
                out = (gates[:, 3 * h:3 * h + 1] * o_c[rs] + gates[:, 3 * h + 1:3 * h + 2] * o_s[rs]
                       + gates[:, 3 * h + 2:3 * h + 3] * o_w[rs])
                o_ref[0, :, h * A_HD:(h + 1) * A_HD] = out


def _nsa_sample(qn3, rows6, u3, pool_cmp, pool_sel, win_t, layer, page_table, wk, wv, tb, past, t_real):
    B, tp, _ = qn3.shape
    n_pages = past // PAGE_SIZE
    pg = 8 if n_pages % 8 == 0 else n_pages
    npg = n_pages // pg
    SL = past + LANES
    nblk = past // CMP_STRIDE
    ns = -(-(past + t_real) // SEL_BLOCK)
    ns_pad = tb['cov'].shape[1]
    n_top = min(TOP_N, ns)
    WK = WINDOW + LANES
    kern = functools.partial(_nsa_sample_kernel, pg=pg, past=past, tp=tp, t_real=t_real, ns=ns, n_top=n_top)

    def page_spec(k):
        return pl.BlockSpec((None, None, 256, PAGE_SIZE), lambda b, p, pt: (layer, pt[b, p * pg + k], 0, 0))

    def c_(shape):
        return pl.BlockSpec(shape, lambda b, p, pt: (0,) * len(shape))

    R = A_REP * tp
    grid_spec = pltpu.PrefetchScalarGridSpec(
        num_scalar_prefetch=1,
        grid=(B, npg),
        in_specs=([page_spec(k) for k in range(pg)] + [page_spec(k) for k in range(pg)]
                  + [pl.BlockSpec((1, tp, W_A), lambda b, p, pt: (b, 0, 0)),
                     pl.BlockSpec((6, tp, 128), lambda b, p, pt: (0, b, 0)),
                     pl.BlockSpec((1, tp, LANES), lambda b, p, pt: (b, 0, OFF['small'] // LANES)),
                     pl.BlockSpec((None, None, 256, WINDOW), lambda b, p, pt: (layer, b, 0, 0)),
                     c_((CMP_STRIDE, 128, 256)), c_((CMP_STRIDE, 128, 256)),
                     c_((A_KV, R, nblk)), c_((A_KV, R, SL)), c_((A_KV, R, WK)),
                     c_((nblk, ns_pad)), c_((ns_pad, SL))]),
        out_specs=pl.BlockSpec((1, tp, W_A), lambda b, p, pt: (b, 0, 0)),
        scratch_shapes=[pltpu.VMEM((2, SL, 128), F32), pltpu.VMEM((2, n_pages + 1, 128, PAGE_SIZE), F32),
                        pltpu.VMEM((nblk, 128), F32), pltpu.VMEM((nblk, 128), F32),
                        pltpu.VMEM((nblk + 16, 128), F32),
                        pltpu.VMEM((tp, ns_pad), F32), pltpu.VMEM((R, SL), F32)],
    )
    return pl.pallas_call(
        kern,
        grid_spec=grid_spec,
        out_shape=jax.ShapeDtypeStruct((B, tp, W_A), F32),
        compiler_params=_cparams(("parallel", "arbitrary")),
        name="nsa_sample",
    )(page_table, *([pool_cmp] * pg), *([pool_sel] * pg), qn3, rows6, u3, win_t, wk, wv,
      tb['bcmp'], tb['bsel'], tb['bwin'], tb['cov'], tb['emat'])


def _tri(L, lower_incl):
    r = lax.broadcasted_iota(jnp.int32, (L, L), 0)
    c = lax.broadcasted_iota(jnp.int32, (L, L), 1)
    return (r >= c) if lower_incl else (r > c)


def _mlstm_kernel(x_ref, small_ref, smallt_ref, brow_ref, bcol_ref, c0_ref, n0_ref, m0_ref,
                  h_ref, co_ref, no_ref, mo_ref, c_s, n_s, m_s, *, L, t_real):
    c = pl.program_id(1)
    nc = pl.num_programs(1)

    @pl.when(c == 0)
    def _():
        c_s[...] = c0_ref[0]
        n_s[...] = n0_ref[0]
        m_s[...] = m0_ref[0]

    sm = small_ref[0]
    smt = smallt_ref[0, 0]
    li_col = sm[:, SM_MIF:SM_MIF + 4] + brow_ref[0:1, 0:4]
    lf_col = _log_sigmoid(sm[:, SM_MIF + 4:SM_MIF + 8] + brow_ref[0:1, 4:8])
    li_row = smt[0:4, :] + bcol_ref[0:4, :]
    lf_row = _log_sigmoid(smt[4:8, :] + bcol_ref[4:8, :])
    if t_real % L:
        tcol = lax.broadcasted_iota(jnp.int32, (L, 4), 0) + c * L
        trow = lax.broadcasted_iota(jnp.int32, (4, L), 1) + c * L
        li_col = jnp.where(tcol < t_real, li_col, NEG)
        lf_col = jnp.where(tcol < t_real, lf_col, 0.0)
        li_row = jnp.where(trow < t_real, li_row, NEG)
        lf_row = jnp.where(trow < t_real, lf_row, 0.0)
    low = _tri(L, True)
    b_col = jnp.dot(low.astype(F32), lf_col, precision=HI, preferred_element_type=F32)
    b_row = lax.dot_general(lf_row, low.astype(F32), NT, precision=HI, preferred_element_type=F32)

    for h in range(M_HEADS):
        hs = slice(h * M_HD, (h + 1) * M_HD)
        q = x_ref[0, :, h * M_HD:(h + 1) * M_HD]
        k = x_ref[0, :, W_M + h * M_HD:W_M + (h + 1) * M_HD] * (M_HD ** -0.5)
        v = x_ref[0, :, 2 * W_M + h * M_HD:2 * W_M + (h + 1) * M_HD]
        bc = b_col[:, h:h + 1]
        br = b_row[h:h + 1, :]
        Dm = jnp.where(low, bc - br + li_row[h:h + 1, :], NEG)
        m_prev = m_s[h][:, 0:1]
        a = bc + m_prev
        mt = jnp.maximum(a, jnp.max(Dm, axis=-1, keepdims=True))
        S = lax.dot_general(q, k, NT, preferred_element_type=F32) * jnp.exp(Dm - mt)
        inter = jnp.exp(a - mt)
        C = c_s[h]
        n = n_s[h]
        num = inter * jnp.dot(q, C, preferred_element_type=F32) + jnp.dot(S, v, preferred_element_type=F32)
        den = inter * jnp.sum(q * n, axis=-1, keepdims=True) + jnp.sum(S, axis=-1, keepdims=True)
        h_ref[0, :, hs] = num / jnp.maximum(jnp.abs(den), jnp.exp(-mt))
        bL = bc[L - 1:L, :]
        wlog = bL - bc + li_col[:, h:h + 1]
        m_new = jnp.maximum(bL + m_prev, jnp.max(wlog, axis=0, keepdims=True))
        ws = jnp.exp(wlog - m_new)
        dec = jnp.exp(bL + m_prev - m_new)
        kw = k * ws
        c_s[h] = dec * C + lax.dot_general(kw, v, TN, preferred_element_type=F32)
        n_s[h] = dec * n + jnp.sum(kw, axis=0, keepdims=True)
        m_s[h] = jnp.broadcast_to(m_new, (1, LANES))

    @pl.when(c == nc - 1)
    def _():
        co_ref[0] = c_s[...]
        no_ref[0] = n_s[...]
        mo_ref[0] = m_s[...]


def _mlstm(u3, smallt, m_bi, m_bf, C0, n0, m0, L, t_real):
    B, T, _ = u3.shape
    nc = T // L
    brow = jnp.concatenate([m_bi, m_bf]).reshape(1, 8)
    bcol = jnp.concatenate([m_bi, m_bf]).reshape(8, 1)
    n0 = n0.reshape(B, M_HEADS, 1, M_HD)
    m0 = jnp.broadcast_to(m0[:, :, None, None], (B, M_HEADS, 1, LANES))
    kern = functools.partial(_mlstm_kernel, L=L, t_real=t_real)
    st = lambda shape: pl.BlockSpec(shape, lambda b, c: (b,) + (0,) * (len(shape) - 1))
    h, Cn, nn, mn = pl.pallas_call(
        kern,
        grid=(B, nc),
        in_specs=[pl.BlockSpec((1, L, 3 * W_M), lambda b, c: (b, c, OFF['m_qkv'] // (3 * W_M))),
                  pl.BlockSpec((1, L, LANES), lambda b, c: (b, c, OFF['small'] // LANES)),
                  pl.BlockSpec((1, 1, 16, L), lambda b, c: (b, c, 0, 0)),
                  pl.BlockSpec((1, 8), lambda b, c: (0, 0)), pl.BlockSpec((8, 1), lambda b, c: (0, 0)),
                  st((1, M_HEADS, M_HD, M_HD)), st((1, M_HEADS, 1, M_HD)), st((1, M_HEADS, 1, LANES))],
        out_specs=[pl.BlockSpec((1, L, W_M), lambda b, c: (b, c, 0)),
                   st((1, M_HEADS, M_HD, M_HD)), st((1, M_HEADS, 1, M_HD)), st((1, M_HEADS, 1, LANES))],
        out_shape=[jax.ShapeDtypeStruct((B, T, W_M), F32),
                   jax.ShapeDtypeStruct((B, M_HEADS, M_HD, M_HD), F32),
                   jax.ShapeDtypeStruct((B, M_HEADS, 1, M_HD), F32),
                   jax.ShapeDtypeStruct((B, M_HEADS, 1, LANES), F32)],
        scratch_shapes=[pltpu.VMEM((M_HEADS, M_HD, M_HD), F32), pltpu.VMEM((M_HEADS, 1, M_HD), F32),
                        pltpu.VMEM((M_HEADS, 1, LANES), F32)],
        compiler_params=_cparams(("parallel", "arbitrary")),
        name="mlstm",
    )(u3, u3, smallt, brow, bcol, C0, n0, m0)
    return h, Cn, nn.reshape(B, M_HEADS, M_HD), mn[:, :, 0, 0]


def _gdn_a_kernel(x_ref, small_ref, smallt_ref, cw_ref, prow_ref, pcol_ref, buf_ref,
                  u_ref, w_ref, qe_ref, kd_ref, at_ref, eg_ref, xbuf, *, L, CB, t_real):
    c = pl.program_id(1)
    PRE = 8
    R = CB * L

    @pl.when(c == 0)
    def _():
        xbuf[0:PRE, :] = buf_ref[0]

    xbuf[PRE:PRE + R, :] = x_ref[0]
    conv = jnp.zeros((R, 3 * W_G), F32)
    for j in range(CONV_W):
        conv = conv + xbuf[pl.ds(PRE - (CONV_W - 1) + j, R), :] * cw_ref[j:j + 1, :]
    xbuf[0:PRE, :] = xbuf[R:R + PRE, :]
    conv = _silu(conv)

    sm = small_ref[0]
    g_all = -jnp.exp(prow_ref[0:1, 0:4]) * _softplus(sm[:, SM_GAB:SM_GAB + 4] + prow_ref[1:2, 0:4])
    beta_all = jax.nn.sigmoid(sm[:, SM_GAB + 4:SM_GAB + 8])
    lowf = _tri(L, True).astype(F32)

    for cb in range(CB):
        rs = slice(cb * L, (cb + 1) * L)
        smt = smallt_ref[0, cb]
        g_col = g_all[rs]
        beta_col = beta_all[rs]
        g_row = -jnp.exp(pcol_ref[0:4, 0:1]) * _softplus(smt[8:12, :] + pcol_ref[0:4, 1:2])
        if t_real % L:
            t0 = (c * CB + cb) * L
            tcol = lax.broadcasted_iota(jnp.int32, (L, 4), 0) + t0
            trow = lax.broadcasted_iota(jnp.int32, (4, L), 1) + t0
            g_col = jnp.where(tcol < t_real, g_col, 0.0)
            beta_col = jnp.where(tcol < t_real, beta_col, 0.0)
            g_row = jnp.where(trow < t_real, g_row, 0.0)
        G_col = jnp.dot(lowf, g_col, precision=HI, preferred_element_type=F32)
        G_row = lax.dot_general(g_row, lowf, NT, precision=HI, preferred_element_type=F32)

        qs, ks, kbs, rhss = [], [], [], []
        for h in range(G_HEADS):
            hs = slice(h * G_HD, (h + 1) * G_HD)
            cq = conv[rs, h * G_HD:(h + 1) * G_HD]
            ck = conv[rs, W_G + h * G_HD:W_G + (h + 1) * G_HD]
            v = conv[rs, 2 * W_G + h * G_HD:2 * W_G + (h + 1) * G_HD]
            q = cq * lax.rsqrt(jnp.sum(cq * cq, axis=-1, keepdims=True) + EPS) * (G_HD ** -0.5)
            k = ck * lax.rsqrt(jnp.sum(ck * ck, axis=-1, keepdims=True) + EPS)
            Gc = G_col[:, h:h + 1]
            bcol = beta_col[:, h:h + 1]
            kb = k * bcol
            eG = jnp.exp(Gc)
            GL = Gc[L - 1:L, :]
            qs.append(q)
            ks.append(k)
            kbs.append(kb)
            rhss.append(jnp.concatenate([v * bcol, kb * eG], axis=1))
            qe_ref[0, rs, hs] = q * eG
            kd_ref[0, rs, hs] = k * jnp.exp(GL - Gc)
            eg_ref[0, cb, h:h + 1, :] = jnp.broadcast_to(jnp.exp(GL), (1, LANES))
        eg_ref[0, cb, G_HEADS:8, :] = jnp.zeros((8 - G_HEADS, LANES), F32)

        H = G_HEADS
        lane_h = lax.broadcasted_iota(jnp.int32, (L, H * L), 1) // L
        blockmask = (lax.broadcasted_iota(jnp.int32, (H * L, H * L), 0) // L
                     == lax.broadcasted_iota(jnp.int32, (H * L, H * L), 1) // L)

        def pack_diag(full):
            out = full[0:L]
            for h in range(1, H):
                out = jnp.where(lane_h == h, full[h * L:(h + 1) * L], out)
            return out

        def block_diag(m, parts):
            if L % 16:
                return _split(jnp.where(blockmask, jnp.concatenate([m] * H, axis=0), 0.0))
            return tuple(jnp.where(blockmask, jnp.concatenate([p] * H, axis=0), jnp.zeros((), BF16))
                         for p in parts)

        Gc_p = jnp.concatenate([jnp.broadcast_to(G_col[:, h:h + 1], (L, L)) for h in range(H)], axis=1)
        Gr_p = jnp.concatenate([jnp.broadcast_to(G_row[h:h + 1, :], (L, L)) for h in range(H)], axis=1)
        row_p = lax.broadcasted_iota(jnp.int32, (L, H * L), 0)
        col_p = lax.broadcasted_iota(jnp.int32, (L, H * L), 1) % L
        low_p = row_p >= col_p
        dmask = jnp.where(low_p, jnp.exp(jnp.where(low_p, Gc_p - Gr_p, 0.0)), 0.0)
        k_st = _split(jnp.concatenate(ks, axis=0))
        A = pack_diag(_dot3(_split(jnp.concatenate(kbs, axis=0)), k_st, NT)) * jnp.where(row_p > col_p, dmask, 0.0)
        attn = pack_diag(lax.dot_general(jnp.concatenate(qs, axis=0).astype(BF16), k_st[0], NT,
                                         preferred_element_type=F32)) * dmask
        at_ref[0, rs, :] = attn
        X = jnp.where(row_p == col_p, 1.0, 0.0) - A
        As = _split(A)
        Pw = _dot3(As, block_diag(A, As))
        span = 2
        while span < L:
            Ps = _split(Pw)
            Pbd = block_diag(Pw, Ps)
            X = X + _dot3(_split(X), Pbd)
            span *= 2
            if span < L:
                Pw = _dot3(Ps, Pbd)
        sol = _dot3(block_diag(X, _split(X)), _split(jnp.concatenate(rhss, axis=0)))
        for h in range(H):
            hs = slice(h * G_HD, (h + 1) * G_HD)
            u_ref[0, rs, hs] = sol[h * L:(h + 1) * L, 0:G_HD]
            w_ref[0, rs, hs] = sol[h * L:(h + 1) * L, G_HD:2 * G_HD]


def _gdn_b_kernel(u_ref, w_ref, qe_ref, kd_ref, at_ref, eg_ref, s0_ref, o_ref, so_ref, s_s, *, L, BB):
    c = pl.program_id(1)
    nc = pl.num_programs(1)

    @pl.when(c == 0)
    def _():
        s_s[...] = s0_ref[...]

    for b in range(BB):
        for h in range(G_HEADS):
            hs = slice(h * G_HD, (h + 1) * G_HD)
            S = s_s[b, h]
            wq = jnp.concatenate([w_ref[b, :, hs], qe_ref[b, :, hs]], axis=0)
            r = jnp.dot(wq, S, preferred_element_type=F32)
            v_new = u_ref[b, :, hs] - r[0:L]
            o_ref[b, :, hs] = r[L:2 * L] + jnp.dot(at_ref[b, :, h * L:(h + 1) * L], v_new,
                                                   preferred_element_type=F32)
            s_s[b, h] = eg_ref[b, 0, h:h + 1, :] * S + lax.dot_general(kd_ref[b, :, hs], v_new, TN,
                                                                      preferred_element_type=F32)

    @pl.when(c == nc - 1)
    def _():
        so_ref[...] = s_s[...]


def _gdn(u3, smallt, g_conv, g_A_log, g_dt_bias, buf, S0, L, t_real):
    B, T, _ = u3.shape
    nc = T // L
    CB = 2 if nc % 2 == 0 else 1
    BB = 8 if B % 8 == 0 else B
    prow = jnp.stack([g_A_log, g_dt_bias])
    pcol = jnp.stack([g_A_log, g_dt_bias], axis=1)
    buf8 = jnp.concatenate([jnp.zeros((B, 8 - (CONV_W - 1), 3 * W_G), F32), buf], axis=1)
    st = lambda shape: pl.BlockSpec(shape, lambda b, c: (b,) + (0,) * (len(shape) - 1))
    cst = lambda shape: pl.BlockSpec(shape, lambda b, c: (0,) * len(shape))
    row = lambda nb, r, w: pl.BlockSpec((nb, r, w), lambda b, c: (b, c, 0))
    tok = lambda w: jax.ShapeDtypeStruct((B, T, w), F32)
    u, w, qe, kd, at, eg = pl.pallas_call(
        functools.partial(_gdn_a_kernel, L=L, CB=CB, t_real=t_real),
        grid=(B, nc // CB),
        in_specs=[pl.BlockSpec((1, CB * L, 3 * W_G), lambda b, c: (b, c, OFF['g_qkv'] // (3 * W_G))),
                  pl.BlockSpec((1, CB * L, LANES), lambda b, c: (b, c, OFF['small'] // LANES)),
                  pl.BlockSpec((1, CB, 16, L), lambda b, c: (b, c, 0, 0)),
                  cst((CONV_W, 3 * W_G)), cst((2, 4)), cst((4, 2)), st((1, 8, 3 * W_G))],
        out_specs=[row(1, CB * L, W_G)] * 4 + [row(1, CB * L, G_HEADS * L),
                                               pl.BlockSpec((1, CB, 8, LANES), lambda b, c: (b, c, 0, 0))],
        out_shape=[tok(W_G)] * 4 + [tok(G_HEADS * L), jax.ShapeDtypeStruct((B, nc, 8, LANES), F32)],
        scratch_shapes=[pltpu.VMEM((CB * L + 8, 3 * W_G), F32)],
        compiler_params=_cparams(("parallel", "arbitrary")),
        name="gdn_a",
    )(u3, u3, smallt, g_conv, prow, pcol, buf8)
    sblk = pl.BlockSpec((BB, G_HEADS, G_HD, G_HD), lambda b, c: (b, 0, 0, 0))
    return pl.pallas_call(
        functools.partial(_gdn_b_kernel, L=L, BB=BB),
        grid=(B // BB, nc),
        in_specs=[row(BB, L, W_G)] * 4 + [row(BB, L, G_HEADS * L),
                                          pl.BlockSpec((BB, 1, 8, LANES), lambda b, c: (b, c, 0, 0)), sblk],
        out_specs=[row(BB, L, W_G), sblk],
        out_shape=[tok(W_G), jax.ShapeDtypeStruct((B, G_HEADS, G_HD, G_HD), F32)],
        scratch_shapes=[pltpu.VMEM((BB, G_HEADS, G_HD, G_HD), F32)],
        compiler_params=_cparams(("parallel", "arbitrary")),
        name="gdn_b",
    )(u, w, qe, kd, at, eg, S0)


def _head_rmsnorm(x, gain_ref, nheads, hd):
    outs = []
    for h in range(nheads):
        xh = x[:, h * hd:(h + 1) * hd]
        ms = jnp.mean(xh * xh, axis=-1, keepdims=True)
        outs.append(xh * lax.rsqrt(ms + EPS) * gain_ref[...])
    return jnp.concatenate(outs, axis=1)


def _merge_kernel(x_ref, oa_ref, hm_ref, og_ref, az_ref, mo_ref, mz_ref, gz_ref, mg_ref,
                  wb_ref, wo_ref, mhn_ref, ghn_ref, y_ref):
    oa = oa_ref[...] * _silu(az_ref[...])
    om = _head_rmsnorm(hm_ref[...], mhn_ref, M_HEADS, M_HD) * jax.nn.sigmoid(mo_ref[...]) * _silu(mz_ref[...])
    og = _head_rmsnorm(og_ref[...], ghn_ref, G_HEADS, G_HD) * _silu(gz_ref[...])
    y = jnp.zeros(y_ref.shape, F32)
    for i, br in enumerate((oa, om, og)):
        proj = jnp.dot(br.astype(BF16), wb_ref[i], preferred_element_type=F32)
        y = y + jax.nn.sigmoid(mg_ref[:, i * D_MODEL:(i + 1) * D_MODEL]) * proj
    y_ref[...] = x_ref[...] + jnp.dot(y.astype(BF16), wo_ref[...], preferred_element_type=F32)


def _merge_out(x2, o_a, h_m, o_g, u2, w_branch, w_out, m_hn, g_hn):
    n = x2.shape[0]
    tm = min(512, n)
    row = lambda w, off: pl.BlockSpec((tm, w), lambda i: (i, off // w))
    cst = lambda shape: pl.BlockSpec(shape, lambda i: (0,) * len(shape))
    return pl.pallas_call(
        _merge_kernel,
        grid=(n // tm,),
        in_specs=[row(D_MODEL, 0), row(W_A, 0), row(W_M, 0), row(W_G, 0),
                  row(W_A, OFF['a_z']), row(W_M, OFF['m_o']), row(W_M, OFF['m_z']), row(W_G, OFF['g_z']),
                  row(N_BRANCH * D_MODEL, OFF['merge']),
                  cst((N_BRANCH, W_A, D_MODEL)), cst((D_MODEL, D_MODEL)), cst((1, M_HD)), cst((1, G_HD))],
        out_specs=row(D_MODEL, 0),
        out_shape=jax.ShapeDtypeStruct((n, D_MODEL), F32),
        compiler_params=_cparams(("parallel",)),
        name="merge_out",
    )(x2, o_a, h_m, o_g, u2, u2, u2, u2, u2, w_branch.astype(BF16), w_out.astype(BF16),
      m_hn.reshape(1, M_HD), g_hn.reshape(1, G_HD))


def _small_t(u3, L):
    B, T, _ = u3.shape
    s = u3[:, :, OFF['small'] + SM_MIF:OFF['small'] + SM_MIF + 16]
    return s.reshape(B, T // L, L, 16).transpose(0, 1, 3, 2)


def _layer(x, lw, tb, past, q_off):
    B, T, _ = x.shape
    x2 = x.reshape(B * T, D_MODEL)
    u2 = _proj_in(x2, lw['norm_g'], lw['w_perm'])
    if past is None:
        tp, L = T, math.gcd(T, M_CHUNK)
        u3 = u2.reshape(B, T, N_PERM)
        C0 = jnp.zeros((B, M_HEADS, M_HD, M_HD), F32)
        n0 = jnp.zeros((B, M_HEADS, M_HD), F32)
        m0 = jnp.zeros((B, M_HEADS), F32)
        S0 = jnp.zeros((B, G_HEADS, G_HD, G_HD), F32)
        buf = jnp.zeros((B, CONV_W - 1, 3 * W_G), F32)
    else:
        tp = -(-T // 8) * 8
        L = tp
        u3 = jnp.pad(u2.reshape(B, T, N_PERM), ((0, 0), (0, tp - T), (0, 0)))
        C0, n0, m0, S0, buf = past['mC'], past['mn'], past['mm'], past['gS'], past['gconv']
    up = u3.reshape(B * tp, N_PERM)
    qn, rows6 = _prep_a(up, lw['a_qn'], lw['a_kn'])
    qn3 = qn.reshape(B, tp, W_A)
    if past is None:
        o_a = _nsa_prompt(qn3, rows6, u3, lw['wk'], lw['wv'], tb)
    else:
        o_a = _nsa_sample(qn3, rows6, u3, past['cmp'], past['sel'], past['win_t'], past['layer'], past['page_table'],
                          lw['wk'], lw['wv'], tb, q_off, T)
    smallt = _small_t(u3, L)
    h_m, Cn, nn, mn = _mlstm(u3, smallt, lw['m_bi'], lw['m_bf'], C0, n0, m0, L, T)
    o_g, Sn = _gdn(u3, smallt, lw['g_conv'], lw['g_A_log'], lw['g_dt_bias'], buf, S0, L, T)
    if tp != T:
        o_a, h_m, o_g = o_a[:, :T], h_m[:, :T], o_g[:, :T]
    y2 = _merge_out(x2, o_a.reshape(B * T, W_A), h_m.reshape(B * T, W_M), o_g.reshape(B * T, W_G), u2,
                    lw['w_branch'], lw['w_out'], lw['m_hn'], lw['g_hn'])
    rows = rows6.reshape(3, 2, B, tp, A_KV, A_HD)[:, :, :, :T]
    new_cmp, new_sel, new_win = (jnp.moveaxis(rows[b], 0, 2) for b in range(3))
    g_qkv = u3[:, :T, OFF['g_qkv']:OFF['g_qkv'] + 3 * W_G]
    if past is None:
        win = new_win
    else:
        win = jnp.concatenate([past['win'], new_win], axis=1)
    full = jnp.concatenate([buf, g_qkv[:, max(T - (CONV_W - 1), 0):]], axis=1)
    state = dict(cmp=new_cmp, sel=new_sel, win=win[:, -min(WINDOW, win.shape[1]):],
                 mC=Cn, mn=nn, mm=mn, gS=Sn, gconv=full[:, -(CONV_W - 1):])
    return y2.reshape(B, T, D_MODEL), state


def kernel(x_prompt, x_sample, cache_cmp_kv, cache_sel_kv, cache_win_kv, state_mlstm_C, state_mlstm_n,
           state_mlstm_m, state_gdn_S, state_gdn_conv, page_table, norm_g, w_in, a_qn, a_kn, a_cmp_wk,
           a_cmp_wv, rel_bias, m_bi, m_bf, m_hn, g_conv, g_A_log, g_dt_bias, g_hn, w_branch, w_out):
    names = ('cmp', 'sel', 'win', 'mC', 'mn', 'mm', 'gS', 'gconv')
    st_p = {k: [] for k in names}
    st_s = {k: [] for k in names}
    past_len = page_table.shape[1] * PAGE_SIZE
    n_pool = cache_cmp_kv.shape[1]
    pos_minor = lambda c: jnp.swapaxes(c.reshape(c.shape[0], c.shape[1], c.shape[2], 2 * A_KV * A_HD), 2, 3)
    pool_cmp, pool_sel, win_t = pos_minor(cache_cmp_kv), pos_minor(cache_sel_kv), pos_minor(cache_win_kv)
    w_t = jnp.swapaxes(w_in, 1, 2)
    db, dt = x_sample.shape[0], x_sample.shape[1]
    tb_p = _prompt_tables(rel_bias, x_prompt.shape[1])
    tb_s = _sample_tables(rel_bias, past_len, -(-dt // 8) * 8, dt)
    y_p, y_s = x_prompt, x_sample
    for l in range(DEPTH):
        lw = dict(norm_g=norm_g[l], w_perm=_permute_w_in(w_t, l), a_qn=a_qn[l], a_kn=a_kn[l],
                  wk=_cmp_weights(a_cmp_wk[l]), wv=_cmp_weights(a_cmp_wv[l]),
                  m_bi=m_bi[l], m_bf=m_bf[l], m_hn=m_hn[l], g_conv=g_conv[l], g_A_log=g_A_log[l],
                  g_dt_bias=g_dt_bias[l], g_hn=g_hn[l], w_branch=w_branch[l], w_out=w_out[l])
        y_p, new_p = _layer(y_p, lw, tb_p, None, 0)
        past = dict(cmp=pool_cmp, sel=pool_sel, win_t=win_t, layer=l, win=cache_win_kv[l], page_table=page_table,
                    mC=state_mlstm_C[l], mn=state_mlstm_n[l], mm=state_mlstm_m[l],
                    gS=state_gdn_S[l], gconv=state_gdn_conv[l])
        y_s, new_s = _layer(y_s, lw, tb_s, past, past_len)
        for k in names:
            st_p[k].append(new_p[k])
            st_s[k].append(new_s[k])
    P = {k: jnp.stack(v) for k, v in st_p.items()}
    S = {k: jnp.stack(v) for k, v in st_s.items()}
    return (y_p, y_s, P['cmp'], S['cmp'], P['sel'], S['sel'], P['win'], S['win'],
            P['mC'], S['mC'], P['mn'], S['mn'], P['mm'], S['mm'], P['gS'], S['gS'], P['gconv'], S['gconv'])
```

```python
import functools
import math

import jax
import jax.numpy as jnp
from jax import lax
from jax.experimental import pallas as pl
from jax.experimental.pallas import tpu as pltpu

D_MODEL = 1024
DEPTH = 2
PAGE_SIZE = 128
A_HEADS = 8
A_KV = 2
A_REP = A_HEADS // A_KV
A_HD = 64
CMP_BLOCK = 32
CMP_STRIDE = 16
SEL_BLOCK = 64
TOP_N = 16
WINDOW = 512
Q_BLOCK = 128
N_BUCKETS = 32
MAX_DIST = 2048
M_HEADS = 4
M_HD = 128
M_CHUNK = 64
G_HEADS = 4
G_HD = 128
G_CHUNK = 64
CONV_W = 4
W_A = A_HEADS * A_HD
W_M = M_HEADS * M_HD
W_G = G_HEADS * G_HD
N_BRANCH = 3
EPS = 1e-6
NEG = -1e30
TINY = 1e-30

F32 = jnp.float32
BF16 = jnp.bfloat16
HI = lax.Precision.HIGHEST
NN = (((1,), (0,)), ((), ()))
NT = (((1,), (1,)), ((), ()))
TN = (((0,), (0,)), ((), ()))

LANES = 128
VMEM_LIMIT = 56 * 1024 * 1024

IN_ORDER = ('a_q', 'a_kv', 'a_gate', 'a_z', 'm_qkv', 'm_if', 'm_o', 'm_z', 'g_qkv', 'g_ab', 'g_z', 'merge')
IN_WIDTH = dict(a_q=W_A, a_kv=3 * 2 * A_KV * A_HD, a_gate=A_HEADS * 3, a_z=W_A, m_qkv=3 * W_M, m_if=2 * M_HEADS,
                m_o=W_M, m_z=W_M, g_qkv=3 * W_G, g_ab=2 * G_HEADS, g_z=W_G, merge=N_BRANCH * D_MODEL)
OFF = dict(merge=0, m_qkv=3072, g_qkv=4608, a_q=6144, a_z=6656, m_o=7168, m_z=7680, g_z=8192, a_kv=8704,
           small=9472)
N_PERM = 9600
SM_GATE, SM_MIF, SM_GAB = 0, 24, 32


def _cparams(sem):
    return pltpu.CompilerParams(dimension_semantics=sem, vmem_limit_bytes=VMEM_LIMIT)


def _silu(x):
    return x * jax.nn.sigmoid(x)


def _log_sigmoid(x):
    return jnp.minimum(x, 0.0) - jnp.log(1.0 + jnp.exp(-jnp.abs(x)))


def _softplus(x):
    return jnp.maximum(x, 0.0) + jnp.log(1.0 + jnp.exp(-jnp.abs(x)))


def _split(a):
    hi = a.astype(BF16)
    return hi, (a - hi.astype(F32)).astype(BF16)


def _dot3(a, b, dims=NN):
    mm = lambda x, y: lax.dot_general(x, y, dims, preferred_element_type=F32)
    return mm(a[0], b[0]) + mm(a[0], b[1]) + mm(a[1], b[0])


def _src_offsets():
    offs, off = {}, 0
    for name in IN_ORDER:
        offs[name] = off
        off += IN_WIDTH[name]
    return offs, off


def _permute_kernel(w_ref, o_ref):
    src, _ = _src_offsets()
    small = []
    for name in IN_ORDER:
        w = IN_WIDTH[name]
        rows = w_ref[src[name]:src[name] + w, :]
        if w % LANES:
            small.append(rows)
        else:
            o_ref[:, OFF[name]:OFF[name] + w] = rows.T.astype(BF16)
    used = sum(r.shape[0] for r in small)
    small.append(jnp.zeros((LANES - used, w_ref.shape[1]), F32))
    o_ref[:, OFF['small']:N_PERM] = jnp.concatenate(small, axis=0).T.astype(BF16)


def _permute_w_in(w_t, layer):
    _, n_in, d = w_t.shape
    tr = 128
    return pl.pallas_call(
        _permute_kernel,
        grid=(d // tr,),
        in_specs=[pl.BlockSpec((None, n_in, tr), lambda i: (layer, 0, i))],
        out_specs=pl.BlockSpec((tr, N_PERM), lambda i: (i, 0)),
        out_shape=jax.ShapeDtypeStruct((d, N_PERM), BF16),
        compiler_params=_cparams(("parallel",)),
        name="permute_w",
    )(w_t)


def _rel_bucket(dist):
    n = jnp.maximum(dist, 0)
    exact = N_BUCKETS // 2
    nf = jnp.maximum(n, exact).astype(F32)
    large = exact + (jnp.log(nf / exact) / math.log(MAX_DIST / exact) * (N_BUCKETS - exact)).astype(jnp.int32)
    return jnp.where(n < exact, n, jnp.minimum(large, N_BUCKETS - 1))


def _bias_kernel(thr_ref, tab_ref, d_ref, o_ref):
    n = jnp.maximum(d_ref[0], 0)
    for h in range(A_HEADS):
        acc = jnp.full(n.shape, tab_ref[h], F32)
        for k in range(1, N_BUCKETS):
            acc = jnp.where(n >= thr_ref[k], tab_ref[k * A_HEADS + h], acc)
        o_ref[0, h // A_REP, h % A_REP] = acc


def _bias_rows(rel_bias, dist):
    N, Q, K = dist.shape
    nmax = 2 * MAX_DIST
    thr = jnp.sum(_rel_bucket(jnp.arange(nmax))[None, :] < jnp.arange(N_BUCKETS)[:, None], axis=1).astype(jnp.int32)
    smem = pl.BlockSpec(memory_space=pltpu.SMEM)
    out = pl.pallas_call(
        _bias_kernel,
        grid=(N,),
        in_specs=[smem, smem, pl.BlockSpec((1, Q, K), lambda i: (i, 0, 0))],
        out_specs=pl.BlockSpec((1, A_KV, A_REP, Q, K), lambda i: (i, 0, 0, 0, 0)),
        out_shape=jax.ShapeDtypeStruct((N, A_KV, A_REP, Q, K), F32),
        compiler_params=_cparams(("parallel",)),
        name="bias_rows",
    )(thr, rel_bias.astype(F32).reshape(N_BUCKETS * A_HEADS), dist.astype(jnp.int32))
    return out.reshape(N, A_KV, A_REP * Q, K)


def _cmp_weights(w):
    wr = w.reshape(A_KV, 2, CMP_STRIDE, A_HD, A_HD)
    eye = jnp.eye(A_KV, dtype=w.dtype)
    full = jnp.einsum('gmjde,gh->jgdmhe', wr, eye)
    return full.reshape(CMP_STRIDE, A_KV * A_HD, 2 * A_KV * A_HD).astype(BF16)


def _cover_t(ns_pad, nch):
    s0 = jnp.arange(ns_pad)[:, None] * SEL_BLOCK
    c0 = jnp.arange(nch)[None, :] * CMP_STRIDE
    return ((c0 < s0 + SEL_BLOCK) & (s0 <= c0 + CMP_BLOCK - 1)).astype(F32)


def _expand_mat(ns_pad, nk):
    return (jnp.arange(nk)[None, :] // SEL_BLOCK == jnp.arange(ns_pad)[:, None]).astype(BF16)


def _gate_expand():
    col = jnp.arange(N_BRANCH * W_A)
    src = 3 * ((col % W_A) // A_HD) + col // W_A
    return (jnp.arange(LANES)[:, None] == src[None, :]).astype(BF16)


def _prompt_tables(rel_bias, T):
    QB = Q_BLOCK
    nqb = T // QB
    nch = T // CMP_STRIDE
    ns = T // SEL_BLOCK
    ns_pad = -(-ns // 8) * 8
    i_ = jnp.arange(QB)
    t = (jnp.arange(nqb) * QB)[:, None, None] + i_[None, :, None]
    cend = (jnp.arange(nch) * CMP_STRIDE + CMP_BLOCK - 1)[None, None, :]
    d = (jnp.arange(nqb) * QB)[:, None, None] + i_[None, :, None] - i_[None, None, :]
    nw = WINDOW // QB + 1
    return dict(
        bcmp=_bias_rows(rel_bias, t - cend),
        tsel=_bias_rows(rel_bias, d),
        covt=_cover_t(ns_pad, nch),
        e3=_expand_mat(LANES, T).reshape(LANES, nqb, QB).transpose(1, 0, 2),
        wm=((d[:nw] >= 0) & (d[:nw] < WINDOW)).astype(F32),
        cm=(d[:2] >= 0).astype(F32),
        gx=_gate_expand())


def _sample_tables(rel_bias, past, tp, t_real):
    SL = past + LANES
    nblk = past // CMP_STRIDE
    ns = -(-(past + t_real) // SEL_BLOCK)
    ns_pad = -(-ns // LANES) * LANES
    WK = WINDOW + LANES
    i_ = jnp.arange(tp)
    t = past + i_
    rows = lambda dist: _bias_rows(rel_bias, dist[None])[0]
    return dict(
        bcmp=rows(t[:, None] - (jnp.arange(nblk) * CMP_STRIDE + CMP_BLOCK - 1)[None, :]),
        bsel=rows(t[:, None] - jnp.arange(SL)[None, :]),
        bwin=rows(i_[:, None] + WINDOW - jnp.arange(WK)[None, :]),
        cov=_cover_t(ns_pad, nblk).T, emat=_expand_mat(ns_pad, SL))


def _proj_in_kernel(x_ref, g_ref, w_ref, o_ref, hn_ref):
    @pl.when(pl.program_id(1) == 0)
    def _():
        x = x_ref[...]
        ms = jnp.mean(x * x, axis=-1, keepdims=True)
        hn_ref[...] = (x * lax.rsqrt(ms + EPS) * g_ref[...]).astype(BF16)

    o_ref[...] = jnp.dot(hn_ref[...], w_ref[...], preferred_element_type=F32)


def _proj_in(x2, norm_g, w_perm):
    n = x2.shape[0]
    tm = min(1024, n)
    tn = 1920
    return pl.pallas_call(
        _proj_in_kernel,
        grid=(n // tm, N_PERM // tn),
        in_specs=[pl.BlockSpec((tm, D_MODEL), lambda i, j: (i, 0)),
                  pl.BlockSpec((1, D_MODEL), lambda i, j: (0, 0)),
                  pl.BlockSpec((D_MODEL, tn), lambda i, j: (0, j))],
        out_specs=pl.BlockSpec((tm, tn), lambda i, j: (i, j)),
        out_shape=jax.ShapeDtypeStruct((n, N_PERM), F32),
        scratch_shapes=[pltpu.VMEM((tm, D_MODEL), BF16)],
        compiler_params=_cparams(("parallel", "arbitrary")),
        name="proj_in",
    )(x2, norm_g.reshape(1, D_MODEL), w_perm)


def _prep_a_kernel(q_ref, kv0_ref, kv1_ref, kv2_ref, bdq_ref, bdk_ref, qg_ref, kg_ref, qo_ref, ro_ref):
    q = q_ref[...]
    ms = jnp.dot(q * q, bdq_ref[...], precision=HI, preferred_element_type=F32) * (1.0 / A_HD)
    qo_ref[...] = q * lax.rsqrt(ms + EPS) * qg_ref[...] * (A_HD ** -0.5)
    for b, kv_ref in enumerate((kv0_ref, kv1_ref, kv2_ref)):
        k = kv_ref[:, 0:128]
        ms = jnp.dot(k * k, bdk_ref[...], precision=HI, preferred_element_type=F32) * (1.0 / A_HD)
        ro_ref[2 * b] = k * lax.rsqrt(ms + EPS) * kg_ref[b:b + 1, :]
        ro_ref[2 * b + 1] = kv_ref[:, 128:256]


def _prep_a(u2, a_qn, a_kn):
    n = u2.shape[0]
    tm = min(512, n)
    bd = lambda w: (jnp.arange(w)[:, None] // A_HD == jnp.arange(w)[None, :] // A_HD).astype(F32)
    qg = jnp.tile(a_qn, A_HEADS).reshape(1, W_A)
    kg = jnp.tile(a_kn, (1, A_KV))
    const = lambda shape: pl.BlockSpec(shape, lambda i: (0, 0))
    kvs = lambda b: pl.BlockSpec((tm, 256), lambda i: (i, OFF['a_kv'] // 256 + b))
    return pl.pallas_call(
        _prep_a_kernel,
        grid=(n // tm,),
        in_specs=[pl.BlockSpec((tm, W_A), lambda i: (i, OFF['a_q'] // W_A)), kvs(0), kvs(1), kvs(2),
                  const((W_A, W_A)), const((128, 128)), const((1, W_A)), const((3, 128))],
        out_specs=[pl.BlockSpec((tm, W_A), lambda i: (i, 0)), pl.BlockSpec((6, tm, 128), lambda i: (0, i, 0))],
        out_shape=[jax.ShapeDtypeStruct((n, W_A), F32), jax.ShapeDtypeStruct((6, n, 128), F32)],
        compiler_params=_cparams(("parallel",)),
        name="prep_a",
    )(u2, u2, u2, u2, bd(W_A), bd(128), qg, kg)


def _stack_heads(q, g):
    return jnp.concatenate([q[:, (A_REP * g + r) * A_HD:(A_REP * g + r + 1) * A_HD] for r in range(A_REP)], axis=0)


def _rep_rows(x):
    return jnp.concatenate([x] * A_REP, axis=0)


def _masked_softmax(s, valid):
    sm = jnp.where(valid, s, NEG)
    m = jnp.max(sm, axis=-1, keepdims=True)
    e = jnp.where(valid, jnp.exp(sm - m), 0.0)
    l = jnp.sum(e, axis=-1, keepdims=True)
    return e / jnp.maximum(l, TINY)


def _select_blocks(pc_sum, covt_ref, st_ref, t_row, ns, n_top):
    ns_pad = covt_ref.shape[0]
    imp = lax.dot_general(covt_ref[...], pc_sum, NT, precision=HI, preferred_element_type=F32)
    jj = lax.broadcasted_iota(jnp.int32, (ns_pad, LANES), 0)
    tt = jnp.broadcast_to(t_row, (ns_pad, LANES))
    cur = tt // SEL_BLOCK
    forced = (jj == 0) | (jj == cur) | (jj == cur - 1)
    future = jj * SEL_BLOCK > tt
    score = jnp.where(future, NEG, jnp.where(forced, -NEG, imp))
    score = jnp.where(jj < ns, score, -jnp.inf)
    st_ref[...] = score

    def beats(k, rank):
        row = jnp.broadcast_to(st_ref[pl.ds(k, 1), :], (ns_pad, LANES))
        b = (row > score) | ((row == score) & (k < jj))
        return rank + jnp.where(b, 1.0, 0.0)

    rank = jnp.zeros((ns_pad, LANES), F32)
    if ns <= 32:
        for k in range(ns):
            rank = beats(k, rank)
    else:
        rank = lax.fori_loop(0, ns, beats, rank)
    return jnp.where(rank < n_top, 1.0, 0.0)


def _pad_rows(x, rows):
    if x.shape[0] == rows:
        return x
    return jnp.concatenate([x, jnp.zeros((rows - x.shape[0], x.shape[1]), x.dtype)], axis=0)


def _select_rows(pc_sum, cov_ref, sel_ref, t0, ns, n_top, nq):
    Q = pc_sum.shape[0]
    ns_pad = cov_ref.shape[1]
    nk = -(-ns // 8) * 8
    imp = jnp.dot(pc_sum, cov_ref[...], precision=HI, preferred_element_type=F32)
    jj = lax.broadcasted_iota(jnp.int32, (Q, ns_pad), 1)
    tt = lax.broadcasted_iota(jnp.int32, (Q, ns_pad), 0) + t0
    cur = tt // SEL_BLOCK
    forced = (jj == 0) | (jj == cur) | (jj == cur - 1)
    future = jj * SEL_BLOCK > tt
    score = jnp.where(future, NEG, jnp.where(forced, -NEG, imp))
    score = jnp.where(jj < ns, score, -jnp.inf)
    score_col = _pad_rows(score, LANES).T
    kk = lax.broadcasted_iota(jnp.int32, (nk, ns_pad), 0)
    jl = lax.broadcasted_iota(jnp.int32, (nk, ns_pad), 1)
    sel_ref[...] = jnp.zeros(sel_ref.shape, F32)
    for i in range(nq):
        col = jnp.broadcast_to(score_col[0:nk, i:i + 1], (nk, ns_pad))
        row = jnp.broadcast_to(score[i:i + 1, :], (nk, ns_pad))
        beats = (col > row) | ((col == row) & (kk < jl))
        rank = jnp.sum(jnp.where(beats, 1.0, 0.0), axis=0, keepdims=True)
        sel_ref[i:i + 1, :] = jnp.where(rank < n_top, 1.0, 0.0)
    return sel_ref[...]


def _compress(src_ref, nrow, wk_ref, wv_ref, pbuf_ref, kc_ref, vc_ref):
    acck = jnp.zeros((nrow, 256), F32)
    accv = jnp.zeros((nrow, 256), F32)
    for j in range(CMP_STRIDE):
        xk = src_ref[0, pl.ds(j, nrow, stride=CMP_STRIDE), :].astype(BF16)
        xv = src_ref[1, pl.ds(j, nrow, stride=CMP_STRIDE), :].astype(BF16)
        acck = acck + jnp.dot(xk, wk_ref[j], preferred_element_type=F32)
        accv = accv + jnp.dot(xv, wv_ref[j], preferred_element_type=F32)
    nout = kc_ref.shape[0]
    for acc, dst in ((acck, kc_ref), (accv, vc_ref)):
        pbuf_ref[0:nrow, :] = acc[:, 128:256]
        dst[...] = acc[0:nout, 0:128] + pbuf_ref[pl.ds(1, nout), :]


def _gated_sum(gates, gx_ref, ocat_ref):
    gh, gl = _split(gates)
    gexp = (jnp.dot(gh, gx_ref[...], preferred_element_type=F32)
            + jnp.dot(gl, gx_ref[...], preferred_element_type=F32))
    out = gexp[:, 0:W_A] * ocat_ref[0]
    for c in range(1, N_BRANCH):
        out = out + gexp[:, c * W_A:(c + 1) * W_A] * ocat_ref[c]
    return out


def _nsa_prompt_kernel(q_ref, rows_ref, small_ref, wk_ref, wv_ref, bcmp_ref, tsel_ref, covt_ref, e3_ref,
                       wm_ref, cm_ref, gx_ref,
                       o_ref, kc_ref, vc_ref, pbuf_ref, sel_ref, acc_ref, m_ref, st_ref, ocat_ref,
                       *, ns, n_top):
    bi = pl.program_id(1)
    T = rows_ref.shape[1]
    nch = T // CMP_STRIDE
    QB = Q_BLOCK
    SEL, WIN = 1, 2

    @pl.when(bi == 0)
    def _():
        pbuf_ref[...] = jnp.zeros(pbuf_ref.shape, F32)
        _compress(rows_ref, nch, wk_ref, wv_ref, pbuf_ref, kc_ref, vc_ref)

    t0 = bi * QB
    q = q_ref[0]
    tc = lax.broadcasted_iota(jnp.int32, (QB, nch), 0) + t0
    cend = lax.broadcasted_iota(jnp.int32, (QB, nch), 1) * CMP_STRIDE + (CMP_BLOCK - 1)
    cvalid = _rep_rows(tc - cend >= 0)
    t_row = lax.broadcasted_iota(jnp.int32, (1, LANES), 1) + t0
    lane = lax.broadcasted_iota(jnp.int32, (QB, LANES), 1)
    in_g = [(lane >= g * A_HD) & (lane < (g + 1) * A_HD) for g in range(A_KV)]
    qg = [_stack_heads(q, g).astype(BF16) for g in range(A_KV)]

    def put_heads(c, g, o):
        for r in range(A_REP):
            h = A_REP * g + r
            ocat_ref[c, :, h * A_HD:(h + 1) * A_HD] = o[r * QB:(r + 1) * QB]

    for g in range(A_KV):
        kc = kc_ref[:, g * A_HD:(g + 1) * A_HD].astype(BF16)
        vc = vc_ref[:, g * A_HD:(g + 1) * A_HD].astype(BF16)
        s = lax.dot_general(qg[g], kc, NT, preferred_element_type=F32) + bcmp_ref[0, g]
        p_c = _masked_softmax(s, cvalid)
        put_heads(0, g, jnp.dot(p_c.astype(BF16), vc, preferred_element_type=F32))
        pc_sum = p_c[0:QB] + p_c[QB:2 * QB] + p_c[2 * QB:3 * QB] + p_c[3 * QB:4 * QB]
        sel_t = _select_blocks(pc_sum, covt_ref, st_ref, t_row, ns, n_top)
        sel_ref[g] = _pad_rows(sel_t, LANES).T.astype(BF16)

    m_ref[...] = jnp.full(m_ref.shape, NEG, F32)
    acc_ref[...] = jnp.zeros(acc_ref.shape, F32)

    def tiles(specs):
        cat = lambda xs, ax: xs[0] if len(xs) == 1 else jnp.concatenate(xs, axis=ax)
        cis = [2 * (br - 1) + g for br, g, _, _ in specs]
        scs, v1s, valids = [], [], []
        for br, g, kb, nb in specs:
            ks, vs, bs = [], [], []
            for j in range(nb):
                off = pl.multiple_of((kb + j) * QB, QB)
                ks.append(rows_ref[2 * br, pl.ds(off, QB), g * A_HD:(g + 1) * A_HD])
                vs.append(jnp.where(in_g[g], rows_ref[2 * br + 1, pl.ds(off, QB), :], 1.0))
                bs.append(tsel_ref[bi - kb - j, g])
            v1s.append(cat(vs, 0).astype(BF16))
            scs.append(lax.dot_general(qg[g], cat(ks, 0).astype(BF16), NT, preferred_element_type=F32)
                       + cat(bs, 1))
        for br, g, kb, nb in specs:
            ms = []
            for j in range(nb):
                if br == SEL:
                    mt = jnp.dot(sel_ref[g], e3_ref[kb + j], preferred_element_type=F32)
                    ms.append(mt * cm_ref[jnp.minimum(bi - kb, 1)] if nb == 1 else mt)
                else:
                    ms.append(wm_ref[bi - kb - j])
            valids.append(_rep_rows(cat(ms, 1)) > 0.5)
        sms = [jnp.where(v, sc, NEG) for v, sc in zip(valids, scs)]
        m_prevs = [m_ref[ci] for ci in cis]
        m_news = [jnp.maximum(mp, jnp.max(sm, axis=-1, keepdims=True)) for mp, sm in zip(m_prevs, sms)]
        ps = [jnp.where(v, jnp.exp(sm - cat([mn] * spec[3], 1)), 0.0).astype(BF16)
              for v, sm, mn, spec in zip(valids, sms, m_news, specs)]
        pvs = [jnp.dot(p, v1, preferred_element_type=F32) for p, v1 in zip(ps, v1s)]
        for ci, mp, mn, pv in zip(cis, m_prevs, m_news, pvs):
            acc_ref[ci] = jnp.exp(mp - mn) * acc_ref[ci] + pv
            m_ref[ci] = mn

    def body_sel(kb, carry):
        tiles([(SEL, g, kb, 1) for g in range(A_KV)])
        return carry

    def body_sel2(pair, carry):
        tiles([(SEL, g, 2 * pair, 2) for g in range(A_KV)])
        return carry

    def body_both(kb, carry):
        tiles([(br, g, kb, 1) for g in range(A_KV) for br in (SEL, WIN)])
        return carry

    lo = jnp.maximum(bi - WINDOW // QB, 0)
    lax.fori_loop(0, lo // 2, body_sel2, 0)
    lax.fori_loop(2 * (lo // 2), lo, body_sel, 0)
    lax.fori_loop(lo, bi + 1, body_both, 0)

    for br in (SEL, WIN):
        for g in range(A_KV):
            acc = acc_ref[2 * (br - 1) + g]
            linv = pltpu.roll(1.0 / jnp.maximum(acc, TINY), A_HD, axis=1)
            put_heads(br, g, (acc * linv)[:, g * A_HD:(g + 1) * A_HD])

    o_ref[0] = _gated_sum(jax.nn.sigmoid(small_ref[0]), gx_ref, ocat_ref)


def _nsa_prompt(qn3, rows6, u3, wk, wv, tb):
    B, T, _ = qn3.shape
    QB = Q_BLOCK
    nqb = T // QB
    nch = T // CMP_STRIDE
    ns = T // SEL_BLOCK
    ns_pad = tb['covt'].shape[0]
    n_top = min(TOP_N, ns)
    nw = tb['wm'].shape[0]
    kern = functools.partial(_nsa_prompt_kernel, ns=ns, n_top=n_top)
    c2 = lambda shape: pl.BlockSpec(shape, lambda b, i: (0,) * len(shape))
    return pl.pallas_call(
        kern,
        grid=(B, nqb),
        in_specs=[pl.BlockSpec((1, QB, W_A), lambda b, i: (b, i, 0)),
                  pl.BlockSpec((6, T, 128), lambda b, i: (0, b, 0)),
                  pl.BlockSpec((1, QB, LANES), lambda b, i: (b, i, OFF['small'] // LANES)),
                  c2((CMP_STRIDE, 128, 256)), c2((CMP_STRIDE, 128, 256)),
                  pl.BlockSpec((1, A_KV, A_REP * QB, nch), lambda b, i: (i, 0, 0, 0)),
                  c2((nqb, A_KV, A_REP * QB, QB)),
                  c2((ns_pad, nch)), c2((nqb, LANES, QB)), c2((nw, QB, QB)), c2((2, QB, QB)),
                  c2((LANES, N_BRANCH * W_A))],
        out_specs=pl.BlockSpec((1, QB, W_A), lambda b, i: (b, i, 0)),
        out_shape=jax.ShapeDtypeStruct((B, T, W_A), F32),
        scratch_shapes=[pltpu.VMEM((nch, 128), F32), pltpu.VMEM((nch, 128), F32),
                        pltpu.VMEM((nch + 8, 128), F32),
                        pltpu.VMEM((A_KV, QB, LANES), BF16),
                        pltpu.VMEM((2 * A_KV, A_REP * QB, LANES), F32),
                        pltpu.VMEM((2 * A_KV, A_REP * QB, LANES), F32),
                        pltpu.VMEM((ns_pad, LANES), F32),
                        pltpu.VMEM((N_BRANCH, QB, W_A), F32)],
        compiler_params=_cparams(("parallel", "arbitrary")),
        name="nsa_prompt",
    )(qn3, rows6, u3, wk, wv, tb['bcmp'], tb['tsel'], tb['covt'], tb['e3'], tb['wm'], tb['cm'], tb['gx'])


def _nsa_sample_kernel(pt_ref, *refs, pg, past, tp, t_real, ns, n_top):
    cmp_refs = refs[0:pg]
    sel_refs = refs[pg:2 * pg]
    (q_ref, rows_ref, small_ref, win_ref, wk_ref, wv_ref, bcmp_ref, bsel_ref, bwin_ref, cov_ref, e_ref,
     o_ref, cslab, sslab, kc_ref, vc_ref, pbuf_ref, st_ref, s_ref) = refs[2 * pg:]
    p = pl.program_id(1)
    npg = pl.num_programs(1)
    SL = cslab.shape[1]
    n_pages = past // PAGE_SIZE
    nblk = kc_ref.shape[0]
    WK = WINDOW + LANES

    for k in range(pg):
        page = p * pg + k
        off = pl.multiple_of(page * PAGE_SIZE, PAGE_SIZE)
        for kv in range(2):
            cslab[kv, pl.ds(off, PAGE_SIZE), :] = cmp_refs[k][kv * 128:(kv + 1) * 128, :].T
            sslab[kv, page] = sel_refs[k][kv * 128:(kv + 1) * 128, :]

    @pl.when(p == npg - 1)
    def _():
        new_t = [_pad_rows(rows_ref[i], LANES).T for i in range(2, 6)]
        for kv in range(2):
            cslab[kv, past:past + tp, :] = rows_ref[kv]
            cslab[kv, past + tp:SL, :] = jnp.zeros((SL - past - tp, 128), F32)
            sslab[kv, n_pages] = new_t[kv]
        pbuf_ref[...] = jnp.zeros(pbuf_ref.shape, F32)
        _compress(cslab, nblk + 8, wk_ref, wv_ref, pbuf_ref, kc_ref, vc_ref)

        q = q_ref[0]
        gates = jax.nn.sigmoid(small_ref[0])
        R = A_REP * tp
        ti = lax.broadcasted_iota(jnp.int32, (tp, SL), 0) + past
        causal = ti - lax.broadcasted_iota(jnp.int32, (tp, SL), 1) >= 0
        tc = lax.broadcasted_iota(jnp.int32, (tp, nblk), 0) + past
        cend = lax.broadcasted_iota(jnp.int32, (tp, nblk), 1) * CMP_STRIDE + (CMP_BLOCK - 1)
        cvalid = _rep_rows(tc - cend >= 0)
        wd = (lax.broadcasted_iota(jnp.int32, (tp, WK), 0) + WINDOW
              - lax.broadcasted_iota(jnp.int32, (tp, WK), 1))
        wvalid = _rep_rows((wd >= 0) & (wd < WINDOW))
        npt = n_pages + 1
        nck = 5 if npt % 5 == 0 else 1
        cpt = npt // nck

        for g in range(A_KV):
            gs = slice(g * A_HD, (g + 1) * A_HD)
            qg = _stack_heads(q, g).astype(BF16)
            s = lax.dot_general(qg, kc_ref[:, gs].astype(BF16), NT, preferred_element_type=F32) + bcmp_ref[g]
            p_c = _masked_softmax(s, cvalid)
            o_c = jnp.dot(p_c.astype(BF16), vc_ref[:, gs].astype(BF16), preferred_element_type=F32)
            pc_sum = p_c[0:tp] + p_c[tp:2 * tp] + p_c[2 * tp:3 * tp] + p_c[3 * tp:4 * tp]

            sel = _select_rows(pc_sum, cov_ref, st_ref, past, ns, n_top, t_real).astype(BF16)
            kmask = jnp.dot(sel, e_ref[...], preferred_element_type=F32)
            svalid = _rep_rows(causal & (kmask > 0.5))

            def sel_t(kv, c):
                return jnp.concatenate([sslab[kv, t, gs, :] for t in range(c * cpt, (c + 1) * cpt)],
                                       axis=1).astype(BF16)

            ck = cpt * LANES
            for c in range(nck):
                s_ref[:, c * ck:(c + 1) * ck] = jnp.dot(qg, sel_t(0, c), preferred_element_type=F32)
            p_s = _masked_softmax(s_ref[...] + bsel_ref[g], svalid)
            s_ref[...] = p_s
            o_s = jnp.zeros((R, A_HD), F32)
            for c in range(nck):
                o_s = o_s + lax.dot_general(s_ref[:, c * ck:(c + 1) * ck].astype(BF16), sel_t(1, c), NT,
                                            preferred_element_type=F32)

            kw = jnp.concatenate([win_ref[gs, :], new_t[2][gs, :]], axis=1).astype(BF16)
            vw = jnp.concatenate([win_ref[128 + g * A_HD:128 + (g + 1) * A_HD, :], new_t[3][gs, :]],
                                 axis=1).astype(BF16)
            sw = jnp.dot(qg, kw, preferred_element_type=F32) + bwin_ref[g]
            p_w = _masked_softmax(sw, wvalid)
            o_w = lax.dot_general(p_w.astype(BF16), vw, NT, preferred_element_type=F32)

            for r in range(A_REP):
                h = A_REP * g + r
                rs = slice(r * tp, (r + 1) * tp)
                out = (gates[:, 3 * h:3 * h + 1] * o_c[rs] + gates[:, 3 * h + 1:3 * h + 2] * o_s[rs]
                       + gates[:, 3 * h + 2:3 * h + 3] * o_w[rs])
                o_ref[0, :, h * A_HD:(h + 1) * A_HD] = out


def _nsa_sample(qn3, rows6, u3, pool_cmp, pool_sel, win_t, layer, page_table, wk, wv, tb, past, t_real):
    B, tp, _ = qn3.shape
    n_pages = past // PAGE_SIZE
    pg = 8 if n_pages % 8 == 0 else n_pages
    npg = n_pages // pg
    SL = past + LANES
    nblk = past // CMP_STRIDE
    ns = -(-(past + t_real) // SEL_BLOCK)
    ns_pad = tb['cov'].shape[1]
    n_top = min(TOP_N, ns)
    WK = WINDOW + LANES
    kern = functools.partial(_nsa_sample_kernel, pg=pg, past=past, tp=tp, t_real=t_real, ns=ns, n_top=n_top)

    def page_spec(k):
        return pl.BlockSpec((None, None, 256, PAGE_SIZE), lambda b, p, pt: (layer, pt[b, p * pg + k], 0, 0))

    def c_(shape):
        return pl.BlockSpec(shape, lambda b, p, pt: (0,) * len(shape))

    R = A_REP * tp
    grid_spec = pltpu.PrefetchScalarGridSpec(
        num_scalar_prefetch=1,
        grid=(B, npg),
        in_specs=([page_spec(k) for k in range(pg)] + [page_spec(k) for k in range(pg)]
                  + [pl.BlockSpec((1, tp, W_A), lambda b, p, pt: (b, 0, 0)),
                     pl.BlockSpec((6, tp, 128), lambda b, p, pt: (0, b, 0)),
                     pl.BlockSpec((1, tp, LANES), lambda b, p, pt: (b, 0, OFF['small'] // LANES)),
                     pl.BlockSpec((None, None, 256, WINDOW), lambda b, p, pt: (layer, b, 0, 0)),
                     c_((CMP_STRIDE, 128, 256)), c_((CMP_STRIDE, 128, 256)),
                     c_((A_KV, R, nblk)), c_((A_KV, R, SL)), c_((A_KV, R, WK)),
                     c_((nblk, ns_pad)), c_((ns_pad, SL))]),
        out_specs=pl.BlockSpec((1, tp, W_A), lambda b, p, pt: (b, 0, 0)),
        scratch_shapes=[pltpu.VMEM((2, SL, 128), F32), pltpu.VMEM((2, n_pages + 1, 128, PAGE_SIZE), F32),
                        pltpu.VMEM((nblk, 128), F32), pltpu.VMEM((nblk, 128), F32),
                        pltpu.VMEM((nblk + 16, 128), F32),
                        pltpu.VMEM((tp, ns_pad), F32), pltpu.VMEM((R, SL), F32)],
    )
    return pl.pallas_call(
        kern,
        grid_spec=grid_spec,
        out_shape=jax.ShapeDtypeStruct((B, tp, W_A), F32),
        compiler_params=_cparams(("parallel", "arbitrary")),
        name="nsa_sample",
    )(page_table, *([pool_cmp] * pg), *([pool_sel] * pg), qn3, rows6, u3, win_t, wk, wv,
      tb['bcmp'], tb['bsel'], tb['bwin'], tb['cov'], tb['emat'])


def _tri(L, lower_incl):
    r = lax.broadcasted_iota(jnp.int32, (L, L), 0)
    c = lax.broadcasted_iota(jnp.int32, (L, L), 1)
    return (r >= c) if lower_incl else (r > c)


def _mlstm_kernel(x_ref, small_ref, smallt_ref, brow_ref, bcol_ref, c0_ref, n0_ref, m0_ref,
                  h_ref, co_ref, no_ref, mo_ref, c_s, n_s, m_s, *, L, t_real):
    c = pl.program_id(1)
    nc = pl.num_programs(1)

    @pl.when(c == 0)
    def _():
        c_s[...] = c0_ref[0]
        n_s[...] = n0_ref[0]
        m_s[...] = m0_ref[0]

    sm = small_ref[0]
    smt = smallt_ref[0, 0]
    li_col = sm[:, SM_MIF:SM_MIF + 4] + brow_ref[0:1, 0:4]
    lf_col = _log_sigmoid(sm[:, SM_MIF + 4:SM_MIF + 8] + brow_ref[0:1, 4:8])
    li_row = smt[0:4, :] + bcol_ref[0:4, :]
    lf_row = _log_sigmoid(smt[4:8, :] + bcol_ref[4:8, :])
    if t_real % L:
        tcol = lax.broadcasted_iota(jnp.int32, (L, 4), 0) + c * L
        trow = lax.broadcasted_iota(jnp.int32, (4, L), 1) + c * L
        li_col = jnp.where(tcol < t_real, li_col, NEG)
        lf_col = jnp.where(tcol < t_real, lf_col, 0.0)
        li_row = jnp.where(trow < t_real, li_row, NEG)
        lf_row = jnp.where(trow < t_real, lf_row, 0.0)
    low = _tri(L, True)
    b_col = jnp.dot(low.astype(F32), lf_col, precision=HI, preferred_element_type=F32)
    b_row = lax.dot_general(lf_row, low.astype(F32), NT, precision=HI, preferred_element_type=F32)

    hd = range(M_HEADS)
    q = [x_ref[0, :, h * M_HD:(h + 1) * M_HD] for h in hd]
    k = [x_ref[0, :, W_M + h * M_HD:W_M + (h + 1) * M_HD] * (M_HD ** -0.5) for h in hd]
    v = [x_ref[0, :, 2 * W_M + h * M_HD:2 * W_M + (h + 1) * M_HD] for h in hd]
    bc = [b_col[:, h:h + 1] for h in hd]
    Dm = [jnp.where(low, bc[h] - b_row[h:h + 1, :] + li_row[h:h + 1, :], NEG) for h in hd]
    m_prev = [m_s[h][:, 0:1] for h in hd]
    a = [bc[h] + m_prev[h] for h in hd]
    mt = [jnp.maximum(a[h], jnp.max(Dm[h], axis=-1, keepdims=True)) for h in hd]
    qk = [lax.dot_general(q[h], k[h], NT, preferred_element_type=F32) for h in hd]
    S = [qk[h] * jnp.exp(Dm[h] - mt[h]) for h in hd]
    inter = [jnp.exp(a[h] - mt[h]) for h in hd]
    C = [c_s[h] for h in hd]
    n = [n_s[h] for h in hd]
    qC = [jnp.dot(q[h], C[h], preferred_element_type=F32) for h in hd]
    Sv = [jnp.dot(S[h], v[h], preferred_element_type=F32) for h in hd]
    den = [inter[h] * jnp.sum(q[h] * n[h], axis=-1, keepdims=True) + jnp.sum(S[h], axis=-1, keepdims=True)
           for h in hd]
    for h in hd:
        h_ref[0, :, h * M_HD:(h + 1) * M_HD] = ((inter[h] * qC[h] + Sv[h])
                                                / jnp.maximum(jnp.abs(den[h]), jnp.exp(-mt[h])))
    bL = [bc[h][L - 1:L, :] for h in hd]
    wlog = [bL[h] - bc[h] + li_col[:, h:h + 1] for h in hd]
    m_new = [jnp.maximum(bL[h] + m_prev[h], jnp.max(wlog[h], axis=0, keepdims=True)) for h in hd]
    dec = [jnp.exp(bL[h] + m_prev[h] - m_new[h]) for h in hd]
    kw = [k[h] * jnp.exp(wlog[h] - m_new[h]) for h in hd]
    kv = [lax.dot_general(kw[h], v[h], TN, preferred_element_type=F32) for h in hd]
    for h in hd:
        c_s[h] = dec[h] * C[h] + kv[h]
        n_s[h] = dec[h] * n[h] + jnp.sum(kw[h], axis=0, keepdims=True)
        m_s[h] = jnp.broadcast_to(m_new[h], (1, LANES))

    @pl.when(c == nc - 1)
    def _():
        co_ref[0] = c_s[...]
        no_ref[0] = n_s[...]
        mo_ref[0] = m_s[...]


def _mlstm(u3, smallt, m_bi, m_bf, C0, n0, m0, L, t_real):
    B, T, _ = u3.shape
    nc = T // L
    brow = jnp.concatenate([m_bi, m_bf]).reshape(1, 8)
    bcol = jnp.concatenate([m_bi, m_bf]).reshape(8, 1)
    n0 = n0.reshape(B, M_HEADS, 1, M_HD)
    m0 = jnp.broadcast_to(m0[:, :, None, None], (B, M_HEADS, 1, LANES))
    kern = functools.partial(_mlstm_kernel, L=L, t_real=t_real)
    st = lambda shape: pl.BlockSpec(shape, lambda b, c: (b,) + (0,) * (len(shape) - 1))
    h, Cn, nn, mn = pl.pallas_call(
        kern,
        grid=(B, nc),
        in_specs=[pl.BlockSpec((1, L, 3 * W_M), lambda b, c: (b, c, OFF['m_qkv'] // (3 * W_M))),
                  pl.BlockSpec((1, L, LANES), lambda b, c: (b, c, OFF['small'] // LANES)),
                  pl.BlockSpec((1, 1, 16, L), lambda b, c: (b, c, 0, 0)),
                  pl.BlockSpec((1, 8), lambda b, c: (0, 0)), pl.BlockSpec((8, 1), lambda b, c: (0, 0)),
                  st((1, M_HEADS, M_HD, M_HD)), st((1, M_HEADS, 1, M_HD)), st((1, M_HEADS, 1, LANES))],
        out_specs=[pl.BlockSpec((1, L, W_M), lambda b, c: (b, c, 0)),
                   st((1, M_HEADS, M_HD, M_HD)), st((1, M_HEADS, 1, M_HD)), st((1, M_HEADS, 1, LANES))],
        out_shape=[jax.ShapeDtypeStruct((B, T, W_M), F32),
                   jax.ShapeDtypeStruct((B, M_HEADS, M_HD, M_HD), F32),
                   jax.ShapeDtypeStruct((B, M_HEADS, 1, M_HD), F32),
                   jax.ShapeDtypeStruct((B, M_HEADS, 1, LANES), F32)],
        scratch_shapes=[pltpu.VMEM((M_HEADS, M_HD, M_HD), F32), pltpu.VMEM((M_HEADS, 1, M_HD), F32),
                        pltpu.VMEM((M_HEADS, 1, LANES), F32)],
        compiler_params=_cparams(("parallel", "arbitrary")),
        name="mlstm",
    )(u3, u3, smallt, brow, bcol, C0, n0, m0)
    return h, Cn, nn.reshape(B, M_HEADS, M_HD), mn[:, :, 0, 0]


def _gdn_a_kernel(x_ref, small_ref, smallt_ref, cw_ref, prow_ref, pcol_ref, buf_ref,
                  u_ref, w_ref, qe_ref, kd_ref, at_ref, eg_ref, xbuf, *, L, CB, t_real):
    c = pl.program_id(1)
    PRE = 8
    R = CB * L

    @pl.when(c == 0)
    def _():
        xbuf[0:PRE, :] = buf_ref[0]

    xbuf[PRE:PRE + R, :] = x_ref[0]
    conv = jnp.zeros((R, 3 * W_G), F32)
    for j in range(CONV_W):
        conv = conv + xbuf[pl.ds(PRE - (CONV_W - 1) + j, R), :] * cw_ref[j:j + 1, :]
    xbuf[0:PRE, :] = xbuf[R:R + PRE, :]
    conv = _silu(conv)

    sm = small_ref[0]
    g_all = -jnp.exp(prow_ref[0:1, 0:4]) * _softplus(sm[:, SM_GAB:SM_GAB + 4] + prow_ref[1:2, 0:4])
    beta_all = jax.nn.sigmoid(sm[:, SM_GAB + 4:SM_GAB + 8])
    lowf = _tri(L, True).astype(F32)

    H = G_HEADS
    lane_h = lax.broadcasted_iota(jnp.int32, (L, H * L), 1) // L
    row_p = lax.broadcasted_iota(jnp.int32, (L, H * L), 0)
    col_p = lax.broadcasted_iota(jnp.int32, (L, H * L), 1) % L
    blockmask = (lax.broadcasted_iota(jnp.int32, (H * L, H * L), 0) // L
                 == lax.broadcasted_iota(jnp.int32, (H * L, H * L), 1) // L)

    def pack_diag(full):
        out = full[0:L]
        for h in range(1, H):
            out = jnp.where(lane_h == h, full[h * L:(h + 1) * L], out)
        return out

    def block_diag(m, parts):
        if L % 16:
            return _split(jnp.where(blockmask, jnp.concatenate([m] * H, axis=0), 0.0))
        return tuple(jnp.where(blockmask, jnp.concatenate([p] * H, axis=0), jnp.zeros((), BF16))
                     for p in parts)

    a_list, rhs_list = [], []
    for cb in range(CB):
        rs = slice(cb * L, (cb + 1) * L)
        smt = smallt_ref[0, cb]
        g_col = g_all[rs]
        beta_col = beta_all[rs]
        g_row = -jnp.exp(pcol_ref[0:4, 0:1]) * _softplus(smt[8:12, :] + pcol_ref[0:4, 1:2])
        if t_real % L:
            t0 = (c * CB + cb) * L
            tcol = lax.broadcasted_iota(jnp.int32, (L, 4), 0) + t0
            trow = lax.broadcasted_iota(jnp.int32, (4, L), 1) + t0
            g_col = jnp.where(tcol < t_real, g_col, 0.0)
            beta_col = jnp.where(tcol < t_real, beta_col, 0.0)
            g_row = jnp.where(trow < t_real, g_row, 0.0)
        G_col = jnp.dot(lowf, g_col, precision=HI, preferred_element_type=F32)
        G_row = lax.dot_general(g_row, lowf, NT, precision=HI, preferred_element_type=F32)

        qs, ks, kbs, rhss = [], [], [], []
        for h in range(G_HEADS):
            hs = slice(h * G_HD, (h + 1) * G_HD)
            cq = conv[rs, h * G_HD:(h + 1) * G_HD]
            ck = conv[rs, W_G + h * G_HD:W_G + (h + 1) * G_HD]
            v = conv[rs, 2 * W_G + h * G_HD:2 * W_G + (h + 1) * G_HD]
            q = cq * lax.rsqrt(jnp.sum(cq * cq, axis=-1, keepdims=True) + EPS) * (G_HD ** -0.5)
            k = ck * lax.rsqrt(jnp.sum(ck * ck, axis=-1, keepdims=True) + EPS)
            Gc = G_col[:, h:h + 1]
            bcol = beta_col[:, h:h + 1]
            kb = k * bcol
            eG = jnp.exp(Gc)
            GL = Gc[L - 1:L, :]
            qs.append(q)
            ks.append(k)
            kbs.append(kb)
            rhss.append(jnp.concatenate([v * bcol, kb * eG], axis=1))
            qe_ref[0, rs, hs] = q * eG
            kd_ref[0, rs, hs] = k * jnp.exp(GL - Gc)
            eg_ref[0, cb, h:h + 1, :] = jnp.broadcast_to(jnp.exp(GL), (1, LANES))
        eg_ref[0, cb, G_HEADS:8, :] = jnp.zeros((8 - G_HEADS, LANES), F32)

        Gc_p = jnp.concatenate([jnp.broadcast_to(G_col[:, h:h + 1], (L, L)) for h in range(H)], axis=1)
        Gr_p = jnp.concatenate([jnp.broadcast_to(G_row[h:h + 1, :], (L, L)) for h in range(H)], axis=1)
        low_p = row_p >= col_p
        dmask = jnp.where(low_p, jnp.exp(jnp.where(low_p, Gc_p - Gr_p, 0.0)), 0.0)
        k_st = _split(jnp.concatenate(ks, axis=0))
        A = pack_diag(_dot3(_split(jnp.concatenate(kbs, axis=0)), k_st, NT)) * jnp.where(row_p > col_p, dmask, 0.0)
        attn = pack_diag(lax.dot_general(jnp.concatenate(qs, axis=0).astype(BF16), k_st[0], NT,
                                         preferred_element_type=F32)) * dmask
        at_ref[0, rs, :] = attn
        a_list.append(A)
        rhs_list.append(_split(jnp.concatenate(rhss, axis=0)))

    chunks = range(CB)
    X = [jnp.where(row_p == col_p, 1.0, 0.0) - a_list[cb] for cb in chunks]
    As = [_split(a_list[cb]) for cb in chunks]
    Pw = [_dot3(As[cb], block_diag(a_list[cb], As[cb])) for cb in chunks]
    span = 2
    while span < L:
        Ps = [_split(Pw[cb]) for cb in chunks]
        Pbd = [block_diag(Pw[cb], Ps[cb]) for cb in chunks]
        X = [X[cb] + _dot3(_split(X[cb]), Pbd[cb]) for cb in chunks]
        span *= 2
        if span < L:
            Pw = [_dot3(Ps[cb], Pbd[cb]) for cb in chunks]
    sol = [_dot3(block_diag(X[cb], _split(X[cb])), rhs_list[cb]) for cb in chunks]
    for cb in chunks:
        for h in range(H):
            rs = slice(cb * L, (cb + 1) * L)
            hs = slice(h * G_HD, (h + 1) * G_HD)
            u_ref[0, rs, hs] = sol[cb][h * L:(h + 1) * L, 0:G_HD]
            w_ref[0, rs, hs] = sol[cb][h * L:(h + 1) * L, G_HD:2 * G_HD]


def _gdn_b_kernel(u_ref, w_ref, qe_ref, kd_ref, at_ref, eg_ref, s0_ref, o_ref, so_ref, s_s, *, L, BB):
    c = pl.program_id(1)
    nc = pl.num_programs(1)

    @pl.when(c == 0)
    def _():
        s_s[...] = s0_ref[...]

    ch = [(b, h, slice(h * G_HD, (h + 1) * G_HD)) for b in range(BB) for h in range(G_HEADS)]
    S = [s_s[b, h] for b, h, _ in ch]
    r = [jnp.dot(jnp.concatenate([w_ref[b, :, hs], qe_ref[b, :, hs]], axis=0), S[i], preferred_element_type=F32)
         for i, (b, h, hs) in enumerate(ch)]
    v_new = [u_ref[b, :, hs] - r[i][0:L] for i, (b, h, hs) in enumerate(ch)]
    av = [jnp.dot(at_ref[b, :, h * L:(h + 1) * L], v_new[i], preferred_element_type=F32)
          for i, (b, h, hs) in enumerate(ch)]
    kdv = [lax.dot_general(kd_ref[b, :, hs], v_new[i], TN, preferred_element_type=F32)
           for i, (b, h, hs) in enumerate(ch)]
    for i, (b, h, hs) in enumerate(ch):
        o_ref[b, :, hs] = r[i][L:2 * L] + av[i]
        s_s[b, h] = eg_ref[b, 0, h:h + 1, :] * S[i] + kdv[i]

    @pl.when(c == nc - 1)
    def _():
        so_ref[...] = s_s[...]


def _gdn(u3, smallt, g_conv, g_A_log, g_dt_bias, buf, S0, L, t_real):
    B, T, _ = u3.shape
    nc = T // L
    CB = 4 if nc % 4 == 0 else 1
    BB = 8 if B % 8 == 0 else B
    prow = jnp.stack([g_A_log, g_dt_bias])
    pcol = jnp.stack([g_A_log, g_dt_bias], axis=1)
    buf8 = jnp.concatenate([jnp.zeros((B, 8 - (CONV_W - 1), 3 * W_G), F32), buf], axis=1)
    st = lambda shape: pl.BlockSpec(shape, lambda b, c: (b,) + (0,) * (len(shape) - 1))
    cst = lambda shape: pl.BlockSpec(shape, lambda b, c: (0,) * len(shape))
    row = lambda nb, r, w: pl.BlockSpec((nb, r, w), lambda b, c: (b, c, 0))
    tok = lambda w: jax.ShapeDtypeStruct((B, T, w), F32)
    u, w, qe, kd, at, eg = pl.pallas_call(
        functools.partial(_gdn_a_kernel, L=L, CB=CB, t_real=t_real),
        grid=(B, nc // CB),
        in_specs=[pl.BlockSpec((1, CB * L, 3 * W_G), lambda b, c: (b, c, OFF['g_qkv'] // (3 * W_G))),
                  pl.BlockSpec((1, CB * L, LANES), lambda b, c: (b, c, OFF['small'] // LANES)),
                  pl.BlockSpec((1, CB, 16, L), lambda b, c: (b, c, 0, 0)),
                  cst((CONV_W, 3 * W_G)), cst((2, 4)), cst((4, 2)), st((1, 8, 3 * W_G))],
        out_specs=[row(1, CB * L, W_G)] * 4 + [row(1, CB * L, G_HEADS * L),
                                               pl.BlockSpec((1, CB, 8, LANES), lambda b, c: (b, c, 0, 0))],
        out_shape=[tok(W_G)] * 4 + [tok(G_HEADS * L), jax.ShapeDtypeStruct((B, nc, 8, LANES), F32)],
        scratch_shapes=[pltpu.VMEM((CB * L + 8, 3 * W_G), F32)],
        compiler_params=_cparams(("parallel", "arbitrary")),
        name="gdn_a",
    )(u3, u3, smallt, g_conv, prow, pcol, buf8)
    sblk = pl.BlockSpec((BB, G_HEADS, G_HD, G_HD), lambda b, c: (b, 0, 0, 0))
    return pl.pallas_call(
        functools.partial(_gdn_b_kernel, L=L, BB=BB),
        grid=(B // BB, nc),
        in_specs=[row(BB, L, W_G)] * 4 + [row(BB, L, G_HEADS * L),
                                          pl.BlockSpec((BB, 1, 8, LANES), lambda b, c: (b, c, 0, 0)), sblk],
        out_specs=[row(BB, L, W_G), sblk],
        out_shape=[tok(W_G), jax.ShapeDtypeStruct((B, G_HEADS, G_HD, G_HD), F32)],
        scratch_shapes=[pltpu.VMEM((BB, G_HEADS, G_HD, G_HD), F32)],
        compiler_params=_cparams(("parallel", "arbitrary")),
        name="gdn_b",
    )(u, w, qe, kd, at, eg, S0)


def _head_rmsnorm(x, gain_ref, nheads, hd):
    outs = []
    for h in range(nheads):
        xh = x[:, h * hd:(h + 1) * hd]
        ms = jnp.mean(xh * xh, axis=-1, keepdims=True)
        outs.append(xh * lax.rsqrt(ms + EPS) * gain_ref[...])
    return jnp.concatenate(outs, axis=1)


def _merge_kernel(x_ref, oa_ref, hm_ref, og_ref, az_ref, mo_ref, mz_ref, gz_ref, mg_ref,
                  wb_ref, wo_ref, mhn_ref, ghn_ref, y_ref):
    oa = oa_ref[...] * _silu(az_ref[...])
    om = _head_rmsnorm(hm_ref[...], mhn_ref, M_HEADS, M_HD) * jax.nn.sigmoid(mo_ref[...]) * _silu(mz_ref[...])
    og = _head_rmsnorm(og_ref[...], ghn_ref, G_HEADS, G_HD) * _silu(gz_ref[...])
    y = jnp.zeros(y_ref.shape, F32)
    for i, br in enumerate((oa, om, og)):
        proj = jnp.dot(br.astype(BF16), wb_ref[i], preferred_element_type=F32)
        y = y + jax.nn.sigmoid(mg_ref[:, i * D_MODEL:(i + 1) * D_MODEL]) * proj
    y_ref[...] = x_ref[...] + jnp.dot(y.astype(BF16), wo_ref[...], preferred_element_type=F32)


def _merge_out(x2, o_a, h_m, o_g, u2, w_branch, w_out, m_hn, g_hn):
    n = x2.shape[0]
    tm = min(512, n)
    row = lambda w, off: pl.BlockSpec((tm, w), lambda i: (i, off // w))
    cst = lambda shape: pl.BlockSpec(shape, lambda i: (0,) * len(shape))
    return pl.pallas_call(
        _merge_kernel,
        grid=(n // tm,),
        in_specs=[row(D_MODEL, 0), row(W_A, 0), row(W_M, 0), row(W_G, 0),
                  row(W_A, OFF['a_z']), row(W_M, OFF['m_o']), row(W_M, OFF['m_z']), row(W_G, OFF['g_z']),
                  row(N_BRANCH * D_MODEL, OFF['merge']),
                  cst((N_BRANCH, W_A, D_MODEL)), cst((D_MODEL, D_MODEL)), cst((1, M_HD)), cst((1, G_HD))],
        out_specs=row(D_MODEL, 0),
        out_shape=jax.ShapeDtypeStruct((n, D_MODEL), F32),
        compiler_params=_cparams(("parallel",)),
        name="merge_out",
    )(x2, o_a, h_m, o_g, u2, u2, u2, u2, u2, w_branch.astype(BF16), w_out.astype(BF16),
      m_hn.reshape(1, M_HD), g_hn.reshape(1, G_HD))


def _small_t(u3, L):
    B, T, _ = u3.shape
    s = u3[:, :, OFF['small'] + SM_MIF:OFF['small'] + SM_MIF + 16]
    return s.reshape(B, T // L, L, 16).transpose(0, 1, 3, 2)


def _layer(x, lw, tb, past, q_off):
    B, T, _ = x.shape
    x2 = x.reshape(B * T, D_MODEL)
    u2 = _proj_in(x2, lw['norm_g'], lw['w_perm'])
    if past is None:
        tp, L = T, math.gcd(T, M_CHUNK)
        u3 = u2.reshape(B, T, N_PERM)
        C0 = jnp.zeros((B, M_HEADS, M_HD, M_HD), F32)
        n0 = jnp.zeros((B, M_HEADS, M_HD), F32)
        m0 = jnp.zeros((B, M_HEADS), F32)
        S0 = jnp.zeros((B, G_HEADS, G_HD, G_HD), F32)
        buf = jnp.zeros((B, CONV_W - 1, 3 * W_G), F32)
    else:
        tp = -(-T // 8) * 8
        L = tp
        u3 = jnp.pad(u2.reshape(B, T, N_PERM), ((0, 0), (0, tp - T), (0, 0)))
        C0, n0, m0, S0, buf = past['mC'], past['mn'], past['mm'], past['gS'], past['gconv']
    up = u3.reshape(B * tp, N_PERM)
    qn, rows6 = _prep_a(up, lw['a_qn'], lw['a_kn'])
    qn3 = qn.reshape(B, tp, W_A)
    if past is None:
        o_a = _nsa_prompt(qn3, rows6, u3, lw['wk'], lw['wv'], tb)
    else:
        o_a = _nsa_sample(qn3, rows6, u3, past['cmp'], past['sel'], past['win_t'], past['layer'], past['page_table'],
                          lw['wk'], lw['wv'], tb, q_off, T)
    smallt = _small_t(u3, L)
    h_m, Cn, nn, mn = _mlstm(u3, smallt, lw['m_bi'], lw['m_bf'], C0, n0, m0, L, T)
    o_g, Sn = _gdn(u3, smallt, lw['g_conv'], lw['g_A_log'], lw['g_dt_bias'], buf, S0, L, T)
    if tp != T:
        o_a, h_m, o_g = o_a[:, :T], h_m[:, :T], o_g[:, :T]
    y2 = _merge_out(x2, o_a.reshape(B * T, W_A), h_m.reshape(B * T, W_M), o_g.reshape(B * T, W_G), u2,
                    lw['w_branch'], lw['w_out'], lw['m_hn'], lw['g_hn'])
    rows = rows6.reshape(3, 2, B, tp, A_KV, A_HD)[:, :, :, :T]
    new_cmp, new_sel, new_win = (jnp.moveaxis(rows[b], 0, 2) for b in range(3))
    g_qkv = u3[:, :T, OFF['g_qkv']:OFF['g_qkv'] + 3 * W_G]
    if past is None:
        win = new_win
    else:
        win = jnp.concatenate([past['win'], new_win], axis=1)
    full = jnp.concatenate([buf, g_qkv[:, max(T - (CONV_W - 1), 0):]], axis=1)
    state = dict(cmp=new_cmp, sel=new_sel, win=win[:, -min(WINDOW, win.shape[1]):],
                 mC=Cn, mn=nn, mm=mn, gS=Sn, gconv=full[:, -(CONV_W - 1):])
    return y2.reshape(B, T, D_MODEL), state


def kernel(x_prompt, x_sample, cache_cmp_kv, cache_sel_kv, cache_win_kv, state_mlstm_C, state_mlstm_n,
           state_mlstm_m, state_gdn_S, state_gdn_conv, page_table, norm_g, w_in, a_qn, a_kn, a_cmp_wk,
           a_cmp_wv, rel_bias, m_bi, m_bf, m_hn, g_conv, g_A_log, g_dt_bias, g_hn, w_branch, w_out):
    names = ('cmp', 'sel', 'win', 'mC', 'mn', 'mm', 'gS', 'gconv')
    st_p = {k: [] for k in names}
    st_s = {k: [] for k in names}
    past_len = page_table.shape[1] * PAGE_SIZE
    n_pool = cache_cmp_kv.shape[1]
    pos_minor = lambda c: jnp.swapaxes(c.reshape(c.shape[0], c.shape[1], c.shape[2], 2 * A_KV * A_HD), 2, 3)
    pool_cmp, pool_sel, win_t = pos_minor(cache_cmp_kv), pos_minor(cache_sel_kv), pos_minor(cache_win_kv)
    w_t = jnp.swapaxes(w_in, 1, 2)
    db, dt = x_sample.shape[0], x_sample.shape[1]
    tb_p = _prompt_tables(rel_bias, x_prompt.shape[1])
    tb_s = _sample_tables(rel_bias, past_len, -(-dt // 8) * 8, dt)
    y_p, y_s = x_prompt, x_sample
    for l in range(DEPTH):
        lw = dict(norm_g=norm_g[l], w_perm=_permute_w_in(w_t, l), a_qn=a_qn[l], a_kn=a_kn[l],
                  wk=_cmp_weights(a_cmp_wk[l]), wv=_cmp_weights(a_cmp_wv[l]),
                  m_bi=m_bi[l], m_bf=m_bf[l], m_hn=m_hn[l], g_conv=g_conv[l], g_A_log=g_A_log[l],
                  g_dt_bias=g_dt_bias[l], g_hn=g_hn[l], w_branch=w_branch[l], w_out=w_out[l])
        y_p, new_p = _layer(y_p, lw, tb_p, None, 0)
        past = dict(cmp=pool_cmp, sel=pool_sel, win_t=win_t, layer=l, win=cache_win_kv[l], page_table=page_table,
                    mC=state_mlstm_C[l], mn=state_mlstm_n[l], mm=state_mlstm_m[l],
                    gS=state_gdn_S[l], gconv=state_gdn_conv[l])
        y_s, new_s = _layer(y_s, lw, tb_s, past, past_len)
        for k in names:
            st_p[k].append(new_p[k])
            st_s[k].append(new_s[k])
    P = {k: jnp.stack(v) for k, v in st_p.items()}
    S = {k: jnp.stack(v) for k, v in st_s.items()}
    return (y_p, y_s, P['cmp'], S['cmp'], P['sel'], S['sel'], P['win'], S['win'],
            P['mC'], S['mC'], P['mn'], S['mn'], P['mm'], S['mm'], P['gS'], S['gS'], P['gconv'], S['gconv'])
```

```python
import functools
import math

import jax
import jax.numpy as jnp
from jax import lax
from jax.experimental import pallas as pl
from jax.experimental.pallas import tpu as pltpu

D_MODEL = 1024
DEPTH = 2
PAGE_SIZE = 128
A_HEADS = 8
A_KV = 2
A_REP = A_HEADS // A_KV
A_HD = 64
CMP_BLOCK = 32
CMP_STRIDE = 16
SEL_BLOCK = 64
TOP_N = 16
WINDOW = 512
Q_BLOCK = 128
N_BUCKETS = 32
MAX_DIST = 2048
M_HEADS = 4
M_HD = 128
M_CHUNK = 64
G_HEADS = 4
G_HD = 128
G_CHUNK = 64
CONV_W = 4
W_A = A_HEADS * A_HD
W_M = M_HEADS * M_HD
W_G = G_HEADS * G_HD
N_BRANCH = 3
EPS = 1e-6
NEG = -1e30
TINY = 1e-30
LOG2E = math.log2(math.e)

F32 = jnp.float32
BF16 = jnp.bfloat16
HI = lax.Precision.HIGHEST
NN = (((1,), (0,)), ((), ()))
NT = (((1,), (1,)), ((), ()))
TN = (((0,), (0,)), ((), ()))

LANES = 128
VMEM_LIMIT = 56 * 1024 * 1024

IN_ORDER = ('a_q', 'a_kv', 'a_gate', 'a_z', 'm_qkv', 'm_if', 'm_o', 'm_z', 'g_qkv', 'g_ab', 'g_z', 'merge')
IN_WIDTH = dict(a_q=W_A, a_kv=3 * 2 * A_KV * A_HD, a_gate=A_HEADS * 3, a_z=W_A, m_qkv=3 * W_M, m_if=2 * M_HEADS,
                m_o=W_M, m_z=W_M, g_qkv=3 * W_G, g_ab=2 * G_HEADS, g_z=W_G, merge=N_BRANCH * D_MODEL)
OFF = dict(merge=0, m_qkv=3072, g_qkv=4608, a_q=6144, a_z=6656, m_o=7168, m_z=7680, g_z=8192, a_kv=8704,
           small=9472)
N_PERM = 9600
SM_GATE, SM_MIF, SM_GAB = 0, 24, 32


def _cparams(sem):
    return pltpu.CompilerParams(dimension_semantics=sem, vmem_limit_bytes=VMEM_LIMIT)


def _silu(x):
    return x * jax.nn.sigmoid(x)


def _log_sigmoid(x):
    return jnp.minimum(x, 0.0) - jnp.log(1.0 + jnp.exp(-jnp.abs(x)))


def _softplus(x):
    return jnp.maximum(x, 0.0) + jnp.log(1.0 + jnp.exp(-jnp.abs(x)))


def _split(a):
    hi = a.astype(BF16)
    return hi, (a - hi.astype(F32)).astype(BF16)


def _dot3(a, b, dims=NN):
    mm = lambda x, y: lax.dot_general(x, y, dims, preferred_element_type=F32)
    return mm(a[0], b[0]) + mm(a[0], b[1]) + mm(a[1], b[0])


def _src_offsets():
    offs, off = {}, 0
    for name in IN_ORDER:
        offs[name] = off
        off += IN_WIDTH[name]
    return offs, off


def _permute_kernel(w_ref, o_ref):
    src, _ = _src_offsets()
    small = []
    for name in IN_ORDER:
        w = IN_WIDTH[name]
        rows = w_ref[src[name]:src[name] + w, :]
        if w % LANES:
            small.append(rows)
        else:
            o_ref[:, OFF[name]:OFF[name] + w] = rows.T.astype(BF16)
    used = sum(r.shape[0] for r in small)
    small.append(jnp.zeros((LANES - used, w_ref.shape[1]), F32))
    o_ref[:, OFF['small']:N_PERM] = jnp.concatenate(small, axis=0).T.astype(BF16)


def _permute_w_in(w_t, layer):
    _, n_in, d = w_t.shape
    tr = 128
    return pl.pallas_call(
        _permute_kernel,
        grid=(d // tr,),
        in_specs=[pl.BlockSpec((None, n_in, tr), lambda i: (layer, 0, i))],
        out_specs=pl.BlockSpec((tr, N_PERM), lambda i: (i, 0)),
        out_shape=jax.ShapeDtypeStruct((d, N_PERM), BF16),
        compiler_params=_cparams(("parallel",)),
        name="permute_w",
    )(w_t)


def _rel_bucket(dist):
    n = jnp.maximum(dist, 0)
    exact = N_BUCKETS // 2
    nf = jnp.maximum(n, exact).astype(F32)
    large = exact + (jnp.log(nf / exact) / math.log(MAX_DIST / exact) * (N_BUCKETS - exact)).astype(jnp.int32)
    return jnp.where(n < exact, n, jnp.minimum(large, N_BUCKETS - 1))


def _bias_kernel(thr_ref, tab_ref, d_ref, o_ref):
    n = jnp.maximum(d_ref[0], 0)
    for h in range(A_HEADS):
        acc = jnp.full(n.shape, tab_ref[h], F32)
        for k in range(1, N_BUCKETS):
            acc = jnp.where(n >= thr_ref[k], tab_ref[k * A_HEADS + h], acc)
        o_ref[0, h // A_REP, h % A_REP] = acc


def _bias_rows(rel_bias, dist):
    N, Q, K = dist.shape
    nmax = 2 * MAX_DIST
    thr = jnp.sum(_rel_bucket(jnp.arange(nmax))[None, :] < jnp.arange(N_BUCKETS)[:, None], axis=1).astype(jnp.int32)
    smem = pl.BlockSpec(memory_space=pltpu.SMEM)
    out = pl.pallas_call(
        _bias_kernel,
        grid=(N,),
        in_specs=[smem, smem, pl.BlockSpec((1, Q, K), lambda i: (i, 0, 0))],
        out_specs=pl.BlockSpec((1, A_KV, A_REP, Q, K), lambda i: (i, 0, 0, 0, 0)),
        out_shape=jax.ShapeDtypeStruct((N, A_KV, A_REP, Q, K), F32),
        compiler_params=_cparams(("parallel",)),
        name="bias_rows",
    )(thr, rel_bias.astype(F32).reshape(N_BUCKETS * A_HEADS), dist.astype(jnp.int32))
    return out.reshape(N, A_KV, A_REP * Q, K)


def _cmp_weights(w):
    wr = w.reshape(A_KV, 2, CMP_STRIDE, A_HD, A_HD)
    eye = jnp.eye(A_KV, dtype=w.dtype)
    full = jnp.einsum('gmjde,gh->jgdmhe', wr, eye)
    return full.reshape(CMP_STRIDE, A_KV * A_HD, 2 * A_KV * A_HD).astype(BF16)


def _cover_t(ns_pad, nch):
    s0 = jnp.arange(ns_pad)[:, None] * SEL_BLOCK
    c0 = jnp.arange(nch)[None, :] * CMP_STRIDE
    return ((c0 < s0 + SEL_BLOCK) & (s0 <= c0 + CMP_BLOCK - 1)).astype(F32)


def _expand_mat(ns_pad, nk):
    return (jnp.arange(nk)[None, :] // SEL_BLOCK == jnp.arange(ns_pad)[:, None]).astype(BF16)


def _gate_expand():
    col = jnp.arange(N_BRANCH * W_A)
    src = 3 * ((col % W_A) // A_HD) + col // W_A
    return (jnp.arange(LANES)[:, None] == src[None, :]).astype(BF16)


def _phase_perm():
    r = jnp.arange(PAGE_SIZE)
    per_page = PAGE_SIZE // CMP_STRIDE
    return (jnp.arange(PAGE_SIZE)[None, :] == ((r % per_page) * CMP_STRIDE + r // per_page)[:, None]).astype(BF16)


def _prompt_tables(rel_bias, T):
    QB = Q_BLOCK
    nqb = T // QB
    nch = T // CMP_STRIDE
    ns = T // SEL_BLOCK
    ns_pad = -(-ns // 8) * 8
    i_ = jnp.arange(QB)
    t = (jnp.arange(nqb) * QB)[:, None, None] + i_[None, :, None]
    cend = (jnp.arange(nch) * CMP_STRIDE + CMP_BLOCK - 1)[None, None, :]
    d = (jnp.arange(nqb) * QB)[:, None, None] + i_[None, :, None] - i_[None, None, :]
    nw = WINDOW // QB + 1
    return dict(
        bcmp=_bias_rows(rel_bias, t - cend),
        tsel=_bias_rows(rel_bias * LOG2E, d),
        covt=_cover_t(ns_pad, nch),
        e3=_expand_mat(LANES, T).reshape(LANES, nqb, QB).transpose(1, 0, 2),
        wm=jnp.where((d[:nw] >= 0) & (d[:nw] < WINDOW), 0.0, NEG),
        cm=(d[:2] >= 0).astype(F32),
        gx=_gate_expand())


def _sample_tables(rel_bias, past, tp, t_real):
    SL = past + LANES
    nblk = past // CMP_STRIDE
    ns = -(-(past + t_real) // SEL_BLOCK)
    ns_pad = -(-ns // LANES) * LANES
    WK = WINDOW + LANES
    i_ = jnp.arange(tp)
    t = past + i_
    rows = lambda dist: _bias_rows(rel_bias, dist[None])[0]
    return dict(
        bcmp=rows(t[:, None] - (jnp.arange(nblk) * CMP_STRIDE + CMP_BLOCK - 1)[None, :]),
        bsel=rows(t[:, None] - jnp.arange(SL)[None, :]),
        bwin=rows(i_[:, None] + WINDOW - jnp.arange(WK)[None, :]),
        cov=_cover_t(ns_pad, nblk).T, emat=_expand_mat(ns_pad, SL), perm=_phase_perm())


def _proj_in_kernel(x_ref, g_ref, w_ref, o_ref, hn_ref):
    @pl.when(pl.program_id(1) == 0)
    def _():
        x = x_ref[...]
        ms = jnp.mean(x * x, axis=-1, keepdims=True)
        hn_ref[...] = (x * lax.rsqrt(ms + EPS) * g_ref[...]).astype(BF16)

    o_ref[...] = jnp.dot(hn_ref[...], w_ref[...], preferred_element_type=F32)


def _proj_in(x2, norm_g, w_perm):
    n = x2.shape[0]
    tm = min(1024, n)
    tn = 1920
    return pl.pallas_call(
        _proj_in_kernel,
        grid=(n // tm, N_PERM // tn),
        in_specs=[pl.BlockSpec((tm, D_MODEL), lambda i, j: (i, 0)),
                  pl.BlockSpec((1, D_MODEL), lambda i, j: (0, 0)),
                  pl.BlockSpec((D_MODEL, tn), lambda i, j: (0, j))],
        out_specs=pl.BlockSpec((tm, tn), lambda i, j: (i, j)),
        out_shape=jax.ShapeDtypeStruct((n, N_PERM), F32),
        scratch_shapes=[pltpu.VMEM((tm, D_MODEL), BF16)],
        compiler_params=_cparams(("parallel", "arbitrary")),
        name="proj_in",
    )(x2, norm_g.reshape(1, D_MODEL), w_perm)


def _prep_a_kernel(q_ref, kv0_ref, kv1_ref, kv2_ref, bdq_ref, bdk_ref, qg_ref, kg_ref, qo_ref, ro_ref):
    q = q_ref[...]
    ms = jnp.dot(q * q, bdq_ref[...], precision=HI, preferred_element_type=F32) * (1.0 / A_HD)
    qo_ref[...] = q * lax.rsqrt(ms + EPS) * qg_ref[...] * (A_HD ** -0.5)
    for b, kv_ref in enumerate((kv0_ref, kv1_ref, kv2_ref)):
        k = kv_ref[:, 0:128]
        ms = jnp.dot(k * k, bdk_ref[...], precision=HI, preferred_element_type=F32) * (1.0 / A_HD)
        ro_ref[2 * b] = k * lax.rsqrt(ms + EPS) * kg_ref[b:b + 1, :]
        ro_ref[2 * b + 1] = kv_ref[:, 128:256]


def _prep_a(u2, a_qn, a_kn):
    n = u2.shape[0]
    tm = min(512, n)
    bd = lambda w: (jnp.arange(w)[:, None] // A_HD == jnp.arange(w)[None, :] // A_HD).astype(F32)
    qg = jnp.tile(a_qn, A_HEADS).reshape(1, W_A)
    kg = jnp.tile(a_kn, (1, A_KV))
    const = lambda shape: pl.BlockSpec(shape, lambda i: (0, 0))
    kvs = lambda b: pl.BlockSpec((tm, 256), lambda i: (i, OFF['a_kv'] // 256 + b))
    return pl.pallas_call(
        _prep_a_kernel,
        grid=(n // tm,),
        in_specs=[pl.BlockSpec((tm, W_A), lambda i: (i, OFF['a_q'] // W_A)), kvs(0), kvs(1), kvs(2),
                  const((W_A, W_A)), const((128, 128)), const((1, W_A)), const((3, 128))],
        out_specs=[pl.BlockSpec((tm, W_A), lambda i: (i, 0)), pl.BlockSpec((6, tm, 128), lambda i: (0, i, 0))],
        out_shape=[jax.ShapeDtypeStruct((n, W_A), F32), jax.ShapeDtypeStruct((6, n, 128), F32)],
        compiler_params=_cparams(("parallel",)),
        name="prep_a",
    )(u2, u2, u2, u2, bd(W_A), bd(128), qg, kg)


def _stack_heads(q, g):
    return jnp.concatenate([q[:, (A_REP * g + r) * A_HD:(A_REP * g + r + 1) * A_HD] for r in range(A_REP)], axis=0)


def _rep_rows(x):
    return jnp.concatenate([x] * A_REP, axis=0)


def _masked_softmax(s, valid):
    sm = jnp.where(valid, s, NEG)
    m = jnp.max(sm, axis=-1, keepdims=True)
    e = jnp.where(valid, jnp.exp(sm - m), 0.0)
    l = jnp.sum(e, axis=-1, keepdims=True)
    return e / jnp.maximum(l, TINY)


def _select_blocks(pc_sum, covt_ref, st_ref, t_row, ns, n_top):
    ns_pad = covt_ref.shape[0]
    imp = lax.dot_general(covt_ref[...], pc_sum, NT, precision=HI, preferred_element_type=F32)
    jj = lax.broadcasted_iota(jnp.int32, (ns_pad, LANES), 0)
    tt = jnp.broadcast_to(t_row, (ns_pad, LANES))
    cur = tt // SEL_BLOCK
    forced = (jj == 0) | (jj == cur) | (jj == cur - 1)
    future = jj * SEL_BLOCK > tt
    score = jnp.where(future, NEG, jnp.where(forced, -NEG, imp))
    score = jnp.where(jj < ns, score, -jnp.inf)
    st_ref[...] = score

    def beats(k, rank):
        row = jnp.broadcast_to(st_ref[pl.ds(k, 1), :], (ns_pad, LANES))
        b = (row > score) | ((row == score) & (k < jj))
        return rank + jnp.where(b, 1.0, 0.0)

    rank = jnp.zeros((ns_pad, LANES), F32)
    if ns <= 32:
        for k in range(ns):
            rank = beats(k, rank)
    else:
        rank = lax.fori_loop(0, ns, beats, rank)
    return jnp.where(rank < n_top, 1.0, 0.0)


def _pad_rows(x, rows):
    if x.shape[0] == rows:
        return x
    return jnp.concatenate([x, jnp.zeros((rows - x.shape[0], x.shape[1]), x.dtype)], axis=0)


def _select_rows(pc_sum, cov_ref, sel_ref, t0, ns, n_top, nq):
    Q = pc_sum.shape[0]
    ns_pad = cov_ref.shape[1]
    nk = -(-ns // 8) * 8
    imp = jnp.dot(pc_sum, cov_ref[...], precision=HI, preferred_element_type=F32)
    jj = lax.broadcasted_iota(jnp.int32, (Q, ns_pad), 1)
    tt = lax.broadcasted_iota(jnp.int32, (Q, ns_pad), 0) + t0
    cur = tt // SEL_BLOCK
    forced = (jj == 0) | (jj == cur) | (jj == cur - 1)
    future = jj * SEL_BLOCK > tt
    score = jnp.where(future, NEG, jnp.where(forced, -NEG, imp))
    score = jnp.where(jj < ns, score, -jnp.inf)
    score_col = _pad_rows(score, LANES).T
    kk = lax.broadcasted_iota(jnp.int32, (nk, ns_pad), 0)
    jl = lax.broadcasted_iota(jnp.int32, (nk, ns_pad), 1)
    sel_ref[...] = jnp.zeros(sel_ref.shape, F32)
    for i in range(nq):
        col = jnp.broadcast_to(score_col[0:nk, i:i + 1], (nk, ns_pad))
        row = jnp.broadcast_to(score[i:i + 1, :], (nk, ns_pad))
        beats = (col > row) | ((col == row) & (kk < jl))
        rank = jnp.sum(jnp.where(beats, 1.0, 0.0), axis=0, keepdims=True)
        sel_ref[i:i + 1, :] = jnp.where(rank < n_top, 1.0, 0.0)
    return sel_ref[...]


def _compress(src_ref, nrow, wk_ref, wv_ref, pbuf_ref, kc_ref, vc_ref, by_phase=False):
    acck = jnp.zeros((nrow, 256), F32)
    accv = jnp.zeros((nrow, 256), F32)
    for j in range(CMP_STRIDE):
        if by_phase:
            xk = src_ref[0, j, 0:nrow, :].astype(BF16)
            xv = src_ref[1, j, 0:nrow, :].astype(BF16)
        else:
            xk = src_ref[0, pl.ds(j, nrow, stride=CMP_STRIDE), :].astype(BF16)
            xv = src_ref[1, pl.ds(j, nrow, stride=CMP_STRIDE), :].astype(BF16)
        acck = acck + jnp.dot(xk, wk_ref[j], preferred_element_type=F32)
        accv = accv + jnp.dot(xv, wv_ref[j], preferred_element_type=F32)
    nout = kc_ref.shape[0]
    for acc, dst in ((acck, kc_ref), (accv, vc_ref)):
        pbuf_ref[0:nrow, :] = acc[:, 128:256]
        dst[...] = acc[0:nout, 0:128] + pbuf_ref[pl.ds(1, nout), :]


def _gated_sum(gates, gx_ref, ocat_ref):
    gh, gl = _split(gates)
    gexp = (jnp.dot(gh, gx_ref[...], preferred_element_type=F32)
            + jnp.dot(gl, gx_ref[...], preferred_element_type=F32))
    out = gexp[:, 0:W_A] * ocat_ref[0]
    for c in range(1, N_BRANCH):
        out = out + gexp[:, c * W_A:(c + 1) * W_A] * ocat_ref[c]
    return out


def _nsa_prompt_kernel(q_ref, rows_ref, small_ref, wk_ref, wv_ref, bcmp_ref, tsel_ref, covt_ref, e3_ref,
                       wm_ref, cm_ref, gx_ref,
                       o_ref, kc_ref, vc_ref, pbuf_ref, sel_ref, acc_ref, m_ref, st_ref, ocat_ref,
                       *, ns, n_top):
    bi = pl.program_id(1)
    T = rows_ref.shape[1]
    nch = T // CMP_STRIDE
    QB = Q_BLOCK
    SEL, WIN = 1, 2

    @pl.when(bi == 0)
    def _():
        pbuf_ref[...] = jnp.zeros(pbuf_ref.shape, F32)
        _compress(rows_ref, nch, wk_ref, wv_ref, pbuf_ref, kc_ref, vc_ref)

    t0 = bi * QB
    q = q_ref[0]
    tc = lax.broadcasted_iota(jnp.int32, (QB, nch), 0) + t0
    cend = lax.broadcasted_iota(jnp.int32, (QB, nch), 1) * CMP_STRIDE + (CMP_BLOCK - 1)
    cvalid = _rep_rows(tc - cend >= 0)
    t_row = lax.broadcasted_iota(jnp.int32, (1, LANES), 1) + t0
    lane = lax.broadcasted_iota(jnp.int32, (QB, LANES), 1)
    in_g = [(lane >= g * A_HD) & (lane < (g + 1) * A_HD) for g in range(A_KV)]
    qg = [_stack_heads(q, g).astype(BF16) for g in range(A_KV)]
    qg2 = [(_stack_heads(q, g) * LOG2E).astype(BF16) for g in range(A_KV)]

    def put_heads(c, g, o):
        for r in range(A_REP):
            h = A_REP * g + r
            ocat_ref[c, :, h * A_HD:(h + 1) * A_HD] = o[r * QB:(r + 1) * QB]

    for g in range(A_KV):
        kc = kc_ref[:, g * A_HD:(g + 1) * A_HD].astype(BF16)
        vc = vc_ref[:, g * A_HD:(g + 1) * A_HD].astype(BF16)
        s = lax.dot_general(qg[g], kc, NT, preferred_element_type=F32) + bcmp_ref[0, g]
        p_c = _masked_softmax(s, cvalid)
        put_heads(0, g, jnp.dot(p_c.astype(BF16), vc, preferred_element_type=F32))
        pc_sum = p_c[0:QB] + p_c[QB:2 * QB] + p_c[2 * QB:3 * QB] + p_c[3 * QB:4 * QB]
        sel_t = _select_blocks(pc_sum, covt_ref, st_ref, t_row, ns, n_top)
        sel_ref[g] = _pad_rows(sel_t, LANES).T.astype(BF16)

    m_ref[...] = jnp.full(m_ref.shape, NEG, F32)
    acc_ref[...] = jnp.zeros(acc_ref.shape, F32)

    def tiles(specs):
        cat = lambda xs, ax: xs[0] if len(xs) == 1 else jnp.concatenate(xs, axis=ax)
        cis = [2 * (br - 1) + g for br, g, _, _ in specs]
        scs, v1s = [], []
        for br, g, kb, nb in specs:
            ks, vs, bs = [], [], []
            for j in range(nb):
                off = pl.multiple_of((kb + j) * QB, QB)
                ks.append(rows_ref[2 * br, pl.ds(off, QB), g * A_HD:(g + 1) * A_HD])
                vs.append(jnp.where(in_g[g], rows_ref[2 * br + 1, pl.ds(off, QB), :], 1.0))
                bs.append(tsel_ref[bi - kb - j, g])
            v1s.append(cat(vs, 0).astype(BF16))
            scs.append(lax.dot_general(qg2[g], cat(ks, 0).astype(BF16), NT, preferred_element_type=F32)
                       + cat(bs, 1))
        masks = []
        for br, g, kb, nb in specs:
            ms = []
            for j in range(nb):
                if br == SEL:
                    mt = jnp.dot(sel_ref[g], e3_ref[kb + j], preferred_element_type=F32)
                    mt = mt * cm_ref[jnp.minimum(bi - kb, 1)] if nb == 1 else mt
                    ms.append((mt - 1.0) * (-NEG))
                else:
                    ms.append(wm_ref[bi - kb - j])
            masks.append(_rep_rows(cat(ms, 1)))
        sms = [sc + mk for sc, mk in zip(scs, masks)]
        m_prevs = [m_ref[ci] for ci in cis]
        m_news = [jnp.maximum(mp, jnp.max(sm, axis=-1, keepdims=True)) for mp, sm in zip(m_prevs, sms)]
        ps = [jnp.exp2(sm - cat([mn] * spec[3], 1)).astype(BF16) for sm, mn, spec in zip(sms, m_news, specs)]
        pvs = [jnp.dot(p, v1, preferred_element_type=F32) for p, v1 in zip(ps, v1s)]
        for ci, mp, mn, pv in zip(cis, m_prevs, m_news, pvs):
            acc_ref[ci] = jnp.exp2(mp - mn) * acc_ref[ci] + pv
            m_ref[ci] = mn

    def body_sel(kb, carry):
        tiles([(SEL, g, kb, 1) for g in range(A_KV)])
        return carry

    def body_sel2(pair, carry):
        tiles([(SEL, g, 2 * pair, 2) for g in range(A_KV)])
        return carry

    def body_both(kb, carry):
        tiles([(br, g, kb, 1) for g in range(A_KV) for br in (SEL, WIN)])
        return carry

    lo = jnp.maximum(bi - WINDOW // QB, 0)
    lax.fori_loop(0, lo // 2, body_sel2, 0)
    lax.fori_loop(2 * (lo // 2), lo, body_sel, 0)
    lax.fori_loop(lo, bi + 1, body_both, 0)

    for br in (SEL, WIN):
        for g in range(A_KV):
            acc = acc_ref[2 * (br - 1) + g]
            linv = pltpu.roll(1.0 / jnp.maximum(acc, TINY), A_HD, axis=1)
            put_heads(br, g, (acc * linv)[:, g * A_HD:(g + 1) * A_HD])

    o_ref[0] = _gated_sum(jax.nn.sigmoid(small_ref[0]), gx_ref, ocat_ref)


def _nsa_prompt(qn3, rows6, u3, wk, wv, tb):
    B, T, _ = qn3.shape
    QB = Q_BLOCK
    nqb = T // QB
    nch = T // CMP_STRIDE
    ns = T // SEL_BLOCK
    ns_pad = tb['covt'].shape[0]
    n_top = min(TOP_N, ns)
    nw = tb['wm'].shape[0]
    kern = functools.partial(_nsa_prompt_kernel, ns=ns, n_top=n_top)
    c2 = lambda shape: pl.BlockSpec(shape, lambda b, i: (0,) * len(shape))
    return pl.pallas_call(
        kern,
        grid=(B, nqb),
        in_specs=[pl.BlockSpec((1, QB, W_A), lambda b, i: (b, i, 0)),
                  pl.BlockSpec((6, T, 128), lambda b, i: (0, b, 0)),
                  pl.BlockSpec((1, QB, LANES), lambda b, i: (b, i, OFF['small'] // LANES)),
                  c2((CMP_STRIDE, 128, 256)), c2((CMP_STRIDE, 128, 256)),
                  pl.BlockSpec((1, A_KV, A_REP * QB, nch), lambda b, i: (i, 0, 0, 0)),
                  c2((nqb, A_KV, A_REP * QB, QB)),
                  c2((ns_pad, nch)), c2((nqb, LANES, QB)), c2((nw, QB, QB)), c2((2, QB, QB)),
                  c2((LANES, N_BRANCH * W_A))],
        out_specs=pl.BlockSpec((1, QB, W_A), lambda b, i: (b, i, 0)),
        out_shape=jax.ShapeDtypeStruct((B, T, W_A), F32),
        scratch_shapes=[pltpu.VMEM((nch, 128), F32), pltpu.VMEM((nch, 128), F32),
                        pltpu.VMEM((nch + 8, 128), F32),
                        pltpu.VMEM((A_KV, QB, LANES), BF16),
                        pltpu.VMEM((2 * A_KV, A_REP * QB, LANES), F32),
                        pltpu.VMEM((2 * A_KV, A_REP * QB, LANES), F32),
                        pltpu.VMEM((ns_pad, LANES), F32),
                        pltpu.VMEM((N_BRANCH, QB, W_A), F32)],
        compiler_params=_cparams(("parallel", "arbitrary")),
        name="nsa_prompt",
    )(qn3, rows6, u3, wk, wv, tb['bcmp'], tb['tsel'], tb['covt'], tb['e3'], tb['wm'], tb['cm'], tb['gx'])


def _nsa_sample_kernel(pt_ref, *refs, pg, past, tp, t_real, ns, n_top):
    cmp_refs = refs[0:pg]
    sel_refs = refs[pg:2 * pg]
    (q_ref, rows_ref, small_ref, win_ref, wk_ref, wv_ref, bcmp_ref, bsel_ref, bwin_ref, cov_ref, e_ref, perm_ref,
     o_ref, cslab, sslab, kc_ref, vc_ref, pbuf_ref, st_ref, s_ref) = refs[2 * pg:]
    p = pl.program_id(1)
    npg = pl.num_programs(1)
    n_pages = past // PAGE_SIZE
    SL = (n_pages + 1) * PAGE_SIZE
    nblk = kc_ref.shape[0]
    WK = WINDOW + LANES
    per_page = PAGE_SIZE // CMP_STRIDE

    for k in range(pg):
        page = p * pg + k
        row0 = pl.multiple_of(page * per_page, per_page)
        for kv in range(2):
            tile = cmp_refs[k][kv * 128:(kv + 1) * 128, :].astype(BF16)
            ordered = lax.dot_general(perm_ref[...], tile, NT, preferred_element_type=F32)
            for j in range(CMP_STRIDE):
                cslab[kv, j, pl.ds(row0, per_page), :] = ordered[j * per_page:(j + 1) * per_page]
            sslab[kv, page] = sel_refs[k][kv * 128:(kv + 1) * 128, :]

    @pl.when(p == npg - 1)
    def _():
        new_t = [_pad_rows(rows_ref[i], LANES).T for i in range(2, 6)]
        for kv in range(2):
            cslab[kv, :, nblk:nblk + 16, :] = jnp.zeros((CMP_STRIDE, 16, 128), F32)
            for i in range(tp):
                cslab[kv, i, nblk:nblk + 1, :] = rows_ref[kv, i:i + 1, :]
            sslab[kv, n_pages] = new_t[kv]
        pbuf_ref[...] = jnp.zeros(pbuf_ref.shape, F32)
        _compress(cslab, nblk + 8, wk_ref, wv_ref, pbuf_ref, kc_ref, vc_ref, by_phase=True)

        q = q_ref[0]
        gates = jax.nn.sigmoid(small_ref[0])
        R = A_REP * tp
        zq = jnp.zeros((R, A_HD), F32)
        q2 = jnp.concatenate([jnp.concatenate([_stack_heads(q, 0), zq], axis=1),
                              jnp.concatenate([zq, _stack_heads(q, 1)], axis=1)], axis=0).astype(BF16)
        rep2 = lambda x: jnp.concatenate([x] * (A_KV * A_REP), axis=0)
        ti = lax.broadcasted_iota(jnp.int32, (tp, SL), 0) + past
        causal = ti - lax.broadcasted_iota(jnp.int32, (tp, SL), 1) >= 0
        tc = lax.broadcasted_iota(jnp.int32, (tp, nblk), 0) + past
        cend = lax.broadcasted_iota(jnp.int32, (tp, nblk), 1) * CMP_STRIDE + (CMP_BLOCK - 1)
        wd = (lax.broadcasted_iota(jnp.int32, (tp, WK), 0) + WINDOW
              - lax.broadcasted_iota(jnp.int32, (tp, WK), 1))
        npt = n_pages + 1
        nck = 5 if npt % 5 == 0 else 1
        cpt = npt // nck
        ck = cpt * LANES

        s = lax.dot_general(q2, kc_ref[...].astype(BF16), NT, preferred_element_type=F32) + bcmp_ref[...]
        p_c = _masked_softmax(s, rep2(tc - cend >= 0))
        o_c = jnp.dot(p_c.astype(BF16), vc_ref[...].astype(BF16), preferred_element_type=F32)

        kmasks = []
        for g in range(A_KV):
            pg_ = p_c[g * R:(g + 1) * R]
            pc_sum = pg_[0:tp] + pg_[tp:2 * tp] + pg_[2 * tp:3 * tp] + pg_[3 * tp:4 * tp]
            sel = _select_rows(pc_sum, cov_ref, st_ref, past, ns, n_top, t_real).astype(BF16)
            km = jnp.dot(sel, e_ref[...], preferred_element_type=F32)
            kmasks.append(_rep_rows(causal & (km > 0.5)))
        svalid = jnp.concatenate(kmasks, axis=0)

        def sel_t(kv, c):
            return jnp.concatenate([sslab[kv, t] for t in range(c * cpt, (c + 1) * cpt)], axis=1).astype(BF16)

        for c in range(nck):
            s_ref[:, c * ck:(c + 1) * ck] = jnp.dot(q2, sel_t(0, c), preferred_element_type=F32)
        s_ref[...] = _masked_softmax(s_ref[...] + bsel_ref[...], svalid)
        o_s = jnp.zeros((A_KV * R, 2 * A_HD), F32)
        for c in range(nck):
            o_s = o_s + lax.dot_general(s_ref[:, c * ck:(c + 1) * ck].astype(BF16), sel_t(1, c), NT,
                                        preferred_element_type=F32)

        kw = jnp.concatenate([win_ref[0:128, :], new_t[2]], axis=1).astype(BF16)
        vw = jnp.concatenate([win_ref[128:256, :], new_t[3]], axis=1).astype(BF16)
        sw = jnp.dot(q2, kw, preferred_element_type=F32) + bwin_ref[...]
        p_w = _masked_softmax(sw, rep2((wd >= 0) & (wd < WINDOW)))
        o_w = lax.dot_general(p_w.astype(BF16), vw, NT, preferred_element_type=F32)

        for g in range(A_KV):
            gs = slice(g * A_HD, (g + 1) * A_HD)
            for r in range(A_REP):
                h = A_REP * g + r
                rs = slice(g * R + r * tp, g * R + (r + 1) * tp)
                out = (gates[:, 3 * h:3 * h + 1] * o_c[rs, gs] + gates[:, 3 * h + 1:3 * h + 2] * o_s[rs, gs]
                       + gates[:, 3 * h + 2:3 * h + 3] * o_w[rs, gs])
                o_ref[0, :, h * A_HD:(h + 1) * A_HD] = out


def _nsa_sample(qn3, rows6, u3, pool_cmp, pool_sel, win_t, layer, page_table, wk, wv, tb, past, t_real):
    B, tp, _ = qn3.shape
    n_pages = past // PAGE_SIZE
    pg = 8 if n_pages % 8 == 0 else n_pages
    npg = n_pages // pg
    SL = past + LANES
    nblk = past // CMP_STRIDE
    ns = -(-(past + t_real) // SEL_BLOCK)
    ns_pad = tb['cov'].shape[1]
    n_top = min(TOP_N, ns)
    WK = WINDOW + LANES
    kern = functools.partial(_nsa_sample_kernel, pg=pg, past=past, tp=tp, t_real=t_real, ns=ns, n_top=n_top)

    def page_spec(k):
        return pl.BlockSpec((None, None, 256, PAGE_SIZE), lambda b, p, pt: (layer, pt[b, p * pg + k], 0, 0))

    def c_(shape):
        return pl.BlockSpec(shape, lambda b, p, pt: (0,) * len(shape))

    R = A_REP * tp
    grid_spec = pltpu.PrefetchScalarGridSpec(
        num_scalar_prefetch=1,
        grid=(B, npg),
        in_specs=([page_spec(k) for k in range(pg)] + [page_spec(k) for k in range(pg)]
                  + [pl.BlockSpec((1, tp, W_A), lambda b, p, pt: (b, 0, 0)),
                     pl.BlockSpec((6, tp, 128), lambda b, p, pt: (0, b, 0)),
                     pl.BlockSpec((1, tp, LANES), lambda b, p, pt: (b, 0, OFF['small'] // LANES)),
                     pl.BlockSpec((None, None, 256, WINDOW), lambda b, p, pt: (layer, b, 0, 0)),
                     c_((CMP_STRIDE, 128, 256)), c_((CMP_STRIDE, 128, 256)),
                     c_((A_KV * R, nblk)), c_((A_KV * R, SL)), c_((A_KV * R, WK)),
                     c_((nblk, ns_pad)), c_((ns_pad, SL)), c_((PAGE_SIZE, PAGE_SIZE))]),
        out_specs=pl.BlockSpec((1, tp, W_A), lambda b, p, pt: (b, 0, 0)),
        scratch_shapes=[pltpu.VMEM((2, CMP_STRIDE, nblk + 16, 128), F32),
                        pltpu.VMEM((2, n_pages + 1, 128, PAGE_SIZE), F32),
                        pltpu.VMEM((nblk, 128), F32), pltpu.VMEM((nblk, 128), F32),
                        pltpu.VMEM((nblk + 16, 128), F32),
                        pltpu.VMEM((tp, ns_pad), F32), pltpu.VMEM((A_KV * R, SL), F32)],
    )
    return pl.pallas_call(
        kern,
        grid_spec=grid_spec,
        out_shape=jax.ShapeDtypeStruct((B, tp, W_A), F32),
        compiler_params=_cparams(("parallel", "arbitrary")),
        name="nsa_sample",
    )(page_table, *([pool_cmp] * pg), *([pool_sel] * pg), qn3, rows6, u3, win_t, wk, wv,
      *(tb[k].reshape(A_KV * R, -1) for k in ('bcmp', 'bsel', 'bwin')), tb['cov'], tb['emat'], tb['perm'])


def _tri(L, lower_incl):
    r = lax.broadcasted_iota(jnp.int32, (L, L), 0)
    c = lax.broadcasted_iota(jnp.int32, (L, L), 1)
    return (r >= c) if lower_incl else (r > c)


def _mlstm_kernel(x_ref, small_ref, smallt_ref, brow_ref, bcol_ref, c0_ref, n0_ref, m0_ref,
                  h_ref, co_ref, no_ref, mo_ref, c_s, n_s, m_s, *, L, t_real):
    c = pl.program_id(1)
    nc = pl.num_programs(1)

    @pl.when(c == 0)
    def _():
        c_s[...] = c0_ref[0]
        n_s[...] = n0_ref[0]
        m_s[...] = m0_ref[0]

    sm = small_ref[0]
    smt = smallt_ref[0, 0]
    li_col = sm[:, SM_MIF:SM_MIF + 4] + brow_ref[0:1, 0:4]
    lf_col = _log_sigmoid(sm[:, SM_MIF + 4:SM_MIF + 8] + brow_ref[0:1, 4:8])
    li_row = smt[0:4, :] + bcol_ref[0:4, :]
    lf_row = _log_sigmoid(smt[4:8, :] + bcol_ref[4:8, :])
    if t_real % L:
        tcol = lax.broadcasted_iota(jnp.int32, (L, 4), 0) + c * L
        trow = lax.broadcasted_iota(jnp.int32, (4, L), 1) + c * L
        li_col = jnp.where(tcol < t_real, li_col, NEG)
        lf_col = jnp.where(tcol < t_real, lf_col, 0.0)
        li_row = jnp.where(trow < t_real, li_row, NEG)
        lf_row = jnp.where(trow < t_real, lf_row, 0.0)
    low = _tri(L, True)
    b_col = jnp.dot(low.astype(F32), lf_col, precision=HI, preferred_element_type=F32)
    b_row = lax.dot_general(lf_row, low.astype(F32), NT, precision=HI, preferred_element_type=F32)

    hd = range(M_HEADS)
    q = [x_ref[0, :, h * M_HD:(h + 1) * M_HD] for h in hd]
    k = [x_ref[0, :, W_M + h * M_HD:W_M + (h + 1) * M_HD] * (M_HD ** -0.5) for h in hd]
    v = [x_ref[0, :, 2 * W_M + h * M_HD:2 * W_M + (h + 1) * M_HD] for h in hd]
    bc = [b_col[:, h:h + 1] for h in hd]
    Dm = [jnp.where(low, bc[h] - b_row[h:h + 1, :] + li_row[h:h + 1, :], NEG) for h in hd]
    m_prev = [m_s[h][:, 0:1] for h in hd]
    a = [bc[h] + m_prev[h] for h in hd]
    mt = [jnp.maximum(a[h], jnp.max(Dm[h], axis=-1, keepdims=True)) for h in hd]
    qk = [lax.dot_general(q[h], k[h], NT, preferred_element_type=F32) for h in hd]
    S = [qk[h] * jnp.exp(Dm[h] - mt[h]) for h in hd]
    inter = [jnp.exp(a[h] - mt[h]) for h in hd]
    C = [c_s[h] for h in hd]
    n = [n_s[h] for h in hd]
    qC = [jnp.dot(q[h], C[h], preferred_element_type=F32) for h in hd]
    Sv = [jnp.dot(S[h], v[h], preferred_element_type=F32) for h in hd]
    den = [inter[h] * jnp.sum(q[h] * n[h], axis=-1, keepdims=True) + jnp.sum(S[h], axis=-1, keepdims=True)
           for h in hd]
    for h in hd:
        h_ref[0, :, h * M_HD:(h + 1) * M_HD] = ((inter[h] * qC[h] + Sv[h])
                                                / jnp.maximum(jnp.abs(den[h]), jnp.exp(-mt[h])))
    bL = [bc[h][L - 1:L, :] for h in hd]
    wlog = [bL[h] - bc[h] + li_col[:, h:h + 1] for h in hd]
    m_new = [jnp.maximum(bL[h] + m_prev[h], jnp.max(wlog[h], axis=0, keepdims=True)) for h in hd]
    dec = [jnp.exp(bL[h] + m_prev[h] - m_new[h]) for h in hd]
    kw = [k[h] * jnp.exp(wlog[h] - m_new[h]) for h in hd]
    kv = [lax.dot_general(kw[h], v[h], TN, preferred_element_type=F32) for h in hd]
    for h in hd:
        c_s[h] = dec[h] * C[h] + kv[h]
        n_s[h] = dec[h] * n[h] + jnp.sum(kw[h], axis=0, keepdims=True)
        m_s[h] = jnp.broadcast_to(m_new[h], (1, LANES))

    @pl.when(c == nc - 1)
    def _():
        co_ref[0] = c_s[...]
        no_ref[0] = n_s[...]
        mo_ref[0] = m_s[...]


def _mlstm(u3, smallt, m_bi, m_bf, C0, n0, m0, L, t_real):
    B, T, _ = u3.shape
    nc = T // L
    brow = jnp.concatenate([m_bi, m_bf]).reshape(1, 8)
    bcol = jnp.concatenate([m_bi, m_bf]).reshape(8, 1)
    n0 = n0.reshape(B, M_HEADS, 1, M_HD)
    m0 = jnp.broadcast_to(m0[:, :, None, None], (B, M_HEADS, 1, LANES))
    kern = functools.partial(_mlstm_kernel, L=L, t_real=t_real)
    st = lambda shape: pl.BlockSpec(shape, lambda b, c: (b,) + (0,) * (len(shape) - 1))
    h, Cn, nn, mn = pl.pallas_call(
        kern,
        grid=(B, nc),
        in_specs=[pl.BlockSpec((1, L, 3 * W_M), lambda b, c: (b, c, OFF['m_qkv'] // (3 * W_M))),
                  pl.BlockSpec((1, L, LANES), lambda b, c: (b, c, OFF['small'] // LANES)),
                  pl.BlockSpec((1, 1, 16, L), lambda b, c: (b, c, 0, 0)),
                  pl.BlockSpec((1, 8), lambda b, c: (0, 0)), pl.BlockSpec((8, 1), lambda b, c: (0, 0)),
                  st((1, M_HEADS, M_HD, M_HD)), st((1, M_HEADS, 1, M_HD)), st((1, M_HEADS, 1, LANES))],
        out_specs=[pl.BlockSpec((1, L, W_M), lambda b, c: (b, c, 0)),
                   st((1, M_HEADS, M_HD, M_HD)), st((1, M_HEADS, 1, M_HD)), st((1, M_HEADS, 1, LANES))],
        out_shape=[jax.ShapeDtypeStruct((B, T, W_M), F32),
                   jax.ShapeDtypeStruct((B, M_HEADS, M_HD, M_HD), F32),
                   jax.ShapeDtypeStruct((B, M_HEADS, 1, M_HD), F32),
                   jax.ShapeDtypeStruct((B, M_HEADS, 1, LANES), F32)],
        scratch_shapes=[pltpu.VMEM((M_HEADS, M_HD, M_HD), F32), pltpu.VMEM((M_HEADS, 1, M_HD), F32),
                        pltpu.VMEM((M_HEADS, 1, LANES), F32)],
        compiler_params=_cparams(("parallel", "arbitrary")),
        name="mlstm",
    )(u3, u3, smallt, brow, bcol, C0, n0, m0)
    return h, Cn, nn.reshape(B, M_HEADS, M_HD), mn[:, :, 0, 0]


def _gdn_a_kernel(x_ref, small_ref, smallt_ref, cw_ref, prow_ref, pcol_ref, buf_ref,
                  u_ref, w_ref, qe_ref, kd_ref, at_ref, eg_ref, xbuf, *, L, CB, t_real):
    c = pl.program_id(1)
    PRE = 8
    R = CB * L

    @pl.when(c == 0)
    def _():
        xbuf[0:PRE, :] = buf_ref[0]

    xbuf[PRE:PRE + R, :] = x_ref[0]
    conv = jnp.zeros((R, 3 * W_G), F32)
    for j in range(CONV_W):
        conv = conv + xbuf[pl.ds(PRE - (CONV_W - 1) + j, R), :] * cw_ref[j:j + 1, :]
    xbuf[0:PRE, :] = xbuf[R:R + PRE, :]
    conv = _silu(conv)

    sm = small_ref[0]
    g_all = -jnp.exp(prow_ref[0:1, 0:4]) * _softplus(sm[:, SM_GAB:SM_GAB + 4] + prow_ref[1:2, 0:4])
    beta_all = jax.nn.sigmoid(sm[:, SM_GAB + 4:SM_GAB + 8])
    lowf = _tri(L, True).astype(F32)

    H = G_HEADS
    lane_h = lax.broadcasted_iota(jnp.int32, (L, H * L), 1) // L
    row_p = lax.broadcasted_iota(jnp.int32, (L, H * L), 0)
    col_p = lax.broadcasted_iota(jnp.int32, (L, H * L), 1) % L
    blockmask = (lax.broadcasted_iota(jnp.int32, (H * L, H * L), 0) // L
                 == lax.broadcasted_iota(jnp.int32, (H * L, H * L), 1) // L)

    def pack_diag(full):
        out = full[0:L]
        for h in range(1, H):
            out = jnp.where(lane_h == h, full[h * L:(h + 1) * L], out)
        return out

    def block_diag(m, parts):
        if L % 16:
            return _split(jnp.where(blockmask, jnp.concatenate([m] * H, axis=0), 0.0))
        return tuple(jnp.where(blockmask, jnp.concatenate([p] * H, axis=0), jnp.zeros((), BF16))
                     for p in parts)

    a_list, rhs_list = [], []
    for cb in range(CB):
        rs = slice(cb * L, (cb + 1) * L)
        smt = smallt_ref[0, cb]
        g_col = g_all[rs]
        beta_col = beta_all[rs]
        g_row = -jnp.exp(pcol_ref[0:4, 0:1]) * _softplus(smt[8:12, :] + pcol_ref[0:4, 1:2])
        if t_real % L:
            t0 = (c * CB + cb) * L
            tcol = lax.broadcasted_iota(jnp.int32, (L, 4), 0) + t0
            trow = lax.broadcasted_iota(jnp.int32, (4, L), 1) + t0
            g_col = jnp.where(tcol < t_real, g_col, 0.0)
            beta_col = jnp.where(tcol < t_real, beta_col, 0.0)
            g_row = jnp.where(trow < t_real, g_row, 0.0)
        G_col = jnp.dot(lowf, g_col, precision=HI, preferred_element_type=F32)
        G_row = lax.dot_general(g_row, lowf, NT, precision=HI, preferred_element_type=F32)

        qs, ks, kbs, rhss = [], [], [], []
        for h in range(G_HEADS):
            hs = slice(h * G_HD, (h + 1) * G_HD)
            cq = conv[rs, h * G_HD:(h + 1) * G_HD]
            ck = conv[rs, W_G + h * G_HD:W_G + (h + 1) * G_HD]
            v = conv[rs, 2 * W_G + h * G_HD:2 * W_G + (h + 1) * G_HD]
            q = cq * lax.rsqrt(jnp.sum(cq * cq, axis=-1, keepdims=True) + EPS) * (G_HD ** -0.5)
            k = ck * lax.rsqrt(jnp.sum(ck * ck, axis=-1, keepdims=True) + EPS)
            Gc = G_col[:, h:h + 1]
            bcol = beta_col[:, h:h + 1]
            kb = k * bcol
            eG = jnp.exp(Gc)
            GL = Gc[L - 1:L, :]
            qs.append(q)
            ks.append(k)
            kbs.append(kb)
            rhss.append(jnp.concatenate([v * bcol, kb * eG], axis=1))
            qe_ref[0, rs, hs] = q * eG
            kd_ref[0, rs, hs] = k * jnp.exp(GL - Gc)
            eg_ref[0, cb, h:h + 1, :] = jnp.broadcast_to(jnp.exp(GL), (1, LANES))
        eg_ref[0, cb, G_HEADS:8, :] = jnp.zeros((8 - G_HEADS, LANES), F32)

        Gc_p = jnp.concatenate([jnp.broadcast_to(G_col[:, h:h + 1], (L, L)) for h in range(H)], axis=1)
        Gr_p = jnp.concatenate([jnp.broadcast_to(G_row[h:h + 1, :], (L, L)) for h in range(H)], axis=1)
        low_p = row_p >= col_p
        dmask = jnp.where(low_p, jnp.exp(jnp.where(low_p, Gc_p - Gr_p, 0.0)), 0.0)
        k_st = _split(jnp.concatenate(ks, axis=0))
        A = pack_diag(_dot3(_split(jnp.concatenate(kbs, axis=0)), k_st, NT)) * jnp.where(row_p > col_p, dmask, 0.0)
        attn = pack_diag(lax.dot_general(jnp.concatenate(qs, axis=0).astype(BF16), k_st[0], NT,
                                         preferred_element_type=F32)) * dmask
        at_ref[0, rs, :] = attn
        a_list.append(A)
        rhs_list.append(_split(jnp.concatenate(rhss, axis=0)))

    chunks = range(CB)
    X = [jnp.where(row_p == col_p, 1.0, 0.0) - a_list[cb] for cb in chunks]
    As = [_split(a_list[cb]) for cb in chunks]
    Pw = [_dot3(As[cb], block_diag(a_list[cb], As[cb])) for cb in chunks]
    span = 2
    while span < L:
        Ps = [_split(Pw[cb]) for cb in chunks]
        Pbd = [block_diag(Pw[cb], Ps[cb]) for cb in chunks]
        X = [X[cb] + _dot3(_split(X[cb]), Pbd[cb]) for cb in chunks]
        span *= 2
        if span < L:
            Pw = [_dot3(Ps[cb], Pbd[cb]) for cb in chunks]
    sol = [_dot3(block_diag(X[cb], _split(X[cb])), rhs_list[cb]) for cb in chunks]
    for cb in chunks:
        for h in range(H):
            rs = slice(cb * L, (cb + 1) * L)
            hs = slice(h * G_HD, (h + 1) * G_HD)
            u_ref[0, rs, hs] = sol[cb][h * L:(h + 1) * L, 0:G_HD]
            w_ref[0, rs, hs] = sol[cb][h * L:(h + 1) * L, G_HD:2 * G_HD]


def _gdn_b_kernel(u_ref, w_ref, qe_ref, kd_ref, at_ref, eg_ref, s0_ref, o_ref, so_ref, s_s, *, L, BB):
    c = pl.program_id(1)
    nc = pl.num_programs(1)

    @pl.when(c == 0)
    def _():
        s_s[...] = s0_ref[...]

    ch = [(b, h, slice(h * G_HD, (h + 1) * G_HD)) for b in range(BB) for h in range(G_HEADS)]
    S = [s_s[b, h] for b, h, _ in ch]
    r = [jnp.dot(jnp.concatenate([w_ref[b, :, hs], qe_ref[b, :, hs]], axis=0), S[i], preferred_element_type=F32)
         for i, (b, h, hs) in enumerate(ch)]
    v_new = [u_ref[b, :, hs] - r[i][0:L] for i, (b, h, hs) in enumerate(ch)]
    av = [jnp.dot(at_ref[b, :, h * L:(h + 1) * L], v_new[i], preferred_element_type=F32)
          for i, (b, h, hs) in enumerate(ch)]
    kdv = [lax.dot_general(kd_ref[b, :, hs], v_new[i], TN, preferred_element_type=F32)
           for i, (b, h, hs) in enumerate(ch)]
    for i, (b, h, hs) in enumerate(ch):
        o_ref[b, :, hs] = r[i][L:2 * L] + av[i]
        s_s[b, h] = eg_ref[b, 0, h:h + 1, :] * S[i] + kdv[i]

    @pl.when(c == nc - 1)
    def _():
        so_ref[...] = s_s[...]


def _gdn(u3, smallt, g_conv, g_A_log, g_dt_bias, buf, S0, L, t_real):
    B, T, _ = u3.shape
    nc = T // L
    CB = 4 if nc % 4 == 0 else 1
    BB = 8 if B % 8 == 0 else B
    prow = jnp.stack([g_A_log, g_dt_bias])
    pcol = jnp.stack([g_A_log, g_dt_bias], axis=1)
    buf8 = jnp.concatenate([jnp.zeros((B, 8 - (CONV_W - 1), 3 * W_G), F32), buf], axis=1)
    st = lambda shape: pl.BlockSpec(shape, lambda b, c: (b,) + (0,) * (len(shape) - 1))
    cst = lambda shape: pl.BlockSpec(shape, lambda b, c: (0,) * len(shape))
    row = lambda nb, r, w: pl.BlockSpec((nb, r, w), lambda b, c: (b, c, 0))
    tok = lambda w: jax.ShapeDtypeStruct((B, T, w), F32)
    u, w, qe, kd, at, eg = pl.pallas_call(
        functools.partial(_gdn_a_kernel, L=L, CB=CB, t_real=t_real),
        grid=(B, nc // CB),
        in_specs=[pl.BlockSpec((1, CB * L, 3 * W_G), lambda b, c: (b, c, OFF['g_qkv'] // (3 * W_G))),
                  pl.BlockSpec((1, CB * L, LANES), lambda b, c: (b, c, OFF['small'] // LANES)),
                  pl.BlockSpec((1, CB, 16, L), lambda b, c: (b, c, 0, 0)),
                  cst((CONV_W, 3 * W_G)), cst((2, 4)), cst((4, 2)), st((1, 8, 3 * W_G))],
        out_specs=[row(1, CB * L, W_G)] * 4 + [row(1, CB * L, G_HEADS * L),
                                               pl.BlockSpec((1, CB, 8, LANES), lambda b, c: (b, c, 0, 0))],
        out_shape=[tok(W_G)] * 4 + [tok(G_HEADS * L), jax.ShapeDtypeStruct((B, nc, 8, LANES), F32)],
        scratch_shapes=[pltpu.VMEM((CB * L + 8, 3 * W_G), F32)],
        compiler_params=_cparams(("parallel", "arbitrary")),
        name="gdn_a",
    )(u3, u3, smallt, g_conv, prow, pcol, buf8)
    sblk = pl.BlockSpec((BB, G_HEADS, G_HD, G_HD), lambda b, c: (b, 0, 0, 0))
    return pl.pallas_call(
        functools.partial(_gdn_b_kernel, L=L, BB=BB),
        grid=(B // BB, nc),
        in_specs=[row(BB, L, W_G)] * 4 + [row(BB, L, G_HEADS * L),
                                          pl.BlockSpec((BB, 1, 8, LANES), lambda b, c: (b, c, 0, 0)), sblk],
        out_specs=[row(BB, L, W_G), sblk],
        out_shape=[tok(W_G), jax.ShapeDtypeStruct((B, G_HEADS, G_HD, G_HD), F32)],
        scratch_shapes=[pltpu.VMEM((BB, G_HEADS, G_HD, G_HD), F32)],
        compiler_params=_cparams(("parallel", "arbitrary")),
        name="gdn_b",
    )(u, w, qe, kd, at, eg, S0)


def _head_rmsnorm(x, gain_ref, nheads, hd):
    outs = []
    for h in range(nheads):
        xh = x[:, h * hd:(h + 1) * hd]
        ms = jnp.mean(xh * xh, axis=-1, keepdims=True)
        outs.append(xh * lax.rsqrt(ms + EPS) * gain_ref[...])
    return jnp.concatenate(outs, axis=1)


def _merge_kernel(x_ref, oa_ref, hm_ref, og_ref, az_ref, mo_ref, mz_ref, gz_ref, mg_ref,
                  wb_ref, wo_ref, mhn_ref, ghn_ref, y_ref):
    oa = oa_ref[...] * _silu(az_ref[...])
    om = _head_rmsnorm(hm_ref[...], mhn_ref, M_HEADS, M_HD) * jax.nn.sigmoid(mo_ref[...]) * _silu(mz_ref[...])
    og = _head_rmsnorm(og_ref[...], ghn_ref, G_HEADS, G_HD) * _silu(gz_ref[...])
    y = jnp.zeros(y_ref.shape, F32)
    for i, br in enumerate((oa, om, og)):
        proj = jnp.dot(br.astype(BF16), wb_ref[i], preferred_element_type=F32)
        y = y + jax.nn.sigmoid(mg_ref[:, i * D_MODEL:(i + 1) * D_MODEL]) * proj
    y_ref[...] = x_ref[...] + jnp.dot(y.astype(BF16), wo_ref[...], preferred_element_type=F32)


def _merge_out(x2, o_a, h_m, o_g, u2, w_branch, w_out, m_hn, g_hn):
    n = x2.shape[0]
    tm = min(512, n)
    row = lambda w, off: pl.BlockSpec((tm, w), lambda i: (i, off // w))
    cst = lambda shape: pl.BlockSpec(shape, lambda i: (0,) * len(shape))
    return pl.pallas_call(
        _merge_kernel,
        grid=(n // tm,),
        in_specs=[row(D_MODEL, 0), row(W_A, 0), row(W_M, 0), row(W_G, 0),
                  row(W_A, OFF['a_z']), row(W_M, OFF['m_o']), row(W_M, OFF['m_z']), row(W_G, OFF['g_z']),
                  row(N_BRANCH * D_MODEL, OFF['merge']),
                  cst((N_BRANCH, W_A, D_MODEL)), cst((D_MODEL, D_MODEL)), cst((1, M_HD)), cst((1, G_HD))],
        out_specs=row(D_MODEL, 0),
        out_shape=jax.ShapeDtypeStruct((n, D_MODEL), F32),
        compiler_params=_cparams(("parallel",)),
        name="merge_out",
    )(x2, o_a, h_m, o_g, u2, u2, u2, u2, u2, w_branch.astype(BF16), w_out.astype(BF16),
      m_hn.reshape(1, M_HD), g_hn.reshape(1, G_HD))


def _small_t(u3, L):
    B, T, _ = u3.shape
    s = u3[:, :, OFF['small'] + SM_MIF:OFF['small'] + SM_MIF + 16]
    return s.reshape(B, T // L, L, 16).transpose(0, 1, 3, 2)


def _layer(x, lw, tb, past, q_off):
    B, T, _ = x.shape
    x2 = x.reshape(B * T, D_MODEL)
    u2 = _proj_in(x2, lw['norm_g'], lw['w_perm'])
    if past is None:
        tp, L = T, math.gcd(T, M_CHUNK)
        u3 = u2.reshape(B, T, N_PERM)
        C0 = jnp.zeros((B, M_HEADS, M_HD, M_HD), F32)
        n0 = jnp.zeros((B, M_HEADS, M_HD), F32)
        m0 = jnp.zeros((B, M_HEADS), F32)
        S0 = jnp.zeros((B, G_HEADS, G_HD, G_HD), F32)
        buf = jnp.zeros((B, CONV_W - 1, 3 * W_G), F32)
    else:
        tp = -(-T // 8) * 8
        L = tp
        u3 = jnp.pad(u2.reshape(B, T, N_PERM), ((0, 0), (0, tp - T), (0, 0)))
        C0, n0, m0, S0, buf = past['mC'], past['mn'], past['mm'], past['gS'], past['gconv']
    up = u3.reshape(B * tp, N_PERM)
    qn, rows6 = _prep_a(up, lw['a_qn'], lw['a_kn'])
    qn3 = qn.reshape(B, tp, W_A)
    if past is None:
        o_a = _nsa_prompt(qn3, rows6, u3, lw['wk'], lw['wv'], tb)
    else:
        o_a = _nsa_sample(qn3, rows6, u3, past['cmp'], past['sel'], past['win_t'], past['layer'], past['page_table'],
                          lw['wk'], lw['wv'], tb, q_off, T)
    smallt = _small_t(u3, L)
    h_m, Cn, nn, mn = _mlstm(u3, smallt, lw['m_bi'], lw['m_bf'], C0, n0, m0, L, T)
    o_g, Sn = _gdn(u3, smallt, lw['g_conv'], lw['g_A_log'], lw['g_dt_bias'], buf, S0, L, T)
    if tp != T:
        o_a, h_m, o_g = o_a[:, :T], h_m[:, :T], o_g[:, :T]
    y2 = _merge_out(x2, o_a.reshape(B * T, W_A), h_m.reshape(B * T, W_M), o_g.reshape(B * T, W_G), u2,
                    lw['w_branch'], lw['w_out'], lw['m_hn'], lw['g_hn'])
    rows = rows6.reshape(3, 2, B, tp, A_KV, A_HD)[:, :, :, :T]
    new_cmp, new_sel, new_win = (jnp.moveaxis(rows[b], 0, 2) for b in range(3))
    g_qkv = u3[:, :T, OFF['g_qkv']:OFF['g_qkv'] + 3 * W_G]
    if past is None:
        win = new_win
    else:
        win = jnp.concatenate([past['win'], new_win], axis=1)
    full = jnp.concatenate([buf, g_qkv[:, max(T - (CONV_W - 1), 0):]], axis=1)
    state = dict(cmp=new_cmp, sel=new_sel, win=win[:, -min(WINDOW, win.shape[1]):],
                 mC=Cn, mn=nn, mm=mn, gS=Sn, gconv=full[:, -(CONV_W - 1):])
    return y2.reshape(B, T, D_MODEL), state


def kernel(x_prompt, x_sample, cache_cmp_kv, cache_sel_kv, cache_win_kv, state_mlstm_C, state_mlstm_n,
           state_mlstm_m, state_gdn_S, state_gdn_conv, page_table, norm_g, w_in, a_qn, a_kn, a_cmp_wk,
           a_cmp_wv, rel_bias, m_bi, m_bf, m_hn, g_conv, g_A_log, g_dt_bias, g_hn, w_branch, w_out):
    names = ('cmp', 'sel', 'win', 'mC', 'mn', 'mm', 'gS', 'gconv')
    st_p = {k: [] for k in names}
    st_s = {k: [] for k in names}
    past_len = page_table.shape[1] * PAGE_SIZE
    n_pool = cache_cmp_kv.shape[1]
    pos_minor = lambda c: jnp.swapaxes(c.reshape(c.shape[0], c.shape[1], c.shape[2], 2 * A_KV * A_HD), 2, 3)
    pool_cmp, pool_sel, win_t = pos_minor(cache_cmp_kv), pos_minor(cache_sel_kv), pos_minor(cache_win_kv)
    w_t = jnp.swapaxes(w_in, 1, 2)
    db, dt = x_sample.shape[0], x_sample.shape[1]
    tb_p = _prompt_tables(rel_bias, x_prompt.shape[1])
    tb_s = _sample_tables(rel_bias, past_len, -(-dt // 8) * 8, dt)
    y_p, y_s = x_prompt, x_sample
    for l in range(DEPTH):
        lw = dict(norm_g=norm_g[l], w_perm=_permute_w_in(w_t, l), a_qn=a_qn[l], a_kn=a_kn[l],
                  wk=_cmp_weights(a_cmp_wk[l]), wv=_cmp_weights(a_cmp_wv[l]),
                  m_bi=m_bi[l], m_bf=m_bf[l], m_hn=m_hn[l], g_conv=g_conv[l], g_A_log=g_A_log[l],
                  g_dt_bias=g_dt_bias[l], g_hn=g_hn[l], w_branch=w_branch[l], w_out=w_out[l])
        y_p, new_p = _layer(y_p, lw, tb_p, None, 0)
        past = dict(cmp=pool_cmp, sel=pool_sel, win_t=win_t, layer=l, win=cache_win_kv[l], page_table=page_table,
                    mC=state_mlstm_C[l], mn=state_mlstm_n[l], mm=state_mlstm_m[l],
                    gS=state_gdn_S[l], gconv=state_gdn_conv[l])
        y_s, new_s = _layer(y_s, lw, tb_s, past, past_len)
        for k in names:
            st_p[k].append(new_p[k])
            st_s[k].append(new_s[k])
    P = {k: jnp.stack(v) for k, v in st_p.items()}
    S = {k: jnp.stack(v) for k, v in st_s.items()}
    return (y_p, y_s, P['cmp'], S['cmp'], P['sel'], S['sel'], P['win'], S['win'],
            P['mC'], S['mC'], P['mn'], S['mn'], P['mm'], S['mm'], P['gS'], S['gS'], P['gconv'], S['gconv'])
```

```python
import functools
import math

import jax
import jax.numpy as jnp
from jax import lax
from jax.experimental import pallas as pl
from jax.experimental.pallas import tpu as pltpu

D_MODEL = 1024
DEPTH = 2
PAGE_SIZE = 128
A_HEADS = 8
A_KV = 2
A_REP = A_HEADS // A_KV
A_HD = 64
CMP_BLOCK = 32
CMP_STRIDE = 16
SEL_BLOCK = 64
TOP_N = 16
WINDOW = 512
Q_BLOCK = 128
N_BUCKETS = 32
MAX_DIST = 2048
M_HEADS = 4
M_HD = 128
M_CHUNK = 64
G_HEADS = 4
G_HD = 128
G_CHUNK = 64
CONV_W = 4
W_A = A_HEADS * A_HD
W_M = M_HEADS * M_HD
W_G = G_HEADS * G_HD
N_BRANCH = 3
EPS = 1e-6
NEG = -1e30
TINY = 1e-30
LOG2E = math.log2(math.e)

F32 = jnp.float32
BF16 = jnp.bfloat16
HI = lax.Precision.HIGHEST
NN = (((1,), (0,)), ((), ()))
NT = (((1,), (1,)), ((), ()))
TN = (((0,), (0,)), ((), ()))

LANES = 128
VMEM_LIMIT = 56 * 1024 * 1024

IN_ORDER = ('a_q', 'a_kv', 'a_gate', 'a_z', 'm_qkv', 'm_if', 'm_o', 'm_z', 'g_qkv', 'g_ab', 'g_z', 'merge')
IN_WIDTH = dict(a_q=W_A, a_kv=3 * 2 * A_KV * A_HD, a_gate=A_HEADS * 3, a_z=W_A, m_qkv=3 * W_M, m_if=2 * M_HEADS,
                m_o=W_M, m_z=W_M, g_qkv=3 * W_G, g_ab=2 * G_HEADS, g_z=W_G, merge=N_BRANCH * D_MODEL)
OFF = dict(merge=0, m_qkv=3072, g_qkv=4608, a_q=6144, a_z=6656, m_o=7168, m_z=7680, g_z=8192, a_kv=8704,
           small=9472)
N_PERM = 9600
SM_GATE, SM_MIF, SM_GAB = 0, 24, 32


def _cparams(sem):
    return pltpu.CompilerParams(dimension_semantics=sem, vmem_limit_bytes=VMEM_LIMIT)


def _silu(x):
    return x * jax.nn.sigmoid(x)


def _log_sigmoid(x):
    return jnp.minimum(x, 0.0) - jnp.log(1.0 + jnp.exp(-jnp.abs(x)))


def _softplus(x):
    return jnp.maximum(x, 0.0) + jnp.log(1.0 + jnp.exp(-jnp.abs(x)))


def _split(a):
    hi = a.astype(BF16)
    return hi, (a - hi.astype(F32)).astype(BF16)


def _dot3(a, b, dims=NN):
    mm = lambda x, y: lax.dot_general(x, y, dims, preferred_element_type=F32)
    return mm(a[0], b[0]) + mm(a[0], b[1]) + mm(a[1], b[0])


def _src_offsets():
    offs, off = {}, 0
    for name in IN_ORDER:
        offs[name] = off
        off += IN_WIDTH[name]
    return offs, off


def _permute_kernel(w_ref, o_ref):
    src, _ = _src_offsets()
    small = []
    for name in IN_ORDER:
        w = IN_WIDTH[name]
        rows = w_ref[src[name]:src[name] + w, :]
        if w % LANES:
            small.append(rows)
        else:
            o_ref[:, OFF[name]:OFF[name] + w] = rows.T.astype(BF16)
    used = sum(r.shape[0] for r in small)
    small.append(jnp.zeros((LANES - used, w_ref.shape[1]), F32))
    o_ref[:, OFF['small']:N_PERM] = jnp.concatenate(small, axis=0).T.astype(BF16)


def _permute_w_in(w_t, layer):
    _, n_in, d = w_t.shape
    tr = 128
    return pl.pallas_call(
        _permute_kernel,
        grid=(d // tr,),
        in_specs=[pl.BlockSpec((None, n_in, tr), lambda i: (layer, 0, i))],
        out_specs=pl.BlockSpec((tr, N_PERM), lambda i: (i, 0)),
        out_shape=jax.ShapeDtypeStruct((d, N_PERM), BF16),
        compiler_params=_cparams(("parallel",)),
        name="permute_w",
    )(w_t)


def _rel_bucket(dist):
    n = jnp.maximum(dist, 0)
    exact = N_BUCKETS // 2
    nf = jnp.maximum(n, exact).astype(F32)
    large = exact + (jnp.log(nf / exact) / math.log(MAX_DIST / exact) * (N_BUCKETS - exact)).astype(jnp.int32)
    return jnp.where(n < exact, n, jnp.minimum(large, N_BUCKETS - 1))


def _bias_kernel(thr_ref, tab_ref, d_ref, o_ref):
    n = jnp.maximum(d_ref[0], 0)
    for h in range(A_HEADS):
        acc = jnp.full(n.shape, tab_ref[h], F32)
        for k in range(1, N_BUCKETS):
            acc = jnp.where(n >= thr_ref[k], tab_ref[k * A_HEADS + h], acc)
        o_ref[0, h // A_REP, h % A_REP] = acc


def _bias_rows(rel_bias, dist):
    N, Q, K = dist.shape
    nmax = 2 * MAX_DIST
    thr = jnp.sum(_rel_bucket(jnp.arange(nmax))[None, :] < jnp.arange(N_BUCKETS)[:, None], axis=1).astype(jnp.int32)
    smem = pl.BlockSpec(memory_space=pltpu.SMEM)
    out = pl.pallas_call(
        _bias_kernel,
        grid=(N,),
        in_specs=[smem, smem, pl.BlockSpec((1, Q, K), lambda i: (i, 0, 0))],
        out_specs=pl.BlockSpec((1, A_KV, A_REP, Q, K), lambda i: (i, 0, 0, 0, 0)),
        out_shape=jax.ShapeDtypeStruct((N, A_KV, A_REP, Q, K), F32),
        compiler_params=_cparams(("parallel",)),
        name="bias_rows",
    )(thr, rel_bias.astype(F32).reshape(N_BUCKETS * A_HEADS), dist.astype(jnp.int32))
    return out.reshape(N, A_KV, A_REP * Q, K)


def _cmp_weights(w):
    wr = w.reshape(A_KV, 2, CMP_STRIDE, A_HD, A_HD)
    eye = jnp.eye(A_KV, dtype=w.dtype)
    full = jnp.einsum('gmjde,gh->jgdmhe', wr, eye)
    return full.reshape(CMP_STRIDE, A_KV * A_HD, 2 * A_KV * A_HD).astype(BF16)


def _cover_t(ns_pad, nch):
    s0 = jnp.arange(ns_pad)[:, None] * SEL_BLOCK
    c0 = jnp.arange(nch)[None, :] * CMP_STRIDE
    return ((c0 < s0 + SEL_BLOCK) & (s0 <= c0 + CMP_BLOCK - 1)).astype(F32)


def _expand_mat(ns_pad, nk):
    return (jnp.arange(nk)[None, :] // SEL_BLOCK == jnp.arange(ns_pad)[:, None]).astype(BF16)


def _gate_expand():
    col = jnp.arange(N_BRANCH * W_A)
    src = 3 * ((col % W_A) // A_HD) + col // W_A
    return (jnp.arange(LANES)[:, None] == src[None, :]).astype(BF16)


def _phase_perm():
    r = jnp.arange(PAGE_SIZE)
    per_page = PAGE_SIZE // CMP_STRIDE
    return (jnp.arange(PAGE_SIZE)[None, :] == ((r % per_page) * CMP_STRIDE + r // per_page)[:, None]).astype(BF16)


def _prompt_tables(rel_bias, T):
    QB = Q_BLOCK
    nqb = T // QB
    nch = T // CMP_STRIDE
    ns = T // SEL_BLOCK
    ns_pad = -(-ns // 8) * 8
    i_ = jnp.arange(QB)
    t = (jnp.arange(nqb) * QB)[:, None, None] + i_[None, :, None]
    cend = (jnp.arange(nch) * CMP_STRIDE + CMP_BLOCK - 1)[None, None, :]
    d = (jnp.arange(nqb) * QB)[:, None, None] + i_[None, :, None] - i_[None, None, :]
    nw = WINDOW // QB + 1
    return dict(
        bcmp=_bias_rows(rel_bias, t - cend),
        tsel=_bias_rows(rel_bias * LOG2E, d),
        covt=_cover_t(ns_pad, nch),
        e3=_expand_mat(LANES, T).reshape(LANES, nqb, QB).transpose(1, 0, 2),
        wm=jnp.where((d[:nw] >= 0) & (d[:nw] < WINDOW), 0.0, NEG),
        cm=(d[:2] >= 0).astype(F32),
        gx=_gate_expand())


def _sample_tables(rel_bias, past, tp, t_real):
    SL = past + LANES
    nblk = past // CMP_STRIDE
    ns = -(-(past + t_real) // SEL_BLOCK)
    ns_pad = -(-ns // LANES) * LANES
    WK = WINDOW + LANES
    i_ = jnp.arange(tp)
    t = past + i_
    rows = lambda dist: _bias_rows(rel_bias, dist[None])[0]
    return dict(
        bcmp=rows(t[:, None] - (jnp.arange(nblk) * CMP_STRIDE + CMP_BLOCK - 1)[None, :]),
        bsel=rows(t[:, None] - jnp.arange(SL)[None, :]),
        bwin=rows(i_[:, None] + WINDOW - jnp.arange(WK)[None, :]),
        cov=_cover_t(ns_pad, nblk).T, emat=_expand_mat(ns_pad, SL), perm=_phase_perm())


def _proj_in_kernel(x_ref, g_ref, w_ref, o_ref, hn_ref):
    @pl.when(pl.program_id(1) == 0)
    def _():
        x = x_ref[...]
        ms = jnp.mean(x * x, axis=-1, keepdims=True)
        hn_ref[...] = (x * lax.rsqrt(ms + EPS) * g_ref[...]).astype(BF16)

    o_ref[...] = jnp.dot(hn_ref[...], w_ref[...], preferred_element_type=F32).astype(o_ref.dtype)


def _proj_in(x2, norm_g, w_perm, out_dtype):
    n = x2.shape[0]
    tm = min(1024, n)
    tn = 1920
    return pl.pallas_call(
        _proj_in_kernel,
        grid=(n // tm, N_PERM // tn),
        in_specs=[pl.BlockSpec((tm, D_MODEL), lambda i, j: (i, 0)),
                  pl.BlockSpec((1, D_MODEL), lambda i, j: (0, 0)),
                  pl.BlockSpec((D_MODEL, tn), lambda i, j: (0, j))],
        out_specs=pl.BlockSpec((tm, tn), lambda i, j: (i, j)),
        out_shape=jax.ShapeDtypeStruct((n, N_PERM), out_dtype),
        scratch_shapes=[pltpu.VMEM((tm, D_MODEL), BF16)],
        compiler_params=_cparams(("parallel", "arbitrary")),
        name="proj_in",
    )(x2, norm_g.reshape(1, D_MODEL), w_perm)


def _prep_a_kernel(q_ref, kv0_ref, kv1_ref, kv2_ref, bdq_ref, bdk_ref, qg_ref, kg_ref, qo_ref, ro_ref):
    q = q_ref[...].astype(F32)
    ms = jnp.dot(q * q, bdq_ref[...], precision=HI, preferred_element_type=F32) * (1.0 / A_HD)
    qo_ref[...] = q * lax.rsqrt(ms + EPS) * qg_ref[...] * (A_HD ** -0.5)
    for b, kv_ref in enumerate((kv0_ref, kv1_ref, kv2_ref)):
        k = kv_ref[:, 0:128].astype(F32)
        ms = jnp.dot(k * k, bdk_ref[...], precision=HI, preferred_element_type=F32) * (1.0 / A_HD)
        ro_ref[2 * b] = k * lax.rsqrt(ms + EPS) * kg_ref[b:b + 1, :]
        ro_ref[2 * b + 1] = kv_ref[:, 128:256].astype(F32)


def _prep_a(u2, a_qn, a_kn):
    n = u2.shape[0]
    tm = min(512, n)
    bd = lambda w: (jnp.arange(w)[:, None] // A_HD == jnp.arange(w)[None, :] // A_HD).astype(F32)
    qg = jnp.tile(a_qn, A_HEADS).reshape(1, W_A)
    kg = jnp.tile(a_kn, (1, A_KV))
    const = lambda shape: pl.BlockSpec(shape, lambda i: (0, 0))
    kvs = lambda b: pl.BlockSpec((tm, 256), lambda i: (i, OFF['a_kv'] // 256 + b))
    return pl.pallas_call(
        _prep_a_kernel,
        grid=(n // tm,),
        in_specs=[pl.BlockSpec((tm, W_A), lambda i: (i, OFF['a_q'] // W_A)), kvs(0), kvs(1), kvs(2),
                  const((W_A, W_A)), const((128, 128)), const((1, W_A)), const((3, 128))],
        out_specs=[pl.BlockSpec((tm, W_A), lambda i: (i, 0)), pl.BlockSpec((6, tm, 128), lambda i: (0, i, 0))],
        out_shape=[jax.ShapeDtypeStruct((n, W_A), F32), jax.ShapeDtypeStruct((6, n, 128), F32)],
        compiler_params=_cparams(("parallel",)),
        name="prep_a",
    )(u2, u2, u2, u2, bd(W_A), bd(128), qg, kg)


def _stack_heads(q, g):
    return jnp.concatenate([q[:, (A_REP * g + r) * A_HD:(A_REP * g + r + 1) * A_HD] for r in range(A_REP)], axis=0)


def _rep_rows(x):
    return jnp.concatenate([x] * A_REP, axis=0)


def _masked_softmax(s, valid):
    sm = jnp.where(valid, s, NEG)
    m = jnp.max(sm, axis=-1, keepdims=True)
    e = jnp.where(valid, jnp.exp(sm - m), 0.0)
    l = jnp.sum(e, axis=-1, keepdims=True)
    return e / jnp.maximum(l, TINY)


def _select_blocks(pc_sum, covt_ref, st_ref, t_row, ns, n_top):
    ns_pad = covt_ref.shape[0]
    imp = lax.dot_general(covt_ref[...], pc_sum, NT, precision=HI, preferred_element_type=F32)
    jj = lax.broadcasted_iota(jnp.int32, (ns_pad, LANES), 0)
    tt = jnp.broadcast_to(t_row, (ns_pad, LANES))
    cur = tt // SEL_BLOCK
    forced = (jj == 0) | (jj == cur) | (jj == cur - 1)
    future = jj * SEL_BLOCK > tt
    score = jnp.where(future, NEG, jnp.where(forced, -NEG, imp))
    score = jnp.where(jj < ns, score, -jnp.inf)
    st_ref[...] = score

    def beats(k, rank):
        row = jnp.broadcast_to(st_ref[pl.ds(k, 1), :], (ns_pad, LANES))
        b = (row > score) | ((row == score) & (k < jj))
        return rank + jnp.where(b, 1.0, 0.0)

    rank = jnp.zeros((ns_pad, LANES), F32)
    if ns <= 32:
        for k in range(ns):
            rank = beats(k, rank)
    else:
        rank = lax.fori_loop(0, ns, beats, rank)
    return jnp.where(rank < n_top, 1.0, 0.0)


def _pad_rows(x, rows):
    if x.shape[0] == rows:
        return x
    return jnp.concatenate([x, jnp.zeros((rows - x.shape[0], x.shape[1]), x.dtype)], axis=0)


def _select_rows(pc_sum, cov_ref, sel_ref, t0, ns, n_top, nq):
    Q = pc_sum.shape[0]
    ns_pad = cov_ref.shape[1]
    nk = -(-ns // 8) * 8
    imp = jnp.dot(pc_sum, cov_ref[...], precision=HI, preferred_element_type=F32)
    jj = lax.broadcasted_iota(jnp.int32, (Q, ns_pad), 1)
    tt = lax.broadcasted_iota(jnp.int32, (Q, ns_pad), 0) + t0
    cur = tt // SEL_BLOCK
    forced = (jj == 0) | (jj == cur) | (jj == cur - 1)
    future = jj * SEL_BLOCK > tt
    score = jnp.where(future, NEG, jnp.where(forced, -NEG, imp))
    score = jnp.where(jj < ns, score, -jnp.inf)
    score_col = _pad_rows(score, LANES).T
    kk = lax.broadcasted_iota(jnp.int32, (nk, ns_pad), 0)
    jl = lax.broadcasted_iota(jnp.int32, (nk, ns_pad), 1)
    sel_ref[...] = jnp.zeros(sel_ref.shape, F32)
    for i in range(nq):
        col = jnp.broadcast_to(score_col[0:nk, i:i + 1], (nk, ns_pad))
        row = jnp.broadcast_to(score[i:i + 1, :], (nk, ns_pad))
        beats = (col > row) | ((col == row) & (kk < jl))
        rank = jnp.sum(jnp.where(beats, 1.0, 0.0), axis=0, keepdims=True)
        sel_ref[i:i + 1, :] = jnp.where(rank < n_top, 1.0, 0.0)
    return sel_ref[...]


def _compress(src_ref, nrow, wk_ref, wv_ref, pbuf_ref, kc_ref, vc_ref, by_phase=False):
    acck = jnp.zeros((nrow, 256), F32)
    accv = jnp.zeros((nrow, 256), F32)
    for j in range(CMP_STRIDE):
        if by_phase:
            xk = src_ref[0, j, 0:nrow, :].astype(BF16)
            xv = src_ref[1, j, 0:nrow, :].astype(BF16)
        else:
            xk = src_ref[0, pl.ds(j, nrow, stride=CMP_STRIDE), :].astype(BF16)
            xv = src_ref[1, pl.ds(j, nrow, stride=CMP_STRIDE), :].astype(BF16)
        acck = acck + jnp.dot(xk, wk_ref[j], preferred_element_type=F32)
        accv = accv + jnp.dot(xv, wv_ref[j], preferred_element_type=F32)
    nout = kc_ref.shape[0]
    for acc, dst in ((acck, kc_ref), (accv, vc_ref)):
        pbuf_ref[0:nrow, :] = acc[:, 128:256]
        dst[...] = acc[0:nout, 0:128] + pbuf_ref[pl.ds(1, nout), :]


def _gated_sum(gates, gx_ref, ocat_ref):
    gh, gl = _split(gates)
    gexp = (jnp.dot(gh, gx_ref[...], preferred_element_type=F32)
            + jnp.dot(gl, gx_ref[...], preferred_element_type=F32))
    out = gexp[:, 0:W_A] * ocat_ref[0]
    for c in range(1, N_BRANCH):
        out = out + gexp[:, c * W_A:(c + 1) * W_A] * ocat_ref[c]
    return out


def _nsa_prompt_kernel(q_ref, rows_ref, small_ref, wk_ref, wv_ref, bcmp_ref, tsel_ref, covt_ref, e3_ref,
                       wm_ref, cm_ref, gx_ref,
                       o_ref, kc_ref, vc_ref, pbuf_ref, sel_ref, acc_ref, m_ref, st_ref, ocat_ref,
                       *, ns, n_top):
    bi = pl.program_id(1)
    T = rows_ref.shape[1]
    nch = T // CMP_STRIDE
    QB = Q_BLOCK
    SEL, WIN = 1, 2

    @pl.when(bi == 0)
    def _():
        pbuf_ref[...] = jnp.zeros(pbuf_ref.shape, F32)
        _compress(rows_ref, nch, wk_ref, wv_ref, pbuf_ref, kc_ref, vc_ref)

    t0 = bi * QB
    q = q_ref[0]
    tc = lax.broadcasted_iota(jnp.int32, (QB, nch), 0) + t0
    cend = lax.broadcasted_iota(jnp.int32, (QB, nch), 1) * CMP_STRIDE + (CMP_BLOCK - 1)
    cvalid = _rep_rows(tc - cend >= 0)
    t_row = lax.broadcasted_iota(jnp.int32, (1, LANES), 1) + t0
    lane = lax.broadcasted_iota(jnp.int32, (QB, LANES), 1)
    in_g = [(lane >= g * A_HD) & (lane < (g + 1) * A_HD) for g in range(A_KV)]
    qg = [_stack_heads(q, g).astype(BF16) for g in range(A_KV)]
    qg2 = [(_stack_heads(q, g) * LOG2E).astype(BF16) for g in range(A_KV)]

    def put_heads(c, g, o):
        for r in range(A_REP):
            h = A_REP * g + r
            ocat_ref[c, :, h * A_HD:(h + 1) * A_HD] = o[r * QB:(r + 1) * QB]

    for g in range(A_KV):
        kc = kc_ref[:, g * A_HD:(g + 1) * A_HD].astype(BF16)
        vc = vc_ref[:, g * A_HD:(g + 1) * A_HD].astype(BF16)
        s = lax.dot_general(qg[g], kc, NT, preferred_element_type=F32) + bcmp_ref[0, g]
        p_c = _masked_softmax(s, cvalid)
        put_heads(0, g, jnp.dot(p_c.astype(BF16), vc, preferred_element_type=F32))
        pc_sum = p_c[0:QB] + p_c[QB:2 * QB] + p_c[2 * QB:3 * QB] + p_c[3 * QB:4 * QB]
        sel_t = _select_blocks(pc_sum, covt_ref, st_ref, t_row, ns, n_top)
        sel_ref[g] = _pad_rows(sel_t, LANES).T.astype(BF16)

    m_ref[...] = jnp.full(m_ref.shape, NEG, F32)
    acc_ref[...] = jnp.zeros(acc_ref.shape, F32)

    def tiles(specs):
        cat = lambda xs, ax: xs[0] if len(xs) == 1 else jnp.concatenate(xs, axis=ax)
        cis = [2 * (br - 1) + g for br, g, _, _ in specs]
        scs, v1s = [], []
        for br, g, kb, nb in specs:
            ks, vs, bs = [], [], []
            for j in range(nb):
                off = pl.multiple_of((kb + j) * QB, QB)
                ks.append(rows_ref[2 * br, pl.ds(off, QB), g * A_HD:(g + 1) * A_HD])
                vs.append(jnp.where(in_g[g], rows_ref[2 * br + 1, pl.ds(off, QB), :], 1.0))
                bs.append(tsel_ref[bi - kb - j, g])
            v1s.append(cat(vs, 0).astype(BF16))
            scs.append(lax.dot_general(qg2[g], cat(ks, 0).astype(BF16), NT, preferred_element_type=F32)
                       + cat(bs, 1))
        masks = []
        for br, g, kb, nb in specs:
            ms = []
            for j in range(nb):
                if br == SEL:
                    mt = jnp.dot(sel_ref[g], e3_ref[kb + j], preferred_element_type=F32)
                    mt = mt * cm_ref[jnp.minimum(bi - kb, 1)] if nb == 1 else mt
                    ms.append((mt - 1.0) * (-NEG))
                else:
                    ms.append(wm_ref[bi - kb - j])
            masks.append(_rep_rows(cat(ms, 1)))
        sms = [sc + mk for sc, mk in zip(scs, masks)]
        m_prevs = [m_ref[ci] for ci in cis]
        m_news = [jnp.maximum(mp, jnp.max(sm, axis=-1, keepdims=True)) for mp, sm in zip(m_prevs, sms)]
        ps = [jnp.exp2(sm - cat([mn] * spec[3], 1)).astype(BF16) for sm, mn, spec in zip(sms, m_news, specs)]
        pvs = [jnp.dot(p, v1, preferred_element_type=F32) for p, v1 in zip(ps, v1s)]
        for ci, mp, mn, pv in zip(cis, m_prevs, m_news, pvs):
            acc_ref[ci] = jnp.exp2(mp - mn) * acc_ref[ci] + pv
            m_ref[ci] = mn

    def body_sel(kb, carry):
        tiles([(SEL, g, kb, 1) for g in range(A_KV)])
        return carry

    def body_sel2(pair, carry):
        tiles([(SEL, g, 2 * pair, 2) for g in range(A_KV)])
        return carry

    def body_both(kb, carry):
        tiles([(br, g, kb, 1) for g in range(A_KV) for br in (SEL, WIN)])
        return carry

    lo = jnp.maximum(bi - WINDOW // QB, 0)
    lax.fori_loop(0, lo // 2, body_sel2, 0)
    lax.fori_loop(2 * (lo // 2), lo, body_sel, 0)
    lax.fori_loop(lo, bi + 1, body_both, 0)

    for br in (SEL, WIN):
        for g in range(A_KV):
            acc = acc_ref[2 * (br - 1) + g]
            linv = pltpu.roll(1.0 / jnp.maximum(acc, TINY), A_HD, axis=1)
            put_heads(br, g, (acc * linv)[:, g * A_HD:(g + 1) * A_HD])

    o_ref[0] = _gated_sum(jax.nn.sigmoid(small_ref[0].astype(F32)), gx_ref, ocat_ref)


def _nsa_prompt(qn3, rows6, u3, wk, wv, tb):
    B, T, _ = qn3.shape
    QB = Q_BLOCK
    nqb = T // QB
    nch = T // CMP_STRIDE
    ns = T // SEL_BLOCK
    ns_pad = tb['covt'].shape[0]
    n_top = min(TOP_N, ns)
    nw = tb['wm'].shape[0]
    kern = functools.partial(_nsa_prompt_kernel, ns=ns, n_top=n_top)
    c2 = lambda shape: pl.BlockSpec(shape, lambda b, i: (0,) * len(shape))
    return pl.pallas_call(
        kern,
        grid=(B, nqb),
        in_specs=[pl.BlockSpec((1, QB, W_A), lambda b, i: (b, i, 0)),
                  pl.BlockSpec((6, T, 128), lambda b, i: (0, b, 0)),
                  pl.BlockSpec((1, QB, LANES), lambda b, i: (b, i, OFF['small'] // LANES)),
                  c2((CMP_STRIDE, 128, 256)), c2((CMP_STRIDE, 128, 256)),
                  pl.BlockSpec((1, A_KV, A_REP * QB, nch), lambda b, i: (i, 0, 0, 0)),
                  c2((nqb, A_KV, A_REP * QB, QB)),
                  c2((ns_pad, nch)), c2((nqb, LANES, QB)), c2((nw, QB, QB)), c2((2, QB, QB)),
                  c2((LANES, N_BRANCH * W_A))],
        out_specs=pl.BlockSpec((1, QB, W_A), lambda b, i: (b, i, 0)),
        out_shape=jax.ShapeDtypeStruct((B, T, W_A), F32),
        scratch_shapes=[pltpu.VMEM((nch, 128), F32), pltpu.VMEM((nch, 128), F32),
                        pltpu.VMEM((nch + 8, 128), F32),
                        pltpu.VMEM((A_KV, QB, LANES), BF16),
                        pltpu.VMEM((2 * A_KV, A_REP * QB, LANES), F32),
                        pltpu.VMEM((2 * A_KV, A_REP * QB, LANES), F32),
                        pltpu.VMEM((ns_pad, LANES), F32),
                        pltpu.VMEM((N_BRANCH, QB, W_A), F32)],
        compiler_params=_cparams(("parallel", "arbitrary")),
        name="nsa_prompt",
    )(qn3, rows6, u3, wk, wv, tb['bcmp'], tb['tsel'], tb['covt'], tb['e3'], tb['wm'], tb['cm'], tb['gx'])


def _nsa_sample_kernel(pt_ref, *refs, pg, past, tp, t_real, ns, n_top):
    cmp_refs = refs[0:pg]
    sel_refs = refs[pg:2 * pg]
    (q_ref, rows_ref, small_ref, win_ref, wk_ref, wv_ref, bcmp_ref, bsel_ref, bwin_ref, cov_ref, e_ref, perm_ref,
     o_ref, cslab, sslab, kc_ref, vc_ref, pbuf_ref, st_ref, s_ref) = refs[2 * pg:]
    p = pl.program_id(1)
    npg = pl.num_programs(1)
    n_pages = past // PAGE_SIZE
    SL = (n_pages + 1) * PAGE_SIZE
    nblk = kc_ref.shape[0]
    WK = WINDOW + LANES
    per_page = PAGE_SIZE // CMP_STRIDE

    for k in range(pg):
        page = p * pg + k
        row0 = pl.multiple_of(page * per_page, per_page)
        for kv in range(2):
            tile = cmp_refs[k][kv * 128:(kv + 1) * 128, :].astype(BF16)
            ordered = lax.dot_general(perm_ref[...], tile, NT, preferred_element_type=F32)
            for j in range(CMP_STRIDE):
                cslab[kv, j, pl.ds(row0, per_page), :] = ordered[j * per_page:(j + 1) * per_page]
            sslab[kv, page] = sel_refs[k][kv * 128:(kv + 1) * 128, :]

    @pl.when(p == npg - 1)
    def _():
        new_t = [_pad_rows(rows_ref[i], LANES).T for i in range(2, 6)]
        for kv in range(2):
            cslab[kv, :, nblk:nblk + 16, :] = jnp.zeros((CMP_STRIDE, 16, 128), F32)
            for i in range(tp):
                cslab[kv, i, nblk:nblk + 1, :] = rows_ref[kv, i:i + 1, :]
            sslab[kv, n_pages] = new_t[kv]
        pbuf_ref[...] = jnp.zeros(pbuf_ref.shape, F32)
        _compress(cslab, nblk + 8, wk_ref, wv_ref, pbuf_ref, kc_ref, vc_ref, by_phase=True)

        q = q_ref[0]
        gates = jax.nn.sigmoid(small_ref[0].astype(F32))
        R = A_REP * tp
        zq = jnp.zeros((R, A_HD), F32)
        q2 = jnp.concatenate([jnp.concatenate([_stack_heads(q, 0), zq], axis=1),
                              jnp.concatenate([zq, _stack_heads(q, 1)], axis=1)], axis=0).astype(BF16)
        rep2 = lambda x: jnp.concatenate([x] * (A_KV * A_REP), axis=0)
        ti = lax.broadcasted_iota(jnp.int32, (tp, SL), 0) + past
        causal = ti - lax.broadcasted_iota(jnp.int32, (tp, SL), 1) >= 0
        tc = lax.broadcasted_iota(jnp.int32, (tp, nblk), 0) + past
        cend = lax.broadcasted_iota(jnp.int32, (tp, nblk), 1) * CMP_STRIDE + (CMP_BLOCK - 1)
        wd = (lax.broadcasted_iota(jnp.int32, (tp, WK), 0) + WINDOW
              - lax.broadcasted_iota(jnp.int32, (tp, WK), 1))
        npt = n_pages + 1
        nck = 5 if npt % 5 == 0 else 1
        cpt = npt // nck
        ck = cpt * LANES

        s = lax.dot_general(q2, kc_ref[...].astype(BF16), NT, preferred_element_type=F32) + bcmp_ref[...]
        p_c = _masked_softmax(s, rep2(tc - cend >= 0))
        o_c = jnp.dot(p_c.astype(BF16), vc_ref[...].astype(BF16), preferred_element_type=F32)

        kmasks = []
        for g in range(A_KV):
            pg_ = p_c[g * R:(g + 1) * R]
            pc_sum = pg_[0:tp] + pg_[tp:2 * tp] + pg_[2 * tp:3 * tp] + pg_[3 * tp:4 * tp]
            sel = _select_rows(pc_sum, cov_ref, st_ref, past, ns, n_top, t_real).astype(BF16)
            km = jnp.dot(sel, e_ref[...], preferred_element_type=F32)
            kmasks.append(_rep_rows(causal & (km > 0.5)))
        svalid = jnp.concatenate(kmasks, axis=0)

        def sel_t(kv, c):
            return jnp.concatenate([sslab[kv, t] for t in range(c * cpt, (c + 1) * cpt)], axis=1).astype(BF16)

        for c in range(nck):
            s_ref[:, c * ck:(c + 1) * ck] = jnp.dot(q2, sel_t(0, c), preferred_element_type=F32)
        s_ref[...] = _masked_softmax(s_ref[...] + bsel_ref[...], svalid)
        o_s = jnp.zeros((A_KV * R, 2 * A_HD), F32)
        for c in range(nck):
            o_s = o_s + lax.dot_general(s_ref[:, c * ck:(c + 1) * ck].astype(BF16), sel_t(1, c), NT,
                                        preferred_element_type=F32)

        kw = jnp.concatenate([win_ref[0:128, :], new_t[2]], axis=1).astype(BF16)
        vw = jnp.concatenate([win_ref[128:256, :], new_t[3]], axis=1).astype(BF16)
        sw = jnp.dot(q2, kw, preferred_element_type=F32) + bwin_ref[...]
        p_w = _masked_softmax(sw, rep2((wd >= 0) & (wd < WINDOW)))
        o_w = lax.dot_general(p_w.astype(BF16), vw, NT, preferred_element_type=F32)

        for g in range(A_KV):
            gs = slice(g * A_HD, (g + 1) * A_HD)
            for r in range(A_REP):
                h = A_REP * g + r
                rs = slice(g * R + r * tp, g * R + (r + 1) * tp)
                out = (gates[:, 3 * h:3 * h + 1] * o_c[rs, gs] + gates[:, 3 * h + 1:3 * h + 2] * o_s[rs, gs]
                       + gates[:, 3 * h + 2:3 * h + 3] * o_w[rs, gs])
                o_ref[0, :, h * A_HD:(h + 1) * A_HD] = out


def _nsa_sample(qn3, rows6, u3, pool_cmp, pool_sel, win_t, layer, page_table, wk, wv, tb, past, t_real):
    B, tp, _ = qn3.shape
    n_pages = past // PAGE_SIZE
    pg = 8 if n_pages % 8 == 0 else n_pages
    npg = n_pages // pg
    SL = past + LANES
    nblk = past // CMP_STRIDE
    ns = -(-(past + t_real) // SEL_BLOCK)
    ns_pad = tb['cov'].shape[1]
    n_top = min(TOP_N, ns)
    WK = WINDOW + LANES
    kern = functools.partial(_nsa_sample_kernel, pg=pg, past=past, tp=tp, t_real=t_real, ns=ns, n_top=n_top)

    def page_spec(k):
        return pl.BlockSpec((None, None, 256, PAGE_SIZE), lambda b, p, pt: (layer, pt[b, p * pg + k], 0, 0))

    def c_(shape):
        return pl.BlockSpec(shape, lambda b, p, pt: (0,) * len(shape))

    R = A_REP * tp
    grid_spec = pltpu.PrefetchScalarGridSpec(
        num_scalar_prefetch=1,
        grid=(B, npg),
        in_specs=([page_spec(k) for k in range(pg)] + [page_spec(k) for k in range(pg)]
                  + [pl.BlockSpec((1, tp, W_A), lambda b, p, pt: (b, 0, 0)),
                     pl.BlockSpec((6, tp, 128), lambda b, p, pt: (0, b, 0)),
                     pl.BlockSpec((1, tp, LANES), lambda b, p, pt: (b, 0, OFF['small'] // LANES)),
                     pl.BlockSpec((None, None, 256, WINDOW), lambda b, p, pt: (layer, b, 0, 0)),
                     c_((CMP_STRIDE, 128, 256)), c_((CMP_STRIDE, 128, 256)),
                     c_((A_KV * R, nblk)), c_((A_KV * R, SL)), c_((A_KV * R, WK)),
                     c_((nblk, ns_pad)), c_((ns_pad, SL)), c_((PAGE_SIZE, PAGE_SIZE))]),
        out_specs=pl.BlockSpec((1, tp, W_A), lambda b, p, pt: (b, 0, 0)),
        scratch_shapes=[pltpu.VMEM((2, CMP_STRIDE, nblk + 16, 128), F32),
                        pltpu.VMEM((2, n_pages + 1, 128, PAGE_SIZE), F32),
                        pltpu.VMEM((nblk, 128), F32), pltpu.VMEM((nblk, 128), F32),
                        pltpu.VMEM((nblk + 16, 128), F32),
                        pltpu.VMEM((tp, ns_pad), F32), pltpu.VMEM((A_KV * R, SL), F32)],
    )
    return pl.pallas_call(
        kern,
        grid_spec=grid_spec,
        out_shape=jax.ShapeDtypeStruct((B, tp, W_A), F32),
        compiler_params=_cparams(("parallel", "arbitrary")),
        name="nsa_sample",
    )(page_table, *([pool_cmp] * pg), *([pool_sel] * pg), qn3, rows6, u3, win_t, wk, wv,
      *(tb[k].reshape(A_KV * R, -1) for k in ('bcmp', 'bsel', 'bwin')), tb['cov'], tb['emat'], tb['perm'])


def _tri(L, lower_incl):
    r = lax.broadcasted_iota(jnp.int32, (L, L), 0)
    c = lax.broadcasted_iota(jnp.int32, (L, L), 1)
    return (r >= c) if lower_incl else (r > c)


def _mlstm_kernel(x_ref, small_ref, smallt_ref, brow_ref, bcol_ref, c0_ref, n0_ref, m0_ref,
                  h_ref, co_ref, no_ref, mo_ref, c_s, n_s, m_s, *, L, t_real):
    c = pl.program_id(1)
    nc = pl.num_programs(1)

    @pl.when(c == 0)
    def _():
        c_s[...] = c0_ref[0]
        n_s[...] = n0_ref[0]
        m_s[...] = m0_ref[0]

    sm = small_ref[0].astype(F32)
    smt = smallt_ref[0, 0]
    li_col = sm[:, SM_MIF:SM_MIF + 4] + brow_ref[0:1, 0:4]
    lf_col = _log_sigmoid(sm[:, SM_MIF + 4:SM_MIF + 8] + brow_ref[0:1, 4:8])
    li_row = smt[0:4, :] + bcol_ref[0:4, :]
    lf_row = _log_sigmoid(smt[4:8, :] + bcol_ref[4:8, :])
    if t_real % L:
        tcol = lax.broadcasted_iota(jnp.int32, (L, 4), 0) + c * L
        trow = lax.broadcasted_iota(jnp.int32, (4, L), 1) + c * L
        li_col = jnp.where(tcol < t_real, li_col, NEG)
        lf_col = jnp.where(tcol < t_real, lf_col, 0.0)
        li_row = jnp.where(trow < t_real, li_row, NEG)
        lf_row = jnp.where(trow < t_real, lf_row, 0.0)
    low = _tri(L, True)
    b_col = jnp.dot(low.astype(F32), lf_col, precision=HI, preferred_element_type=F32)
    b_row = lax.dot_general(lf_row, low.astype(F32), NT, precision=HI, preferred_element_type=F32)

    hd = range(M_HEADS)
    q = [x_ref[0, :, h * M_HD:(h + 1) * M_HD].astype(F32) for h in hd]
    k = [x_ref[0, :, W_M + h * M_HD:W_M + (h + 1) * M_HD].astype(F32) * (M_HD ** -0.5) for h in hd]
    v = [x_ref[0, :, 2 * W_M + h * M_HD:2 * W_M + (h + 1) * M_HD].astype(F32) for h in hd]
    bc = [b_col[:, h:h + 1] for h in hd]
    Dm = [jnp.where(low, bc[h] - b_row[h:h + 1, :] + li_row[h:h + 1, :], NEG) for h in hd]
    m_prev = [m_s[h][:, 0:1] for h in hd]
    a = [bc[h] + m_prev[h] for h in hd]
    mt = [jnp.maximum(a[h], jnp.max(Dm[h], axis=-1, keepdims=True)) for h in hd]
    qk = [lax.dot_general(q[h], k[h], NT, preferred_element_type=F32) for h in hd]
    S = [qk[h] * jnp.exp(Dm[h] - mt[h]) for h in hd]
    inter = [jnp.exp(a[h] - mt[h]) for h in hd]
    C = [c_s[h] for h in hd]
    n = [n_s[h] for h in hd]
    qC = [jnp.dot(q[h], C[h], preferred_element_type=F32) for h in hd]
    Sv = [jnp.dot(S[h], v[h], preferred_element_type=F32) for h in hd]
    den = [inter[h] * jnp.sum(q[h] * n[h], axis=-1, keepdims=True) + jnp.sum(S[h], axis=-1, keepdims=True)
           for h in hd]
    for h in hd:
        h_ref[0, :, h * M_HD:(h + 1) * M_HD] = ((inter[h] * qC[h] + Sv[h])
                                                / jnp.maximum(jnp.abs(den[h]), jnp.exp(-mt[h])))
    bL = [bc[h][L - 1:L, :] for h in hd]
    wlog = [bL[h] - bc[h] + li_col[:, h:h + 1] for h in hd]
    m_new = [jnp.maximum(bL[h] + m_prev[h], jnp.max(wlog[h], axis=0, keepdims=True)) for h in hd]
    dec = [jnp.exp(bL[h] + m_prev[h] - m_new[h]) for h in hd]
    kw = [k[h] * jnp.exp(wlog[h] - m_new[h]) for h in hd]
    kv = [lax.dot_general(kw[h], v[h], TN, preferred_element_type=F32) for h in hd]
    for h in hd:
        c_s[h] = dec[h] * C[h] + kv[h]
        n_s[h] = dec[h] * n[h] + jnp.sum(kw[h], axis=0, keepdims=True)
        m_s[h] = jnp.broadcast_to(m_new[h], (1, LANES))

    @pl.when(c == nc - 1)
    def _():
        co_ref[0] = c_s[...]
        no_ref[0] = n_s[...]
        mo_ref[0] = m_s[...]


def _mlstm(u3, smallt, m_bi, m_bf, C0, n0, m0, L, t_real):
    B, T, _ = u3.shape
    nc = T // L
    brow = jnp.concatenate([m_bi, m_bf]).reshape(1, 8)
    bcol = jnp.concatenate([m_bi, m_bf]).reshape(8, 1)
    n0 = n0.reshape(B, M_HEADS, 1, M_HD)
    m0 = jnp.broadcast_to(m0[:, :, None, None], (B, M_HEADS, 1, LANES))
    kern = functools.partial(_mlstm_kernel, L=L, t_real=t_real)
    st = lambda shape: pl.BlockSpec(shape, lambda b, c: (b,) + (0,) * (len(shape) - 1))
    h, Cn, nn, mn = pl.pallas_call(
        kern,
        grid=(B, nc),
        in_specs=[pl.BlockSpec((1, L, 3 * W_M), lambda b, c: (b, c, OFF['m_qkv'] // (3 * W_M))),
                  pl.BlockSpec((1, L, LANES), lambda b, c: (b, c, OFF['small'] // LANES)),
                  pl.BlockSpec((1, 1, 16, L), lambda b, c: (b, c, 0, 0)),
                  pl.BlockSpec((1, 8), lambda b, c: (0, 0)), pl.BlockSpec((8, 1), lambda b, c: (0, 0)),
                  st((1, M_HEADS, M_HD, M_HD)), st((1, M_HEADS, 1, M_HD)), st((1, M_HEADS, 1, LANES))],
        out_specs=[pl.BlockSpec((1, L, W_M), lambda b, c: (b, c, 0)),
                   st((1, M_HEADS, M_HD, M_HD)), st((1, M_HEADS, 1, M_HD)), st((1, M_HEADS, 1, LANES))],
        out_shape=[jax.ShapeDtypeStruct((B, T, W_M), F32),
                   jax.ShapeDtypeStruct((B, M_HEADS, M_HD, M_HD), F32),
                   jax.ShapeDtypeStruct((B, M_HEADS, 1, M_HD), F32),
                   jax.ShapeDtypeStruct((B, M_HEADS, 1, LANES), F32)],
        scratch_shapes=[pltpu.VMEM((M_HEADS, M_HD, M_HD), F32), pltpu.VMEM((M_HEADS, 1, M_HD), F32),
                        pltpu.VMEM((M_HEADS, 1, LANES), F32)],
        compiler_params=_cparams(("parallel", "arbitrary")),
        name="mlstm",
    )(u3, u3, smallt, brow, bcol, C0, n0, m0)
    return h, Cn, nn.reshape(B, M_HEADS, M_HD), mn[:, :, 0, 0]


def _gdn_a_kernel(x_ref, small_ref, smallt_ref, cw_ref, prow_ref, pcol_ref, buf_ref,
                  u_ref, w_ref, qe_ref, kd_ref, at_ref, eg_ref, xbuf, *, L, CB, t_real):
    c = pl.program_id(1)
    PRE = 8
    R = CB * L

    @pl.when(c == 0)
    def _():
        xbuf[0:PRE, :] = buf_ref[0]

    xbuf[PRE:PRE + R, :] = x_ref[0].astype(F32)
    conv = jnp.zeros((R, 3 * W_G), F32)
    for j in range(CONV_W):
        conv = conv + xbuf[pl.ds(PRE - (CONV_W - 1) + j, R), :] * cw_ref[j:j + 1, :]
    xbuf[0:PRE, :] = xbuf[R:R + PRE, :]
    conv = _silu(conv)

    sm = small_ref[0].astype(F32)
    g_all = -jnp.exp(prow_ref[0:1, 0:4]) * _softplus(sm[:, SM_GAB:SM_GAB + 4] + prow_ref[1:2, 0:4])
    beta_all = jax.nn.sigmoid(sm[:, SM_GAB + 4:SM_GAB + 8])
    lowf = _tri(L, True).astype(F32)

    H = G_HEADS
    lane_h = lax.broadcasted_iota(jnp.int32, (L, H * L), 1) // L
    row_p = lax.broadcasted_iota(jnp.int32, (L, H * L), 0)
    col_p = lax.broadcasted_iota(jnp.int32, (L, H * L), 1) % L
    blockmask = (lax.broadcasted_iota(jnp.int32, (H * L, H * L), 0) // L
                 == lax.broadcasted_iota(jnp.int32, (H * L, H * L), 1) // L)

    def pack_diag(full):
        out = full[0:L]
        for h in range(1, H):
            out = jnp.where(lane_h == h, full[h * L:(h + 1) * L], out)
        return out

    def block_diag(m, parts):
        if L % 16:
            return _split(jnp.where(blockmask, jnp.concatenate([m] * H, axis=0), 0.0))
        return tuple(jnp.where(blockmask, jnp.concatenate([p] * H, axis=0), jnp.zeros((), BF16))
                     for p in parts)

    a_list, rhs_list = [], []
    for cb in range(CB):
        rs = slice(cb * L, (cb + 1) * L)
        smt = smallt_ref[0, cb]
        g_col = g_all[rs]
        beta_col = beta_all[rs]
        g_row = -jnp.exp(pcol_ref[0:4, 0:1]) * _softplus(smt[8:12, :] + pcol_ref[0:4, 1:2])
        if t_real % L:
            t0 = (c * CB + cb) * L
            tcol = lax.broadcasted_iota(jnp.int32, (L, 4), 0) + t0
            trow = lax.broadcasted_iota(jnp.int32, (4, L), 1) + t0
            g_col = jnp.where(tcol < t_real, g_col, 0.0)
            beta_col = jnp.where(tcol < t_real, beta_col, 0.0)
            g_row = jnp.where(trow < t_real, g_row, 0.0)
        G_col = jnp.dot(lowf, g_col, precision=HI, preferred_element_type=F32)
        G_row = lax.dot_general(g_row, lowf, NT, precision=HI, preferred_element_type=F32)

        qs, ks, kbs, rhss = [], [], [], []
        for h in range(G_HEADS):
            hs = slice(h * G_HD, (h + 1) * G_HD)
            cq = conv[rs, h * G_HD:(h + 1) * G_HD]
            ck = conv[rs, W_G + h * G_HD:W_G + (h + 1) * G_HD]
            v = conv[rs, 2 * W_G + h * G_HD:2 * W_G + (h + 1) * G_HD]
            q = cq * lax.rsqrt(jnp.sum(cq * cq, axis=-1, keepdims=True) + EPS) * (G_HD ** -0.5)
            k = ck * lax.rsqrt(jnp.sum(ck * ck, axis=-1, keepdims=True) + EPS)
            Gc = G_col[:, h:h + 1]
            bcol = beta_col[:, h:h + 1]
            kb = k * bcol
            eG = jnp.exp(Gc)
            GL = Gc[L - 1:L, :]
            qs.append(q)
            ks.append(k)
            kbs.append(kb)
            rhss.append(jnp.concatenate([v * bcol, kb * eG], axis=1))
            qe_ref[0, rs, hs] = q * eG
            kd_ref[0, rs, hs] = k * jnp.exp(GL - Gc)
            eg_ref[0, cb, h:h + 1, :] = jnp.broadcast_to(jnp.exp(GL), (1, LANES))
        eg_ref[0, cb, G_HEADS:8, :] = jnp.zeros((8 - G_HEADS, LANES), F32)

        Gc_p = jnp.concatenate([jnp.broadcast_to(G_col[:, h:h + 1], (L, L)) for h in range(H)], axis=1)
        Gr_p = jnp.concatenate([jnp.broadcast_to(G_row[h:h + 1, :], (L, L)) for h in range(H)], axis=1)
        low_p = row_p >= col_p
        dmask = jnp.where(low_p, jnp.exp(jnp.where(low_p, Gc_p - Gr_p, 0.0)), 0.0)
        k_st = _split(jnp.concatenate(ks, axis=0))
        A = pack_diag(_dot3(_split(jnp.concatenate(kbs, axis=0)), k_st, NT)) * jnp.where(row_p > col_p, dmask, 0.0)
        attn = pack_diag(lax.dot_general(jnp.concatenate(qs, axis=0).astype(BF16), k_st[0], NT,
                                         preferred_element_type=F32)) * dmask
        at_ref[0, rs, :] = attn
        a_list.append(A)
        rhs_list.append(_split(jnp.concatenate(rhss, axis=0)))

    chunks = range(CB)
    X = [jnp.where(row_p == col_p, 1.0, 0.0) - a_list[cb] for cb in chunks]
    As = [_split(a_list[cb]) for cb in chunks]
    Pw = [_dot3(As[cb], block_diag(a_list[cb], As[cb])) for cb in chunks]
    span = 2
    while span < L:
        Ps = [_split(Pw[cb]) for cb in chunks]
        Pbd = [block_diag(Pw[cb], Ps[cb]) for cb in chunks]
        X = [X[cb] + _dot3(_split(X[cb]), Pbd[cb]) for cb in chunks]
        span *= 2
        if span < L:
            Pw = [_dot3(Ps[cb], Pbd[cb]) for cb in chunks]
    sol = [_dot3(block_diag(X[cb], _split(X[cb])), rhs_list[cb]) for cb in chunks]
    for cb in chunks:
        for h in range(H):
            rs = slice(cb * L, (cb + 1) * L)
            hs = slice(h * G_HD, (h + 1) * G_HD)
            u_ref[0, rs, hs] = sol[cb][h * L:(h + 1) * L, 0:G_HD]
            w_ref[0, rs, hs] = sol[cb][h * L:(h + 1) * L, G_HD:2 * G_HD]


def _gdn_b_kernel(u_ref, w_ref, qe_ref, kd_ref, at_ref, eg_ref, s0_ref, o_ref, so_ref, s_s, *, L, BB):
    c = pl.program_id(1)
    nc = pl.num_programs(1)

    @pl.when(c == 0)
    def _():
        s_s[...] = s0_ref[...]

    ch = [(b, h, slice(h * G_HD, (h + 1) * G_HD)) for b in range(BB) for h in range(G_HEADS)]
    S = [s_s[b, h] for b, h, _ in ch]
    r = [jnp.dot(jnp.concatenate([w_ref[b, :, hs], qe_ref[b, :, hs]], axis=0), S[i], preferred_element_type=F32)
         for i, (b, h, hs) in enumerate(ch)]
    v_new = [u_ref[b, :, hs] - r[i][0:L] for i, (b, h, hs) in enumerate(ch)]
    av = [jnp.dot(at_ref[b, :, h * L:(h + 1) * L], v_new[i], preferred_element_type=F32)
          for i, (b, h, hs) in enumerate(ch)]
    kdv = [lax.dot_general(kd_ref[b, :, hs], v_new[i], TN, preferred_element_type=F32)
           for i, (b, h, hs) in enumerate(ch)]
    for i, (b, h, hs) in enumerate(ch):
        o_ref[b, :, hs] = r[i][L:2 * L] + av[i]
        s_s[b, h] = eg_ref[b, 0, h:h + 1, :] * S[i] + kdv[i]

    @pl.when(c == nc - 1)
    def _():
        so_ref[...] = s_s[...]


def _gdn(u3, smallt, g_conv, g_A_log, g_dt_bias, buf, S0, L, t_real):
    B, T, _ = u3.shape
    nc = T // L
    CB = 4 if nc % 4 == 0 else 1
    BB = 8 if B % 8 == 0 else B
    prow = jnp.stack([g_A_log, g_dt_bias])
    pcol = jnp.stack([g_A_log, g_dt_bias], axis=1)
    buf8 = jnp.concatenate([jnp.zeros((B, 8 - (CONV_W - 1), 3 * W_G), F32), buf], axis=1)
    st = lambda shape: pl.BlockSpec(shape, lambda b, c: (b,) + (0,) * (len(shape) - 1))
    cst = lambda shape: pl.BlockSpec(shape, lambda b, c: (0,) * len(shape))
    row = lambda nb, r, w: pl.BlockSpec((nb, r, w), lambda b, c: (b, c, 0))
    tok = lambda w: jax.ShapeDtypeStruct((B, T, w), F32)
    u, w, qe, kd, at, eg = pl.pallas_call(
        functools.partial(_gdn_a_kernel, L=L, CB=CB, t_real=t_real),
        grid=(B, nc // CB),
        in_specs=[pl.BlockSpec((1, CB * L, 3 * W_G), lambda b, c: (b, c, OFF['g_qkv'] // (3 * W_G))),
                  pl.BlockSpec((1, CB * L, LANES), lambda b, c: (b, c, OFF['small'] // LANES)),
                  pl.BlockSpec((1, CB, 16, L), lambda b, c: (b, c, 0, 0)),
                  cst((CONV_W, 3 * W_G)), cst((2, 4)), cst((4, 2)), st((1, 8, 3 * W_G))],
        out_specs=[row(1, CB * L, W_G)] * 4 + [row(1, CB * L, G_HEADS * L),
                                               pl.BlockSpec((1, CB, 8, LANES), lambda b, c: (b, c, 0, 0))],
        out_shape=[tok(W_G)] * 4 + [tok(G_HEADS * L), jax.ShapeDtypeStruct((B, nc, 8, LANES), F32)],
        scratch_shapes=[pltpu.VMEM((CB * L + 8, 3 * W_G), F32)],
        compiler_params=_cparams(("parallel", "arbitrary")),
        name="gdn_a",
    )(u3, u3, smallt, g_conv, prow, pcol, buf8)
    sblk = pl.BlockSpec((BB, G_HEADS, G_HD, G_HD), lambda b, c: (b, 0, 0, 0))
    return pl.pallas_call(
        functools.partial(_gdn_b_kernel, L=L, BB=BB),
        grid=(B // BB, nc),
        in_specs=[row(BB, L, W_G)] * 4 + [row(BB, L, G_HEADS * L),
                                          pl.BlockSpec((BB, 1, 8, LANES), lambda b, c: (b, c, 0, 0)), sblk],
        out_specs=[row(BB, L, W_G), sblk],
        out_shape=[tok(W_G), jax.ShapeDtypeStruct((B, G_HEADS, G_HD, G_HD), F32)],
        scratch_shapes=[pltpu.VMEM((BB, G_HEADS, G_HD, G_HD), F32)],
        compiler_params=_cparams(("parallel", "arbitrary")),
        name="gdn_b",
    )(u, w, qe, kd, at, eg, S0)


def _head_rmsnorm(x, gain_ref, nheads, hd):
    outs = []
    for h in range(nheads):
        xh = x[:, h * hd:(h + 1) * hd]
        ms = jnp.mean(xh * xh, axis=-1, keepdims=True)
        outs.append(xh * lax.rsqrt(ms + EPS) * gain_ref[...])
    return jnp.concatenate(outs, axis=1)


def _merge_kernel(x_ref, oa_ref, hm_ref, og_ref, az_ref, mo_ref, mz_ref, gz_ref, mg_ref,
                  wb_ref, wo_ref, mhn_ref, ghn_ref, y_ref):
    oa = oa_ref[...] * _silu(az_ref[...].astype(F32))
    om = (_head_rmsnorm(hm_ref[...], mhn_ref, M_HEADS, M_HD) * jax.nn.sigmoid(mo_ref[...].astype(F32))
          * _silu(mz_ref[...].astype(F32)))
    og = _head_rmsnorm(og_ref[...], ghn_ref, G_HEADS, G_HD) * _silu(gz_ref[...].astype(F32))
    y = jnp.zeros(y_ref.shape, F32)
    for i, br in enumerate((oa, om, og)):
        proj = jnp.dot(br.astype(BF16), wb_ref[i], preferred_element_type=F32)
        y = y + jax.nn.sigmoid(mg_ref[:, i * D_MODEL:(i + 1) * D_MODEL].astype(F32)) * proj
    y_ref[...] = x_ref[...] + jnp.dot(y.astype(BF16), wo_ref[...], preferred_element_type=F32)


def _merge_out(x2, o_a, h_m, o_g, u2, w_branch, w_out, m_hn, g_hn):
    n = x2.shape[0]
    tm = min(512, n)
    row = lambda w, off: pl.BlockSpec((tm, w), lambda i: (i, off // w))
    cst = lambda shape: pl.BlockSpec(shape, lambda i: (0,) * len(shape))
    return pl.pallas_call(
        _merge_kernel,
        grid=(n // tm,),
        in_specs=[row(D_MODEL, 0), row(W_A, 0), row(W_M, 0), row(W_G, 0),
                  row(W_A, OFF['a_z']), row(W_M, OFF['m_o']), row(W_M, OFF['m_z']), row(W_G, OFF['g_z']),
                  row(N_BRANCH * D_MODEL, OFF['merge']),
                  cst((N_BRANCH, W_A, D_MODEL)), cst((D_MODEL, D_MODEL)), cst((1, M_HD)), cst((1, G_HD))],
        out_specs=row(D_MODEL, 0),
        out_shape=jax.ShapeDtypeStruct((n, D_MODEL), F32),
        compiler_params=_cparams(("parallel",)),
        name="merge_out",
    )(x2, o_a, h_m, o_g, u2, u2, u2, u2, u2, w_branch.astype(BF16), w_out.astype(BF16),
      m_hn.reshape(1, M_HD), g_hn.reshape(1, G_HD))


def _small_t(u3, L):
    B, T, _ = u3.shape
    s = u3[:, :, OFF['small'] + SM_MIF:OFF['small'] + SM_MIF + 16].astype(F32)
    return s.reshape(B, T // L, L, 16).transpose(0, 1, 3, 2)


def _layer(x, lw, tb, past, q_off):
    B, T, _ = x.shape
    if past is None:
        tp, L = T, math.gcd(T, M_CHUNK)
    else:
        tp = -(-T // 8) * 8
        L = tp
        x = jnp.pad(x, ((0, 0), (0, tp - T), (0, 0)))
    x2 = x.reshape(B * tp, D_MODEL)
    u2 = _proj_in(x2, lw['norm_g'], lw['w_perm'], BF16 if past is None else F32)
    u3 = u2.reshape(B, tp, N_PERM)
    if past is None:
        C0 = jnp.zeros((B, M_HEADS, M_HD, M_HD), F32)
        n0 = jnp.zeros((B, M_HEADS, M_HD), F32)
        m0 = jnp.zeros((B, M_HEADS), F32)
        S0 = jnp.zeros((B, G_HEADS, G_HD, G_HD), F32)
        buf = jnp.zeros((B, CONV_W - 1, 3 * W_G), F32)
    else:
        C0, n0, m0, S0, buf = past['mC'], past['mn'], past['mm'], past['gS'], past['gconv']
    qn, rows6 = _prep_a(u2, lw['a_qn'], lw['a_kn'])
    qn3 = qn.reshape(B, tp, W_A)
    if past is None:
        o_a = _nsa_prompt(qn3, rows6, u3, lw['wk'], lw['wv'], tb)
    else:
        o_a = _nsa_sample(qn3, rows6, u3, past['cmp'], past['sel'], past['win_t'], past['layer'], past['page_table'],
                          lw['wk'], lw['wv'], tb, q_off, T)
    smallt = _small_t(u3, L)
    h_m, Cn, nn, mn = _mlstm(u3, smallt, lw['m_bi'], lw['m_bf'], C0, n0, m0, L, T)
    o_g, Sn = _gdn(u3, smallt, lw['g_conv'], lw['g_A_log'], lw['g_dt_bias'], buf, S0, L, T)
    y2 = _merge_out(x2, o_a.reshape(B * tp, W_A), h_m.reshape(B * tp, W_M), o_g.reshape(B * tp, W_G), u2,
                    lw['w_branch'], lw['w_out'], lw['m_hn'], lw['g_hn'])
    rows = rows6.reshape(3, 2, B, tp, A_KV, A_HD)[:, :, :, :T]
    new_cmp, new_sel, new_win = (jnp.moveaxis(rows[b], 0, 2) for b in range(3))
    g_qkv = u3[:, max(T - (CONV_W - 1), 0):T, OFF['g_qkv']:OFF['g_qkv'] + 3 * W_G].astype(F32)
    if past is None:
        win = new_win
    else:
        win = jnp.concatenate([past['win'], new_win], axis=1)
    full = jnp.concatenate([buf, g_qkv], axis=1)
    state = dict(cmp=new_cmp, sel=new_sel, win=win[:, -min(WINDOW, win.shape[1]):],
                 mC=Cn, mn=nn, mm=mn, gS=Sn, gconv=full[:, -(CONV_W - 1):])
    return y2.reshape(B, tp, D_MODEL)[:, :T], state


def kernel(x_prompt, x_sample, cache_cmp_kv, cache_sel_kv, cache_win_kv, state_mlstm_C, state_mlstm_n,
           state_mlstm_m, state_gdn_S, state_gdn_conv, page_table, norm_g, w_in, a_qn, a_kn, a_cmp_wk,
           a_cmp_wv, rel_bias, m_bi, m_bf, m_hn, g_conv, g_A_log, g_dt_bias, g_hn, w_branch, w_out):
    names = ('cmp', 'sel', 'win', 'mC', 'mn', 'mm', 'gS', 'gconv')
    st_p = {k: [] for k in names}
    st_s = {k: [] for k in names}
    past_len = page_table.shape[1] * PAGE_SIZE
    n_pool = cache_cmp_kv.shape[1]
    pos_minor = lambda c: jnp.swapaxes(c.reshape(c.shape[0], c.shape[1], c.shape[2], 2 * A_KV * A_HD), 2, 3)
    pool_cmp, pool_sel, win_t = pos_minor(cache_cmp_kv), pos_minor(cache_sel_kv), pos_minor(cache_win_kv)
    w_t = jnp.swapaxes(w_in, 1, 2)
    db, dt = x_sample.shape[0], x_sample.shape[1]
    tb_p = _prompt_tables(rel_bias, x_prompt.shape[1])
    tb_s = _sample_tables(rel_bias, past_len, -(-dt // 8) * 8, dt)
    y_p, y_s = x_prompt, x_sample
    for l in range(DEPTH):
        lw = dict(norm_g=norm_g[l], w_perm=_permute_w_in(w_t, l), a_qn=a_qn[l], a_kn=a_kn[l],
                  wk=_cmp_weights(a_cmp_wk[l]), wv=_cmp_weights(a_cmp_wv[l]),
                  m_bi=m_bi[l], m_bf=m_bf[l], m_hn=m_hn[l], g_conv=g_conv[l], g_A_log=g_A_log[l],
                  g_dt_bias=g_dt_bias[l], g_hn=g_hn[l], w_branch=w_branch[l], w_out=w_out[l])
        y_p, new_p = _layer(y_p, lw, tb_p, None, 0)
        past = dict(cmp=pool_cmp, sel=pool_sel, win_t=win_t, layer=l, win=cache_win_kv[l], page_table=page_table,
                    mC=state_mlstm_C[l], mn=state_mlstm_n[l], mm=state_mlstm_m[l],
                    gS=state_gdn_S[l], gconv=state_gdn_conv[l])
        y_s, new_s = _layer(y_s, lw, tb_s, past, past_len)
        for k in names:
            st_p[k].append(new_p[k])
            st_s[k].append(new_s[k])
    P = {k: jnp.stack(v) for k, v in st_p.items()}
    S = {k: jnp.stack(v) for k, v in st_s.items()}
    return (y_p, y_s, P['cmp'], S['cmp'], P['sel'], S['sel'], P['win'], S['win'],
            P['mC'], S['mC'], P['mn'], S['mn'], P['mm'], S['mm'], P['gS'], S['gS'], P['gconv'], S['gconv'])
```

```python
import functools
import math

import jax
import jax.numpy as jnp
from jax import lax
from jax.experimental import pallas as pl
from jax.experimental.pallas import tpu as pltpu

D_MODEL = 1024
DEPTH = 2
PAGE_SIZE = 128
A_HEADS = 8
A_KV = 2
A_REP = A_HEADS // A_KV
A_HD = 64
CMP_BLOCK = 32
CMP_STRIDE = 16
SEL_BLOCK = 64
TOP_N = 16
WINDOW = 512
Q_BLOCK = 128
N_BUCKETS = 32
MAX_DIST = 2048
M_HEADS = 4
M_HD = 128
M_CHUNK = 64
G_HEADS = 4
G_HD = 128
G_CHUNK = 64
CONV_W = 4
W_A = A_HEADS * A_HD
W_M = M_HEADS * M_HD
W_G = G_HEADS * G_HD
N_BRANCH = 3
EPS = 1e-6
NEG = -1e30
TINY = 1e-30
LOG2E = math.log2(math.e)

F32 = jnp.float32
BF16 = jnp.bfloat16
HI = lax.Precision.HIGHEST
NN = (((1,), (0,)), ((), ()))
NT = (((1,), (1,)), ((), ()))
TN = (((0,), (0,)), ((), ()))

LANES = 128
VMEM_LIMIT = 56 * 1024 * 1024

IN_ORDER = ('a_q', 'a_kv', 'a_gate', 'a_z', 'm_qkv', 'm_if', 'm_o', 'm_z', 'g_qkv', 'g_ab', 'g_z', 'merge')
IN_WIDTH = dict(a_q=W_A, a_kv=3 * 2 * A_KV * A_HD, a_gate=A_HEADS * 3, a_z=W_A, m_qkv=3 * W_M, m_if=2 * M_HEADS,
                m_o=W_M, m_z=W_M, g_qkv=3 * W_G, g_ab=2 * G_HEADS, g_z=W_G, merge=N_BRANCH * D_MODEL)
OFF = dict(merge=0, m_qkv=3072, g_qkv=4608, a_q=6144, a_z=6656, m_o=7168, m_z=7680, g_z=8192, a_kv=8704,
           small=9472)
N_PERM = 9600
SM_GATE, SM_MIF, SM_GAB = 0, 24, 32


def _cparams(sem):
    return pltpu.CompilerParams(dimension_semantics=sem, vmem_limit_bytes=VMEM_LIMIT)


def _silu(x):
    return x * jax.nn.sigmoid(x)


def _log_sigmoid(x):
    return jnp.minimum(x, 0.0) - jnp.log(1.0 + jnp.exp(-jnp.abs(x)))


def _softplus(x):
    return jnp.maximum(x, 0.0) + jnp.log(1.0 + jnp.exp(-jnp.abs(x)))


def _split(a):
    hi = a.astype(BF16)
    return hi, (a - hi.astype(F32)).astype(BF16)


def _dot3(a, b, dims=NN):
    mm = lambda x, y: lax.dot_general(x, y, dims, preferred_element_type=F32)
    return mm(a[0], b[0]) + mm(a[0], b[1]) + mm(a[1], b[0])


def _src_offsets():
    offs, off = {}, 0
    for name in IN_ORDER:
        offs[name] = off
        off += IN_WIDTH[name]
    return offs, off


def _permute_kernel(w_ref, o_ref):
    src, _ = _src_offsets()
    small = []
    for name in IN_ORDER:
        w = IN_WIDTH[name]
        rows = w_ref[src[name]:src[name] + w, :]
        if w % LANES:
            small.append(rows)
        else:
            o_ref[:, OFF[name]:OFF[name] + w] = rows.T.astype(BF16)
    used = sum(r.shape[0] for r in small)
    small.append(jnp.zeros((LANES - used, w_ref.shape[1]), F32))
    o_ref[:, OFF['small']:N_PERM] = jnp.concatenate(small, axis=0).T.astype(BF16)


def _permute_w_in(w_t, layer):
    _, n_in, d = w_t.shape
    tr = 128
    return pl.pallas_call(
        _permute_kernel,
        grid=(d // tr,),
        in_specs=[pl.BlockSpec((None, n_in, tr), lambda i: (layer, 0, i))],
        out_specs=pl.BlockSpec((tr, N_PERM), lambda i: (i, 0)),
        out_shape=jax.ShapeDtypeStruct((d, N_PERM), BF16),
        compiler_params=_cparams(("parallel",)),
        name="permute_w",
    )(w_t)


def _rel_bucket(dist):
    n = jnp.maximum(dist, 0)
    exact = N_BUCKETS // 2
    nf = jnp.maximum(n, exact).astype(F32)
    large = exact + (jnp.log(nf / exact) / math.log(MAX_DIST / exact) * (N_BUCKETS - exact)).astype(jnp.int32)
    return jnp.where(n < exact, n, jnp.minimum(large, N_BUCKETS - 1))


def _bias_kernel(thr_ref, tab_ref, d_ref, o_ref):
    n = jnp.maximum(d_ref[0], 0)
    for h in range(A_HEADS):
        acc = jnp.full(n.shape, tab_ref[h], F32)
        for k in range(1, N_BUCKETS):
            acc = jnp.where(n >= thr_ref[k], tab_ref[k * A_HEADS + h], acc)
        o_ref[0, h // A_REP, h % A_REP] = acc


def _bias_rows(rel_bias, dist, split_rows=False):
    N, Q, K = dist.shape
    nmax = 2 * MAX_DIST
    thr = jnp.sum(_rel_bucket(jnp.arange(nmax))[None, :] < jnp.arange(N_BUCKETS)[:, None], axis=1).astype(jnp.int32)
    smem = pl.BlockSpec(memory_space=pltpu.SMEM)
    out = pl.pallas_call(
        _bias_kernel,
        grid=(N,),
        in_specs=[smem, smem, pl.BlockSpec((1, Q, K), lambda i: (i, 0, 0))],
        out_specs=pl.BlockSpec((1, A_KV, A_REP, Q, K), lambda i: (i, 0, 0, 0, 0)),
        out_shape=jax.ShapeDtypeStruct((N, A_KV, A_REP, Q, K), F32),
        compiler_params=_cparams(("parallel",)),
        name="bias_rows",
    )(thr, rel_bias.astype(F32).reshape(N_BUCKETS * A_HEADS), dist.astype(jnp.int32))
    return out if split_rows else out.reshape(N, A_KV, A_REP * Q, K)


def _cmp_weights(w):
    wr = w.reshape(A_KV, 2, CMP_STRIDE, A_HD, A_HD)
    eye = jnp.eye(A_KV, dtype=w.dtype)
    full = jnp.einsum('gmjde,gh->jgdmhe', wr, eye)
    return full.reshape(CMP_STRIDE, A_KV * A_HD, 2 * A_KV * A_HD).astype(BF16)


def _cover_t(ns_pad, nch):
    s0 = jnp.arange(ns_pad)[:, None] * SEL_BLOCK
    c0 = jnp.arange(nch)[None, :] * CMP_STRIDE
    return ((c0 < s0 + SEL_BLOCK) & (s0 <= c0 + CMP_BLOCK - 1)).astype(F32)


def _expand_mat(ns_pad, nk):
    return (jnp.arange(nk)[None, :] // SEL_BLOCK == jnp.arange(ns_pad)[:, None]).astype(BF16)


def _gate_expand():
    col = jnp.arange(N_BRANCH * W_A)
    src = 3 * ((col % W_A) // A_HD) + col // W_A
    return (jnp.arange(LANES)[:, None] == src[None, :]).astype(BF16)


def _phase_perm():
    r = jnp.arange(PAGE_SIZE)
    per_page = PAGE_SIZE // CMP_STRIDE
    return (jnp.arange(PAGE_SIZE)[None, :] == ((r % per_page) * CMP_STRIDE + r // per_page)[:, None]).astype(BF16)


def _prompt_tables(rel_bias, T):
    QB = Q_BLOCK
    nqb = T // QB
    nch = T // CMP_STRIDE
    ns = T // SEL_BLOCK
    ns_pad = -(-ns // 8) * 8
    i_ = jnp.arange(QB)
    t = (jnp.arange(nqb) * QB)[:, None, None] + i_[None, :, None]
    cend = (jnp.arange(nch) * CMP_STRIDE + CMP_BLOCK - 1)[None, None, :]
    d = (jnp.arange(nqb) * QB)[:, None, None] + i_[None, :, None] - i_[None, None, :]
    nw = WINDOW // QB + 1
    dt = jnp.swapaxes(d, 1, 2)
    tsel = _bias_rows(rel_bias * LOG2E, dt, split_rows=True)
    return dict(
        bcmp=_bias_rows(rel_bias, t - cend),
        tselt=tsel.transpose(0, 1, 3, 2, 4).reshape(nqb, A_KV, QB, A_REP * QB),
        covt=_cover_t(ns_pad, nch),
        wmt=jnp.where((dt[:nw] >= 0) & (dt[:nw] < WINDOW), 0.0, NEG),
        cmt=jnp.where(dt[:2] >= 0, 0.0, NEG),
        gx=_gate_expand())


def _sample_tables(rel_bias, past, tp, t_real):
    SL = past + LANES
    nblk = past // CMP_STRIDE
    ns = -(-(past + t_real) // SEL_BLOCK)
    ns_pad = -(-ns // LANES) * LANES
    WK = WINDOW + LANES
    i_ = jnp.arange(tp)
    t = past + i_
    rows = lambda dist: _bias_rows(rel_bias, dist[None])[0]
    return dict(
        bcmp=rows(t[:, None] - (jnp.arange(nblk) * CMP_STRIDE + CMP_BLOCK - 1)[None, :]),
        bsel=rows(t[:, None] - jnp.arange(SL)[None, :]),
        bwin=rows(i_[:, None] + WINDOW - jnp.arange(WK)[None, :]),
        cov=_cover_t(ns_pad, nblk).T, emat=_expand_mat(ns_pad, SL), perm=_phase_perm())


def _proj_in_kernel(x_ref, g_ref, w_ref, o_ref, hn_ref):
    @pl.when(pl.program_id(1) == 0)
    def _():
        x = x_ref[...]
        ms = jnp.mean(x * x, axis=-1, keepdims=True)
        hn_ref[...] = (x * lax.rsqrt(ms + EPS) * g_ref[...]).astype(BF16)

    o_ref[...] = jnp.dot(hn_ref[...], w_ref[...], preferred_element_type=F32).astype(o_ref.dtype)


def _proj_in(x2, norm_g, w_perm, out_dtype):
    n = x2.shape[0]
    tm = min(1024, n)
    tn = 1920
    return pl.pallas_call(
        _proj_in_kernel,
        grid=(n // tm, N_PERM // tn),
        in_specs=[pl.BlockSpec((tm, D_MODEL), lambda i, j: (i, 0)),
                  pl.BlockSpec((1, D_MODEL), lambda i, j: (0, 0)),
                  pl.BlockSpec((D_MODEL, tn), lambda i, j: (0, j))],
        out_specs=pl.BlockSpec((tm, tn), lambda i, j: (i, j)),
        out_shape=jax.ShapeDtypeStruct((n, N_PERM), out_dtype),
        scratch_shapes=[pltpu.VMEM((tm, D_MODEL), BF16)],
        compiler_params=_cparams(("parallel", "arbitrary")),
        name="proj_in",
    )(x2, norm_g.reshape(1, D_MODEL), w_perm)


def _prep_a_kernel(q_ref, kv0_ref, kv1_ref, kv2_ref, bdq_ref, bdk_ref, qg_ref, kg_ref, qo_ref, ro_ref):
    q = q_ref[...].astype(F32)
    ms = jnp.dot(q * q, bdq_ref[...], precision=HI, preferred_element_type=F32) * (1.0 / A_HD)
    qo_ref[...] = q * lax.rsqrt(ms + EPS) * qg_ref[...] * (A_HD ** -0.5)
    for b, kv_ref in enumerate((kv0_ref, kv1_ref, kv2_ref)):
        k = kv_ref[:, 0:128].astype(F32)
        ms = jnp.dot(k * k, bdk_ref[...], precision=HI, preferred_element_type=F32) * (1.0 / A_HD)
        ro_ref[2 * b] = k * lax.rsqrt(ms + EPS) * kg_ref[b:b + 1, :]
        ro_ref[2 * b + 1] = kv_ref[:, 128:256].astype(F32)


def _prep_a(u2, a_qn, a_kn):
    n = u2.shape[0]
    tm = min(512, n)
    bd = lambda w: (jnp.arange(w)[:, None] // A_HD == jnp.arange(w)[None, :] // A_HD).astype(F32)
    qg = jnp.tile(a_qn, A_HEADS).reshape(1, W_A)
    kg = jnp.tile(a_kn, (1, A_KV))
    const = lambda shape: pl.BlockSpec(shape, lambda i: (0, 0))
    kvs = lambda b: pl.BlockSpec((tm, 256), lambda i: (i, OFF['a_kv'] // 256 + b))
    return pl.pallas_call(
        _prep_a_kernel,
        grid=(n // tm,),
        in_specs=[pl.BlockSpec((tm, W_A), lambda i: (i, OFF['a_q'] // W_A)), kvs(0), kvs(1), kvs(2),
                  const((W_A, W_A)), const((128, 128)), const((1, W_A)), const((3, 128))],
        out_specs=[pl.BlockSpec((tm, W_A), lambda i: (i, 0)), pl.BlockSpec((6, tm, 128), lambda i: (0, i, 0))],
        out_shape=[jax.ShapeDtypeStruct((n, W_A), F32), jax.ShapeDtypeStruct((6, n, 128), F32)],
        compiler_params=_cparams(("parallel",)),
        name="prep_a",
    )(u2, u2, u2, u2, bd(W_A), bd(128), qg, kg)


def _stack_heads(q, g):
    return jnp.concatenate([q[:, (A_REP * g + r) * A_HD:(A_REP * g + r + 1) * A_HD] for r in range(A_REP)], axis=0)


def _rep_rows(x):
    return jnp.concatenate([x] * A_REP, axis=0)


def _masked_softmax(s, valid):
    sm = jnp.where(valid, s, NEG)
    m = jnp.max(sm, axis=-1, keepdims=True)
    e = jnp.where(valid, jnp.exp(sm - m), 0.0)
    l = jnp.sum(e, axis=-1, keepdims=True)
    return e / jnp.maximum(l, TINY)


def _select_blocks(pc_sum, covt_ref, st_ref, t_row, ns, n_top):
    ns_pad = covt_ref.shape[0]
    imp = lax.dot_general(covt_ref[...], pc_sum, NT, precision=HI, preferred_element_type=F32)
    jj = lax.broadcasted_iota(jnp.int32, (ns_pad, LANES), 0)
    tt = jnp.broadcast_to(t_row, (ns_pad, LANES))
    cur = tt // SEL_BLOCK
    forced = (jj == 0) | (jj == cur) | (jj == cur - 1)
    future = jj * SEL_BLOCK > tt
    score = jnp.where(future, NEG, jnp.where(forced, -NEG, imp))
    score = jnp.where(jj < ns, score, -jnp.inf)
    st_ref[...] = score

    def beats(k, rank):
        row = jnp.broadcast_to(st_ref[pl.ds(k, 1), :], (ns_pad, LANES))
        b = (row > score) | ((row == score) & (k < jj))
        return rank + jnp.where(b, 1.0, 0.0)

    rank = jnp.zeros((ns_pad, LANES), F32)
    if ns <= 32:
        for k in range(ns):
            rank = beats(k, rank)
    else:
        rank = lax.fori_loop(0, ns, beats, rank)
    return jnp.where(rank < n_top, 1.0, 0.0)


def _pad_rows(x, rows):
    if x.shape[0] == rows:
        return x
    return jnp.concatenate([x, jnp.zeros((rows - x.shape[0], x.shape[1]), x.dtype)], axis=0)


def _select_rows(pc_sum, cov_ref, sel_ref, t0, ns, n_top, nq):
    Q = pc_sum.shape[0]
    ns_pad = cov_ref.shape[1]
    nk = -(-ns // 8) * 8
    imp = jnp.dot(pc_sum, cov_ref[...], precision=HI, preferred_element_type=F32)
    jj = lax.broadcasted_iota(jnp.int32, (Q, ns_pad), 1)
    tt = lax.broadcasted_iota(jnp.int32, (Q, ns_pad), 0) + t0
    cur = tt // SEL_BLOCK
    forced = (jj == 0) | (jj == cur) | (jj == cur - 1)
    future = jj * SEL_BLOCK > tt
    score = jnp.where(future, NEG, jnp.where(forced, -NEG, imp))
    score = jnp.where(jj < ns, score, -jnp.inf)
    score_col = _pad_rows(score, LANES).T
    kk = lax.broadcasted_iota(jnp.int32, (nk, ns_pad), 0)
    jl = lax.broadcasted_iota(jnp.int32, (nk, ns_pad), 1)
    sel_ref[...] = jnp.zeros(sel_ref.shape, F32)
    for i in range(nq):
        col = jnp.broadcast_to(score_col[0:nk, i:i + 1], (nk, ns_pad))
        row = jnp.broadcast_to(score[i:i + 1, :], (nk, ns_pad))
        beats = (col > row) | ((col == row) & (kk < jl))
        rank = jnp.sum(jnp.where(beats, 1.0, 0.0), axis=0, keepdims=True)
        sel_ref[i:i + 1, :] = jnp.where(rank < n_top, 1.0, 0.0)
    return sel_ref[...]


def _compress(src_ref, nrow, wk_ref, wv_ref, pbuf_ref, kc_ref, vc_ref, by_phase=False):
    acck = jnp.zeros((nrow, 256), F32)
    accv = jnp.zeros((nrow, 256), F32)
    for j in range(CMP_STRIDE):
        if by_phase:
            xk = src_ref[0, j, 0:nrow, :].astype(BF16)
            xv = src_ref[1, j, 0:nrow, :].astype(BF16)
        else:
            xk = src_ref[0, pl.ds(j, nrow, stride=CMP_STRIDE), :].astype(BF16)
            xv = src_ref[1, pl.ds(j, nrow, stride=CMP_STRIDE), :].astype(BF16)
        acck = acck + jnp.dot(xk, wk_ref[j], preferred_element_type=F32)
        accv = accv + jnp.dot(xv, wv_ref[j], preferred_element_type=F32)
    nout = kc_ref.shape[0]
    for acc, dst in ((acck, kc_ref), (accv, vc_ref)):
        pbuf_ref[0:nrow, :] = acc[:, 128:256]
        dst[...] = acc[0:nout, 0:128] + pbuf_ref[pl.ds(1, nout), :]


def _gated_sum(gates, gx_ref, ocat_ref):
    gh, gl = _split(gates)
    gexp = (jnp.dot(gh, gx_ref[...], preferred_element_type=F32)
            + jnp.dot(gl, gx_ref[...], preferred_element_type=F32))
    out = gexp[:, 0:W_A] * ocat_ref[0]
    for c in range(1, N_BRANCH):
        out = out + gexp[:, c * W_A:(c + 1) * W_A] * ocat_ref[c]
    return out


def _nsa_prompt_kernel(q_ref, rows_ref, small_ref, wk_ref, wv_ref, bcmp_ref, tselt_ref, covt_ref,
                       wmt_ref, cmt_ref, gx_ref,
                       o_ref, kc_ref, vc_ref, pbuf_ref, mb_ref, acc_ref, m_ref, l_ref, st_ref, ocat_ref, ocatt_ref,
                       *, ns, n_top):
    bi = pl.program_id(1)
    T = rows_ref.shape[1]
    nch = T // CMP_STRIDE
    QB = Q_BLOCK
    SEL, WIN = 1, 2

    @pl.when(bi == 0)
    def _():
        pbuf_ref[...] = jnp.zeros(pbuf_ref.shape, F32)
        _compress(rows_ref, nch, wk_ref, wv_ref, pbuf_ref, kc_ref, vc_ref)

    t0 = bi * QB
    q = q_ref[0]
    tc = lax.broadcasted_iota(jnp.int32, (QB, nch), 0) + t0
    cend = lax.broadcasted_iota(jnp.int32, (QB, nch), 1) * CMP_STRIDE + (CMP_BLOCK - 1)
    cvalid = _rep_rows(tc - cend >= 0)
    t_row = lax.broadcasted_iota(jnp.int32, (1, LANES), 1) + t0
    qg = [_stack_heads(q, g).astype(BF16) for g in range(A_KV)]
    qt = (q * LOG2E).T
    qgt = [jnp.concatenate([qt[(A_REP * g + r) * A_HD:(A_REP * g + r + 1) * A_HD] for r in range(A_REP)],
                           axis=1).astype(BF16) for g in range(A_KV)]

    for g in range(A_KV):
        kc = kc_ref[:, g * A_HD:(g + 1) * A_HD].astype(BF16)
        vc = vc_ref[:, g * A_HD:(g + 1) * A_HD].astype(BF16)
        s = lax.dot_general(qg[g], kc, NT, preferred_element_type=F32) + bcmp_ref[0, g]
        p_c = _masked_softmax(s, cvalid)
        o_c = jnp.dot(p_c.astype(BF16), vc, preferred_element_type=F32)
        for r in range(A_REP):
            h = A_REP * g + r
            ocat_ref[0, :, h * A_HD:(h + 1) * A_HD] = o_c[r * QB:(r + 1) * QB]
        pc_sum = p_c[0:QB] + p_c[QB:2 * QB] + p_c[2 * QB:3 * QB] + p_c[3 * QB:4 * QB]
        sel_t = _select_blocks(pc_sum, covt_ref, st_ref, t_row, ns, n_top)
        mb_ref[g] = (sel_t - 1.0) * (-NEG)

    m_ref[...] = jnp.full(m_ref.shape, NEG, F32)
    l_ref[...] = jnp.zeros(l_ref.shape, F32)
    acc_ref[...] = jnp.zeros(acc_ref.shape, F32)
    lanes4 = lambda x: jnp.concatenate([x] * A_REP, axis=1)
    half = SEL_BLOCK

    def tiles(specs):
        cat = lambda xs: xs[0] if len(xs) == 1 else jnp.concatenate(xs, axis=0)
        cis = [2 * (br - 1) + g for br, g, _, _ in specs]
        sms, vs = [], []
        for br, g, kb, nb in specs:
            ks, vv, bs, ms = [], [], [], []
            for j in range(nb):
                off = pl.multiple_of((kb + j) * QB, QB)
                ks.append(rows_ref[2 * br, pl.ds(off, QB), g * A_HD:(g + 1) * A_HD])
                vv.append(rows_ref[2 * br + 1, pl.ds(off, QB), g * A_HD:(g + 1) * A_HD])
                bs.append(tselt_ref[bi - kb - j, g])
                if br == SEL:
                    blk = (QB // half) * (kb + j)
                    mt = jnp.concatenate([jnp.broadcast_to(mb_ref[g, pl.ds(blk + a, 1), :], (half, QB))
                                          for a in range(QB // half)], axis=0)
                    ms.append(mt + cmt_ref[jnp.minimum(bi - kb, 1)] if nb == 1 else mt)
                else:
                    ms.append(wmt_ref[bi - kb - j])
            vs.append(cat(vv).astype(BF16))
            sms.append(jnp.dot(cat(ks).astype(BF16), qgt[g], preferred_element_type=F32)
                       + cat(bs) + lanes4(cat(ms)))
        m_prevs = [m_ref[ci] for ci in cis]
        m_news = [jnp.maximum(mp, jnp.max(sm, axis=0, keepdims=True)) for mp, sm in zip(m_prevs, sms)]
        ps = [jnp.exp2(sm - mn) for sm, mn in zip(sms, m_news)]
        sums = [jnp.sum(p, axis=0, keepdims=True) for p in ps]
        pvs = [lax.dot_general(v, p.astype(BF16), TN, preferred_element_type=F32) for v, p in zip(vs, ps)]
        for ci, mp, mn, sm_, pv in zip(cis, m_prevs, m_news, sums, pvs):
            alpha = jnp.exp2(mp - mn)
            l_ref[ci] = alpha * l_ref[ci] + sm_
            acc_ref[ci] = alpha * acc_ref[ci] + pv
            m_ref[ci] = mn

    def body_sel(kb, carry):
        tiles([(SEL, g, kb, 1) for g in range(A_KV)])
        return carry

    def body_sel2(pair, carry):
        tiles([(SEL, g, 2 * pair, 2) for g in range(A_KV)])
        return carry

    def body_both(kb, carry):
        tiles([(br, g, kb, 1) for g in range(A_KV) for br in (SEL, WIN)])
        return carry

    lo = jnp.maximum(bi - WINDOW // QB, 0)
    lax.fori_loop(0, lo // 2, body_sel2, 0)
    lax.fori_loop(2 * (lo // 2), lo, body_sel, 0)
    lax.fori_loop(lo, bi + 1, body_both, 0)

    for br in (SEL, WIN):
        for g in range(A_KV):
            ci = 2 * (br - 1) + g
            o_t = acc_ref[ci] / jnp.maximum(l_ref[ci], TINY)
            for r in range(A_REP):
                h = A_REP * g + r
                ocatt_ref[br - 1, h * A_HD:(h + 1) * A_HD, :] = o_t[:, r * QB:(r + 1) * QB]
        ocat_ref[br] = ocatt_ref[br - 1].T

    o_ref[0] = _gated_sum(jax.nn.sigmoid(small_ref[0].astype(F32)), gx_ref, ocat_ref)


def _nsa_prompt(qn3, rows6, u3, wk, wv, tb):
    B, T, _ = qn3.shape
    QB = Q_BLOCK
    nqb = T // QB
    nch = T // CMP_STRIDE
    ns = T // SEL_BLOCK
    ns_pad = tb['covt'].shape[0]
    n_top = min(TOP_N, ns)
    nw = tb['wmt'].shape[0]
    kern = functools.partial(_nsa_prompt_kernel, ns=ns, n_top=n_top)
    c2 = lambda shape: pl.BlockSpec(shape, lambda b, i: (0,) * len(shape))
    return pl.pallas_call(
        kern,
        grid=(B, nqb),
        in_specs=[pl.BlockSpec((1, QB, W_A), lambda b, i: (b, i, 0)),
                  pl.BlockSpec((6, T, 128), lambda b, i: (0, b, 0)),
                  pl.BlockSpec((1, QB, LANES), lambda b, i: (b, i, OFF['small'] // LANES)),
                  c2((CMP_STRIDE, 128, 256)), c2((CMP_STRIDE, 128, 256)),
                  pl.BlockSpec((1, A_KV, A_REP * QB, nch), lambda b, i: (i, 0, 0, 0)),
                  c2((nqb, A_KV, QB, A_REP * QB)),
                  c2((ns_pad, nch)), c2((nw, QB, QB)), c2((2, QB, QB)),
                  c2((LANES, N_BRANCH * W_A))],
        out_specs=pl.BlockSpec((1, QB, W_A), lambda b, i: (b, i, 0)),
        out_shape=jax.ShapeDtypeStruct((B, T, W_A), F32),
        scratch_shapes=[pltpu.VMEM((nch, 128), F32), pltpu.VMEM((nch, 128), F32),
                        pltpu.VMEM((nch + 8, 128), F32),
                        pltpu.VMEM((A_KV, ns_pad, QB), F32),
                        pltpu.VMEM((2 * A_KV, A_HD, A_REP * QB), F32),
                        pltpu.VMEM((2 * A_KV, 1, A_REP * QB), F32),
                        pltpu.VMEM((2 * A_KV, 1, A_REP * QB), F32),
                        pltpu.VMEM((ns_pad, LANES), F32),
                        pltpu.VMEM((N_BRANCH, QB, W_A), F32),
                        pltpu.VMEM((2, W_A, QB), F32)],
        compiler_params=_cparams(("parallel", "arbitrary")),
        name="nsa_prompt",
    )(qn3, rows6, u3, wk, wv, tb['bcmp'], tb['tselt'], tb['covt'], tb['wmt'], tb['cmt'], tb['gx'])


def _nsa_sample_kernel(pt_ref, *refs, pg, past, tp, t_real, ns, n_top):
    cmp_refs = refs[0:pg]
    sel_refs = refs[pg:2 * pg]
    (q_ref, rows_ref, small_ref, win_ref, wk_ref, wv_ref, bcmp_ref, bsel_ref, bwin_ref, cov_ref, e_ref, perm_ref,
     o_ref, cslab, sslab, kc_ref, vc_ref, pbuf_ref, st_ref, s_ref) = refs[2 * pg:]
    p = pl.program_id(1)
    npg = pl.num_programs(1)
    n_pages = past // PAGE_SIZE
    SL = (n_pages + 1) * PAGE_SIZE
    nblk = kc_ref.shape[0]
    WK = WINDOW + LANES
    per_page = PAGE_SIZE // CMP_STRIDE

    for k in range(pg):
        page = p * pg + k
        row0 = pl.multiple_of(page * per_page, per_page)
        for kv in range(2):
            tile = cmp_refs[k][kv * 128:(kv + 1) * 128, :].astype(BF16)
            ordered = lax.dot_general(perm_ref[...], tile, NT, preferred_element_type=F32)
            for j in range(CMP_STRIDE):
                cslab[kv, j, pl.ds(row0, per_page), :] = ordered[j * per_page:(j + 1) * per_page]
            sslab[kv, page] = sel_refs[k][kv * 128:(kv + 1) * 128, :]

    @pl.when(p == npg - 1)
    def _():
        new_t = [_pad_rows(rows_ref[i], LANES).T for i in range(2, 6)]
        for kv in range(2):
            cslab[kv, :, nblk:nblk + 16, :] = jnp.zeros((CMP_STRIDE, 16, 128), F32)
            for i in range(tp):
                cslab[kv, i, nblk:nblk + 1, :] = rows_ref[kv, i:i + 1, :]
            sslab[kv, n_pages] = new_t[kv]
        pbuf_ref[...] = jnp.zeros(pbuf_ref.shape, F32)
        _compress(cslab, nblk + 8, wk_ref, wv_ref, pbuf_ref, kc_ref, vc_ref, by_phase=True)

        q = q_ref[0]
        gates = jax.nn.sigmoid(small_ref[0].astype(F32))
        R = A_REP * tp
        zq = jnp.zeros((R, A_HD), F32)
        q2 = jnp.concatenate([jnp.concatenate([_stack_heads(q, 0), zq], axis=1),
                              jnp.concatenate([zq, _stack_heads(q, 1)], axis=1)], axis=0).astype(BF16)
        rep2 = lambda x: jnp.concatenate([x] * (A_KV * A_REP), axis=0)
        ti = lax.broadcasted_iota(jnp.int32, (tp, SL), 0) + past
        causal = ti - lax.broadcasted_iota(jnp.int32, (tp, SL), 1) >= 0
        tc = lax.broadcasted_iota(jnp.int32, (tp, nblk), 0) + past
        cend = lax.broadcasted_iota(jnp.int32, (tp, nblk), 1) * CMP_STRIDE + (CMP_BLOCK - 1)
        wd = (lax.broadcasted_iota(jnp.int32, (tp, WK), 0) + WINDOW
              - lax.broadcasted_iota(jnp.int32, (tp, WK), 1))
        npt = n_pages + 1
        nck = 5 if npt % 5 == 0 else 1
        cpt = npt // nck
        ck = cpt * LANES

        s = lax.dot_general(q2, kc_ref[...].astype(BF16), NT, preferred_element_type=F32) + bcmp_ref[...]
        p_c = _masked_softmax(s, rep2(tc - cend >= 0))
        o_c = jnp.dot(p_c.astype(BF16), vc_ref[...].astype(BF16), preferred_element_type=F32)

        kmasks = []
        for g in range(A_KV):
            pg_ = p_c[g * R:(g + 1) * R]
            pc_sum = pg_[0:tp] + pg_[tp:2 * tp] + pg_[2 * tp:3 * tp] + pg_[3 * tp:4 * tp]
            sel = _select_rows(pc_sum, cov_ref, st_ref, past, ns, n_top, t_real).astype(BF16)
            km = jnp.dot(sel, e_ref[...], preferred_element_type=F32)
            kmasks.append(_rep_rows(causal & (km > 0.5)))
        svalid = jnp.concatenate(kmasks, axis=0)

        def sel_t(kv, c):
            return jnp.concatenate([sslab[kv, t] for t in range(c * cpt, (c + 1) * cpt)], axis=1).astype(BF16)

        for c in range(nck):
            s_ref[:, c * ck:(c + 1) * ck] = jnp.dot(q2, sel_t(0, c), preferred_element_type=F32)
        s_ref[...] = _masked_softmax(s_ref[...] + bsel_ref[...], svalid)
        o_s = jnp.zeros((A_KV * R, 2 * A_HD), F32)
        for c in range(nck):
            o_s = o_s + lax.dot_general(s_ref[:, c * ck:(c + 1) * ck].astype(BF16), sel_t(1, c), NT,
                                        preferred_element_type=F32)

        kw = jnp.concatenate([win_ref[0:128, :], new_t[2]], axis=1).astype(BF16)
        vw = jnp.concatenate([win_ref[128:256, :], new_t[3]], axis=1).astype(BF16)
        sw = jnp.dot(q2, kw, preferred_element_type=F32) + bwin_ref[...]
        p_w = _masked_softmax(sw, rep2((wd >= 0) & (wd < WINDOW)))
        o_w = lax.dot_general(p_w.astype(BF16), vw, NT, preferred_element_type=F32)

        for g in range(A_KV):
            gs = slice(g * A_HD, (g + 1) * A_HD)
            for r in range(A_REP):
                h = A_REP * g + r
                rs = slice(g * R + r * tp, g * R + (r + 1) * tp)
                out = (gates[:, 3 * h:3 * h + 1] * o_c[rs, gs] + gates[:, 3 * h + 1:3 * h + 2] * o_s[rs, gs]
                       + gates[:, 3 * h + 2:3 * h + 3] * o_w[rs, gs])
                o_ref[0, :, h * A_HD:(h + 1) * A_HD] = out


def _nsa_sample(qn3, rows6, u3, pool_cmp, pool_sel, win_t, layer, page_table, wk, wv, tb, past, t_real):
    B, tp, _ = qn3.shape
    n_pages = past // PAGE_SIZE
    pg = 8 if n_pages % 8 == 0 else n_pages
    npg = n_pages // pg
    SL = past + LANES
    nblk = past // CMP_STRIDE
    ns = -(-(past + t_real) // SEL_BLOCK)
    ns_pad = tb['cov'].shape[1]
    n_top = min(TOP_N, ns)
    WK = WINDOW + LANES
    kern = functools.partial(_nsa_sample_kernel, pg=pg, past=past, tp=tp, t_real=t_real, ns=ns, n_top=n_top)

    def page_spec(k):
        return pl.BlockSpec((None, None, 256, PAGE_SIZE), lambda b, p, pt: (layer, pt[b, p * pg + k], 0, 0))

    def c_(shape):
        return pl.BlockSpec(shape, lambda b, p, pt: (0,) * len(shape))

    R = A_REP * tp
    grid_spec = pltpu.PrefetchScalarGridSpec(
        num_scalar_prefetch=1,
        grid=(B, npg),
        in_specs=([page_spec(k) for k in range(pg)] + [page_spec(k) for k in range(pg)]
                  + [pl.BlockSpec((1, tp, W_A), lambda b, p, pt: (b, 0, 0)),
                     pl.BlockSpec((6, tp, 128), lambda b, p, pt: (0, b, 0)),
                     pl.BlockSpec((1, tp, LANES), lambda b, p, pt: (b, 0, OFF['small'] // LANES)),
                     pl.BlockSpec((None, None, 256, WINDOW), lambda b, p, pt: (layer, b, 0, 0)),
                     c_((CMP_STRIDE, 128, 256)), c_((CMP_STRIDE, 128, 256)),
                     c_((A_KV * R, nblk)), c_((A_KV * R, SL)), c_((A_KV * R, WK)),
                     c_((nblk, ns_pad)), c_((ns_pad, SL)), c_((PAGE_SIZE, PAGE_SIZE))]),
        out_specs=pl.BlockSpec((1, tp, W_A), lambda b, p, pt: (b, 0, 0)),
        scratch_shapes=[pltpu.VMEM((2, CMP_STRIDE, nblk + 16, 128), F32),
                        pltpu.VMEM((2, n_pages + 1, 128, PAGE_SIZE), F32),
                        pltpu.VMEM((nblk, 128), F32), pltpu.VMEM((nblk, 128), F32),
                        pltpu.VMEM((nblk + 16, 128), F32),
                        pltpu.VMEM((tp, ns_pad), F32), pltpu.VMEM((A_KV * R, SL), F32)],
    )
    return pl.pallas_call(
        kern,
        grid_spec=grid_spec,
        out_shape=jax.ShapeDtypeStruct((B, tp, W_A), F32),
        compiler_params=_cparams(("parallel", "arbitrary")),
        name="nsa_sample",
    )(page_table, *([pool_cmp] * pg), *([pool_sel] * pg), qn3, rows6, u3, win_t, wk, wv,
      *(tb[k].reshape(A_KV * R, -1) for k in ('bcmp', 'bsel', 'bwin')), tb['cov'], tb['emat'], tb['perm'])


def _tri(L, lower_incl):
    r = lax.broadcasted_iota(jnp.int32, (L, L), 0)
    c = lax.broadcasted_iota(jnp.int32, (L, L), 1)
    return (r >= c) if lower_incl else (r > c)


def _mlstm_kernel(x_ref, small_ref, smallt_ref, brow_ref, bcol_ref, c0_ref, n0_ref, m0_ref,
                  h_ref, co_ref, no_ref, mo_ref, c_s, n_s, m_s, *, L, t_real):
    c = pl.program_id(1)
    nc = pl.num_programs(1)

    @pl.when(c == 0)
    def _():
        c_s[...] = c0_ref[0]
        n_s[...] = n0_ref[0]
        m_s[...] = m0_ref[0]

    sm = small_ref[0].astype(F32)
    smt = smallt_ref[0, 0]
    li_col = sm[:, SM_MIF:SM_MIF + 4] + brow_ref[0:1, 0:4]
    lf_col = _log_sigmoid(sm[:, SM_MIF + 4:SM_MIF + 8] + brow_ref[0:1, 4:8])
    li_row = smt[0:4, :] + bcol_ref[0:4, :]
    lf_row = _log_sigmoid(smt[4:8, :] + bcol_ref[4:8, :])
    if t_real % L:
        tcol = lax.broadcasted_iota(jnp.int32, (L, 4), 0) + c * L
        trow = lax.broadcasted_iota(jnp.int32, (4, L), 1) + c * L
        li_col = jnp.where(tcol < t_real, li_col, NEG)
        lf_col = jnp.where(tcol < t_real, lf_col, 0.0)
        li_row = jnp.where(trow < t_real, li_row, NEG)
        lf_row = jnp.where(trow < t_real, lf_row, 0.0)
    low = _tri(L, True)
    b_col = jnp.dot(low.astype(F32), lf_col, precision=HI, preferred_element_type=F32)
    b_row = lax.dot_general(lf_row, low.astype(F32), NT, precision=HI, preferred_element_type=F32)

    hd = range(M_HEADS)
    q = [x_ref[0, :, h * M_HD:(h + 1) * M_HD].astype(F32) for h in hd]
    k = [x_ref[0, :, W_M + h * M_HD:W_M + (h + 1) * M_HD].astype(F32) * (M_HD ** -0.5) for h in hd]
    v = [x_ref[0, :, 2 * W_M + h * M_HD:2 * W_M + (h + 1) * M_HD].astype(F32) for h in hd]
    bc = [b_col[:, h:h + 1] for h in hd]
    Dm = [jnp.where(low, bc[h] - b_row[h:h + 1, :] + li_row[h:h + 1, :], NEG) for h in hd]
    m_prev = [m_s[h][:, 0:1] for h in hd]
    a = [bc[h] + m_prev[h] for h in hd]
    mt = [jnp.maximum(a[h], jnp.max(Dm[h], axis=-1, keepdims=True)) for h in hd]
    qk = [lax.dot_general(q[h], k[h], NT, preferred_element_type=F32) for h in hd]
    S = [qk[h] * jnp.exp(Dm[h] - mt[h]) for h in hd]
    inter = [jnp.exp(a[h] - mt[h]) for h in hd]
    C = [c_s[h] for h in hd]
    n = [n_s[h] for h in hd]
    qC = [jnp.dot(q[h], C[h], preferred_element_type=F32) for h in hd]
    Sv = [jnp.dot(S[h], v[h], preferred_element_type=F32) for h in hd]
    den = [inter[h] * jnp.sum(q[h] * n[h], axis=-1, keepdims=True) + jnp.sum(S[h], axis=-1, keepdims=True)
           for h in hd]
    for h in hd:
        h_ref[0, :, h * M_HD:(h + 1) * M_HD] = ((inter[h] * qC[h] + Sv[h])
                                                / jnp.maximum(jnp.abs(den[h]), jnp.exp(-mt[h])))
    bL = [bc[h][L - 1:L, :] for h in hd]
    wlog = [bL[h] - bc[h] + li_col[:, h:h + 1] for h in hd]
    m_new = [jnp.maximum(bL[h] + m_prev[h], jnp.max(wlog[h], axis=0, keepdims=True)) for h in hd]
    dec = [jnp.exp(bL[h] + m_prev[h] - m_new[h]) for h in hd]
    kw = [k[h] * jnp.exp(wlog[h] - m_new[h]) for h in hd]
    kv = [lax.dot_general(kw[h], v[h], TN, preferred_element_type=F32) for h in hd]
    for h in hd:
        c_s[h] = dec[h] * C[h] + kv[h]
        n_s[h] = dec[h] * n[h] + jnp.sum(kw[h], axis=0, keepdims=True)
        m_s[h] = jnp.broadcast_to(m_new[h], (1, LANES))

    @pl.when(c == nc - 1)
    def _():
        co_ref[0] = c_s[...]
        no_ref[0] = n_s[...]
        mo_ref[0] = m_s[...]


def _mlstm(u3, smallt, m_bi, m_bf, C0, n0, m0, L, t_real):
    B, T, _ = u3.shape
    nc = T // L
    brow = jnp.concatenate([m_bi, m_bf]).reshape(1, 8)
    bcol = jnp.concatenate([m_bi, m_bf]).reshape(8, 1)
    n0 = n0.reshape(B, M_HEADS, 1, M_HD)
    m0 = jnp.broadcast_to(m0[:, :, None, None], (B, M_HEADS, 1, LANES))
    kern = functools.partial(_mlstm_kernel, L=L, t_real=t_real)
    st = lambda shape: pl.BlockSpec(shape, lambda b, c: (b,) + (0,) * (len(shape) - 1))
    h, Cn, nn, mn = pl.pallas_call(
        kern,
        grid=(B, nc),
        in_specs=[pl.BlockSpec((1, L, 3 * W_M), lambda b, c: (b, c, OFF['m_qkv'] // (3 * W_M))),
                  pl.BlockSpec((1, L, LANES), lambda b, c: (b, c, OFF['small'] // LANES)),
                  pl.BlockSpec((1, 1, 16, L), lambda b, c: (b, c, 0, 0)),
                  pl.BlockSpec((1, 8), lambda b, c: (0, 0)), pl.BlockSpec((8, 1), lambda b, c: (0, 0)),
                  st((1, M_HEADS, M_HD, M_HD)), st((1, M_HEADS, 1, M_HD)), st((1, M_HEADS, 1, LANES))],
        out_specs=[pl.BlockSpec((1, L, W_M), lambda b, c: (b, c, 0)),
                   st((1, M_HEADS, M_HD, M_HD)), st((1, M_HEADS, 1, M_HD)), st((1, M_HEADS, 1, LANES))],
        out_shape=[jax.ShapeDtypeStruct((B, T, W_M), F32),
                   jax.ShapeDtypeStruct((B, M_HEADS, M_HD, M_HD), F32),
                   jax.ShapeDtypeStruct((B, M_HEADS, 1, M_HD), F32),
                   jax.ShapeDtypeStruct((B, M_HEADS, 1, LANES), F32)],
        scratch_shapes=[pltpu.VMEM((M_HEADS, M_HD, M_HD), F32), pltpu.VMEM((M_HEADS, 1, M_HD), F32),
                        pltpu.VMEM((M_HEADS, 1, LANES), F32)],
        compiler_params=_cparams(("parallel", "arbitrary")),
        name="mlstm",
    )(u3, u3, smallt, brow, bcol, C0, n0, m0)
    return h, Cn, nn.reshape(B, M_HEADS, M_HD), mn[:, :, 0, 0]


def _gdn_a_kernel(x_ref, small_ref, smallt_ref, cw_ref, prow_ref, pcol_ref, buf_ref,
                  u_ref, w_ref, qe_ref, kd_ref, at_ref, eg_ref, xbuf, *, L, CB, t_real):
    c = pl.program_id(1)
    PRE = 8
    R = CB * L

    @pl.when(c == 0)
    def _():
        xbuf[0:PRE, :] = buf_ref[0]

    xbuf[PRE:PRE + R, :] = x_ref[0].astype(F32)
    conv = jnp.zeros((R, 3 * W_G), F32)
    for j in range(CONV_W):
        conv = conv + xbuf[pl.ds(PRE - (CONV_W - 1) + j, R), :] * cw_ref[j:j + 1, :]
    xbuf[0:PRE, :] = xbuf[R:R + PRE, :]
    conv = _silu(conv)

    sm = small_ref[0].astype(F32)
    g_all = -jnp.exp(prow_ref[0:1, 0:4]) * _softplus(sm[:, SM_GAB:SM_GAB + 4] + prow_ref[1:2, 0:4])
    beta_all = jax.nn.sigmoid(sm[:, SM_GAB + 4:SM_GAB + 8])
    lowf = _tri(L, True).astype(F32)

    H = G_HEADS
    lane_h = lax.broadcasted_iota(jnp.int32, (L, H * L), 1) // L
    row_p = lax.broadcasted_iota(jnp.int32, (L, H * L), 0)
    col_p = lax.broadcasted_iota(jnp.int32, (L, H * L), 1) % L
    blockmask = (lax.broadcasted_iota(jnp.int32, (H * L, H * L), 0) // L
                 == lax.broadcasted_iota(jnp.int32, (H * L, H * L), 1) // L)

    def pack_diag(full):
        out = full[0:L]
        for h in range(1, H):
            out = jnp.where(lane_h == h, full[h * L:(h + 1) * L], out)
        return out

    def block_diag(m, parts):
        if L % 16:
            return _split(jnp.where(blockmask, jnp.concatenate([m] * H, axis=0), 0.0))
        return tuple(jnp.where(blockmask, jnp.concatenate([p] * H, axis=0), jnp.zeros((), BF16))
                     for p in parts)

    a_list, rhs_list = [], []
    for cb in range(CB):
        rs = slice(cb * L, (cb + 1) * L)
        smt = smallt_ref[0, cb]
        g_col = g_all[rs]
        beta_col = beta_all[rs]
        g_row = -jnp.exp(pcol_ref[0:4, 0:1]) * _softplus(smt[8:12, :] + pcol_ref[0:4, 1:2])
        if t_real % L:
            t0 = (c * CB + cb) * L
            tcol = lax.broadcasted_iota(jnp.int32, (L, 4), 0) + t0
            trow = lax.broadcasted_iota(jnp.int32, (4, L), 1) + t0
            g_col = jnp.where(tcol < t_real, g_col, 0.0)
            beta_col = jnp.where(tcol < t_real, beta_col, 0.0)
            g_row = jnp.where(trow < t_real, g_row, 0.0)
        G_col = jnp.dot(lowf, g_col, precision=HI, preferred_element_type=F32)
        G_row = lax.dot_general(g_row, lowf, NT, precision=HI, preferred_element_type=F32)

        qs, ks, kbs, rhss = [], [], [], []
        for h in range(G_HEADS):
            hs = slice(h * G_HD, (h + 1) * G_HD)
            cq = conv[rs, h * G_HD:(h + 1) * G_HD]
            ck = conv[rs, W_G + h * G_HD:W_G + (h + 1) * G_HD]
            v = conv[rs, 2 * W_G + h * G_HD:2 * W_G + (h + 1) * G_HD]
            q = cq * lax.rsqrt(jnp.sum(cq * cq, axis=-1, keepdims=True) + EPS) * (G_HD ** -0.5)
            k = ck * lax.rsqrt(jnp.sum(ck * ck, axis=-1, keepdims=True) + EPS)
            Gc = G_col[:, h:h + 1]
            bcol = beta_col[:, h:h + 1]
            kb = k * bcol
            eG = jnp.exp(Gc)
            GL = Gc[L - 1:L, :]
            qs.append(q)
            ks.append(k)
            kbs.append(kb)
            rhss.append(jnp.concatenate([v * bcol, kb * eG], axis=1))
            qe_ref[0, rs, hs] = q * eG
            kd_ref[0, rs, hs] = k * jnp.exp(GL - Gc)
            eg_ref[0, cb, h:h + 1, :] = jnp.broadcast_to(jnp.exp(GL), (1, LANES))
        eg_ref[0, cb, G_HEADS:8, :] = jnp.zeros((8 - G_HEADS, LANES), F32)

        Gc_p = jnp.concatenate([jnp.broadcast_to(G_col[:, h:h + 1], (L, L)) for h in range(H)], axis=1)
        Gr_p = jnp.concatenate([jnp.broadcast_to(G_row[h:h + 1, :], (L, L)) for h in range(H)], axis=1)
        low_p = row_p >= col_p
        dmask = jnp.where(low_p, jnp.exp(jnp.where(low_p, Gc_p - Gr_p, 0.0)), 0.0)
        k_st = _split(jnp.concatenate(ks, axis=0))
        A = pack_diag(_dot3(_split(jnp.concatenate(kbs, axis=0)), k_st, NT)) * jnp.where(row_p > col_p, dmask, 0.0)
        attn = pack_diag(lax.dot_general(jnp.concatenate(qs, axis=0).astype(BF16), k_st[0], NT,
                                         preferred_element_type=F32)) * dmask
        at_ref[0, rs, :] = attn
        a_list.append(A)
        rhs_list.append(_split(jnp.concatenate(rhss, axis=0)))

    chunks = range(CB)
    X = [jnp.where(row_p == col_p, 1.0, 0.0) - a_list[cb] for cb in chunks]
    As = [_split(a_list[cb]) for cb in chunks]
    Pw = [_dot3(As[cb], block_diag(a_list[cb], As[cb])) for cb in chunks]
    span = 2
    while span < L:
        Ps = [_split(Pw[cb]) for cb in chunks]
        Pbd = [block_diag(Pw[cb], Ps[cb]) for cb in chunks]
        X = [X[cb] + _dot3(_split(X[cb]), Pbd[cb]) for cb in chunks]
        span *= 2
        if span < L:
            Pw = [_dot3(Ps[cb], Pbd[cb]) for cb in chunks]
    sol = [_dot3(block_diag(X[cb], _split(X[cb])), rhs_list[cb]) for cb in chunks]
    for cb in chunks:
        for h in range(H):
            rs = slice(cb * L, (cb + 1) * L)
            hs = slice(h * G_HD, (h + 1) * G_HD)
            u_ref[0, rs, hs] = sol[cb][h * L:(h + 1) * L, 0:G_HD]
            w_ref[0, rs, hs] = sol[cb][h * L:(h + 1) * L, G_HD:2 * G_HD]


def _gdn_b_kernel(u_ref, w_ref, qe_ref, kd_ref, at_ref, eg_ref, s0_ref, o_ref, so_ref, s_s, *, L, BB):
    c = pl.program_id(1)
    nc = pl.num_programs(1)

    @pl.when(c == 0)
    def _():
        s_s[...] = s0_ref[...]

    ch = [(b, h, slice(h * G_HD, (h + 1) * G_HD)) for b in range(BB) for h in range(G_HEADS)]
    S = [s_s[b, h] for b, h, _ in ch]
    r = [jnp.dot(jnp.concatenate([w_ref[b, :, hs], qe_ref[b, :, hs]], axis=0), S[i], preferred_element_type=F32)
         for i, (b, h, hs) in enumerate(ch)]
    v_new = [u_ref[b, :, hs] - r[i][0:L] for i, (b, h, hs) in enumerate(ch)]
    av = [jnp.dot(at_ref[b, :, h * L:(h + 1) * L], v_new[i], preferred_element_type=F32)
          for i, (b, h, hs) in enumerate(ch)]
    kdv = [lax.dot_general(kd_ref[b, :, hs], v_new[i], TN, preferred_element_type=F32)
           for i, (b, h, hs) in enumerate(ch)]
    for i, (b, h, hs) in enumerate(ch):
        o_ref[b, :, hs] = r[i][L:2 * L] + av[i]
        s_s[b, h] = eg_ref[b, 0, h:h + 1, :] * S[i] + kdv[i]

    @pl.when(c == nc - 1)
    def _():
        so_ref[...] = s_s[...]


def _gdn(u3, smallt, g_conv, g_A_log, g_dt_bias, buf, S0, L, t_real):
    B, T, _ = u3.shape
    nc = T // L
    CB = 4 if nc % 4 == 0 else 1
    BB = 8 if B % 8 == 0 else B
    prow = jnp.stack([g_A_log, g_dt_bias])
    pcol = jnp.stack([g_A_log, g_dt_bias], axis=1)
    buf8 = jnp.concatenate([jnp.zeros((B, 8 - (CONV_W - 1), 3 * W_G), F32), buf], axis=1)
    st = lambda shape: pl.BlockSpec(shape, lambda b, c: (b,) + (0,) * (len(shape) - 1))
    cst = lambda shape: pl.BlockSpec(shape, lambda b, c: (0,) * len(shape))
    row = lambda nb, r, w: pl.BlockSpec((nb, r, w), lambda b, c: (b, c, 0))
    tok = lambda w: jax.ShapeDtypeStruct((B, T, w), F32)
    u, w, qe, kd, at, eg = pl.pallas_call(
        functools.partial(_gdn_a_kernel, L=L, CB=CB, t_real=t_real),
        grid=(B, nc // CB),
        in_specs=[pl.BlockSpec((1, CB * L, 3 * W_G), lambda b, c: (b, c, OFF['g_qkv'] // (3 * W_G))),
                  pl.BlockSpec((1, CB * L, LANES), lambda b, c: (b, c, OFF['small'] // LANES)),
                  pl.BlockSpec((1, CB, 16, L), lambda b, c: (b, c, 0, 0)),
                  cst((CONV_W, 3 * W_G)), cst((2, 4)), cst((4, 2)), st((1, 8, 3 * W_G))],
        out_specs=[row(1, CB * L, W_G)] * 4 + [row(1, CB * L, G_HEADS * L),
                                               pl.BlockSpec((1, CB, 8, LANES), lambda b, c: (b, c, 0, 0))],
        out_shape=[tok(W_G)] * 4 + [tok(G_HEADS * L), jax.ShapeDtypeStruct((B, nc, 8, LANES), F32)],
        scratch_shapes=[pltpu.VMEM((CB * L + 8, 3 * W_G), F32)],
        compiler_params=_cparams(("parallel", "arbitrary")),
        name="gdn_a",
    )(u3, u3, smallt, g_conv, prow, pcol, buf8)
    sblk = pl.BlockSpec((BB, G_HEADS, G_HD, G_HD), lambda b, c: (b, 0, 0, 0))
    return pl.pallas_call(
        functools.partial(_gdn_b_kernel, L=L, BB=BB),
        grid=(B // BB, nc),
        in_specs=[row(BB, L, W_G)] * 4 + [row(BB, L, G_HEADS * L),
                                          pl.BlockSpec((BB, 1, 8, LANES), lambda b, c: (b, c, 0, 0)), sblk],
        out_specs=[row(BB, L, W_G), sblk],
        out_shape=[tok(W_G), jax.ShapeDtypeStruct((B, G_HEADS, G_HD, G_HD), F32)],
        scratch_shapes=[pltpu.VMEM((BB, G_HEADS, G_HD, G_HD), F32)],
        compiler_params=_cparams(("parallel", "arbitrary")),
        name="gdn_b",
    )(u, w, qe, kd, at, eg, S0)


def _head_rmsnorm(x, gain_ref, nheads, hd):
    outs = []
    for h in range(nheads):
        xh = x[:, h * hd:(h + 1) * hd]
        ms = jnp.mean(xh * xh, axis=-1, keepdims=True)
        outs.append(xh * lax.rsqrt(ms + EPS) * gain_ref[...])
    return jnp.concatenate(outs, axis=1)


def _merge_kernel(x_ref, oa_ref, hm_ref, og_ref, az_ref, mo_ref, mz_ref, gz_ref, mg_ref,
                  wb_ref, wo_ref, mhn_ref, ghn_ref, y_ref):
    oa = oa_ref[...] * _silu(az_ref[...].astype(F32))
    om = (_head_rmsnorm(hm_ref[...], mhn_ref, M_HEADS, M_HD) * jax.nn.sigmoid(mo_ref[...].astype(F32))
          * _silu(mz_ref[...].astype(F32)))
    og = _head_rmsnorm(og_ref[...], ghn_ref, G_HEADS, G_HD) * _silu(gz_ref[...].astype(F32))
    y = jnp.zeros(y_ref.shape, F32)
    for i, br in enumerate((oa, om, og)):
        proj = jnp.dot(br.astype(BF16), wb_ref[i], preferred_element_type=F32)
        y = y + jax.nn.sigmoid(mg_ref[:, i * D_MODEL:(i + 1) * D_MODEL].astype(F32)) * proj
    y_ref[...] = x_ref[...] + jnp.dot(y.astype(BF16), wo_ref[...], preferred_element_type=F32)


def _merge_out(x2, o_a, h_m, o_g, u2, w_branch, w_out, m_hn, g_hn):
    n = x2.shape[0]
    tm = min(512, n)
    row = lambda w, off: pl.BlockSpec((tm, w), lambda i: (i, off // w))
    cst = lambda shape: pl.BlockSpec(shape, lambda i: (0,) * len(shape))
    return pl.pallas_call(
        _merge_kernel,
        grid=(n // tm,),
        in_specs=[row(D_MODEL, 0), row(W_A, 0), row(W_M, 0), row(W_G, 0),
                  row(W_A, OFF['a_z']), row(W_M, OFF['m_o']), row(W_M, OFF['m_z']), row(W_G, OFF['g_z']),
                  row(N_BRANCH * D_MODEL, OFF['merge']),
                  cst((N_BRANCH, W_A, D_MODEL)), cst((D_MODEL, D_MODEL)), cst((1, M_HD)), cst((1, G_HD))],
        out_specs=row(D_MODEL, 0),
        out_shape=jax.ShapeDtypeStruct((n, D_MODEL), F32),
        compiler_params=_cparams(("parallel",)),
        name="merge_out",
    )(x2, o_a, h_m, o_g, u2, u2, u2, u2, u2, w_branch.astype(BF16), w_out.astype(BF16),
      m_hn.reshape(1, M_HD), g_hn.reshape(1, G_HD))


def _small_t(u3, L):
    B, T, _ = u3.shape
    s = u3[:, :, OFF['small'] + SM_MIF:OFF['small'] + SM_MIF + 16].astype(F32)
    return s.reshape(B, T // L, L, 16).transpose(0, 1, 3, 2)


def _layer(x, lw, tb, past, q_off):
    B, T, _ = x.shape
    if past is None:
        tp, L = T, math.gcd(T, M_CHUNK)
    else:
        tp = -(-T // 8) * 8
        L = tp
        x = jnp.pad(x, ((0, 0), (0, tp - T), (0, 0)))
    x2 = x.reshape(B * tp, D_MODEL)
    u2 = _proj_in(x2, lw['norm_g'], lw['w_perm'], BF16 if past is None else F32)
    u3 = u2.reshape(B, tp, N_PERM)
    if past is None:
        C0 = jnp.zeros((B, M_HEADS, M_HD, M_HD), F32)
        n0 = jnp.zeros((B, M_HEADS, M_HD), F32)
        m0 = jnp.zeros((B, M_HEADS), F32)
        S0 = jnp.zeros((B, G_HEADS, G_HD, G_HD), F32)
        buf = jnp.zeros((B, CONV_W - 1, 3 * W_G), F32)
    else:
        C0, n0, m0, S0, buf = past['mC'], past['mn'], past['mm'], past['gS'], past['gconv']
    qn, rows6 = _prep_a(u2, lw['a_qn'], lw['a_kn'])
    qn3 = qn.reshape(B, tp, W_A)
    if past is None:
        o_a = _nsa_prompt(qn3, rows6, u3, lw['wk'], lw['wv'], tb)
    else:
        o_a = _nsa_sample(qn3, rows6, u3, past['cmp'], past['sel'], past['win_t'], past['layer'], past['page_table'],
                          lw['wk'], lw['wv'], tb, q_off, T)
    smallt = _small_t(u3, L)
    h_m, Cn, nn, mn = _mlstm(u3, smallt, lw['m_bi'], lw['m_bf'], C0, n0, m0, L, T)
    o_g, Sn = _gdn(u3, smallt, lw['g_conv'], lw['g_A_log'], lw['g_dt_bias'], buf, S0, L, T)
    y2 = _merge_out(x2, o_a.reshape(B * tp, W_A), h_m.reshape(B * tp, W_M), o_g.reshape(B * tp, W_G), u2,
                    lw['w_branch'], lw['w_out'], lw['m_hn'], lw['g_hn'])
    rows = rows6.reshape(3, 2, B, tp, A_KV, A_HD)[:, :, :, :T]
    new_cmp, new_sel, new_win = (jnp.moveaxis(rows[b], 0, 2) for b in range(3))
    g_qkv = u3[:, max(T - (CONV_W - 1), 0):T, OFF['g_qkv']:OFF['g_qkv'] + 3 * W_G].astype(F32)
    if past is None:
        win = new_win
    else:
        win = jnp.concatenate([past['win'], new_win], axis=1)
    full = jnp.concatenate([buf, g_qkv], axis=1)
    state = dict(cmp=new_cmp, sel=new_sel, win=win[:, -min(WINDOW, win.shape[1]):],
                 mC=Cn, mn=nn, mm=mn, gS=Sn, gconv=full[:, -(CONV_W - 1):])
    return y2.reshape(B, tp, D_MODEL)[:, :T], state


def kernel(x_prompt, x_sample, cache_cmp_kv, cache_sel_kv, cache_win_kv, state_mlstm_C, state_mlstm_n,
           state_mlstm_m, state_gdn_S, state_gdn_conv, page_table, norm_g, w_in, a_qn, a_kn, a_cmp_wk,
           a_cmp_wv, rel_bias, m_bi, m_bf, m_hn, g_conv, g_A_log, g_dt_bias, g_hn, w_branch, w_out):
    names = ('cmp', 'sel', 'win', 'mC', 'mn', 'mm', 'gS', 'gconv')
    st_p = {k: [] for k in names}
    st_s = {k: [] for k in names}
    past_len = page_table.shape[1] * PAGE_SIZE
    n_pool = cache_cmp_kv.shape[1]
    pos_minor = lambda c: jnp.swapaxes(c.reshape(c.shape[0], c.shape[1], c.shape[2], 2 * A_KV * A_HD), 2, 3)
    pool_cmp, pool_sel, win_t = pos_minor(cache_cmp_kv), pos_minor(cache_sel_kv), pos_minor(cache_win_kv)
    w_t = jnp.swapaxes(w_in, 1, 2)
    db, dt = x_sample.shape[0], x_sample.shape[1]
    tb_p = _prompt_tables(rel_bias, x_prompt.shape[1])
    tb_s = _sample_tables(rel_bias, past_len, -(-dt // 8) * 8, dt)
    y_p, y_s = x_prompt, x_sample
    for l in range(DEPTH):
        lw = dict(norm_g=norm_g[l], w_perm=_permute_w_in(w_t, l), a_qn=a_qn[l], a_kn=a_kn[l],
                  wk=_cmp_weights(a_cmp_wk[l]), wv=_cmp_weights(a_cmp_wv[l]),
                  m_bi=m_bi[l], m_bf=m_bf[l], m_hn=m_hn[l], g_conv=g_conv[l], g_A_log=g_A_log[l],
                  g_dt_bias=g_dt_bias[l], g_hn=g_hn[l], w_branch=w_branch[l], w_out=w_out[l])
        y_p, new_p = _layer(y_p, lw, tb_p, None, 0)
        past = dict(cmp=pool_cmp, sel=pool_sel, win_t=win_t, layer=l, win=cache_win_kv[l], page_table=page_table,
                    mC=state_mlstm_C[l], mn=state_mlstm_n[l], mm=state_mlstm_m[l],
                    gS=state_gdn_S[l], gconv=state_gdn_conv[l])
        y_s, new_s = _layer(y_s, lw, tb_s, past, past_len)
        for k in names:
            st_p[k].append(new_p[k])
            st_s[k].append(new_s[k])
    P = {k: jnp.stack(v) for k, v in st_p.items()}
    S = {k: jnp.stack(v) for k, v in st_s.items()}
    return (y_p, y_s, P['cmp'], S['cmp'], P['sel'], S['sel'], P['win'], S['win'],
            P['mC'], S['mC'], P['mn'], S['mn'], P['mm'], S['mm'], P['gS'], S['gS'], P['gconv'], S['gconv'])
```

```python
import functools
import math

import jax
import jax.numpy as jnp
from jax import lax
from jax.experimental import pallas as pl
from jax.experimental.pallas import tpu as pltpu

D_MODEL = 1024
DEPTH = 2
PAGE_SIZE = 128
A_HEADS = 8
A_KV = 2
A_REP = A_HEADS // A_KV
A_HD = 64
CMP_BLOCK = 32
CMP_STRIDE = 16
SEL_BLOCK = 64
TOP_N = 16
WINDOW = 512
Q_BLOCK = 128
N_BUCKETS = 32
MAX_DIST = 2048
M_HEADS = 4
M_HD = 128
M_CHUNK = 64
G_HEADS = 4
G_HD = 128
G_CHUNK = 64
CONV_W = 4
W_A = A_HEADS * A_HD
W_M = M_HEADS * M_HD
W_G = G_HEADS * G_HD
N_BRANCH = 3
EPS = 1e-6
NEG = -1e30
TINY = 1e-30
LOG2E = math.log2(math.e)

F32 = jnp.float32
BF16 = jnp.bfloat16
HI = lax.Precision.HIGHEST
NN = (((1,), (0,)), ((), ()))
NT = (((1,), (1,)), ((), ()))
TN = (((0,), (0,)), ((), ()))

LANES = 128
VMEM_LIMIT = 56 * 1024 * 1024

IN_ORDER = ('a_q', 'a_kv', 'a_gate', 'a_z', 'm_qkv', 'm_if', 'm_o', 'm_z', 'g_qkv', 'g_ab', 'g_z', 'merge')
IN_WIDTH = dict(a_q=W_A, a_kv=3 * 2 * A_KV * A_HD, a_gate=A_HEADS * 3, a_z=W_A, m_qkv=3 * W_M, m_if=2 * M_HEADS,
                m_o=W_M, m_z=W_M, g_qkv=3 * W_G, g_ab=2 * G_HEADS, g_z=W_G, merge=N_BRANCH * D_MODEL)
OFF = dict(merge=0, m_qkv=3072, g_qkv=4608, a_q=6144, a_z=6656, m_o=7168, m_z=7680, g_z=8192, a_kv=8704,
           small=9472)
N_PERM = 9600
SM_GATE, SM_MIF, SM_GAB = 0, 24, 32


def _cparams(sem):
    return pltpu.CompilerParams(dimension_semantics=sem, vmem_limit_bytes=VMEM_LIMIT)


def _silu(x):
    return x * jax.nn.sigmoid(x)


def _log_sigmoid(x):
    return jnp.minimum(x, 0.0) - jnp.log(1.0 + jnp.exp(-jnp.abs(x)))


def _softplus(x):
    return jnp.maximum(x, 0.0) + jnp.log(1.0 + jnp.exp(-jnp.abs(x)))


def _split(a):
    hi = a.astype(BF16)
    return hi, (a - hi.astype(F32)).astype(BF16)


def _dot3(a, b, dims=NN):
    mm = lambda x, y: lax.dot_general(x, y, dims, preferred_element_type=F32)
    return mm(a[0], b[0]) + mm(a[0], b[1]) + mm(a[1], b[0])


def _src_offsets():
    offs, off = {}, 0
    for name in IN_ORDER:
        offs[name] = off
        off += IN_WIDTH[name]
    return offs, off


def _permute_kernel(w_ref, o_ref):
    src, _ = _src_offsets()
    small = []
    for name in IN_ORDER:
        w = IN_WIDTH[name]
        rows = w_ref[src[name]:src[name] + w, :]
        if w % LANES:
            small.append(rows)
        else:
            o_ref[:, OFF[name]:OFF[name] + w] = rows.T.astype(BF16)
    used = sum(r.shape[0] for r in small)
    small.append(jnp.zeros((LANES - used, w_ref.shape[1]), F32))
    o_ref[:, OFF['small']:N_PERM] = jnp.concatenate(small, axis=0).T.astype(BF16)


def _permute_w_in(w_t, layer):
    _, n_in, d = w_t.shape
    tr = 128
    return pl.pallas_call(
        _permute_kernel,
        grid=(d // tr,),
        in_specs=[pl.BlockSpec((None, n_in, tr), lambda i: (layer, 0, i))],
        out_specs=pl.BlockSpec((tr, N_PERM), lambda i: (i, 0)),
        out_shape=jax.ShapeDtypeStruct((d, N_PERM), BF16),
        compiler_params=_cparams(("parallel",)),
        name="permute_w",
    )(w_t)


def _rel_bucket(dist):
    n = jnp.maximum(dist, 0)
    exact = N_BUCKETS // 2
    nf = jnp.maximum(n, exact).astype(F32)
    large = exact + (jnp.log(nf / exact) / math.log(MAX_DIST / exact) * (N_BUCKETS - exact)).astype(jnp.int32)
    return jnp.where(n < exact, n, jnp.minimum(large, N_BUCKETS - 1))


def _bias_kernel(thr_ref, tab_ref, d_ref, o_ref):
    n = jnp.maximum(d_ref[0], 0)
    for h in range(A_HEADS):
        acc = jnp.full(n.shape, tab_ref[h], F32)
        for k in range(1, N_BUCKETS):
            acc = jnp.where(n >= thr_ref[k], tab_ref[k * A_HEADS + h], acc)
        o_ref[0, h // A_REP, h % A_REP] = acc


def _bias_rows(rel_bias, dist, split_rows=False):
    N, Q, K = dist.shape
    nmax = 2 * MAX_DIST
    thr = jnp.sum(_rel_bucket(jnp.arange(nmax))[None, :] < jnp.arange(N_BUCKETS)[:, None], axis=1).astype(jnp.int32)
    smem = pl.BlockSpec(memory_space=pltpu.SMEM)
    out = pl.pallas_call(
        _bias_kernel,
        grid=(N,),
        in_specs=[smem, smem, pl.BlockSpec((1, Q, K), lambda i: (i, 0, 0))],
        out_specs=pl.BlockSpec((1, A_KV, A_REP, Q, K), lambda i: (i, 0, 0, 0, 0)),
        out_shape=jax.ShapeDtypeStruct((N, A_KV, A_REP, Q, K), F32),
        compiler_params=_cparams(("parallel",)),
        name="bias_rows",
    )(thr, rel_bias.astype(F32).reshape(N_BUCKETS * A_HEADS), dist.astype(jnp.int32))
    return out if split_rows else out.reshape(N, A_KV, A_REP * Q, K)


def _cmp_weights(w):
    wr = w.reshape(A_KV, 2, CMP_STRIDE, A_HD, A_HD)
    eye = jnp.eye(A_KV, dtype=w.dtype)
    full = jnp.einsum('gmjde,gh->jgdmhe', wr, eye)
    return full.reshape(CMP_STRIDE, A_KV * A_HD, 2 * A_KV * A_HD).astype(BF16)


def _cover_t(ns_pad, nch):
    s0 = jnp.arange(ns_pad)[:, None] * SEL_BLOCK
    c0 = jnp.arange(nch)[None, :] * CMP_STRIDE
    return ((c0 < s0 + SEL_BLOCK) & (s0 <= c0 + CMP_BLOCK - 1)).astype(F32)


def _expand_mat(ns_pad, nk):
    return (jnp.arange(nk)[None, :] // SEL_BLOCK == jnp.arange(ns_pad)[:, None]).astype(BF16)


def _gate_expand():
    col = jnp.arange(N_BRANCH * W_A)
    src = 3 * ((col % W_A) // A_HD) + col // W_A
    return (jnp.arange(LANES)[:, None] == src[None, :]).astype(BF16)


def _phase_perm():
    r = jnp.arange(PAGE_SIZE)
    per_page = PAGE_SIZE // CMP_STRIDE
    return (jnp.arange(PAGE_SIZE)[None, :] == ((r % per_page) * CMP_STRIDE + r // per_page)[:, None]).astype(BF16)


def _prompt_tables(rel_bias, T):
    QB = Q_BLOCK
    nqb = T // QB
    nch = T // CMP_STRIDE
    ns = T // SEL_BLOCK
    ns_pad = -(-ns // 8) * 8
    i_ = jnp.arange(QB)
    t = (jnp.arange(nqb) * QB)[:, None, None] + i_[None, :, None]
    cend = (jnp.arange(nch) * CMP_STRIDE + CMP_BLOCK - 1)[None, None, :]
    d = (jnp.arange(nqb) * QB)[:, None, None] + i_[None, :, None] - i_[None, None, :]
    nw = WINDOW // QB + 1
    dt = jnp.swapaxes(d, 1, 2)
    tsel = _bias_rows(rel_bias * LOG2E, dt, split_rows=True)
    return dict(
        bcmp=_bias_rows(rel_bias, t - cend),
        tselt=tsel.transpose(0, 1, 3, 2, 4).reshape(nqb, A_KV, QB, A_REP * QB),
        covt=_cover_t(ns_pad, nch),
        wmt=jnp.where((dt[:nw] >= 0) & (dt[:nw] < WINDOW), 0.0, NEG),
        cmt=jnp.where(dt[:2] >= 0, 0.0, NEG),
        gx=_gate_expand())


def _sample_tables(rel_bias, past, tp, t_real):
    SL = past + LANES
    nblk = past // CMP_STRIDE
    ns = -(-(past + t_real) // SEL_BLOCK)
    ns_pad = -(-ns // LANES) * LANES
    WK = WINDOW + LANES
    i_ = jnp.arange(tp)
    t = past + i_
    rows = lambda dist: _bias_rows(rel_bias, dist[None])[0]
    return dict(
        bcmp=rows(t[:, None] - (jnp.arange(nblk) * CMP_STRIDE + CMP_BLOCK - 1)[None, :]),
        bsel=rows(t[:, None] - jnp.arange(SL)[None, :]),
        bwin=rows(i_[:, None] + WINDOW - jnp.arange(WK)[None, :]),
        cov=_cover_t(ns_pad, nblk).T, emat=_expand_mat(ns_pad, SL), perm=_phase_perm())


def _proj_in_kernel(x_ref, g_ref, w_ref, o_ref, hn_ref):
    @pl.when(pl.program_id(1) == 0)
    def _():
        x = x_ref[...]
        ms = jnp.mean(x * x, axis=-1, keepdims=True)
        hn_ref[...] = (x * lax.rsqrt(ms + EPS) * g_ref[...]).astype(BF16)

    o_ref[...] = jnp.dot(hn_ref[...], w_ref[...], preferred_element_type=F32).astype(o_ref.dtype)


def _proj_in(x2, norm_g, w_perm, out_dtype):
    n = x2.shape[0]
    tm = min(1024, n)
    tn = 1920
    return pl.pallas_call(
        _proj_in_kernel,
        grid=(n // tm, N_PERM // tn),
        in_specs=[pl.BlockSpec((tm, D_MODEL), lambda i, j: (i, 0)),
                  pl.BlockSpec((1, D_MODEL), lambda i, j: (0, 0)),
                  pl.BlockSpec((D_MODEL, tn), lambda i, j: (0, j))],
        out_specs=pl.BlockSpec((tm, tn), lambda i, j: (i, j)),
        out_shape=jax.ShapeDtypeStruct((n, N_PERM), out_dtype),
        scratch_shapes=[pltpu.VMEM((tm, D_MODEL), BF16)],
        compiler_params=_cparams(("parallel", "arbitrary")),
        name="proj_in",
    )(x2, norm_g.reshape(1, D_MODEL), w_perm)


def _prep_a_kernel(q_ref, kv0_ref, kv1_ref, kv2_ref, bdq_ref, bdk_ref, qg_ref, kg_ref, qo_ref, ro_ref, *rt_ref):
    q = q_ref[...].astype(F32)
    ms = jnp.dot(q * q, bdq_ref[...], precision=HI, preferred_element_type=F32) * (1.0 / A_HD)
    qo_ref[...] = q * lax.rsqrt(ms + EPS) * qg_ref[...] * (A_HD ** -0.5)
    for b, kv_ref in enumerate((kv0_ref, kv1_ref, kv2_ref)):
        k = kv_ref[:, 0:128].astype(F32)
        ms = jnp.dot(k * k, bdk_ref[...], precision=HI, preferred_element_type=F32) * (1.0 / A_HD)
        kn = k * lax.rsqrt(ms + EPS) * kg_ref[b:b + 1, :]
        v = kv_ref[:, 128:256].astype(F32)
        ro_ref[2 * b] = kn
        ro_ref[2 * b + 1] = v
        if rt_ref:
            rt_ref[b][0:128, :] = kn.T
            rt_ref[b][128:256, :] = v.T


def _prep_a(u2, a_qn, a_kn, seq=None):
    n = u2.shape[0]
    tm = min(512, n)
    out_specs = [pl.BlockSpec((tm, W_A), lambda i: (i, 0)), pl.BlockSpec((6, tm, 128), lambda i: (0, i, 0))]
    out_shape = [jax.ShapeDtypeStruct((n, W_A), F32), jax.ShapeDtypeStruct((6, n, 128), F32)]
    if seq is not None:
        per = seq // tm
        out_specs += [pl.BlockSpec((None, 256, tm), lambda i: (i // per, 0, i % per))] * 3
        out_shape += [jax.ShapeDtypeStruct((n // seq, 256, seq), F32)] * 3
    bd = lambda w: (jnp.arange(w)[:, None] // A_HD == jnp.arange(w)[None, :] // A_HD).astype(F32)
    qg = jnp.tile(a_qn, A_HEADS).reshape(1, W_A)
    kg = jnp.tile(a_kn, (1, A_KV))
    const = lambda shape: pl.BlockSpec(shape, lambda i: (0, 0))
    kvs = lambda b: pl.BlockSpec((tm, 256), lambda i: (i, OFF['a_kv'] // 256 + b))
    return pl.pallas_call(
        _prep_a_kernel,
        grid=(n // tm,),
        in_specs=[pl.BlockSpec((tm, W_A), lambda i: (i, OFF['a_q'] // W_A)), kvs(0), kvs(1), kvs(2),
                  const((W_A, W_A)), const((128, 128)), const((1, W_A)), const((3, 128))],
        out_specs=out_specs,
        out_shape=out_shape,
        compiler_params=_cparams(("parallel",)),
        name="prep_a",
    )(u2, u2, u2, u2, bd(W_A), bd(128), qg, kg)


def _stack_heads(q, g):
    return jnp.concatenate([q[:, (A_REP * g + r) * A_HD:(A_REP * g + r + 1) * A_HD] for r in range(A_REP)], axis=0)


def _rep_rows(x):
    return jnp.concatenate([x] * A_REP, axis=0)


def _masked_softmax(s, valid):
    sm = jnp.where(valid, s, NEG)
    m = jnp.max(sm, axis=-1, keepdims=True)
    e = jnp.where(valid, jnp.exp(sm - m), 0.0)
    l = jnp.sum(e, axis=-1, keepdims=True)
    return e / jnp.maximum(l, TINY)


def _select_blocks(pc_sum, covt_ref, st_ref, t_row, ns, n_top):
    ns_pad = covt_ref.shape[0]
    imp = lax.dot_general(covt_ref[...], pc_sum, NT, precision=HI, preferred_element_type=F32)
    jj = lax.broadcasted_iota(jnp.int32, (ns_pad, LANES), 0)
    tt = jnp.broadcast_to(t_row, (ns_pad, LANES))
    cur = tt // SEL_BLOCK
    forced = (jj == 0) | (jj == cur) | (jj == cur - 1)
    future = jj * SEL_BLOCK > tt
    score = jnp.where(future, NEG, jnp.where(forced, -NEG, imp))
    score = jnp.where(jj < ns, score, -jnp.inf)
    st_ref[...] = score

    def beats(k, rank):
        row = jnp.broadcast_to(st_ref[pl.ds(k, 1), :], (ns_pad, LANES))
        b = (row > score) | ((row == score) & (k < jj))
        return rank + jnp.where(b, 1.0, 0.0)

    rank = jnp.zeros((ns_pad, LANES), F32)
    if ns <= 32:
        for k in range(ns):
            rank = beats(k, rank)
    else:
        rank = lax.fori_loop(0, ns, beats, rank)
    return jnp.where(rank < n_top, 1.0, 0.0)


def _pad_rows(x, rows):
    if x.shape[0] == rows:
        return x
    return jnp.concatenate([x, jnp.zeros((rows - x.shape[0], x.shape[1]), x.dtype)], axis=0)


def _select_rows(pc_sum, cov_ref, sel_ref, t0, ns, n_top, nq):
    Q = pc_sum.shape[0]
    ns_pad = cov_ref.shape[1]
    nk = -(-ns // 8) * 8
    imp = jnp.dot(pc_sum, cov_ref[...], precision=HI, preferred_element_type=F32)
    jj = lax.broadcasted_iota(jnp.int32, (Q, ns_pad), 1)
    tt = lax.broadcasted_iota(jnp.int32, (Q, ns_pad), 0) + t0
    cur = tt // SEL_BLOCK
    forced = (jj == 0) | (jj == cur) | (jj == cur - 1)
    future = jj * SEL_BLOCK > tt
    score = jnp.where(future, NEG, jnp.where(forced, -NEG, imp))
    score = jnp.where(jj < ns, score, -jnp.inf)
    score_col = _pad_rows(score, LANES).T
    kk = lax.broadcasted_iota(jnp.int32, (nk, ns_pad), 0)
    jl = lax.broadcasted_iota(jnp.int32, (nk, ns_pad), 1)
    sel_ref[...] = jnp.zeros(sel_ref.shape, F32)
    for i in range(nq):
        col = jnp.broadcast_to(score_col[0:nk, i:i + 1], (nk, ns_pad))
        row = jnp.broadcast_to(score[i:i + 1, :], (nk, ns_pad))
        beats = (col > row) | ((col == row) & (kk < jl))
        rank = jnp.sum(jnp.where(beats, 1.0, 0.0), axis=0, keepdims=True)
        sel_ref[i:i + 1, :] = jnp.where(rank < n_top, 1.0, 0.0)
    return sel_ref[...]


def _compress(src_ref, nrow, wk_ref, wv_ref, pbuf_ref, kc_ref, vc_ref, by_phase=False):
    acck = jnp.zeros((nrow, 256), F32)
    accv = jnp.zeros((nrow, 256), F32)
    for j in range(CMP_STRIDE):
        if by_phase:
            xk = src_ref[0, j, 0:nrow, :].astype(BF16)
            xv = src_ref[1, j, 0:nrow, :].astype(BF16)
        else:
            xk = src_ref[0, pl.ds(j, nrow, stride=CMP_STRIDE), :].astype(BF16)
            xv = src_ref[1, pl.ds(j, nrow, stride=CMP_STRIDE), :].astype(BF16)
        acck = acck + jnp.dot(xk, wk_ref[j], preferred_element_type=F32)
        accv = accv + jnp.dot(xv, wv_ref[j], preferred_element_type=F32)
    nout = kc_ref.shape[0]
    for acc, dst in ((acck, kc_ref), (accv, vc_ref)):
        pbuf_ref[0:nrow, :] = acc[:, 128:256]
        dst[...] = acc[0:nout, 0:128] + pbuf_ref[pl.ds(1, nout), :]


def _gated_sum(gates, gx_ref, ocat_ref):
    gh, gl = _split(gates)
    gexp = (jnp.dot(gh, gx_ref[...], preferred_element_type=F32)
            + jnp.dot(gl, gx_ref[...], preferred_element_type=F32))
    out = gexp[:, 0:W_A] * ocat_ref[0]
    for c in range(1, N_BRANCH):
        out = out + gexp[:, c * W_A:(c + 1) * W_A] * ocat_ref[c]
    return out


def _nsa_prompt_kernel(q_ref, rows_ref, small_ref, wk_ref, wv_ref, bcmp_ref, tselt_ref, covt_ref,
                       wmt_ref, cmt_ref, gx_ref,
                       o_ref, kc_ref, vc_ref, pbuf_ref, mb_ref, acc_ref, m_ref, l_ref, st_ref, ocat_ref, ocatt_ref,
                       *, ns, n_top):
    bi = pl.program_id(1)
    T = rows_ref.shape[1]
    nch = T // CMP_STRIDE
    QB = Q_BLOCK
    SEL, WIN = 1, 2

    @pl.when(bi == 0)
    def _():
        pbuf_ref[...] = jnp.zeros(pbuf_ref.shape, F32)
        _compress(rows_ref, nch, wk_ref, wv_ref, pbuf_ref, kc_ref, vc_ref)

    t0 = bi * QB
    q = q_ref[0]
    tc = lax.broadcasted_iota(jnp.int32, (QB, nch), 0) + t0
    cend = lax.broadcasted_iota(jnp.int32, (QB, nch), 1) * CMP_STRIDE + (CMP_BLOCK - 1)
    cvalid = _rep_rows(tc - cend >= 0)
    t_row = lax.broadcasted_iota(jnp.int32, (1, LANES), 1) + t0
    qg = [_stack_heads(q, g).astype(BF16) for g in range(A_KV)]
    qt = (q * LOG2E).T
    qgt = [jnp.concatenate([qt[(A_REP * g + r) * A_HD:(A_REP * g + r + 1) * A_HD] for r in range(A_REP)],
                           axis=1).astype(BF16) for g in range(A_KV)]

    for g in range(A_KV):
        kc = kc_ref[:, g * A_HD:(g + 1) * A_HD].astype(BF16)
        vc = vc_ref[:, g * A_HD:(g + 1) * A_HD].astype(BF16)
        s = lax.dot_general(qg[g], kc, NT, preferred_element_type=F32) + bcmp_ref[0, g]
        p_c = _masked_softmax(s, cvalid)
        o_c = jnp.dot(p_c.astype(BF16), vc, preferred_element_type=F32)
        for r in range(A_REP):
            h = A_REP * g + r
            ocat_ref[0, :, h * A_HD:(h + 1) * A_HD] = o_c[r * QB:(r + 1) * QB]
        pc_sum = p_c[0:QB] + p_c[QB:2 * QB] + p_c[2 * QB:3 * QB] + p_c[3 * QB:4 * QB]
        sel_t = _select_blocks(pc_sum, covt_ref, st_ref, t_row, ns, n_top)
        mb_ref[g] = (sel_t - 1.0) * (-NEG)

    m_ref[...] = jnp.full(m_ref.shape, NEG, F32)
    l_ref[...] = jnp.zeros(l_ref.shape, F32)
    acc_ref[...] = jnp.zeros(acc_ref.shape, F32)
    lanes4 = lambda x: jnp.concatenate([x] * A_REP, axis=1)
    half = SEL_BLOCK

    def tiles(specs):
        cat = lambda xs: xs[0] if len(xs) == 1 else jnp.concatenate(xs, axis=0)
        cis = [2 * (br - 1) + g for br, g, _, _ in specs]
        sms, vs = [], []
        for br, g, kb, nb in specs:
            ks, vv, bs, ms = [], [], [], []
            for j in range(nb):
                off = pl.multiple_of((kb + j) * QB, QB)
                ks.append(rows_ref[2 * br, pl.ds(off, QB), g * A_HD:(g + 1) * A_HD])
                vv.append(rows_ref[2 * br + 1, pl.ds(off, QB), g * A_HD:(g + 1) * A_HD])
                bs.append(tselt_ref[bi - kb - j, g])
                if br == SEL:
                    blk = (QB // half) * (kb + j)
                    mt = jnp.concatenate([jnp.broadcast_to(mb_ref[g, pl.ds(blk + a, 1), :], (half, QB))
                                          for a in range(QB // half)], axis=0)
                    ms.append(mt + cmt_ref[jnp.minimum(bi - kb, 1)] if nb == 1 else mt)
                else:
                    ms.append(wmt_ref[bi - kb - j])
            vs.append(cat(vv).astype(BF16))
            sms.append(jnp.dot(cat(ks).astype(BF16), qgt[g], preferred_element_type=F32)
                       + cat(bs) + lanes4(cat(ms)))
        m_prevs = [m_ref[ci] for ci in cis]
        m_news = [jnp.maximum(mp, jnp.max(sm, axis=0, keepdims=True)) for mp, sm in zip(m_prevs, sms)]
        ps = [jnp.exp2(sm - mn) for sm, mn in zip(sms, m_news)]
        sums = [jnp.sum(p, axis=0, keepdims=True) for p in ps]
        pvs = [lax.dot_general(v, p.astype(BF16), TN, preferred_element_type=F32) for v, p in zip(vs, ps)]
        for ci, mp, mn, sm_, pv in zip(cis, m_prevs, m_news, sums, pvs):
            alpha = jnp.exp2(mp - mn)
            l_ref[ci] = alpha * l_ref[ci] + sm_
            acc_ref[ci] = alpha * acc_ref[ci] + pv
            m_ref[ci] = mn

    def body_sel(kb, carry):
        tiles([(SEL, g, kb, 1) for g in range(A_KV)])
        return carry

    def body_sel2(pair, carry):
        tiles([(SEL, g, 2 * pair, 2) for g in range(A_KV)])
        return carry

    def body_both(kb, carry):
        tiles([(br, g, kb, 1) for g in range(A_KV) for br in (SEL, WIN)])
        return carry

    lo = jnp.maximum(bi - WINDOW // QB, 0)
    lax.fori_loop(0, lo // 2, body_sel2, 0)
    lax.fori_loop(2 * (lo // 2), lo, body_sel, 0)
    lax.fori_loop(lo, bi + 1, body_both, 0)

    for br in (SEL, WIN):
        for g in range(A_KV):
            ci = 2 * (br - 1) + g
            o_t = acc_ref[ci] / jnp.maximum(l_ref[ci], TINY)
            for r in range(A_REP):
                h = A_REP * g + r
                ocatt_ref[br - 1, h * A_HD:(h + 1) * A_HD, :] = o_t[:, r * QB:(r + 1) * QB]
        ocat_ref[br] = ocatt_ref[br - 1].T

    o_ref[0] = _gated_sum(jax.nn.sigmoid(small_ref[0].astype(F32)), gx_ref, ocat_ref)


def _nsa_prompt(qn3, rows6, u3, wk, wv, tb):
    B, T, _ = qn3.shape
    QB = Q_BLOCK
    nqb = T // QB
    nch = T // CMP_STRIDE
    ns = T // SEL_BLOCK
    ns_pad = tb['covt'].shape[0]
    n_top = min(TOP_N, ns)
    nw = tb['wmt'].shape[0]
    kern = functools.partial(_nsa_prompt_kernel, ns=ns, n_top=n_top)
    c2 = lambda shape: pl.BlockSpec(shape, lambda b, i: (0,) * len(shape))
    return pl.pallas_call(
        kern,
        grid=(B, nqb),
        in_specs=[pl.BlockSpec((1, QB, W_A), lambda b, i: (b, i, 0)),
                  pl.BlockSpec((6, T, 128), lambda b, i: (0, b, 0)),
                  pl.BlockSpec((1, QB, LANES), lambda b, i: (b, i, OFF['small'] // LANES)),
                  c2((CMP_STRIDE, 128, 256)), c2((CMP_STRIDE, 128, 256)),
                  pl.BlockSpec((1, A_KV, A_REP * QB, nch), lambda b, i: (i, 0, 0, 0)),
                  c2((nqb, A_KV, QB, A_REP * QB)),
                  c2((ns_pad, nch)), c2((nw, QB, QB)), c2((2, QB, QB)),
                  c2((LANES, N_BRANCH * W_A))],
        out_specs=pl.BlockSpec((1, QB, W_A), lambda b, i: (b, i, 0)),
        out_shape=jax.ShapeDtypeStruct((B, T, W_A), F32),
        scratch_shapes=[pltpu.VMEM((nch, 128), F32), pltpu.VMEM((nch, 128), F32),
                        pltpu.VMEM((nch + 8, 128), F32),
                        pltpu.VMEM((A_KV, ns_pad, QB), F32),
                        pltpu.VMEM((2 * A_KV, A_HD, A_REP * QB), F32),
                        pltpu.VMEM((2 * A_KV, 1, A_REP * QB), F32),
                        pltpu.VMEM((2 * A_KV, 1, A_REP * QB), F32),
                        pltpu.VMEM((ns_pad, LANES), F32),
                        pltpu.VMEM((N_BRANCH, QB, W_A), F32),
                        pltpu.VMEM((2, W_A, QB), F32)],
        compiler_params=_cparams(("parallel", "arbitrary")),
        name="nsa_prompt",
    )(qn3, rows6, u3, wk, wv, tb['bcmp'], tb['tselt'], tb['covt'], tb['wmt'], tb['cmt'], tb['gx'])


def _nsa_sample_kernel(pt_ref, *refs, pg, past, tp, t_real, ns, n_top):
    cmp_refs = refs[0:pg]
    sel_refs = refs[pg:2 * pg]
    (q_ref, rows_ref, small_ref, win_ref, wk_ref, wv_ref, bcmp_ref, bsel_ref, bwin_ref, cov_ref, e_ref, perm_ref,
     o_ref, cslab, sslab, kc_ref, vc_ref, pbuf_ref, st_ref, s_ref) = refs[2 * pg:]
    p = pl.program_id(1)
    npg = pl.num_programs(1)
    n_pages = past // PAGE_SIZE
    SL = (n_pages + 1) * PAGE_SIZE
    nblk = kc_ref.shape[0]
    WK = WINDOW + LANES
    per_page = PAGE_SIZE // CMP_STRIDE

    for k in range(pg):
        page = p * pg + k
        row0 = pl.multiple_of(page * per_page, per_page)
        ordered = lax.dot_general(perm_ref[...], cmp_refs[k][...].astype(BF16), NT,
                                  preferred_element_type=F32)
        for kv in range(2):
            for j in range(CMP_STRIDE):
                cslab[kv, j, pl.ds(row0, per_page), :] = ordered[j * per_page:(j + 1) * per_page,
                                                                 kv * 128:(kv + 1) * 128]
            sslab[kv, page] = sel_refs[k][kv * 128:(kv + 1) * 128, :]

    @pl.when(p == npg - 1)
    def _():
        new_t = [_pad_rows(rows_ref[i], LANES).T for i in range(2, 6)]
        for kv in range(2):
            cslab[kv, :, nblk:nblk + 16, :] = jnp.zeros((CMP_STRIDE, 16, 128), F32)
            for i in range(tp):
                cslab[kv, i, nblk:nblk + 1, :] = rows_ref[kv, i:i + 1, :]
            sslab[kv, n_pages] = new_t[kv]
        pbuf_ref[...] = jnp.zeros(pbuf_ref.shape, F32)
        _compress(cslab, nblk + 8, wk_ref, wv_ref, pbuf_ref, kc_ref, vc_ref, by_phase=True)

        q = q_ref[0]
        gates = jax.nn.sigmoid(small_ref[0].astype(F32))
        R = A_REP * tp
        zq = jnp.zeros((R, A_HD), F32)
        q2 = jnp.concatenate([jnp.concatenate([_stack_heads(q, 0), zq], axis=1),
                              jnp.concatenate([zq, _stack_heads(q, 1)], axis=1)], axis=0).astype(BF16)
        rep2 = lambda x: jnp.concatenate([x] * (A_KV * A_REP), axis=0)
        ti = lax.broadcasted_iota(jnp.int32, (tp, SL), 0) + past
        causal = ti - lax.broadcasted_iota(jnp.int32, (tp, SL), 1) >= 0
        tc = lax.broadcasted_iota(jnp.int32, (tp, nblk), 0) + past
        cend = lax.broadcasted_iota(jnp.int32, (tp, nblk), 1) * CMP_STRIDE + (CMP_BLOCK - 1)
        wd = (lax.broadcasted_iota(jnp.int32, (tp, WK), 0) + WINDOW
              - lax.broadcasted_iota(jnp.int32, (tp, WK), 1))
        npt = n_pages + 1
        nck = 5 if npt % 5 == 0 else 1
        cpt = npt // nck
        ck = cpt * LANES

        s = lax.dot_general(q2, kc_ref[...].astype(BF16), NT, preferred_element_type=F32) + bcmp_ref[...]
        p_c = _masked_softmax(s, rep2(tc - cend >= 0))
        o_c = jnp.dot(p_c.astype(BF16), vc_ref[...].astype(BF16), preferred_element_type=F32)

        kmasks = []
        for g in range(A_KV):
            pg_ = p_c[g * R:(g + 1) * R]
            pc_sum = pg_[0:tp] + pg_[tp:2 * tp] + pg_[2 * tp:3 * tp] + pg_[3 * tp:4 * tp]
            sel = _select_rows(pc_sum, cov_ref, st_ref, past, ns, n_top, t_real).astype(BF16)
            km = jnp.dot(sel, e_ref[...], preferred_element_type=F32)
            kmasks.append(_rep_rows(causal & (km > 0.5)))
        svalid = jnp.concatenate(kmasks, axis=0)

        def sel_t(kv, c):
            return jnp.concatenate([sslab[kv, t] for t in range(c * cpt, (c + 1) * cpt)], axis=1).astype(BF16)

        for c in range(nck):
            s_ref[:, c * ck:(c + 1) * ck] = jnp.dot(q2, sel_t(0, c), preferred_element_type=F32)
        s_ref[...] = _masked_softmax(s_ref[...] + bsel_ref[...], svalid)
        o_s = jnp.zeros((A_KV * R, 2 * A_HD), F32)
        for c in range(nck):
            o_s = o_s + lax.dot_general(s_ref[:, c * ck:(c + 1) * ck].astype(BF16), sel_t(1, c), NT,
                                        preferred_element_type=F32)

        kw = jnp.concatenate([win_ref[0:128, :], new_t[2]], axis=1).astype(BF16)
        vw = jnp.concatenate([win_ref[128:256, :], new_t[3]], axis=1).astype(BF16)
        sw = jnp.dot(q2, kw, preferred_element_type=F32) + bwin_ref[...]
        p_w = _masked_softmax(sw, rep2((wd >= 0) & (wd < WINDOW)))
        o_w = lax.dot_general(p_w.astype(BF16), vw, NT, preferred_element_type=F32)

        for g in range(A_KV):
            gs = slice(g * A_HD, (g + 1) * A_HD)
            for r in range(A_REP):
                h = A_REP * g + r
                rs = slice(g * R + r * tp, g * R + (r + 1) * tp)
                out = (gates[:, 3 * h:3 * h + 1] * o_c[rs, gs] + gates[:, 3 * h + 1:3 * h + 2] * o_s[rs, gs]
                       + gates[:, 3 * h + 2:3 * h + 3] * o_w[rs, gs])
                o_ref[0, :, h * A_HD:(h + 1) * A_HD] = out


def _nsa_sample(qn3, rows6, u3, pool_cmp, pool_sel, win_t, layer, page_table, wk, wv, tb, past, t_real):
    B, tp, _ = qn3.shape
    n_pages = past // PAGE_SIZE
    pg = 16 if n_pages % 16 == 0 else n_pages
    npg = n_pages // pg
    SL = past + LANES
    nblk = past // CMP_STRIDE
    ns = -(-(past + t_real) // SEL_BLOCK)
    ns_pad = tb['cov'].shape[1]
    n_top = min(TOP_N, ns)
    WK = WINDOW + LANES
    kern = functools.partial(_nsa_sample_kernel, pg=pg, past=past, tp=tp, t_real=t_real, ns=ns, n_top=n_top)

    def page_spec(k):
        return pl.BlockSpec((None, None, 256, PAGE_SIZE), lambda b, p, pt: (layer, pt[b, p * pg + k], 0, 0))

    def c_(shape):
        return pl.BlockSpec(shape, lambda b, p, pt: (0,) * len(shape))

    R = A_REP * tp
    grid_spec = pltpu.PrefetchScalarGridSpec(
        num_scalar_prefetch=1,
        grid=(B, npg),
        in_specs=([page_spec(k) for k in range(pg)] + [page_spec(k) for k in range(pg)]
                  + [pl.BlockSpec((1, tp, W_A), lambda b, p, pt: (b, 0, 0)),
                     pl.BlockSpec((6, tp, 128), lambda b, p, pt: (0, b, 0)),
                     pl.BlockSpec((1, tp, LANES), lambda b, p, pt: (b, 0, OFF['small'] // LANES)),
                     pl.BlockSpec((None, None, 256, WINDOW), lambda b, p, pt: (layer, b, 0, 0)),
                     c_((CMP_STRIDE, 128, 256)), c_((CMP_STRIDE, 128, 256)),
                     c_((A_KV * R, nblk)), c_((A_KV * R, SL)), c_((A_KV * R, WK)),
                     c_((nblk, ns_pad)), c_((ns_pad, SL)), c_((PAGE_SIZE, PAGE_SIZE))]),
        out_specs=pl.BlockSpec((1, tp, W_A), lambda b, p, pt: (b, 0, 0)),
        scratch_shapes=[pltpu.VMEM((2, CMP_STRIDE, nblk + 16, 128), F32),
                        pltpu.VMEM((2, n_pages + 1, 128, PAGE_SIZE), F32),
                        pltpu.VMEM((nblk, 128), F32), pltpu.VMEM((nblk, 128), F32),
                        pltpu.VMEM((nblk + 16, 128), F32),
                        pltpu.VMEM((tp, ns_pad), F32), pltpu.VMEM((A_KV * R, SL), F32)],
    )
    return pl.pallas_call(
        kern,
        grid_spec=grid_spec,
        out_shape=jax.ShapeDtypeStruct((B, tp, W_A), F32),
        compiler_params=_cparams(("parallel", "arbitrary")),
        name="nsa_sample",
    )(page_table, *([pool_cmp] * pg), *([pool_sel] * pg), qn3, rows6, u3, win_t, wk, wv,
      *(tb[k].reshape(A_KV * R, -1) for k in ('bcmp', 'bsel', 'bwin')), tb['cov'], tb['emat'], tb['perm'])


def _tri(L, lower_incl):
    r = lax.broadcasted_iota(jnp.int32, (L, L), 0)
    c = lax.broadcasted_iota(jnp.int32, (L, L), 1)
    return (r >= c) if lower_incl else (r > c)


def _mlstm_kernel(x_ref, small_ref, smallt_ref, brow_ref, bcol_ref, c0_ref, n0_ref, m0_ref,
                  h_ref, co_ref, no_ref, mo_ref, c_s, n_s, m_s, *, L, t_real):
    c = pl.program_id(1)
    nc = pl.num_programs(1)

    @pl.when(c == 0)
    def _():
        c_s[...] = c0_ref[0]
        n_s[...] = n0_ref[0]
        m_s[...] = m0_ref[0]

    sm = small_ref[0].astype(F32)
    smt = smallt_ref[0, 0]
    li_col = sm[:, SM_MIF:SM_MIF + 4] + brow_ref[0:1, 0:4]
    lf_col = _log_sigmoid(sm[:, SM_MIF + 4:SM_MIF + 8] + brow_ref[0:1, 4:8])
    li_row = smt[0:4, :] + bcol_ref[0:4, :]
    lf_row = _log_sigmoid(smt[4:8, :] + bcol_ref[4:8, :])
    if t_real % L:
        tcol = lax.broadcasted_iota(jnp.int32, (L, 4), 0) + c * L
        trow = lax.broadcasted_iota(jnp.int32, (4, L), 1) + c * L
        li_col = jnp.where(tcol < t_real, li_col, NEG)
        lf_col = jnp.where(tcol < t_real, lf_col, 0.0)
        li_row = jnp.where(trow < t_real, li_row, NEG)
        lf_row = jnp.where(trow < t_real, lf_row, 0.0)
    low = _tri(L, True)
    b_col = jnp.dot(low.astype(F32), lf_col, precision=HI, preferred_element_type=F32)
    b_row = lax.dot_general(lf_row, low.astype(F32), NT, precision=HI, preferred_element_type=F32)

    hd = range(M_HEADS)
    q = [x_ref[0, :, h * M_HD:(h + 1) * M_HD].astype(F32) for h in hd]
    k = [x_ref[0, :, W_M + h * M_HD:W_M + (h + 1) * M_HD].astype(F32) * (M_HD ** -0.5) for h in hd]
    v = [x_ref[0, :, 2 * W_M + h * M_HD:2 * W_M + (h + 1) * M_HD].astype(F32) for h in hd]
    bc = [b_col[:, h:h + 1] for h in hd]
    Dm = [jnp.where(low, bc[h] - b_row[h:h + 1, :] + li_row[h:h + 1, :], NEG) for h in hd]
    m_prev = [m_s[h][:, 0:1] for h in hd]
    a = [bc[h] + m_prev[h] for h in hd]
    mt = [jnp.maximum(a[h], jnp.max(Dm[h], axis=-1, keepdims=True)) for h in hd]
    qk = [lax.dot_general(q[h], k[h], NT, preferred_element_type=F32) for h in hd]
    S = [qk[h] * jnp.exp(Dm[h] - mt[h]) for h in hd]
    inter = [jnp.exp(a[h] - mt[h]) for h in hd]
    C = [c_s[h] for h in hd]
    n = [n_s[h] for h in hd]
    qC = [jnp.dot(q[h], C[h], preferred_element_type=F32) for h in hd]
    Sv = [jnp.dot(S[h], v[h], preferred_element_type=F32) for h in hd]
    den = [inter[h] * jnp.sum(q[h] * n[h], axis=-1, keepdims=True) + jnp.sum(S[h], axis=-1, keepdims=True)
           for h in hd]
    for h in hd:
        h_ref[0, :, h * M_HD:(h + 1) * M_HD] = ((inter[h] * qC[h] + Sv[h])
                                                / jnp.maximum(jnp.abs(den[h]), jnp.exp(-mt[h])))
    bL = [bc[h][L - 1:L, :] for h in hd]
    wlog = [bL[h] - bc[h] + li_col[:, h:h + 1] for h in hd]
    m_new = [jnp.maximum(bL[h] + m_prev[h], jnp.max(wlog[h], axis=0, keepdims=True)) for h in hd]
    dec = [jnp.exp(bL[h] + m_prev[h] - m_new[h]) for h in hd]
    kw = [k[h] * jnp.exp(wlog[h] - m_new[h]) for h in hd]
    kv = [lax.dot_general(kw[h], v[h], TN, preferred_element_type=F32) for h in hd]
    for h in hd:
        c_s[h] = dec[h] * C[h] + kv[h]
        n_s[h] = dec[h] * n[h] + jnp.sum(kw[h], axis=0, keepdims=True)
        m_s[h] = jnp.broadcast_to(m_new[h], (1, LANES))

    @pl.when(c == nc - 1)
    def _():
        co_ref[0] = c_s[...]
        no_ref[0] = n_s[...]
        mo_ref[0] = m_s[...]


def _mlstm(u3, smallt, m_bi, m_bf, C0, n0, m0, L, t_real):
    B, T, _ = u3.shape
    nc = T // L
    brow = jnp.concatenate([m_bi, m_bf]).reshape(1, 8)
    bcol = jnp.concatenate([m_bi, m_bf]).reshape(8, 1)
    n0 = n0.reshape(B, M_HEADS, 1, M_HD)
    m0 = jnp.broadcast_to(m0[:, :, None, None], (B, M_HEADS, 1, LANES))
    kern = functools.partial(_mlstm_kernel, L=L, t_real=t_real)
    st = lambda shape: pl.BlockSpec(shape, lambda b, c: (b,) + (0,) * (len(shape) - 1))
    h, Cn, nn, mn = pl.pallas_call(
        kern,
        grid=(B, nc),
        in_specs=[pl.BlockSpec((1, L, 3 * W_M), lambda b, c: (b, c, OFF['m_qkv'] // (3 * W_M))),
                  pl.BlockSpec((1, L, LANES), lambda b, c: (b, c, OFF['small'] // LANES)),
                  pl.BlockSpec((1, 1, 16, L), lambda b, c: (b, c, 0, 0)),
                  pl.BlockSpec((1, 8), lambda b, c: (0, 0)), pl.BlockSpec((8, 1), lambda b, c: (0, 0)),
                  st((1, M_HEADS, M_HD, M_HD)), st((1, M_HEADS, 1, M_HD)), st((1, M_HEADS, 1, LANES))],
        out_specs=[pl.BlockSpec((1, L, W_M), lambda b, c: (b, c, 0)),
                   st((1, M_HEADS, M_HD, M_HD)), st((1, M_HEADS, 1, M_HD)), st((1, M_HEADS, 1, LANES))],
        out_shape=[jax.ShapeDtypeStruct((B, T, W_M), F32),
                   jax.ShapeDtypeStruct((B, M_HEADS, M_HD, M_HD), F32),
                   jax.ShapeDtypeStruct((B, M_HEADS, 1, M_HD), F32),
                   jax.ShapeDtypeStruct((B, M_HEADS, 1, LANES), F32)],
        scratch_shapes=[pltpu.VMEM((M_HEADS, M_HD, M_HD), F32), pltpu.VMEM((M_HEADS, 1, M_HD), F32),
                        pltpu.VMEM((M_HEADS, 1, LANES), F32)],
        compiler_params=_cparams(("parallel", "arbitrary")),
        name="mlstm",
    )(u3, u3, smallt, brow, bcol, C0, n0, m0)
    return h, Cn, nn.reshape(B, M_HEADS, M_HD), mn[:, :, 0, 0]


def _gdn_a_kernel(x_ref, small_ref, smallt_ref, cw_ref, prow_ref, pcol_ref, buf_ref,
                  u_ref, w_ref, qe_ref, kd_ref, at_ref, eg_ref, xbuf, *, L, CB, t_real):
    c = pl.program_id(1)
    PRE = 8
    R = CB * L

    @pl.when(c == 0)
    def _():
        xbuf[0:PRE, :] = buf_ref[0]

    xbuf[PRE:PRE + R, :] = x_ref[0].astype(F32)
    conv = jnp.zeros((R, 3 * W_G), F32)
    for j in range(CONV_W):
        conv = conv + xbuf[pl.ds(PRE - (CONV_W - 1) + j, R), :] * cw_ref[j:j + 1, :]
    xbuf[0:PRE, :] = xbuf[R:R + PRE, :]
    conv = _silu(conv)

    sm = small_ref[0].astype(F32)
    g_all = -jnp.exp(prow_ref[0:1, 0:4]) * _softplus(sm[:, SM_GAB:SM_GAB + 4] + prow_ref[1:2, 0:4])
    beta_all = jax.nn.sigmoid(sm[:, SM_GAB + 4:SM_GAB + 8])
    lowf = _tri(L, True).astype(F32)

    H = G_HEADS
    lane_h = lax.broadcasted_iota(jnp.int32, (L, H * L), 1) // L
    row_p = lax.broadcasted_iota(jnp.int32, (L, H * L), 0)
    col_p = lax.broadcasted_iota(jnp.int32, (L, H * L), 1) % L
    blockmask = (lax.broadcasted_iota(jnp.int32, (H * L, H * L), 0) // L
                 == lax.broadcasted_iota(jnp.int32, (H * L, H * L), 1) // L)

    def pack_diag(full):
        out = full[0:L]
        for h in range(1, H):
            out = jnp.where(lane_h == h, full[h * L:(h + 1) * L], out)
        return out

    def block_diag(m, parts):
        if L % 16:
            return _split(jnp.where(blockmask, jnp.concatenate([m] * H, axis=0), 0.0))
        return tuple(jnp.where(blockmask, jnp.concatenate([p] * H, axis=0), jnp.zeros((), BF16))
                     for p in parts)

    a_list, rhs_list = [], []
    for cb in range(CB):
        rs = slice(cb * L, (cb + 1) * L)
        smt = smallt_ref[0, cb]
        g_col = g_all[rs]
        beta_col = beta_all[rs]
        g_row = -jnp.exp(pcol_ref[0:4, 0:1]) * _softplus(smt[8:12, :] + pcol_ref[0:4, 1:2])
        if t_real % L:
            t0 = (c * CB + cb) * L
            tcol = lax.broadcasted_iota(jnp.int32, (L, 4), 0) + t0
            trow = lax.broadcasted_iota(jnp.int32, (4, L), 1) + t0
            g_col = jnp.where(tcol < t_real, g_col, 0.0)
            beta_col = jnp.where(tcol < t_real, beta_col, 0.0)
            g_row = jnp.where(trow < t_real, g_row, 0.0)
        G_col = jnp.dot(lowf, g_col, precision=HI, preferred_element_type=F32)
        G_row = lax.dot_general(g_row, lowf, NT, precision=HI, preferred_element_type=F32)

        qs, ks, kbs, rhss = [], [], [], []
        for h in range(G_HEADS):
            hs = slice(h * G_HD, (h + 1) * G_HD)
            cq = conv[rs, h * G_HD:(h + 1) * G_HD]
            ck = conv[rs, W_G + h * G_HD:W_G + (h + 1) * G_HD]
            v = conv[rs, 2 * W_G + h * G_HD:2 * W_G + (h + 1) * G_HD]
            q = cq * lax.rsqrt(jnp.sum(cq * cq, axis=-1, keepdims=True) + EPS) * (G_HD ** -0.5)
            k = ck * lax.rsqrt(jnp.sum(ck * ck, axis=-1, keepdims=True) + EPS)
            Gc = G_col[:, h:h + 1]
            bcol = beta_col[:, h:h + 1]
            kb = k * bcol
            eG = jnp.exp(Gc)
            GL = Gc[L - 1:L, :]
            qs.append(q)
            ks.append(k)
            kbs.append(kb)
            rhss.append(jnp.concatenate([v * bcol, kb * eG], axis=1))
            qe_ref[0, rs, hs] = q * eG
            kd_ref[0, rs, hs] = k * jnp.exp(GL - Gc)
            eg_ref[0, cb, h:h + 1, :] = jnp.broadcast_to(jnp.exp(GL), (1, LANES))
        eg_ref[0, cb, G_HEADS:8, :] = jnp.zeros((8 - G_HEADS, LANES), F32)

        Gc_p = jnp.concatenate([jnp.broadcast_to(G_col[:, h:h + 1], (L, L)) for h in range(H)], axis=1)
        Gr_p = jnp.concatenate([jnp.broadcast_to(G_row[h:h + 1, :], (L, L)) for h in range(H)], axis=1)
        low_p = row_p >= col_p
        dmask = jnp.where(low_p, jnp.exp(jnp.where(low_p, Gc_p - Gr_p, 0.0)), 0.0)
        k_st = _split(jnp.concatenate(ks, axis=0))
        A = pack_diag(_dot3(_split(jnp.concatenate(kbs, axis=0)), k_st, NT)) * jnp.where(row_p > col_p, dmask, 0.0)
        attn = pack_diag(lax.dot_general(jnp.concatenate(qs, axis=0).astype(BF16), k_st[0], NT,
                                         preferred_element_type=F32)) * dmask
        at_ref[0, rs, :] = attn
        a_list.append(A)
        rhs_list.append(_split(jnp.concatenate(rhss, axis=0)))

    chunks = range(CB)
    X = [jnp.where(row_p == col_p, 1.0, 0.0) - a_list[cb] for cb in chunks]
    As = [_split(a_list[cb]) for cb in chunks]
    Pw = [_dot3(As[cb], block_diag(a_list[cb], As[cb])) for cb in chunks]
    span = 2
    while span < L:
        Ps = [_split(Pw[cb]) for cb in chunks]
        Pbd = [block_diag(Pw[cb], Ps[cb]) for cb in chunks]
        X = [X[cb] + _dot3(_split(X[cb]), Pbd[cb]) for cb in chunks]
        span *= 2
        if span < L:
            Pw = [_dot3(Ps[cb], Pbd[cb]) for cb in chunks]
    sol = [_dot3(block_diag(X[cb], _split(X[cb])), rhs_list[cb]) for cb in chunks]
    for cb in chunks:
        for h in range(H):
            rs = slice(cb * L, (cb + 1) * L)
            hs = slice(h * G_HD, (h + 1) * G_HD)
            u_ref[0, rs, hs] = sol[cb][h * L:(h + 1) * L, 0:G_HD]
            w_ref[0, rs, hs] = sol[cb][h * L:(h + 1) * L, G_HD:2 * G_HD]


def _gdn_b_kernel(u_ref, w_ref, qe_ref, kd_ref, at_ref, eg_ref, s0_ref, o_ref, so_ref, s_s, *, L, BB):
    c = pl.program_id(1)
    nc = pl.num_programs(1)

    @pl.when(c == 0)
    def _():
        s_s[...] = s0_ref[...]

    ch = [(b, h, slice(h * G_HD, (h + 1) * G_HD)) for b in range(BB) for h in range(G_HEADS)]
    S = [s_s[b, h] for b, h, _ in ch]
    r = [jnp.dot(jnp.concatenate([w_ref[b, :, hs], qe_ref[b, :, hs]], axis=0), S[i], preferred_element_type=F32)
         for i, (b, h, hs) in enumerate(ch)]
    v_new = [u_ref[b, :, hs] - r[i][0:L] for i, (b, h, hs) in enumerate(ch)]
    av = [jnp.dot(at_ref[b, :, h * L:(h + 1) * L], v_new[i], preferred_element_type=F32)
          for i, (b, h, hs) in enumerate(ch)]
    kdv = [lax.dot_general(kd_ref[b, :, hs], v_new[i], TN, preferred_element_type=F32)
           for i, (b, h, hs) in enumerate(ch)]
    for i, (b, h, hs) in enumerate(ch):
        o_ref[b, :, hs] = r[i][L:2 * L] + av[i]
        s_s[b, h] = eg_ref[b, 0, h:h + 1, :] * S[i] + kdv[i]

    @pl.when(c == nc - 1)
    def _():
        so_ref[...] = s_s[...]


def _gdn(u3, smallt, g_conv, g_A_log, g_dt_bias, buf, S0, L, t_real):
    B, T, _ = u3.shape
    nc = T // L
    CB = 4 if nc % 4 == 0 else 1
    BB = 8 if B % 8 == 0 else B
    prow = jnp.stack([g_A_log, g_dt_bias])
    pcol = jnp.stack([g_A_log, g_dt_bias], axis=1)
    buf8 = jnp.concatenate([jnp.zeros((B, 8 - (CONV_W - 1), 3 * W_G), F32), buf], axis=1)
    st = lambda shape: pl.BlockSpec(shape, lambda b, c: (b,) + (0,) * (len(shape) - 1))
    cst = lambda shape: pl.BlockSpec(shape, lambda b, c: (0,) * len(shape))
    row = lambda nb, r, w: pl.BlockSpec((nb, r, w), lambda b, c: (b, c, 0))
    tok = lambda w: jax.ShapeDtypeStruct((B, T, w), F32)
    u, w, qe, kd, at, eg = pl.pallas_call(
        functools.partial(_gdn_a_kernel, L=L, CB=CB, t_real=t_real),
        grid=(B, nc // CB),
        in_specs=[pl.BlockSpec((1, CB * L, 3 * W_G), lambda b, c: (b, c, OFF['g_qkv'] // (3 * W_G))),
                  pl.BlockSpec((1, CB * L, LANES), lambda b, c: (b, c, OFF['small'] // LANES)),
                  pl.BlockSpec((1, CB, 16, L), lambda b, c: (b, c, 0, 0)),
                  cst((CONV_W, 3 * W_G)), cst((2, 4)), cst((4, 2)), st((1, 8, 3 * W_G))],
        out_specs=[row(1, CB * L, W_G)] * 4 + [row(1, CB * L, G_HEADS * L),
                                               pl.BlockSpec((1, CB, 8, LANES), lambda b, c: (b, c, 0, 0))],
        out_shape=[tok(W_G)] * 4 + [tok(G_HEADS * L), jax.ShapeDtypeStruct((B, nc, 8, LANES), F32)],
        scratch_shapes=[pltpu.VMEM((CB * L + 8, 3 * W_G), F32)],
        compiler_params=_cparams(("parallel", "arbitrary")),
        name="gdn_a",
    )(u3, u3, smallt, g_conv, prow, pcol, buf8)
    sblk = pl.BlockSpec((BB, G_HEADS, G_HD, G_HD), lambda b, c: (b, 0, 0, 0))
    return pl.pallas_call(
        functools.partial(_gdn_b_kernel, L=L, BB=BB),
        grid=(B // BB, nc),
        in_specs=[row(BB, L, W_G)] * 4 + [row(BB, L, G_HEADS * L),
                                          pl.BlockSpec((BB, 1, 8, LANES), lambda b, c: (b, c, 0, 0)), sblk],
        out_specs=[row(BB, L, W_G), sblk],
        out_shape=[tok(W_G), jax.ShapeDtypeStruct((B, G_HEADS, G_HD, G_HD), F32)],
        scratch_shapes=[pltpu.VMEM((BB, G_HEADS, G_HD, G_HD), F32)],
        compiler_params=_cparams(("parallel", "arbitrary")),
        name="gdn_b",
    )(u, w, qe, kd, at, eg, S0)


def _head_rmsnorm(x, gain_ref, nheads, hd):
    outs = []
    for h in range(nheads):
        xh = x[:, h * hd:(h + 1) * hd]
        ms = jnp.mean(xh * xh, axis=-1, keepdims=True)
        outs.append(xh * lax.rsqrt(ms + EPS) * gain_ref[...])
    return jnp.concatenate(outs, axis=1)


def _merge_kernel(x_ref, oa_ref, hm_ref, og_ref, az_ref, mo_ref, mz_ref, gz_ref, mg_ref,
                  wb_ref, wo_ref, mhn_ref, ghn_ref, y_ref):
    oa = oa_ref[...] * _silu(az_ref[...].astype(F32))
    om = (_head_rmsnorm(hm_ref[...], mhn_ref, M_HEADS, M_HD) * jax.nn.sigmoid(mo_ref[...].astype(F32))
          * _silu(mz_ref[...].astype(F32)))
    og = _head_rmsnorm(og_ref[...], ghn_ref, G_HEADS, G_HD) * _silu(gz_ref[...].astype(F32))
    y = jnp.zeros(y_ref.shape, F32)
    for i, br in enumerate((oa, om, og)):
        proj = jnp.dot(br.astype(BF16), wb_ref[i], preferred_element_type=F32)
        y = y + jax.nn.sigmoid(mg_ref[:, i * D_MODEL:(i + 1) * D_MODEL].astype(F32)) * proj
    y_ref[...] = x_ref[...] + jnp.dot(y.astype(BF16), wo_ref[...], preferred_element_type=F32)


def _merge_out(x2, o_a, h_m, o_g, u2, w_branch, w_out, m_hn, g_hn):
    n = x2.shape[0]
    tm = min(512, n)
    row = lambda w, off: pl.BlockSpec((tm, w), lambda i: (i, off // w))
    cst = lambda shape: pl.BlockSpec(shape, lambda i: (0,) * len(shape))
    return pl.pallas_call(
        _merge_kernel,
        grid=(n // tm,),
        in_specs=[row(D_MODEL, 0), row(W_A, 0), row(W_M, 0), row(W_G, 0),
                  row(W_A, OFF['a_z']), row(W_M, OFF['m_o']), row(W_M, OFF['m_z']), row(W_G, OFF['g_z']),
                  row(N_BRANCH * D_MODEL, OFF['merge']),
                  cst((N_BRANCH, W_A, D_MODEL)), cst((D_MODEL, D_MODEL)), cst((1, M_HD)), cst((1, G_HD))],
        out_specs=row(D_MODEL, 0),
        out_shape=jax.ShapeDtypeStruct((n, D_MODEL), F32),
        compiler_params=_cparams(("parallel",)),
        name="merge_out",
    )(x2, o_a, h_m, o_g, u2, u2, u2, u2, u2, w_branch.astype(BF16), w_out.astype(BF16),
      m_hn.reshape(1, M_HD), g_hn.reshape(1, G_HD))


def _small_t(u3, L):
    B, T, _ = u3.shape
    s = u3[:, :, OFF['small'] + SM_MIF:OFF['small'] + SM_MIF + 16].astype(F32)
    return s.reshape(B, T // L, L, 16).transpose(0, 1, 3, 2)


def _layer(x, lw, tb, past, q_off):
    B, T, _ = x.shape
    if past is None:
        tp, L = T, math.gcd(T, M_CHUNK)
    else:
        tp = -(-T // 8) * 8
        L = tp
        x = jnp.pad(x, ((0, 0), (0, tp - T), (0, 0)))
    x2 = x.reshape(B * tp, D_MODEL)
    u2 = _proj_in(x2, lw['norm_g'], lw['w_perm'], BF16 if past is None else F32)
    u3 = u2.reshape(B, tp, N_PERM)
    if past is None:
        C0 = jnp.zeros((B, M_HEADS, M_HD, M_HD), F32)
        n0 = jnp.zeros((B, M_HEADS, M_HD), F32)
        m0 = jnp.zeros((B, M_HEADS), F32)
        S0 = jnp.zeros((B, G_HEADS, G_HD, G_HD), F32)
        buf = jnp.zeros((B, CONV_W - 1, 3 * W_G), F32)
    else:
        C0, n0, m0, S0, buf = past['mC'], past['mn'], past['mm'], past['gS'], past['gconv']
    qn, rows6, *rows_t = _prep_a(u2, lw['a_qn'], lw['a_kn'], T if past is None and T % 512 == 0 else None)
    qn3 = qn.reshape(B, tp, W_A)
    if past is None:
        o_a = _nsa_prompt(qn3, rows6, u3, lw['wk'], lw['wv'], tb)
    else:
        o_a = _nsa_sample(qn3, rows6, u3, past['cmp'], past['sel'], past['win_t'], past['layer'], past['page_table'],
                          lw['wk'], lw['wv'], tb, q_off, T)
    smallt = _small_t(u3, L)
    h_m, Cn, nn, mn = _mlstm(u3, smallt, lw['m_bi'], lw['m_bf'], C0, n0, m0, L, T)
    o_g, Sn = _gdn(u3, smallt, lw['g_conv'], lw['g_A_log'], lw['g_dt_bias'], buf, S0, L, T)
    y2 = _merge_out(x2, o_a.reshape(B * tp, W_A), h_m.reshape(B * tp, W_M), o_g.reshape(B * tp, W_G), u2,
                    lw['w_branch'], lw['w_out'], lw['m_hn'], lw['g_hn'])
    if rows_t:
        new_cmp, new_sel, new_win = (rows_t[b].reshape(B, 2, A_KV, A_HD, T).transpose(0, 4, 1, 2, 3)
                                     for b in range(3))
    else:
        rows = rows6.reshape(3, 2, B, tp, A_KV, A_HD)[:, :, :, :T]
        new_cmp, new_sel, new_win = (jnp.moveaxis(rows[b], 0, 2) for b in range(3))
    g_qkv = u3[:, max(T - (CONV_W - 1), 0):T, OFF['g_qkv']:OFF['g_qkv'] + 3 * W_G].astype(F32)
    if past is None:
        win = new_win
    else:
        win = jnp.concatenate([past['win'], new_win], axis=1)
    full = jnp.concatenate([buf, g_qkv], axis=1)
    state = dict(cmp=new_cmp, sel=new_sel, win=win[:, -min(WINDOW, win.shape[1]):],
                 mC=Cn, mn=nn, mm=mn, gS=Sn, gconv=full[:, -(CONV_W - 1):])
    return y2.reshape(B, tp, D_MODEL)[:, :T], state


def kernel(x_prompt, x_sample, cache_cmp_kv, cache_sel_kv, cache_win_kv, state_mlstm_C, state_mlstm_n,
           state_mlstm_m, state_gdn_S, state_gdn_conv, page_table, norm_g, w_in, a_qn, a_kn, a_cmp_wk,
           a_cmp_wv, rel_bias, m_bi, m_bf, m_hn, g_conv, g_A_log, g_dt_bias, g_hn, w_branch, w_out):
    names = ('cmp', 'sel', 'win', 'mC', 'mn', 'mm', 'gS', 'gconv')
    st_p = {k: [] for k in names}
    st_s = {k: [] for k in names}
    past_len = page_table.shape[1] * PAGE_SIZE
    n_pool = cache_cmp_kv.shape[1]
    pos_minor = lambda c: jnp.swapaxes(c.reshape(c.shape[0], c.shape[1], c.shape[2], 2 * A_KV * A_HD), 2, 3)
    pool_cmp, pool_sel, win_t = pos_minor(cache_cmp_kv), pos_minor(cache_sel_kv), pos_minor(cache_win_kv)
    w_t = jnp.swapaxes(w_in, 1, 2)
    db, dt = x_sample.shape[0], x_sample.shape[1]
    tb_p = _prompt_tables(rel_bias, x_prompt.shape[1])
    tb_s = _sample_tables(rel_bias, past_len, -(-dt // 8) * 8, dt)
    y_p, y_s = x_prompt, x_sample
    for l in range(DEPTH):
        lw = dict(norm_g=norm_g[l], w_perm=_permute_w_in(w_t, l), a_qn=a_qn[l], a_kn=a_kn[l],
                  wk=_cmp_weights(a_cmp_wk[l]), wv=_cmp_weights(a_cmp_wv[l]),
                  m_bi=m_bi[l], m_bf=m_bf[l], m_hn=m_hn[l], g_conv=g_conv[l], g_A_log=g_A_log[l],
                  g_dt_bias=g_dt_bias[l], g_hn=g_hn[l], w_branch=w_branch[l], w_out=w_out[l])
        y_p, new_p = _layer(y_p, lw, tb_p, None, 0)
        past = dict(cmp=pool_cmp, sel=pool_sel, win_t=win_t, layer=l, win=cache_win_kv[l], page_table=page_table,
                    mC=state_mlstm_C[l], mn=state_mlstm_n[l], mm=state_mlstm_m[l],
                    gS=state_gdn_S[l], gconv=state_gdn_conv[l])
        y_s, new_s = _layer(y_s, lw, tb_s, past, past_len)
        for k in names:
            st_p[k].append(new_p[k])
            st_s[k].append(new_s[k])
    P = {k: jnp.stack(v) for k, v in st_p.items()}
    S = {k: jnp.stack(v) for k, v in st_s.items()}
    return (y_p, y_s, P['cmp'], S['cmp'], P['sel'], S['sel'], P['win'], S['win'],
            P['mC'], S['mC'], P['mn'], S['mn'], P['mm'], S['mm'], P['gS'], S['gS'], P['gconv'], S['gconv'])
```

```python
import functools
import math

import jax
import jax.numpy as jnp
from jax import lax
from jax.experimental import pallas as pl
from jax.experimental.pallas import tpu as pltpu

D_MODEL = 1024
DEPTH = 2
PAGE_SIZE = 128
A_HEADS = 8
A_KV = 2
A_REP = A_HEADS // A_KV
A_HD = 64
CMP_BLOCK = 32
CMP_STRIDE = 16
SEL_BLOCK = 64
TOP_N = 16
WINDOW = 512
Q_BLOCK = 128
N_BUCKETS = 32
MAX_DIST = 2048
M_HEADS = 4
M_HD = 128
M_CHUNK = 64
G_HEADS = 4
G_HD = 128
G_CHUNK = 64
CONV_W = 4
W_A = A_HEADS * A_HD
W_M = M_HEADS * M_HD
W_G = G_HEADS * G_HD
N_BRANCH = 3
EPS = 1e-6
NEG = -1e30
TINY = 1e-30
LOG2E = math.log2(math.e)

F32 = jnp.float32
BF16 = jnp.bfloat16
HI = lax.Precision.HIGHEST
NN = (((1,), (0,)), ((), ()))
NT = (((1,), (1,)), ((), ()))
TN = (((0,), (0,)), ((), ()))

LANES = 128
VMEM_LIMIT = 56 * 1024 * 1024

IN_ORDER = ('a_q', 'a_kv', 'a_gate', 'a_z', 'm_qkv', 'm_if', 'm_o', 'm_z', 'g_qkv', 'g_ab', 'g_z', 'merge')
IN_WIDTH = dict(a_q=W_A, a_kv=3 * 2 * A_KV * A_HD, a_gate=A_HEADS * 3, a_z=W_A, m_qkv=3 * W_M, m_if=2 * M_HEADS,
                m_o=W_M, m_z=W_M, g_qkv=3 * W_G, g_ab=2 * G_HEADS, g_z=W_G, merge=N_BRANCH * D_MODEL)
OFF = dict(merge=0, m_qkv=3072, g_qkv=4608, a_q=6144, a_z=6656, m_o=7168, m_z=7680, g_z=8192, a_kv=8704,
           small=9472)
N_PERM = 9600
SM_GATE, SM_MIF, SM_GAB = 0, 24, 32


def _cparams(sem):
    return pltpu.CompilerParams(dimension_semantics=sem, vmem_limit_bytes=VMEM_LIMIT)


def _silu(x):
    return x * jax.nn.sigmoid(x)


def _log_sigmoid(x):
    return jnp.minimum(x, 0.0) - jnp.log(1.0 + jnp.exp(-jnp.abs(x)))


def _softplus(x):
    return jnp.maximum(x, 0.0) + jnp.log(1.0 + jnp.exp(-jnp.abs(x)))


def _split(a):
    hi = a.astype(BF16)
    return hi, (a - hi.astype(F32)).astype(BF16)


def _dot3(a, b, dims=NN):
    mm = lambda x, y: lax.dot_general(x, y, dims, preferred_element_type=F32)
    return mm(a[0], b[0]) + mm(a[0], b[1]) + mm(a[1], b[0])


def _src_offsets():
    offs, off = {}, 0
    for name in IN_ORDER:
        offs[name] = off
        off += IN_WIDTH[name]
    return offs, off


def _permute_kernel(w_ref, o_ref):
    src, _ = _src_offsets()
    small = []
    for name in IN_ORDER:
        w = IN_WIDTH[name]
        rows = w_ref[src[name]:src[name] + w, :]
        if w % LANES:
            small.append(rows)
        else:
            o_ref[:, OFF[name]:OFF[name] + w] = rows.T.astype(BF16)
    used = sum(r.shape[0] for r in small)
    small.append(jnp.zeros((LANES - used, w_ref.shape[1]), F32))
    o_ref[:, OFF['small']:N_PERM] = jnp.concatenate(small, axis=0).T.astype(BF16)


def _permute_w_in(w_t, layer):
    _, n_in, d = w_t.shape
    tr = 128
    return pl.pallas_call(
        _permute_kernel,
        grid=(d // tr,),
        in_specs=[pl.BlockSpec((None, n_in, tr), lambda i: (layer, 0, i))],
        out_specs=pl.BlockSpec((tr, N_PERM), lambda i: (i, 0)),
        out_shape=jax.ShapeDtypeStruct((d, N_PERM), BF16),
        compiler_params=_cparams(("parallel",)),
        name="permute_w",
    )(w_t)


def _rel_bucket(dist):
    n = jnp.maximum(dist, 0)
    exact = N_BUCKETS // 2
    nf = jnp.maximum(n, exact).astype(F32)
    large = exact + (jnp.log(nf / exact) / math.log(MAX_DIST / exact) * (N_BUCKETS - exact)).astype(jnp.int32)
    return jnp.where(n < exact, n, jnp.minimum(large, N_BUCKETS - 1))


def _bias_kernel(thr_ref, tab_ref, d_ref, o_ref):
    n = jnp.maximum(d_ref[0], 0)
    for h in range(A_HEADS):
        acc = jnp.full(n.shape, tab_ref[h], F32)
        for k in range(1, N_BUCKETS):
            acc = jnp.where(n >= thr_ref[k], tab_ref[k * A_HEADS + h], acc)
        o_ref[0, h // A_REP, h % A_REP] = acc


def _bias_rows(rel_bias, dist, split_rows=False):
    N, Q, K = dist.shape
    nmax = 2 * MAX_DIST
    thr = jnp.sum(_rel_bucket(jnp.arange(nmax))[None, :] < jnp.arange(N_BUCKETS)[:, None], axis=1).astype(jnp.int32)
    smem = pl.BlockSpec(memory_space=pltpu.SMEM)
    out = pl.pallas_call(
        _bias_kernel,
        grid=(N,),
        in_specs=[smem, smem, pl.BlockSpec((1, Q, K), lambda i: (i, 0, 0))],
        out_specs=pl.BlockSpec((1, A_KV, A_REP, Q, K), lambda i: (i, 0, 0, 0, 0)),
        out_shape=jax.ShapeDtypeStruct((N, A_KV, A_REP, Q, K), F32),
        compiler_params=_cparams(("parallel",)),
        name="bias_rows",
    )(thr, rel_bias.astype(F32).reshape(N_BUCKETS * A_HEADS), dist.astype(jnp.int32))
    return out if split_rows else out.reshape(N, A_KV, A_REP * Q, K)


def _cmp_weights(w):
    wr = w.reshape(A_KV, 2, CMP_STRIDE, A_HD, A_HD)
    eye = jnp.eye(A_KV, dtype=w.dtype)
    full = jnp.einsum('gmjde,gh->jgdmhe', wr, eye)
    return full.reshape(CMP_STRIDE, A_KV * A_HD, 2 * A_KV * A_HD).astype(BF16)


def _cover_t(ns_pad, nch):
    s0 = jnp.arange(ns_pad)[:, None] * SEL_BLOCK
    c0 = jnp.arange(nch)[None, :] * CMP_STRIDE
    return ((c0 < s0 + SEL_BLOCK) & (s0 <= c0 + CMP_BLOCK - 1)).astype(F32)


def _expand_mat(ns_pad, nk):
    return (jnp.arange(nk)[None, :] // SEL_BLOCK == jnp.arange(ns_pad)[:, None]).astype(BF16)


def _gate_expand():
    col = jnp.arange(N_BRANCH * W_A)
    src = 3 * ((col % W_A) // A_HD) + col // W_A
    return (jnp.arange(LANES)[:, None] == src[None, :]).astype(BF16)


def _phase_perm():
    r = jnp.arange(PAGE_SIZE)
    per_page = PAGE_SIZE // CMP_STRIDE
    return (jnp.arange(PAGE_SIZE)[None, :] == ((r % per_page) * CMP_STRIDE + r // per_page)[:, None]).astype(BF16)


def _prompt_tables(rel_bias, T):
    QB = Q_BLOCK
    nqb = T // QB
    nch = T // CMP_STRIDE
    ns = T // SEL_BLOCK
    ns_pad = -(-ns // 8) * 8
    i_ = jnp.arange(QB)
    t = (jnp.arange(nqb) * QB)[:, None, None] + i_[None, :, None]
    cend = (jnp.arange(nch) * CMP_STRIDE + CMP_BLOCK - 1)[None, None, :]
    d = (jnp.arange(nqb) * QB)[:, None, None] + i_[None, :, None] - i_[None, None, :]
    nw = WINDOW // QB + 1
    dt = jnp.swapaxes(d, 1, 2)
    tsel = _bias_rows(rel_bias * LOG2E, dt, split_rows=True)
    return dict(
        bcmp=_bias_rows(rel_bias, t - cend),
        tselt=tsel.transpose(0, 1, 3, 2, 4).reshape(nqb, A_KV, QB, A_REP * QB),
        covt=_cover_t(ns_pad, nch),
        wmt=jnp.where((dt[:nw] >= 0) & (dt[:nw] < WINDOW), 0.0, NEG),
        cmt=jnp.where(dt[:2] >= 0, 0.0, NEG),
        gx=_gate_expand())


def _sample_tables(rel_bias, past, tp, t_real):
    SL = past + LANES
    nblk = past // CMP_STRIDE
    ns = -(-(past + t_real) // SEL_BLOCK)
    ns_pad = -(-ns // LANES) * LANES
    WK = WINDOW + LANES
    i_ = jnp.arange(tp)
    t = past + i_
    rows = lambda dist: _bias_rows(rel_bias, dist[None])[0]
    return dict(
        bcmp=rows(t[:, None] - (jnp.arange(nblk) * CMP_STRIDE + CMP_BLOCK - 1)[None, :]),
        bsel=rows(t[:, None] - jnp.arange(SL)[None, :]),
        bwin=rows(i_[:, None] + WINDOW - jnp.arange(WK)[None, :]),
        cov=_cover_t(ns_pad, nblk).T, emat=_expand_mat(ns_pad, SL), perm=_phase_perm())


def _proj_in_kernel(x_ref, g_ref, w_ref, o_ref, hn_ref):
    @pl.when(pl.program_id(1) == 0)
    def _():
        x = x_ref[...]
        ms = jnp.mean(x * x, axis=-1, keepdims=True)
        hn_ref[...] = (x * lax.rsqrt(ms + EPS) * g_ref[...]).astype(BF16)

    o_ref[...] = jnp.dot(hn_ref[...], w_ref[...], preferred_element_type=F32).astype(o_ref.dtype)


def _proj_in(x2, norm_g, w_perm, out_dtype):
    n = x2.shape[0]
    tm = min(2048, n)
    tn = 1920
    return pl.pallas_call(
        _proj_in_kernel,
        grid=(n // tm, N_PERM // tn),
        in_specs=[pl.BlockSpec((tm, D_MODEL), lambda i, j: (i, 0)),
                  pl.BlockSpec((1, D_MODEL), lambda i, j: (0, 0)),
                  pl.BlockSpec((D_MODEL, tn), lambda i, j: (0, j))],
        out_specs=pl.BlockSpec((tm, tn), lambda i, j: (i, j)),
        out_shape=jax.ShapeDtypeStruct((n, N_PERM), out_dtype),
        scratch_shapes=[pltpu.VMEM((tm, D_MODEL), BF16)],
        compiler_params=_cparams(("parallel", "arbitrary")),
        name="proj_in",
    )(x2, norm_g.reshape(1, D_MODEL), w_perm)


def _prep_a_kernel(q_ref, kv0_ref, kv1_ref, kv2_ref, bdq_ref, bdk_ref, qg_ref, kg_ref, qo_ref, ro_ref, *rt_ref):
    q = q_ref[...].astype(F32)
    ms = jnp.dot(q * q, bdq_ref[...], precision=HI, preferred_element_type=F32) * (1.0 / A_HD)
    qo_ref[...] = q * lax.rsqrt(ms + EPS) * qg_ref[...] * (A_HD ** -0.5)
    for b, kv_ref in enumerate((kv0_ref, kv1_ref, kv2_ref)):
        k = kv_ref[:, 0:128].astype(F32)
        ms = jnp.dot(k * k, bdk_ref[...], precision=HI, preferred_element_type=F32) * (1.0 / A_HD)
        kn = k * lax.rsqrt(ms + EPS) * kg_ref[b:b + 1, :]
        v = kv_ref[:, 128:256].astype(F32)
        ro_ref[2 * b] = kn
        ro_ref[2 * b + 1] = v
        if rt_ref:
            rt_ref[b][0:128, :] = kn.T
            rt_ref[b][128:256, :] = v.T


def _prep_a(u2, a_qn, a_kn, seq=None):
    n = u2.shape[0]
    tm = min(512, n)
    out_specs = [pl.BlockSpec((tm, W_A), lambda i: (i, 0)), pl.BlockSpec((6, tm, 128), lambda i: (0, i, 0))]
    out_shape = [jax.ShapeDtypeStruct((n, W_A), F32), jax.ShapeDtypeStruct((6, n, 128), F32)]
    if seq is not None:
        per = seq // tm
        out_specs += [pl.BlockSpec((None, 256, tm), lambda i: (i // per, 0, i % per))] * 3
        out_shape += [jax.ShapeDtypeStruct((n // seq, 256, seq), F32)] * 3
    bd = lambda w: (jnp.arange(w)[:, None] // A_HD == jnp.arange(w)[None, :] // A_HD).astype(F32)
    qg = jnp.tile(a_qn, A_HEADS).reshape(1, W_A)
    kg = jnp.tile(a_kn, (1, A_KV))
    const = lambda shape: pl.BlockSpec(shape, lambda i: (0, 0))
    kvs = lambda b: pl.BlockSpec((tm, 256), lambda i: (i, OFF['a_kv'] // 256 + b))
    return pl.pallas_call(
        _prep_a_kernel,
        grid=(n // tm,),
        in_specs=[pl.BlockSpec((tm, W_A), lambda i: (i, OFF['a_q'] // W_A)), kvs(0), kvs(1), kvs(2),
                  const((W_A, W_A)), const((128, 128)), const((1, W_A)), const((3, 128))],
        out_specs=out_specs,
        out_shape=out_shape,
        compiler_params=_cparams(("parallel",)),
        name="prep_a",
    )(u2, u2, u2, u2, bd(W_A), bd(128), qg, kg)


def _stack_heads(q, g):
    return jnp.concatenate([q[:, (A_REP * g + r) * A_HD:(A_REP * g + r + 1) * A_HD] for r in range(A_REP)], axis=0)


def _rep_rows(x):
    return jnp.concatenate([x] * A_REP, axis=0)


def _masked_softmax(s, valid):
    sm = jnp.where(valid, s, NEG)
    m = jnp.max(sm, axis=-1, keepdims=True)
    e = jnp.where(valid, jnp.exp(sm - m), 0.0)
    l = jnp.sum(e, axis=-1, keepdims=True)
    return e / jnp.maximum(l, TINY)


def _select_blocks(pc_sum, covt_ref, st_ref, t_row, ns, n_top):
    ns_pad = covt_ref.shape[0]
    imp = lax.dot_general(covt_ref[...], pc_sum, NT, precision=HI, preferred_element_type=F32)
    jj = lax.broadcasted_iota(jnp.int32, (ns_pad, LANES), 0)
    tt = jnp.broadcast_to(t_row, (ns_pad, LANES))
    cur = tt // SEL_BLOCK
    forced = (jj == 0) | (jj == cur) | (jj == cur - 1)
    future = jj * SEL_BLOCK > tt
    score = jnp.where(future, NEG, jnp.where(forced, -NEG, imp))
    score = jnp.where(jj < ns, score, -jnp.inf)
    st_ref[...] = score

    def beats(k, rank):
        row = jnp.broadcast_to(st_ref[pl.ds(k, 1), :], (ns_pad, LANES))
        b = (row > score) | ((row == score) & (k < jj))
        return rank + jnp.where(b, 1.0, 0.0)

    rank = jnp.zeros((ns_pad, LANES), F32)
    if ns <= 32:
        for k in range(ns):
            rank = beats(k, rank)
    else:
        rank = lax.fori_loop(0, ns, beats, rank)
    return jnp.where(rank < n_top, 1.0, 0.0)


def _pad_rows(x, rows):
    if x.shape[0] == rows:
        return x
    return jnp.concatenate([x, jnp.zeros((rows - x.shape[0], x.shape[1]), x.dtype)], axis=0)


def _select_rows(pc_sum, cov_ref, sel_ref, t0, ns, n_top, nq):
    Q = pc_sum.shape[0]
    ns_pad = cov_ref.shape[1]
    nk = -(-ns // 8) * 8
    imp = jnp.dot(pc_sum, cov_ref[...], precision=HI, preferred_element_type=F32)
    jj = lax.broadcasted_iota(jnp.int32, (Q, ns_pad), 1)
    tt = lax.broadcasted_iota(jnp.int32, (Q, ns_pad), 0) + t0
    cur = tt // SEL_BLOCK
    forced = (jj == 0) | (jj == cur) | (jj == cur - 1)
    future = jj * SEL_BLOCK > tt
    score = jnp.where(future, NEG, jnp.where(forced, -NEG, imp))
    score = jnp.where(jj < ns, score, -jnp.inf)
    score_col = _pad_rows(score, LANES).T
    kk = lax.broadcasted_iota(jnp.int32, (nk, ns_pad), 0)
    jl = lax.broadcasted_iota(jnp.int32, (nk, ns_pad), 1)
    sel_ref[...] = jnp.zeros(sel_ref.shape, F32)
    for i in range(nq):
        col = jnp.broadcast_to(score_col[0:nk, i:i + 1], (nk, ns_pad))
        row = jnp.broadcast_to(score[i:i + 1, :], (nk, ns_pad))
        beats = (col > row) | ((col == row) & (kk < jl))
        rank = jnp.sum(jnp.where(beats, 1.0, 0.0), axis=0, keepdims=True)
        sel_ref[i:i + 1, :] = jnp.where(rank < n_top, 1.0, 0.0)
    return sel_ref[...]


def _compress(src_ref, nrow, wk_ref, wv_ref, pbuf_ref, kc_ref, vc_ref, by_phase=False):
    acck = jnp.zeros((nrow, 256), F32)
    accv = jnp.zeros((nrow, 256), F32)
    for j in range(CMP_STRIDE):
        if by_phase:
            xk = src_ref[0, j, 0:nrow, :].astype(BF16)
            xv = src_ref[1, j, 0:nrow, :].astype(BF16)
        else:
            xk = src_ref[0, pl.ds(j, nrow, stride=CMP_STRIDE), :].astype(BF16)
            xv = src_ref[1, pl.ds(j, nrow, stride=CMP_STRIDE), :].astype(BF16)
        acck = acck + jnp.dot(xk, wk_ref[j], preferred_element_type=F32)
        accv = accv + jnp.dot(xv, wv_ref[j], preferred_element_type=F32)
    nout = kc_ref.shape[0]
    for acc, dst in ((acck, kc_ref), (accv, vc_ref)):
        pbuf_ref[0:nrow, :] = acc[:, 128:256]
        dst[...] = acc[0:nout, 0:128] + pbuf_ref[pl.ds(1, nout), :]


def _gated_sum(gates, gx_ref, ocat_ref):
    gh, gl = _split(gates)
    gexp = (jnp.dot(gh, gx_ref[...], preferred_element_type=F32)
            + jnp.dot(gl, gx_ref[...], preferred_element_type=F32))
    out = gexp[:, 0:W_A] * ocat_ref[0]
    for c in range(1, N_BRANCH):
        out = out + gexp[:, c * W_A:(c + 1) * W_A] * ocat_ref[c]
    return out


def _nsa_prompt_kernel(q_ref, rows_ref, small_ref, wk_ref, wv_ref, bcmp_ref, tselt_ref, covt_ref,
                       wmt_ref, cmt_ref, gx_ref,
                       o_ref, kc_ref, vc_ref, pbuf_ref, mb_ref, acc_ref, m_ref, l_ref, st_ref, ocat_ref, ocatt_ref,
                       *, ns, n_top):
    bi = pl.program_id(1)
    T = rows_ref.shape[1]
    nch = T // CMP_STRIDE
    QB = Q_BLOCK
    SEL, WIN = 1, 2

    @pl.when(bi == 0)
    def _():
        pbuf_ref[...] = jnp.zeros(pbuf_ref.shape, F32)
        _compress(rows_ref, nch, wk_ref, wv_ref, pbuf_ref, kc_ref, vc_ref)

    t0 = bi * QB
    q = q_ref[0]
    tc = lax.broadcasted_iota(jnp.int32, (QB, nch), 0) + t0
    cend = lax.broadcasted_iota(jnp.int32, (QB, nch), 1) * CMP_STRIDE + (CMP_BLOCK - 1)
    cvalid = _rep_rows(tc - cend >= 0)
    t_row = lax.broadcasted_iota(jnp.int32, (1, LANES), 1) + t0
    qg = [_stack_heads(q, g).astype(BF16) for g in range(A_KV)]
    qt = (q * LOG2E).T
    qgt = [jnp.concatenate([qt[(A_REP * g + r) * A_HD:(A_REP * g + r + 1) * A_HD] for r in range(A_REP)],
                           axis=1).astype(BF16) for g in range(A_KV)]

    for g in range(A_KV):
        kc = kc_ref[:, g * A_HD:(g + 1) * A_HD].astype(BF16)
        vc = vc_ref[:, g * A_HD:(g + 1) * A_HD].astype(BF16)
        s = lax.dot_general(qg[g], kc, NT, preferred_element_type=F32) + bcmp_ref[0, g]
        p_c = _masked_softmax(s, cvalid)
        o_c = jnp.dot(p_c.astype(BF16), vc, preferred_element_type=F32)
        for r in range(A_REP):
            h = A_REP * g + r
            ocat_ref[0, :, h * A_HD:(h + 1) * A_HD] = o_c[r * QB:(r + 1) * QB]
        pc_sum = p_c[0:QB] + p_c[QB:2 * QB] + p_c[2 * QB:3 * QB] + p_c[3 * QB:4 * QB]
        sel_t = _select_blocks(pc_sum, covt_ref, st_ref, t_row, ns, n_top)
        mb_ref[g] = (sel_t - 1.0) * (-NEG)

    m_ref[...] = jnp.full(m_ref.shape, NEG, F32)
    l_ref[...] = jnp.zeros(l_ref.shape, F32)
    acc_ref[...] = jnp.zeros(acc_ref.shape, F32)
    lanes4 = lambda x: jnp.concatenate([x] * A_REP, axis=1)
    half = SEL_BLOCK

    def tiles(specs):
        cat = lambda xs: xs[0] if len(xs) == 1 else jnp.concatenate(xs, axis=0)
        cis = [2 * (br - 1) + g for br, g, _, _ in specs]
        sms, vs = [], []
        for br, g, kb, nb in specs:
            ks, vv, bs, ms = [], [], [], []
            for j in range(nb):
                off = pl.multiple_of((kb + j) * QB, QB)
                ks.append(rows_ref[2 * br, pl.ds(off, QB), g * A_HD:(g + 1) * A_HD])
                vv.append(rows_ref[2 * br + 1, pl.ds(off, QB), g * A_HD:(g + 1) * A_HD])
                bs.append(tselt_ref[bi - kb - j, g])
                if br == SEL:
                    blk = (QB // half) * (kb + j)
                    mt = jnp.concatenate([jnp.broadcast_to(mb_ref[g, pl.ds(blk + a, 1), :], (half, QB))
                                          for a in range(QB // half)], axis=0)
                    ms.append(mt + cmt_ref[jnp.minimum(bi - kb, 1)] if nb == 1 else mt)
                else:
                    ms.append(wmt_ref[bi - kb - j])
            vs.append(cat(vv).astype(BF16))
            sms.append(jnp.dot(cat(ks).astype(BF16), qgt[g], preferred_element_type=F32)
                       + cat(bs) + lanes4(cat(ms)))
        m_prevs = [m_ref[ci] for ci in cis]
        m_news = [jnp.maximum(mp, jnp.max(sm, axis=0, keepdims=True)) for mp, sm in zip(m_prevs, sms)]
        ps = [jnp.exp2(sm - mn) for sm, mn in zip(sms, m_news)]
        sums = [jnp.sum(p, axis=0, keepdims=True) for p in ps]
        pvs = [lax.dot_general(v, p.astype(BF16), TN, preferred_element_type=F32) for v, p in zip(vs, ps)]
        for ci, mp, mn, sm_, pv in zip(cis, m_prevs, m_news, sums, pvs):
            alpha = jnp.exp2(mp - mn)
            l_ref[ci] = alpha * l_ref[ci] + sm_
            acc_ref[ci] = alpha * acc_ref[ci] + pv
            m_ref[ci] = mn

    def body_sel(kb, carry):
        tiles([(SEL, g, kb, 1) for g in range(A_KV)])
        return carry

    def body_sel2(pair, carry):
        tiles([(SEL, g, 2 * pair, 2) for g in range(A_KV)])
        return carry

    def body_both(kb, carry):
        tiles([(br, g, kb, 1) for g in range(A_KV) for br in (SEL, WIN)])
        return carry

    lo = jnp.maximum(bi - WINDOW // QB, 0)
    lax.fori_loop(0, lo // 2, body_sel2, 0)
    lax.fori_loop(2 * (lo // 2), lo, body_sel, 0)
    lax.fori_loop(lo, bi + 1, body_both, 0)

    for br in (SEL, WIN):
        for g in range(A_KV):
            ci = 2 * (br - 1) + g
            o_t = acc_ref[ci] / jnp.maximum(l_ref[ci], TINY)
            for r in range(A_REP):
                h = A_REP * g + r
                ocatt_ref[br - 1, h * A_HD:(h + 1) * A_HD, :] = o_t[:, r * QB:(r + 1) * QB]
        ocat_ref[br] = ocatt_ref[br - 1].T

    o_ref[0] = _gated_sum(jax.nn.sigmoid(small_ref[0].astype(F32)), gx_ref, ocat_ref)


def _nsa_prompt(qn3, rows6, u3, wk, wv, tb):
    B, T, _ = qn3.shape
    QB = Q_BLOCK
    nqb = T // QB
    nch = T // CMP_STRIDE
    ns = T // SEL_BLOCK
    ns_pad = tb['covt'].shape[0]
    n_top = min(TOP_N, ns)
    nw = tb['wmt'].shape[0]
    kern = functools.partial(_nsa_prompt_kernel, ns=ns, n_top=n_top)
    c2 = lambda shape: pl.BlockSpec(shape, lambda b, i: (0,) * len(shape))
    return pl.pallas_call(
        kern,
        grid=(B, nqb),
        in_specs=[pl.BlockSpec((1, QB, W_A), lambda b, i: (b, i, 0)),
                  pl.BlockSpec((6, T, 128), lambda b, i: (0, b, 0)),
                  pl.BlockSpec((1, QB, LANES), lambda b, i: (b, i, OFF['small'] // LANES)),
                  c2((CMP_STRIDE, 128, 256)), c2((CMP_STRIDE, 128, 256)),
                  pl.BlockSpec((1, A_KV, A_REP * QB, nch), lambda b, i: (i, 0, 0, 0)),
                  c2((nqb, A_KV, QB, A_REP * QB)),
                  c2((ns_pad, nch)), c2((nw, QB, QB)), c2((2, QB, QB)),
                  c2((LANES, N_BRANCH * W_A))],
        out_specs=pl.BlockSpec((1, QB, W_A), lambda b, i: (b, i, 0)),
        out_shape=jax.ShapeDtypeStruct((B, T, W_A), F32),
        scratch_shapes=[pltpu.VMEM((nch, 128), F32), pltpu.VMEM((nch, 128), F32),
                        pltpu.VMEM((nch + 8, 128), F32),
                        pltpu.VMEM((A_KV, ns_pad, QB), F32),
                        pltpu.VMEM((2 * A_KV, A_HD, A_REP * QB), F32),
                        pltpu.VMEM((2 * A_KV, 1, A_REP * QB), F32),
                        pltpu.VMEM((2 * A_KV, 1, A_REP * QB), F32),
                        pltpu.VMEM((ns_pad, LANES), F32),
                        pltpu.VMEM((N_BRANCH, QB, W_A), F32),
                        pltpu.VMEM((2, W_A, QB), F32)],
        compiler_params=_cparams(("parallel", "arbitrary")),
        name="nsa_prompt",
    )(qn3, rows6, u3, wk, wv, tb['bcmp'], tb['tselt'], tb['covt'], tb['wmt'], tb['cmt'], tb['gx'])


def _nsa_sample_kernel(pt_ref, *refs, pg, past, tp, t_real, ns, n_top):
    cmp_refs = refs[0:pg]
    sel_refs = refs[pg:2 * pg]
    (q_ref, rows_ref, small_ref, win_ref, wk_ref, wv_ref, bcmp_ref, bsel_ref, bwin_ref, cov_ref, e_ref, perm_ref,
     o_ref, cslab, sslab, kc_ref, vc_ref, pbuf_ref, st_ref, s_ref) = refs[2 * pg:]
    p = pl.program_id(1)
    npg = pl.num_programs(1)
    n_pages = past // PAGE_SIZE
    SL = (n_pages + 1) * PAGE_SIZE
    nblk = kc_ref.shape[0]
    WK = WINDOW + LANES
    per_page = PAGE_SIZE // CMP_STRIDE

    for k in range(pg):
        page = p * pg + k
        row0 = pl.multiple_of(page * per_page, per_page)
        ordered = lax.dot_general(perm_ref[...], cmp_refs[k][...].astype(BF16), NT,
                                  preferred_element_type=F32)
        for kv in range(2):
            for j in range(CMP_STRIDE):
                cslab[kv, j, pl.ds(row0, per_page), :] = ordered[j * per_page:(j + 1) * per_page,
                                                                 kv * 128:(kv + 1) * 128]
            sslab[kv, page] = sel_refs[k][kv * 128:(kv + 1) * 128, :]

    @pl.when(p == npg - 1)
    def _():
        new_t = [_pad_rows(rows_ref[i], LANES).T for i in range(2, 6)]
        for kv in range(2):
            cslab[kv, :, nblk:nblk + 16, :] = jnp.zeros((CMP_STRIDE, 16, 128), F32)
            for i in range(tp):
                cslab[kv, i, nblk:nblk + 1, :] = rows_ref[kv, i:i + 1, :]
            sslab[kv, n_pages] = new_t[kv]
        pbuf_ref[...] = jnp.zeros(pbuf_ref.shape, F32)
        _compress(cslab, nblk + 8, wk_ref, wv_ref, pbuf_ref, kc_ref, vc_ref, by_phase=True)

        q = q_ref[0]
        gates = jax.nn.sigmoid(small_ref[0].astype(F32))
        R = A_REP * tp
        zq = jnp.zeros((R, A_HD), F32)
        q2 = jnp.concatenate([jnp.concatenate([_stack_heads(q, 0), zq], axis=1),
                              jnp.concatenate([zq, _stack_heads(q, 1)], axis=1)], axis=0).astype(BF16)
        rep2 = lambda x: jnp.concatenate([x] * (A_KV * A_REP), axis=0)
        ti = lax.broadcasted_iota(jnp.int32, (tp, SL), 0) + past
        causal = ti - lax.broadcasted_iota(jnp.int32, (tp, SL), 1) >= 0
        tc = lax.broadcasted_iota(jnp.int32, (tp, nblk), 0) + past
        cend = lax.broadcasted_iota(jnp.int32, (tp, nblk), 1) * CMP_STRIDE + (CMP_BLOCK - 1)
        wd = (lax.broadcasted_iota(jnp.int32, (tp, WK), 0) + WINDOW
              - lax.broadcasted_iota(jnp.int32, (tp, WK), 1))
        npt = n_pages + 1
        nck = 5 if npt % 5 == 0 else 1
        cpt = npt // nck
        ck = cpt * LANES

        s = lax.dot_general(q2, kc_ref[...].astype(BF16), NT, preferred_element_type=F32) + bcmp_ref[...]
        p_c = _masked_softmax(s, rep2(tc - cend >= 0))
        o_c = jnp.dot(p_c.astype(BF16), vc_ref[...].astype(BF16), preferred_element_type=F32)

        kmasks = []
        for g in range(A_KV):
            pg_ = p_c[g * R:(g + 1) * R]
            pc_sum = pg_[0:tp] + pg_[tp:2 * tp] + pg_[2 * tp:3 * tp] + pg_[3 * tp:4 * tp]
            sel = _select_rows(pc_sum, cov_ref, st_ref, past, ns, n_top, t_real).astype(BF16)
            km = jnp.dot(sel, e_ref[...], preferred_element_type=F32)
            kmasks.append(_rep_rows(causal & (km > 0.5)))
        svalid = jnp.concatenate(kmasks, axis=0)

        def sel_t(kv, c):
            return jnp.concatenate([sslab[kv, t] for t in range(c * cpt, (c + 1) * cpt)], axis=1).astype(BF16)

        for c in range(nck):
            s_ref[:, c * ck:(c + 1) * ck] = jnp.dot(q2, sel_t(0, c), preferred_element_type=F32)
        s_ref[...] = _masked_softmax(s_ref[...] + bsel_ref[...], svalid)
        o_s = jnp.zeros((A_KV * R, 2 * A_HD), F32)
        for c in range(nck):
            o_s = o_s + lax.dot_general(s_ref[:, c * ck:(c + 1) * ck].astype(BF16), sel_t(1, c), NT,
                                        preferred_element_type=F32)

        kw = jnp.concatenate([win_ref[0:128, :], new_t[2]], axis=1).astype(BF16)
        vw = jnp.concatenate([win_ref[128:256, :], new_t[3]], axis=1).astype(BF16)
        sw = jnp.dot(q2, kw, preferred_element_type=F32) + bwin_ref[...]
        p_w = _masked_softmax(sw, rep2((wd >= 0) & (wd < WINDOW)))
        o_w = lax.dot_general(p_w.astype(BF16), vw, NT, preferred_element_type=F32)

        for g in range(A_KV):
            gs = slice(g * A_HD, (g + 1) * A_HD)
            for r in range(A_REP):
                h = A_REP * g + r
                rs = slice(g * R + r * tp, g * R + (r + 1) * tp)
                out = (gates[:, 3 * h:3 * h + 1] * o_c[rs, gs] + gates[:, 3 * h + 1:3 * h + 2] * o_s[rs, gs]
                       + gates[:, 3 * h + 2:3 * h + 3] * o_w[rs, gs])
                o_ref[0, :, h * A_HD:(h + 1) * A_HD] = out


def _nsa_sample(qn3, rows6, u3, pool_cmp, pool_sel, win_t, layer, page_table, wk, wv, tb, past, t_real):
    B, tp, _ = qn3.shape
    n_pages = past // PAGE_SIZE
    pg = 16 if n_pages % 16 == 0 else n_pages
    npg = n_pages // pg
    SL = past + LANES
    nblk = past // CMP_STRIDE
    ns = -(-(past + t_real) // SEL_BLOCK)
    ns_pad = tb['cov'].shape[1]
    n_top = min(TOP_N, ns)
    WK = WINDOW + LANES
    kern = functools.partial(_nsa_sample_kernel, pg=pg, past=past, tp=tp, t_real=t_real, ns=ns, n_top=n_top)

    def page_spec(k):
        return pl.BlockSpec((None, None, 256, PAGE_SIZE), lambda b, p, pt: (layer, pt[b, p * pg + k], 0, 0))

    def c_(shape):
        return pl.BlockSpec(shape, lambda b, p, pt: (0,) * len(shape))

    R = A_REP * tp
    grid_spec = pltpu.PrefetchScalarGridSpec(
        num_scalar_prefetch=1,
        grid=(B, npg),
        in_specs=([page_spec(k) for k in range(pg)] + [page_spec(k) for k in range(pg)]
                  + [pl.BlockSpec((1, tp, W_A), lambda b, p, pt: (b, 0, 0)),
                     pl.BlockSpec((6, tp, 128), lambda b, p, pt: (0, b, 0)),
                     pl.BlockSpec((1, tp, LANES), lambda b, p, pt: (b, 0, OFF['small'] // LANES)),
                     pl.BlockSpec((None, None, 256, WINDOW), lambda b, p, pt: (layer, b, 0, 0)),
                     c_((CMP_STRIDE, 128, 256)), c_((CMP_STRIDE, 128, 256)),
                     c_((A_KV * R, nblk)), c_((A_KV * R, SL)), c_((A_KV * R, WK)),
                     c_((nblk, ns_pad)), c_((ns_pad, SL)), c_((PAGE_SIZE, PAGE_SIZE))]),
        out_specs=pl.BlockSpec((1, tp, W_A), lambda b, p, pt: (b, 0, 0)),
        scratch_shapes=[pltpu.VMEM((2, CMP_STRIDE, nblk + 16, 128), F32),
                        pltpu.VMEM((2, n_pages + 1, 128, PAGE_SIZE), F32),
                        pltpu.VMEM((nblk, 128), F32), pltpu.VMEM((nblk, 128), F32),
                        pltpu.VMEM((nblk + 16, 128), F32),
                        pltpu.VMEM((tp, ns_pad), F32), pltpu.VMEM((A_KV * R, SL), F32)],
    )
    return pl.pallas_call(
        kern,
        grid_spec=grid_spec,
        out_shape=jax.ShapeDtypeStruct((B, tp, W_A), F32),
        compiler_params=_cparams(("parallel", "arbitrary")),
        name="nsa_sample",
    )(page_table, *([pool_cmp] * pg), *([pool_sel] * pg), qn3, rows6, u3, win_t, wk, wv,
      *(tb[k].reshape(A_KV * R, -1) for k in ('bcmp', 'bsel', 'bwin')), tb['cov'], tb['emat'], tb['perm'])


def _tri(L, lower_incl):
    r = lax.broadcasted_iota(jnp.int32, (L, L), 0)
    c = lax.broadcasted_iota(jnp.int32, (L, L), 1)
    return (r >= c) if lower_incl else (r > c)


def _mlstm_kernel(x_ref, small_ref, smallt_ref, brow_ref, bcol_ref, c0_ref, n0_ref, m0_ref,
                  h_ref, co_ref, no_ref, mo_ref, c_s, m_s, *, L, CB, t_real):
    c = pl.program_id(1)
    nc = pl.num_programs(1)

    @pl.when(c == 0)
    def _():
        c_s[:, :, 0:M_HD] = c0_ref[0]
        c_s[:, :, M_HD:2 * M_HD] = n0_ref[0]
        m_s[...] = m0_ref[0]

    sm = small_ref[0].astype(F32)
    low = _tri(L, True)
    lowf = low.astype(F32)
    cbs = range(CB)
    hd = range(M_HEADS)
    ch = [(cb, h) for cb in cbs for h in hd]

    li_col, b_col, b_row, li_row = [], [], [], []
    for cb in cbs:
        rs = slice(cb * L, (cb + 1) * L)
        smt = smallt_ref[0, cb]
        lic = sm[rs, SM_MIF:SM_MIF + 4] + brow_ref[0:1, 0:4]
        lfc = _log_sigmoid(sm[rs, SM_MIF + 4:SM_MIF + 8] + brow_ref[0:1, 4:8])
        lir = smt[0:4, :] + bcol_ref[0:4, :]
        lfr = _log_sigmoid(smt[4:8, :] + bcol_ref[4:8, :])
        if t_real % L:
            t0 = (c * CB + cb) * L
            tcol = lax.broadcasted_iota(jnp.int32, (L, 4), 0) + t0
            trow = lax.broadcasted_iota(jnp.int32, (4, L), 1) + t0
            lic = jnp.where(tcol < t_real, lic, NEG)
            lfc = jnp.where(tcol < t_real, lfc, 0.0)
            lir = jnp.where(trow < t_real, lir, NEG)
            lfr = jnp.where(trow < t_real, lfr, 0.0)
        li_col.append(lic)
        li_row.append(lir)
        b_col.append(jnp.dot(lowf, lfc, precision=HI, preferred_element_type=F32))
        b_row.append(lax.dot_general(lfr, lowf, NT, precision=HI, preferred_element_type=F32))

    rows = lambda cb: slice(cb * L, (cb + 1) * L)
    q = {(cb, h): x_ref[0, rows(cb), h * M_HD:(h + 1) * M_HD].astype(F32) for cb, h in ch}
    k = {(cb, h): x_ref[0, rows(cb), W_M + h * M_HD:W_M + (h + 1) * M_HD].astype(F32) * (M_HD ** -0.5)
         for cb, h in ch}
    v = {(cb, h): x_ref[0, rows(cb), 2 * W_M + h * M_HD:2 * W_M + (h + 1) * M_HD].astype(F32) for cb, h in ch}
    bc = {(cb, h): b_col[cb][:, h:h + 1] for cb, h in ch}
    Dm = {(cb, h): jnp.where(low, bc[cb, h] - b_row[cb][h:h + 1, :] + li_row[cb][h:h + 1, :], NEG) for cb, h in ch}
    dmax = {x: jnp.max(Dm[x], axis=-1, keepdims=True) for x in ch}
    qk = {x: lax.dot_general(q[x], k[x], NT, preferred_element_type=F32) for x in ch}
    bL = {x: bc[x][L - 1:L, :] for x in ch}
    wlog = {(cb, h): bL[cb, h] - bc[cb, h] + li_col[cb][:, h:h + 1] for cb, h in ch}
    wmax = {x: jnp.max(wlog[x], axis=0, keepdims=True) for x in ch}

    m_prev, m_new = {}, {}
    for h in hd:
        m = m_s[h][:, 0:1]
        for cb in cbs:
            m_prev[cb, h] = m
            m = jnp.maximum(bL[cb, h] + m, wmax[cb, h])
            m_new[cb, h] = m
    a = {x: bc[x] + m_prev[x] for x in ch}
    mt = {x: jnp.maximum(a[x], dmax[x]) for x in ch}
    S = {x: qk[x] * jnp.exp(Dm[x] - mt[x]) for x in ch}
    inter = {x: jnp.exp(a[x] - mt[x]) for x in ch}
    v1 = {x: jnp.concatenate([v[x], jnp.ones((L, M_HD), F32)], axis=1) for x in ch}
    Sv = {x: jnp.dot(S[x], v1[x], preferred_element_type=F32) for x in ch}
    dec = {x: jnp.exp(bL[x] + m_prev[x] - m_new[x]) for x in ch}
    kw = {x: k[x] * jnp.exp(wlog[x] - m_new[x]) for x in ch}
    kv = {x: lax.dot_general(kw[x], v1[x], TN, preferred_element_type=F32) for x in ch}

    CN = {h: c_s[h] for h in hd}
    for cb in cbs:
        qC = {h: jnp.dot(q[cb, h], CN[h], preferred_element_type=F32) for h in hd}
        for h in hd:
            x = (cb, h)
            num = inter[x] * qC[h] + Sv[x]
            den = num[:, M_HD:2 * M_HD]
            h_ref[0, rows(cb), h * M_HD:(h + 1) * M_HD] = (num[:, 0:M_HD]
                                                           / jnp.maximum(jnp.abs(den), jnp.exp(-mt[x])))
            CN[h] = dec[x] * CN[h] + kv[x]
    for h in hd:
        c_s[h] = CN[h]
        m_s[h] = jnp.broadcast_to(m_new[CB - 1, h], (1, LANES))

    @pl.when(c == nc - 1)
    def _():
        co_ref[0] = c_s[:, :, 0:M_HD]
        no_ref[0] = c_s[:, :, M_HD:2 * M_HD]
        mo_ref[0] = m_s[...]


def _mlstm(u3, smallt, m_bi, m_bf, C0, n0, m0, L, t_real):
    B, T, _ = u3.shape
    nc = T // L
    brow = jnp.concatenate([m_bi, m_bf]).reshape(1, 8)
    bcol = jnp.concatenate([m_bi, m_bf]).reshape(8, 1)
    n0 = jnp.broadcast_to(n0[:, :, :, None], (B, M_HEADS, M_HD, M_HD))
    m0 = jnp.broadcast_to(m0[:, :, None, None], (B, M_HEADS, 1, LANES))
    CB = 2 if nc % 2 == 0 else 1
    kern = functools.partial(_mlstm_kernel, L=L, CB=CB, t_real=t_real)
    st = lambda shape: pl.BlockSpec(shape, lambda b, c: (b,) + (0,) * (len(shape) - 1))
    h, Cn, nn, mn = pl.pallas_call(
        kern,
        grid=(B, nc // CB),
        in_specs=[pl.BlockSpec((1, CB * L, 3 * W_M), lambda b, c: (b, c, OFF['m_qkv'] // (3 * W_M))),
                  pl.BlockSpec((1, CB * L, LANES), lambda b, c: (b, c, OFF['small'] // LANES)),
                  pl.BlockSpec((1, CB, 16, L), lambda b, c: (b, c, 0, 0)),
                  pl.BlockSpec((1, 8), lambda b, c: (0, 0)), pl.BlockSpec((8, 1), lambda b, c: (0, 0)),
                  st((1, M_HEADS, M_HD, M_HD)), st((1, M_HEADS, M_HD, M_HD)), st((1, M_HEADS, 1, LANES))],
        out_specs=[pl.BlockSpec((1, CB * L, W_M), lambda b, c: (b, c, 0)),
                   st((1, M_HEADS, M_HD, M_HD)), st((1, M_HEADS, M_HD, M_HD)), st((1, M_HEADS, 1, LANES))],
        out_shape=[jax.ShapeDtypeStruct((B, T, W_M), F32),
                   jax.ShapeDtypeStruct((B, M_HEADS, M_HD, M_HD), F32),
                   jax.ShapeDtypeStruct((B, M_HEADS, M_HD, M_HD), F32),
                   jax.ShapeDtypeStruct((B, M_HEADS, 1, LANES), F32)],
        scratch_shapes=[pltpu.VMEM((M_HEADS, M_HD, 2 * M_HD), F32), pltpu.VMEM((M_HEADS, 1, LANES), F32)],
        compiler_params=_cparams(("parallel", "arbitrary")),
        name="mlstm",
    )(u3, u3, smallt, brow, bcol, C0, n0, m0)
    return h, Cn, nn[:, :, :, 0], mn[:, :, 0, 0]


def _gdn_a_kernel(x_ref, small_ref, smallt_ref, cw_ref, prow_ref, pcol_ref, buf_ref,
                  u_ref, w_ref, qe_ref, kd_ref, at_ref, eg_ref, xbuf, *, L, CB, t_real):
    c = pl.program_id(1)
    PRE = 8
    R = CB * L

    @pl.when(c == 0)
    def _():
        xbuf[0:PRE, :] = buf_ref[0]

    xbuf[PRE:PRE + R, :] = x_ref[0].astype(F32)
    conv = jnp.zeros((R, 3 * W_G), F32)
    for j in range(CONV_W):
        conv = conv + xbuf[pl.ds(PRE - (CONV_W - 1) + j, R), :] * cw_ref[j:j + 1, :]
    xbuf[0:PRE, :] = xbuf[R:R + PRE, :]
    conv = _silu(conv)

    sm = small_ref[0].astype(F32)
    g_all = -jnp.exp(prow_ref[0:1, 0:4]) * _softplus(sm[:, SM_GAB:SM_GAB + 4] + prow_ref[1:2, 0:4])
    beta_all = jax.nn.sigmoid(sm[:, SM_GAB + 4:SM_GAB + 8])
    lowf = _tri(L, True).astype(F32)

    H = G_HEADS
    lane_h = lax.broadcasted_iota(jnp.int32, (L, H * L), 1) // L
    row_p = lax.broadcasted_iota(jnp.int32, (L, H * L), 0)
    col_p = lax.broadcasted_iota(jnp.int32, (L, H * L), 1) % L
    blockmask = (lax.broadcasted_iota(jnp.int32, (H * L, H * L), 0) // L
                 == lax.broadcasted_iota(jnp.int32, (H * L, H * L), 1) // L)

    def pack_diag(full):
        out = full[0:L]
        for h in range(1, H):
            out = jnp.where(lane_h == h, full[h * L:(h + 1) * L], out)
        return out

    def block_diag(m, parts):
        if L % 16:
            return _split(jnp.where(blockmask, jnp.concatenate([m] * H, axis=0), 0.0))
        return tuple(jnp.where(blockmask, jnp.concatenate([p] * H, axis=0), jnp.zeros((), BF16))
                     for p in parts)

    a_list, rhs_list = [], []
    for cb in range(CB):
        rs = slice(cb * L, (cb + 1) * L)
        smt = smallt_ref[0, cb]
        g_col = g_all[rs]
        beta_col = beta_all[rs]
        g_row = -jnp.exp(pcol_ref[0:4, 0:1]) * _softplus(smt[8:12, :] + pcol_ref[0:4, 1:2])
        if t_real % L:
            t0 = (c * CB + cb) * L
            tcol = lax.broadcasted_iota(jnp.int32, (L, 4), 0) + t0
            trow = lax.broadcasted_iota(jnp.int32, (4, L), 1) + t0
            g_col = jnp.where(tcol < t_real, g_col, 0.0)
            beta_col = jnp.where(tcol < t_real, beta_col, 0.0)
            g_row = jnp.where(trow < t_real, g_row, 0.0)
        G_col = jnp.dot(lowf, g_col, precision=HI, preferred_element_type=F32)
        G_row = lax.dot_general(g_row, lowf, NT, precision=HI, preferred_element_type=F32)

        qs, ks, kbs, rhss = [], [], [], []
        for h in range(G_HEADS):
            hs = slice(h * G_HD, (h + 1) * G_HD)
            cq = conv[rs, h * G_HD:(h + 1) * G_HD]
            ck = conv[rs, W_G + h * G_HD:W_G + (h + 1) * G_HD]
            v = conv[rs, 2 * W_G + h * G_HD:2 * W_G + (h + 1) * G_HD]
            q = cq * lax.rsqrt(jnp.sum(cq * cq, axis=-1, keepdims=True) + EPS) * (G_HD ** -0.5)
            k = ck * lax.rsqrt(jnp.sum(ck * ck, axis=-1, keepdims=True) + EPS)
            Gc = G_col[:, h:h + 1]
            bcol = beta_col[:, h:h + 1]
            kb = k * bcol
            eG = jnp.exp(Gc)
            GL = Gc[L - 1:L, :]
            qs.append(q)
            ks.append(k)
            kbs.append(kb)
            rhss.append(jnp.concatenate([v * bcol, kb * eG], axis=1))
            qe_ref[0, rs, hs] = q * eG
            kd_ref[0, rs, hs] = k * jnp.exp(GL - Gc)
            eg_ref[0, cb, h:h + 1, :] = jnp.broadcast_to(jnp.exp(GL), (1, LANES))
        eg_ref[0, cb, G_HEADS:8, :] = jnp.zeros((8 - G_HEADS, LANES), F32)

        Gc_p = jnp.concatenate([jnp.broadcast_to(G_col[:, h:h + 1], (L, L)) for h in range(H)], axis=1)
        Gr_p = jnp.concatenate([jnp.broadcast_to(G_row[h:h + 1, :], (L, L)) for h in range(H)], axis=1)
        low_p = row_p >= col_p
        dmask = jnp.where(low_p, jnp.exp(jnp.where(low_p, Gc_p - Gr_p, 0.0)), 0.0)
        k_st = _split(jnp.concatenate(ks, axis=0))
        A = pack_diag(_dot3(_split(jnp.concatenate(kbs, axis=0)), k_st, NT)) * jnp.where(row_p > col_p, dmask, 0.0)
        attn = pack_diag(lax.dot_general(jnp.concatenate(qs, axis=0).astype(BF16), k_st[0], NT,
                                         preferred_element_type=F32)) * dmask
        at_ref[0, rs, :] = attn
        a_list.append(A)
        rhs_list.append(_split(jnp.concatenate(rhss, axis=0)))

    chunks = range(CB)
    X = [jnp.where(row_p == col_p, 1.0, 0.0) - a_list[cb] for cb in chunks]
    As = [_split(a_list[cb]) for cb in chunks]
    Pw = [_dot3(As[cb], block_diag(a_list[cb], As[cb])) for cb in chunks]
    span = 2
    while span < L:
        Ps = [_split(Pw[cb]) for cb in chunks]
        Pbd = [block_diag(Pw[cb], Ps[cb]) for cb in chunks]
        X = [X[cb] + _dot3(_split(X[cb]), Pbd[cb]) for cb in chunks]
        span *= 2
        if span < L:
            Pw = [_dot3(Ps[cb], Pbd[cb]) for cb in chunks]
    sol = [_dot3(block_diag(X[cb], _split(X[cb])), rhs_list[cb]) for cb in chunks]
    for cb in chunks:
        for h in range(H):
            rs = slice(cb * L, (cb + 1) * L)
            hs = slice(h * G_HD, (h + 1) * G_HD)
            u_ref[0, rs, hs] = sol[cb][h * L:(h + 1) * L, 0:G_HD]
            w_ref[0, rs, hs] = sol[cb][h * L:(h + 1) * L, G_HD:2 * G_HD]


def _gdn_b_kernel(u_ref, w_ref, qe_ref, kd_ref, at_ref, eg_ref, s0_ref, o_ref, so_ref, s_s, *, L, BB):
    c = pl.program_id(1)
    nc = pl.num_programs(1)

    @pl.when(c == 0)
    def _():
        s_s[...] = s0_ref[...]

    ch = [(b, h, slice(h * G_HD, (h + 1) * G_HD)) for b in range(BB) for h in range(G_HEADS)]
    S = [s_s[b, h] for b, h, _ in ch]
    r = [jnp.dot(jnp.concatenate([w_ref[b, :, hs], qe_ref[b, :, hs]], axis=0), S[i], preferred_element_type=F32)
         for i, (b, h, hs) in enumerate(ch)]
    v_new = [u_ref[b, :, hs] - r[i][0:L] for i, (b, h, hs) in enumerate(ch)]
    av = [jnp.dot(at_ref[b, :, h * L:(h + 1) * L], v_new[i], preferred_element_type=F32)
          for i, (b, h, hs) in enumerate(ch)]
    kdv = [lax.dot_general(kd_ref[b, :, hs], v_new[i], TN, preferred_element_type=F32)
           for i, (b, h, hs) in enumerate(ch)]
    for i, (b, h, hs) in enumerate(ch):
        o_ref[b, :, hs] = r[i][L:2 * L] + av[i]
        s_s[b, h] = eg_ref[b, 0, h:h + 1, :] * S[i] + kdv[i]

    @pl.when(c == nc - 1)
    def _():
        so_ref[...] = s_s[...]


def _gdn(u3, smallt, g_conv, g_A_log, g_dt_bias, buf, S0, L, t_real):
    B, T, _ = u3.shape
    nc = T // L
    CB = 4 if nc % 4 == 0 else 1
    BB = 8 if B % 8 == 0 else B
    prow = jnp.stack([g_A_log, g_dt_bias])
    pcol = jnp.stack([g_A_log, g_dt_bias], axis=1)
    buf8 = jnp.concatenate([jnp.zeros((B, 8 - (CONV_W - 1), 3 * W_G), F32), buf], axis=1)
    st = lambda shape: pl.BlockSpec(shape, lambda b, c: (b,) + (0,) * (len(shape) - 1))
    cst = lambda shape: pl.BlockSpec(shape, lambda b, c: (0,) * len(shape))
    row = lambda nb, r, w: pl.BlockSpec((nb, r, w), lambda b, c: (b, c, 0))
    tok = lambda w: jax.ShapeDtypeStruct((B, T, w), F32)
    u, w, qe, kd, at, eg = pl.pallas_call(
        functools.partial(_gdn_a_kernel, L=L, CB=CB, t_real=t_real),
        grid=(B, nc // CB),
        in_specs=[pl.BlockSpec((1, CB * L, 3 * W_G), lambda b, c: (b, c, OFF['g_qkv'] // (3 * W_G))),
                  pl.BlockSpec((1, CB * L, LANES), lambda b, c: (b, c, OFF['small'] // LANES)),
                  pl.BlockSpec((1, CB, 16, L), lambda b, c: (b, c, 0, 0)),
                  cst((CONV_W, 3 * W_G)), cst((2, 4)), cst((4, 2)), st((1, 8, 3 * W_G))],
        out_specs=[row(1, CB * L, W_G)] * 4 + [row(1, CB * L, G_HEADS * L),
                                               pl.BlockSpec((1, CB, 8, LANES), lambda b, c: (b, c, 0, 0))],
        out_shape=[tok(W_G)] * 4 + [tok(G_HEADS * L), jax.ShapeDtypeStruct((B, nc, 8, LANES), F32)],
        scratch_shapes=[pltpu.VMEM((CB * L + 8, 3 * W_G), F32)],
        compiler_params=_cparams(("parallel", "arbitrary")),
        name="gdn_a",
    )(u3, u3, smallt, g_conv, prow, pcol, buf8)
    sblk = pl.BlockSpec((BB, G_HEADS, G_HD, G_HD), lambda b, c: (b, 0, 0, 0))
    return pl.pallas_call(
        functools.partial(_gdn_b_kernel, L=L, BB=BB),
        grid=(B // BB, nc),
        in_specs=[row(BB, L, W_G)] * 4 + [row(BB, L, G_HEADS * L),
                                          pl.BlockSpec((BB, 1, 8, LANES), lambda b, c: (b, c, 0, 0)), sblk],
        out_specs=[row(BB, L, W_G), sblk],
        out_shape=[tok(W_G), jax.ShapeDtypeStruct((B, G_HEADS, G_HD, G_HD), F32)],
        scratch_shapes=[pltpu.VMEM((BB, G_HEADS, G_HD, G_HD), F32)],
        compiler_params=_cparams(("parallel", "arbitrary")),
        name="gdn_b",
    )(u, w, qe, kd, at, eg, S0)


def _head_rmsnorm(x, gain_ref, nheads, hd):
    outs = []
    for h in range(nheads):
        xh = x[:, h * hd:(h + 1) * hd]
        ms = jnp.mean(xh * xh, axis=-1, keepdims=True)
        outs.append(xh * lax.rsqrt(ms + EPS) * gain_ref[...])
    return jnp.concatenate(outs, axis=1)


def _merge_kernel(x_ref, oa_ref, hm_ref, og_ref, az_ref, mo_ref, mz_ref, gz_ref, mg_ref,
                  wb_ref, wo_ref, mhn_ref, ghn_ref, y_ref):
    oa = oa_ref[...] * _silu(az_ref[...].astype(F32))
    om = (_head_rmsnorm(hm_ref[...], mhn_ref, M_HEADS, M_HD) * jax.nn.sigmoid(mo_ref[...].astype(F32))
          * _silu(mz_ref[...].astype(F32)))
    og = _head_rmsnorm(og_ref[...], ghn_ref, G_HEADS, G_HD) * _silu(gz_ref[...].astype(F32))
    y = jnp.zeros(y_ref.shape, F32)
    for i, br in enumerate((oa, om, og)):
        proj = jnp.dot(br.astype(BF16), wb_ref[i], preferred_element_type=F32)
        y = y + jax.nn.sigmoid(mg_ref[:, i * D_MODEL:(i + 1) * D_MODEL].astype(F32)) * proj
    y_ref[...] = x_ref[...] + jnp.dot(y.astype(BF16), wo_ref[...], preferred_element_type=F32)


def _merge_out(x2, o_a, h_m, o_g, u2, w_branch, w_out, m_hn, g_hn):
    n = x2.shape[0]
    tm = min(512, n)
    row = lambda w, off: pl.BlockSpec((tm, w), lambda i: (i, off // w))
    cst = lambda shape: pl.BlockSpec(shape, lambda i: (0,) * len(shape))
    return pl.pallas_call(
        _merge_kernel,
        grid=(n // tm,),
        in_specs=[row(D_MODEL, 0), row(W_A, 0), row(W_M, 0), row(W_G, 0),
                  row(W_A, OFF['a_z']), row(W_M, OFF['m_o']), row(W_M, OFF['m_z']), row(W_G, OFF['g_z']),
                  row(N_BRANCH * D_MODEL, OFF['merge']),
                  cst((N_BRANCH, W_A, D_MODEL)), cst((D_MODEL, D_MODEL)), cst((1, M_HD)), cst((1, G_HD))],
        out_specs=row(D_MODEL, 0),
        out_shape=jax.ShapeDtypeStruct((n, D_MODEL), F32),
        compiler_params=_cparams(("parallel",)),
        name="merge_out",
    )(x2, o_a, h_m, o_g, u2, u2, u2, u2, u2, w_branch.astype(BF16), w_out.astype(BF16),
      m_hn.reshape(1, M_HD), g_hn.reshape(1, G_HD))


def _small_t(u3, L):
    B, T, _ = u3.shape
    s = u3[:, :, OFF['small'] + SM_MIF:OFF['small'] + SM_MIF + 16].astype(F32)
    return s.reshape(B, T // L, L, 16).transpose(0, 1, 3, 2)


def _layer(x, lw, tb, past, q_off):
    B, T, _ = x.shape
    if past is None:
        tp, L = T, math.gcd(T, M_CHUNK)
    else:
        tp = -(-T // 8) * 8
        L = tp
        x = jnp.pad(x, ((0, 0), (0, tp - T), (0, 0)))
    x2 = x.reshape(B * tp, D_MODEL)
    u2 = _proj_in(x2, lw['norm_g'], lw['w_perm'], BF16 if past is None else F32)
    u3 = u2.reshape(B, tp, N_PERM)
    if past is None:
        C0 = jnp.zeros((B, M_HEADS, M_HD, M_HD), F32)
        n0 = jnp.zeros((B, M_HEADS, M_HD), F32)
        m0 = jnp.zeros((B, M_HEADS), F32)
        S0 = jnp.zeros((B, G_HEADS, G_HD, G_HD), F32)
        buf = jnp.zeros((B, CONV_W - 1, 3 * W_G), F32)
    else:
        C0, n0, m0, S0, buf = past['mC'], past['mn'], past['mm'], past['gS'], past['gconv']
    qn, rows6, *rows_t = _prep_a(u2, lw['a_qn'], lw['a_kn'], T if past is None and T % 512 == 0 else None)
    qn3 = qn.reshape(B, tp, W_A)
    if past is None:
        o_a = _nsa_prompt(qn3, rows6, u3, lw['wk'], lw['wv'], tb)
    else:
        o_a = _nsa_sample(qn3, rows6, u3, past['cmp'], past['sel'], past['win_t'], past['layer'], past['page_table'],
                          lw['wk'], lw['wv'], tb, q_off, T)
    smallt = _small_t(u3, L)
    h_m, Cn, nn, mn = _mlstm(u3, smallt, lw['m_bi'], lw['m_bf'], C0, n0, m0, L, T)
    o_g, Sn = _gdn(u3, smallt, lw['g_conv'], lw['g_A_log'], lw['g_dt_bias'], buf, S0, L, T)
    y2 = _merge_out(x2, o_a.reshape(B * tp, W_A), h_m.reshape(B * tp, W_M), o_g.reshape(B * tp, W_G), u2,
                    lw['w_branch'], lw['w_out'], lw['m_hn'], lw['g_hn'])
    if rows_t:
        new_cmp, new_sel, new_win = (rows_t[b].reshape(B, 2, A_KV, A_HD, T).transpose(0, 4, 1, 2, 3)
                                     for b in range(3))
    else:
        rows = rows6.reshape(3, 2, B, tp, A_KV, A_HD)[:, :, :, :T]
        new_cmp, new_sel, new_win = (jnp.moveaxis(rows[b], 0, 2) for b in range(3))
    g_qkv = u3[:, max(T - (CONV_W - 1), 0):T, OFF['g_qkv']:OFF['g_qkv'] + 3 * W_G].astype(F32)
    if past is None:
        win = new_win
    else:
        win = jnp.concatenate([past['win'], new_win], axis=1)
    full = jnp.concatenate([buf, g_qkv], axis=1)
    state = dict(cmp=new_cmp, sel=new_sel, win=win[:, -min(WINDOW, win.shape[1]):],
                 mC=Cn, mn=nn, mm=mn, gS=Sn, gconv=full[:, -(CONV_W - 1):])
    return y2.reshape(B, tp, D_MODEL)[:, :T], state


def kernel(x_prompt, x_sample, cache_cmp_kv, cache_sel_kv, cache_win_kv, state_mlstm_C, state_mlstm_n,
           state_mlstm_m, state_gdn_S, state_gdn_conv, page_table, norm_g, w_in, a_qn, a_kn, a_cmp_wk,
           a_cmp_wv, rel_bias, m_bi, m_bf, m_hn, g_conv, g_A_log, g_dt_bias, g_hn, w_branch, w_out):
    names = ('cmp', 'sel', 'win', 'mC', 'mn', 'mm', 'gS', 'gconv')
    st_p = {k: [] for k in names}
    st_s = {k: [] for k in names}
    past_len = page_table.shape[1] * PAGE_SIZE
    n_pool = cache_cmp_kv.shape[1]
    pos_minor = lambda c: jnp.swapaxes(c.reshape(c.shape[0], c.shape[1], c.shape[2], 2 * A_KV * A_HD), 2, 3)
    pool_cmp, pool_sel, win_t = pos_minor(cache_cmp_kv), pos_minor(cache_sel_kv), pos_minor(cache_win_kv)
    w_t = jnp.swapaxes(w_in, 1, 2)
    db, dt = x_sample.shape[0], x_sample.shape[1]
    tb_p = _prompt_tables(rel_bias, x_prompt.shape[1])
    tb_s = _sample_tables(rel_bias, past_len, -(-dt // 8) * 8, dt)
    y_p, y_s = x_prompt, x_sample
    for l in range(DEPTH):
        lw = dict(norm_g=norm_g[l], w_perm=_permute_w_in(w_t, l), a_qn=a_qn[l], a_kn=a_kn[l],
                  wk=_cmp_weights(a_cmp_wk[l]), wv=_cmp_weights(a_cmp_wv[l]),
                  m_bi=m_bi[l], m_bf=m_bf[l], m_hn=m_hn[l], g_conv=g_conv[l], g_A_log=g_A_log[l],
                  g_dt_bias=g_dt_bias[l], g_hn=g_hn[l], w_branch=w_branch[l], w_out=w_out[l])
        y_p, new_p = _layer(y_p, lw, tb_p, None, 0)
        past = dict(cmp=pool_cmp, sel=pool_sel, win_t=win_t, layer=l, win=cache_win_kv[l], page_table=page_table,
                    mC=state_mlstm_C[l], mn=state_mlstm_n[l], mm=state_mlstm_m[l],
                    gS=state_gdn_S[l], gconv=state_gdn_conv[l])
        y_s, new_s = _layer(y_s, lw, tb_s, past, past_len)
        for k in names:
            st_p[k].append(new_p[k])
            st_s[k].append(new_s[k])
    P = {k: jnp.stack(v) for k, v in st_p.items()}
    S = {k: jnp.stack(v) for k, v in st_s.items()}
    return (y_p, y_s, P['cmp'], S['cmp'], P['sel'], S['sel'], P['win'], S['win'],
            P['mC'], S['mC'], P['mn'], S['mn'], P['mm'], S['mm'], P['gS'], S['gS'], P['gconv'], S['gconv'])
```

```python
import functools
import math

import jax
import jax.numpy as jnp
from jax import lax
from jax.experimental import pallas as pl
from jax.experimental.pallas import tpu as pltpu

D_MODEL = 1024
DEPTH = 2
PAGE_SIZE = 128
A_HEADS = 8
A_KV = 2
A_REP = A_HEADS // A_KV
A_HD = 64
CMP_BLOCK = 32
CMP_STRIDE = 16
SEL_BLOCK = 64
TOP_N = 16
WINDOW = 512
Q_BLOCK = 128
N_BUCKETS = 32
MAX_DIST = 2048
M_HEADS = 4
M_HD = 128
M_CHUNK = 64
G_HEADS = 4
G_HD = 128
G_CHUNK = 64
CONV_W = 4
W_A = A_HEADS * A_HD
W_M = M_HEADS * M_HD
W_G = G_HEADS * G_HD
N_BRANCH = 3
EPS = 1e-6
NEG = -1e30
TINY = 1e-30
LOG2E = math.log2(math.e)

F32 = jnp.float32
BF16 = jnp.bfloat16
HI = lax.Precision.HIGHEST
NN = (((1,), (0,)), ((), ()))
NT = (((1,), (1,)), ((), ()))
TN = (((0,), (0,)), ((), ()))

LANES = 128
VMEM_LIMIT = 56 * 1024 * 1024

IN_ORDER = ('a_q', 'a_kv', 'a_gate', 'a_z', 'm_qkv', 'm_if', 'm_o', 'm_z', 'g_qkv', 'g_ab', 'g_z', 'merge')
IN_WIDTH = dict(a_q=W_A, a_kv=3 * 2 * A_KV * A_HD, a_gate=A_HEADS * 3, a_z=W_A, m_qkv=3 * W_M, m_if=2 * M_HEADS,
                m_o=W_M, m_z=W_M, g_qkv=3 * W_G, g_ab=2 * G_HEADS, g_z=W_G, merge=N_BRANCH * D_MODEL)
OFF = dict(merge=0, m_qkv=3072, g_qkv=4608, a_q=6144, a_z=6656, m_o=7168, m_z=7680, g_z=8192, a_kv=8704,
           small=9472)
N_PERM = 9600
SM_GATE, SM_MIF, SM_GAB = 0, 24, 32


def _cparams(sem):
    return pltpu.CompilerParams(dimension_semantics=sem, vmem_limit_bytes=VMEM_LIMIT)


def _silu(x):
    return x * jax.nn.sigmoid(x)


def _log_sigmoid(x):
    return jnp.minimum(x, 0.0) - jnp.log(1.0 + jnp.exp(-jnp.abs(x)))


def _softplus(x):
    return jnp.maximum(x, 0.0) + jnp.log(1.0 + jnp.exp(-jnp.abs(x)))


def _split(a):
    hi = a.astype(BF16)
    return hi, (a - hi.astype(F32)).astype(BF16)


def _dot3(a, b, dims=NN):
    mm = lambda x, y: lax.dot_general(x, y, dims, preferred_element_type=F32)
    return mm(a[0], b[0]) + mm(a[0], b[1]) + mm(a[1], b[0])


def _src_offsets():
    offs, off = {}, 0
    for name in IN_ORDER:
        offs[name] = off
        off += IN_WIDTH[name]
    return offs, off


def _permute_kernel(w_ref, o_ref):
    src, _ = _src_offsets()
    small = []
    for name in IN_ORDER:
        w = IN_WIDTH[name]
        rows = w_ref[src[name]:src[name] + w, :]
        if w % LANES:
            small.append(rows)
        else:
            o_ref[:, OFF[name]:OFF[name] + w] = rows.T.astype(BF16)
    used = sum(r.shape[0] for r in small)
    small.append(jnp.zeros((LANES - used, w_ref.shape[1]), F32))
    o_ref[:, OFF['small']:N_PERM] = jnp.concatenate(small, axis=0).T.astype(BF16)


def _permute_w_in(w_t, layer):
    _, n_in, d = w_t.shape
    tr = 128
    return pl.pallas_call(
        _permute_kernel,
        grid=(d // tr,),
        in_specs=[pl.BlockSpec((None, n_in, tr), lambda i: (layer, 0, i))],
        out_specs=pl.BlockSpec((tr, N_PERM), lambda i: (i, 0)),
        out_shape=jax.ShapeDtypeStruct((d, N_PERM), BF16),
        compiler_params=_cparams(("parallel",)),
        name="permute_w",
    )(w_t)


def _rel_bucket(dist):
    n = jnp.maximum(dist, 0)
    exact = N_BUCKETS // 2
    nf = jnp.maximum(n, exact).astype(F32)
    large = exact + (jnp.log(nf / exact) / math.log(MAX_DIST / exact) * (N_BUCKETS - exact)).astype(jnp.int32)
    return jnp.where(n < exact, n, jnp.minimum(large, N_BUCKETS - 1))


def _bias_kernel(thr_ref, tab_ref, d_ref, o_ref):
    n = jnp.maximum(d_ref[0], 0)
    for h in range(A_HEADS):
        acc = jnp.full(n.shape, tab_ref[h], F32)
        for k in range(1, N_BUCKETS):
            acc = jnp.where(n >= thr_ref[k], tab_ref[k * A_HEADS + h], acc)
        o_ref[0, h // A_REP, h % A_REP] = acc


def _bias_rows(rel_bias, dist, split_rows=False):
    N, Q, K = dist.shape
    nmax = 2 * MAX_DIST
    thr = jnp.sum(_rel_bucket(jnp.arange(nmax))[None, :] < jnp.arange(N_BUCKETS)[:, None], axis=1).astype(jnp.int32)
    smem = pl.BlockSpec(memory_space=pltpu.SMEM)
    out = pl.pallas_call(
        _bias_kernel,
        grid=(N,),
        in_specs=[smem, smem, pl.BlockSpec((1, Q, K), lambda i: (i, 0, 0))],
        out_specs=pl.BlockSpec((1, A_KV, A_REP, Q, K), lambda i: (i, 0, 0, 0, 0)),
        out_shape=jax.ShapeDtypeStruct((N, A_KV, A_REP, Q, K), F32),
        compiler_params=_cparams(("parallel",)),
        name="bias_rows",
    )(thr, rel_bias.astype(F32).reshape(N_BUCKETS * A_HEADS), dist.astype(jnp.int32))
    return out if split_rows else out.reshape(N, A_KV, A_REP * Q, K)


def _cmp_weights(w):
    wr = w.reshape(A_KV, 2, CMP_STRIDE, A_HD, A_HD)
    eye = jnp.eye(A_KV, dtype=w.dtype)
    full = jnp.einsum('gmjde,gh->jgdmhe', wr, eye)
    return full.reshape(CMP_STRIDE, A_KV * A_HD, 2 * A_KV * A_HD).astype(BF16)


def _cover_t(ns_pad, nch):
    s0 = jnp.arange(ns_pad)[:, None] * SEL_BLOCK
    c0 = jnp.arange(nch)[None, :] * CMP_STRIDE
    return ((c0 < s0 + SEL_BLOCK) & (s0 <= c0 + CMP_BLOCK - 1)).astype(F32)


def _gate_expand():
    col = jnp.arange(N_BRANCH * W_A)
    src = 3 * ((col % W_A) // A_HD) + col // W_A
    return (jnp.arange(LANES)[:, None] == src[None, :]).astype(BF16)


def _phase_perm(pages=2):
    n = pages * PAGE_SIZE
    r = jnp.arange(n)
    blocks = n // CMP_STRIDE
    return (jnp.arange(n)[None, :] == ((r % blocks) * CMP_STRIDE + r // blocks)[:, None]).astype(BF16)


def _prompt_tables(rel_bias, T):
    QB = Q_BLOCK
    nqb = T // QB
    nch = T // CMP_STRIDE
    ns = T // SEL_BLOCK
    ns_pad = -(-ns // 8) * 8
    i_ = jnp.arange(QB)
    t = (jnp.arange(nqb) * QB)[:, None, None] + i_[None, :, None]
    cend = (jnp.arange(nch) * CMP_STRIDE + CMP_BLOCK - 1)[None, None, :]
    d = (jnp.arange(nqb) * QB)[:, None, None] + i_[None, :, None] - i_[None, None, :]
    nw = WINDOW // QB + 1
    dt = jnp.swapaxes(d, 1, 2)
    tsel = _bias_rows(rel_bias * LOG2E, dt, split_rows=True)
    return dict(
        bcmp=_bias_rows(rel_bias, t - cend),
        tselt=tsel.transpose(0, 1, 3, 2, 4).reshape(nqb, A_KV, QB, A_REP * QB),
        covt=_cover_t(ns_pad, nch),
        wmt=jnp.where((dt[:nw] >= 0) & (dt[:nw] < WINDOW), 0.0, NEG),
        cmt=jnp.where(dt[:2] >= 0, 0.0, NEG),
        gx=_gate_expand())


def _sample_tables(rel_bias, past, tp, t_real):
    SL = past + LANES
    nblk = past // CMP_STRIDE
    ns = -(-(past + t_real) // SEL_BLOCK)
    ns_pad = -(-ns // LANES) * LANES
    WK = WINDOW + LANES
    i_ = jnp.arange(tp)
    t = past + i_
    rows = lambda dist: _bias_rows(rel_bias, dist[None])[0]
    return dict(
        bcmp=rows(t[:, None] - (jnp.arange(nblk) * CMP_STRIDE + CMP_BLOCK - 1)[None, :]),
        bsel=rows(t[:, None] - jnp.arange(SL)[None, :]),
        bwin=rows(i_[:, None] + WINDOW - jnp.arange(WK)[None, :]),
        cov=_cover_t(ns_pad, nblk).T, perm=_phase_perm())


def _proj_in_kernel(x_ref, g_ref, w_ref, o_ref, hn_ref):
    @pl.when(pl.program_id(1) == 0)
    def _():
        x = x_ref[...]
        ms = jnp.mean(x * x, axis=-1, keepdims=True)
        hn_ref[...] = (x * lax.rsqrt(ms + EPS) * g_ref[...]).astype(BF16)

    o_ref[...] = jnp.dot(hn_ref[...], w_ref[...], preferred_element_type=F32).astype(o_ref.dtype)


def _proj_in(x2, norm_g, w_perm, out_dtype):
    n = x2.shape[0]
    tm = min(2048, n)
    tn = 1920
    return pl.pallas_call(
        _proj_in_kernel,
        grid=(n // tm, N_PERM // tn),
        in_specs=[pl.BlockSpec((tm, D_MODEL), lambda i, j: (i, 0)),
                  pl.BlockSpec((1, D_MODEL), lambda i, j: (0, 0)),
                  pl.BlockSpec((D_MODEL, tn), lambda i, j: (0, j))],
        out_specs=pl.BlockSpec((tm, tn), lambda i, j: (i, j)),
        out_shape=jax.ShapeDtypeStruct((n, N_PERM), out_dtype),
        scratch_shapes=[pltpu.VMEM((tm, D_MODEL), BF16)],
        compiler_params=_cparams(("parallel", "arbitrary")),
        name="proj_in",
    )(x2, norm_g.reshape(1, D_MODEL), w_perm)


def _prep_a_kernel(q_ref, kv0_ref, kv1_ref, kv2_ref, bdq_ref, bdk_ref, qg_ref, kg_ref, qo_ref, ro_ref, *rt_ref):
    q = q_ref[...].astype(F32)
    ms = jnp.dot(q * q, bdq_ref[...], precision=HI, preferred_element_type=F32) * (1.0 / A_HD)
    qo_ref[...] = q * lax.rsqrt(ms + EPS) * qg_ref[...] * (A_HD ** -0.5)
    for b, kv_ref in enumerate((kv0_ref, kv1_ref, kv2_ref)):
        k = kv_ref[:, 0:128].astype(F32)
        ms = jnp.dot(k * k, bdk_ref[...], precision=HI, preferred_element_type=F32) * (1.0 / A_HD)
        kn = k * lax.rsqrt(ms + EPS) * kg_ref[b:b + 1, :]
        v = kv_ref[:, 128:256].astype(F32)
        ro_ref[2 * b] = kn
        ro_ref[2 * b + 1] = v
        if rt_ref:
            rt_ref[b][0:128, :] = kn.T
            rt_ref[b][128:256, :] = v.T


def _prep_a(u2, a_qn, a_kn, seq=None):
    n = u2.shape[0]
    tm = min(512, n)
    out_specs = [pl.BlockSpec((tm, W_A), lambda i: (i, 0)), pl.BlockSpec((6, tm, 128), lambda i: (0, i, 0))]
    out_shape = [jax.ShapeDtypeStruct((n, W_A), F32), jax.ShapeDtypeStruct((6, n, 128), F32)]
    if seq is not None:
        per = seq // tm
        out_specs += [pl.BlockSpec((None, 256, tm), lambda i: (i // per, 0, i % per))] * 3
        out_shape += [jax.ShapeDtypeStruct((n // seq, 256, seq), F32)] * 3
    bd = lambda w: (jnp.arange(w)[:, None] // A_HD == jnp.arange(w)[None, :] // A_HD).astype(F32)
    qg = jnp.tile(a_qn, A_HEADS).reshape(1, W_A)
    kg = jnp.tile(a_kn, (1, A_KV))
    const = lambda shape: pl.BlockSpec(shape, lambda i: (0, 0))
    kvs = lambda b: pl.BlockSpec((tm, 256), lambda i: (i, OFF['a_kv'] // 256 + b))
    return pl.pallas_call(
        _prep_a_kernel,
        grid=(n // tm,),
        in_specs=[pl.BlockSpec((tm, W_A), lambda i: (i, OFF['a_q'] // W_A)), kvs(0), kvs(1), kvs(2),
                  const((W_A, W_A)), const((128, 128)), const((1, W_A)), const((3, 128))],
        out_specs=out_specs,
        out_shape=out_shape,
        compiler_params=_cparams(("parallel",)),
        name="prep_a",
    )(u2, u2, u2, u2, bd(W_A), bd(128), qg, kg)


def _stack_heads(q, g):
    return jnp.concatenate([q[:, (A_REP * g + r) * A_HD:(A_REP * g + r + 1) * A_HD] for r in range(A_REP)], axis=0)


def _rep_rows(x):
    return jnp.concatenate([x] * A_REP, axis=0)


def _masked_softmax(s, valid):
    sm = jnp.where(valid, s, NEG)
    m = jnp.max(sm, axis=-1, keepdims=True)
    e = jnp.where(valid, jnp.exp(sm - m), 0.0)
    l = jnp.sum(e, axis=-1, keepdims=True)
    return e / jnp.maximum(l, TINY)


def _select_blocks(pc_sum, covt_ref, st_ref, t_row, ns, n_top):
    ns_pad = covt_ref.shape[0]
    imp = lax.dot_general(covt_ref[...], pc_sum, NT, precision=HI, preferred_element_type=F32)
    jj = lax.broadcasted_iota(jnp.int32, (ns_pad, LANES), 0)
    tt = jnp.broadcast_to(t_row, (ns_pad, LANES))
    cur = tt // SEL_BLOCK
    forced = (jj == 0) | (jj == cur) | (jj == cur - 1)
    future = jj * SEL_BLOCK > tt
    score = jnp.where(future, NEG, jnp.where(forced, -NEG, imp))
    score = jnp.where(jj < ns, score, -jnp.inf)
    st_ref[...] = score

    def beats(k, rank):
        row = jnp.broadcast_to(st_ref[pl.ds(k, 1), :], (ns_pad, LANES))
        b = (row > score) | ((row == score) & (k < jj))
        return rank + jnp.where(b, 1.0, 0.0)

    rank = jnp.zeros((ns_pad, LANES), F32)
    if ns <= 32:
        for k in range(ns):
            rank = beats(k, rank)
    else:
        rank = lax.fori_loop(0, ns, beats, rank)
    return jnp.where(rank < n_top, 1.0, 0.0)


def _pad_rows(x, rows):
    if x.shape[0] == rows:
        return x
    return jnp.concatenate([x, jnp.zeros((rows - x.shape[0], x.shape[1]), x.dtype)], axis=0)


def _select_rows(pc_sum, cov_ref, sel_ref, t0, ns, n_top, nq):
    Q = pc_sum.shape[0]
    ns_pad = cov_ref.shape[1]
    nk = -(-ns // 8) * 8
    imp = jnp.dot(pc_sum, cov_ref[...], precision=HI, preferred_element_type=F32)
    jj = lax.broadcasted_iota(jnp.int32, (Q, ns_pad), 1)
    tt = lax.broadcasted_iota(jnp.int32, (Q, ns_pad), 0) + t0
    cur = tt // SEL_BLOCK
    forced = (jj == 0) | (jj == cur) | (jj == cur - 1)
    future = jj * SEL_BLOCK > tt
    score = jnp.where(future, NEG, jnp.where(forced, -NEG, imp))
    score = jnp.where(jj < ns, score, -jnp.inf)
    score_col = _pad_rows(score, LANES).T
    kk = lax.broadcasted_iota(jnp.int32, (nk, ns_pad), 0)
    jl = lax.broadcasted_iota(jnp.int32, (nk, ns_pad), 1)
    sel_ref[...] = jnp.zeros(sel_ref.shape, F32)
    for i in range(nq):
        col = jnp.broadcast_to(score_col[0:nk, i:i + 1], (nk, ns_pad))
        row = jnp.broadcast_to(score[i:i + 1, :], (nk, ns_pad))
        beats = (col > row) | ((col == row) & (kk < jl))
        rank = jnp.sum(jnp.where(beats, 1.0, 0.0), axis=0, keepdims=True)
        sel_ref[i:i + 1, :] = jnp.where(rank < n_top, 1.0, 0.0)
    return sel_ref[...]


def _compress(src_ref, nrow, wk_ref, wv_ref, pbuf_ref, kc_ref, vc_ref, by_phase=False):
    acck = jnp.zeros((nrow, 256), F32)
    accv = jnp.zeros((nrow, 256), F32)
    for j in range(CMP_STRIDE):
        if by_phase:
            xk = src_ref[0, j, 0:nrow, :].astype(BF16)
            xv = src_ref[1, j, 0:nrow, :].astype(BF16)
        else:
            xk = src_ref[0, pl.ds(j, nrow, stride=CMP_STRIDE), :].astype(BF16)
            xv = src_ref[1, pl.ds(j, nrow, stride=CMP_STRIDE), :].astype(BF16)
        acck = acck + jnp.dot(xk, wk_ref[j], preferred_element_type=F32)
        accv = accv + jnp.dot(xv, wv_ref[j], preferred_element_type=F32)
    nout = kc_ref.shape[0]
    for acc, dst in ((acck, kc_ref), (accv, vc_ref)):
        pbuf_ref[0:nrow, :] = acc[:, 128:256]
        dst[...] = acc[0:nout, 0:128] + pbuf_ref[pl.ds(1, nout), :]


def _gated_sum(gates, gx_ref, ocat_ref):
    gh, gl = _split(gates)
    gexp = (jnp.dot(gh, gx_ref[...], preferred_element_type=F32)
            + jnp.dot(gl, gx_ref[...], preferred_element_type=F32))
    out = gexp[:, 0:W_A] * ocat_ref[0]
    for c in range(1, N_BRANCH):
        out = out + gexp[:, c * W_A:(c + 1) * W_A] * ocat_ref[c]
    return out


def _nsa_prompt_kernel(q_ref, rows_ref, small_ref, wk_ref, wv_ref, bcmp_ref, tselt_ref, covt_ref,
                       wmt_ref, cmt_ref, gx_ref,
                       o_ref, kc_ref, vc_ref, pbuf_ref, mb_ref, acc_ref, m_ref, l_ref, st_ref, ocat_ref, ocatt_ref,
                       *, ns, n_top):
    bi = pl.program_id(1)
    T = rows_ref.shape[1]
    nch = T // CMP_STRIDE
    QB = Q_BLOCK
    SEL, WIN = 1, 2

    @pl.when(bi == 0)
    def _():
        pbuf_ref[...] = jnp.zeros(pbuf_ref.shape, F32)
        _compress(rows_ref, nch, wk_ref, wv_ref, pbuf_ref, kc_ref, vc_ref)

    t0 = bi * QB
    q = q_ref[0]
    tc = lax.broadcasted_iota(jnp.int32, (QB, nch), 0) + t0
    cend = lax.broadcasted_iota(jnp.int32, (QB, nch), 1) * CMP_STRIDE + (CMP_BLOCK - 1)
    cvalid = _rep_rows(tc - cend >= 0)
    t_row = lax.broadcasted_iota(jnp.int32, (1, LANES), 1) + t0
    qg = [_stack_heads(q, g).astype(BF16) for g in range(A_KV)]
    qt = (q * LOG2E).T
    qgt = [jnp.concatenate([qt[(A_REP * g + r) * A_HD:(A_REP * g + r + 1) * A_HD] for r in range(A_REP)],
                           axis=1).astype(BF16) for g in range(A_KV)]

    for g in range(A_KV):
        kc = kc_ref[:, g * A_HD:(g + 1) * A_HD].astype(BF16)
        vc = vc_ref[:, g * A_HD:(g + 1) * A_HD].astype(BF16)
        s = lax.dot_general(qg[g], kc, NT, preferred_element_type=F32) + bcmp_ref[0, g]
        p_c = _masked_softmax(s, cvalid)
        o_c = jnp.dot(p_c.astype(BF16), vc, preferred_element_type=F32)
        for r in range(A_REP):
            h = A_REP * g + r
            ocat_ref[0, :, h * A_HD:(h + 1) * A_HD] = o_c[r * QB:(r + 1) * QB]
        pc_sum = p_c[0:QB] + p_c[QB:2 * QB] + p_c[2 * QB:3 * QB] + p_c[3 * QB:4 * QB]
        sel_t = _select_blocks(pc_sum, covt_ref, st_ref, t_row, ns, n_top)
        mb_ref[g] = (sel_t - 1.0) * (-NEG)

    m_ref[...] = jnp.full(m_ref.shape, NEG, F32)
    l_ref[...] = jnp.zeros(l_ref.shape, F32)
    acc_ref[...] = jnp.zeros(acc_ref.shape, F32)
    lanes4 = lambda x: jnp.concatenate([x] * A_REP, axis=1)
    half = SEL_BLOCK

    def tiles(specs):
        cat = lambda xs: xs[0] if len(xs) == 1 else jnp.concatenate(xs, axis=0)
        cis = [2 * (br - 1) + g for br, g, _, _ in specs]
        sms, vs = [], []
        for br, g, kb, nb in specs:
            ks, vv, bs, ms = [], [], [], []
            for j in range(nb):
                off = pl.multiple_of((kb + j) * QB, QB)
                ks.append(rows_ref[2 * br, pl.ds(off, QB), g * A_HD:(g + 1) * A_HD])
                vv.append(rows_ref[2 * br + 1, pl.ds(off, QB), g * A_HD:(g + 1) * A_HD])
                bs.append(tselt_ref[bi - kb - j, g])
                if br == SEL:
                    blk = (QB // half) * (kb + j)
                    mt = jnp.concatenate([jnp.broadcast_to(mb_ref[g, pl.ds(blk + a, 1), :], (half, QB))
                                          for a in range(QB // half)], axis=0)
                    ms.append(mt + cmt_ref[jnp.minimum(bi - kb, 1)] if nb == 1 else mt)
                else:
                    ms.append(wmt_ref[bi - kb - j])
            vs.append(cat(vv).astype(BF16))
            sms.append(jnp.dot(cat(ks).astype(BF16), qgt[g], preferred_element_type=F32)
                       + cat(bs) + lanes4(cat(ms)))
        m_prevs = [m_ref[ci] for ci in cis]
        m_news = [jnp.maximum(mp, jnp.max(sm, axis=0, keepdims=True)) for mp, sm in zip(m_prevs, sms)]
        ps = [jnp.exp2(sm - mn) for sm, mn in zip(sms, m_news)]
        sums = [jnp.sum(p, axis=0, keepdims=True) for p in ps]
        pvs = [lax.dot_general(v, p.astype(BF16), TN, preferred_element_type=F32) for v, p in zip(vs, ps)]
        for ci, mp, mn, sm_, pv in zip(cis, m_prevs, m_news, sums, pvs):
            alpha = jnp.exp2(mp - mn)
            l_ref[ci] = alpha * l_ref[ci] + sm_
            acc_ref[ci] = alpha * acc_ref[ci] + pv
            m_ref[ci] = mn

    def body_sel(kb, carry):
        tiles([(SEL, g, kb, 1) for g in range(A_KV)])
        return carry

    def body_sel2(pair, carry):
        tiles([(SEL, g, 2 * pair, 2) for g in range(A_KV)])
        return carry

    def body_both(kb, carry):
        tiles([(br, g, kb, 1) for g in range(A_KV) for br in (SEL, WIN)])
        return carry

    lo = jnp.maximum(bi - WINDOW // QB, 0)
    lax.fori_loop(0, lo // 2, body_sel2, 0)
    lax.fori_loop(2 * (lo // 2), lo, body_sel, 0)
    lax.fori_loop(lo, bi + 1, body_both, 0)

    for br in (SEL, WIN):
        for g in range(A_KV):
            ci = 2 * (br - 1) + g
            o_t = acc_ref[ci] / jnp.maximum(l_ref[ci], TINY)
            for r in range(A_REP):
                h = A_REP * g + r
                ocatt_ref[br - 1, h * A_HD:(h + 1) * A_HD, :] = o_t[:, r * QB:(r + 1) * QB]
        ocat_ref[br] = ocatt_ref[br - 1].T

    o_ref[0] = _gated_sum(jax.nn.sigmoid(small_ref[0].astype(F32)), gx_ref, ocat_ref)


def _nsa_prompt(qn3, rows6, u3, wk, wv, tb):
    B, T, _ = qn3.shape
    QB = Q_BLOCK
    nqb = T // QB
    nch = T // CMP_STRIDE
    ns = T // SEL_BLOCK
    ns_pad = tb['covt'].shape[0]
    n_top = min(TOP_N, ns)
    nw = tb['wmt'].shape[0]
    kern = functools.partial(_nsa_prompt_kernel, ns=ns, n_top=n_top)
    c2 = lambda shape: pl.BlockSpec(shape, lambda b, i: (0,) * len(shape))
    return pl.pallas_call(
        kern,
        grid=(B, nqb),
        in_specs=[pl.BlockSpec((1, QB, W_A), lambda b, i: (b, i, 0)),
                  pl.BlockSpec((6, T, 128), lambda b, i: (0, b, 0)),
                  pl.BlockSpec((1, QB, LANES), lambda b, i: (b, i, OFF['small'] // LANES)),
                  c2((CMP_STRIDE, 128, 256)), c2((CMP_STRIDE, 128, 256)),
                  pl.BlockSpec((1, A_KV, A_REP * QB, nch), lambda b, i: (i, 0, 0, 0)),
                  c2((nqb, A_KV, QB, A_REP * QB)),
                  c2((ns_pad, nch)), c2((nw, QB, QB)), c2((2, QB, QB)),
                  c2((LANES, N_BRANCH * W_A))],
        out_specs=pl.BlockSpec((1, QB, W_A), lambda b, i: (b, i, 0)),
        out_shape=jax.ShapeDtypeStruct((B, T, W_A), F32),
        scratch_shapes=[pltpu.VMEM((nch, 128), F32), pltpu.VMEM((nch, 128), F32),
                        pltpu.VMEM((nch + 8, 128), F32),
                        pltpu.VMEM((A_KV, ns_pad, QB), F32),
                        pltpu.VMEM((2 * A_KV, A_HD, A_REP * QB), F32),
                        pltpu.VMEM((2 * A_KV, 1, A_REP * QB), F32),
                        pltpu.VMEM((2 * A_KV, 1, A_REP * QB), F32),
                        pltpu.VMEM((ns_pad, LANES), F32),
                        pltpu.VMEM((N_BRANCH, QB, W_A), F32),
                        pltpu.VMEM((2, W_A, QB), F32)],
        compiler_params=_cparams(("parallel", "arbitrary")),
        name="nsa_prompt",
    )(qn3, rows6, u3, wk, wv, tb['bcmp'], tb['tselt'], tb['covt'], tb['wmt'], tb['cmt'], tb['gx'])


def _nsa_sample_kernel(pt_ref, *refs, pg, past, tp, t_real, ns, n_top):
    cmp_refs = refs[0:pg]
    sel_refs = refs[pg:2 * pg]
    (q_ref, rows_ref, small_ref, win_ref, wk_ref, wv_ref, bcmp_ref, bsel_ref, bwin_ref, cov_ref, perm_ref,
     o_ref, cslab, sslab, kc_ref, vc_ref, pbuf_ref, st_ref, s_ref) = refs[2 * pg:]
    p = pl.program_id(1)
    npg = pl.num_programs(1)
    n_pages = past // PAGE_SIZE
    SL = (n_pages + 1) * PAGE_SIZE
    nblk = kc_ref.shape[0]
    WK = WINDOW + LANES
    pair_blocks = 2 * PAGE_SIZE // CMP_STRIDE

    for kk in range(pg // 2):
        page = p * pg + 2 * kk
        row0 = pl.multiple_of(page * (PAGE_SIZE // CMP_STRIDE), pair_blocks)
        pair = jnp.concatenate([cmp_refs[2 * kk][...], cmp_refs[2 * kk + 1][...]], axis=1).astype(BF16)
        ordered = lax.dot_general(perm_ref[...], pair, NT, preferred_element_type=F32)
        for kv in range(2):
            for j in range(CMP_STRIDE):
                cslab[kv, j, pl.ds(row0, pair_blocks), :] = ordered[j * pair_blocks:(j + 1) * pair_blocks,
                                                                    kv * 128:(kv + 1) * 128].astype(BF16)
            for k in (2 * kk, 2 * kk + 1):
                sslab[kv, p * pg + k] = sel_refs[k][kv * 128:(kv + 1) * 128, :].astype(BF16)

    @pl.when(p == npg - 1)
    def _():
        new_t = [_pad_rows(rows_ref[i], LANES).T for i in range(2, 6)]
        zrows = jnp.zeros((15, 128), F32)
        for kv in range(2):
            for j in range(CMP_STRIDE):
                blk = jnp.concatenate([rows_ref[kv, j:j + 1, :], zrows], axis=0) if j < tp else jnp.zeros((16, 128), F32)
                cslab[kv, j, nblk:nblk + 16, :] = blk.astype(BF16)
            sslab[kv, n_pages] = new_t[kv].astype(BF16)
        pbuf_ref[...] = jnp.zeros(pbuf_ref.shape, F32)
        _compress(cslab, nblk + 16, wk_ref, wv_ref, pbuf_ref, kc_ref, vc_ref, by_phase=True)

        q = q_ref[0]
        gates = jax.nn.sigmoid(small_ref[0].astype(F32))
        R = A_REP * tp
        zq = jnp.zeros((R, A_HD), F32)
        q2 = jnp.concatenate([jnp.concatenate([_stack_heads(q, 0), zq], axis=1),
                              jnp.concatenate([zq, _stack_heads(q, 1)], axis=1)], axis=0).astype(BF16)
        rep2 = lambda x: jnp.concatenate([x] * (A_KV * A_REP), axis=0)
        ti = lax.broadcasted_iota(jnp.int32, (tp, SL), 0) + past
        causal = ti - lax.broadcasted_iota(jnp.int32, (tp, SL), 1) >= 0
        tc = lax.broadcasted_iota(jnp.int32, (tp, nblk), 0) + past
        cend = lax.broadcasted_iota(jnp.int32, (tp, nblk), 1) * CMP_STRIDE + (CMP_BLOCK - 1)
        wd = (lax.broadcasted_iota(jnp.int32, (tp, WK), 0) + WINDOW
              - lax.broadcasted_iota(jnp.int32, (tp, WK), 1))
        npt = n_pages + 1
        nck = 5 if npt % 5 == 0 else 1
        cpt = npt // nck
        ck = cpt * LANES

        s = lax.dot_general(q2, kc_ref[...].astype(BF16), NT, preferred_element_type=F32) + bcmp_ref[...]
        p_c = _masked_softmax(s, rep2(tc - cend >= 0))
        o_c = jnp.dot(p_c.astype(BF16), vc_ref[...].astype(BF16), preferred_element_type=F32)

        kmasks = []
        for g in range(A_KV):
            pg_ = p_c[g * R:(g + 1) * R]
            pc_sum = pg_[0:tp] + pg_[tp:2 * tp] + pg_[2 * tp:3 * tp] + pg_[3 * tp:4 * tp]
            sel = _select_rows(pc_sum, cov_ref, st_ref, past, ns, n_top, t_real)
            km = jnp.concatenate([jnp.broadcast_to(sel[:, b:b + 1], (tp, SEL_BLOCK))
                                  for b in range(SL // SEL_BLOCK)], axis=1)
            kmasks.append(_rep_rows(causal & (km > 0.5)))
        svalid = jnp.concatenate(kmasks, axis=0)

        def sel_t(kv, c):
            return jnp.concatenate([sslab[kv, t] for t in range(c * cpt, (c + 1) * cpt)], axis=1)

        for c in range(nck):
            s_ref[:, c * ck:(c + 1) * ck] = jnp.dot(q2, sel_t(0, c), preferred_element_type=F32)
        s_ref[...] = _masked_softmax(s_ref[...] + bsel_ref[...], svalid)
        o_s = jnp.zeros((A_KV * R, 2 * A_HD), F32)
        for c in range(nck):
            o_s = o_s + lax.dot_general(s_ref[:, c * ck:(c + 1) * ck].astype(BF16), sel_t(1, c), NT,
                                        preferred_element_type=F32)

        kw = jnp.concatenate([win_ref[0:128, :], new_t[2]], axis=1).astype(BF16)
        vw = jnp.concatenate([win_ref[128:256, :], new_t[3]], axis=1).astype(BF16)
        sw = jnp.dot(q2, kw, preferred_element_type=F32) + bwin_ref[...]
        p_w = _masked_softmax(sw, rep2((wd >= 0) & (wd < WINDOW)))
        o_w = lax.dot_general(p_w.astype(BF16), vw, NT, preferred_element_type=F32)

        for g in range(A_KV):
            gs = slice(g * A_HD, (g + 1) * A_HD)
            for r in range(A_REP):
                h = A_REP * g + r
                rs = slice(g * R + r * tp, g * R + (r + 1) * tp)
                out = (gates[:, 3 * h:3 * h + 1] * o_c[rs, gs] + gates[:, 3 * h + 1:3 * h + 2] * o_s[rs, gs]
                       + gates[:, 3 * h + 2:3 * h + 3] * o_w[rs, gs])
                o_ref[0, :, h * A_HD:(h + 1) * A_HD] = out


def _nsa_sample(qn3, rows6, u3, pool_cmp, pool_sel, win_t, layer, page_table, wk, wv, tb, past, t_real):
    B, tp, _ = qn3.shape
    n_pages = past // PAGE_SIZE
    pg = 16 if n_pages % 16 == 0 else n_pages
    npg = n_pages // pg
    SL = past + LANES
    nblk = past // CMP_STRIDE
    ns = -(-(past + t_real) // SEL_BLOCK)
    ns_pad = tb['cov'].shape[1]
    n_top = min(TOP_N, ns)
    WK = WINDOW + LANES
    kern = functools.partial(_nsa_sample_kernel, pg=pg, past=past, tp=tp, t_real=t_real, ns=ns, n_top=n_top)

    def page_spec(k):
        return pl.BlockSpec((None, None, 256, PAGE_SIZE), lambda b, p, pt: (layer, pt[b, p * pg + k], 0, 0))

    def c_(shape):
        return pl.BlockSpec(shape, lambda b, p, pt: (0,) * len(shape))

    R = A_REP * tp
    grid_spec = pltpu.PrefetchScalarGridSpec(
        num_scalar_prefetch=1,
        grid=(B, npg),
        in_specs=([page_spec(k) for k in range(pg)] + [page_spec(k) for k in range(pg)]
                  + [pl.BlockSpec((1, tp, W_A), lambda b, p, pt: (b, 0, 0)),
                     pl.BlockSpec((6, tp, 128), lambda b, p, pt: (0, b, 0)),
                     pl.BlockSpec((1, tp, LANES), lambda b, p, pt: (b, 0, OFF['small'] // LANES)),
                     pl.BlockSpec((None, None, 256, WINDOW), lambda b, p, pt: (layer, b, 0, 0)),
                     c_((CMP_STRIDE, 128, 256)), c_((CMP_STRIDE, 128, 256)),
                     c_((A_KV * R, nblk)), c_((A_KV * R, SL)), c_((A_KV * R, WK)),
                     c_((nblk, ns_pad)), c_((2 * PAGE_SIZE, 2 * PAGE_SIZE))]),
        out_specs=pl.BlockSpec((1, tp, W_A), lambda b, p, pt: (b, 0, 0)),
        scratch_shapes=[pltpu.VMEM((2, CMP_STRIDE, nblk + 16, 128), BF16),
                        pltpu.VMEM((2, n_pages + 1, 128, PAGE_SIZE), BF16),
                        pltpu.VMEM((nblk, 128), F32), pltpu.VMEM((nblk, 128), F32),
                        pltpu.VMEM((nblk + 16, 128), F32),
                        pltpu.VMEM((tp, ns_pad), F32), pltpu.VMEM((A_KV * R, SL), F32)],
    )
    return pl.pallas_call(
        kern,
        grid_spec=grid_spec,
        out_shape=jax.ShapeDtypeStruct((B, tp, W_A), F32),
        compiler_params=_cparams(("parallel", "arbitrary")),
        name="nsa_sample",
    )(page_table, *([pool_cmp] * pg), *([pool_sel] * pg), qn3, rows6, u3, win_t, wk, wv,
      *(tb[k].reshape(A_KV * R, -1) for k in ('bcmp', 'bsel', 'bwin')), tb['cov'], tb['perm'])


def _tri(L, lower_incl):
    r = lax.broadcasted_iota(jnp.int32, (L, L), 0)
    c = lax.broadcasted_iota(jnp.int32, (L, L), 1)
    return (r >= c) if lower_incl else (r > c)


def _mlstm_kernel(x_ref, small_ref, smallt_ref, brow_ref, bcol_ref, c0_ref, n0_ref, m0_ref,
                  h_ref, co_ref, no_ref, mo_ref, c_s, m_s, *, L, CB, t_real):
    c = pl.program_id(1)
    nc = pl.num_programs(1)

    @pl.when(c == 0)
    def _():
        c_s[:, :, 0:M_HD] = c0_ref[0]
        c_s[:, :, M_HD:2 * M_HD] = n0_ref[0]
        m_s[...] = m0_ref[0]

    sm = small_ref[0].astype(F32)
    low = _tri(L, True)
    lowf = low.astype(F32)
    cbs = range(CB)
    hd = range(M_HEADS)
    ch = [(cb, h) for cb in cbs for h in hd]

    li_col, b_col, b_row, li_row = [], [], [], []
    for cb in cbs:
        rs = slice(cb * L, (cb + 1) * L)
        smt = smallt_ref[0, cb]
        lic = sm[rs, SM_MIF:SM_MIF + 4] + brow_ref[0:1, 0:4]
        lfc = _log_sigmoid(sm[rs, SM_MIF + 4:SM_MIF + 8] + brow_ref[0:1, 4:8])
        lir = smt[0:4, :] + bcol_ref[0:4, :]
        lfr = _log_sigmoid(smt[4:8, :] + bcol_ref[4:8, :])
        if t_real % L:
            t0 = (c * CB + cb) * L
            tcol = lax.broadcasted_iota(jnp.int32, (L, 4), 0) + t0
            trow = lax.broadcasted_iota(jnp.int32, (4, L), 1) + t0
            lic = jnp.where(tcol < t_real, lic, NEG)
            lfc = jnp.where(tcol < t_real, lfc, 0.0)
            lir = jnp.where(trow < t_real, lir, NEG)
            lfr = jnp.where(trow < t_real, lfr, 0.0)
        li_col.append(lic)
        li_row.append(lir)
        b_col.append(jnp.dot(lowf, lfc, precision=HI, preferred_element_type=F32))
        b_row.append(lax.dot_general(lfr, lowf, NT, precision=HI, preferred_element_type=F32))

    rows = lambda cb: slice(cb * L, (cb + 1) * L)
    q = {(cb, h): x_ref[0, rows(cb), h * M_HD:(h + 1) * M_HD].astype(F32) for cb, h in ch}
    k = {(cb, h): x_ref[0, rows(cb), W_M + h * M_HD:W_M + (h + 1) * M_HD].astype(F32) * (M_HD ** -0.5)
         for cb, h in ch}
    v = {(cb, h): x_ref[0, rows(cb), 2 * W_M + h * M_HD:2 * W_M + (h + 1) * M_HD].astype(F32) for cb, h in ch}
    bc = {(cb, h): b_col[cb][:, h:h + 1] for cb, h in ch}
    Dm = {(cb, h): jnp.where(low, bc[cb, h] - b_row[cb][h:h + 1, :] + li_row[cb][h:h + 1, :], NEG) for cb, h in ch}
    dmax = {x: jnp.max(Dm[x], axis=-1, keepdims=True) for x in ch}
    qk = {x: lax.dot_general(q[x], k[x], NT, preferred_element_type=F32) for x in ch}
    bL = {x: bc[x][L - 1:L, :] for x in ch}
    wlog = {(cb, h): bL[cb, h] - bc[cb, h] + li_col[cb][:, h:h + 1] for cb, h in ch}
    wmax = {x: jnp.max(wlog[x], axis=0, keepdims=True) for x in ch}

    m_prev, m_new = {}, {}
    for h in hd:
        m = m_s[h][:, 0:1]
        for cb in cbs:
            m_prev[cb, h] = m
            m = jnp.maximum(bL[cb, h] + m, wmax[cb, h])
            m_new[cb, h] = m
    a = {x: bc[x] + m_prev[x] for x in ch}
    mt = {x: jnp.maximum(a[x], dmax[x]) for x in ch}
    S = {x: qk[x] * jnp.exp(Dm[x] - mt[x]) for x in ch}
    inter = {x: jnp.exp(a[x] - mt[x]) for x in ch}
    v1 = {x: jnp.concatenate([v[x], jnp.ones((L, M_HD), F32)], axis=1) for x in ch}
    Sv = {x: jnp.dot(S[x], v1[x], preferred_element_type=F32) for x in ch}
    dec = {x: jnp.exp(bL[x] + m_prev[x] - m_new[x]) for x in ch}
    kw = {x: k[x] * jnp.exp(wlog[x] - m_new[x]) for x in ch}
    kv = {x: lax.dot_general(kw[x], v1[x], TN, preferred_element_type=F32) for x in ch}

    CN = {h: c_s[h] for h in hd}
    for cb in cbs:
        qC = {h: jnp.dot(q[cb, h], CN[h], preferred_element_type=F32) for h in hd}
        for h in hd:
            x = (cb, h)
            num = inter[x] * qC[h] + Sv[x]
            den = num[:, M_HD:2 * M_HD]
            h_ref[0, rows(cb), h * M_HD:(h + 1) * M_HD] = (num[:, 0:M_HD]
                                                           / jnp.maximum(jnp.abs(den), jnp.exp(-mt[x])))
            CN[h] = dec[x] * CN[h] + kv[x]
    for h in hd:
        c_s[h] = CN[h]
        m_s[h] = jnp.broadcast_to(m_new[CB - 1, h], (1, LANES))

    @pl.when(c == nc - 1)
    def _():
        co_ref[0] = c_s[:, :, 0:M_HD]
        no_ref[0] = c_s[:, :, M_HD:2 * M_HD]
        mo_ref[0] = m_s[...]


def _mlstm(u3, smallt, m_bi, m_bf, C0, n0, m0, L, t_real):
    B, T, _ = u3.shape
    nc = T // L
    brow = jnp.concatenate([m_bi, m_bf]).reshape(1, 8)
    bcol = jnp.concatenate([m_bi, m_bf]).reshape(8, 1)
    n0 = jnp.broadcast_to(n0[:, :, :, None], (B, M_HEADS, M_HD, M_HD))
    m0 = jnp.broadcast_to(m0[:, :, None, None], (B, M_HEADS, 1, LANES))
    CB = 2 if nc % 2 == 0 else 1
    kern = functools.partial(_mlstm_kernel, L=L, CB=CB, t_real=t_real)
    st = lambda shape: pl.BlockSpec(shape, lambda b, c: (b,) + (0,) * (len(shape) - 1))
    h, Cn, nn, mn = pl.pallas_call(
        kern,
        grid=(B, nc // CB),
        in_specs=[pl.BlockSpec((1, CB * L, 3 * W_M), lambda b, c: (b, c, OFF['m_qkv'] // (3 * W_M))),
                  pl.BlockSpec((1, CB * L, LANES), lambda b, c: (b, c, OFF['small'] // LANES)),
                  pl.BlockSpec((1, CB, 16, L), lambda b, c: (b, c, 0, 0)),
                  pl.BlockSpec((1, 8), lambda b, c: (0, 0)), pl.BlockSpec((8, 1), lambda b, c: (0, 0)),
                  st((1, M_HEADS, M_HD, M_HD)), st((1, M_HEADS, M_HD, M_HD)), st((1, M_HEADS, 1, LANES))],
        out_specs=[pl.BlockSpec((1, CB * L, W_M), lambda b, c: (b, c, 0)),
                   st((1, M_HEADS, M_HD, M_HD)), st((1, M_HEADS, M_HD, M_HD)), st((1, M_HEADS, 1, LANES))],
        out_shape=[jax.ShapeDtypeStruct((B, T, W_M), F32),
                   jax.ShapeDtypeStruct((B, M_HEADS, M_HD, M_HD), F32),
                   jax.ShapeDtypeStruct((B, M_HEADS, M_HD, M_HD), F32),
                   jax.ShapeDtypeStruct((B, M_HEADS, 1, LANES), F32)],
        scratch_shapes=[pltpu.VMEM((M_HEADS, M_HD, 2 * M_HD), F32), pltpu.VMEM((M_HEADS, 1, LANES), F32)],
        compiler_params=_cparams(("parallel", "arbitrary")),
        name="mlstm",
    )(u3, u3, smallt, brow, bcol, C0, n0, m0)
    return h, Cn, nn[:, :, :, 0], mn[:, :, 0, 0]


def _gdn_a_kernel(x_ref, small_ref, smallt_ref, cw_ref, prow_ref, pcol_ref, buf_ref,
                  u_ref, w_ref, qe_ref, kd_ref, at_ref, eg_ref, xbuf, *, L, CB, t_real):
    c = pl.program_id(1)
    PRE = 8
    R = CB * L

    @pl.when(c == 0)
    def _():
        xbuf[0:PRE, :] = buf_ref[0]

    xbuf[PRE:PRE + R, :] = x_ref[0].astype(F32)
    conv = jnp.zeros((R, 3 * W_G), F32)
    for j in range(CONV_W):
        conv = conv + xbuf[pl.ds(PRE - (CONV_W - 1) + j, R), :] * cw_ref[j:j + 1, :]
    xbuf[0:PRE, :] = xbuf[R:R + PRE, :]
    conv = _silu(conv)

    sm = small_ref[0].astype(F32)
    g_all = -jnp.exp(prow_ref[0:1, 0:4]) * _softplus(sm[:, SM_GAB:SM_GAB + 4] + prow_ref[1:2, 0:4])
    beta_all = jax.nn.sigmoid(sm[:, SM_GAB + 4:SM_GAB + 8])
    lowf = _tri(L, True).astype(F32)

    H = G_HEADS
    lane_h = lax.broadcasted_iota(jnp.int32, (L, H * L), 1) // L
    row_p = lax.broadcasted_iota(jnp.int32, (L, H * L), 0)
    col_p = lax.broadcasted_iota(jnp.int32, (L, H * L), 1) % L
    blockmask = (lax.broadcasted_iota(jnp.int32, (H * L, H * L), 0) // L
                 == lax.broadcasted_iota(jnp.int32, (H * L, H * L), 1) // L)

    def pack_diag(full):
        out = full[0:L]
        for h in range(1, H):
            out = jnp.where(lane_h == h, full[h * L:(h + 1) * L], out)
        return out

    def block_diag(m, parts):
        if L % 16:
            return _split(jnp.where(blockmask, jnp.concatenate([m] * H, axis=0), 0.0))
        return tuple(jnp.where(blockmask, jnp.concatenate([p] * H, axis=0), jnp.zeros((), BF16))
                     for p in parts)

    a_list, rhs_list = [], []
    for cb in range(CB):
        rs = slice(cb * L, (cb + 1) * L)
        smt = smallt_ref[0, cb]
        g_col = g_all[rs]
        beta_col = beta_all[rs]
        g_row = -jnp.exp(pcol_ref[0:4, 0:1]) * _softplus(smt[8:12, :] + pcol_ref[0:4, 1:2])
        if t_real % L:
            t0 = (c * CB + cb) * L
            tcol = lax.broadcasted_iota(jnp.int32, (L, 4), 0) + t0
            trow = lax.broadcasted_iota(jnp.int32, (4, L), 1) + t0
            g_col = jnp.where(tcol < t_real, g_col, 0.0)
            beta_col = jnp.where(tcol < t_real, beta_col, 0.0)
            g_row = jnp.where(trow < t_real, g_row, 0.0)
        G_col = jnp.dot(lowf, g_col, precision=HI, preferred_element_type=F32)
        G_row = lax.dot_general(g_row, lowf, NT, precision=HI, preferred_element_type=F32)

        qs, ks, kbs, rhss = [], [], [], []
        for h in range(G_HEADS):
            hs = slice(h * G_HD, (h + 1) * G_HD)
            cq = conv[rs, h * G_HD:(h + 1) * G_HD]
            ck = conv[rs, W_G + h * G_HD:W_G + (h + 1) * G_HD]
            v = conv[rs, 2 * W_G + h * G_HD:2 * W_G + (h + 1) * G_HD]
            q = cq * lax.rsqrt(jnp.sum(cq * cq, axis=-1, keepdims=True) + EPS) * (G_HD ** -0.5)
            k = ck * lax.rsqrt(jnp.sum(ck * ck, axis=-1, keepdims=True) + EPS)
            Gc = G_col[:, h:h + 1]
            bcol = beta_col[:, h:h + 1]
            kb = k * bcol
            eG = jnp.exp(Gc)
            GL = Gc[L - 1:L, :]
            qs.append(q)
            ks.append(k)
            kbs.append(kb)
            rhss.append(jnp.concatenate([v * bcol, kb * eG], axis=1))
            qe_ref[0, rs, hs] = q * eG
            kd_ref[0, rs, hs] = k * jnp.exp(GL - Gc)
            eg_ref[0, cb, h:h + 1, :] = jnp.broadcast_to(jnp.exp(GL), (1, LANES))
        eg_ref[0, cb, G_HEADS:8, :] = jnp.zeros((8 - G_HEADS, LANES), F32)

        Gc_p = jnp.concatenate([jnp.broadcast_to(G_col[:, h:h + 1], (L, L)) for h in range(H)], axis=1)
        Gr_p = jnp.concatenate([jnp.broadcast_to(G_row[h:h + 1, :], (L, L)) for h in range(H)], axis=1)
        low_p = row_p >= col_p
        dmask = jnp.where(low_p, jnp.exp(jnp.where(low_p, Gc_p - Gr_p, 0.0)), 0.0)
        k_st = _split(jnp.concatenate(ks, axis=0))
        A = pack_diag(_dot3(_split(jnp.concatenate(kbs, axis=0)), k_st, NT)) * jnp.where(row_p > col_p, dmask, 0.0)
        attn = pack_diag(lax.dot_general(jnp.concatenate(qs, axis=0).astype(BF16), k_st[0], NT,
                                         preferred_element_type=F32)) * dmask
        at_ref[0, rs, :] = attn
        a_list.append(A)
        rhs_list.append(_split(jnp.concatenate(rhss, axis=0)))

    chunks = range(CB)
    X = [jnp.where(row_p == col_p, 1.0, 0.0) - a_list[cb] for cb in chunks]
    As = [_split(a_list[cb]) for cb in chunks]
    Pw = [_dot3(As[cb], block_diag(a_list[cb], As[cb])) for cb in chunks]
    span = 2
    while span < L:
        Ps = [_split(Pw[cb]) for cb in chunks]
        Pbd = [block_diag(Pw[cb], Ps[cb]) for cb in chunks]
        X = [X[cb] + _dot3(_split(X[cb]), Pbd[cb]) for cb in chunks]
        span *= 2
        if span < L:
            Pw = [_dot3(Ps[cb], Pbd[cb]) for cb in chunks]
    sol = [_dot3(block_diag(X[cb], _split(X[cb])), rhs_list[cb]) for cb in chunks]
    for cb in chunks:
        for h in range(H):
            rs = slice(cb * L, (cb + 1) * L)
            hs = slice(h * G_HD, (h + 1) * G_HD)
            u_ref[0, rs, hs] = sol[cb][h * L:(h + 1) * L, 0:G_HD]
            w_ref[0, rs, hs] = sol[cb][h * L:(h + 1) * L, G_HD:2 * G_HD]


def _gdn_b_kernel(u_ref, w_ref, qe_ref, kd_ref, at_ref, eg_ref, s0_ref, o_ref, so_ref, s_s, *, L, BB):
    c = pl.program_id(1)
    nc = pl.num_programs(1)

    @pl.when(c == 0)
    def _():
        s_s[...] = s0_ref[...]

    ch = [(b, h, slice(h * G_HD, (h + 1) * G_HD)) for b in range(BB) for h in range(G_HEADS)]
    S = [s_s[b, h] for b, h, _ in ch]
    r = [jnp.dot(jnp.concatenate([w_ref[b, :, hs], qe_ref[b, :, hs]], axis=0), S[i], preferred_element_type=F32)
         for i, (b, h, hs) in enumerate(ch)]
    v_new = [u_ref[b, :, hs] - r[i][0:L] for i, (b, h, hs) in enumerate(ch)]
    av = [jnp.dot(at_ref[b, :, h * L:(h + 1) * L], v_new[i], preferred_element_type=F32)
          for i, (b, h, hs) in enumerate(ch)]
    kdv = [lax.dot_general(kd_ref[b, :, hs], v_new[i], TN, preferred_element_type=F32)
           for i, (b, h, hs) in enumerate(ch)]
    for i, (b, h, hs) in enumerate(ch):
        o_ref[b, :, hs] = r[i][L:2 * L] + av[i]
        s_s[b, h] = eg_ref[b, 0, h:h + 1, :] * S[i] + kdv[i]

    @pl.when(c == nc - 1)
    def _():
        so_ref[...] = s_s[...]


def _gdn(u3, smallt, g_conv, g_A_log, g_dt_bias, buf, S0, L, t_real):
    B, T, _ = u3.shape
    nc = T // L
    CB = 4 if nc % 4 == 0 else 1
    BB = 8 if B % 8 == 0 else B
    prow = jnp.stack([g_A_log, g_dt_bias])
    pcol = jnp.stack([g_A_log, g_dt_bias], axis=1)
    buf8 = jnp.concatenate([jnp.zeros((B, 8 - (CONV_W - 1), 3 * W_G), F32), buf], axis=1)
    st = lambda shape: pl.BlockSpec(shape, lambda b, c: (b,) + (0,) * (len(shape) - 1))
    cst = lambda shape: pl.BlockSpec(shape, lambda b, c: (0,) * len(shape))
    row = lambda nb, r, w: pl.BlockSpec((nb, r, w), lambda b, c: (b, c, 0))
    tok = lambda w: jax.ShapeDtypeStruct((B, T, w), F32)
    u, w, qe, kd, at, eg = pl.pallas_call(
        functools.partial(_gdn_a_kernel, L=L, CB=CB, t_real=t_real),
        grid=(B, nc // CB),
        in_specs=[pl.BlockSpec((1, CB * L, 3 * W_G), lambda b, c: (b, c, OFF['g_qkv'] // (3 * W_G))),
                  pl.BlockSpec((1, CB * L, LANES), lambda b, c: (b, c, OFF['small'] // LANES)),
                  pl.BlockSpec((1, CB, 16, L), lambda b, c: (b, c, 0, 0)),
                  cst((CONV_W, 3 * W_G)), cst((2, 4)), cst((4, 2)), st((1, 8, 3 * W_G))],
        out_specs=[row(1, CB * L, W_G)] * 4 + [row(1, CB * L, G_HEADS * L),
                                               pl.BlockSpec((1, CB, 8, LANES), lambda b, c: (b, c, 0, 0))],
        out_shape=[tok(W_G)] * 4 + [tok(G_HEADS * L), jax.ShapeDtypeStruct((B, nc, 8, LANES), F32)],
        scratch_shapes=[pltpu.VMEM((CB * L + 8, 3 * W_G), F32)],
        compiler_params=_cparams(("parallel", "arbitrary")),
        name="gdn_a",
    )(u3, u3, smallt, g_conv, prow, pcol, buf8)
    sblk = pl.BlockSpec((BB, G_HEADS, G_HD, G_HD), lambda b, c: (b, 0, 0, 0))
    return pl.pallas_call(
        functools.partial(_gdn_b_kernel, L=L, BB=BB),
        grid=(B // BB, nc),
        in_specs=[row(BB, L, W_G)] * 4 + [row(BB, L, G_HEADS * L),
                                          pl.BlockSpec((BB, 1, 8, LANES), lambda b, c: (b, c, 0, 0)), sblk],
        out_specs=[row(BB, L, W_G), sblk],
        out_shape=[tok(W_G), jax.ShapeDtypeStruct((B, G_HEADS, G_HD, G_HD), F32)],
        scratch_shapes=[pltpu.VMEM((BB, G_HEADS, G_HD, G_HD), F32)],
        compiler_params=_cparams(("parallel", "arbitrary")),
        name="gdn_b",
    )(u, w, qe, kd, at, eg, S0)


def _head_rmsnorm(x, gain_ref, nheads, hd):
    outs = []
    for h in range(nheads):
        xh = x[:, h * hd:(h + 1) * hd]
        ms = jnp.mean(xh * xh, axis=-1, keepdims=True)
        outs.append(xh * lax.rsqrt(ms + EPS) * gain_ref[...])
    return jnp.concatenate(outs, axis=1)


def _merge_kernel(x_ref, oa_ref, hm_ref, og_ref, az_ref, mo_ref, mz_ref, gz_ref, mg_ref,
                  wb_ref, wo_ref, mhn_ref, ghn_ref, y_ref):
    oa = oa_ref[...] * _silu(az_ref[...].astype(F32))
    om = (_head_rmsnorm(hm_ref[...], mhn_ref, M_HEADS, M_HD) * jax.nn.sigmoid(mo_ref[...].astype(F32))
          * _silu(mz_ref[...].astype(F32)))
    og = _head_rmsnorm(og_ref[...], ghn_ref, G_HEADS, G_HD) * _silu(gz_ref[...].astype(F32))
    y = jnp.zeros(y_ref.shape, F32)
    for i, br in enumerate((oa, om, og)):
        proj = jnp.dot(br.astype(BF16), wb_ref[i], preferred_element_type=F32)
        y = y + jax.nn.sigmoid(mg_ref[:, i * D_MODEL:(i + 1) * D_MODEL].astype(F32)) * proj
    y_ref[...] = x_ref[...] + jnp.dot(y.astype(BF16), wo_ref[...], preferred_element_type=F32)


def _merge_out(x2, o_a, h_m, o_g, u2, w_branch, w_out, m_hn, g_hn):
    n = x2.shape[0]
    tm = min(512, n)
    row = lambda w, off: pl.BlockSpec((tm, w), lambda i: (i, off // w))
    cst = lambda shape: pl.BlockSpec(shape, lambda i: (0,) * len(shape))
    return pl.pallas_call(
        _merge_kernel,
        grid=(n // tm,),
        in_specs=[row(D_MODEL, 0), row(W_A, 0), row(W_M, 0), row(W_G, 0),
                  row(W_A, OFF['a_z']), row(W_M, OFF['m_o']), row(W_M, OFF['m_z']), row(W_G, OFF['g_z']),
                  row(N_BRANCH * D_MODEL, OFF['merge']),
                  cst((N_BRANCH, W_A, D_MODEL)), cst((D_MODEL, D_MODEL)), cst((1, M_HD)), cst((1, G_HD))],
        out_specs=row(D_MODEL, 0),
        out_shape=jax.ShapeDtypeStruct((n, D_MODEL), F32),
        compiler_params=_cparams(("parallel",)),
        name="merge_out",
    )(x2, o_a, h_m, o_g, u2, u2, u2, u2, u2, w_branch.astype(BF16), w_out.astype(BF16),
      m_hn.reshape(1, M_HD), g_hn.reshape(1, G_HD))


def _small_t(u3, L):
    B, T, _ = u3.shape
    s = u3[:, :, OFF['small'] + SM_MIF:OFF['small'] + SM_MIF + 16].astype(F32)
    return s.reshape(B, T // L, L, 16).transpose(0, 1, 3, 2)


def _layer(x, lw, tb, past, q_off):
    B, T, _ = x.shape
    if past is None:
        tp, L = T, math.gcd(T, M_CHUNK)
    else:
        tp = -(-T // 8) * 8
        L = tp
        x = jnp.pad(x, ((0, 0), (0, tp - T), (0, 0)))
    x2 = x.reshape(B * tp, D_MODEL)
    u2 = _proj_in(x2, lw['norm_g'], lw['w_perm'], BF16 if past is None else F32)
    u3 = u2.reshape(B, tp, N_PERM)
    if past is None:
        C0 = jnp.zeros((B, M_HEADS, M_HD, M_HD), F32)
        n0 = jnp.zeros((B, M_HEADS, M_HD), F32)
        m0 = jnp.zeros((B, M_HEADS), F32)
        S0 = jnp.zeros((B, G_HEADS, G_HD, G_HD), F32)
        buf = jnp.zeros((B, CONV_W - 1, 3 * W_G), F32)
    else:
        C0, n0, m0, S0, buf = past['mC'], past['mn'], past['mm'], past['gS'], past['gconv']
    qn, rows6, *rows_t = _prep_a(u2, lw['a_qn'], lw['a_kn'], T if past is None and T % 512 == 0 else None)
    qn3 = qn.reshape(B, tp, W_A)
    if past is None:
        o_a = _nsa_prompt(qn3, rows6, u3, lw['wk'], lw['wv'], tb)
    else:
        o_a = _nsa_sample(qn3, rows6, u3, past['cmp'], past['sel'], past['win_t'], past['layer'], past['page_table'],
                          lw['wk'], lw['wv'], tb, q_off, T)
    smallt = _small_t(u3, L)
    h_m, Cn, nn, mn = _mlstm(u3, smallt, lw['m_bi'], lw['m_bf'], C0, n0, m0, L, T)
    o_g, Sn = _gdn(u3, smallt, lw['g_conv'], lw['g_A_log'], lw['g_dt_bias'], buf, S0, L, T)
    y2 = _merge_out(x2, o_a.reshape(B * tp, W_A), h_m.reshape(B * tp, W_M), o_g.reshape(B * tp, W_G), u2,
                    lw['w_branch'], lw['w_out'], lw['m_hn'], lw['g_hn'])
    if rows_t:
        new_cmp, new_sel, new_win = (rows_t[b].reshape(B, 2, A_KV, A_HD, T).transpose(0, 4, 1, 2, 3)
                                     for b in range(3))
    else:
        rows = rows6.reshape(3, 2, B, tp, A_KV, A_HD)[:, :, :, :T]
        new_cmp, new_sel, new_win = (jnp.moveaxis(rows[b], 0, 2) for b in range(3))
    g_qkv = u3[:, max(T - (CONV_W - 1), 0):T, OFF['g_qkv']:OFF['g_qkv'] + 3 * W_G].astype(F32)
    if past is None:
        win = new_win
    else:
        win = jnp.concatenate([past['win'], new_win], axis=1)
    full = jnp.concatenate([buf, g_qkv], axis=1)
    state = dict(cmp=new_cmp, sel=new_sel, win=win[:, -min(WINDOW, win.shape[1]):],
                 mC=Cn, mn=nn, mm=mn, gS=Sn, gconv=full[:, -(CONV_W - 1):])
    return y2.reshape(B, tp, D_MODEL)[:, :T], state


def kernel(x_prompt, x_sample, cache_cmp_kv, cache_sel_kv, cache_win_kv, state_mlstm_C, state_mlstm_n,
           state_mlstm_m, state_gdn_S, state_gdn_conv, page_table, norm_g, w_in, a_qn, a_kn, a_cmp_wk,
           a_cmp_wv, rel_bias, m_bi, m_bf, m_hn, g_conv, g_A_log, g_dt_bias, g_hn, w_branch, w_out):
    names = ('cmp', 'sel', 'win', 'mC', 'mn', 'mm', 'gS', 'gconv')
    st_p = {k: [] for k in names}
    st_s = {k: [] for k in names}
    past_len = page_table.shape[1] * PAGE_SIZE
    n_pool = cache_cmp_kv.shape[1]
    pos_minor = lambda c: jnp.swapaxes(c.reshape(c.shape[0], c.shape[1], c.shape[2], 2 * A_KV * A_HD), 2, 3)
    pool_cmp, pool_sel, win_t = pos_minor(cache_cmp_kv), pos_minor(cache_sel_kv), pos_minor(cache_win_kv)
    w_t = jnp.swapaxes(w_in, 1, 2)
    db, dt = x_sample.shape[0], x_sample.shape[1]
    tb_p = _prompt_tables(rel_bias, x_prompt.shape[1])
    tb_s = _sample_tables(rel_bias, past_len, -(-dt // 8) * 8, dt)
    y_p, y_s = x_prompt, x_sample
    for l in range(DEPTH):
        lw = dict(norm_g=norm_g[l], w_perm=_permute_w_in(w_t, l), a_qn=a_qn[l], a_kn=a_kn[l],
                  wk=_cmp_weights(a_cmp_wk[l]), wv=_cmp_weights(a_cmp_wv[l]),
                  m_bi=m_bi[l], m_bf=m_bf[l], m_hn=m_hn[l], g_conv=g_conv[l], g_A_log=g_A_log[l],
                  g_dt_bias=g_dt_bias[l], g_hn=g_hn[l], w_branch=w_branch[l], w_out=w_out[l])
        y_p, new_p = _layer(y_p, lw, tb_p, None, 0)
        past = dict(cmp=pool_cmp, sel=pool_sel, win_t=win_t, layer=l, win=cache_win_kv[l], page_table=page_table,
                    mC=state_mlstm_C[l], mn=state_mlstm_n[l], mm=state_mlstm_m[l],
                    gS=state_gdn_S[l], gconv=state_gdn_conv[l])
        y_s, new_s = _layer(y_s, lw, tb_s, past, past_len)
        for k in names:
            st_p[k].append(new_p[k])
            st_s[k].append(new_s[k])
    P = {k: jnp.stack(v) for k, v in st_p.items()}
    S = {k: jnp.stack(v) for k, v in st_s.items()}
    return (y_p, y_s, P['cmp'], S['cmp'], P['sel'], S['sel'], P['win'], S['win'],
            P['mC'], S['mC'], P['mn'], S['mn'], P['mm'], S['mm'], P['gS'], S['gS'], P['gconv'], S['gconv'])
```

```python
import functools
import math

import jax
import jax.numpy as jnp
from jax import lax
from jax.experimental import pallas as pl
from jax.experimental.pallas import tpu as pltpu

D_MODEL = 1024
DEPTH = 2
PAGE_SIZE = 128
A_HEADS = 8
A_KV = 2
A_REP = A_HEADS // A_KV
A_HD = 64
CMP_BLOCK = 32
CMP_STRIDE = 16
SEL_BLOCK = 64
TOP_N = 16
WINDOW = 512
Q_BLOCK = 128
N_BUCKETS = 32
MAX_DIST = 2048
M_HEADS = 4
M_HD = 128
M_CHUNK = 64
G_HEADS = 4
G_HD = 128
G_CHUNK = 64
CONV_W = 4
W_A = A_HEADS * A_HD
W_M = M_HEADS * M_HD
W_G = G_HEADS * G_HD
N_BRANCH = 3
EPS = 1e-6
NEG = -1e30
TINY = 1e-30
LOG2E = math.log2(math.e)

F32 = jnp.float32
BF16 = jnp.bfloat16
HI = lax.Precision.HIGHEST
NN = (((1,), (0,)), ((), ()))
NT = (((1,), (1,)), ((), ()))
TN = (((0,), (0,)), ((), ()))

LANES = 128
VMEM_LIMIT = 56 * 1024 * 1024

IN_ORDER = ('a_q', 'a_kv', 'a_gate', 'a_z', 'm_qkv', 'm_if', 'm_o', 'm_z', 'g_qkv', 'g_ab', 'g_z', 'merge')
IN_WIDTH = dict(a_q=W_A, a_kv=3 * 2 * A_KV * A_HD, a_gate=A_HEADS * 3, a_z=W_A, m_qkv=3 * W_M, m_if=2 * M_HEADS,
                m_o=W_M, m_z=W_M, g_qkv=3 * W_G, g_ab=2 * G_HEADS, g_z=W_G, merge=N_BRANCH * D_MODEL)
OFF = dict(merge=0, m_qkv=3072, g_qkv=4608, a_q=6144, a_z=6656, m_o=7168, m_z=7680, g_z=8192, a_kv=8704,
           small=9472)
N_PERM = 9600
SM_GATE, SM_MIF, SM_GAB = 0, 24, 32


def _cparams(sem):
    return pltpu.CompilerParams(dimension_semantics=sem, vmem_limit_bytes=VMEM_LIMIT)


def _silu(x):
    return x * jax.nn.sigmoid(x)


def _log_sigmoid(x):
    return jnp.minimum(x, 0.0) - jnp.log(1.0 + jnp.exp(-jnp.abs(x)))


def _softplus(x):
    return jnp.maximum(x, 0.0) + jnp.log(1.0 + jnp.exp(-jnp.abs(x)))


def _split(a):
    hi = a.astype(BF16)
    return hi, (a - hi.astype(F32)).astype(BF16)


def _dot3(a, b, dims=NN):
    mm = lambda x, y: lax.dot_general(x, y, dims, preferred_element_type=F32)
    return mm(a[0], b[0]) + mm(a[0], b[1]) + mm(a[1], b[0])


def _src_offsets():
    offs, off = {}, 0
    for name in IN_ORDER:
        offs[name] = off
        off += IN_WIDTH[name]
    return offs, off


def _permute_kernel(w_ref, o_ref):
    src, _ = _src_offsets()
    small = []
    for name in IN_ORDER:
        w = IN_WIDTH[name]
        rows = w_ref[src[name]:src[name] + w, :]
        if w % LANES:
            small.append(rows)
        else:
            o_ref[:, OFF[name]:OFF[name] + w] = rows.T.astype(BF16)
    used = sum(r.shape[0] for r in small)
    small.append(jnp.zeros((LANES - used, w_ref.shape[1]), F32))
    o_ref[:, OFF['small']:N_PERM] = jnp.concatenate(small, axis=0).T.astype(BF16)


def _permute_w_in(w_t, layer):
    _, n_in, d = w_t.shape
    tr = 128
    return pl.pallas_call(
        _permute_kernel,
        grid=(d // tr,),
        in_specs=[pl.BlockSpec((None, n_in, tr), lambda i: (layer, 0, i))],
        out_specs=pl.BlockSpec((tr, N_PERM), lambda i: (i, 0)),
        out_shape=jax.ShapeDtypeStruct((d, N_PERM), BF16),
        compiler_params=_cparams(("parallel",)),
        name="permute_w",
    )(w_t)


def _rel_bucket(dist):
    n = jnp.maximum(dist, 0)
    exact = N_BUCKETS // 2
    nf = jnp.maximum(n, exact).astype(F32)
    large = exact + (jnp.log(nf / exact) / math.log(MAX_DIST / exact) * (N_BUCKETS - exact)).astype(jnp.int32)
    return jnp.where(n < exact, n, jnp.minimum(large, N_BUCKETS - 1))


def _bias_kernel(thr_ref, tab_ref, d_ref, o_ref):
    n = jnp.maximum(d_ref[0], 0)
    for h in range(A_HEADS):
        acc = jnp.full(n.shape, tab_ref[h], F32)
        for k in range(1, N_BUCKETS):
            acc = jnp.where(n >= thr_ref[k], tab_ref[k * A_HEADS + h], acc)
        o_ref[0, h // A_REP, h % A_REP] = acc


def _bias_rows(rel_bias, dist, split_rows=False):
    N, Q, K = dist.shape
    nmax = 2 * MAX_DIST
    thr = jnp.sum(_rel_bucket(jnp.arange(nmax))[None, :] < jnp.arange(N_BUCKETS)[:, None], axis=1).astype(jnp.int32)
    smem = pl.BlockSpec(memory_space=pltpu.SMEM)
    out = pl.pallas_call(
        _bias_kernel,
        grid=(N,),
        in_specs=[smem, smem, pl.BlockSpec((1, Q, K), lambda i: (i, 0, 0))],
        out_specs=pl.BlockSpec((1, A_KV, A_REP, Q, K), lambda i: (i, 0, 0, 0, 0)),
        out_shape=jax.ShapeDtypeStruct((N, A_KV, A_REP, Q, K), F32),
        compiler_params=_cparams(("parallel",)),
        name="bias_rows",
    )(thr, rel_bias.astype(F32).reshape(N_BUCKETS * A_HEADS), dist.astype(jnp.int32))
    return out if split_rows else out.reshape(N, A_KV, A_REP * Q, K)


def _cmp_weights(w):
    wr = w.reshape(A_KV, 2, CMP_STRIDE, A_HD, A_HD)
    eye = jnp.eye(A_KV, dtype=w.dtype)
    full = jnp.einsum('gmjde,gh->jgdmhe', wr, eye)
    return full.reshape(CMP_STRIDE, A_KV * A_HD, 2 * A_KV * A_HD).astype(BF16)


def _cover_t(ns_pad, nch):
    s0 = jnp.arange(ns_pad)[:, None] * SEL_BLOCK
    c0 = jnp.arange(nch)[None, :] * CMP_STRIDE
    return ((c0 < s0 + SEL_BLOCK) & (s0 <= c0 + CMP_BLOCK - 1)).astype(F32)


def _gate_expand():
    col = jnp.arange(N_BRANCH * W_A)
    src = 3 * ((col % W_A) // A_HD) + col // W_A
    return (jnp.arange(LANES)[:, None] == src[None, :]).astype(BF16)


def _phase_perm(pages=2):
    n = pages * PAGE_SIZE
    r = jnp.arange(n)
    blocks = n // CMP_STRIDE
    return (jnp.arange(n)[None, :] == ((r % blocks) * CMP_STRIDE + r // blocks)[:, None]).astype(BF16)


def _prompt_tables(rel_bias, T):
    QB = Q_BLOCK
    nqb = T // QB
    nch = T // CMP_STRIDE
    ns = T // SEL_BLOCK
    ns_pad = -(-ns // 8) * 8
    i_ = jnp.arange(QB)
    t = (jnp.arange(nqb) * QB)[:, None, None] + i_[None, :, None]
    cend = (jnp.arange(nch) * CMP_STRIDE + CMP_BLOCK - 1)[None, None, :]
    d = (jnp.arange(nqb) * QB)[:, None, None] + i_[None, :, None] - i_[None, None, :]
    nw = min(WINDOW // QB + 1, nqb)
    dt = jnp.swapaxes(d, 1, 2)
    tsel = _bias_rows(rel_bias * LOG2E, dt, split_rows=True)
    return dict(
        bcmp=_bias_rows(rel_bias, t - cend),
        tselt=tsel.transpose(0, 1, 3, 2, 4).reshape(nqb, A_KV, QB, A_REP * QB),
        covt=_cover_t(ns_pad, nch),
        twin=(tsel[:nw] + jnp.where((dt[:nw] >= 0) & (dt[:nw] < WINDOW), 0.0, NEG)[:, None, None]
              ).transpose(0, 1, 3, 2, 4).reshape(nw, A_KV, QB, A_REP * QB),
        cmt=jnp.where(dt[:2] >= 0, 0.0, NEG),
        gx=_gate_expand())


def _sample_tables(rel_bias, past, tp, t_real):
    SL = past + LANES
    nblk = past // CMP_STRIDE
    ns = -(-(past + t_real) // SEL_BLOCK)
    ns_pad = -(-ns // LANES) * LANES
    WK = WINDOW + LANES
    i_ = jnp.arange(tp)
    t = past + i_
    rows = lambda dist: _bias_rows(rel_bias, dist[None])[0]
    return dict(
        bcmp=rows(t[:, None] - (jnp.arange(nblk) * CMP_STRIDE + CMP_BLOCK - 1)[None, :]),
        bsel=rows(t[:, None] - jnp.arange(SL)[None, :]),
        bwin=rows(i_[:, None] + WINDOW - jnp.arange(WK)[None, :]),
        cov=_cover_t(ns_pad, nblk).T, perm=_phase_perm())


def _proj_in_kernel(x_ref, g_ref, w_ref, o_ref, hn_ref):
    @pl.when(pl.program_id(1) == 0)
    def _():
        x = x_ref[...]
        ms = jnp.mean(x * x, axis=-1, keepdims=True)
        hn_ref[...] = (x * lax.rsqrt(ms + EPS) * g_ref[...]).astype(BF16)

    o_ref[...] = jnp.dot(hn_ref[...], w_ref[...], preferred_element_type=F32).astype(o_ref.dtype)


def _proj_in(x2, norm_g, w_perm, out_dtype):
    n = x2.shape[0]
    tm = min(2048, n)
    tn = 1920
    return pl.pallas_call(
        _proj_in_kernel,
        grid=(n // tm, N_PERM // tn),
        in_specs=[pl.BlockSpec((tm, D_MODEL), lambda i, j: (i, 0)),
                  pl.BlockSpec((1, D_MODEL), lambda i, j: (0, 0)),
                  pl.BlockSpec((D_MODEL, tn), lambda i, j: (0, j))],
        out_specs=pl.BlockSpec((tm, tn), lambda i, j: (i, j)),
        out_shape=jax.ShapeDtypeStruct((n, N_PERM), out_dtype),
        scratch_shapes=[pltpu.VMEM((tm, D_MODEL), BF16)],
        compiler_params=_cparams(("parallel", "arbitrary")),
        name="proj_in",
    )(x2, norm_g.reshape(1, D_MODEL), w_perm)


def _prep_a_kernel(q_ref, kv0_ref, kv1_ref, kv2_ref, bdq_ref, bdk_ref, qg_ref, kg_ref, qo_ref, ro_ref, *rt_ref):
    def group_mean(x2, bd_ref):
        hi, lo = _split(x2)
        return (jnp.dot(hi, bd_ref[...], preferred_element_type=F32)
                + jnp.dot(lo, bd_ref[...], preferred_element_type=F32)) * (1.0 / A_HD)

    q = q_ref[...].astype(F32)
    ms = group_mean(q * q, bdq_ref)
    qo_ref[...] = q * lax.rsqrt(ms + EPS) * qg_ref[...] * (A_HD ** -0.5)
    for b, kv_ref in enumerate((kv0_ref, kv1_ref, kv2_ref)):
        k = kv_ref[:, 0:128].astype(F32)
        ms = group_mean(k * k, bdk_ref)
        kn = k * lax.rsqrt(ms + EPS) * kg_ref[b:b + 1, :]
        v = kv_ref[:, 128:256].astype(F32)
        ro_ref[2 * b] = kn
        ro_ref[2 * b + 1] = v
        if rt_ref:
            rt_ref[b][0:128, :] = kn.T
            rt_ref[b][128:256, :] = v.T


def _prep_a(u2, a_qn, a_kn, seq=None):
    n = u2.shape[0]
    tm = min(512, n)
    out_specs = [pl.BlockSpec((tm, W_A), lambda i: (i, 0)), pl.BlockSpec((6, tm, 128), lambda i: (0, i, 0))]
    out_shape = [jax.ShapeDtypeStruct((n, W_A), F32), jax.ShapeDtypeStruct((6, n, 128), F32)]
    if seq is not None:
        per = seq // tm
        out_specs += [pl.BlockSpec((None, 256, tm), lambda i: (i // per, 0, i % per))] * 3
        out_shape += [jax.ShapeDtypeStruct((n // seq, 256, seq), F32)] * 3
    bd = lambda w: (jnp.arange(w)[:, None] // A_HD == jnp.arange(w)[None, :] // A_HD).astype(BF16)
    qg = jnp.tile(a_qn, A_HEADS).reshape(1, W_A)
    kg = jnp.tile(a_kn, (1, A_KV))
    const = lambda shape: pl.BlockSpec(shape, lambda i: (0, 0))
    kvs = lambda b: pl.BlockSpec((tm, 256), lambda i: (i, OFF['a_kv'] // 256 + b))
    return pl.pallas_call(
        _prep_a_kernel,
        grid=(n // tm,),
        in_specs=[pl.BlockSpec((tm, W_A), lambda i: (i, OFF['a_q'] // W_A)), kvs(0), kvs(1), kvs(2),
                  const((W_A, W_A)), const((128, 128)), const((1, W_A)), const((3, 128))],
        out_specs=out_specs,
        out_shape=out_shape,
        compiler_params=_cparams(("parallel",)),
        name="prep_a",
    )(u2, u2, u2, u2, bd(W_A), bd(128), qg, kg)


def _stack_heads(q, g):
    return jnp.concatenate([q[:, (A_REP * g + r) * A_HD:(A_REP * g + r + 1) * A_HD] for r in range(A_REP)], axis=0)


def _rep_rows(x):
    return jnp.concatenate([x] * A_REP, axis=0)


def _masked_softmax(s, valid):
    sm = jnp.where(valid, s, NEG)
    m = jnp.max(sm, axis=-1, keepdims=True)
    e = jnp.where(valid, jnp.exp(sm - m), 0.0)
    l = jnp.sum(e, axis=-1, keepdims=True)
    return e / jnp.maximum(l, TINY)


def _select_blocks(pc_sum, covt_ref, st_ref, t_row, ns, n_top):
    ns_pad = covt_ref.shape[0]
    imp = lax.dot_general(covt_ref[...], pc_sum, NT, precision=HI, preferred_element_type=F32)
    jj = lax.broadcasted_iota(jnp.int32, (ns_pad, LANES), 0)
    tt = jnp.broadcast_to(t_row, (ns_pad, LANES))
    cur = tt // SEL_BLOCK
    forced = (jj == 0) | (jj == cur) | (jj == cur - 1)
    future = jj * SEL_BLOCK > tt
    score = jnp.where(future, NEG, jnp.where(forced, -NEG, imp))
    score = jnp.where(jj < ns, score, -jnp.inf)
    st_ref[...] = score

    def beats(k, rank):
        row = jnp.broadcast_to(st_ref[pl.ds(k, 1), :], (ns_pad, LANES))
        b = (row > score) | ((row == score) & (k < jj))
        return rank + jnp.where(b, 1.0, 0.0)

    rank = jnp.zeros((ns_pad, LANES), F32)
    if ns <= 32:
        for k in range(ns):
            rank = beats(k, rank)
    else:
        rank = lax.fori_loop(0, ns, beats, rank)
    return jnp.where(rank < n_top, 1.0, 0.0)


def _pad_rows(x, rows):
    if x.shape[0] == rows:
        return x
    return jnp.concatenate([x, jnp.zeros((rows - x.shape[0], x.shape[1]), x.dtype)], axis=0)


def _select_rows(pc_sum, cov_ref, sel_ref, t0, ns, n_top, nq):
    Q = pc_sum.shape[0]
    ns_pad = cov_ref.shape[1]
    nk = -(-ns // 8) * 8
    imp = jnp.dot(pc_sum, cov_ref[...], precision=HI, preferred_element_type=F32)
    jj = lax.broadcasted_iota(jnp.int32, (Q, ns_pad), 1)
    tt = lax.broadcasted_iota(jnp.int32, (Q, ns_pad), 0) + t0
    cur = tt // SEL_BLOCK
    forced = (jj == 0) | (jj == cur) | (jj == cur - 1)
    future = jj * SEL_BLOCK > tt
    score = jnp.where(future, NEG, jnp.where(forced, -NEG, imp))
    score = jnp.where(jj < ns, score, -jnp.inf)
    score_col = _pad_rows(score, LANES).T
    kk = lax.broadcasted_iota(jnp.int32, (nk, ns_pad), 0)
    jl = lax.broadcasted_iota(jnp.int32, (nk, ns_pad), 1)
    sel_ref[...] = jnp.zeros(sel_ref.shape, F32)
    for i in range(nq):
        col = jnp.broadcast_to(score_col[0:nk, i:i + 1], (nk, ns_pad))
        row = jnp.broadcast_to(score[i:i + 1, :], (nk, ns_pad))
        beats = (col > row) | ((col == row) & (kk < jl))
        rank = jnp.sum(jnp.where(beats, 1.0, 0.0), axis=0, keepdims=True)
        sel_ref[i:i + 1, :] = jnp.where(rank < n_top, 1.0, 0.0)
    return sel_ref[...]


def _compress(src_ref, nrow, wk_ref, wv_ref, pbuf_ref, kc_ref, vc_ref, by_phase=False):
    acck = jnp.zeros((nrow, 256), F32)
    accv = jnp.zeros((nrow, 256), F32)
    for j in range(CMP_STRIDE):
        if by_phase:
            xk = src_ref[0, j, 0:nrow, :].astype(BF16)
            xv = src_ref[1, j, 0:nrow, :].astype(BF16)
        else:
            xk = src_ref[0, pl.ds(j, nrow, stride=CMP_STRIDE), :].astype(BF16)
            xv = src_ref[1, pl.ds(j, nrow, stride=CMP_STRIDE), :].astype(BF16)
        acck = acck + jnp.dot(xk, wk_ref[j], preferred_element_type=F32)
        accv = accv + jnp.dot(xv, wv_ref[j], preferred_element_type=F32)
    nout = kc_ref.shape[0]
    for acc, dst in ((acck, kc_ref), (accv, vc_ref)):
        pbuf_ref[0:nrow, :] = acc[:, 128:256]
        dst[...] = acc[0:nout, 0:128] + pbuf_ref[pl.ds(1, nout), :]


def _gated_sum(gates, gx_ref, ocat_ref):
    gh, gl = _split(gates)
    gexp = (jnp.dot(gh, gx_ref[...], preferred_element_type=F32)
            + jnp.dot(gl, gx_ref[...], preferred_element_type=F32))
    out = gexp[:, 0:W_A] * ocat_ref[0]
    for c in range(1, N_BRANCH):
        out = out + gexp[:, c * W_A:(c + 1) * W_A] * ocat_ref[c]
    return out


def _nsa_prompt_kernel(q_ref, rows_ref, small_ref, wk_ref, wv_ref, bcmp_ref, tselt_ref, covt_ref,
                       twin_ref, cmt_ref, gx_ref,
                       o_ref, kc_ref, vc_ref, pbuf_ref, mb_ref, acc_ref, m_ref, l_ref, st_ref, ocat_ref, ocatt_ref,
                       *, ns, n_top):
    bi = pl.program_id(1)
    T = rows_ref.shape[1]
    nch = T // CMP_STRIDE
    QB = Q_BLOCK
    SEL, WIN = 1, 2

    @pl.when(bi == 0)
    def _():
        pbuf_ref[...] = jnp.zeros(pbuf_ref.shape, F32)
        _compress(rows_ref, nch, wk_ref, wv_ref, pbuf_ref, kc_ref, vc_ref)

    t0 = bi * QB
    q = q_ref[0]
    tc = lax.broadcasted_iota(jnp.int32, (QB, nch), 0) + t0
    cend = lax.broadcasted_iota(jnp.int32, (QB, nch), 1) * CMP_STRIDE + (CMP_BLOCK - 1)
    cvalid = _rep_rows(tc - cend >= 0)
    t_row = lax.broadcasted_iota(jnp.int32, (1, LANES), 1) + t0
    qg = [_stack_heads(q, g).astype(BF16) for g in range(A_KV)]
    qt = (q * LOG2E).T
    qgt = [jnp.concatenate([qt[(A_REP * g + r) * A_HD:(A_REP * g + r + 1) * A_HD] for r in range(A_REP)],
                           axis=1).astype(BF16) for g in range(A_KV)]

    for g in range(A_KV):
        kc = kc_ref[:, g * A_HD:(g + 1) * A_HD].astype(BF16)
        vc = vc_ref[:, g * A_HD:(g + 1) * A_HD].astype(BF16)
        s = lax.dot_general(qg[g], kc, NT, preferred_element_type=F32) + bcmp_ref[0, g]
        p_c = _masked_softmax(s, cvalid)
        o_c = jnp.dot(p_c.astype(BF16), vc, preferred_element_type=F32)
        for r in range(A_REP):
            h = A_REP * g + r
            ocat_ref[0, :, h * A_HD:(h + 1) * A_HD] = o_c[r * QB:(r + 1) * QB]
        pc_sum = p_c[0:QB] + p_c[QB:2 * QB] + p_c[2 * QB:3 * QB] + p_c[3 * QB:4 * QB]
        sel_t = _select_blocks(pc_sum, covt_ref, st_ref, t_row, ns, n_top)
        mb_ref[g] = (sel_t - 1.0) * (-NEG)

    m_ref[...] = jnp.full(m_ref.shape, NEG, F32)
    l_ref[...] = jnp.zeros(l_ref.shape, F32)
    acc_ref[...] = jnp.zeros(acc_ref.shape, F32)
    lanes4 = lambda x: jnp.concatenate([x] * A_REP, axis=1)
    half = SEL_BLOCK

    def tiles(specs):
        cat = lambda xs: xs[0] if len(xs) == 1 else jnp.concatenate(xs, axis=0)
        cis = [2 * (br - 1) + g for br, g, _, _ in specs]
        sms, vs = [], []
        for br, g, kb, nb in specs:
            ks, vv, bs, ms = [], [], [], []
            for j in range(nb):
                off = pl.multiple_of((kb + j) * QB, QB)
                ks.append(rows_ref[2 * br, pl.ds(off, QB), g * A_HD:(g + 1) * A_HD])
                vv.append(rows_ref[2 * br + 1, pl.ds(off, QB), g * A_HD:(g + 1) * A_HD])
                if br == SEL:
                    bs.append(tselt_ref[bi - kb - j, g])
                    blk = (QB // half) * (kb + j)
                    mt = jnp.concatenate([jnp.broadcast_to(mb_ref[g, pl.ds(blk + a, 1), :], (half, QB))
                                          for a in range(QB // half)], axis=0)
                    ms.append(mt + cmt_ref[jnp.minimum(bi - kb, 1)] if nb == 1 else mt)
                else:
                    bs.append(twin_ref[bi - kb - j, g])
            vs.append(cat(vv).astype(BF16))
            sc = jnp.dot(cat(ks).astype(BF16), qgt[g], preferred_element_type=F32) + cat(bs)
            sms.append(sc + lanes4(cat(ms)) if ms else sc)
        m_prevs = [m_ref[ci] for ci in cis]
        m_news = [jnp.maximum(mp, jnp.max(sm, axis=0, keepdims=True)) for mp, sm in zip(m_prevs, sms)]
        ps = [jnp.exp2(sm - mn) for sm, mn in zip(sms, m_news)]
        sums = [jnp.sum(p, axis=0, keepdims=True) for p in ps]
        pvs = [lax.dot_general(v, p.astype(BF16), TN, preferred_element_type=F32) for v, p in zip(vs, ps)]
        for ci, mp, mn, sm_, pv in zip(cis, m_prevs, m_news, sums, pvs):
            alpha = jnp.exp2(mp - mn)
            l_ref[ci] = alpha * l_ref[ci] + sm_
            acc_ref[ci] = alpha * acc_ref[ci] + pv
            m_ref[ci] = mn

    def body_sel(kb, carry):
        tiles([(SEL, g, kb, 1) for g in range(A_KV)])
        return carry

    def body_sel2(pair, carry):
        tiles([(SEL, g, 2 * pair, 2) for g in range(A_KV)])
        return carry

    def body_both(kb, carry):
        tiles([(br, g, kb, 1) for g in range(A_KV) for br in (SEL, WIN)])
        return carry

    lo = jnp.maximum(bi - WINDOW // QB, 0)
    lax.fori_loop(0, lo // 2, body_sel2, 0)
    lax.fori_loop(2 * (lo // 2), lo, body_sel, 0)
    lax.fori_loop(lo, bi + 1, body_both, 0)

    for br in (SEL, WIN):
        for g in range(A_KV):
            ci = 2 * (br - 1) + g
            o_t = acc_ref[ci] / jnp.maximum(l_ref[ci], TINY)
            for r in range(A_REP):
                h = A_REP * g + r
                ocatt_ref[br - 1, h * A_HD:(h + 1) * A_HD, :] = o_t[:, r * QB:(r + 1) * QB]
        ocat_ref[br] = ocatt_ref[br - 1].T

    o_ref[0] = _gated_sum(jax.nn.sigmoid(small_ref[0].astype(F32)), gx_ref, ocat_ref)


def _nsa_prompt(qn3, rows6, u3, wk, wv, tb):
    B, T, _ = qn3.shape
    QB = Q_BLOCK
    nqb = T // QB
    nch = T // CMP_STRIDE
    ns = T // SEL_BLOCK
    ns_pad = tb['covt'].shape[0]
    n_top = min(TOP_N, ns)
    nw = tb['twin'].shape[0]
    kern = functools.partial(_nsa_prompt_kernel, ns=ns, n_top=n_top)
    c2 = lambda shape: pl.BlockSpec(shape, lambda b, i: (0,) * len(shape))
    return pl.pallas_call(
        kern,
        grid=(B, nqb),
        in_specs=[pl.BlockSpec((1, QB, W_A), lambda b, i: (b, i, 0)),
                  pl.BlockSpec((6, T, 128), lambda b, i: (0, b, 0)),
                  pl.BlockSpec((1, QB, LANES), lambda b, i: (b, i, OFF['small'] // LANES)),
                  c2((CMP_STRIDE, 128, 256)), c2((CMP_STRIDE, 128, 256)),
                  pl.BlockSpec((1, A_KV, A_REP * QB, nch), lambda b, i: (i, 0, 0, 0)),
                  c2((nqb, A_KV, QB, A_REP * QB)),
                  c2((ns_pad, nch)), c2((nw, A_KV, QB, A_REP * QB)), c2((2, QB, QB)),
                  c2((LANES, N_BRANCH * W_A))],
        out_specs=pl.BlockSpec((1, QB, W_A), lambda b, i: (b, i, 0)),
        out_shape=jax.ShapeDtypeStruct((B, T, W_A), F32),
        scratch_shapes=[pltpu.VMEM((nch, 128), F32), pltpu.VMEM((nch, 128), F32),
                        pltpu.VMEM((nch + 8, 128), F32),
                        pltpu.VMEM((A_KV, ns_pad, QB), F32),
                        pltpu.VMEM((2 * A_KV, A_HD, A_REP * QB), F32),
                        pltpu.VMEM((2 * A_KV, 1, A_REP * QB), F32),
                        pltpu.VMEM((2 * A_KV, 1, A_REP * QB), F32),
                        pltpu.VMEM((ns_pad, LANES), F32),
                        pltpu.VMEM((N_BRANCH, QB, W_A), F32),
                        pltpu.VMEM((2, W_A, QB), F32)],
        compiler_params=_cparams(("parallel", "arbitrary")),
        name="nsa_prompt",
    )(qn3, rows6, u3, wk, wv, tb['bcmp'], tb['tselt'], tb['covt'], tb['twin'], tb['cmt'], tb['gx'])


def _nsa_sample_kernel(pt_ref, *refs, pg, past, tp, t_real, ns, n_top):
    cmp_refs = refs[0:pg]
    sel_refs = refs[pg:2 * pg]
    (q_ref, rows_ref, small_ref, win_ref, wk_ref, wv_ref, bcmp_ref, bsel_ref, bwin_ref, cov_ref, perm_ref,
     o_ref, cslab, sslab, kc_ref, vc_ref, pbuf_ref, st_ref, s_ref) = refs[2 * pg:]
    p = pl.program_id(1)
    npg = pl.num_programs(1)
    n_pages = past // PAGE_SIZE
    SL = (n_pages + 1) * PAGE_SIZE
    nblk = kc_ref.shape[0]
    WK = WINDOW + LANES
    pair_blocks = 2 * PAGE_SIZE // CMP_STRIDE

    for kk in range(pg // 2):
        page = p * pg + 2 * kk
        row0 = pl.multiple_of(page * (PAGE_SIZE // CMP_STRIDE), pair_blocks)
        pair = jnp.concatenate([cmp_refs[2 * kk][...], cmp_refs[2 * kk + 1][...]], axis=1).astype(BF16)
        ordered = lax.dot_general(perm_ref[...], pair, NT, preferred_element_type=F32)
        for kv in range(2):
            for j in range(CMP_STRIDE):
                cslab[kv, j, pl.ds(row0, pair_blocks), :] = ordered[j * pair_blocks:(j + 1) * pair_blocks,
                                                                    kv * 128:(kv + 1) * 128].astype(BF16)
            for k in (2 * kk, 2 * kk + 1):
                sslab[kv, p * pg + k] = sel_refs[k][kv * 128:(kv + 1) * 128, :].astype(BF16)

    @pl.when(p == npg - 1)
    def _():
        new_t = [_pad_rows(rows_ref[i], LANES).T for i in range(2, 6)]
        zrows = jnp.zeros((15, 128), F32)
        for kv in range(2):
            for j in range(CMP_STRIDE):
                blk = jnp.concatenate([rows_ref[kv, j:j + 1, :], zrows], axis=0) if j < tp else jnp.zeros((16, 128), F32)
                cslab[kv, j, nblk:nblk + 16, :] = blk.astype(BF16)
            sslab[kv, n_pages] = new_t[kv].astype(BF16)
        pbuf_ref[...] = jnp.zeros(pbuf_ref.shape, F32)
        _compress(cslab, nblk + 16, wk_ref, wv_ref, pbuf_ref, kc_ref, vc_ref, by_phase=True)

        q = q_ref[0]
        gates = jax.nn.sigmoid(small_ref[0].astype(F32))
        R = A_REP * tp
        zq = jnp.zeros((R, A_HD), F32)
        q2 = jnp.concatenate([jnp.concatenate([_stack_heads(q, 0), zq], axis=1),
                              jnp.concatenate([zq, _stack_heads(q, 1)], axis=1)], axis=0).astype(BF16)
        rep2 = lambda x: jnp.concatenate([x] * (A_KV * A_REP), axis=0)
        ti = lax.broadcasted_iota(jnp.int32, (tp, SL), 0) + past
        causal = ti - lax.broadcasted_iota(jnp.int32, (tp, SL), 1) >= 0
        tc = lax.broadcasted_iota(jnp.int32, (tp, nblk), 0) + past
        cend = lax.broadcasted_iota(jnp.int32, (tp, nblk), 1) * CMP_STRIDE + (CMP_BLOCK - 1)
        wd = (lax.broadcasted_iota(jnp.int32, (tp, WK), 0) + WINDOW
              - lax.broadcasted_iota(jnp.int32, (tp, WK), 1))
        npt = n_pages + 1
        nck = 5 if npt % 5 == 0 else 1
        cpt = npt // nck
        ck = cpt * LANES

        s = lax.dot_general(q2, kc_ref[...].astype(BF16), NT, preferred_element_type=F32) + bcmp_ref[...]
        p_c = _masked_softmax(s, rep2(tc - cend >= 0))
        o_c = jnp.dot(p_c.astype(BF16), vc_ref[...].astype(BF16), preferred_element_type=F32)

        kmasks = []
        for g in range(A_KV):
            pg_ = p_c[g * R:(g + 1) * R]
            pc_sum = pg_[0:tp] + pg_[tp:2 * tp] + pg_[2 * tp:3 * tp] + pg_[3 * tp:4 * tp]
            sel = _select_rows(pc_sum, cov_ref, st_ref, past, ns, n_top, t_real)
            km = jnp.concatenate([jnp.broadcast_to(sel[:, b:b + 1], (tp, SEL_BLOCK))
                                  for b in range(SL // SEL_BLOCK)], axis=1)
            kmasks.append(_rep_rows(causal & (km > 0.5)))
        svalid = jnp.concatenate(kmasks, axis=0)

        def sel_t(kv, c):
            return jnp.concatenate([sslab[kv, t] for t in range(c * cpt, (c + 1) * cpt)], axis=1)

        for c in range(nck):
            s_ref[:, c * ck:(c + 1) * ck] = jnp.dot(q2, sel_t(0, c), preferred_element_type=F32)
        s_ref[...] = _masked_softmax(s_ref[...] + bsel_ref[...], svalid)
        o_s = jnp.zeros((A_KV * R, 2 * A_HD), F32)
        for c in range(nck):
            o_s = o_s + lax.dot_general(s_ref[:, c * ck:(c + 1) * ck].astype(BF16), sel_t(1, c), NT,
                                        preferred_element_type=F32)

        kw = jnp.concatenate([win_ref[0:128, :], new_t[2]], axis=1).astype(BF16)
        vw = jnp.concatenate([win_ref[128:256, :], new_t[3]], axis=1).astype(BF16)
        sw = jnp.dot(q2, kw, preferred_element_type=F32) + bwin_ref[...]
        p_w = _masked_softmax(sw, rep2((wd >= 0) & (wd < WINDOW)))
        o_w = lax.dot_general(p_w.astype(BF16), vw, NT, preferred_element_type=F32)

        for g in range(A_KV):
            gs = slice(g * A_HD, (g + 1) * A_HD)
            for r in range(A_REP):
                h = A_REP * g + r
                rs = slice(g * R + r * tp, g * R + (r + 1) * tp)
                out = (gates[:, 3 * h:3 * h + 1] * o_c[rs, gs] + gates[:, 3 * h + 1:3 * h + 2] * o_s[rs, gs]
                       + gates[:, 3 * h + 2:3 * h + 3] * o_w[rs, gs])
                o_ref[0, :, h * A_HD:(h + 1) * A_HD] = out


def _nsa_sample(qn3, rows6, u3, pool_cmp, pool_sel, win_t, layer, page_table, wk, wv, tb, past, t_real):
    B, tp, _ = qn3.shape
    n_pages = past // PAGE_SIZE
    pg = 16 if n_pages % 16 == 0 else n_pages
    npg = n_pages // pg
    SL = past + LANES
    nblk = past // CMP_STRIDE
    ns = -(-(past + t_real) // SEL_BLOCK)
    ns_pad = tb['cov'].shape[1]
    n_top = min(TOP_N, ns)
    WK = WINDOW + LANES
    kern = functools.partial(_nsa_sample_kernel, pg=pg, past=past, tp=tp, t_real=t_real, ns=ns, n_top=n_top)

    def page_spec(k):
        return pl.BlockSpec((None, None, 256, PAGE_SIZE), lambda b, p, pt: (layer, pt[b, p * pg + k], 0, 0))

    def c_(shape):
        return pl.BlockSpec(shape, lambda b, p, pt: (0,) * len(shape))

    R = A_REP * tp
    grid_spec = pltpu.PrefetchScalarGridSpec(
        num_scalar_prefetch=1,
        grid=(B, npg),
        in_specs=([page_spec(k) for k in range(pg)] + [page_spec(k) for k in range(pg)]
                  + [pl.BlockSpec((1, tp, W_A), lambda b, p, pt: (b, 0, 0)),
                     pl.BlockSpec((6, tp, 128), lambda b, p, pt: (0, b, 0)),
                     pl.BlockSpec((1, tp, LANES), lambda b, p, pt: (b, 0, OFF['small'] // LANES)),
                     pl.BlockSpec((None, None, 256, WINDOW), lambda b, p, pt: (layer, b, 0, 0)),
                     c_((CMP_STRIDE, 128, 256)), c_((CMP_STRIDE, 128, 256)),
                     c_((A_KV * R, nblk)), c_((A_KV * R, SL)), c_((A_KV * R, WK)),
                     c_((nblk, ns_pad)), c_((2 * PAGE_SIZE, 2 * PAGE_SIZE))]),
        out_specs=pl.BlockSpec((1, tp, W_A), lambda b, p, pt: (b, 0, 0)),
        scratch_shapes=[pltpu.VMEM((2, CMP_STRIDE, nblk + 16, 128), BF16),
                        pltpu.VMEM((2, n_pages + 1, 128, PAGE_SIZE), BF16),
                        pltpu.VMEM((nblk, 128), F32), pltpu.VMEM((nblk, 128), F32),
                        pltpu.VMEM((nblk + 16, 128), F32),
                        pltpu.VMEM((tp, ns_pad), F32), pltpu.VMEM((A_KV * R, SL), F32)],
    )
    return pl.pallas_call(
        kern,
        grid_spec=grid_spec,
        out_shape=jax.ShapeDtypeStruct((B, tp, W_A), F32),
        compiler_params=_cparams(("parallel", "arbitrary")),
        name="nsa_sample",
    )(page_table, *([pool_cmp] * pg), *([pool_sel] * pg), qn3, rows6, u3, win_t, wk, wv,
      *(tb[k].reshape(A_KV * R, -1) for k in ('bcmp', 'bsel', 'bwin')), tb['cov'], tb['perm'])


def _tri(L, lower_incl):
    r = lax.broadcasted_iota(jnp.int32, (L, L), 0)
    c = lax.broadcasted_iota(jnp.int32, (L, L), 1)
    return (r >= c) if lower_incl else (r > c)


def _mlstm_kernel(x_ref, small_ref, smallt_ref, brow_ref, bcol_ref, c0_ref, n0_ref, m0_ref,
                  h_ref, co_ref, no_ref, mo_ref, c_s, m_s, *, L, CB, t_real):
    c = pl.program_id(1)
    nc = pl.num_programs(1)

    @pl.when(c == 0)
    def _():
        c_s[:, :, 0:M_HD] = c0_ref[0]
        c_s[:, :, M_HD:2 * M_HD] = n0_ref[0]
        m_s[...] = m0_ref[0]

    sm = small_ref[0].astype(F32)
    low = _tri(L, True)
    lowf = low.astype(F32)
    cbs = range(CB)
    hd = range(M_HEADS)
    ch = [(cb, h) for cb in cbs for h in hd]

    li_col, b_col, b_row, li_row = [], [], [], []
    for cb in cbs:
        rs = slice(cb * L, (cb + 1) * L)
        smt = smallt_ref[0, cb]
        lic = sm[rs, SM_MIF:SM_MIF + 4] + brow_ref[0:1, 0:4]
        lfc = _log_sigmoid(sm[rs, SM_MIF + 4:SM_MIF + 8] + brow_ref[0:1, 4:8])
        lir = smt[0:4, :] + bcol_ref[0:4, :]
        lfr = _log_sigmoid(smt[4:8, :] + bcol_ref[4:8, :])
        if t_real % L:
            t0 = (c * CB + cb) * L
            tcol = lax.broadcasted_iota(jnp.int32, (L, 4), 0) + t0
            trow = lax.broadcasted_iota(jnp.int32, (4, L), 1) + t0
            lic = jnp.where(tcol < t_real, lic, NEG)
            lfc = jnp.where(tcol < t_real, lfc, 0.0)
            lir = jnp.where(trow < t_real, lir, NEG)
            lfr = jnp.where(trow < t_real, lfr, 0.0)
        li_col.append(lic)
        li_row.append(lir)
        b_col.append(jnp.dot(lowf, lfc, precision=HI, preferred_element_type=F32))
        b_row.append(lax.dot_general(lfr, lowf, NT, precision=HI, preferred_element_type=F32))

    rows = lambda cb: slice(cb * L, (cb + 1) * L)
    q = {(cb, h): x_ref[0, rows(cb), h * M_HD:(h + 1) * M_HD].astype(F32) for cb, h in ch}
    k = {(cb, h): x_ref[0, rows(cb), W_M + h * M_HD:W_M + (h + 1) * M_HD].astype(F32) * (M_HD ** -0.5)
         for cb, h in ch}
    v = {(cb, h): x_ref[0, rows(cb), 2 * W_M + h * M_HD:2 * W_M + (h + 1) * M_HD].astype(F32) for cb, h in ch}
    bc = {(cb, h): b_col[cb][:, h:h + 1] for cb, h in ch}
    Dm = {(cb, h): jnp.where(low, bc[cb, h] - b_row[cb][h:h + 1, :] + li_row[cb][h:h + 1, :], NEG) for cb, h in ch}
    dmax = {x: jnp.max(Dm[x], axis=-1, keepdims=True) for x in ch}
    qk = {x: lax.dot_general(q[x], k[x], NT, preferred_element_type=F32) for x in ch}
    bL = {x: bc[x][L - 1:L, :] for x in ch}
    wlog = {(cb, h): bL[cb, h] - bc[cb, h] + li_col[cb][:, h:h + 1] for cb, h in ch}
    wmax = {x: jnp.max(wlog[x], axis=0, keepdims=True) for x in ch}

    m_prev, m_new = {}, {}
    for h in hd:
        m = m_s[h][:, 0:1]
        for cb in cbs:
            m_prev[cb, h] = m
            m = jnp.maximum(bL[cb, h] + m, wmax[cb, h])
            m_new[cb, h] = m
    a = {x: bc[x] + m_prev[x] for x in ch}
    mt = {x: jnp.maximum(a[x], dmax[x]) for x in ch}
    S = {x: qk[x] * jnp.exp(Dm[x] - mt[x]) for x in ch}
    inter = {x: jnp.exp(a[x] - mt[x]) for x in ch}
    v1 = {x: jnp.concatenate([v[x], jnp.ones((L, M_HD), F32)], axis=1) for x in ch}
    Sv = {x: jnp.dot(S[x], v1[x], preferred_element_type=F32) for x in ch}
    dec = {x: jnp.exp(bL[x] + m_prev[x] - m_new[x]) for x in ch}
    kw = {x: k[x] * jnp.exp(wlog[x] - m_new[x]) for x in ch}
    kv = {x: lax.dot_general(kw[x], v1[x], TN, preferred_element_type=F32) for x in ch}

    CN = {h: c_s[h] for h in hd}
    for cb in cbs:
        qC = {h: jnp.dot(q[cb, h], CN[h], preferred_element_type=F32) for h in hd}
        for h in hd:
            x = (cb, h)
            num = inter[x] * qC[h] + Sv[x]
            den = num[:, M_HD:2 * M_HD]
            h_ref[0, rows(cb), h * M_HD:(h + 1) * M_HD] = (num[:, 0:M_HD]
                                                           / jnp.maximum(jnp.abs(den), jnp.exp(-mt[x])))
            CN[h] = dec[x] * CN[h] + kv[x]
    for h in hd:
        c_s[h] = CN[h]
        m_s[h] = jnp.broadcast_to(m_new[CB - 1, h], (1, LANES))

    @pl.when(c == nc - 1)
    def _():
        co_ref[0] = c_s[:, :, 0:M_HD]
        no_ref[0] = c_s[:, :, M_HD:2 * M_HD]
        mo_ref[0] = m_s[...]


def _mlstm(u3, smallt, m_bi, m_bf, C0, n0, m0, L, t_real):
    B, T, _ = u3.shape
    nc = T // L
    brow = jnp.concatenate([m_bi, m_bf]).reshape(1, 8)
    bcol = jnp.concatenate([m_bi, m_bf]).reshape(8, 1)
    n0 = jnp.broadcast_to(n0[:, :, :, None], (B, M_HEADS, M_HD, M_HD))
    m0 = jnp.broadcast_to(m0[:, :, None, None], (B, M_HEADS, 1, LANES))
    CB = 2 if nc % 2 == 0 else 1
    kern = functools.partial(_mlstm_kernel, L=L, CB=CB, t_real=t_real)
    st = lambda shape: pl.BlockSpec(shape, lambda b, c: (b,) + (0,) * (len(shape) - 1))
    h, Cn, nn, mn = pl.pallas_call(
        kern,
        grid=(B, nc // CB),
        in_specs=[pl.BlockSpec((1, CB * L, 3 * W_M), lambda b, c: (b, c, OFF['m_qkv'] // (3 * W_M))),
                  pl.BlockSpec((1, CB * L, LANES), lambda b, c: (b, c, OFF['small'] // LANES)),
                  pl.BlockSpec((1, CB, 16, L), lambda b, c: (b, c, 0, 0)),
                  pl.BlockSpec((1, 8), lambda b, c: (0, 0)), pl.BlockSpec((8, 1), lambda b, c: (0, 0)),
                  st((1, M_HEADS, M_HD, M_HD)), st((1, M_HEADS, M_HD, M_HD)), st((1, M_HEADS, 1, LANES))],
        out_specs=[pl.BlockSpec((1, CB * L, W_M), lambda b, c: (b, c, 0)),
                   st((1, M_HEADS, M_HD, M_HD)), st((1, M_HEADS, M_HD, M_HD)), st((1, M_HEADS, 1, LANES))],
        out_shape=[jax.ShapeDtypeStruct((B, T, W_M), F32),
                   jax.ShapeDtypeStruct((B, M_HEADS, M_HD, M_HD), F32),
                   jax.ShapeDtypeStruct((B, M_HEADS, M_HD, M_HD), F32),
                   jax.ShapeDtypeStruct((B, M_HEADS, 1, LANES), F32)],
        scratch_shapes=[pltpu.VMEM((M_HEADS, M_HD, 2 * M_HD), F32), pltpu.VMEM((M_HEADS, 1, LANES), F32)],
        compiler_params=_cparams(("parallel", "arbitrary")),
        name="mlstm",
    )(u3, u3, smallt, brow, bcol, C0, n0, m0)
    return h, Cn, nn[:, :, :, 0], mn[:, :, 0, 0]


def _gdn_a_kernel(x_ref, small_ref, smallt_ref, cw_ref, prow_ref, pcol_ref, buf_ref,
                  u_ref, w_ref, qe_ref, kd_ref, at_ref, eg_ref, xbuf, *, L, CB, t_real):
    c = pl.program_id(1)
    PRE = 8
    R = CB * L

    @pl.when(c == 0)
    def _():
        xbuf[0:PRE, :] = buf_ref[0]

    xbuf[PRE:PRE + R, :] = x_ref[0].astype(F32)
    conv = jnp.zeros((R, 3 * W_G), F32)
    for j in range(CONV_W):
        conv = conv + xbuf[pl.ds(PRE - (CONV_W - 1) + j, R), :] * cw_ref[j:j + 1, :]
    xbuf[0:PRE, :] = xbuf[R:R + PRE, :]
    conv = _silu(conv)

    sm = small_ref[0].astype(F32)
    g_all = -jnp.exp(prow_ref[0:1, 0:4]) * _softplus(sm[:, SM_GAB:SM_GAB + 4] + prow_ref[1:2, 0:4])
    beta_all = jax.nn.sigmoid(sm[:, SM_GAB + 4:SM_GAB + 8])
    lowf = _tri(L, True).astype(F32)

    H = G_HEADS
    lane_h = lax.broadcasted_iota(jnp.int32, (L, H * L), 1) // L
    row_p = lax.broadcasted_iota(jnp.int32, (L, H * L), 0)
    col_p = lax.broadcasted_iota(jnp.int32, (L, H * L), 1) % L
    blockmask = (lax.broadcasted_iota(jnp.int32, (H * L, H * L), 0) // L
                 == lax.broadcasted_iota(jnp.int32, (H * L, H * L), 1) // L)

    def pack_diag(full):
        out = full[0:L]
        for h in range(1, H):
            out = jnp.where(lane_h == h, full[h * L:(h + 1) * L], out)
        return out

    def block_diag(m, parts):
        if L % 16:
            return _split(jnp.where(blockmask, jnp.concatenate([m] * H, axis=0), 0.0))
        return tuple(jnp.where(blockmask, jnp.concatenate([p] * H, axis=0), jnp.zeros((), BF16))
                     for p in parts)

    a_list, rhs_list = [], []
    for cb in range(CB):
        rs = slice(cb * L, (cb + 1) * L)
        smt = smallt_ref[0, cb]
        g_col = g_all[rs]
        beta_col = beta_all[rs]
        g_row = -jnp.exp(pcol_ref[0:4, 0:1]) * _softplus(smt[8:12, :] + pcol_ref[0:4, 1:2])
        if t_real % L:
            t0 = (c * CB + cb) * L
            tcol = lax.broadcasted_iota(jnp.int32, (L, 4), 0) + t0
            trow = lax.broadcasted_iota(jnp.int32, (4, L), 1) + t0
            g_col = jnp.where(tcol < t_real, g_col, 0.0)
            beta_col = jnp.where(tcol < t_real, beta_col, 0.0)
            g_row = jnp.where(trow < t_real, g_row, 0.0)
        G_col = jnp.dot(lowf, g_col, precision=HI, preferred_element_type=F32)
        G_row = lax.dot_general(g_row, lowf, NT, precision=HI, preferred_element_type=F32)

        qs, ks, kbs, rhss = [], [], [], []
        for h in range(G_HEADS):
            hs = slice(h * G_HD, (h + 1) * G_HD)
            cq = conv[rs, h * G_HD:(h + 1) * G_HD]
            ck = conv[rs, W_G + h * G_HD:W_G + (h + 1) * G_HD]
            v = conv[rs, 2 * W_G + h * G_HD:2 * W_G + (h + 1) * G_HD]
            q = cq * lax.rsqrt(jnp.sum(cq * cq, axis=-1, keepdims=True) + EPS) * (G_HD ** -0.5)
            k = ck * lax.rsqrt(jnp.sum(ck * ck, axis=-1, keepdims=True) + EPS)
            Gc = G_col[:, h:h + 1]
            bcol = beta_col[:, h:h + 1]
            kb = k * bcol
            eG = jnp.exp(Gc)
            GL = Gc[L - 1:L, :]
            qs.append(q)
            ks.append(k)
            kbs.append(kb)
            rhss.append(jnp.concatenate([v * bcol, kb * eG], axis=1))
            qe_ref[0, rs, hs] = q * eG
            kd_ref[0, rs, hs] = k * jnp.exp(GL - Gc)
            eg_ref[0, cb, h:h + 1, :] = jnp.broadcast_to(jnp.exp(GL), (1, LANES))
        eg_ref[0, cb, G_HEADS:8, :] = jnp.zeros((8 - G_HEADS, LANES), F32)

        Gc_p = jnp.concatenate([jnp.broadcast_to(G_col[:, h:h + 1], (L, L)) for h in range(H)], axis=1)
        Gr_p = jnp.concatenate([jnp.broadcast_to(G_row[h:h + 1, :], (L, L)) for h in range(H)], axis=1)
        low_p = row_p >= col_p
        dmask = jnp.where(low_p, jnp.exp(jnp.where(low_p, Gc_p - Gr_p, 0.0)), 0.0)
        k_st = _split(jnp.concatenate(ks, axis=0))
        A = pack_diag(_dot3(_split(jnp.concatenate(kbs, axis=0)), k_st, NT)) * jnp.where(row_p > col_p, dmask, 0.0)
        attn = pack_diag(lax.dot_general(jnp.concatenate(qs, axis=0).astype(BF16), k_st[0], NT,
                                         preferred_element_type=F32)) * dmask
        at_ref[0, rs, :] = attn
        a_list.append(A)
        rhs_list.append(_split(jnp.concatenate(rhss, axis=0)))

    chunks = range(CB)
    X = [jnp.where(row_p == col_p, 1.0, 0.0) - a_list[cb] for cb in chunks]
    As = [_split(a_list[cb]) for cb in chunks]
    Pw = [_dot3(As[cb], block_diag(a_list[cb], As[cb])) for cb in chunks]
    span = 2
    while span < L:
        Ps = [_split(Pw[cb]) for cb in chunks]
        Pbd = [block_diag(Pw[cb], Ps[cb]) for cb in chunks]
        X = [X[cb] + _dot3(_split(X[cb]), Pbd[cb]) for cb in chunks]
        span *= 2
        if span < L:
            Pw = [_dot3(Ps[cb], Pbd[cb]) for cb in chunks]
    sol = [_dot3(block_diag(X[cb], _split(X[cb])), rhs_list[cb]) for cb in chunks]
    for cb in chunks:
        for h in range(H):
            rs = slice(cb * L, (cb + 1) * L)
            hs = slice(h * G_HD, (h + 1) * G_HD)
            u_ref[0, rs, hs] = sol[cb][h * L:(h + 1) * L, 0:G_HD]
            w_ref[0, rs, hs] = sol[cb][h * L:(h + 1) * L, G_HD:2 * G_HD]


def _gdn_b_kernel(u_ref, w_ref, qe_ref, kd_ref, at_ref, eg_ref, s0_ref, o_ref, so_ref, s_s, *, L, BB):
    c = pl.program_id(1)
    nc = pl.num_programs(1)

    @pl.when(c == 0)
    def _():
        s_s[...] = s0_ref[...]

    ch = [(b, h, slice(h * G_HD, (h + 1) * G_HD)) for b in range(BB) for h in range(G_HEADS)]
    S = [s_s[b, h] for b, h, _ in ch]
    r = [jnp.dot(jnp.concatenate([w_ref[b, :, hs], qe_ref[b, :, hs]], axis=0), S[i], preferred_element_type=F32)
         for i, (b, h, hs) in enumerate(ch)]
    v_new = [u_ref[b, :, hs] - r[i][0:L] for i, (b, h, hs) in enumerate(ch)]
    av = [jnp.dot(at_ref[b, :, h * L:(h + 1) * L], v_new[i], preferred_element_type=F32)
          for i, (b, h, hs) in enumerate(ch)]
    kdv = [lax.dot_general(kd_ref[b, :, hs], v_new[i], TN, preferred_element_type=F32)
           for i, (b, h, hs) in enumerate(ch)]
    for i, (b, h, hs) in enumerate(ch):
        o_ref[b, :, hs] = r[i][L:2 * L] + av[i]
        s_s[b, h] = eg_ref[b, 0, h:h + 1, :] * S[i] + kdv[i]

    @pl.when(c == nc - 1)
    def _():
        so_ref[...] = s_s[...]


def _gdn(u3, smallt, g_conv, g_A_log, g_dt_bias, buf, S0, L, t_real):
    B, T, _ = u3.shape
    nc = T // L
    CB = 4 if nc % 4 == 0 else 1
    BB = 8 if B % 8 == 0 else B
    prow = jnp.stack([g_A_log, g_dt_bias])
    pcol = jnp.stack([g_A_log, g_dt_bias], axis=1)
    buf8 = jnp.concatenate([jnp.zeros((B, 8 - (CONV_W - 1), 3 * W_G), F32), buf], axis=1)
    st = lambda shape: pl.BlockSpec(shape, lambda b, c: (b,) + (0,) * (len(shape) - 1))
    cst = lambda shape: pl.BlockSpec(shape, lambda b, c: (0,) * len(shape))
    row = lambda nb, r, w: pl.BlockSpec((nb, r, w), lambda b, c: (b, c, 0))
    tok = lambda w: jax.ShapeDtypeStruct((B, T, w), F32)
    u, w, qe, kd, at, eg = pl.pallas_call(
        functools.partial(_gdn_a_kernel, L=L, CB=CB, t_real=t_real),
        grid=(B, nc // CB),
        in_specs=[pl.BlockSpec((1, CB * L, 3 * W_G), lambda b, c: (b, c, OFF['g_qkv'] // (3 * W_G))),
                  pl.BlockSpec((1, CB * L, LANES), lambda b, c: (b, c, OFF['small'] // LANES)),
                  pl.BlockSpec((1, CB, 16, L), lambda b, c: (b, c, 0, 0)),
                  cst((CONV_W, 3 * W_G)), cst((2, 4)), cst((4, 2)), st((1, 8, 3 * W_G))],
        out_specs=[row(1, CB * L, W_G)] * 4 + [row(1, CB * L, G_HEADS * L),
                                               pl.BlockSpec((1, CB, 8, LANES), lambda b, c: (b, c, 0, 0))],
        out_shape=[tok(W_G)] * 4 + [tok(G_HEADS * L), jax.ShapeDtypeStruct((B, nc, 8, LANES), F32)],
        scratch_shapes=[pltpu.VMEM((CB * L + 8, 3 * W_G), F32)],
        compiler_params=_cparams(("parallel", "arbitrary")),
        name="gdn_a",
    )(u3, u3, smallt, g_conv, prow, pcol, buf8)
    sblk = pl.BlockSpec((BB, G_HEADS, G_HD, G_HD), lambda b, c: (b, 0, 0, 0))
    return pl.pallas_call(
        functools.partial(_gdn_b_kernel, L=L, BB=BB),
        grid=(B // BB, nc),
        in_specs=[row(BB, L, W_G)] * 4 + [row(BB, L, G_HEADS * L),
                                          pl.BlockSpec((BB, 1, 8, LANES), lambda b, c: (b, c, 0, 0)), sblk],
        out_specs=[row(BB, L, W_G), sblk],
        out_shape=[tok(W_G), jax.ShapeDtypeStruct((B, G_HEADS, G_HD, G_HD), F32)],
        scratch_shapes=[pltpu.VMEM((BB, G_HEADS, G_HD, G_HD), F32)],
        compiler_params=_cparams(("parallel", "arbitrary")),
        name="gdn_b",
    )(u, w, qe, kd, at, eg, S0)


def _head_rmsnorm(x, gain_ref, nheads, hd):
    outs = []
    for h in range(nheads):
        xh = x[:, h * hd:(h + 1) * hd]
        ms = jnp.mean(xh * xh, axis=-1, keepdims=True)
        outs.append(xh * lax.rsqrt(ms + EPS) * gain_ref[...])
    return jnp.concatenate(outs, axis=1)


def _merge_kernel(x_ref, oa_ref, hm_ref, og_ref, az_ref, mo_ref, mz_ref, gz_ref, mg_ref,
                  wb_ref, wo_ref, mhn_ref, ghn_ref, y_ref):
    oa = oa_ref[...] * _silu(az_ref[...].astype(F32))
    om = (_head_rmsnorm(hm_ref[...], mhn_ref, M_HEADS, M_HD) * jax.nn.sigmoid(mo_ref[...].astype(F32))
          * _silu(mz_ref[...].astype(F32)))
    og = _head_rmsnorm(og_ref[...], ghn_ref, G_HEADS, G_HD) * _silu(gz_ref[...].astype(F32))
    y = jnp.zeros(y_ref.shape, F32)
    for i, br in enumerate((oa, om, og)):
        proj = jnp.dot(br.astype(BF16), wb_ref[i], preferred_element_type=F32)
        y = y + jax.nn.sigmoid(mg_ref[:, i * D_MODEL:(i + 1) * D_MODEL].astype(F32)) * proj
    y_ref[...] = x_ref[...] + jnp.dot(y.astype(BF16), wo_ref[...], preferred_element_type=F32)


def _merge_out(x2, o_a, h_m, o_g, u2, w_branch, w_out, m_hn, g_hn):
    n = x2.shape[0]
    tm = min(512, n)
    row = lambda w, off: pl.BlockSpec((tm, w), lambda i: (i, off // w))
    cst = lambda shape: pl.BlockSpec(shape, lambda i: (0,) * len(shape))
    return pl.pallas_call(
        _merge_kernel,
        grid=(n // tm,),
        in_specs=[row(D_MODEL, 0), row(W_A, 0), row(W_M, 0), row(W_G, 0),
                  row(W_A, OFF['a_z']), row(W_M, OFF['m_o']), row(W_M, OFF['m_z']), row(W_G, OFF['g_z']),
                  row(N_BRANCH * D_MODEL, OFF['merge']),
                  cst((N_BRANCH, W_A, D_MODEL)), cst((D_MODEL, D_MODEL)), cst((1, M_HD)), cst((1, G_HD))],
        out_specs=row(D_MODEL, 0),
        out_shape=jax.ShapeDtypeStruct((n, D_MODEL), F32),
        compiler_params=_cparams(("parallel",)),
        name="merge_out",
    )(x2, o_a, h_m, o_g, u2, u2, u2, u2, u2, w_branch.astype(BF16), w_out.astype(BF16),
      m_hn.reshape(1, M_HD), g_hn.reshape(1, G_HD))


def _small_t(u3, L):
    B, T, _ = u3.shape
    s = u3[:, :, OFF['small'] + SM_MIF:OFF['small'] + SM_MIF + 16].astype(F32)
    return s.reshape(B, T // L, L, 16).transpose(0, 1, 3, 2)


def _layer(x, lw, tb, past, q_off):
    B, T, _ = x.shape
    if past is None:
        tp, L = T, math.gcd(T, M_CHUNK)
    else:
        tp = -(-T // 8) * 8
        L = tp
        x = jnp.pad(x, ((0, 0), (0, tp - T), (0, 0)))
    x2 = x.reshape(B * tp, D_MODEL)
    u2 = _proj_in(x2, lw['norm_g'], lw['w_perm'], BF16 if past is None else F32)
    u3 = u2.reshape(B, tp, N_PERM)
    if past is None:
        C0 = jnp.zeros((B, M_HEADS, M_HD, M_HD), F32)
        n0 = jnp.zeros((B, M_HEADS, M_HD), F32)
        m0 = jnp.zeros((B, M_HEADS), F32)
        S0 = jnp.zeros((B, G_HEADS, G_HD, G_HD), F32)
        buf = jnp.zeros((B, CONV_W - 1, 3 * W_G), F32)
    else:
        C0, n0, m0, S0, buf = past['mC'], past['mn'], past['mm'], past['gS'], past['gconv']
    qn, rows6, *rows_t = _prep_a(u2, lw['a_qn'], lw['a_kn'], T if past is None and T % 512 == 0 else None)
    qn3 = qn.reshape(B, tp, W_A)
    if past is None:
        o_a = _nsa_prompt(qn3, rows6, u3, lw['wk'], lw['wv'], tb)
    else:
        o_a = _nsa_sample(qn3, rows6, u3, past['cmp'], past['sel'], past['win_t'], past['layer'], past['page_table'],
                          lw['wk'], lw['wv'], tb, q_off, T)
    smallt = _small_t(u3, L)
    h_m, Cn, nn, mn = _mlstm(u3, smallt, lw['m_bi'], lw['m_bf'], C0, n0, m0, L, T)
    o_g, Sn = _gdn(u3, smallt, lw['g_conv'], lw['g_A_log'], lw['g_dt_bias'], buf, S0, L, T)
    y2 = _merge_out(x2, o_a.reshape(B * tp, W_A), h_m.reshape(B * tp, W_M), o_g.reshape(B * tp, W_G), u2,
                    lw['w_branch'], lw['w_out'], lw['m_hn'], lw['g_hn'])
    if rows_t:
        new_cmp, new_sel, new_win = (rows_t[b].reshape(B, 2, A_KV, A_HD, T).transpose(0, 4, 1, 2, 3)
                                     for b in range(3))
    else:
        rows = rows6.reshape(3, 2, B, tp, A_KV, A_HD)[:, :, :, :T]
        new_cmp, new_sel, new_win = (jnp.moveaxis(rows[b], 0, 2) for b in range(3))
    g_qkv = u3[:, max(T - (CONV_W - 1), 0):T, OFF['g_qkv']:OFF['g_qkv'] + 3 * W_G].astype(F32)
    if past is None:
        win = new_win
    else:
        win = jnp.concatenate([past['win'], new_win], axis=1)
    full = jnp.concatenate([buf, g_qkv], axis=1)
    state = dict(cmp=new_cmp, sel=new_sel, win=win[:, -min(WINDOW, win.shape[1]):],
                 mC=Cn, mn=nn, mm=mn, gS=Sn, gconv=full[:, -(CONV_W - 1):])
    return y2.reshape(B, tp, D_MODEL)[:, :T], state


def kernel(x_prompt, x_sample, cache_cmp_kv, cache_sel_kv, cache_win_kv, state_mlstm_C, state_mlstm_n,
           state_mlstm_m, state_gdn_S, state_gdn_conv, page_table, norm_g, w_in, a_qn, a_kn, a_cmp_wk,
           a_cmp_wv, rel_bias, m_bi, m_bf, m_hn, g_conv, g_A_log, g_dt_bias, g_hn, w_branch, w_out):
    names = ('cmp', 'sel', 'win', 'mC', 'mn', 'mm', 'gS', 'gconv')
    st_p = {k: [] for k in names}
    st_s = {k: [] for k in names}
    past_len = page_table.shape[1] * PAGE_SIZE
    n_pool = cache_cmp_kv.shape[1]
    pos_minor = lambda c: jnp.swapaxes(c.reshape(c.shape[0], c.shape[1], c.shape[2], 2 * A_KV * A_HD), 2, 3)
    pool_cmp, pool_sel, win_t = pos_minor(cache_cmp_kv), pos_minor(cache_sel_kv), pos_minor(cache_win_kv)
    w_t = jnp.swapaxes(w_in, 1, 2)
    db, dt = x_sample.shape[0], x_sample.shape[1]
    tb_p = _prompt_tables(rel_bias, x_prompt.shape[1])
    tb_s = _sample_tables(rel_bias, past_len, -(-dt // 8) * 8, dt)
    y_p, y_s = x_prompt, x_sample
    for l in range(DEPTH):
        lw = dict(norm_g=norm_g[l], w_perm=_permute_w_in(w_t, l), a_qn=a_qn[l], a_kn=a_kn[l],
                  wk=_cmp_weights(a_cmp_wk[l]), wv=_cmp_weights(a_cmp_wv[l]),
                  m_bi=m_bi[l], m_bf=m_bf[l], m_hn=m_hn[l], g_conv=g_conv[l], g_A_log=g_A_log[l],
                  g_dt_bias=g_dt_bias[l], g_hn=g_hn[l], w_branch=w_branch[l], w_out=w_out[l])
        y_p, new_p = _layer(y_p, lw, tb_p, None, 0)
        past = dict(cmp=pool_cmp, sel=pool_sel, win_t=win_t, layer=l, win=cache_win_kv[l], page_table=page_table,
                    mC=state_mlstm_C[l], mn=state_mlstm_n[l], mm=state_mlstm_m[l],
                    gS=state_gdn_S[l], gconv=state_gdn_conv[l])
        y_s, new_s = _layer(y_s, lw, tb_s, past, past_len)
        for k in names:
            st_p[k].append(new_p[k])
            st_s[k].append(new_s[k])
    P = {k: jnp.stack(v) for k, v in st_p.items()}
    S = {k: jnp.stack(v) for k, v in st_s.items()}
    return (y_p, y_s, P['cmp'], S['cmp'], P['sel'], S['sel'], P['win'], S['win'],
            P['mC'], S['mC'], P['mn'], S['mn'], P['mm'], S['mm'], P['gS'], S['gS'], P['gconv'], S['gconv'])
```

```python
import functools
import math

import jax
import jax.numpy as jnp
from jax import lax
from jax.experimental import pallas as pl
from jax.experimental.pallas import tpu as pltpu

D_MODEL = 1024
DEPTH = 2
PAGE_SIZE = 128
A_HEADS = 8
A_KV = 2
A_REP = A_HEADS // A_KV
A_HD = 64
CMP_BLOCK = 32
CMP_STRIDE = 16
SEL_BLOCK = 64
TOP_N = 16
WINDOW = 512
Q_BLOCK = 128
N_BUCKETS = 32
MAX_DIST = 2048
M_HEADS = 4
M_HD = 128
M_CHUNK = 64
G_HEADS = 4
G_HD = 128
G_CHUNK = 64
CONV_W = 4
W_A = A_HEADS * A_HD
W_M = M_HEADS * M_HD
W_G = G_HEADS * G_HD
N_BRANCH = 3
EPS = 1e-6
NEG = -1e30
TINY = 1e-30
LOG2E = math.log2(math.e)

F32 = jnp.float32
BF16 = jnp.bfloat16
HI = lax.Precision.HIGHEST
NN = (((1,), (0,)), ((), ()))
NT = (((1,), (1,)), ((), ()))
TN = (((0,), (0,)), ((), ()))

LANES = 128
VMEM_LIMIT = 56 * 1024 * 1024

IN_ORDER = ('a_q', 'a_kv', 'a_gate', 'a_z', 'm_qkv', 'm_if', 'm_o', 'm_z', 'g_qkv', 'g_ab', 'g_z', 'merge')
IN_WIDTH = dict(a_q=W_A, a_kv=3 * 2 * A_KV * A_HD, a_gate=A_HEADS * 3, a_z=W_A, m_qkv=3 * W_M, m_if=2 * M_HEADS,
                m_o=W_M, m_z=W_M, g_qkv=3 * W_G, g_ab=2 * G_HEADS, g_z=W_G, merge=N_BRANCH * D_MODEL)
OFF = dict(merge=0, m_qkv=3072, g_qkv=4608, a_q=6144, a_z=6656, m_o=7168, m_z=7680, g_z=8192, a_kv=8704,
           small=9472)
N_PERM = 9600
SM_GATE, SM_MIF, SM_GAB = 0, 24, 32


def _cparams(sem):
    return pltpu.CompilerParams(dimension_semantics=sem, vmem_limit_bytes=VMEM_LIMIT)


def _silu(x):
    return x * jax.nn.sigmoid(x)


def _log_sigmoid(x):
    return jnp.minimum(x, 0.0) - jnp.log(1.0 + jnp.exp(-jnp.abs(x)))


def _softplus(x):
    return jnp.maximum(x, 0.0) + jnp.log(1.0 + jnp.exp(-jnp.abs(x)))


def _split(a):
    hi = a.astype(BF16)
    return hi, (a - hi.astype(F32)).astype(BF16)


def _dot3(a, b, dims=NN):
    mm = lambda x, y: lax.dot_general(x, y, dims, preferred_element_type=F32)
    return mm(a[0], b[0]) + mm(a[0], b[1]) + mm(a[1], b[0])


def _src_offsets():
    offs, off = {}, 0
    for name in IN_ORDER:
        offs[name] = off
        off += IN_WIDTH[name]
    return offs, off


def _permute_kernel(w_ref, o_ref):
    src, _ = _src_offsets()
    small = []
    for name in IN_ORDER:
        w = IN_WIDTH[name]
        rows = w_ref[src[name]:src[name] + w, :]
        if w % LANES:
            small.append(rows)
        else:
            o_ref[:, OFF[name]:OFF[name] + w] = rows.T.astype(BF16)
    used = sum(r.shape[0] for r in small)
    small.append(jnp.zeros((LANES - used, w_ref.shape[1]), F32))
    o_ref[:, OFF['small']:N_PERM] = jnp.concatenate(small, axis=0).T.astype(BF16)


def _permute_w_in(w_t, layer):
    _, n_in, d = w_t.shape
    tr = 128
    return pl.pallas_call(
        _permute_kernel,
        grid=(d // tr,),
        in_specs=[pl.BlockSpec((None, n_in, tr), lambda i: (layer, 0, i))],
        out_specs=pl.BlockSpec((tr, N_PERM), lambda i: (i, 0)),
        out_shape=jax.ShapeDtypeStruct((d, N_PERM), BF16),
        compiler_params=_cparams(("parallel",)),
        name="permute_w",
    )(w_t)


def _rel_bucket(dist):
    n = jnp.maximum(dist, 0)
    exact = N_BUCKETS // 2
    nf = jnp.maximum(n, exact).astype(F32)
    large = exact + (jnp.log(nf / exact) / math.log(MAX_DIST / exact) * (N_BUCKETS - exact)).astype(jnp.int32)
    return jnp.where(n < exact, n, jnp.minimum(large, N_BUCKETS - 1))


def _bias_kernel(thr_ref, tab_ref, d_ref, o_ref):
    n = jnp.maximum(d_ref[0], 0)
    for h in range(A_HEADS):
        acc = jnp.full(n.shape, tab_ref[h], F32)
        for k in range(1, N_BUCKETS):
            acc = jnp.where(n >= thr_ref[k], tab_ref[k * A_HEADS + h], acc)
        o_ref[0, h // A_REP, h % A_REP] = acc


def _bias_rows(rel_bias, dist, split_rows=False):
    N, Q, K = dist.shape
    nmax = 2 * MAX_DIST
    thr = jnp.sum(_rel_bucket(jnp.arange(nmax))[None, :] < jnp.arange(N_BUCKETS)[:, None], axis=1).astype(jnp.int32)
    smem = pl.BlockSpec(memory_space=pltpu.SMEM)
    out = pl.pallas_call(
        _bias_kernel,
        grid=(N,),
        in_specs=[smem, smem, pl.BlockSpec((1, Q, K), lambda i: (i, 0, 0))],
        out_specs=pl.BlockSpec((1, A_KV, A_REP, Q, K), lambda i: (i, 0, 0, 0, 0)),
        out_shape=jax.ShapeDtypeStruct((N, A_KV, A_REP, Q, K), F32),
        compiler_params=_cparams(("parallel",)),
        name="bias_rows",
    )(thr, rel_bias.astype(F32).reshape(N_BUCKETS * A_HEADS), dist.astype(jnp.int32))
    return out if split_rows else out.reshape(N, A_KV, A_REP * Q, K)


def _cmp_weights(w):
    wr = w.reshape(A_KV, 2, CMP_STRIDE, A_HD, A_HD)
    eye = jnp.eye(A_KV, dtype=w.dtype)
    full = jnp.einsum('gmjde,gh->jgdmhe', wr, eye)
    return full.reshape(CMP_STRIDE, A_KV * A_HD, 2 * A_KV * A_HD).astype(BF16)


def _cover_t(ns_pad, nch):
    s0 = jnp.arange(ns_pad)[:, None] * SEL_BLOCK
    c0 = jnp.arange(nch)[None, :] * CMP_STRIDE
    return ((c0 < s0 + SEL_BLOCK) & (s0 <= c0 + CMP_BLOCK - 1)).astype(F32)


def _gate_expand():
    col = jnp.arange(N_BRANCH * W_A)
    src = 3 * ((col % W_A) // A_HD) + col // W_A
    return (jnp.arange(LANES)[:, None] == src[None, :]).astype(BF16)


def _phase_perm(pages=2):
    n = pages * PAGE_SIZE
    r = jnp.arange(n)
    blocks = n // CMP_STRIDE
    return (jnp.arange(n)[None, :] == ((r % blocks) * CMP_STRIDE + r // blocks)[:, None]).astype(BF16)


def _prompt_tables(rel_bias, T):
    QB = Q_BLOCK
    nqb = T // QB
    nch = T // CMP_STRIDE
    ns = T // SEL_BLOCK
    ns_pad = -(-ns // 8) * 8
    i_ = jnp.arange(QB)
    t = (jnp.arange(nqb) * QB)[:, None, None] + i_[None, :, None]
    cend = (jnp.arange(nch) * CMP_STRIDE + CMP_BLOCK - 1)[None, None, :]
    d = (jnp.arange(nqb) * QB)[:, None, None] + i_[None, :, None] - i_[None, None, :]
    nw = min(WINDOW // QB + 1, nqb)
    dt = jnp.swapaxes(d, 1, 2)
    tsel = _bias_rows(rel_bias * LOG2E, dt, split_rows=True)
    return dict(
        bcmp=_bias_rows(rel_bias, t - cend),
        tselt=tsel.transpose(0, 1, 3, 2, 4).reshape(nqb, A_KV, QB, A_REP * QB),
        covt=_cover_t(ns_pad, nch),
        twin=(tsel[:nw] + jnp.where((dt[:nw] >= 0) & (dt[:nw] < WINDOW), 0.0, NEG)[:, None, None]
              ).transpose(0, 1, 3, 2, 4).reshape(nw, A_KV, QB, A_REP * QB),
        cmt=jnp.where(dt[:2] >= 0, 0.0, NEG),
        gx=_gate_expand())


def _sample_tables(rel_bias, past, tp, t_real):
    SL = past + LANES
    nblk = past // CMP_STRIDE
    ns = -(-(past + t_real) // SEL_BLOCK)
    ns_pad = -(-ns // LANES) * LANES
    WK = WINDOW + LANES
    i_ = jnp.arange(tp)
    t = past + i_
    rows = lambda dist: _bias_rows(rel_bias, dist[None])[0]
    return dict(
        bcmp=rows(t[:, None] - (jnp.arange(nblk) * CMP_STRIDE + CMP_BLOCK - 1)[None, :]),
        bsel=rows(t[:, None] - jnp.arange(SL)[None, :]),
        bwin=rows(i_[:, None] + WINDOW - jnp.arange(WK)[None, :]),
        cov=_cover_t(ns_pad, nblk).T, perm=_phase_perm())


def _proj_in_kernel(x_ref, g_ref, w_ref, o_ref, hn_ref):
    @pl.when(pl.program_id(1) == 0)
    def _():
        x = x_ref[...]
        ms = jnp.mean(x * x, axis=-1, keepdims=True)
        hn_ref[...] = (x * lax.rsqrt(ms + EPS) * g_ref[...]).astype(BF16)

    o_ref[...] = jnp.dot(hn_ref[...], w_ref[...], preferred_element_type=F32).astype(o_ref.dtype)


def _proj_in(x2, norm_g, w_perm, out_dtype):
    n = x2.shape[0]
    tm = min(2048, n)
    tn = 1920
    return pl.pallas_call(
        _proj_in_kernel,
        grid=(n // tm, N_PERM // tn),
        in_specs=[pl.BlockSpec((tm, D_MODEL), lambda i, j: (i, 0)),
                  pl.BlockSpec((1, D_MODEL), lambda i, j: (0, 0)),
                  pl.BlockSpec((D_MODEL, tn), lambda i, j: (0, j))],
        out_specs=pl.BlockSpec((tm, tn), lambda i, j: (i, j)),
        out_shape=jax.ShapeDtypeStruct((n, N_PERM), out_dtype),
        scratch_shapes=[pltpu.VMEM((tm, D_MODEL), BF16)],
        compiler_params=_cparams(("parallel", "arbitrary")),
        name="proj_in",
    )(x2, norm_g.reshape(1, D_MODEL), w_perm)


def _prep_a_kernel(q_ref, kv0_ref, kv1_ref, kv2_ref, bdq_ref, bdk_ref, qg_ref, kg_ref, qo_ref, ro_ref, *rt_ref):
    def group_mean(x2, bd_ref):
        hi, lo = _split(x2)
        return (jnp.dot(hi, bd_ref[...], preferred_element_type=F32)
                + jnp.dot(lo, bd_ref[...], preferred_element_type=F32)) * (1.0 / A_HD)

    q = q_ref[...].astype(F32)
    ms = group_mean(q * q, bdq_ref)
    qo_ref[...] = q * lax.rsqrt(ms + EPS) * qg_ref[...] * (A_HD ** -0.5)
    for b, kv_ref in enumerate((kv0_ref, kv1_ref, kv2_ref)):
        k = kv_ref[:, 0:128].astype(F32)
        ms = group_mean(k * k, bdk_ref)
        kn = k * lax.rsqrt(ms + EPS) * kg_ref[b:b + 1, :]
        v = kv_ref[:, 128:256].astype(F32)
        ro_ref[2 * b] = kn
        ro_ref[2 * b + 1] = v
        if rt_ref:
            rt_ref[b][0:128, :] = kn.T
            rt_ref[b][128:256, :] = v.T


def _prep_a(u2, a_qn, a_kn, seq=None):
    n = u2.shape[0]
    tm = min(512, n)
    out_specs = [pl.BlockSpec((tm, W_A), lambda i: (i, 0)), pl.BlockSpec((6, tm, 128), lambda i: (0, i, 0))]
    out_shape = [jax.ShapeDtypeStruct((n, W_A), F32), jax.ShapeDtypeStruct((6, n, 128), F32)]
    if seq is not None:
        per = seq // tm
        out_specs += [pl.BlockSpec((None, 256, tm), lambda i: (i // per, 0, i % per))] * 3
        out_shape += [jax.ShapeDtypeStruct((n // seq, 256, seq), F32)] * 3
    bd = lambda w: (jnp.arange(w)[:, None] // A_HD == jnp.arange(w)[None, :] // A_HD).astype(BF16)
    qg = jnp.tile(a_qn, A_HEADS).reshape(1, W_A)
    kg = jnp.tile(a_kn, (1, A_KV))
    const = lambda shape: pl.BlockSpec(shape, lambda i: (0, 0))
    kvs = lambda b: pl.BlockSpec((tm, 256), lambda i: (i, OFF['a_kv'] // 256 + b))
    return pl.pallas_call(
        _prep_a_kernel,
        grid=(n // tm,),
        in_specs=[pl.BlockSpec((tm, W_A), lambda i: (i, OFF['a_q'] // W_A)), kvs(0), kvs(1), kvs(2),
                  const((W_A, W_A)), const((128, 128)), const((1, W_A)), const((3, 128))],
        out_specs=out_specs,
        out_shape=out_shape,
        compiler_params=_cparams(("parallel",)),
        name="prep_a",
    )(u2, u2, u2, u2, bd(W_A), bd(128), qg, kg)


def _stack_heads(q, g):
    return jnp.concatenate([q[:, (A_REP * g + r) * A_HD:(A_REP * g + r + 1) * A_HD] for r in range(A_REP)], axis=0)


def _rep_rows(x):
    return jnp.concatenate([x] * A_REP, axis=0)


def _masked_softmax(s, valid):
    sm = jnp.where(valid, s, NEG)
    m = jnp.max(sm, axis=-1, keepdims=True)
    e = jnp.where(valid, jnp.exp(sm - m), 0.0)
    l = jnp.sum(e, axis=-1, keepdims=True)
    return e / jnp.maximum(l, TINY)


def _select_blocks(pc_sum, covt_ref, st_ref, t_row, ns, n_top):
    ns_pad = covt_ref.shape[0]
    imp = lax.dot_general(covt_ref[...], pc_sum, NT, precision=HI, preferred_element_type=F32)
    jj = lax.broadcasted_iota(jnp.int32, (ns_pad, LANES), 0)
    tt = jnp.broadcast_to(t_row, (ns_pad, LANES))
    cur = tt // SEL_BLOCK
    forced = (jj == 0) | (jj == cur) | (jj == cur - 1)
    future = jj * SEL_BLOCK > tt
    score = jnp.where(future, NEG, jnp.where(forced, -NEG, imp))
    score = jnp.where(jj < ns, score, -jnp.inf)
    st_ref[...] = score

    def beats(k, rank):
        row = jnp.broadcast_to(st_ref[pl.ds(k, 1), :], (ns_pad, LANES))
        b = (row > score) | ((row == score) & (k < jj))
        return rank + jnp.where(b, 1.0, 0.0)

    rank = jnp.zeros((ns_pad, LANES), F32)
    if ns <= 32:
        for k in range(ns):
            rank = beats(k, rank)
    else:
        rank = lax.fori_loop(0, ns, beats, rank)
    return jnp.where(rank < n_top, 1.0, 0.0)


def _pad_rows(x, rows):
    if x.shape[0] == rows:
        return x
    return jnp.concatenate([x, jnp.zeros((rows - x.shape[0], x.shape[1]), x.dtype)], axis=0)


def _select_rows(pc_sum, cov_ref, sel_ref, t0, ns, n_top, nq):
    Q = pc_sum.shape[0]
    ns_pad = cov_ref.shape[1]
    nk = -(-ns // 8) * 8
    imp = jnp.dot(pc_sum, cov_ref[...], precision=HI, preferred_element_type=F32)
    jj = lax.broadcasted_iota(jnp.int32, (Q, ns_pad), 1)
    tt = lax.broadcasted_iota(jnp.int32, (Q, ns_pad), 0) + t0
    cur = tt // SEL_BLOCK
    forced = (jj == 0) | (jj == cur) | (jj == cur - 1)
    future = jj * SEL_BLOCK > tt
    score = jnp.where(future, NEG, jnp.where(forced, -NEG, imp))
    score = jnp.where(jj < ns, score, -jnp.inf)
    score_col = _pad_rows(score, LANES).T
    kk = lax.broadcasted_iota(jnp.int32, (nk, ns_pad), 0)
    jl = lax.broadcasted_iota(jnp.int32, (nk, ns_pad), 1)
    sel_ref[...] = jnp.zeros(sel_ref.shape, F32)
    for i in range(nq):
        col = jnp.broadcast_to(score_col[0:nk, i:i + 1], (nk, ns_pad))
        row = jnp.broadcast_to(score[i:i + 1, :], (nk, ns_pad))
        beats = (col > row) | ((col == row) & (kk < jl))
        rank = jnp.sum(jnp.where(beats, 1.0, 0.0), axis=0, keepdims=True)
        sel_ref[i:i + 1, :] = jnp.where(rank < n_top, 1.0, 0.0)
    return sel_ref[...]


def _compress(src_ref, nrow, wk_ref, wv_ref, pbuf_ref, kc_ref, vc_ref, by_phase=False):
    acck = jnp.zeros((nrow, 256), F32)
    accv = jnp.zeros((nrow, 256), F32)
    for j in range(CMP_STRIDE):
        if by_phase:
            xk = src_ref[0, j, 0:nrow, :].astype(BF16)
            xv = src_ref[1, j, 0:nrow, :].astype(BF16)
        else:
            xk = src_ref[0, pl.ds(j, nrow, stride=CMP_STRIDE), :].astype(BF16)
            xv = src_ref[1, pl.ds(j, nrow, stride=CMP_STRIDE), :].astype(BF16)
        acck = acck + jnp.dot(xk, wk_ref[j], preferred_element_type=F32)
        accv = accv + jnp.dot(xv, wv_ref[j], preferred_element_type=F32)
    nout = kc_ref.shape[0]
    for acc, dst in ((acck, kc_ref), (accv, vc_ref)):
        pbuf_ref[0:nrow, :] = acc[:, 128:256]
        dst[...] = acc[0:nout, 0:128] + pbuf_ref[pl.ds(1, nout), :]


def _gated_sum(gates, gx_ref, ocat_ref):
    gh, gl = _split(gates)
    gexp = (jnp.dot(gh, gx_ref[...], preferred_element_type=F32)
            + jnp.dot(gl, gx_ref[...], preferred_element_type=F32))
    out = gexp[:, 0:W_A] * ocat_ref[0]
    for c in range(1, N_BRANCH):
        out = out + gexp[:, c * W_A:(c + 1) * W_A] * ocat_ref[c]
    return out


def _nsa_prompt_kernel(q_ref, rows_ref, small_ref, wk_ref, wv_ref, bcmp_ref, tselt_ref, covt_ref,
                       twin_ref, cmt_ref, gx_ref,
                       o_ref, kc_ref, vc_ref, pbuf_ref, mb_ref, acc_ref, m_ref, l_ref, st_ref, ocat_ref, ocatt_ref,
                       *, ns, n_top):
    bi = pl.program_id(1)
    T = rows_ref.shape[1]
    nch = T // CMP_STRIDE
    QB = Q_BLOCK
    SEL, WIN = 1, 2

    @pl.when(bi == 0)
    def _():
        pbuf_ref[...] = jnp.zeros(pbuf_ref.shape, F32)
        _compress(rows_ref, nch, wk_ref, wv_ref, pbuf_ref, kc_ref, vc_ref)

    t0 = bi * QB
    q = q_ref[0]
    tc = lax.broadcasted_iota(jnp.int32, (QB, nch), 0) + t0
    cend = lax.broadcasted_iota(jnp.int32, (QB, nch), 1) * CMP_STRIDE + (CMP_BLOCK - 1)
    cvalid = _rep_rows(tc - cend >= 0)
    t_row = lax.broadcasted_iota(jnp.int32, (1, LANES), 1) + t0
    qg = [_stack_heads(q, g).astype(BF16) for g in range(A_KV)]
    qt = (q * LOG2E).T
    qgt = [jnp.concatenate([qt[(A_REP * g + r) * A_HD:(A_REP * g + r + 1) * A_HD] for r in range(A_REP)],
                           axis=1).astype(BF16) for g in range(A_KV)]

    for g in range(A_KV):
        kc = kc_ref[:, g * A_HD:(g + 1) * A_HD].astype(BF16)
        vc = vc_ref[:, g * A_HD:(g + 1) * A_HD].astype(BF16)
        s = lax.dot_general(qg[g], kc, NT, preferred_element_type=F32) + bcmp_ref[0, g]
        p_c = _masked_softmax(s, cvalid)
        o_c = jnp.dot(p_c.astype(BF16), vc, preferred_element_type=F32)
        for r in range(A_REP):
            h = A_REP * g + r
            ocat_ref[0, :, h * A_HD:(h + 1) * A_HD] = o_c[r * QB:(r + 1) * QB]
        pc_sum = p_c[0:QB] + p_c[QB:2 * QB] + p_c[2 * QB:3 * QB] + p_c[3 * QB:4 * QB]
        sel_t = _select_blocks(pc_sum, covt_ref, st_ref, t_row, ns, n_top)
        mb_ref[g] = (sel_t - 1.0) * (-NEG)

    m_ref[...] = jnp.full(m_ref.shape, NEG, F32)
    l_ref[...] = jnp.zeros(l_ref.shape, F32)
    acc_ref[...] = jnp.zeros(acc_ref.shape, F32)
    lanes4 = lambda x: jnp.concatenate([x] * A_REP, axis=1)
    half = SEL_BLOCK

    def tiles(specs):
        cat = lambda xs: xs[0] if len(xs) == 1 else jnp.concatenate(xs, axis=0)
        cis = [2 * (br - 1) + g for br, g, _, _ in specs]
        sms, vs = [], []
        for br, g, kb, nb in specs:
            ks, vv, bs, ms = [], [], [], []
            for j in range(nb):
                off = pl.multiple_of((kb + j) * QB, QB)
                ks.append(rows_ref[2 * br, pl.ds(off, QB), g * A_HD:(g + 1) * A_HD])
                vv.append(rows_ref[2 * br + 1, pl.ds(off, QB), g * A_HD:(g + 1) * A_HD])
                if br == SEL:
                    bs.append(tselt_ref[bi - kb - j, g])
                    blk = (QB // half) * (kb + j)
                    mt = jnp.concatenate([jnp.broadcast_to(mb_ref[g, pl.ds(blk + a, 1), :], (half, QB))
                                          for a in range(QB // half)], axis=0)
                    ms.append(mt + cmt_ref[jnp.minimum(bi - kb, 1)] if nb == 1 else mt)
                else:
                    bs.append(twin_ref[bi - kb - j, g])
            vs.append(cat(vv).astype(BF16))
            sc = jnp.dot(cat(ks).astype(BF16), qgt[g], preferred_element_type=F32) + cat(bs)
            sms.append(sc + lanes4(cat(ms)) if ms else sc)
        m_prevs = [m_ref[ci] for ci in cis]
        m_news = [jnp.maximum(mp, jnp.max(sm, axis=0, keepdims=True)) for mp, sm in zip(m_prevs, sms)]
        ps = [jnp.exp2(sm - mn) for sm, mn in zip(sms, m_news)]
        sums = [jnp.sum(p, axis=0, keepdims=True) for p in ps]
        pvs = [lax.dot_general(v, p.astype(BF16), TN, preferred_element_type=F32) for v, p in zip(vs, ps)]
        for ci, mp, mn, sm_, pv in zip(cis, m_prevs, m_news, sums, pvs):
            alpha = jnp.exp2(mp - mn)
            l_ref[ci] = alpha * l_ref[ci] + sm_
            acc_ref[ci] = alpha * acc_ref[ci] + pv
            m_ref[ci] = mn

    def body_sel(kb, carry):
        tiles([(SEL, g, kb, 1) for g in range(A_KV)])
        return carry

    def body_sel2(pair, carry):
        tiles([(SEL, g, 2 * pair, 2) for g in range(A_KV)])
        return carry

    def body_both(kb, carry):
        tiles([(br, g, kb, 1) for g in range(A_KV) for br in (SEL, WIN)])
        return carry

    lo = jnp.maximum(bi - WINDOW // QB, 0)
    lax.fori_loop(0, lo // 2, body_sel2, 0)
    lax.fori_loop(2 * (lo // 2), lo, body_sel, 0)
    lax.fori_loop(lo, bi + 1, body_both, 0)

    for br in (SEL, WIN):
        for g in range(A_KV):
            ci = 2 * (br - 1) + g
            o_t = acc_ref[ci] / jnp.maximum(l_ref[ci], TINY)
            for r in range(A_REP):
                h = A_REP * g + r
                ocatt_ref[br - 1, h * A_HD:(h + 1) * A_HD, :] = o_t[:, r * QB:(r + 1) * QB]
        ocat_ref[br] = ocatt_ref[br - 1].T

    o_ref[0] = _gated_sum(jax.nn.sigmoid(small_ref[0].astype(F32)), gx_ref, ocat_ref)


def _nsa_prompt(qn3, rows6, u3, wk, wv, tb):
    B, T, _ = qn3.shape
    QB = Q_BLOCK
    nqb = T // QB
    nch = T // CMP_STRIDE
    ns = T // SEL_BLOCK
    ns_pad = tb['covt'].shape[0]
    n_top = min(TOP_N, ns)
    nw = tb['twin'].shape[0]
    kern = functools.partial(_nsa_prompt_kernel, ns=ns, n_top=n_top)
    c2 = lambda shape: pl.BlockSpec(shape, lambda b, i: (0,) * len(shape))
    return pl.pallas_call(
        kern,
        grid=(B, nqb),
        in_specs=[pl.BlockSpec((1, QB, W_A), lambda b, i: (b, i, 0)),
                  pl.BlockSpec((6, T, 128), lambda b, i: (0, b, 0)),
                  pl.BlockSpec((1, QB, LANES), lambda b, i: (b, i, OFF['small'] // LANES)),
                  c2((CMP_STRIDE, 128, 256)), c2((CMP_STRIDE, 128, 256)),
                  pl.BlockSpec((1, A_KV, A_REP * QB, nch), lambda b, i: (i, 0, 0, 0)),
                  c2((nqb, A_KV, QB, A_REP * QB)),
                  c2((ns_pad, nch)), c2((nw, A_KV, QB, A_REP * QB)), c2((2, QB, QB)),
                  c2((LANES, N_BRANCH * W_A))],
        out_specs=pl.BlockSpec((1, QB, W_A), lambda b, i: (b, i, 0)),
        out_shape=jax.ShapeDtypeStruct((B, T, W_A), F32),
        scratch_shapes=[pltpu.VMEM((nch, 128), F32), pltpu.VMEM((nch, 128), F32),
                        pltpu.VMEM((nch + 8, 128), F32),
                        pltpu.VMEM((A_KV, ns_pad, QB), F32),
                        pltpu.VMEM((2 * A_KV, A_HD, A_REP * QB), F32),
                        pltpu.VMEM((2 * A_KV, 1, A_REP * QB), F32),
                        pltpu.VMEM((2 * A_KV, 1, A_REP * QB), F32),
                        pltpu.VMEM((ns_pad, LANES), F32),
                        pltpu.VMEM((N_BRANCH, QB, W_A), F32),
                        pltpu.VMEM((2, W_A, QB), F32)],
        compiler_params=_cparams(("parallel", "arbitrary")),
        name="nsa_prompt",
    )(qn3, rows6, u3, wk, wv, tb['bcmp'], tb['tselt'], tb['covt'], tb['twin'], tb['cmt'], tb['gx'])


def _nsa_sample_kernel(pt_ref, *refs, pg, past, tp, t_real, ns, n_top):
    cmp_refs = refs[0:pg]
    sel_refs = refs[pg:2 * pg]
    (q_ref, rows_ref, small_ref, win_ref, wk_ref, wv_ref, bcmp_ref, bsel_ref, bwin_ref, cov_ref, perm_ref,
     o_ref, cslab, sslab, kc_ref, vc_ref, pbuf_ref, st_ref, s_ref) = refs[2 * pg:]
    p = pl.program_id(1)
    npg = pl.num_programs(1)
    n_pages = past // PAGE_SIZE
    SL = (n_pages + 1) * PAGE_SIZE
    nblk = kc_ref.shape[0]
    WK = WINDOW + LANES
    pair_blocks = 2 * PAGE_SIZE // CMP_STRIDE

    for kk in range(pg // 2):
        page = p * pg + 2 * kk
        row0 = pl.multiple_of(page * (PAGE_SIZE // CMP_STRIDE), pair_blocks)
        pair = jnp.concatenate([cmp_refs[2 * kk][...], cmp_refs[2 * kk + 1][...]], axis=1).astype(BF16)
        ordered = lax.dot_general(perm_ref[...], pair, NT, preferred_element_type=F32)
        for kv in range(2):
            for j in range(CMP_STRIDE):
                cslab[kv, j, pl.ds(row0, pair_blocks), :] = ordered[j * pair_blocks:(j + 1) * pair_blocks,
                                                                    kv * 128:(kv + 1) * 128].astype(BF16)
            for k in (2 * kk, 2 * kk + 1):
                sslab[kv, p * pg + k] = sel_refs[k][kv * 128:(kv + 1) * 128, :].astype(BF16)

    @pl.when(p == npg - 1)
    def _():
        new_t = [_pad_rows(rows_ref[i], LANES).T for i in range(2, 6)]
        zrows = jnp.zeros((15, 128), F32)
        for kv in range(2):
            for j in range(CMP_STRIDE):
                blk = jnp.concatenate([rows_ref[kv, j:j + 1, :], zrows], axis=0) if j < tp else jnp.zeros((16, 128), F32)
                cslab[kv, j, nblk:nblk + 16, :] = blk.astype(BF16)
            sslab[kv, n_pages] = new_t[kv].astype(BF16)
        pbuf_ref[...] = jnp.zeros(pbuf_ref.shape, F32)
        _compress(cslab, nblk + 16, wk_ref, wv_ref, pbuf_ref, kc_ref, vc_ref, by_phase=True)

        q = q_ref[0]
        gates = jax.nn.sigmoid(small_ref[0].astype(F32))
        R = A_REP * tp
        zq = jnp.zeros((R, A_HD), F32)
        q2 = jnp.concatenate([jnp.concatenate([_stack_heads(q, 0), zq], axis=1),
                              jnp.concatenate([zq, _stack_heads(q, 1)], axis=1)], axis=0).astype(BF16)
        rep2 = lambda x: jnp.concatenate([x] * (A_KV * A_REP), axis=0)
        ti = lax.broadcasted_iota(jnp.int32, (tp, SL), 0) + past
        causal = ti - lax.broadcasted_iota(jnp.int32, (tp, SL), 1) >= 0
        tc = lax.broadcasted_iota(jnp.int32, (tp, nblk), 0) + past
        cend = lax.broadcasted_iota(jnp.int32, (tp, nblk), 1) * CMP_STRIDE + (CMP_BLOCK - 1)
        wd = (lax.broadcasted_iota(jnp.int32, (tp, WK), 0) + WINDOW
              - lax.broadcasted_iota(jnp.int32, (tp, WK), 1))
        npt = n_pages + 1
        nck = 5 if npt % 5 == 0 else 1
        cpt = npt // nck
        ck = cpt * LANES

        s = lax.dot_general(q2, kc_ref[...].astype(BF16), NT, preferred_element_type=F32) + bcmp_ref[...]
        p_c = _masked_softmax(s, rep2(tc - cend >= 0))
        o_c = jnp.dot(p_c.astype(BF16), vc_ref[...].astype(BF16), preferred_element_type=F32)

        kmasks = []
        for g in range(A_KV):
            pg_ = p_c[g * R:(g + 1) * R]
            pc_sum = pg_[0:tp] + pg_[tp:2 * tp] + pg_[2 * tp:3 * tp] + pg_[3 * tp:4 * tp]
            sel = _select_rows(pc_sum, cov_ref, st_ref, past, ns, n_top, t_real)
            km = jnp.concatenate([jnp.broadcast_to(sel[:, b:b + 1], (tp, SEL_BLOCK))
                                  for b in range(SL // SEL_BLOCK)], axis=1)
            kmasks.append(_rep_rows(causal & (km > 0.5)))
        svalid = jnp.concatenate(kmasks, axis=0)

        def sel_t(kv, c):
            return jnp.concatenate([sslab[kv, t] for t in range(c * cpt, (c + 1) * cpt)], axis=1)

        for c in range(nck):
            s_ref[:, c * ck:(c + 1) * ck] = jnp.dot(q2, sel_t(0, c), preferred_element_type=F32)
        s_ref[...] = _masked_softmax(s_ref[...] + bsel_ref[...], svalid)
        o_s = jnp.zeros((A_KV * R, 2 * A_HD), F32)
        for c in range(nck):
            o_s = o_s + lax.dot_general(s_ref[:, c * ck:(c + 1) * ck].astype(BF16), sel_t(1, c), NT,
                                        preferred_element_type=F32)

        kw = jnp.concatenate([win_ref[0:128, :], new_t[2]], axis=1).astype(BF16)
        vw = jnp.concatenate([win_ref[128:256, :], new_t[3]], axis=1).astype(BF16)
        sw = jnp.dot(q2, kw, preferred_element_type=F32) + bwin_ref[...]
        p_w = _masked_softmax(sw, rep2((wd >= 0) & (wd < WINDOW)))
        o_w = lax.dot_general(p_w.astype(BF16), vw, NT, preferred_element_type=F32)

        for g in range(A_KV):
            gs = slice(g * A_HD, (g + 1) * A_HD)
            for r in range(A_REP):
                h = A_REP * g + r
                rs = slice(g * R + r * tp, g * R + (r + 1) * tp)
                out = (gates[:, 3 * h:3 * h + 1] * o_c[rs, gs] + gates[:, 3 * h + 1:3 * h + 2] * o_s[rs, gs]
                       + gates[:, 3 * h + 2:3 * h + 3] * o_w[rs, gs])
                o_ref[0, :, h * A_HD:(h + 1) * A_HD] = out


def _nsa_sample(qn3, rows6, u3, pool_cmp, pool_sel, win_t, layer, page_table, wk, wv, tb, past, t_real):
    B, tp, _ = qn3.shape
    n_pages = past // PAGE_SIZE
    pg = 32 if n_pages % 32 == 0 else n_pages
    npg = n_pages // pg
    SL = past + LANES
    nblk = past // CMP_STRIDE
    ns = -(-(past + t_real) // SEL_BLOCK)
    ns_pad = tb['cov'].shape[1]
    n_top = min(TOP_N, ns)
    WK = WINDOW + LANES
    kern = functools.partial(_nsa_sample_kernel, pg=pg, past=past, tp=tp, t_real=t_real, ns=ns, n_top=n_top)

    def page_spec(k):
        return pl.BlockSpec((None, None, 256, PAGE_SIZE), lambda b, p, pt: (layer, pt[b, p * pg + k], 0, 0))

    def c_(shape):
        return pl.BlockSpec(shape, lambda b, p, pt: (0,) * len(shape))

    R = A_REP * tp
    grid_spec = pltpu.PrefetchScalarGridSpec(
        num_scalar_prefetch=1,
        grid=(B, npg),
        in_specs=([page_spec(k) for k in range(pg)] + [page_spec(k) for k in range(pg)]
                  + [pl.BlockSpec((1, tp, W_A), lambda b, p, pt: (b, 0, 0)),
                     pl.BlockSpec((6, tp, 128), lambda b, p, pt: (0, b, 0)),
                     pl.BlockSpec((1, tp, LANES), lambda b, p, pt: (b, 0, OFF['small'] // LANES)),
                     pl.BlockSpec((None, None, 256, WINDOW), lambda b, p, pt: (layer, b, 0, 0)),
                     c_((CMP_STRIDE, 128, 256)), c_((CMP_STRIDE, 128, 256)),
                     c_((A_KV * R, nblk)), c_((A_KV * R, SL)), c_((A_KV * R, WK)),
                     c_((nblk, ns_pad)), c_((2 * PAGE_SIZE, 2 * PAGE_SIZE))]),
        out_specs=pl.BlockSpec((1, tp, W_A), lambda b, p, pt: (b, 0, 0)),
        scratch_shapes=[pltpu.VMEM((2, CMP_STRIDE, nblk + 16, 128), BF16),
                        pltpu.VMEM((2, n_pages + 1, 128, PAGE_SIZE), BF16),
                        pltpu.VMEM((nblk, 128), F32), pltpu.VMEM((nblk, 128), F32),
                        pltpu.VMEM((nblk + 16, 128), F32),
                        pltpu.VMEM((tp, ns_pad), F32), pltpu.VMEM((A_KV * R, SL), F32)],
    )
    return pl.pallas_call(
        kern,
        grid_spec=grid_spec,
        out_shape=jax.ShapeDtypeStruct((B, tp, W_A), F32),
        compiler_params=_cparams(("parallel", "arbitrary")),
        name="nsa_sample",
    )(page_table, *([pool_cmp] * pg), *([pool_sel] * pg), qn3, rows6, u3, win_t, wk, wv,
      *(tb[k].reshape(A_KV * R, -1) for k in ('bcmp', 'bsel', 'bwin')), tb['cov'], tb['perm'])


def _tri(L, lower_incl):
    r = lax.broadcasted_iota(jnp.int32, (L, L), 0)
    c = lax.broadcasted_iota(jnp.int32, (L, L), 1)
    return (r >= c) if lower_incl else (r > c)


def _mlstm_kernel(x_ref, small_ref, smallt_ref, brow_ref, bcol_ref, c0_ref, n0_ref, m0_ref,
                  h_ref, co_ref, no_ref, mo_ref, c_s, m_s, *, L, CB, t_real):
    c = pl.program_id(1)
    nc = pl.num_programs(1)

    @pl.when(c == 0)
    def _():
        c_s[:, :, 0:M_HD] = c0_ref[0]
        c_s[:, :, M_HD:2 * M_HD] = n0_ref[0]
        m_s[...] = m0_ref[0]

    sm = small_ref[0].astype(F32)
    low = _tri(L, True)
    lowf = low.astype(F32)
    cbs = range(CB)
    hd = range(M_HEADS)
    ch = [(cb, h) for cb in cbs for h in hd]

    li_col, b_col, b_row, li_row = [], [], [], []
    for cb in cbs:
        rs = slice(cb * L, (cb + 1) * L)
        smt = smallt_ref[0, cb]
        lic = sm[rs, SM_MIF:SM_MIF + 4] + brow_ref[0:1, 0:4]
        lfc = _log_sigmoid(sm[rs, SM_MIF + 4:SM_MIF + 8] + brow_ref[0:1, 4:8])
        lir = smt[0:4, :] + bcol_ref[0:4, :]
        lfr = _log_sigmoid(smt[4:8, :] + bcol_ref[4:8, :])
        if t_real % L:
            t0 = (c * CB + cb) * L
            tcol = lax.broadcasted_iota(jnp.int32, (L, 4), 0) + t0
            trow = lax.broadcasted_iota(jnp.int32, (4, L), 1) + t0
            lic = jnp.where(tcol < t_real, lic, NEG)
            lfc = jnp.where(tcol < t_real, lfc, 0.0)
            lir = jnp.where(trow < t_real, lir, NEG)
            lfr = jnp.where(trow < t_real, lfr, 0.0)
        li_col.append(lic)
        li_row.append(lir)
        b_col.append(jnp.dot(lowf, lfc, precision=HI, preferred_element_type=F32))
        b_row.append(lax.dot_general(lfr, lowf, NT, precision=HI, preferred_element_type=F32))

    rows = lambda cb: slice(cb * L, (cb + 1) * L)
    q = {(cb, h): x_ref[0, rows(cb), h * M_HD:(h + 1) * M_HD].astype(F32) for cb, h in ch}
    k = {(cb, h): x_ref[0, rows(cb), W_M + h * M_HD:W_M + (h + 1) * M_HD].astype(F32) * (M_HD ** -0.5)
         for cb, h in ch}
    v = {(cb, h): x_ref[0, rows(cb), 2 * W_M + h * M_HD:2 * W_M + (h + 1) * M_HD].astype(F32) for cb, h in ch}
    bc = {(cb, h): b_col[cb][:, h:h + 1] for cb, h in ch}
    Dm = {(cb, h): jnp.where(low, bc[cb, h] - b_row[cb][h:h + 1, :] + li_row[cb][h:h + 1, :], NEG) for cb, h in ch}
    dmax = {x: jnp.max(Dm[x], axis=-1, keepdims=True) for x in ch}
    qk = {x: lax.dot_general(q[x], k[x], NT, preferred_element_type=F32) for x in ch}
    bL = {x: bc[x][L - 1:L, :] for x in ch}
    wlog = {(cb, h): bL[cb, h] - bc[cb, h] + li_col[cb][:, h:h + 1] for cb, h in ch}
    wmax = {x: jnp.max(wlog[x], axis=0, keepdims=True) for x in ch}

    m_prev, m_new = {}, {}
    for h in hd:
        m = m_s[h][:, 0:1]
        for cb in cbs:
            m_prev[cb, h] = m
            m = jnp.maximum(bL[cb, h] + m, wmax[cb, h])
            m_new[cb, h] = m
    a = {x: bc[x] + m_prev[x] for x in ch}
    mt = {x: jnp.maximum(a[x], dmax[x]) for x in ch}
    S = {x: qk[x] * jnp.exp(Dm[x] - mt[x]) for x in ch}
    inter = {x: jnp.exp(a[x] - mt[x]) for x in ch}
    v1 = {x: jnp.concatenate([v[x], jnp.ones((L, M_HD), F32)], axis=1) for x in ch}
    Sv = {x: jnp.dot(S[x], v1[x], preferred_element_type=F32) for x in ch}
    dec = {x: jnp.exp(bL[x] + m_prev[x] - m_new[x]) for x in ch}
    kw = {x: k[x] * jnp.exp(wlog[x] - m_new[x]) for x in ch}
    kv = {x: lax.dot_general(kw[x], v1[x], TN, preferred_element_type=F32) for x in ch}

    CN = {h: c_s[h] for h in hd}
    for cb in cbs:
        qC = {h: jnp.dot(q[cb, h], CN[h], preferred_element_type=F32) for h in hd}
        for h in hd:
            x = (cb, h)
            num = inter[x] * qC[h] + Sv[x]
            den = num[:, M_HD:2 * M_HD]
            h_ref[0, rows(cb), h * M_HD:(h + 1) * M_HD] = (num[:, 0:M_HD]
                                                           / jnp.maximum(jnp.abs(den), jnp.exp(-mt[x])))
            CN[h] = dec[x] * CN[h] + kv[x]
    for h in hd:
        c_s[h] = CN[h]
        m_s[h] = jnp.broadcast_to(m_new[CB - 1, h], (1, LANES))

    @pl.when(c == nc - 1)
    def _():
        co_ref[0] = c_s[:, :, 0:M_HD]
        no_ref[0] = c_s[:, :, M_HD:2 * M_HD]
        mo_ref[0] = m_s[...]


def _mlstm(u3, smallt, m_bi, m_bf, C0, n0, m0, L, t_real):
    B, T, _ = u3.shape
    nc = T // L
    brow = jnp.concatenate([m_bi, m_bf]).reshape(1, 8)
    bcol = jnp.concatenate([m_bi, m_bf]).reshape(8, 1)
    n0 = jnp.broadcast_to(n0[:, :, :, None], (B, M_HEADS, M_HD, M_HD))
    m0 = jnp.broadcast_to(m0[:, :, None, None], (B, M_HEADS, 1, LANES))
    CB = 4 if nc % 4 == 0 else 1
    kern = functools.partial(_mlstm_kernel, L=L, CB=CB, t_real=t_real)
    st = lambda shape: pl.BlockSpec(shape, lambda b, c: (b,) + (0,) * (len(shape) - 1))
    h, Cn, nn, mn = pl.pallas_call(
        kern,
        grid=(B, nc // CB),
        in_specs=[pl.BlockSpec((1, CB * L, 3 * W_M), lambda b, c: (b, c, OFF['m_qkv'] // (3 * W_M))),
                  pl.BlockSpec((1, CB * L, LANES), lambda b, c: (b, c, OFF['small'] // LANES)),
                  pl.BlockSpec((1, CB, 16, L), lambda b, c: (b, c, 0, 0)),
                  pl.BlockSpec((1, 8), lambda b, c: (0, 0)), pl.BlockSpec((8, 1), lambda b, c: (0, 0)),
                  st((1, M_HEADS, M_HD, M_HD)), st((1, M_HEADS, M_HD, M_HD)), st((1, M_HEADS, 1, LANES))],
        out_specs=[pl.BlockSpec((1, CB * L, W_M), lambda b, c: (b, c, 0)),
                   st((1, M_HEADS, M_HD, M_HD)), st((1, M_HEADS, M_HD, M_HD)), st((1, M_HEADS, 1, LANES))],
        out_shape=[jax.ShapeDtypeStruct((B, T, W_M), F32),
                   jax.ShapeDtypeStruct((B, M_HEADS, M_HD, M_HD), F32),
                   jax.ShapeDtypeStruct((B, M_HEADS, M_HD, M_HD), F32),
                   jax.ShapeDtypeStruct((B, M_HEADS, 1, LANES), F32)],
        scratch_shapes=[pltpu.VMEM((M_HEADS, M_HD, 2 * M_HD), F32), pltpu.VMEM((M_HEADS, 1, LANES), F32)],
        compiler_params=_cparams(("parallel", "arbitrary")),
        name="mlstm",
    )(u3, u3, smallt, brow, bcol, C0, n0, m0)
    return h, Cn, nn[:, :, :, 0], mn[:, :, 0, 0]


def _gdn_a_kernel(x_ref, small_ref, smallt_ref, cw_ref, prow_ref, pcol_ref, buf_ref,
                  u_ref, w_ref, qe_ref, kd_ref, at_ref, eg_ref, xbuf, *, L, CB, t_real):
    c = pl.program_id(1)
    PRE = 8
    R = CB * L

    @pl.when(c == 0)
    def _():
        xbuf[0:PRE, :] = buf_ref[0]

    xbuf[PRE:PRE + R, :] = x_ref[0].astype(F32)
    conv = jnp.zeros((R, 3 * W_G), F32)
    for j in range(CONV_W):
        conv = conv + xbuf[pl.ds(PRE - (CONV_W - 1) + j, R), :] * cw_ref[j:j + 1, :]
    xbuf[0:PRE, :] = xbuf[R:R + PRE, :]
    conv = _silu(conv)

    sm = small_ref[0].astype(F32)
    g_all = -jnp.exp(prow_ref[0:1, 0:4]) * _softplus(sm[:, SM_GAB:SM_GAB + 4] + prow_ref[1:2, 0:4])
    beta_all = jax.nn.sigmoid(sm[:, SM_GAB + 4:SM_GAB + 8])
    lowf = _tri(L, True).astype(F32)

    H = G_HEADS
    lane_h = lax.broadcasted_iota(jnp.int32, (L, H * L), 1) // L
    row_p = lax.broadcasted_iota(jnp.int32, (L, H * L), 0)
    col_p = lax.broadcasted_iota(jnp.int32, (L, H * L), 1) % L
    blockmask = (lax.broadcasted_iota(jnp.int32, (H * L, H * L), 0) // L
                 == lax.broadcasted_iota(jnp.int32, (H * L, H * L), 1) // L)

    def pack_diag(full):
        out = full[0:L]
        for h in range(1, H):
            out = jnp.where(lane_h == h, full[h * L:(h + 1) * L], out)
        return out

    def block_diag(m, parts):
        if L % 16:
            return _split(jnp.where(blockmask, jnp.concatenate([m] * H, axis=0), 0.0))
        return tuple(jnp.where(blockmask, jnp.concatenate([p] * H, axis=0), jnp.zeros((), BF16))
                     for p in parts)

    a_list, rhs_list = [], []
    for cb in range(CB):
        rs = slice(cb * L, (cb + 1) * L)
        smt = smallt_ref[0, cb]
        g_col = g_all[rs]
        beta_col = beta_all[rs]
        g_row = -jnp.exp(pcol_ref[0:4, 0:1]) * _softplus(smt[8:12, :] + pcol_ref[0:4, 1:2])
        if t_real % L:
            t0 = (c * CB + cb) * L
            tcol = lax.broadcasted_iota(jnp.int32, (L, 4), 0) + t0
            trow = lax.broadcasted_iota(jnp.int32, (4, L), 1) + t0
            g_col = jnp.where(tcol < t_real, g_col, 0.0)
            beta_col = jnp.where(tcol < t_real, beta_col, 0.0)
            g_row = jnp.where(trow < t_real, g_row, 0.0)
        G_col = jnp.dot(lowf, g_col, precision=HI, preferred_element_type=F32)
        G_row = lax.dot_general(g_row, lowf, NT, precision=HI, preferred_element_type=F32)

        qs, ks, kbs, rhss = [], [], [], []
        for h in range(G_HEADS):
            hs = slice(h * G_HD, (h + 1) * G_HD)
            cq = conv[rs, h * G_HD:(h + 1) * G_HD]
            ck = conv[rs, W_G + h * G_HD:W_G + (h + 1) * G_HD]
            v = conv[rs, 2 * W_G + h * G_HD:2 * W_G + (h + 1) * G_HD]
            q = cq * lax.rsqrt(jnp.sum(cq * cq, axis=-1, keepdims=True) + EPS) * (G_HD ** -0.5)
            k = ck * lax.rsqrt(jnp.sum(ck * ck, axis=-1, keepdims=True) + EPS)
            Gc = G_col[:, h:h + 1]
            bcol = beta_col[:, h:h + 1]
            kb = k * bcol
            eG = jnp.exp(Gc)
            GL = Gc[L - 1:L, :]
            qs.append(q)
            ks.append(k)
            kbs.append(kb)
            rhss.append(jnp.concatenate([v * bcol, kb * eG], axis=1))
            qe_ref[0, rs, hs] = q * eG
            kd_ref[0, rs, hs] = k * jnp.exp(GL - Gc)
            eg_ref[0, cb, h:h + 1, :] = jnp.broadcast_to(jnp.exp(GL), (1, LANES))
        eg_ref[0, cb, G_HEADS:8, :] = jnp.zeros((8 - G_HEADS, LANES), F32)

        Gc_p = jnp.concatenate([jnp.broadcast_to(G_col[:, h:h + 1], (L, L)) for h in range(H)], axis=1)
        Gr_p = jnp.concatenate([jnp.broadcast_to(G_row[h:h + 1, :], (L, L)) for h in range(H)], axis=1)
        low_p = row_p >= col_p
        dmask = jnp.where(low_p, jnp.exp(jnp.where(low_p, Gc_p - Gr_p, 0.0)), 0.0)
        k_st = _split(jnp.concatenate(ks, axis=0))
        A = pack_diag(_dot3(_split(jnp.concatenate(kbs, axis=0)), k_st, NT)) * jnp.where(row_p > col_p, dmask, 0.0)
        attn = pack_diag(lax.dot_general(jnp.concatenate(qs, axis=0).astype(BF16), k_st[0], NT,
                                         preferred_element_type=F32)) * dmask
        at_ref[0, rs, :] = attn
        a_list.append(A)
        rhs_list.append(_split(jnp.concatenate(rhss, axis=0)))

    chunks = range(CB)
    X = [jnp.where(row_p == col_p, 1.0, 0.0) - a_list[cb] for cb in chunks]
    As = [_split(a_list[cb]) for cb in chunks]
    Pw = [_dot3(As[cb], block_diag(a_list[cb], As[cb])) for cb in chunks]
    span = 2
    while span < L:
        Ps = [_split(Pw[cb]) for cb in chunks]
        Pbd = [block_diag(Pw[cb], Ps[cb]) for cb in chunks]
        X = [X[cb] + _dot3(_split(X[cb]), Pbd[cb]) for cb in chunks]
        span *= 2
        if span < L:
            Pw = [_dot3(Ps[cb], Pbd[cb]) for cb in chunks]
    sol = [_dot3(block_diag(X[cb], _split(X[cb])), rhs_list[cb]) for cb in chunks]
    for cb in chunks:
        for h in range(H):
            rs = slice(cb * L, (cb + 1) * L)
            hs = slice(h * G_HD, (h + 1) * G_HD)
            u_ref[0, rs, hs] = sol[cb][h * L:(h + 1) * L, 0:G_HD]
            w_ref[0, rs, hs] = sol[cb][h * L:(h + 1) * L, G_HD:2 * G_HD]


def _gdn_b_kernel(u_ref, w_ref, qe_ref, kd_ref, at_ref, eg_ref, s0_ref, o_ref, so_ref, s_s, *, L, BB):
    c = pl.program_id(1)
    nc = pl.num_programs(1)

    @pl.when(c == 0)
    def _():
        s_s[...] = s0_ref[...]

    ch = [(b, h, slice(h * G_HD, (h + 1) * G_HD)) for b in range(BB) for h in range(G_HEADS)]
    S = [s_s[b, h] for b, h, _ in ch]
    r = [jnp.dot(jnp.concatenate([w_ref[b, :, hs], qe_ref[b, :, hs]], axis=0), S[i], preferred_element_type=F32)
         for i, (b, h, hs) in enumerate(ch)]
    v_new = [u_ref[b, :, hs] - r[i][0:L] for i, (b, h, hs) in enumerate(ch)]
    av = [jnp.dot(at_ref[b, :, h * L:(h + 1) * L], v_new[i], preferred_element_type=F32)
          for i, (b, h, hs) in enumerate(ch)]
    kdv = [lax.dot_general(kd_ref[b, :, hs], v_new[i], TN, preferred_element_type=F32)
           for i, (b, h, hs) in enumerate(ch)]
    for i, (b, h, hs) in enumerate(ch):
        o_ref[b, :, hs] = r[i][L:2 * L] + av[i]
        s_s[b, h] = eg_ref[b, 0, h:h + 1, :] * S[i] + kdv[i]

    @pl.when(c == nc - 1)
    def _():
        so_ref[...] = s_s[...]


def _gdn(u3, smallt, g_conv, g_A_log, g_dt_bias, buf, S0, L, t_real):
    B, T, _ = u3.shape
    nc = T // L
    CB = 8 if nc % 8 == 0 else 1
    BB = 8 if B % 8 == 0 else B
    prow = jnp.stack([g_A_log, g_dt_bias])
    pcol = jnp.stack([g_A_log, g_dt_bias], axis=1)
    buf8 = jnp.concatenate([jnp.zeros((B, 8 - (CONV_W - 1), 3 * W_G), F32), buf], axis=1)
    st = lambda shape: pl.BlockSpec(shape, lambda b, c: (b,) + (0,) * (len(shape) - 1))
    cst = lambda shape: pl.BlockSpec(shape, lambda b, c: (0,) * len(shape))
    row = lambda nb, r, w: pl.BlockSpec((nb, r, w), lambda b, c: (b, c, 0))
    tok = lambda w: jax.ShapeDtypeStruct((B, T, w), F32)
    u, w, qe, kd, at, eg = pl.pallas_call(
        functools.partial(_gdn_a_kernel, L=L, CB=CB, t_real=t_real),
        grid=(B, nc // CB),
        in_specs=[pl.BlockSpec((1, CB * L, 3 * W_G), lambda b, c: (b, c, OFF['g_qkv'] // (3 * W_G))),
                  pl.BlockSpec((1, CB * L, LANES), lambda b, c: (b, c, OFF['small'] // LANES)),
                  pl.BlockSpec((1, CB, 16, L), lambda b, c: (b, c, 0, 0)),
                  cst((CONV_W, 3 * W_G)), cst((2, 4)), cst((4, 2)), st((1, 8, 3 * W_G))],
        out_specs=[row(1, CB * L, W_G)] * 4 + [row(1, CB * L, G_HEADS * L),
                                               pl.BlockSpec((1, CB, 8, LANES), lambda b, c: (b, c, 0, 0))],
        out_shape=[tok(W_G)] * 4 + [tok(G_HEADS * L), jax.ShapeDtypeStruct((B, nc, 8, LANES), F32)],
        scratch_shapes=[pltpu.VMEM((CB * L + 8, 3 * W_G), F32)],
        compiler_params=_cparams(("parallel", "arbitrary")),
        name="gdn_a",
    )(u3, u3, smallt, g_conv, prow, pcol, buf8)
    sblk = pl.BlockSpec((BB, G_HEADS, G_HD, G_HD), lambda b, c: (b, 0, 0, 0))
    return pl.pallas_call(
        functools.partial(_gdn_b_kernel, L=L, BB=BB),
        grid=(B // BB, nc),
        in_specs=[row(BB, L, W_G)] * 4 + [row(BB, L, G_HEADS * L),
                                          pl.BlockSpec((BB, 1, 8, LANES), lambda b, c: (b, c, 0, 0)), sblk],
        out_specs=[row(BB, L, W_G), sblk],
        out_shape=[tok(W_G), jax.ShapeDtypeStruct((B, G_HEADS, G_HD, G_HD), F32)],
        scratch_shapes=[pltpu.VMEM((BB, G_HEADS, G_HD, G_HD), F32)],
        compiler_params=_cparams(("parallel", "arbitrary")),
        name="gdn_b",
    )(u, w, qe, kd, at, eg, S0)


def _head_rmsnorm(x, gain_ref, nheads, hd):
    outs = []
    for h in range(nheads):
        xh = x[:, h * hd:(h + 1) * hd]
        ms = jnp.mean(xh * xh, axis=-1, keepdims=True)
        outs.append(xh * lax.rsqrt(ms + EPS) * gain_ref[...])
    return jnp.concatenate(outs, axis=1)


def _merge_kernel(x_ref, oa_ref, hm_ref, og_ref, az_ref, mo_ref, mz_ref, gz_ref, mg_ref,
                  wb_ref, wo_ref, mhn_ref, ghn_ref, y_ref):
    oa = oa_ref[...] * _silu(az_ref[...].astype(F32))
    om = (_head_rmsnorm(hm_ref[...], mhn_ref, M_HEADS, M_HD) * jax.nn.sigmoid(mo_ref[...].astype(F32))
          * _silu(mz_ref[...].astype(F32)))
    og = _head_rmsnorm(og_ref[...], ghn_ref, G_HEADS, G_HD) * _silu(gz_ref[...].astype(F32))
    y = jnp.zeros(y_ref.shape, F32)
    for i, br in enumerate((oa, om, og)):
        proj = jnp.dot(br.astype(BF16), wb_ref[i], preferred_element_type=F32)
        y = y + jax.nn.sigmoid(mg_ref[:, i * D_MODEL:(i + 1) * D_MODEL].astype(F32)) * proj
    y_ref[...] = x_ref[...] + jnp.dot(y.astype(BF16), wo_ref[...], preferred_element_type=F32)


def _merge_out(x2, o_a, h_m, o_g, u2, w_branch, w_out, m_hn, g_hn):
    n = x2.shape[0]
    tm = min(512, n)
    row = lambda w, off: pl.BlockSpec((tm, w), lambda i: (i, off // w))
    cst = lambda shape: pl.BlockSpec(shape, lambda i: (0,) * len(shape))
    return pl.pallas_call(
        _merge_kernel,
        grid=(n // tm,),
        in_specs=[row(D_MODEL, 0), row(W_A, 0), row(W_M, 0), row(W_G, 0),
                  row(W_A, OFF['a_z']), row(W_M, OFF['m_o']), row(W_M, OFF['m_z']), row(W_G, OFF['g_z']),
                  row(N_BRANCH * D_MODEL, OFF['merge']),
                  cst((N_BRANCH, W_A, D_MODEL)), cst((D_MODEL, D_MODEL)), cst((1, M_HD)), cst((1, G_HD))],
        out_specs=row(D_MODEL, 0),
        out_shape=jax.ShapeDtypeStruct((n, D_MODEL), F32),
        compiler_params=_cparams(("parallel",)),
        name="merge_out",
    )(x2, o_a, h_m, o_g, u2, u2, u2, u2, u2, w_branch.astype(BF16), w_out.astype(BF16),
      m_hn.reshape(1, M_HD), g_hn.reshape(1, G_HD))


def _small_t(u3, L):
    B, T, _ = u3.shape
    s = u3[:, :, OFF['small'] + SM_MIF:OFF['small'] + SM_MIF + 16].astype(F32)
    return s.reshape(B, T // L, L, 16).transpose(0, 1, 3, 2)


def _layer(x, lw, tb, past, q_off):
    B, T, _ = x.shape
    if past is None:
        tp, L = T, math.gcd(T, M_CHUNK)
    else:
        tp = -(-T // 8) * 8
        L = tp
        x = jnp.pad(x, ((0, 0), (0, tp - T), (0, 0)))
    x2 = x.reshape(B * tp, D_MODEL)
    u2 = _proj_in(x2, lw['norm_g'], lw['w_perm'], BF16 if past is None else F32)
    u3 = u2.reshape(B, tp, N_PERM)
    if past is None:
        C0 = jnp.zeros((B, M_HEADS, M_HD, M_HD), F32)
        n0 = jnp.zeros((B, M_HEADS, M_HD), F32)
        m0 = jnp.zeros((B, M_HEADS), F32)
        S0 = jnp.zeros((B, G_HEADS, G_HD, G_HD), F32)
        buf = jnp.zeros((B, CONV_W - 1, 3 * W_G), F32)
    else:
        C0, n0, m0, S0, buf = past['mC'], past['mn'], past['mm'], past['gS'], past['gconv']
    qn, rows6, *rows_t = _prep_a(u2, lw['a_qn'], lw['a_kn'], T if past is None and T % 512 == 0 else None)
    qn3 = qn.reshape(B, tp, W_A)
    if past is None:
        o_a = _nsa_prompt(qn3, rows6, u3, lw['wk'], lw['wv'], tb)
    else:
        o_a = _nsa_sample(qn3, rows6, u3, past['cmp'], past['sel'], past['win_t'], past['layer'], past['page_table'],
                          lw['wk'], lw['wv'], tb, q_off, T)
    smallt = _small_t(u3, L)
    h_m, Cn, nn, mn = _mlstm(u3, smallt, lw['m_bi'], lw['m_bf'], C0, n0, m0, L, T)
    o_g, Sn = _gdn(u3, smallt, lw['g_conv'], lw['g_A_log'], lw['g_dt_bias'], buf, S0, L, T)
    y2 = _merge_out(x2, o_a.reshape(B * tp, W_A), h_m.reshape(B * tp, W_M), o_g.reshape(B * tp, W_G), u2,
                    lw['w_branch'], lw['w_out'], lw['m_hn'], lw['g_hn'])
    if rows_t:
        new_cmp, new_sel, new_win = (rows_t[b].reshape(B, 2, A_KV, A_HD, T).transpose(0, 4, 1, 2, 3)
                                     for b in range(3))
    else:
        rows = rows6.reshape(3, 2, B, tp, A_KV, A_HD)[:, :, :, :T]
        new_cmp, new_sel, new_win = (jnp.moveaxis(rows[b], 0, 2) for b in range(3))
    g_qkv = u3[:, max(T - (CONV_W - 1), 0):T, OFF['g_qkv']:OFF['g_qkv'] + 3 * W_G].astype(F32)
    if past is None:
        win = new_win
    else:
        win = jnp.concatenate([past['win'], new_win], axis=1)
    full = jnp.concatenate([buf, g_qkv], axis=1)
    state = dict(cmp=new_cmp, sel=new_sel, win=win[:, -min(WINDOW, win.shape[1]):],
                 mC=Cn, mn=nn, mm=mn, gS=Sn, gconv=full[:, -(CONV_W - 1):])
    return y2.reshape(B, tp, D_MODEL)[:, :T], state


def kernel(x_prompt, x_sample, cache_cmp_kv, cache_sel_kv, cache_win_kv, state_mlstm_C, state_mlstm_n,
           state_mlstm_m, state_gdn_S, state_gdn_conv, page_table, norm_g, w_in, a_qn, a_kn, a_cmp_wk,
           a_cmp_wv, rel_bias, m_bi, m_bf, m_hn, g_conv, g_A_log, g_dt_bias, g_hn, w_branch, w_out):
    names = ('cmp', 'sel', 'win', 'mC', 'mn', 'mm', 'gS', 'gconv')
    st_p = {k: [] for k in names}
    st_s = {k: [] for k in names}
    past_len = page_table.shape[1] * PAGE_SIZE
    n_pool = cache_cmp_kv.shape[1]
    pos_minor = lambda c: jnp.swapaxes(c.reshape(c.shape[0], c.shape[1], c.shape[2], 2 * A_KV * A_HD), 2, 3)
    pool_cmp, pool_sel, win_t = pos_minor(cache_cmp_kv), pos_minor(cache_sel_kv), pos_minor(cache_win_kv)
    w_t = jnp.swapaxes(w_in, 1, 2)
    db, dt = x_sample.shape[0], x_sample.shape[1]
    tb_p = _prompt_tables(rel_bias, x_prompt.shape[1])
    tb_s = _sample_tables(rel_bias, past_len, -(-dt // 8) * 8, dt)
    y_p, y_s = x_prompt, x_sample
    for l in range(DEPTH):
        lw = dict(norm_g=norm_g[l], w_perm=_permute_w_in(w_t, l), a_qn=a_qn[l], a_kn=a_kn[l],
                  wk=_cmp_weights(a_cmp_wk[l]), wv=_cmp_weights(a_cmp_wv[l]),
                  m_bi=m_bi[l], m_bf=m_bf[l], m_hn=m_hn[l], g_conv=g_conv[l], g_A_log=g_A_log[l],
                  g_dt_bias=g_dt_bias[l], g_hn=g_hn[l], w_branch=w_branch[l], w_out=w_out[l])
        y_p, new_p = _layer(y_p, lw, tb_p, None, 0)
        past = dict(cmp=pool_cmp, sel=pool_sel, win_t=win_t, layer=l, win=cache_win_kv[l], page_table=page_table,
                    mC=state_mlstm_C[l], mn=state_mlstm_n[l], mm=state_mlstm_m[l],
                    gS=state_gdn_S[l], gconv=state_gdn_conv[l])
        y_s, new_s = _layer(y_s, lw, tb_s, past, past_len)
        for k in names:
            st_p[k].append(new_p[k])
            st_s[k].append(new_s[k])
    P = {k: jnp.stack(v) for k, v in st_p.items()}
    S = {k: jnp.stack(v) for k, v in st_s.items()}
    return (y_p, y_s, P['cmp'], S['cmp'], P['sel'], S['sel'], P['win'], S['win'],
            P['mC'], S['mC'], P['mn'], S['mn'], P['mm'], S['mm'], P['gS'], S['gS'], P['gconv'], S['gconv'])
```

```python
import functools
import math

import jax
import jax.numpy as jnp
from jax import lax
from jax.experimental import pallas as pl
from jax.experimental.pallas import tpu as pltpu

D_MODEL = 1024
DEPTH = 2
PAGE_SIZE = 128
A_HEADS = 8
A_KV = 2
A_REP = A_HEADS // A_KV
A_HD = 64
CMP_BLOCK = 32
CMP_STRIDE = 16
SEL_BLOCK = 64
TOP_N = 16
WINDOW = 512
Q_BLOCK = 128
N_BUCKETS = 32
MAX_DIST = 2048
M_HEADS = 4
M_HD = 128
M_CHUNK = 64
G_HEADS = 4
G_HD = 128
G_CHUNK = 64
CONV_W = 4
W_A = A_HEADS * A_HD
W_M = M_HEADS * M_HD
W_G = G_HEADS * G_HD
N_BRANCH = 3
EPS = 1e-6
NEG = -1e30
TINY = 1e-30
LOG2E = math.log2(math.e)

F32 = jnp.float32
BF16 = jnp.bfloat16
HI = lax.Precision.HIGHEST
NN = (((1,), (0,)), ((), ()))
NT = (((1,), (1,)), ((), ()))
TN = (((0,), (0,)), ((), ()))

LANES = 128
VMEM_LIMIT = 56 * 1024 * 1024

IN_ORDER = ('a_q', 'a_kv', 'a_gate', 'a_z', 'm_qkv', 'm_if', 'm_o', 'm_z', 'g_qkv', 'g_ab', 'g_z', 'merge')
IN_WIDTH = dict(a_q=W_A, a_kv=3 * 2 * A_KV * A_HD, a_gate=A_HEADS * 3, a_z=W_A, m_qkv=3 * W_M, m_if=2 * M_HEADS,
                m_o=W_M, m_z=W_M, g_qkv=3 * W_G, g_ab=2 * G_HEADS, g_z=W_G, merge=N_BRANCH * D_MODEL)
OFF = dict(merge=0, m_qkv=3072, g_qkv=4608, a_q=6144, a_z=6656, m_o=7168, m_z=7680, g_z=8192, a_kv=8704,
           small=9472)
N_PERM = 9600
SM_GATE, SM_MIF, SM_GAB = 0, 24, 32


def _cparams(sem):
    return pltpu.CompilerParams(dimension_semantics=sem, vmem_limit_bytes=VMEM_LIMIT)


def _silu(x):
    return x * jax.nn.sigmoid(x)


def _log_sigmoid(x):
    return jnp.minimum(x, 0.0) - jnp.log(1.0 + jnp.exp(-jnp.abs(x)))


def _softplus(x):
    return jnp.maximum(x, 0.0) + jnp.log(1.0 + jnp.exp(-jnp.abs(x)))


def _split(a):
    hi = a.astype(BF16)
    return hi, (a - hi.astype(F32)).astype(BF16)


def _dot3(a, b, dims=NN):
    mm = lambda x, y: lax.dot_general(x, y, dims, preferred_element_type=F32)
    return mm(a[0], b[0]) + mm(a[0], b[1]) + mm(a[1], b[0])


def _src_offsets():
    offs, off = {}, 0
    for name in IN_ORDER:
        offs[name] = off
        off += IN_WIDTH[name]
    return offs, off


def _permute_kernel(w_ref, o_ref):
    src, _ = _src_offsets()
    small = []
    for name in IN_ORDER:
        w = IN_WIDTH[name]
        rows = w_ref[src[name]:src[name] + w, :]
        if w % LANES:
            small.append(rows)
        else:
            o_ref[:, OFF[name]:OFF[name] + w] = rows.T.astype(BF16)
    used = sum(r.shape[0] for r in small)
    small.append(jnp.zeros((LANES - used, w_ref.shape[1]), F32))
    o_ref[:, OFF['small']:N_PERM] = jnp.concatenate(small, axis=0).T.astype(BF16)


def _permute_w_in(w_t, layer):
    _, n_in, d = w_t.shape
    tr = 128
    return pl.pallas_call(
        _permute_kernel,
        grid=(d // tr,),
        in_specs=[pl.BlockSpec((None, n_in, tr), lambda i: (layer, 0, i))],
        out_specs=pl.BlockSpec((tr, N_PERM), lambda i: (i, 0)),
        out_shape=jax.ShapeDtypeStruct((d, N_PERM), BF16),
        compiler_params=_cparams(("parallel",)),
        name="permute_w",
    )(w_t)


def _rel_bucket(dist):
    n = jnp.maximum(dist, 0)
    exact = N_BUCKETS // 2
    nf = jnp.maximum(n, exact).astype(F32)
    large = exact + (jnp.log(nf / exact) / math.log(MAX_DIST / exact) * (N_BUCKETS - exact)).astype(jnp.int32)
    return jnp.where(n < exact, n, jnp.minimum(large, N_BUCKETS - 1))


def _bias_kernel(thr_ref, tab_ref, d_ref, o_ref):
    n = jnp.maximum(d_ref[0], 0)
    for h in range(A_HEADS):
        acc = jnp.full(n.shape, tab_ref[h], F32)
        for k in range(1, N_BUCKETS):
            acc = jnp.where(n >= thr_ref[k], tab_ref[k * A_HEADS + h], acc)
        o_ref[0, h // A_REP, h % A_REP] = acc


def _bias_rows(rel_bias, dist, split_rows=False):
    N, Q, K = dist.shape
    nmax = 2 * MAX_DIST
    thr = jnp.sum(_rel_bucket(jnp.arange(nmax))[None, :] < jnp.arange(N_BUCKETS)[:, None], axis=1).astype(jnp.int32)
    smem = pl.BlockSpec(memory_space=pltpu.SMEM)
    out = pl.pallas_call(
        _bias_kernel,
        grid=(N,),
        in_specs=[smem, smem, pl.BlockSpec((1, Q, K), lambda i: (i, 0, 0))],
        out_specs=pl.BlockSpec((1, A_KV, A_REP, Q, K), lambda i: (i, 0, 0, 0, 0)),
        out_shape=jax.ShapeDtypeStruct((N, A_KV, A_REP, Q, K), F32),
        compiler_params=_cparams(("parallel",)),
        name="bias_rows",
    )(thr, rel_bias.astype(F32).reshape(N_BUCKETS * A_HEADS), dist.astype(jnp.int32))
    return out if split_rows else out.reshape(N, A_KV, A_REP * Q, K)


def _cmp_weights(w):
    wr = w.reshape(A_KV, 2, CMP_STRIDE, A_HD, A_HD)
    eye = jnp.eye(A_KV, dtype=w.dtype)
    full = jnp.einsum('gmjde,gh->jgdmhe', wr, eye)
    return full.reshape(CMP_STRIDE, A_KV * A_HD, 2 * A_KV * A_HD).astype(BF16)


def _cover_t(ns_pad, nch):
    s0 = jnp.arange(ns_pad)[:, None] * SEL_BLOCK
    c0 = jnp.arange(nch)[None, :] * CMP_STRIDE
    return ((c0 < s0 + SEL_BLOCK) & (s0 <= c0 + CMP_BLOCK - 1)).astype(F32)


def _gate_expand():
    col = jnp.arange(N_BRANCH * W_A)
    src = 3 * ((col % W_A) // A_HD) + col // W_A
    return (jnp.arange(LANES)[:, None] == src[None, :]).astype(BF16)


def _phase_perm(pages=2):
    n = pages * PAGE_SIZE
    r = jnp.arange(n)
    blocks = n // CMP_STRIDE
    return (jnp.arange(n)[None, :] == ((r % blocks) * CMP_STRIDE + r // blocks)[:, None]).astype(BF16)


def _prompt_tables(rel_bias, T):
    QB = Q_BLOCK
    nqb = T // QB
    nch = T // CMP_STRIDE
    ns = T // SEL_BLOCK
    ns_pad = -(-ns // 8) * 8
    i_ = jnp.arange(QB)
    t = (jnp.arange(nqb) * QB)[:, None, None] + i_[None, :, None]
    cend = (jnp.arange(nch) * CMP_STRIDE + CMP_BLOCK - 1)[None, None, :]
    d = (jnp.arange(nqb) * QB)[:, None, None] + i_[None, :, None] - i_[None, None, :]
    nw = min(WINDOW // QB + 1, nqb)
    dt = jnp.swapaxes(d, 1, 2)
    tsel = _bias_rows(rel_bias * LOG2E, dt, split_rows=True)
    return dict(
        bcmp=_bias_rows(rel_bias, t - cend),
        tselt=tsel.transpose(0, 1, 3, 2, 4).reshape(nqb, A_KV, QB, A_REP * QB),
        covt=_cover_t(ns_pad, nch),
        twin=(tsel[:nw] + jnp.where((dt[:nw] >= 0) & (dt[:nw] < WINDOW), 0.0, NEG)[:, None, None]
              ).transpose(0, 1, 3, 2, 4).reshape(nw, A_KV, QB, A_REP * QB),
        cmt=jnp.where(dt[:2] >= 0, 0.0, NEG),
        gx=_gate_expand())


def _sample_tables(rel_bias, past, tp, t_real):
    SL = past + LANES
    nblk = past // CMP_STRIDE
    ns = -(-(past + t_real) // SEL_BLOCK)
    ns_pad = -(-ns // LANES) * LANES
    WK = WINDOW + LANES
    i_ = jnp.arange(tp)
    t = past + i_
    rows = lambda dist: _bias_rows(rel_bias, dist[None])[0]
    return dict(
        bcmp=rows(t[:, None] - (jnp.arange(nblk) * CMP_STRIDE + CMP_BLOCK - 1)[None, :]),
        bsel=rows(t[:, None] - jnp.arange(SL)[None, :]),
        bwin=rows(i_[:, None] + WINDOW - jnp.arange(WK)[None, :]),
        cov=_cover_t(ns_pad, nblk).T, perm=_phase_perm())


def _proj_in_kernel(x_ref, g_ref, w_ref, o_ref, hn_ref):
    @pl.when(pl.program_id(1) == 0)
    def _():
        x = x_ref[...]
        ms = jnp.mean(x * x, axis=-1, keepdims=True)
        hn_ref[...] = (x * lax.rsqrt(ms + EPS) * g_ref[...]).astype(BF16)

    o_ref[...] = jnp.dot(hn_ref[...], w_ref[...], preferred_element_type=F32).astype(o_ref.dtype)


def _proj_in(x2, norm_g, w_perm, out_dtype):
    n = x2.shape[0]
    tm = min(2048, n)
    tn = 1920
    return pl.pallas_call(
        _proj_in_kernel,
        grid=(n // tm, N_PERM // tn),
        in_specs=[pl.BlockSpec((tm, D_MODEL), lambda i, j: (i, 0)),
                  pl.BlockSpec((1, D_MODEL), lambda i, j: (0, 0)),
                  pl.BlockSpec((D_MODEL, tn), lambda i, j: (0, j))],
        out_specs=pl.BlockSpec((tm, tn), lambda i, j: (i, j)),
        out_shape=jax.ShapeDtypeStruct((n, N_PERM), out_dtype),
        scratch_shapes=[pltpu.VMEM((tm, D_MODEL), BF16)],
        compiler_params=_cparams(("parallel", "arbitrary")),
        name="proj_in",
    )(x2, norm_g.reshape(1, D_MODEL), w_perm)


def _prep_a_kernel(q_ref, kv0_ref, kv1_ref, kv2_ref, bdq_ref, bdk_ref, qg_ref, kg_ref, qo_ref, ro_ref, *rt_ref):
    def group_mean(x2, bd_ref):
        hi, lo = _split(x2)
        return (jnp.dot(hi, bd_ref[...], preferred_element_type=F32)
                + jnp.dot(lo, bd_ref[...], preferred_element_type=F32)) * (1.0 / A_HD)

    q = q_ref[...].astype(F32)
    ms = group_mean(q * q, bdq_ref)
    qo_ref[...] = q * lax.rsqrt(ms + EPS) * qg_ref[...] * (A_HD ** -0.5)
    for b, kv_ref in enumerate((kv0_ref, kv1_ref, kv2_ref)):
        k = kv_ref[:, 0:128].astype(F32)
        ms = group_mean(k * k, bdk_ref)
        kn = k * lax.rsqrt(ms + EPS) * kg_ref[b:b + 1, :]
        v = kv_ref[:, 128:256].astype(F32)
        ro_ref[2 * b] = kn
        ro_ref[2 * b + 1] = v
        if rt_ref:
            rt_ref[b][0:128, :] = kn.T
            rt_ref[b][128:256, :] = v.T


def _prep_a(u2, a_qn, a_kn, seq=None):
    n = u2.shape[0]
    tm = min(512, n)
    out_specs = [pl.BlockSpec((tm, W_A), lambda i: (i, 0)), pl.BlockSpec((6, tm, 128), lambda i: (0, i, 0))]
    out_shape = [jax.ShapeDtypeStruct((n, W_A), F32), jax.ShapeDtypeStruct((6, n, 128), F32)]
    if seq is not None:
        per = seq // tm
        out_specs += [pl.BlockSpec((None, 256, tm), lambda i: (i // per, 0, i % per))] * 3
        out_shape += [jax.ShapeDtypeStruct((n // seq, 256, seq), F32)] * 3
    bd = lambda w: (jnp.arange(w)[:, None] // A_HD == jnp.arange(w)[None, :] // A_HD).astype(BF16)
    qg = jnp.tile(a_qn, A_HEADS).reshape(1, W_A)
    kg = jnp.tile(a_kn, (1, A_KV))
    const = lambda shape: pl.BlockSpec(shape, lambda i: (0, 0))
    kvs = lambda b: pl.BlockSpec((tm, 256), lambda i: (i, OFF['a_kv'] // 256 + b))
    return pl.pallas_call(
        _prep_a_kernel,
        grid=(n // tm,),
        in_specs=[pl.BlockSpec((tm, W_A), lambda i: (i, OFF['a_q'] // W_A)), kvs(0), kvs(1), kvs(2),
                  const((W_A, W_A)), const((128, 128)), const((1, W_A)), const((3, 128))],
        out_specs=out_specs,
        out_shape=out_shape,
        compiler_params=_cparams(("parallel",)),
        name="prep_a",
    )(u2, u2, u2, u2, bd(W_A), bd(128), qg, kg)


def _stack_heads(q, g):
    return jnp.concatenate([q[:, (A_REP * g + r) * A_HD:(A_REP * g + r + 1) * A_HD] for r in range(A_REP)], axis=0)


def _rep_rows(x):
    return jnp.concatenate([x] * A_REP, axis=0)


def _masked_softmax(s, valid):
    sm = jnp.where(valid, s, NEG)
    m = jnp.max(sm, axis=-1, keepdims=True)
    e = jnp.where(valid, jnp.exp(sm - m), 0.0)
    l = jnp.sum(e, axis=-1, keepdims=True)
    return e / jnp.maximum(l, TINY)


def _select_blocks(pc_sum, covt_ref, st_ref, t_row, ns, n_top):
    ns_pad = covt_ref.shape[0]
    imp = lax.dot_general(covt_ref[...], pc_sum, NT, precision=HI, preferred_element_type=F32)
    jj = lax.broadcasted_iota(jnp.int32, (ns_pad, LANES), 0)
    tt = jnp.broadcast_to(t_row, (ns_pad, LANES))
    cur = tt // SEL_BLOCK
    forced = (jj == 0) | (jj == cur) | (jj == cur - 1)
    future = jj * SEL_BLOCK > tt
    score = jnp.where(future, NEG, jnp.where(forced, -NEG, imp))
    score = jnp.where(jj < ns, score, -jnp.inf)
    st_ref[...] = score

    def beats(k, rank):
        row = jnp.broadcast_to(st_ref[pl.ds(k, 1), :], (ns_pad, LANES))
        b = (row > score) | ((row == score) & (k < jj))
        return rank + jnp.where(b, 1.0, 0.0)

    rank = jnp.zeros((ns_pad, LANES), F32)
    if ns <= 32:
        for k in range(ns):
            rank = beats(k, rank)
    else:
        rank = lax.fori_loop(0, ns, beats, rank)
    return jnp.where(rank < n_top, 1.0, 0.0)


def _pad_rows(x, rows):
    if x.shape[0] == rows:
        return x
    return jnp.concatenate([x, jnp.zeros((rows - x.shape[0], x.shape[1]), x.dtype)], axis=0)


def _select_rows(pc_sum, cov_ref, sel_ref, t0, ns, n_top, nq):
    Q = pc_sum.shape[0]
    ns_pad = cov_ref.shape[1]
    nk = -(-ns // 8) * 8
    imp = jnp.dot(pc_sum, cov_ref[...], precision=HI, preferred_element_type=F32)
    jj = lax.broadcasted_iota(jnp.int32, (Q, ns_pad), 1)
    tt = lax.broadcasted_iota(jnp.int32, (Q, ns_pad), 0) + t0
    cur = tt // SEL_BLOCK
    forced = (jj == 0) | (jj == cur) | (jj == cur - 1)
    future = jj * SEL_BLOCK > tt
    score = jnp.where(future, NEG, jnp.where(forced, -NEG, imp))
    score = jnp.where(jj < ns, score, -jnp.inf)
    score_col = _pad_rows(score, LANES).T
    kk = lax.broadcasted_iota(jnp.int32, (nk, ns_pad), 0)
    jl = lax.broadcasted_iota(jnp.int32, (nk, ns_pad), 1)
    sel_ref[...] = jnp.zeros(sel_ref.shape, F32)
    for i in range(nq):
        col = jnp.broadcast_to(score_col[0:nk, i:i + 1], (nk, ns_pad))
        row = jnp.broadcast_to(score[i:i + 1, :], (nk, ns_pad))
        beats = (col > row) | ((col == row) & (kk < jl))
        rank = jnp.sum(jnp.where(beats, 1.0, 0.0), axis=0, keepdims=True)
        sel_ref[i:i + 1, :] = jnp.where(rank < n_top, 1.0, 0.0)
    return sel_ref[...]


def _compress(src_ref, nrow, wk_ref, wv_ref, pbuf_ref, kc_ref, vc_ref, by_phase=False):
    acck = jnp.zeros((nrow, 256), F32)
    accv = jnp.zeros((nrow, 256), F32)
    for j in range(CMP_STRIDE):
        if by_phase:
            xk = src_ref[0, j, 0:nrow, :].astype(BF16)
            xv = src_ref[1, j, 0:nrow, :].astype(BF16)
        else:
            xk = src_ref[0, pl.ds(j, nrow, stride=CMP_STRIDE), :].astype(BF16)
            xv = src_ref[1, pl.ds(j, nrow, stride=CMP_STRIDE), :].astype(BF16)
        acck = acck + jnp.dot(xk, wk_ref[j], preferred_element_type=F32)
        accv = accv + jnp.dot(xv, wv_ref[j], preferred_element_type=F32)
    nout = kc_ref.shape[0]
    for acc, dst in ((acck, kc_ref), (accv, vc_ref)):
        pbuf_ref[0:nrow, :] = acc[:, 128:256]
        dst[...] = acc[0:nout, 0:128] + pbuf_ref[pl.ds(1, nout), :]


def _gated_sum(gates, gx_ref, ocat_ref):
    gh, gl = _split(gates)
    gexp = (jnp.dot(gh, gx_ref[...], preferred_element_type=F32)
            + jnp.dot(gl, gx_ref[...], preferred_element_type=F32))
    out = gexp[:, 0:W_A] * ocat_ref[0]
    for c in range(1, N_BRANCH):
        out = out + gexp[:, c * W_A:(c + 1) * W_A] * ocat_ref[c]
    return out


def _nsa_prompt_kernel(q_ref, rows_ref, small_ref, wk_ref, wv_ref, bcmp_ref, tselt_ref, covt_ref,
                       twin_ref, cmt_ref, gx_ref,
                       o_ref, kc_ref, vc_ref, pbuf_ref, mb_ref, acc_ref, m_ref, l_ref, st_ref, ocat_ref, ocatt_ref,
                       *, ns, n_top):
    bi = pl.program_id(1)
    T = rows_ref.shape[1]
    nch = T // CMP_STRIDE
    QB = Q_BLOCK
    SEL, WIN = 1, 2

    @pl.when(bi == 0)
    def _():
        pbuf_ref[...] = jnp.zeros(pbuf_ref.shape, F32)
        _compress(rows_ref, nch, wk_ref, wv_ref, pbuf_ref, kc_ref, vc_ref)

    t0 = bi * QB
    q = q_ref[0]
    tc = lax.broadcasted_iota(jnp.int32, (QB, nch), 0) + t0
    cend = lax.broadcasted_iota(jnp.int32, (QB, nch), 1) * CMP_STRIDE + (CMP_BLOCK - 1)
    cvalid = _rep_rows(tc - cend >= 0)
    t_row = lax.broadcasted_iota(jnp.int32, (1, LANES), 1) + t0
    qg = [_stack_heads(q, g).astype(BF16) for g in range(A_KV)]
    qt = (q * LOG2E).T
    qgt = [jnp.concatenate([qt[(A_REP * g + r) * A_HD:(A_REP * g + r + 1) * A_HD] for r in range(A_REP)],
                           axis=1).astype(BF16) for g in range(A_KV)]

    for g in range(A_KV):
        kc = kc_ref[:, g * A_HD:(g + 1) * A_HD].astype(BF16)
        vc = vc_ref[:, g * A_HD:(g + 1) * A_HD].astype(BF16)
        s = lax.dot_general(qg[g], kc, NT, preferred_element_type=F32) + bcmp_ref[0, g]
        p_c = _masked_softmax(s, cvalid)
        o_c = jnp.dot(p_c.astype(BF16), vc, preferred_element_type=F32)
        for r in range(A_REP):
            h = A_REP * g + r
            ocat_ref[0, :, h * A_HD:(h + 1) * A_HD] = o_c[r * QB:(r + 1) * QB]
        pc_sum = p_c[0:QB] + p_c[QB:2 * QB] + p_c[2 * QB:3 * QB] + p_c[3 * QB:4 * QB]
        sel_t = _select_blocks(pc_sum, covt_ref, st_ref, t_row, ns, n_top)
        mb_ref[g] = (sel_t - 1.0) * (-NEG)

    m_ref[...] = jnp.full(m_ref.shape, NEG, F32)
    l_ref[...] = jnp.zeros(l_ref.shape, F32)
    acc_ref[...] = jnp.zeros(acc_ref.shape, F32)
    lanes4 = lambda x: jnp.concatenate([x] * A_REP, axis=1)
    half = SEL_BLOCK

    def tiles(specs):
        cat = lambda xs: xs[0] if len(xs) == 1 else jnp.concatenate(xs, axis=0)
        cis = [2 * (br - 1) + g for br, g, _, _ in specs]
        sms, vs = [], []
        for br, g, kb, nb in specs:
            ks, vv, bs, ms = [], [], [], []
            for j in range(nb):
                off = pl.multiple_of((kb + j) * QB, QB)
                ks.append(rows_ref[2 * br, pl.ds(off, QB), g * A_HD:(g + 1) * A_HD])
                vv.append(rows_ref[2 * br + 1, pl.ds(off, QB), g * A_HD:(g + 1) * A_HD])
                if br == SEL:
                    bs.append(tselt_ref[bi - kb - j, g])
                    blk = (QB // half) * (kb + j)
                    mt = jnp.concatenate([jnp.broadcast_to(mb_ref[g, pl.ds(blk + a, 1), :], (half, QB))
                                          for a in range(QB // half)], axis=0)
                    ms.append(mt + cmt_ref[jnp.minimum(bi - kb, 1)] if nb == 1 else mt)
                else:
                    bs.append(twin_ref[bi - kb - j, g])
            vs.append(cat(vv).astype(BF16))
            sc = jnp.dot(cat(ks).astype(BF16), qgt[g], preferred_element_type=F32) + cat(bs)
            sms.append(sc + lanes4(cat(ms)) if ms else sc)
        m_prevs = [m_ref[ci] for ci in cis]
        m_news = [jnp.maximum(mp, jnp.max(sm, axis=0, keepdims=True)) for mp, sm in zip(m_prevs, sms)]
        ps = [jnp.exp2(sm - mn) for sm, mn in zip(sms, m_news)]
        sums = [jnp.sum(p, axis=0, keepdims=True) for p in ps]
        pvs = [lax.dot_general(v, p.astype(BF16), TN, preferred_element_type=F32) for v, p in zip(vs, ps)]
        for ci, mp, mn, sm_, pv in zip(cis, m_prevs, m_news, sums, pvs):
            alpha = jnp.exp2(mp - mn)
            l_ref[ci] = alpha * l_ref[ci] + sm_
            acc_ref[ci] = alpha * acc_ref[ci] + pv
            m_ref[ci] = mn

    def body_sel(kb, carry):
        tiles([(SEL, g, kb, 1) for g in range(A_KV)])
        return carry

    def body_sel2(pair, carry):
        tiles([(SEL, g, 2 * pair, 2) for g in range(A_KV)])
        return carry

    def body_both(kb, carry):
        tiles([(br, g, kb, 1) for g in range(A_KV) for br in (SEL, WIN)])
        return carry

    lo = jnp.maximum(bi - WINDOW // QB, 0)
    lax.fori_loop(0, lo // 2, body_sel2, 0)
    lax.fori_loop(2 * (lo // 2), lo, body_sel, 0)
    lax.fori_loop(lo, bi + 1, body_both, 0)

    for br in (SEL, WIN):
        for g in range(A_KV):
            ci = 2 * (br - 1) + g
            o_t = acc_ref[ci] / jnp.maximum(l_ref[ci], TINY)
            for r in range(A_REP):
                h = A_REP * g + r
                ocatt_ref[br - 1, h * A_HD:(h + 1) * A_HD, :] = o_t[:, r * QB:(r + 1) * QB]
        ocat_ref[br] = ocatt_ref[br - 1].T

    o_ref[0] = _gated_sum(jax.nn.sigmoid(small_ref[0].astype(F32)), gx_ref, ocat_ref)


def _nsa_prompt(qn3, rows6, u3, wk, wv, tb):
    B, T, _ = qn3.shape
    QB = Q_BLOCK
    nqb = T // QB
    nch = T // CMP_STRIDE
    ns = T // SEL_BLOCK
    ns_pad = tb['covt'].shape[0]
    n_top = min(TOP_N, ns)
    nw = tb['twin'].shape[0]
    kern = functools.partial(_nsa_prompt_kernel, ns=ns, n_top=n_top)
    c2 = lambda shape: pl.BlockSpec(shape, lambda b, i: (0,) * len(shape))
    return pl.pallas_call(
        kern,
        grid=(B, nqb),
        in_specs=[pl.BlockSpec((1, QB, W_A), lambda b, i: (b, i, 0)),
                  pl.BlockSpec((6, T, 128), lambda b, i: (0, b, 0)),
                  pl.BlockSpec((1, QB, LANES), lambda b, i: (b, i, OFF['small'] // LANES)),
                  c2((CMP_STRIDE, 128, 256)), c2((CMP_STRIDE, 128, 256)),
                  pl.BlockSpec((1, A_KV, A_REP * QB, nch), lambda b, i: (i, 0, 0, 0)),
                  c2((nqb, A_KV, QB, A_REP * QB)),
                  c2((ns_pad, nch)), c2((nw, A_KV, QB, A_REP * QB)), c2((2, QB, QB)),
                  c2((LANES, N_BRANCH * W_A))],
        out_specs=pl.BlockSpec((1, QB, W_A), lambda b, i: (b, i, 0)),
        out_shape=jax.ShapeDtypeStruct((B, T, W_A), F32),
        scratch_shapes=[pltpu.VMEM((nch, 128), F32), pltpu.VMEM((nch, 128), F32),
                        pltpu.VMEM((nch + 8, 128), F32),
                        pltpu.VMEM((A_KV, ns_pad, QB), F32),
                        pltpu.VMEM((2 * A_KV, A_HD, A_REP * QB), F32),
                        pltpu.VMEM((2 * A_KV, 1, A_REP * QB), F32),
                        pltpu.VMEM((2 * A_KV, 1, A_REP * QB), F32),
                        pltpu.VMEM((ns_pad, LANES), F32),
                        pltpu.VMEM((N_BRANCH, QB, W_A), F32),
                        pltpu.VMEM((2, W_A, QB), F32)],
        compiler_params=_cparams(("parallel", "arbitrary")),
        name="nsa_prompt",
    )(qn3, rows6, u3, wk, wv, tb['bcmp'], tb['tselt'], tb['covt'], tb['twin'], tb['cmt'], tb['gx'])


def _nsa_sample_kernel(pt_ref, *refs, pg, past, tp, t_real, ns, n_top):
    cmp_refs = refs[0:pg]
    sel_refs = refs[pg:2 * pg]
    (q_ref, rows_ref, small_ref, win_ref, wk_ref, wv_ref, bcmp_ref, bsel_ref, bwin_ref, cov_ref, perm_ref,
     o_ref, cslab, sslab, kc_ref, vc_ref, pbuf_ref, st_ref, s_ref) = refs[2 * pg:]
    p = pl.program_id(1)
    npg = pl.num_programs(1)
    n_pages = past // PAGE_SIZE
    SL = (n_pages + 1) * PAGE_SIZE
    nblk = kc_ref.shape[0]
    WK = WINDOW + LANES
    pair_blocks = 2 * PAGE_SIZE // CMP_STRIDE

    for kk in range(pg // 2):
        page = p * pg + 2 * kk
        row0 = pl.multiple_of(page * (PAGE_SIZE // CMP_STRIDE), pair_blocks)
        pair = jnp.concatenate([cmp_refs[2 * kk][...], cmp_refs[2 * kk + 1][...]], axis=1).astype(BF16)
        ordered = lax.dot_general(perm_ref[...], pair, NT, preferred_element_type=F32)
        for kv in range(2):
            for j in range(CMP_STRIDE):
                cslab[kv, j, pl.ds(row0, pair_blocks), :] = ordered[j * pair_blocks:(j + 1) * pair_blocks,
                                                                    kv * 128:(kv + 1) * 128].astype(BF16)
            for k in (2 * kk, 2 * kk + 1):
                sslab[kv, p * pg + k] = sel_refs[k][kv * 128:(kv + 1) * 128, :].astype(BF16)

    @pl.when(p == npg - 1)
    def _():
        new_t = [_pad_rows(rows_ref[i], LANES).T for i in range(2, 6)]
        zrows = jnp.zeros((15, 128), F32)
        for kv in range(2):
            for j in range(CMP_STRIDE):
                blk = jnp.concatenate([rows_ref[kv, j:j + 1, :], zrows], axis=0) if j < tp else jnp.zeros((16, 128), F32)
                cslab[kv, j, nblk:nblk + 16, :] = blk.astype(BF16)
            sslab[kv, n_pages] = new_t[kv].astype(BF16)
        pbuf_ref[...] = jnp.zeros(pbuf_ref.shape, F32)
        _compress(cslab, nblk + 16, wk_ref, wv_ref, pbuf_ref, kc_ref, vc_ref, by_phase=True)

        q = q_ref[0]
        gates = jax.nn.sigmoid(small_ref[0].astype(F32))
        R = A_REP * tp
        zq = jnp.zeros((R, A_HD), F32)
        q2 = jnp.concatenate([jnp.concatenate([_stack_heads(q, 0), zq], axis=1),
                              jnp.concatenate([zq, _stack_heads(q, 1)], axis=1)], axis=0).astype(BF16)
        rep2 = lambda x: jnp.concatenate([x] * (A_KV * A_REP), axis=0)
        ti = lax.broadcasted_iota(jnp.int32, (tp, SL), 0) + past
        causal = ti - lax.broadcasted_iota(jnp.int32, (tp, SL), 1) >= 0
        tc = lax.broadcasted_iota(jnp.int32, (tp, nblk), 0) + past
        cend = lax.broadcasted_iota(jnp.int32, (tp, nblk), 1) * CMP_STRIDE + (CMP_BLOCK - 1)
        wd = (lax.broadcasted_iota(jnp.int32, (tp, WK), 0) + WINDOW
              - lax.broadcasted_iota(jnp.int32, (tp, WK), 1))
        npt = n_pages + 1
        nck = 5 if npt % 5 == 0 else 1
        cpt = npt // nck
        ck = cpt * LANES

        s = lax.dot_general(q2, kc_ref[...].astype(BF16), NT, preferred_element_type=F32) + bcmp_ref[...]
        p_c = _masked_softmax(s, rep2(tc - cend >= 0))
        o_c = jnp.dot(p_c.astype(BF16), vc_ref[...].astype(BF16), preferred_element_type=F32)

        kmasks = []
        for g in range(A_KV):
            pg_ = p_c[g * R:(g + 1) * R]
            pc_sum = pg_[0:tp] + pg_[tp:2 * tp] + pg_[2 * tp:3 * tp] + pg_[3 * tp:4 * tp]
            sel = _select_rows(pc_sum, cov_ref, st_ref, past, ns, n_top, t_real)
            km = jnp.concatenate([jnp.broadcast_to(sel[:, b:b + 1], (tp, SEL_BLOCK))
                                  for b in range(SL // SEL_BLOCK)], axis=1)
            kmasks.append(_rep_rows(causal & (km > 0.5)))
        svalid = jnp.concatenate(kmasks, axis=0)

        def sel_t(kv, c):
            return jnp.concatenate([sslab[kv, t] for t in range(c * cpt, (c + 1) * cpt)], axis=1)

        for c in range(nck):
            s_ref[:, c * ck:(c + 1) * ck] = jnp.dot(q2, sel_t(0, c), preferred_element_type=F32)
        s_ref[...] = _masked_softmax(s_ref[...] + bsel_ref[...], svalid)
        o_s = jnp.zeros((A_KV * R, 2 * A_HD), F32)
        for c in range(nck):
            o_s = o_s + lax.dot_general(s_ref[:, c * ck:(c + 1) * ck].astype(BF16), sel_t(1, c), NT,
                                        preferred_element_type=F32)

        kw = jnp.concatenate([win_ref[0:128, :], new_t[2]], axis=1).astype(BF16)
        vw = jnp.concatenate([win_ref[128:256, :], new_t[3]], axis=1).astype(BF16)
        sw = jnp.dot(q2, kw, preferred_element_type=F32) + bwin_ref[...]
        p_w = _masked_softmax(sw, rep2((wd >= 0) & (wd < WINDOW)))
        o_w = lax.dot_general(p_w.astype(BF16), vw, NT, preferred_element_type=F32)

        for g in range(A_KV):
            gs = slice(g * A_HD, (g + 1) * A_HD)
            for r in range(A_REP):
                h = A_REP * g + r
                rs = slice(g * R + r * tp, g * R + (r + 1) * tp)
                out = (gates[:, 3 * h:3 * h + 1] * o_c[rs, gs] + gates[:, 3 * h + 1:3 * h + 2] * o_s[rs, gs]
                       + gates[:, 3 * h + 2:3 * h + 3] * o_w[rs, gs])
                o_ref[0, :, h * A_HD:(h + 1) * A_HD] = out


def _nsa_sample(qn3, rows6, u3, pool_cmp, pool_sel, win_t, layer, page_table, wk, wv, tb, past, t_real):
    B, tp, _ = qn3.shape
    n_pages = past // PAGE_SIZE
    pg = 64 if n_pages % 64 == 0 else n_pages
    npg = n_pages // pg
    SL = past + LANES
    nblk = past // CMP_STRIDE
    ns = -(-(past + t_real) // SEL_BLOCK)
    ns_pad = tb['cov'].shape[1]
    n_top = min(TOP_N, ns)
    WK = WINDOW + LANES
    kern = functools.partial(_nsa_sample_kernel, pg=pg, past=past, tp=tp, t_real=t_real, ns=ns, n_top=n_top)

    def page_spec(k):
        return pl.BlockSpec((None, None, 256, PAGE_SIZE), lambda b, p, pt: (layer, pt[b, p * pg + k], 0, 0))

    def c_(shape):
        return pl.BlockSpec(shape, lambda b, p, pt: (0,) * len(shape))

    R = A_REP * tp
    grid_spec = pltpu.PrefetchScalarGridSpec(
        num_scalar_prefetch=1,
        grid=(B, npg),
        in_specs=([page_spec(k) for k in range(pg)] + [page_spec(k) for k in range(pg)]
                  + [pl.BlockSpec((1, tp, W_A), lambda b, p, pt: (b, 0, 0)),
                     pl.BlockSpec((6, tp, 128), lambda b, p, pt: (0, b, 0)),
                     pl.BlockSpec((1, tp, LANES), lambda b, p, pt: (b, 0, OFF['small'] // LANES)),
                     pl.BlockSpec((None, None, 256, WINDOW), lambda b, p, pt: (layer, b, 0, 0)),
                     c_((CMP_STRIDE, 128, 256)), c_((CMP_STRIDE, 128, 256)),
                     c_((A_KV * R, nblk)), c_((A_KV * R, SL)), c_((A_KV * R, WK)),
                     c_((nblk, ns_pad)), c_((2 * PAGE_SIZE, 2 * PAGE_SIZE))]),
        out_specs=pl.BlockSpec((1, tp, W_A), lambda b, p, pt: (b, 0, 0)),
        scratch_shapes=[pltpu.VMEM((2, CMP_STRIDE, nblk + 16, 128), BF16),
                        pltpu.VMEM((2, n_pages + 1, 128, PAGE_SIZE), BF16),
                        pltpu.VMEM((nblk, 128), F32), pltpu.VMEM((nblk, 128), F32),
                        pltpu.VMEM((nblk + 16, 128), F32),
                        pltpu.VMEM((tp, ns_pad), F32), pltpu.VMEM((A_KV * R, SL), F32)],
    )
    return pl.pallas_call(
        kern,
        grid_spec=grid_spec,
        out_shape=jax.ShapeDtypeStruct((B, tp, W_A), F32),
        compiler_params=_cparams(("parallel", "arbitrary")),
        name="nsa_sample",
    )(page_table, *([pool_cmp] * pg), *([pool_sel] * pg), qn3, rows6, u3, win_t, wk, wv,
      *(tb[k].reshape(A_KV * R, -1) for k in ('bcmp', 'bsel', 'bwin')), tb['cov'], tb['perm'])


def _tri(L, lower_incl):
    r = lax.broadcasted_iota(jnp.int32, (L, L), 0)
    c = lax.broadcasted_iota(jnp.int32, (L, L), 1)
    return (r >= c) if lower_incl else (r > c)


def _mlstm_kernel(x_ref, small_ref, smallt_ref, brow_ref, bcol_ref, c0_ref, n0_ref, m0_ref,
                  h_ref, co_ref, no_ref, mo_ref, c_s, m_s, *, L, CB, t_real):
    c = pl.program_id(1)
    nc = pl.num_programs(1)

    @pl.when(c == 0)
    def _():
        c_s[:, :, 0:M_HD] = c0_ref[0]
        c_s[:, :, M_HD:2 * M_HD] = n0_ref[0]
        m_s[...] = m0_ref[0]

    sm = small_ref[0].astype(F32)
    low = _tri(L, True)
    lowf = low.astype(F32)
    cbs = range(CB)
    hd = range(M_HEADS)
    ch = [(cb, h) for cb in cbs for h in hd]

    li_col, b_col, b_row, li_row = [], [], [], []
    for cb in cbs:
        rs = slice(cb * L, (cb + 1) * L)
        smt = smallt_ref[0, cb]
        lic = sm[rs, SM_MIF:SM_MIF + 4] + brow_ref[0:1, 0:4]
        lfc = _log_sigmoid(sm[rs, SM_MIF + 4:SM_MIF + 8] + brow_ref[0:1, 4:8])
        lir = smt[0:4, :] + bcol_ref[0:4, :]
        lfr = _log_sigmoid(smt[4:8, :] + bcol_ref[4:8, :])
        if t_real % L:
            t0 = (c * CB + cb) * L
            tcol = lax.broadcasted_iota(jnp.int32, (L, 4), 0) + t0
            trow = lax.broadcasted_iota(jnp.int32, (4, L), 1) + t0
            lic = jnp.where(tcol < t_real, lic, NEG)
            lfc = jnp.where(tcol < t_real, lfc, 0.0)
            lir = jnp.where(trow < t_real, lir, NEG)
            lfr = jnp.where(trow < t_real, lfr, 0.0)
        li_col.append(lic)
        li_row.append(lir)
        b_col.append(jnp.dot(lowf, lfc, precision=HI, preferred_element_type=F32))
        b_row.append(lax.dot_general(lfr, lowf, NT, precision=HI, preferred_element_type=F32))

    rows = lambda cb: slice(cb * L, (cb + 1) * L)
    q = {(cb, h): x_ref[0, rows(cb), h * M_HD:(h + 1) * M_HD].astype(F32) for cb, h in ch}
    k = {(cb, h): x_ref[0, rows(cb), W_M + h * M_HD:W_M + (h + 1) * M_HD].astype(F32) * (M_HD ** -0.5)
         for cb, h in ch}
    v = {(cb, h): x_ref[0, rows(cb), 2 * W_M + h * M_HD:2 * W_M + (h + 1) * M_HD].astype(F32) for cb, h in ch}
    bc = {(cb, h): b_col[cb][:, h:h + 1] for cb, h in ch}
    Dm = {(cb, h): jnp.where(low, bc[cb, h] - b_row[cb][h:h + 1, :] + li_row[cb][h:h + 1, :], NEG) for cb, h in ch}
    dmax = {x: jnp.max(Dm[x], axis=-1, keepdims=True) for x in ch}
    qk = {x: lax.dot_general(q[x], k[x], NT, preferred_element_type=F32) for x in ch}
    bL = {x: bc[x][L - 1:L, :] for x in ch}
    wlog = {(cb, h): bL[cb, h] - bc[cb, h] + li_col[cb][:, h:h + 1] for cb, h in ch}
    wmax = {x: jnp.max(wlog[x], axis=0, keepdims=True) for x in ch}

    m_prev, m_new = {}, {}
    for h in hd:
        m = m_s[h][:, 0:1]
        for cb in cbs:
            m_prev[cb, h] = m
            m = jnp.maximum(bL[cb, h] + m, wmax[cb, h])
            m_new[cb, h] = m
    a = {x: bc[x] + m_prev[x] for x in ch}
    mt = {x: jnp.maximum(a[x], dmax[x]) for x in ch}
    S = {x: qk[x] * jnp.exp(Dm[x] - mt[x]) for x in ch}
    inter = {x: jnp.exp(a[x] - mt[x]) for x in ch}
    v1 = {x: jnp.concatenate([v[x], jnp.ones((L, M_HD), F32)], axis=1) for x in ch}
    Sv = {x: jnp.dot(S[x], v1[x], preferred_element_type=F32) for x in ch}
    dec = {x: jnp.exp(bL[x] + m_prev[x] - m_new[x]) for x in ch}
    kw = {x: k[x] * jnp.exp(wlog[x] - m_new[x]) for x in ch}
    kv = {x: lax.dot_general(kw[x], v1[x], TN, preferred_element_type=F32) for x in ch}

    CN = {h: c_s[h] for h in hd}
    for cb in cbs:
        qC = {h: jnp.dot(q[cb, h], CN[h], preferred_element_type=F32) for h in hd}
        for h in hd:
            x = (cb, h)
            num = inter[x] * qC[h] + Sv[x]
            den = num[:, M_HD:2 * M_HD]
            h_ref[0, rows(cb), h * M_HD:(h + 1) * M_HD] = (num[:, 0:M_HD]
                                                           / jnp.maximum(jnp.abs(den), jnp.exp(-mt[x])))
            CN[h] = dec[x] * CN[h] + kv[x]
    for h in hd:
        c_s[h] = CN[h]
        m_s[h] = jnp.broadcast_to(m_new[CB - 1, h], (1, LANES))

    @pl.when(c == nc - 1)
    def _():
        co_ref[0] = c_s[:, :, 0:M_HD]
        no_ref[0] = c_s[:, :, M_HD:2 * M_HD]
        mo_ref[0] = m_s[...]


def _mlstm(u3, smallt, m_bi, m_bf, C0, n0, m0, L, t_real):
    B, T, _ = u3.shape
    nc = T // L
    brow = jnp.concatenate([m_bi, m_bf]).reshape(1, 8)
    bcol = jnp.concatenate([m_bi, m_bf]).reshape(8, 1)
    n0 = jnp.broadcast_to(n0[:, :, :, None], (B, M_HEADS, M_HD, M_HD))
    m0 = jnp.broadcast_to(m0[:, :, None, None], (B, M_HEADS, 1, LANES))
    CB = 8 if nc % 8 == 0 else 1
    kern = functools.partial(_mlstm_kernel, L=L, CB=CB, t_real=t_real)
    st = lambda shape: pl.BlockSpec(shape, lambda b, c: (b,) + (0,) * (len(shape) - 1))
    h, Cn, nn, mn = pl.pallas_call(
        kern,
        grid=(B, nc // CB),
        in_specs=[pl.BlockSpec((1, CB * L, 3 * W_M), lambda b, c: (b, c, OFF['m_qkv'] // (3 * W_M))),
                  pl.BlockSpec((1, CB * L, LANES), lambda b, c: (b, c, OFF['small'] // LANES)),
                  pl.BlockSpec((1, CB, 16, L), lambda b, c: (b, c, 0, 0)),
                  pl.BlockSpec((1, 8), lambda b, c: (0, 0)), pl.BlockSpec((8, 1), lambda b, c: (0, 0)),
                  st((1, M_HEADS, M_HD, M_HD)), st((1, M_HEADS, M_HD, M_HD)), st((1, M_HEADS, 1, LANES))],
        out_specs=[pl.BlockSpec((1, CB * L, W_M), lambda b, c: (b, c, 0)),
                   st((1, M_HEADS, M_HD, M_HD)), st((1, M_HEADS, M_HD, M_HD)), st((1, M_HEADS, 1, LANES))],
        out_shape=[jax.ShapeDtypeStruct((B, T, W_M), F32),
                   jax.ShapeDtypeStruct((B, M_HEADS, M_HD, M_HD), F32),
                   jax.ShapeDtypeStruct((B, M_HEADS, M_HD, M_HD), F32),
                   jax.ShapeDtypeStruct((B, M_HEADS, 1, LANES), F32)],
        scratch_shapes=[pltpu.VMEM((M_HEADS, M_HD, 2 * M_HD), F32), pltpu.VMEM((M_HEADS, 1, LANES), F32)],
        compiler_params=_cparams(("parallel", "arbitrary")),
        name="mlstm",
    )(u3, u3, smallt, brow, bcol, C0, n0, m0)
    return h, Cn, nn[:, :, :, 0], mn[:, :, 0, 0]


def _gdn_a_kernel(x_ref, small_ref, smallt_ref, cw_ref, prow_ref, pcol_ref, buf_ref,
                  u_ref, w_ref, qe_ref, kd_ref, at_ref, eg_ref, xbuf, *, L, CB, t_real):
    c = pl.program_id(1)
    PRE = 8
    R = CB * L

    @pl.when(c == 0)
    def _():
        xbuf[0:PRE, :] = buf_ref[0]

    xbuf[PRE:PRE + R, :] = x_ref[0].astype(F32)
    conv = jnp.zeros((R, 3 * W_G), F32)
    for j in range(CONV_W):
        conv = conv + xbuf[pl.ds(PRE - (CONV_W - 1) + j, R), :] * cw_ref[j:j + 1, :]
    xbuf[0:PRE, :] = xbuf[R:R + PRE, :]
    conv = _silu(conv)

    sm = small_ref[0].astype(F32)
    g_all = -jnp.exp(prow_ref[0:1, 0:4]) * _softplus(sm[:, SM_GAB:SM_GAB + 4] + prow_ref[1:2, 0:4])
    beta_all = jax.nn.sigmoid(sm[:, SM_GAB + 4:SM_GAB + 8])
    lowf = _tri(L, True).astype(F32)

    H = G_HEADS
    lane_h = lax.broadcasted_iota(jnp.int32, (L, H * L), 1) // L
    row_p = lax.broadcasted_iota(jnp.int32, (L, H * L), 0)
    col_p = lax.broadcasted_iota(jnp.int32, (L, H * L), 1) % L
    blockmask = (lax.broadcasted_iota(jnp.int32, (H * L, H * L), 0) // L
                 == lax.broadcasted_iota(jnp.int32, (H * L, H * L), 1) // L)

    def pack_diag(full):
        out = full[0:L]
        for h in range(1, H):
            out = jnp.where(lane_h == h, full[h * L:(h + 1) * L], out)
        return out

    def block_diag(m, parts):
        if L % 16:
            return _split(jnp.where(blockmask, jnp.concatenate([m] * H, axis=0), 0.0))
        return tuple(jnp.where(blockmask, jnp.concatenate([p] * H, axis=0), jnp.zeros((), BF16))
                     for p in parts)

    a_list, rhs_list = [], []
    for cb in range(CB):
        rs = slice(cb * L, (cb + 1) * L)
        smt = smallt_ref[0, cb]
        g_col = g_all[rs]
        beta_col = beta_all[rs]
        g_row = -jnp.exp(pcol_ref[0:4, 0:1]) * _softplus(smt[8:12, :] + pcol_ref[0:4, 1:2])
        if t_real % L:
            t0 = (c * CB + cb) * L
            tcol = lax.broadcasted_iota(jnp.int32, (L, 4), 0) + t0
            trow = lax.broadcasted_iota(jnp.int32, (4, L), 1) + t0
            g_col = jnp.where(tcol < t_real, g_col, 0.0)
            beta_col = jnp.where(tcol < t_real, beta_col, 0.0)
            g_row = jnp.where(trow < t_real, g_row, 0.0)
        G_col = jnp.dot(lowf, g_col, precision=HI, preferred_element_type=F32)
        G_row = lax.dot_general(g_row, lowf, NT, precision=HI, preferred_element_type=F32)

        qs, ks, kbs, rhss = [], [], [], []
        for h in range(G_HEADS):
            hs = slice(h * G_HD, (h + 1) * G_HD)
            cq = conv[rs, h * G_HD:(h + 1) * G_HD]
            ck = conv[rs, W_G + h * G_HD:W_G + (h + 1) * G_HD]
            v = conv[rs, 2 * W_G + h * G_HD:2 * W_G + (h + 1) * G_HD]
            q = cq * lax.rsqrt(jnp.sum(cq * cq, axis=-1, keepdims=True) + EPS) * (G_HD ** -0.5)
            k = ck * lax.rsqrt(jnp.sum(ck * ck, axis=-1, keepdims=True) + EPS)
            Gc = G_col[:, h:h + 1]
            bcol = beta_col[:, h:h + 1]
            kb = k * bcol
            eG = jnp.exp(Gc)
            GL = Gc[L - 1:L, :]
            qs.append(q)
            ks.append(k)
            kbs.append(kb)
            rhss.append(jnp.concatenate([v * bcol, kb * eG], axis=1))
            qe_ref[0, rs, hs] = q * eG
            kd_ref[0, rs, hs] = k * jnp.exp(GL - Gc)
            eg_ref[0, cb, h:h + 1, :] = jnp.broadcast_to(jnp.exp(GL), (1, LANES))
        eg_ref[0, cb, G_HEADS:8, :] = jnp.zeros((8 - G_HEADS, LANES), F32)

        Gc_p = jnp.concatenate([jnp.broadcast_to(G_col[:, h:h + 1], (L, L)) for h in range(H)], axis=1)
        Gr_p = jnp.concatenate([jnp.broadcast_to(G_row[h:h + 1, :], (L, L)) for h in range(H)], axis=1)
        low_p = row_p >= col_p
        dmask = jnp.where(low_p, jnp.exp(jnp.where(low_p, Gc_p - Gr_p, 0.0)), 0.0)
        k_st = _split(jnp.concatenate(ks, axis=0))
        A = pack_diag(_dot3(_split(jnp.concatenate(kbs, axis=0)), k_st, NT)) * jnp.where(row_p > col_p, dmask, 0.0)
        attn = pack_diag(lax.dot_general(jnp.concatenate(qs, axis=0).astype(BF16), k_st[0], NT,
                                         preferred_element_type=F32)) * dmask
        at_ref[0, rs, :] = attn
        a_list.append(A)
        rhs_list.append(_split(jnp.concatenate(rhss, axis=0)))

    chunks = range(CB)
    X = [jnp.where(row_p == col_p, 1.0, 0.0) - a_list[cb] for cb in chunks]
    As = [_split(a_list[cb]) for cb in chunks]
    Pw = [_dot3(As[cb], block_diag(a_list[cb], As[cb])) for cb in chunks]
    span = 2
    while span < L:
        Ps = [_split(Pw[cb]) for cb in chunks]
        Pbd = [block_diag(Pw[cb], Ps[cb]) for cb in chunks]
        X = [X[cb] + _dot3(_split(X[cb]), Pbd[cb]) for cb in chunks]
        span *= 2
        if span < L:
            Pw = [_dot3(Ps[cb], Pbd[cb]) for cb in chunks]
    sol = [_dot3(block_diag(X[cb], _split(X[cb])), rhs_list[cb]) for cb in chunks]
    for cb in chunks:
        for h in range(H):
            rs = slice(cb * L, (cb + 1) * L)
            hs = slice(h * G_HD, (h + 1) * G_HD)
            u_ref[0, rs, hs] = sol[cb][h * L:(h + 1) * L, 0:G_HD]
            w_ref[0, rs, hs] = sol[cb][h * L:(h + 1) * L, G_HD:2 * G_HD]


def _gdn_b_kernel(u_ref, w_ref, qe_ref, kd_ref, at_ref, eg_ref, s0_ref, o_ref, so_ref, s_s, *, L, BB):
    c = pl.program_id(1)
    nc = pl.num_programs(1)

    @pl.when(c == 0)
    def _():
        s_s[...] = s0_ref[...]

    ch = [(b, h, slice(h * G_HD, (h + 1) * G_HD)) for b in range(BB) for h in range(G_HEADS)]
    S = [s_s[b, h] for b, h, _ in ch]
    r = [jnp.dot(jnp.concatenate([w_ref[b, :, hs], qe_ref[b, :, hs]], axis=0), S[i], preferred_element_type=F32)
         for i, (b, h, hs) in enumerate(ch)]
    v_new = [u_ref[b, :, hs] - r[i][0:L] for i, (b, h, hs) in enumerate(ch)]
    av = [jnp.dot(at_ref[b, :, h * L:(h + 1) * L], v_new[i], preferred_element_type=F32)
          for i, (b, h, hs) in enumerate(ch)]
    kdv = [lax.dot_general(kd_ref[b, :, hs], v_new[i], TN, preferred_element_type=F32)
           for i, (b, h, hs) in enumerate(ch)]
    for i, (b, h, hs) in enumerate(ch):
        o_ref[b, :, hs] = r[i][L:2 * L] + av[i]
        s_s[b, h] = eg_ref[b, 0, h:h + 1, :] * S[i] + kdv[i]

    @pl.when(c == nc - 1)
    def _():
        so_ref[...] = s_s[...]


def _gdn(u3, smallt, g_conv, g_A_log, g_dt_bias, buf, S0, L, t_real):
    B, T, _ = u3.shape
    nc = T // L
    CB = 8 if nc % 8 == 0 else 1
    BB = 8 if B % 8 == 0 else B
    prow = jnp.stack([g_A_log, g_dt_bias])
    pcol = jnp.stack([g_A_log, g_dt_bias], axis=1)
    buf8 = jnp.concatenate([jnp.zeros((B, 8 - (CONV_W - 1), 3 * W_G), F32), buf], axis=1)
    st = lambda shape: pl.BlockSpec(shape, lambda b, c: (b,) + (0,) * (len(shape) - 1))
    cst = lambda shape: pl.BlockSpec(shape, lambda b, c: (0,) * len(shape))
    row = lambda nb, r, w: pl.BlockSpec((nb, r, w), lambda b, c: (b, c, 0))
    tok = lambda w: jax.ShapeDtypeStruct((B, T, w), F32)
    u, w, qe, kd, at, eg = pl.pallas_call(
        functools.partial(_gdn_a_kernel, L=L, CB=CB, t_real=t_real),
        grid=(B, nc // CB),
        in_specs=[pl.BlockSpec((1, CB * L, 3 * W_G), lambda b, c: (b, c, OFF['g_qkv'] // (3 * W_G))),
                  pl.BlockSpec((1, CB * L, LANES), lambda b, c: (b, c, OFF['small'] // LANES)),
                  pl.BlockSpec((1, CB, 16, L), lambda b, c: (b, c, 0, 0)),
                  cst((CONV_W, 3 * W_G)), cst((2, 4)), cst((4, 2)), st((1, 8, 3 * W_G))],
        out_specs=[row(1, CB * L, W_G)] * 4 + [row(1, CB * L, G_HEADS * L),
                                               pl.BlockSpec((1, CB, 8, LANES), lambda b, c: (b, c, 0, 0))],
        out_shape=[tok(W_G)] * 4 + [tok(G_HEADS * L), jax.ShapeDtypeStruct((B, nc, 8, LANES), F32)],
        scratch_shapes=[pltpu.VMEM((CB * L + 8, 3 * W_G), F32)],
        compiler_params=_cparams(("parallel", "arbitrary")),
        name="gdn_a",
    )(u3, u3, smallt, g_conv, prow, pcol, buf8)
    sblk = pl.BlockSpec((BB, G_HEADS, G_HD, G_HD), lambda b, c: (b, 0, 0, 0))
    return pl.pallas_call(
        functools.partial(_gdn_b_kernel, L=L, BB=BB),
        grid=(B // BB, nc),
        in_specs=[row(BB, L, W_G)] * 4 + [row(BB, L, G_HEADS * L),
                                          pl.BlockSpec((BB, 1, 8, LANES), lambda b, c: (b, c, 0, 0)), sblk],
        out_specs=[row(BB, L, W_G), sblk],
        out_shape=[tok(W_G), jax.ShapeDtypeStruct((B, G_HEADS, G_HD, G_HD), F32)],
        scratch_shapes=[pltpu.VMEM((BB, G_HEADS, G_HD, G_HD), F32)],
        compiler_params=_cparams(("parallel", "arbitrary")),
        name="gdn_b",
    )(u, w, qe, kd, at, eg, S0)


def _head_rmsnorm(x, gain_ref, nheads, hd):
    outs = []
    for h in range(nheads):
        xh = x[:, h * hd:(h + 1) * hd]
        ms = jnp.mean(xh * xh, axis=-1, keepdims=True)
        outs.append(xh * lax.rsqrt(ms + EPS) * gain_ref[...])
    return jnp.concatenate(outs, axis=1)


def _merge_kernel(x_ref, oa_ref, hm_ref, og_ref, az_ref, mo_ref, mz_ref, gz_ref, mg_ref,
                  wb_ref, wo_ref, mhn_ref, ghn_ref, y_ref):
    oa = oa_ref[...] * _silu(az_ref[...].astype(F32))
    om = (_head_rmsnorm(hm_ref[...], mhn_ref, M_HEADS, M_HD) * jax.nn.sigmoid(mo_ref[...].astype(F32))
          * _silu(mz_ref[...].astype(F32)))
    og = _head_rmsnorm(og_ref[...], ghn_ref, G_HEADS, G_HD) * _silu(gz_ref[...].astype(F32))
    y = jnp.zeros(y_ref.shape, F32)
    for i, br in enumerate((oa, om, og)):
        proj = jnp.dot(br.astype(BF16), wb_ref[i], preferred_element_type=F32)
        y = y + jax.nn.sigmoid(mg_ref[:, i * D_MODEL:(i + 1) * D_MODEL].astype(F32)) * proj
    y_ref[...] = x_ref[...] + jnp.dot(y.astype(BF16), wo_ref[...], preferred_element_type=F32)


def _merge_out(x2, o_a, h_m, o_g, u2, w_branch, w_out, m_hn, g_hn):
    n = x2.shape[0]
    tm = min(512, n)
    row = lambda w, off: pl.BlockSpec((tm, w), lambda i: (i, off // w))
    cst = lambda shape: pl.BlockSpec(shape, lambda i: (0,) * len(shape))
    return pl.pallas_call(
        _merge_kernel,
        grid=(n // tm,),
        in_specs=[row(D_MODEL, 0), row(W_A, 0), row(W_M, 0), row(W_G, 0),
                  row(W_A, OFF['a_z']), row(W_M, OFF['m_o']), row(W_M, OFF['m_z']), row(W_G, OFF['g_z']),
                  row(N_BRANCH * D_MODEL, OFF['merge']),
                  cst((N_BRANCH, W_A, D_MODEL)), cst((D_MODEL, D_MODEL)), cst((1, M_HD)), cst((1, G_HD))],
        out_specs=row(D_MODEL, 0),
        out_shape=jax.ShapeDtypeStruct((n, D_MODEL), F32),
        compiler_params=_cparams(("parallel",)),
        name="merge_out",
    )(x2, o_a, h_m, o_g, u2, u2, u2, u2, u2, w_branch.astype(BF16), w_out.astype(BF16),
      m_hn.reshape(1, M_HD), g_hn.reshape(1, G_HD))


def _small_t(u3, L):
    B, T, _ = u3.shape
    s = u3[:, :, OFF['small'] + SM_MIF:OFF['small'] + SM_MIF + 16].astype(F32)
    return s.reshape(B, T // L, L, 16).transpose(0, 1, 3, 2)


def _layer(x, lw, tb, past, q_off):
    B, T, _ = x.shape
    if past is None:
        tp, L = T, math.gcd(T, M_CHUNK)
    else:
        tp = -(-T // 8) * 8
        L = tp
        x = jnp.pad(x, ((0, 0), (0, tp - T), (0, 0)))
    x2 = x.reshape(B * tp, D_MODEL)
    u2 = _proj_in(x2, lw['norm_g'], lw['w_perm'], BF16 if past is None else F32)
    u3 = u2.reshape(B, tp, N_PERM)
    if past is None:
        C0 = jnp.zeros((B, M_HEADS, M_HD, M_HD), F32)
        n0 = jnp.zeros((B, M_HEADS, M_HD), F32)
        m0 = jnp.zeros((B, M_HEADS), F32)
        S0 = jnp.zeros((B, G_HEADS, G_HD, G_HD), F32)
        buf = jnp.zeros((B, CONV_W - 1, 3 * W_G), F32)
    else:
        C0, n0, m0, S0, buf = past['mC'], past['mn'], past['mm'], past['gS'], past['gconv']
    qn, rows6, *rows_t = _prep_a(u2, lw['a_qn'], lw['a_kn'], T if past is None and T % 512 == 0 else None)
    qn3 = qn.reshape(B, tp, W_A)
    if past is None:
        o_a = _nsa_prompt(qn3, rows6, u3, lw['wk'], lw['wv'], tb)
    else:
        o_a = _nsa_sample(qn3, rows6, u3, past['cmp'], past['sel'], past['win_t'], past['layer'], past['page_table'],
                          lw['wk'], lw['wv'], tb, q_off, T)
    smallt = _small_t(u3, L)
    h_m, Cn, nn, mn = _mlstm(u3, smallt, lw['m_bi'], lw['m_bf'], C0, n0, m0, L, T)
    o_g, Sn = _gdn(u3, smallt, lw['g_conv'], lw['g_A_log'], lw['g_dt_bias'], buf, S0, L, T)
    y2 = _merge_out(x2, o_a.reshape(B * tp, W_A), h_m.reshape(B * tp, W_M), o_g.reshape(B * tp, W_G), u2,
                    lw['w_branch'], lw['w_out'], lw['m_hn'], lw['g_hn'])
    if rows_t:
        new_cmp, new_sel, new_win = (rows_t[b].reshape(B, 2, A_KV, A_HD, T).transpose(0, 4, 1, 2, 3)
                                     for b in range(3))
    else:
        rows = rows6.reshape(3, 2, B, tp, A_KV, A_HD)[:, :, :, :T]
        new_cmp, new_sel, new_win = (jnp.moveaxis(rows[b], 0, 2) for b in range(3))
    g_qkv = u3[:, max(T - (CONV_W - 1), 0):T, OFF['g_qkv']:OFF['g_qkv'] + 3 * W_G].astype(F32)
    if past is None:
        win = new_win
    else:
        win = jnp.concatenate([past['win'], new_win], axis=1)
    full = jnp.concatenate([buf, g_qkv], axis=1)
    state = dict(cmp=new_cmp, sel=new_sel, win=win[:, -min(WINDOW, win.shape[1]):],
                 mC=Cn, mn=nn, mm=mn, gS=Sn, gconv=full[:, -(CONV_W - 1):])
    return y2.reshape(B, tp, D_MODEL)[:, :T], state


def kernel(x_prompt, x_sample, cache_cmp_kv, cache_sel_kv, cache_win_kv, state_mlstm_C, state_mlstm_n,
           state_mlstm_m, state_gdn_S, state_gdn_conv, page_table, norm_g, w_in, a_qn, a_kn, a_cmp_wk,
           a_cmp_wv, rel_bias, m_bi, m_bf, m_hn, g_conv, g_A_log, g_dt_bias, g_hn, w_branch, w_out):
    names = ('cmp', 'sel', 'win', 'mC', 'mn', 'mm', 'gS', 'gconv')
    st_p = {k: [] for k in names}
    st_s = {k: [] for k in names}
    past_len = page_table.shape[1] * PAGE_SIZE
    n_pool = cache_cmp_kv.shape[1]
    pos_minor = lambda c: jnp.swapaxes(c.reshape(c.shape[0], c.shape[1], c.shape[2], 2 * A_KV * A_HD), 2, 3)
    pool_cmp, pool_sel, win_t = pos_minor(cache_cmp_kv), pos_minor(cache_sel_kv), pos_minor(cache_win_kv)
    w_t = jnp.swapaxes(w_in, 1, 2)
    db, dt = x_sample.shape[0], x_sample.shape[1]
    tb_p = _prompt_tables(rel_bias, x_prompt.shape[1])
    tb_s = _sample_tables(rel_bias, past_len, -(-dt // 8) * 8, dt)
    y_p, y_s = x_prompt, x_sample
    for l in range(DEPTH):
        lw = dict(norm_g=norm_g[l], w_perm=_permute_w_in(w_t, l), a_qn=a_qn[l], a_kn=a_kn[l],
                  wk=_cmp_weights(a_cmp_wk[l]), wv=_cmp_weights(a_cmp_wv[l]),
                  m_bi=m_bi[l], m_bf=m_bf[l], m_hn=m_hn[l], g_conv=g_conv[l], g_A_log=g_A_log[l],
                  g_dt_bias=g_dt_bias[l], g_hn=g_hn[l], w_branch=w_branch[l], w_out=w_out[l])
        y_p, new_p = _layer(y_p, lw, tb_p, None, 0)
        past = dict(cmp=pool_cmp, sel=pool_sel, win_t=win_t, layer=l, win=cache_win_kv[l], page_table=page_table,
                    mC=state_mlstm_C[l], mn=state_mlstm_n[l], mm=state_mlstm_m[l],
                    gS=state_gdn_S[l], gconv=state_gdn_conv[l])
        y_s, new_s = _layer(y_s, lw, tb_s, past, past_len)
        for k in names:
            st_p[k].append(new_p[k])
            st_s[k].append(new_s[k])
    P = {k: jnp.stack(v) for k, v in st_p.items()}
    S = {k: jnp.stack(v) for k, v in st_s.items()}
    return (y_p, y_s, P['cmp'], S['cmp'], P['sel'], S['sel'], P['win'], S['win'],
            P['mC'], S['mC'], P['mn'], S['mn'], P['mm'], S['mm'], P['gS'], S['gS'], P['gconv'], S['gconv'])
```

```python
import functools
import math

import jax
import jax.numpy as jnp
from jax import lax
from jax.experimental import pallas as pl
from jax.experimental.pallas import tpu as pltpu

D_MODEL = 1024
DEPTH = 2
PAGE_SIZE = 128
A_HEADS = 8
A_KV = 2
A_REP = A_HEADS // A_KV
A_HD = 64
CMP_BLOCK = 32
CMP_STRIDE = 16
SEL_BLOCK = 64
TOP_N = 16
WINDOW = 512
Q_BLOCK = 128
N_BUCKETS = 32
MAX_DIST = 2048
M_HEADS = 4
M_HD = 128
M_CHUNK = 64
G_HEADS = 4
G_HD = 128
G_CHUNK = 64
CONV_W = 4
W_A = A_HEADS * A_HD
W_M = M_HEADS * M_HD
W_G = G_HEADS * G_HD
N_BRANCH = 3
EPS = 1e-6
NEG = -1e30
TINY = 1e-30
LOG2E = math.log2(math.e)

F32 = jnp.float32
BF16 = jnp.bfloat16
HI = lax.Precision.HIGHEST
NN = (((1,), (0,)), ((), ()))
NT = (((1,), (1,)), ((), ()))
TN = (((0,), (0,)), ((), ()))

LANES = 128
VMEM_LIMIT = 56 * 1024 * 1024

IN_ORDER = ('a_q', 'a_kv', 'a_gate', 'a_z', 'm_qkv', 'm_if', 'm_o', 'm_z', 'g_qkv', 'g_ab', 'g_z', 'merge')
IN_WIDTH = dict(a_q=W_A, a_kv=3 * 2 * A_KV * A_HD, a_gate=A_HEADS * 3, a_z=W_A, m_qkv=3 * W_M, m_if=2 * M_HEADS,
                m_o=W_M, m_z=W_M, g_qkv=3 * W_G, g_ab=2 * G_HEADS, g_z=W_G, merge=N_BRANCH * D_MODEL)
OFF = dict(merge=0, m_qkv=3072, g_qkv=4608, a_q=6144, a_z=6656, m_o=7168, m_z=7680, g_z=8192, a_kv=8704,
           small=9472)
N_PERM = 9600
SM_GATE, SM_MIF, SM_GAB = 0, 24, 32


def _cparams(sem):
    return pltpu.CompilerParams(dimension_semantics=sem, vmem_limit_bytes=VMEM_LIMIT)


def _silu(x):
    return x * jax.nn.sigmoid(x)


def _log_sigmoid(x):
    return jnp.minimum(x, 0.0) - jnp.log(1.0 + jnp.exp(-jnp.abs(x)))


def _softplus(x):
    return jnp.maximum(x, 0.0) + jnp.log(1.0 + jnp.exp(-jnp.abs(x)))


def _split(a):
    hi = a.astype(BF16)
    return hi, (a - hi.astype(F32)).astype(BF16)


def _dot3(a, b, dims=NN):
    mm = lambda x, y: lax.dot_general(x, y, dims, preferred_element_type=F32)
    return mm(a[0], b[0]) + mm(a[0], b[1]) + mm(a[1], b[0])


def _src_offsets():
    offs, off = {}, 0
    for name in IN_ORDER:
        offs[name] = off
        off += IN_WIDTH[name]
    return offs, off


def _permute_kernel(w_ref, o_ref):
    src, _ = _src_offsets()
    small = []
    for name in IN_ORDER:
        w = IN_WIDTH[name]
        rows = w_ref[src[name]:src[name] + w, :]
        if w % LANES:
            small.append(rows)
        else:
            o_ref[:, OFF[name]:OFF[name] + w] = rows.T.astype(BF16)
    used = sum(r.shape[0] for r in small)
    small.append(jnp.zeros((LANES - used, w_ref.shape[1]), F32))
    o_ref[:, OFF['small']:N_PERM] = jnp.concatenate(small, axis=0).T.astype(BF16)


def _permute_w_in(w_t, layer):
    _, n_in, d = w_t.shape
    tr = 128
    return pl.pallas_call(
        _permute_kernel,
        grid=(d // tr,),
        in_specs=[pl.BlockSpec((None, n_in, tr), lambda i: (layer, 0, i))],
        out_specs=pl.BlockSpec((tr, N_PERM), lambda i: (i, 0)),
        out_shape=jax.ShapeDtypeStruct((d, N_PERM), BF16),
        compiler_params=_cparams(("parallel",)),
        name="permute_w",
    )(w_t)


def _rel_bucket(dist):
    n = jnp.maximum(dist, 0)
    exact = N_BUCKETS // 2
    nf = jnp.maximum(n, exact).astype(F32)
    large = exact + (jnp.log(nf / exact) / math.log(MAX_DIST / exact) * (N_BUCKETS - exact)).astype(jnp.int32)
    return jnp.where(n < exact, n, jnp.minimum(large, N_BUCKETS - 1))


def _bias_kernel(thr_ref, tab_ref, d_ref, o_ref):
    n = jnp.maximum(d_ref[0], 0)
    for h in range(A_HEADS):
        acc = jnp.full(n.shape, tab_ref[h], F32)
        for k in range(1, N_BUCKETS):
            acc = jnp.where(n >= thr_ref[k], tab_ref[k * A_HEADS + h], acc)
        o_ref[0, h // A_REP, h % A_REP] = acc


def _bias_rows(rel_bias, dist, split_rows=False):
    N, Q, K = dist.shape
    nmax = 2 * MAX_DIST
    thr = jnp.sum(_rel_bucket(jnp.arange(nmax))[None, :] < jnp.arange(N_BUCKETS)[:, None], axis=1).astype(jnp.int32)
    smem = pl.BlockSpec(memory_space=pltpu.SMEM)
    out = pl.pallas_call(
        _bias_kernel,
        grid=(N,),
        in_specs=[smem, smem, pl.BlockSpec((1, Q, K), lambda i: (i, 0, 0))],
        out_specs=pl.BlockSpec((1, A_KV, A_REP, Q, K), lambda i: (i, 0, 0, 0, 0)),
        out_shape=jax.ShapeDtypeStruct((N, A_KV, A_REP, Q, K), F32),
        compiler_params=_cparams(("parallel",)),
        name="bias_rows",
    )(thr, rel_bias.astype(F32).reshape(N_BUCKETS * A_HEADS), dist.astype(jnp.int32))
    return out if split_rows else out.reshape(N, A_KV, A_REP * Q, K)


def _cmp_weights(w):
    wr = w.reshape(A_KV, 2, CMP_STRIDE, A_HD, A_HD)
    eye = jnp.eye(A_KV, dtype=w.dtype)
    full = jnp.einsum('gmjde,gh->jgdmhe', wr, eye)
    return full.reshape(CMP_STRIDE, A_KV * A_HD, 2 * A_KV * A_HD).astype(BF16)


def _cover_t(ns_pad, nch):
    s0 = jnp.arange(ns_pad)[:, None] * SEL_BLOCK
    c0 = jnp.arange(nch)[None, :] * CMP_STRIDE
    return ((c0 < s0 + SEL_BLOCK) & (s0 <= c0 + CMP_BLOCK - 1)).astype(F32)


def _gate_expand():
    col = jnp.arange(N_BRANCH * W_A)
    src = 3 * ((col % W_A) // A_HD) + col // W_A
    return (jnp.arange(LANES)[:, None] == src[None, :]).astype(BF16)


def _phase_perm(pages=2):
    n = pages * PAGE_SIZE
    r = jnp.arange(n)
    blocks = n // CMP_STRIDE
    return (jnp.arange(n)[None, :] == ((r % blocks) * CMP_STRIDE + r // blocks)[:, None]).astype(BF16)


def _prompt_tables(rel_bias, T):
    QB = Q_BLOCK
    nqb = T // QB
    nch = T // CMP_STRIDE
    ns = T // SEL_BLOCK
    ns_pad = -(-ns // 8) * 8
    i_ = jnp.arange(QB)
    t = (jnp.arange(nqb) * QB)[:, None, None] + i_[None, :, None]
    cend = (jnp.arange(nch) * CMP_STRIDE + CMP_BLOCK - 1)[None, None, :]
    d = (jnp.arange(nqb) * QB)[:, None, None] + i_[None, :, None] - i_[None, None, :]
    nw = min(WINDOW // QB + 1, nqb)
    dt = jnp.swapaxes(d, 1, 2)
    tsel = _bias_rows(rel_bias * LOG2E, dt, split_rows=True)
    return dict(
        bcmp=_bias_rows(rel_bias, t - cend),
        tselt=tsel.transpose(0, 1, 3, 2, 4).reshape(nqb, A_KV, QB, A_REP * QB),
        covt=_cover_t(ns_pad, nch),
        twin=(tsel[:nw] + jnp.where((dt[:nw] >= 0) & (dt[:nw] < WINDOW), 0.0, NEG)[:, None, None]
              ).transpose(0, 1, 3, 2, 4).reshape(nw, A_KV, QB, A_REP * QB),
        cmt=jnp.where(dt[:2] >= 0, 0.0, NEG),
        gx=_gate_expand())


def _sample_tables(rel_bias, past, tp, t_real):
    SL = past + LANES
    nblk = past // CMP_STRIDE
    ns = -(-(past + t_real) // SEL_BLOCK)
    ns_pad = -(-ns // LANES) * LANES
    WK = WINDOW + LANES
    i_ = jnp.arange(tp)
    t = past + i_
    rows = lambda dist: _bias_rows(rel_bias, dist[None])[0]
    return dict(
        bcmp=rows(t[:, None] - (jnp.arange(nblk) * CMP_STRIDE + CMP_BLOCK - 1)[None, :]),
        bsel=rows(t[:, None] - jnp.arange(SL)[None, :]),
        bwin=rows(i_[:, None] + WINDOW - jnp.arange(WK)[None, :]),
        cov=_cover_t(ns_pad, nblk).T, perm=_phase_perm())


def _proj_in_kernel(x_ref, g_ref, w_ref, o_ref, hn_ref):
    @pl.when(pl.program_id(1) == 0)
    def _():
        x = x_ref[...]
        ms = jnp.mean(x * x, axis=-1, keepdims=True)
        hn_ref[...] = (x * lax.rsqrt(ms + EPS) * g_ref[...]).astype(BF16)

    o_ref[...] = jnp.dot(hn_ref[...], w_ref[...], preferred_element_type=F32).astype(o_ref.dtype)


def _proj_in(x2, norm_g, w_perm, out_dtype):
    n = x2.shape[0]
    tm = min(2048, n)
    tn = 1920
    return pl.pallas_call(
        _proj_in_kernel,
        grid=(n // tm, N_PERM // tn),
        in_specs=[pl.BlockSpec((tm, D_MODEL), lambda i, j: (i, 0)),
                  pl.BlockSpec((1, D_MODEL), lambda i, j: (0, 0)),
                  pl.BlockSpec((D_MODEL, tn), lambda i, j: (0, j))],
        out_specs=pl.BlockSpec((tm, tn), lambda i, j: (i, j)),
        out_shape=jax.ShapeDtypeStruct((n, N_PERM), out_dtype),
        scratch_shapes=[pltpu.VMEM((tm, D_MODEL), BF16)],
        compiler_params=_cparams(("parallel", "arbitrary")),
        name="proj_in",
    )(x2, norm_g.reshape(1, D_MODEL), w_perm)


def _prep_a_kernel(q_ref, kv0_ref, kv1_ref, kv2_ref, bdq_ref, bdk_ref, qg_ref, kg_ref, qo_ref, ro_ref, *rt_ref):
    def group_mean(x2, bd_ref):
        hi, lo = _split(x2)
        return (jnp.dot(hi, bd_ref[...], preferred_element_type=F32)
                + jnp.dot(lo, bd_ref[...], preferred_element_type=F32)) * (1.0 / A_HD)

    q = q_ref[...].astype(F32)
    ms = group_mean(q * q, bdq_ref)
    qo_ref[...] = q * lax.rsqrt(ms + EPS) * qg_ref[...] * (A_HD ** -0.5)
    for b, kv_ref in enumerate((kv0_ref, kv1_ref, kv2_ref)):
        k = kv_ref[:, 0:128].astype(F32)
        ms = group_mean(k * k, bdk_ref)
        kn = k * lax.rsqrt(ms + EPS) * kg_ref[b:b + 1, :]
        v = kv_ref[:, 128:256].astype(F32)
        ro_ref[2 * b] = kn
        ro_ref[2 * b + 1] = v
        if rt_ref:
            rt_ref[b][0:128, :] = kn.T
            rt_ref[b][128:256, :] = v.T


def _prep_a(u2, a_qn, a_kn, seq=None):
    n = u2.shape[0]
    tm = min(512, n)
    out_specs = [pl.BlockSpec((tm, W_A), lambda i: (i, 0)), pl.BlockSpec((6, tm, 128), lambda i: (0, i, 0))]
    out_shape = [jax.ShapeDtypeStruct((n, W_A), F32), jax.ShapeDtypeStruct((6, n, 128), F32)]
    if seq is not None:
        per = seq // tm
        out_specs += [pl.BlockSpec((None, 256, tm), lambda i: (i // per, 0, i % per))] * 3
        out_shape += [jax.ShapeDtypeStruct((n // seq, 256, seq), F32)] * 3
    bd = lambda w: (jnp.arange(w)[:, None] // A_HD == jnp.arange(w)[None, :] // A_HD).astype(BF16)
    qg = jnp.tile(a_qn, A_HEADS).reshape(1, W_A)
    kg = jnp.tile(a_kn, (1, A_KV))
    const = lambda shape: pl.BlockSpec(shape, lambda i: (0, 0))
    kvs = lambda b: pl.BlockSpec((tm, 256), lambda i: (i, OFF['a_kv'] // 256 + b))
    return pl.pallas_call(
        _prep_a_kernel,
        grid=(n // tm,),
        in_specs=[pl.BlockSpec((tm, W_A), lambda i: (i, OFF['a_q'] // W_A)), kvs(0), kvs(1), kvs(2),
                  const((W_A, W_A)), const((128, 128)), const((1, W_A)), const((3, 128))],
        out_specs=out_specs,
        out_shape=out_shape,
        compiler_params=_cparams(("parallel",)),
        name="prep_a",
    )(u2, u2, u2, u2, bd(W_A), bd(128), qg, kg)


def _stack_heads(q, g):
    return jnp.concatenate([q[:, (A_REP * g + r) * A_HD:(A_REP * g + r + 1) * A_HD] for r in range(A_REP)], axis=0)


def _rep_rows(x):
    return jnp.concatenate([x] * A_REP, axis=0)


def _masked_softmax(s, valid):
    sm = jnp.where(valid, s, NEG)
    m = jnp.max(sm, axis=-1, keepdims=True)
    e = jnp.where(valid, jnp.exp(sm - m), 0.0)
    l = jnp.sum(e, axis=-1, keepdims=True)
    return e / jnp.maximum(l, TINY)


def _select_blocks(pc_sum, covt_ref, st_ref, t_row, ns, n_top):
    ns_pad = covt_ref.shape[0]
    imp = lax.dot_general(covt_ref[...], pc_sum, NT, precision=HI, preferred_element_type=F32)
    jj = lax.broadcasted_iota(jnp.int32, (ns_pad, LANES), 0)
    tt = jnp.broadcast_to(t_row, (ns_pad, LANES))
    cur = tt // SEL_BLOCK
    forced = (jj == 0) | (jj == cur) | (jj == cur - 1)
    future = jj * SEL_BLOCK > tt
    score = jnp.where(future, NEG, jnp.where(forced, -NEG, imp))
    score = jnp.where(jj < ns, score, -jnp.inf)
    st_ref[...] = score

    def beats(k, rank):
        row = jnp.broadcast_to(st_ref[pl.ds(k, 1), :], (ns_pad, LANES))
        b = (row > score) | ((row == score) & (k < jj))
        return rank + jnp.where(b, 1.0, 0.0)

    rank = jnp.zeros((ns_pad, LANES), F32)
    if ns <= 32:
        for k in range(ns):
            rank = beats(k, rank)
    else:
        rank = lax.fori_loop(0, ns, beats, rank)
    return jnp.where(rank < n_top, 1.0, 0.0)


def _pad_rows(x, rows):
    if x.shape[0] == rows:
        return x
    return jnp.concatenate([x, jnp.zeros((rows - x.shape[0], x.shape[1]), x.dtype)], axis=0)


def _select_rows(pc_sum, cov_ref, sel_ref, t0, ns, n_top, nq):
    Q = pc_sum.shape[0]
    ns_pad = cov_ref.shape[1]
    nk = -(-ns // 8) * 8
    imp = jnp.dot(pc_sum, cov_ref[...], precision=HI, preferred_element_type=F32)
    jj = lax.broadcasted_iota(jnp.int32, (Q, ns_pad), 1)
    tt = lax.broadcasted_iota(jnp.int32, (Q, ns_pad), 0) + t0
    cur = tt // SEL_BLOCK
    forced = (jj == 0) | (jj == cur) | (jj == cur - 1)
    future = jj * SEL_BLOCK > tt
    score = jnp.where(future, NEG, jnp.where(forced, -NEG, imp))
    score = jnp.where(jj < ns, score, -jnp.inf)
    score_col = _pad_rows(score, LANES).T
    kk = lax.broadcasted_iota(jnp.int32, (nk, ns_pad), 0)
    jl = lax.broadcasted_iota(jnp.int32, (nk, ns_pad), 1)
    sel_ref[...] = jnp.zeros(sel_ref.shape, F32)
    for i in range(nq):
        col = jnp.broadcast_to(score_col[0:nk, i:i + 1], (nk, ns_pad))
        row = jnp.broadcast_to(score[i:i + 1, :], (nk, ns_pad))
        beats = (col > row) | ((col == row) & (kk < jl))
        rank = jnp.sum(jnp.where(beats, 1.0, 0.0), axis=0, keepdims=True)
        sel_ref[i:i + 1, :] = jnp.where(rank < n_top, 1.0, 0.0)
    return sel_ref[...]


def _compress(src_ref, nrow, wk_ref, wv_ref, pbuf_ref, kc_ref, vc_ref, by_phase=False):
    acck = jnp.zeros((nrow, 256), F32)
    accv = jnp.zeros((nrow, 256), F32)
    for j in range(CMP_STRIDE):
        if by_phase:
            xk = src_ref[0, j, 0:nrow, :].astype(BF16)
            xv = src_ref[1, j, 0:nrow, :].astype(BF16)
        else:
            xk = src_ref[0, pl.ds(j, nrow, stride=CMP_STRIDE), :].astype(BF16)
            xv = src_ref[1, pl.ds(j, nrow, stride=CMP_STRIDE), :].astype(BF16)
        acck = acck + jnp.dot(xk, wk_ref[j], preferred_element_type=F32)
        accv = accv + jnp.dot(xv, wv_ref[j], preferred_element_type=F32)
    nout = kc_ref.shape[0]
    for acc, dst in ((acck, kc_ref), (accv, vc_ref)):
        pbuf_ref[0:nrow, :] = acc[:, 128:256]
        dst[...] = acc[0:nout, 0:128] + pbuf_ref[pl.ds(1, nout), :]


def _gated_sum(gates, gx_ref, ocat_ref):
    gh, gl = _split(gates)
    gexp = (jnp.dot(gh, gx_ref[...], preferred_element_type=F32)
            + jnp.dot(gl, gx_ref[...], preferred_element_type=F32))
    out = gexp[:, 0:W_A] * ocat_ref[0]
    for c in range(1, N_BRANCH):
        out = out + gexp[:, c * W_A:(c + 1) * W_A] * ocat_ref[c]
    return out


def _nsa_prompt_kernel(q_ref, rows_ref, small_ref, wk_ref, wv_ref, bcmp_ref, tselt_ref, covt_ref,
                       twin_ref, cmt_ref, gx_ref,
                       o_ref, kc_ref, vc_ref, pbuf_ref, mb_ref, acc_ref, m_ref, l_ref, st_ref, ocat_ref, ocatt_ref,
                       *, ns, n_top):
    bi = pl.program_id(1)
    T = rows_ref.shape[1]
    nch = T // CMP_STRIDE
    QB = Q_BLOCK
    SEL, WIN = 1, 2

    @pl.when(bi == 0)
    def _():
        pbuf_ref[...] = jnp.zeros(pbuf_ref.shape, F32)
        _compress(rows_ref, nch, wk_ref, wv_ref, pbuf_ref, kc_ref, vc_ref)

    t0 = bi * QB
    q = q_ref[0]
    tc = lax.broadcasted_iota(jnp.int32, (QB, nch), 0) + t0
    cend = lax.broadcasted_iota(jnp.int32, (QB, nch), 1) * CMP_STRIDE + (CMP_BLOCK - 1)
    cvalid = _rep_rows(tc - cend >= 0)
    t_row = lax.broadcasted_iota(jnp.int32, (1, LANES), 1) + t0
    qg = [_stack_heads(q, g).astype(BF16) for g in range(A_KV)]
    qt = (q * LOG2E).T
    qgt = [jnp.concatenate([qt[(A_REP * g + r) * A_HD:(A_REP * g + r + 1) * A_HD] for r in range(A_REP)],
                           axis=1).astype(BF16) for g in range(A_KV)]

    for g in range(A_KV):
        kc = kc_ref[:, g * A_HD:(g + 1) * A_HD].astype(BF16)
        vc = vc_ref[:, g * A_HD:(g + 1) * A_HD].astype(BF16)
        s = lax.dot_general(qg[g], kc, NT, preferred_element_type=F32) + bcmp_ref[0, g]
        p_c = _masked_softmax(s, cvalid)
        o_c = jnp.dot(p_c.astype(BF16), vc, preferred_element_type=F32)
        for r in range(A_REP):
            h = A_REP * g + r
            ocat_ref[0, :, h * A_HD:(h + 1) * A_HD] = o_c[r * QB:(r + 1) * QB]
        pc_sum = p_c[0:QB] + p_c[QB:2 * QB] + p_c[2 * QB:3 * QB] + p_c[3 * QB:4 * QB]
        sel_t = _select_blocks(pc_sum, covt_ref, st_ref, t_row, ns, n_top)
        mb_ref[g] = (sel_t - 1.0) * (-NEG)

    m_ref[...] = jnp.full(m_ref.shape, NEG, F32)
    l_ref[...] = jnp.zeros(l_ref.shape, F32)
    acc_ref[...] = jnp.zeros(acc_ref.shape, F32)
    lanes4 = lambda x: jnp.concatenate([x] * A_REP, axis=1)
    half = SEL_BLOCK

    def tiles(specs):
        cat = lambda xs: xs[0] if len(xs) == 1 else jnp.concatenate(xs, axis=0)
        cis = [2 * (br - 1) + g for br, g, _, _ in specs]
        sms, vs = [], []
        for br, g, kb, nb in specs:
            ks, vv, bs, ms = [], [], [], []
            for j in range(nb):
                off = pl.multiple_of((kb + j) * QB, QB)
                ks.append(rows_ref[2 * br, pl.ds(off, QB), g * A_HD:(g + 1) * A_HD])
                vv.append(rows_ref[2 * br + 1, pl.ds(off, QB), g * A_HD:(g + 1) * A_HD])
                if br == SEL:
                    bs.append(tselt_ref[bi - kb - j, g])
                    blk = (QB // half) * (kb + j)
                    mt = jnp.concatenate([jnp.broadcast_to(mb_ref[g, pl.ds(blk + a, 1), :], (half, QB))
                                          for a in range(QB // half)], axis=0)
                    ms.append(mt + cmt_ref[jnp.minimum(bi - kb, 1)] if nb == 1 else mt)
                else:
                    bs.append(twin_ref[bi - kb - j, g])
            vs.append(cat(vv).astype(BF16))
            sc = jnp.dot(cat(ks).astype(BF16), qgt[g], preferred_element_type=F32) + cat(bs)
            sms.append(sc + lanes4(cat(ms)) if ms else sc)
        m_prevs = [m_ref[ci] for ci in cis]
        m_news = [jnp.maximum(mp, jnp.max(sm, axis=0, keepdims=True)) for mp, sm in zip(m_prevs, sms)]
        ps = [jnp.exp2(sm - mn) for sm, mn in zip(sms, m_news)]
        sums = [jnp.sum(p, axis=0, keepdims=True) for p in ps]
        pvs = [lax.dot_general(v, p.astype(BF16), TN, preferred_element_type=F32) for v, p in zip(vs, ps)]
        for ci, mp, mn, sm_, pv in zip(cis, m_prevs, m_news, sums, pvs):
            alpha = jnp.exp2(mp - mn)
            l_ref[ci] = alpha * l_ref[ci] + sm_
            acc_ref[ci] = alpha * acc_ref[ci] + pv
            m_ref[ci] = mn

    def body_sel(kb, carry):
        tiles([(SEL, g, kb, 1) for g in range(A_KV)])
        return carry

    def body_sel2(pair, carry):
        tiles([(SEL, g, 2 * pair, 2) for g in range(A_KV)])
        return carry

    def body_both(kb, carry):
        tiles([(br, g, kb, 1) for g in range(A_KV) for br in (SEL, WIN)])
        return carry

    lo = jnp.maximum(bi - WINDOW // QB, 0)
    lax.fori_loop(0, lo // 2, body_sel2, 0)
    lax.fori_loop(2 * (lo // 2), lo, body_sel, 0)
    lax.fori_loop(lo, bi + 1, body_both, 0)

    for br in (SEL, WIN):
        for g in range(A_KV):
            ci = 2 * (br - 1) + g
            o_t = acc_ref[ci] / jnp.maximum(l_ref[ci], TINY)
            for r in range(A_REP):
                h = A_REP * g + r
                ocatt_ref[br - 1, h * A_HD:(h + 1) * A_HD, :] = o_t[:, r * QB:(r + 1) * QB]
        ocat_ref[br] = ocatt_ref[br - 1].T

    o_ref[0] = _gated_sum(jax.nn.sigmoid(small_ref[0].astype(F32)), gx_ref, ocat_ref)


def _nsa_prompt(qn3, rows6, u3, wk, wv, tb):
    B, T, _ = qn3.shape
    QB = Q_BLOCK
    nqb = T // QB
    nch = T // CMP_STRIDE
    ns = T // SEL_BLOCK
    ns_pad = tb['covt'].shape[0]
    n_top = min(TOP_N, ns)
    nw = tb['twin'].shape[0]
    kern = functools.partial(_nsa_prompt_kernel, ns=ns, n_top=n_top)
    c2 = lambda shape: pl.BlockSpec(shape, lambda b, i: (0,) * len(shape))
    return pl.pallas_call(
        kern,
        grid=(B, nqb),
        in_specs=[pl.BlockSpec((1, QB, W_A), lambda b, i: (b, i, 0)),
                  pl.BlockSpec((6, T, 128), lambda b, i: (0, b, 0)),
                  pl.BlockSpec((1, QB, LANES), lambda b, i: (b, i, OFF['small'] // LANES)),
                  c2((CMP_STRIDE, 128, 256)), c2((CMP_STRIDE, 128, 256)),
                  pl.BlockSpec((1, A_KV, A_REP * QB, nch), lambda b, i: (i, 0, 0, 0)),
                  c2((nqb, A_KV, QB, A_REP * QB)),
                  c2((ns_pad, nch)), c2((nw, A_KV, QB, A_REP * QB)), c2((2, QB, QB)),
                  c2((LANES, N_BRANCH * W_A))],
        out_specs=pl.BlockSpec((1, QB, W_A), lambda b, i: (b, i, 0)),
        out_shape=jax.ShapeDtypeStruct((B, T, W_A), F32),
        scratch_shapes=[pltpu.VMEM((nch, 128), F32), pltpu.VMEM((nch, 128), F32),
                        pltpu.VMEM((nch + 8, 128), F32),
                        pltpu.VMEM((A_KV, ns_pad, QB), F32),
                        pltpu.VMEM((2 * A_KV, A_HD, A_REP * QB), F32),
                        pltpu.VMEM((2 * A_KV, 1, A_REP * QB), F32),
                        pltpu.VMEM((2 * A_KV, 1, A_REP * QB), F32),
                        pltpu.VMEM((ns_pad, LANES), F32),
                        pltpu.VMEM((N_BRANCH, QB, W_A), F32),
                        pltpu.VMEM((2, W_A, QB), F32)],
        compiler_params=_cparams(("parallel", "arbitrary")),
        name="nsa_prompt",
    )(qn3, rows6, u3, wk, wv, tb['bcmp'], tb['tselt'], tb['covt'], tb['twin'], tb['cmt'], tb['gx'])


def _nsa_sample_kernel(pt_ref, *refs, pg, past, tp, t_real, ns, n_top):
    cmp_refs = refs[0:pg]
    sel_refs = refs[pg:2 * pg]
    (q_ref, rows_ref, small_ref, win_ref, wk_ref, wv_ref, bcmp_ref, bsel_ref, bwin_ref, cov_ref, perm_ref,
     o_ref, cslab, sslab, kc_ref, vc_ref, pbuf_ref, st_ref, s_ref) = refs[2 * pg:]
    p = pl.program_id(1)
    npg = pl.num_programs(1)
    n_pages = past // PAGE_SIZE
    SL = (n_pages + 1) * PAGE_SIZE
    nblk = kc_ref.shape[0]
    WK = WINDOW + LANES
    pair_blocks = 2 * PAGE_SIZE // CMP_STRIDE

    for kk in range(pg // 2):
        page = p * pg + 2 * kk
        row0 = pl.multiple_of(page * (PAGE_SIZE // CMP_STRIDE), pair_blocks)
        pair = jnp.concatenate([cmp_refs[2 * kk][...], cmp_refs[2 * kk + 1][...]], axis=1).astype(BF16)
        ordered = lax.dot_general(perm_ref[...], pair, NT, preferred_element_type=F32)
        for kv in range(2):
            for j in range(CMP_STRIDE):
                cslab[kv, j, pl.ds(row0, pair_blocks), :] = ordered[j * pair_blocks:(j + 1) * pair_blocks,
                                                                    kv * 128:(kv + 1) * 128].astype(BF16)
            for k in (2 * kk, 2 * kk + 1):
                sslab[kv, p * pg + k] = sel_refs[k][kv * 128:(kv + 1) * 128, :].astype(BF16)

    @pl.when(p == npg - 1)
    def _():
        new_t = [_pad_rows(rows_ref[i], LANES).T for i in range(2, 6)]
        zrows = jnp.zeros((15, 128), F32)
        for kv in range(2):
            for j in range(CMP_STRIDE):
                blk = jnp.concatenate([rows_ref[kv, j:j + 1, :], zrows], axis=0) if j < tp else jnp.zeros((16, 128), F32)
                cslab[kv, j, nblk:nblk + 16, :] = blk.astype(BF16)
            sslab[kv, n_pages] = new_t[kv].astype(BF16)
        pbuf_ref[...] = jnp.zeros(pbuf_ref.shape, F32)
        _compress(cslab, nblk + 16, wk_ref, wv_ref, pbuf_ref, kc_ref, vc_ref, by_phase=True)

        q = q_ref[0]
        gates = jax.nn.sigmoid(small_ref[0].astype(F32))
        R = A_REP * tp
        zq = jnp.zeros((R, A_HD), F32)
        q2 = jnp.concatenate([jnp.concatenate([_stack_heads(q, 0), zq], axis=1),
                              jnp.concatenate([zq, _stack_heads(q, 1)], axis=1)], axis=0).astype(BF16)
        rep2 = lambda x: jnp.concatenate([x] * (A_KV * A_REP), axis=0)
        ti = lax.broadcasted_iota(jnp.int32, (tp, SL), 0) + past
        causal = ti - lax.broadcasted_iota(jnp.int32, (tp, SL), 1) >= 0
        tc = lax.broadcasted_iota(jnp.int32, (tp, nblk), 0) + past
        cend = lax.broadcasted_iota(jnp.int32, (tp, nblk), 1) * CMP_STRIDE + (CMP_BLOCK - 1)
        wd = (lax.broadcasted_iota(jnp.int32, (tp, WK), 0) + WINDOW
              - lax.broadcasted_iota(jnp.int32, (tp, WK), 1))
        npt = n_pages + 1
        nck = 5 if npt % 5 == 0 else 1
        cpt = npt // nck
        ck = cpt * LANES

        s = lax.dot_general(q2, kc_ref[...].astype(BF16), NT, preferred_element_type=F32) + bcmp_ref[...]
        p_c = _masked_softmax(s, rep2(tc - cend >= 0))
        o_c = jnp.dot(p_c.astype(BF16), vc_ref[...].astype(BF16), preferred_element_type=F32)

        kmasks = []
        for g in range(A_KV):
            pg_ = p_c[g * R:(g + 1) * R]
            pc_sum = pg_[0:tp] + pg_[tp:2 * tp] + pg_[2 * tp:3 * tp] + pg_[3 * tp:4 * tp]
            sel = _select_rows(pc_sum, cov_ref, st_ref, past, ns, n_top, t_real)
            km = jnp.concatenate([jnp.broadcast_to(sel[:, b:b + 1], (tp, SEL_BLOCK))
                                  for b in range(SL // SEL_BLOCK)], axis=1)
            kmasks.append(_rep_rows(causal & (km > 0.5)))
        svalid = jnp.concatenate(kmasks, axis=0)

        def sel_t(kv, c):
            return jnp.concatenate([sslab[kv, t] for t in range(c * cpt, (c + 1) * cpt)], axis=1)

        for c in range(nck):
            s_ref[:, c * ck:(c + 1) * ck] = jnp.dot(q2, sel_t(0, c), preferred_element_type=F32)
        s_ref[...] = _masked_softmax(s_ref[...] + bsel_ref[...], svalid)
        o_s = jnp.zeros((A_KV * R, 2 * A_HD), F32)
        for c in range(nck):
            o_s = o_s + lax.dot_general(s_ref[:, c * ck:(c + 1) * ck].astype(BF16), sel_t(1, c), NT,
                                        preferred_element_type=F32)

        kw = jnp.concatenate([win_ref[0:128, :], new_t[2]], axis=1).astype(BF16)
        vw = jnp.concatenate([win_ref[128:256, :], new_t[3]], axis=1).astype(BF16)
        sw = jnp.dot(q2, kw, preferred_element_type=F32) + bwin_ref[...]
        p_w = _masked_softmax(sw, rep2((wd >= 0) & (wd < WINDOW)))
        o_w = lax.dot_general(p_w.astype(BF16), vw, NT, preferred_element_type=F32)

        for g in range(A_KV):
            gs = slice(g * A_HD, (g + 1) * A_HD)
            for r in range(A_REP):
                h = A_REP * g + r
                rs = slice(g * R + r * tp, g * R + (r + 1) * tp)
                out = (gates[:, 3 * h:3 * h + 1] * o_c[rs, gs] + gates[:, 3 * h + 1:3 * h + 2] * o_s[rs, gs]
                       + gates[:, 3 * h + 2:3 * h + 3] * o_w[rs, gs])
                o_ref[0, :, h * A_HD:(h + 1) * A_HD] = out


def _nsa_sample(qn3, rows6, u3, pool_cmp, pool_sel, win_t, layer, page_table, wk, wv, tb, past, t_real):
    B, tp, _ = qn3.shape
    n_pages = past // PAGE_SIZE
    pg = 64 if n_pages % 64 == 0 else n_pages
    npg = n_pages // pg
    SL = past + LANES
    nblk = past // CMP_STRIDE
    ns = -(-(past + t_real) // SEL_BLOCK)
    ns_pad = tb['cov'].shape[1]
    n_top = min(TOP_N, ns)
    WK = WINDOW + LANES
    kern = functools.partial(_nsa_sample_kernel, pg=pg, past=past, tp=tp, t_real=t_real, ns=ns, n_top=n_top)

    def page_spec(k):
        return pl.BlockSpec((None, None, 256, PAGE_SIZE), lambda b, p, pt: (layer, pt[b, p * pg + k], 0, 0))

    def c_(shape):
        return pl.BlockSpec(shape, lambda b, p, pt: (0,) * len(shape))

    R = A_REP * tp
    grid_spec = pltpu.PrefetchScalarGridSpec(
        num_scalar_prefetch=1,
        grid=(B, npg),
        in_specs=([page_spec(k) for k in range(pg)] + [page_spec(k) for k in range(pg)]
                  + [pl.BlockSpec((1, tp, W_A), lambda b, p, pt: (b, 0, 0)),
                     pl.BlockSpec((6, tp, 128), lambda b, p, pt: (0, b, 0)),
                     pl.BlockSpec((1, tp, LANES), lambda b, p, pt: (b, 0, OFF['small'] // LANES)),
                     pl.BlockSpec((None, None, 256, WINDOW), lambda b, p, pt: (layer, b, 0, 0)),
                     c_((CMP_STRIDE, 128, 256)), c_((CMP_STRIDE, 128, 256)),
                     c_((A_KV * R, nblk)), c_((A_KV * R, SL)), c_((A_KV * R, WK)),
                     c_((nblk, ns_pad)), c_((2 * PAGE_SIZE, 2 * PAGE_SIZE))]),
        out_specs=pl.BlockSpec((1, tp, W_A), lambda b, p, pt: (b, 0, 0)),
        scratch_shapes=[pltpu.VMEM((2, CMP_STRIDE, nblk + 16, 128), BF16),
                        pltpu.VMEM((2, n_pages + 1, 128, PAGE_SIZE), BF16),
                        pltpu.VMEM((nblk, 128), F32), pltpu.VMEM((nblk, 128), F32),
                        pltpu.VMEM((nblk + 16, 128), F32),
                        pltpu.VMEM((tp, ns_pad), F32), pltpu.VMEM((A_KV * R, SL), F32)],
    )
    return pl.pallas_call(
        kern,
        grid_spec=grid_spec,
        out_shape=jax.ShapeDtypeStruct((B, tp, W_A), F32),
        compiler_params=_cparams(("parallel", "arbitrary")),
        name="nsa_sample",
    )(page_table, *([pool_cmp] * pg), *([pool_sel] * pg), qn3, rows6, u3, win_t, wk, wv,
      *(tb[k].reshape(A_KV * R, -1) for k in ('bcmp', 'bsel', 'bwin')), tb['cov'], tb['perm'])


def _tri(L, lower_incl):
    r = lax.broadcasted_iota(jnp.int32, (L, L), 0)
    c = lax.broadcasted_iota(jnp.int32, (L, L), 1)
    return (r >= c) if lower_incl else (r > c)


def _mlstm_kernel(x_ref, small_ref, smallt_ref, brow_ref, bcol_ref, c0_ref, n0_ref, m0_ref,
                  h_ref, co_ref, no_ref, mo_ref, c_s, m_s, *, L, CB, t_real):
    c = pl.program_id(1)
    nc = pl.num_programs(1)

    @pl.when(c == 0)
    def _():
        c_s[:, :, 0:M_HD] = c0_ref[0]
        c_s[:, :, M_HD:2 * M_HD] = n0_ref[0]
        m_s[...] = m0_ref[0]

    sm = small_ref[0].astype(F32)
    low = _tri(L, True)
    lowf = low.astype(F32)
    cbs = range(CB)
    hd = range(M_HEADS)
    ch = [(cb, h) for cb in cbs for h in hd]

    li_col, b_col, b_row, li_row = [], [], [], []
    for cb in cbs:
        rs = slice(cb * L, (cb + 1) * L)
        smt = smallt_ref[0, cb]
        lic = sm[rs, SM_MIF:SM_MIF + 4] + brow_ref[0:1, 0:4]
        lfc = _log_sigmoid(sm[rs, SM_MIF + 4:SM_MIF + 8] + brow_ref[0:1, 4:8])
        lir = smt[0:4, :] + bcol_ref[0:4, :]
        lfr = _log_sigmoid(smt[4:8, :] + bcol_ref[4:8, :])
        if t_real % L:
            t0 = (c * CB + cb) * L
            tcol = lax.broadcasted_iota(jnp.int32, (L, 4), 0) + t0
            trow = lax.broadcasted_iota(jnp.int32, (4, L), 1) + t0
            lic = jnp.where(tcol < t_real, lic, NEG)
            lfc = jnp.where(tcol < t_real, lfc, 0.0)
            lir = jnp.where(trow < t_real, lir, NEG)
            lfr = jnp.where(trow < t_real, lfr, 0.0)
        li_col.append(lic)
        li_row.append(lir)
        b_col.append(jnp.dot(lowf, lfc, precision=HI, preferred_element_type=F32))
        b_row.append(lax.dot_general(lfr, lowf, NT, precision=HI, preferred_element_type=F32))

    rows = lambda cb: slice(cb * L, (cb + 1) * L)
    q = {(cb, h): x_ref[0, rows(cb), h * M_HD:(h + 1) * M_HD].astype(F32) for cb, h in ch}
    k = {(cb, h): x_ref[0, rows(cb), W_M + h * M_HD:W_M + (h + 1) * M_HD].astype(F32) * (M_HD ** -0.5)
         for cb, h in ch}
    v = {(cb, h): x_ref[0, rows(cb), 2 * W_M + h * M_HD:2 * W_M + (h + 1) * M_HD].astype(F32) for cb, h in ch}
    bc = {(cb, h): b_col[cb][:, h:h + 1] for cb, h in ch}
    Dm = {(cb, h): jnp.where(low, bc[cb, h] - b_row[cb][h:h + 1, :] + li_row[cb][h:h + 1, :], NEG) for cb, h in ch}
    dmax = {x: jnp.max(Dm[x], axis=-1, keepdims=True) for x in ch}
    qk = {x: lax.dot_general(q[x], k[x], NT, preferred_element_type=F32) for x in ch}
    bL = {x: bc[x][L - 1:L, :] for x in ch}
    wlog = {(cb, h): bL[cb, h] - bc[cb, h] + li_col[cb][:, h:h + 1] for cb, h in ch}
    wmax = {x: jnp.max(wlog[x], axis=0, keepdims=True) for x in ch}

    m_prev, m_new = {}, {}
    for h in hd:
        m = m_s[h][:, 0:1]
        for cb in cbs:
            m_prev[cb, h] = m
            m = jnp.maximum(bL[cb, h] + m, wmax[cb, h])
            m_new[cb, h] = m
    a = {x: bc[x] + m_prev[x] for x in ch}
    mt = {x: jnp.maximum(a[x], dmax[x]) for x in ch}
    S = {x: qk[x] * jnp.exp(Dm[x] - mt[x]) for x in ch}
    inter = {x: jnp.exp(a[x] - mt[x]) for x in ch}
    v1 = {x: jnp.concatenate([v[x], jnp.ones((L, M_HD), F32)], axis=1) for x in ch}
    Sv = {x: jnp.dot(S[x], v1[x], preferred_element_type=F32) for x in ch}
    dec = {x: jnp.exp(bL[x] + m_prev[x] - m_new[x]) for x in ch}
    kw = {x: k[x] * jnp.exp(wlog[x] - m_new[x]) for x in ch}
    kv = {x: lax.dot_general(kw[x], v1[x], TN, preferred_element_type=F32) for x in ch}

    CN = {h: c_s[h] for h in hd}
    for cb in cbs:
        qC = {h: jnp.dot(q[cb, h], CN[h], preferred_element_type=F32) for h in hd}
        for h in hd:
            x = (cb, h)
            num = inter[x] * qC[h] + Sv[x]
            den = num[:, M_HD:2 * M_HD]
            h_ref[0, rows(cb), h * M_HD:(h + 1) * M_HD] = (num[:, 0:M_HD]
                                                           / jnp.maximum(jnp.abs(den), jnp.exp(-mt[x])))
            CN[h] = dec[x] * CN[h] + kv[x]
    for h in hd:
        c_s[h] = CN[h]
        m_s[h] = jnp.broadcast_to(m_new[CB - 1, h], (1, LANES))

    @pl.when(c == nc - 1)
    def _():
        co_ref[0] = c_s[:, :, 0:M_HD]
        no_ref[0] = c_s[:, :, M_HD:2 * M_HD]
        mo_ref[0] = m_s[...]


def _mlstm(u3, smallt, m_bi, m_bf, C0, n0, m0, L, t_real):
    B, T, _ = u3.shape
    nc = T // L
    brow = jnp.concatenate([m_bi, m_bf]).reshape(1, 8)
    bcol = jnp.concatenate([m_bi, m_bf]).reshape(8, 1)
    n0 = jnp.broadcast_to(n0[:, :, :, None], (B, M_HEADS, M_HD, M_HD))
    m0 = jnp.broadcast_to(m0[:, :, None, None], (B, M_HEADS, 1, LANES))
    CB = 16 if nc % 16 == 0 else 1
    kern = functools.partial(_mlstm_kernel, L=L, CB=CB, t_real=t_real)
    st = lambda shape: pl.BlockSpec(shape, lambda b, c: (b,) + (0,) * (len(shape) - 1))
    h, Cn, nn, mn = pl.pallas_call(
        kern,
        grid=(B, nc // CB),
        in_specs=[pl.BlockSpec((1, CB * L, 3 * W_M), lambda b, c: (b, c, OFF['m_qkv'] // (3 * W_M))),
                  pl.BlockSpec((1, CB * L, LANES), lambda b, c: (b, c, OFF['small'] // LANES)),
                  pl.BlockSpec((1, CB, 16, L), lambda b, c: (b, c, 0, 0)),
                  pl.BlockSpec((1, 8), lambda b, c: (0, 0)), pl.BlockSpec((8, 1), lambda b, c: (0, 0)),
                  st((1, M_HEADS, M_HD, M_HD)), st((1, M_HEADS, M_HD, M_HD)), st((1, M_HEADS, 1, LANES))],
        out_specs=[pl.BlockSpec((1, CB * L, W_M), lambda b, c: (b, c, 0)),
                   st((1, M_HEADS, M_HD, M_HD)), st((1, M_HEADS, M_HD, M_HD)), st((1, M_HEADS, 1, LANES))],
        out_shape=[jax.ShapeDtypeStruct((B, T, W_M), F32),
                   jax.ShapeDtypeStruct((B, M_HEADS, M_HD, M_HD), F32),
                   jax.ShapeDtypeStruct((B, M_HEADS, M_HD, M_HD), F32),
                   jax.ShapeDtypeStruct((B, M_HEADS, 1, LANES), F32)],
        scratch_shapes=[pltpu.VMEM((M_HEADS, M_HD, 2 * M_HD), F32), pltpu.VMEM((M_HEADS, 1, LANES), F32)],
        compiler_params=_cparams(("parallel", "arbitrary")),
        name="mlstm",
    )(u3, u3, smallt, brow, bcol, C0, n0, m0)
    return h, Cn, nn[:, :, :, 0], mn[:, :, 0, 0]


def _gdn_a_kernel(x_ref, small_ref, smallt_ref, cw_ref, prow_ref, pcol_ref, buf_ref,
                  u_ref, w_ref, qe_ref, kd_ref, at_ref, eg_ref, xbuf, *, L, CB, t_real):
    c = pl.program_id(1)
    PRE = 8
    R = CB * L

    @pl.when(c == 0)
    def _():
        xbuf[0:PRE, :] = buf_ref[0]

    xbuf[PRE:PRE + R, :] = x_ref[0].astype(F32)
    conv = jnp.zeros((R, 3 * W_G), F32)
    for j in range(CONV_W):
        conv = conv + xbuf[pl.ds(PRE - (CONV_W - 1) + j, R), :] * cw_ref[j:j + 1, :]
    xbuf[0:PRE, :] = xbuf[R:R + PRE, :]
    conv = _silu(conv)

    sm = small_ref[0].astype(F32)
    g_all = -jnp.exp(prow_ref[0:1, 0:4]) * _softplus(sm[:, SM_GAB:SM_GAB + 4] + prow_ref[1:2, 0:4])
    beta_all = jax.nn.sigmoid(sm[:, SM_GAB + 4:SM_GAB + 8])
    lowf = _tri(L, True).astype(F32)

    H = G_HEADS
    lane_h = lax.broadcasted_iota(jnp.int32, (L, H * L), 1) // L
    row_p = lax.broadcasted_iota(jnp.int32, (L, H * L), 0)
    col_p = lax.broadcasted_iota(jnp.int32, (L, H * L), 1) % L
    blockmask = (lax.broadcasted_iota(jnp.int32, (H * L, H * L), 0) // L
                 == lax.broadcasted_iota(jnp.int32, (H * L, H * L), 1) // L)

    def pack_diag(full):
        out = full[0:L]
        for h in range(1, H):
            out = jnp.where(lane_h == h, full[h * L:(h + 1) * L], out)
        return out

    def block_diag(m, parts):
        if L % 16:
            return _split(jnp.where(blockmask, jnp.concatenate([m] * H, axis=0), 0.0))
        return tuple(jnp.where(blockmask, jnp.concatenate([p] * H, axis=0), jnp.zeros((), BF16))
                     for p in parts)

    a_list, rhs_list = [], []
    for cb in range(CB):
        rs = slice(cb * L, (cb + 1) * L)
        smt = smallt_ref[0, cb]
        g_col = g_all[rs]
        beta_col = beta_all[rs]
        g_row = -jnp.exp(pcol_ref[0:4, 0:1]) * _softplus(smt[8:12, :] + pcol_ref[0:4, 1:2])
        if t_real % L:
            t0 = (c * CB + cb) * L
            tcol = lax.broadcasted_iota(jnp.int32, (L, 4), 0) + t0
            trow = lax.broadcasted_iota(jnp.int32, (4, L), 1) + t0
            g_col = jnp.where(tcol < t_real, g_col, 0.0)
            beta_col = jnp.where(tcol < t_real, beta_col, 0.0)
            g_row = jnp.where(trow < t_real, g_row, 0.0)
        G_col = jnp.dot(lowf, g_col, precision=HI, preferred_element_type=F32)
        G_row = lax.dot_general(g_row, lowf, NT, precision=HI, preferred_element_type=F32)

        qs, ks, kbs, rhss = [], [], [], []
        for h in range(G_HEADS):
            hs = slice(h * G_HD, (h + 1) * G_HD)
            cq = conv[rs, h * G_HD:(h + 1) * G_HD]
            ck = conv[rs, W_G + h * G_HD:W_G + (h + 1) * G_HD]
            v = conv[rs, 2 * W_G + h * G_HD:2 * W_G + (h + 1) * G_HD]
            q = cq * lax.rsqrt(jnp.sum(cq * cq, axis=-1, keepdims=True) + EPS) * (G_HD ** -0.5)
            k = ck * lax.rsqrt(jnp.sum(ck * ck, axis=-1, keepdims=True) + EPS)
            Gc = G_col[:, h:h + 1]
            bcol = beta_col[:, h:h + 1]
            kb = k * bcol
            eG = jnp.exp(Gc)
            GL = Gc[L - 1:L, :]
            qs.append(q)
            ks.append(k)
            kbs.append(kb)
            rhss.append(jnp.concatenate([v * bcol, kb * eG], axis=1))
            qe_ref[0, rs, hs] = q * eG
            kd_ref[0, rs, hs] = k * jnp.exp(GL - Gc)
            eg_ref[0, cb, h:h + 1, :] = jnp.broadcast_to(jnp.exp(GL), (1, LANES))
        eg_ref[0, cb, G_HEADS:8, :] = jnp.zeros((8 - G_HEADS, LANES), F32)

        Gc_p = jnp.concatenate([jnp.broadcast_to(G_col[:, h:h + 1], (L, L)) for h in range(H)], axis=1)
        Gr_p = jnp.concatenate([jnp.broadcast_to(G_row[h:h + 1, :], (L, L)) for h in range(H)], axis=1)
        low_p = row_p >= col_p
        dmask = jnp.where(low_p, jnp.exp(jnp.where(low_p, Gc_p - Gr_p, 0.0)), 0.0)
        k_st = _split(jnp.concatenate(ks, axis=0))
        A = pack_diag(_dot3(_split(jnp.concatenate(kbs, axis=0)), k_st, NT)) * jnp.where(row_p > col_p, dmask, 0.0)
        attn = pack_diag(lax.dot_general(jnp.concatenate(qs, axis=0).astype(BF16), k_st[0], NT,
                                         preferred_element_type=F32)) * dmask
        at_ref[0, rs, :] = attn
        a_list.append(A)
        rhs_list.append(_split(jnp.concatenate(rhss, axis=0)))

    chunks = range(CB)
    X = [jnp.where(row_p == col_p, 1.0, 0.0) - a_list[cb] for cb in chunks]
    As = [_split(a_list[cb]) for cb in chunks]
    Pw = [_dot3(As[cb], block_diag(a_list[cb], As[cb])) for cb in chunks]
    span = 2
    while span < L:
        Ps = [_split(Pw[cb]) for cb in chunks]
        Pbd = [block_diag(Pw[cb], Ps[cb]) for cb in chunks]
        X = [X[cb] + _dot3(_split(X[cb]), Pbd[cb]) for cb in chunks]
        span *= 2
        if span < L:
            Pw = [_dot3(Ps[cb], Pbd[cb]) for cb in chunks]
    sol = [_dot3(block_diag(X[cb], _split(X[cb])), rhs_list[cb]) for cb in chunks]
    for cb in chunks:
        for h in range(H):
            rs = slice(cb * L, (cb + 1) * L)
            hs = slice(h * G_HD, (h + 1) * G_HD)
            u_ref[0, rs, hs] = sol[cb][h * L:(h + 1) * L, 0:G_HD]
            w_ref[0, rs, hs] = sol[cb][h * L:(h + 1) * L, G_HD:2 * G_HD]


def _gdn_b_kernel(u_ref, w_ref, qe_ref, kd_ref, at_ref, eg_ref, s0_ref, o_ref, so_ref, s_s, *, L, BB):
    c = pl.program_id(1)
    nc = pl.num_programs(1)

    @pl.when(c == 0)
    def _():
        s_s[...] = s0_ref[...]

    ch = [(b, h, slice(h * G_HD, (h + 1) * G_HD)) for b in range(BB) for h in range(G_HEADS)]
    S = [s_s[b, h] for b, h, _ in ch]
    r = [jnp.dot(jnp.concatenate([w_ref[b, :, hs], qe_ref[b, :, hs]], axis=0), S[i], preferred_element_type=F32)
         for i, (b, h, hs) in enumerate(ch)]
    v_new = [u_ref[b, :, hs] - r[i][0:L] for i, (b, h, hs) in enumerate(ch)]
    av = [jnp.dot(at_ref[b, :, h * L:(h + 1) * L], v_new[i], preferred_element_type=F32)
          for i, (b, h, hs) in enumerate(ch)]
    kdv = [lax.dot_general(kd_ref[b, :, hs], v_new[i], TN, preferred_element_type=F32)
           for i, (b, h, hs) in enumerate(ch)]
    for i, (b, h, hs) in enumerate(ch):
        o_ref[b, :, hs] = r[i][L:2 * L] + av[i]
        s_s[b, h] = eg_ref[b, 0, h:h + 1, :] * S[i] + kdv[i]

    @pl.when(c == nc - 1)
    def _():
        so_ref[...] = s_s[...]


def _gdn(u3, smallt, g_conv, g_A_log, g_dt_bias, buf, S0, L, t_real):
    B, T, _ = u3.shape
    nc = T // L
    CB = 8 if nc % 8 == 0 else 1
    BB = 8 if B % 8 == 0 else B
    prow = jnp.stack([g_A_log, g_dt_bias])
    pcol = jnp.stack([g_A_log, g_dt_bias], axis=1)
    buf8 = jnp.concatenate([jnp.zeros((B, 8 - (CONV_W - 1), 3 * W_G), F32), buf], axis=1)
    st = lambda shape: pl.BlockSpec(shape, lambda b, c: (b,) + (0,) * (len(shape) - 1))
    cst = lambda shape: pl.BlockSpec(shape, lambda b, c: (0,) * len(shape))
    row = lambda nb, r, w: pl.BlockSpec((nb, r, w), lambda b, c: (b, c, 0))
    tok = lambda w: jax.ShapeDtypeStruct((B, T, w), F32)
    u, w, qe, kd, at, eg = pl.pallas_call(
        functools.partial(_gdn_a_kernel, L=L, CB=CB, t_real=t_real),
        grid=(B, nc // CB),
        in_specs=[pl.BlockSpec((1, CB * L, 3 * W_G), lambda b, c: (b, c, OFF['g_qkv'] // (3 * W_G))),
                  pl.BlockSpec((1, CB * L, LANES), lambda b, c: (b, c, OFF['small'] // LANES)),
                  pl.BlockSpec((1, CB, 16, L), lambda b, c: (b, c, 0, 0)),
                  cst((CONV_W, 3 * W_G)), cst((2, 4)), cst((4, 2)), st((1, 8, 3 * W_G))],
        out_specs=[row(1, CB * L, W_G)] * 4 + [row(1, CB * L, G_HEADS * L),
                                               pl.BlockSpec((1, CB, 8, LANES), lambda b, c: (b, c, 0, 0))],
        out_shape=[tok(W_G)] * 4 + [tok(G_HEADS * L), jax.ShapeDtypeStruct((B, nc, 8, LANES), F32)],
        scratch_shapes=[pltpu.VMEM((CB * L + 8, 3 * W_G), F32)],
        compiler_params=_cparams(("parallel", "arbitrary")),
        name="gdn_a",
    )(u3, u3, smallt, g_conv, prow, pcol, buf8)
    sblk = pl.BlockSpec((BB, G_HEADS, G_HD, G_HD), lambda b, c: (b, 0, 0, 0))
    return pl.pallas_call(
        functools.partial(_gdn_b_kernel, L=L, BB=BB),
        grid=(B // BB, nc),
        in_specs=[row(BB, L, W_G)] * 4 + [row(BB, L, G_HEADS * L),
                                          pl.BlockSpec((BB, 1, 8, LANES), lambda b, c: (b, c, 0, 0)), sblk],
        out_specs=[row(BB, L, W_G), sblk],
        out_shape=[tok(W_G), jax.ShapeDtypeStruct((B, G_HEADS, G_HD, G_HD), F32)],
        scratch_shapes=[pltpu.VMEM((BB, G_HEADS, G_HD, G_HD), F32)],
        compiler_params=_cparams(("parallel", "arbitrary")),
        name="gdn_b",
    )(u, w, qe, kd, at, eg, S0)


def _head_rmsnorm(x, gain_ref, nheads, hd):
    outs = []
    for h in range(nheads):
        xh = x[:, h * hd:(h + 1) * hd]
        ms = jnp.mean(xh * xh, axis=-1, keepdims=True)
        outs.append(xh * lax.rsqrt(ms + EPS) * gain_ref[...])
    return jnp.concatenate(outs, axis=1)


def _merge_kernel(x_ref, oa_ref, hm_ref, og_ref, az_ref, mo_ref, mz_ref, gz_ref, mg_ref,
                  wb_ref, wo_ref, mhn_ref, ghn_ref, y_ref):
    oa = oa_ref[...] * _silu(az_ref[...].astype(F32))
    om = (_head_rmsnorm(hm_ref[...], mhn_ref, M_HEADS, M_HD) * jax.nn.sigmoid(mo_ref[...].astype(F32))
          * _silu(mz_ref[...].astype(F32)))
    og = _head_rmsnorm(og_ref[...], ghn_ref, G_HEADS, G_HD) * _silu(gz_ref[...].astype(F32))
    y = jnp.zeros(y_ref.shape, F32)
    for i, br in enumerate((oa, om, og)):
        proj = jnp.dot(br.astype(BF16), wb_ref[i], preferred_element_type=F32)
        y = y + jax.nn.sigmoid(mg_ref[:, i * D_MODEL:(i + 1) * D_MODEL].astype(F32)) * proj
    y_ref[...] = x_ref[...] + jnp.dot(y.astype(BF16), wo_ref[...], preferred_element_type=F32)


def _merge_out(x2, o_a, h_m, o_g, u2, w_branch, w_out, m_hn, g_hn):
    n = x2.shape[0]
    tm = min(512, n)
    row = lambda w, off: pl.BlockSpec((tm, w), lambda i: (i, off // w))
    cst = lambda shape: pl.BlockSpec(shape, lambda i: (0,) * len(shape))
    return pl.pallas_call(
        _merge_kernel,
        grid=(n // tm,),
        in_specs=[row(D_MODEL, 0), row(W_A, 0), row(W_M, 0), row(W_G, 0),
                  row(W_A, OFF['a_z']), row(W_M, OFF['m_o']), row(W_M, OFF['m_z']), row(W_G, OFF['g_z']),
                  row(N_BRANCH * D_MODEL, OFF['merge']),
                  cst((N_BRANCH, W_A, D_MODEL)), cst((D_MODEL, D_MODEL)), cst((1, M_HD)), cst((1, G_HD))],
        out_specs=row(D_MODEL, 0),
        out_shape=jax.ShapeDtypeStruct((n, D_MODEL), F32),
        compiler_params=_cparams(("parallel",)),
        name="merge_out",
    )(x2, o_a, h_m, o_g, u2, u2, u2, u2, u2, w_branch.astype(BF16), w_out.astype(BF16),
      m_hn.reshape(1, M_HD), g_hn.reshape(1, G_HD))


def _small_t(u3, L):
    B, T, _ = u3.shape
    s = u3[:, :, OFF['small'] + SM_MIF:OFF['small'] + SM_MIF + 16].astype(F32)
    return s.reshape(B, T // L, L, 16).transpose(0, 1, 3, 2)


def _layer(x, lw, tb, past, q_off):
    B, T, _ = x.shape
    if past is None:
        tp, L = T, math.gcd(T, M_CHUNK)
    else:
        tp = -(-T // 8) * 8
        L = tp
        x = jnp.pad(x, ((0, 0), (0, tp - T), (0, 0)))
    x2 = x.reshape(B * tp, D_MODEL)
    u2 = _proj_in(x2, lw['norm_g'], lw['w_perm'], BF16 if past is None else F32)
    u3 = u2.reshape(B, tp, N_PERM)
    if past is None:
        C0 = jnp.zeros((B, M_HEADS, M_HD, M_HD), F32)
        n0 = jnp.zeros((B, M_HEADS, M_HD), F32)
        m0 = jnp.zeros((B, M_HEADS), F32)
        S0 = jnp.zeros((B, G_HEADS, G_HD, G_HD), F32)
        buf = jnp.zeros((B, CONV_W - 1, 3 * W_G), F32)
    else:
        C0, n0, m0, S0, buf = past['mC'], past['mn'], past['mm'], past['gS'], past['gconv']
    qn, rows6, *rows_t = _prep_a(u2, lw['a_qn'], lw['a_kn'], T if past is None and T % 512 == 0 else None)
    qn3 = qn.reshape(B, tp, W_A)
    if past is None:
        o_a = _nsa_prompt(qn3, rows6, u3, lw['wk'], lw['wv'], tb)
    else:
        o_a = _nsa_sample(qn3, rows6, u3, past['cmp'], past['sel'], past['win_t'], past['layer'], past['page_table'],
                          lw['wk'], lw['wv'], tb, q_off, T)
    smallt = _small_t(u3, L)
    h_m, Cn, nn, mn = _mlstm(u3, smallt, lw['m_bi'], lw['m_bf'], C0, n0, m0, L, T)
    o_g, Sn = _gdn(u3, smallt, lw['g_conv'], lw['g_A_log'], lw['g_dt_bias'], buf, S0, L, T)
    y2 = _merge_out(x2, o_a.reshape(B * tp, W_A), h_m.reshape(B * tp, W_M), o_g.reshape(B * tp, W_G), u2,
                    lw['w_branch'], lw['w_out'], lw['m_hn'], lw['g_hn'])
    if rows_t:
        new_cmp, new_sel, new_win = (rows_t[b].reshape(B, 2, A_KV, A_HD, T).transpose(0, 4, 1, 2, 3)
                                     for b in range(3))
    else:
        rows = rows6.reshape(3, 2, B, tp, A_KV, A_HD)[:, :, :, :T]
        new_cmp, new_sel, new_win = (jnp.moveaxis(rows[b], 0, 2) for b in range(3))
    g_qkv = u3[:, max(T - (CONV_W - 1), 0):T, OFF['g_qkv']:OFF['g_qkv'] + 3 * W_G].astype(F32)
    if past is None:
        win = new_win
    else:
        win = jnp.concatenate([past['win'], new_win], axis=1)
    full = jnp.concatenate([buf, g_qkv], axis=1)
    state = dict(cmp=new_cmp, sel=new_sel, win=win[:, -min(WINDOW, win.shape[1]):],
                 mC=Cn, mn=nn, mm=mn, gS=Sn, gconv=full[:, -(CONV_W - 1):])
    return y2.reshape(B, tp, D_MODEL)[:, :T], state


def kernel(x_prompt, x_sample, cache_cmp_kv, cache_sel_kv, cache_win_kv, state_mlstm_C, state_mlstm_n,
           state_mlstm_m, state_gdn_S, state_gdn_conv, page_table, norm_g, w_in, a_qn, a_kn, a_cmp_wk,
           a_cmp_wv, rel_bias, m_bi, m_bf, m_hn, g_conv, g_A_log, g_dt_bias, g_hn, w_branch, w_out):
    names = ('cmp', 'sel', 'win', 'mC', 'mn', 'mm', 'gS', 'gconv')
    st_p = {k: [] for k in names}
    st_s = {k: [] for k in names}
    past_len = page_table.shape[1] * PAGE_SIZE
    n_pool = cache_cmp_kv.shape[1]
    pos_minor = lambda c: jnp.swapaxes(c.reshape(c.shape[0], c.shape[1], c.shape[2], 2 * A_KV * A_HD), 2, 3)
    pool_cmp, pool_sel, win_t = pos_minor(cache_cmp_kv), pos_minor(cache_sel_kv), pos_minor(cache_win_kv)
    w_t = jnp.swapaxes(w_in, 1, 2)
    db, dt = x_sample.shape[0], x_sample.shape[1]
    tb_p = _prompt_tables(rel_bias, x_prompt.shape[1])
    tb_s = _sample_tables(rel_bias, past_len, -(-dt // 8) * 8, dt)
    y_p, y_s = x_prompt, x_sample
    for l in range(DEPTH):
        lw = dict(norm_g=norm_g[l], w_perm=_permute_w_in(w_t, l), a_qn=a_qn[l], a_kn=a_kn[l],
                  wk=_cmp_weights(a_cmp_wk[l]), wv=_cmp_weights(a_cmp_wv[l]),
                  m_bi=m_bi[l], m_bf=m_bf[l], m_hn=m_hn[l], g_conv=g_conv[l], g_A_log=g_A_log[l],
                  g_dt_bias=g_dt_bias[l], g_hn=g_hn[l], w_branch=w_branch[l], w_out=w_out[l])
        y_p, new_p = _layer(y_p, lw, tb_p, None, 0)
        past = dict(cmp=pool_cmp, sel=pool_sel, win_t=win_t, layer=l, win=cache_win_kv[l], page_table=page_table,
                    mC=state_mlstm_C[l], mn=state_mlstm_n[l], mm=state_mlstm_m[l],
                    gS=state_gdn_S[l], gconv=state_gdn_conv[l])
        y_s, new_s = _layer(y_s, lw, tb_s, past, past_len)
        for k in names:
            st_p[k].append(new_p[k])
            st_s[k].append(new_s[k])
    P = {k: jnp.stack(v) for k, v in st_p.items()}
    S = {k: jnp.stack(v) for k, v in st_s.items()}
    return (y_p, y_s, P['cmp'], S['cmp'], P['sel'], S['sel'], P['win'], S['win'],
            P['mC'], S['mC'], P['mn'], S['mn'], P['mm'], S['mm'], P['gS'], S['gS'], P['gconv'], S['gconv'])
```
